```python
import math
import jax
import jax.numpy as jnp
from jax import lax
import numpy as np

D_MODEL = 2048
BATCH = 8
SEQ = 2048
DEPTH = 2

MEM_LEN = 256
EPS = 1e-6
ROPE_THETA = 10000.0
Q_BLOCK = 128
NEG_INF = -1e30
HEAD_DIM = 64

SWA_HEADS = 16
SWA_KV_HEADS = 4
SWA_WINDOW = 128
SWA_Q_W = SWA_HEADS * HEAD_DIM
SWA_KV_W = SWA_KV_HEADS * HEAD_DIM

RWKV_HEADS = 16
RWKV_HEAD = 64
RWKV_DIM = RWKV_HEADS * RWKV_HEAD
DECAY_LORA = 64
AAA_LORA = 64
GATE_LORA = 160
RWKV_GN_EPS = 64e-5
RWKV_W = 3 * RWKV_DIM + DECAY_LORA + AAA_LORA + GATE_LORA

MLA_HEADS = 16
MLA_Q_RANK = 512
MLA_KV_RANK = 256
MLA_NOPE = 64
MLA_ROPE = 32
MLA_V = 64

DIFF_HEADS = 8
DIFF_QK = 64
DIFF_V = 2 * DIFF_QK

MEM_HEADS = 4
MEM_HEAD_DIM = 128
MEM_W = MEM_HEADS * MEM_HEAD_DIM

D_FF = 5632

AB_IN = SWA_Q_W + 2 * SWA_KV_W + RWKV_W
AB_OUT = SWA_Q_W + RWKV_DIM
CD_IN = MLA_Q_RANK + MLA_KV_RANK + MLA_ROPE + 2 * (DIFF_HEADS * 2 * DIFF_QK) + DIFF_HEADS * DIFF_V
CD_OUT = MLA_HEADS * MLA_V + DIFF_HEADS * DIFF_V
N_EVEN = (DEPTH + 1) // 2
N_ODD = DEPTH // 2

kernel_name = 'hybrid_swa_rwkv7_mla_diff_macaron'


def rms_norm(x, g):
    xf = x.astype(jnp.float32)
    y = xf * lax.rsqrt(jnp.mean(xf * xf, axis=-1, keepdims=True) + EPS)
    return (y * g.astype(jnp.float32)).astype(x.dtype)


def rope_tables(positions, dim):
    inv = 1.0 / (ROPE_THETA ** (jnp.arange(0, dim, 2, dtype=jnp.float32) / dim))
    ang = positions.astype(jnp.float32)[..., None] * inv
    return jnp.cos(ang), jnp.sin(ang)


def apply_rope(x, cos, sin):
    x1, x2 = jnp.split(x.astype(jnp.float32), 2, axis=-1)
    c = cos[:, :, None, :]
    s = sin[:, :, None, :]
    return jnp.concatenate([x1 * c - x2 * s, x1 * s + x2 * c], axis=-1).astype(x.dtype)


def swiglu(h, w_gate, w_up, w_down):
    return (jax.nn.silu(h @ w_gate) * (h @ w_up)) @ w_down


def sliding_window_sink_attention(q, k, v, sinks):
    B, S, H, d = q.shape
    KV = k.shape[2]
    G = H // KV
    nb = S // Q_BLOCK
    qb = q.reshape(B, nb, Q_BLOCK, KV, G, d)

    def with_prev(t):
        t = t.reshape(B, nb, Q_BLOCK, KV, d)
        prev = jnp.concatenate([jnp.zeros_like(t[:, :1]), t[:, :-1]], axis=1)
        return jnp.concatenate([prev, t], axis=2)

    kk, vv = with_prev(k), with_prev(v)
    s = jnp.einsum('bnqkgd,bnskd->bnkgqs', qb, kk).astype(jnp.float32) * (d ** -0.5)
    k_idx = jnp.arange(2 * Q_BLOCK)[None, :]
    rel = (jnp.arange(Q_BLOCK)[:, None] + Q_BLOCK) - k_idx
    band = (rel >= 0) & (rel < SWA_WINDOW)
    blk_ok = (jnp.arange(nb)[:, None, None] > 0) | (k_idx[None] >= Q_BLOCK)
    valid = (band[None] & blk_ok)[None, :, None, None]
    s = jnp.where(valid, s, NEG_INF)
    sink = sinks.astype(jnp.float32).reshape(1, 1, KV, G, 1, 1)
    m = jnp.maximum(jnp.max(s, axis=-1, keepdims=True), sink)
    p = jnp.exp(s - m)
    p = p / (jnp.sum(p, axis=-1, keepdims=True) + jnp.exp(sink - m))
    o = jnp.einsum('bnkgqs,bnskd->bnqkgd', p.astype(v.dtype), vv)
    return o.reshape(B, S, H, d)


def rwkv7_time_mix(proj, mu, w0, w2, a0, a2, g2, k_k, k_a, r_k, gn_g, gn_b):
    B, S, _ = proj.shape
    f32 = jnp.float32
    p = proj.astype(f32)
    prev = jnp.concatenate([jnp.zeros_like(p[:, :1]), p[:, :-1]], axis=1)
    xs = p + (prev - p) * mu.astype(f32)
    c3 = 3 * RWKV_DIM
    r, k, v, w_lo, a_lo, g_lo = jnp.split(
        xs, [RWKV_DIM, 2 * RWKV_DIM, c3, c3 + DECAY_LORA, c3 + DECAY_LORA + AAA_LORA], axis=-1)
    w = -jax.nn.softplus(-(w0.astype(f32) + jnp.tanh(w_lo) @ w2.astype(f32))) - 0.5
    a = jax.nn.sigmoid(a0.astype(f32) + a_lo @ a2.astype(f32))
    g = jax.nn.sigmoid(g_lo) @ g2.astype(f32)
    hs = (B, S, RWKV_HEADS, RWKV_HEAD)
    r, k, v, w, a = r.reshape(hs), k.reshape(hs), v.reshape(hs), w.reshape(hs), a.reshape(hs)
    kk = k * k_k.astype(f32).reshape(RWKV_HEADS, RWKV_HEAD)
    kk = kk / jnp.maximum(jnp.sqrt(jnp.sum(kk * kk, axis=-1, keepdims=True)), 1e-12)
    k = k * (1.0 + (a - 1.0) * k_a.astype(f32).reshape(RWKV_HEADS, RWKV_HEAD))
    decay = jnp.exp(-jnp.exp(w))

    def step(state, inp):
        r_t, k_t, v_t, d_t, kk_t, a_t = inp
        sa = jnp.einsum('bhij,bhj->bhi', state, -kk_t)
        state = (state * d_t[:, :, None, :] + sa[..., None] * (kk_t * a_t)[:, :, None, :]
                 + v_t[..., None] * k_t[:, :, None, :])
        return state, jnp.einsum('bhij,bhj->bhi', state, r_t)

    sf = lambda t: jnp.swapaxes(t, 0, 1)
    init = jnp.zeros((B, RWKV_HEADS, RWKV_HEAD, RWKV_HEAD), f32)
    _, y = lax.scan(step, init, (sf(r), sf(k), sf(v), sf(decay), sf(kk), sf(a)))
    y = sf(y)
    mean = jnp.mean(y, axis=-1, keepdims=True)
    var = jnp.mean(jnp.square(y - mean), axis=-1, keepdims=True)
    y = ((y - mean) * lax.rsqrt(var + RWKV_GN_EPS)).reshape(B, S, RWKV_DIM) * gn_g.astype(f32) + gn_b.astype(f32)
    bonus = jnp.sum(r * k * r_k.astype(f32).reshape(RWKV_HEADS, RWKV_HEAD), axis=-1, keepdims=True) * v
    out = (y + bonus.reshape(B, S, RWKV_DIM)) * g
    return out.astype(proj.dtype)


def causal_block_probs(q_blk, k, q_start):
    s = jnp.einsum('bqhd,bshd->bhqs', q_blk, k).astype(jnp.float32)
    q_pos = q_start + jnp.arange(Q_BLOCK)
    mask = jnp.arange(k.shape[1])[None, :] <= q_pos[:, None]
    return jax.nn.softmax(jnp.where(mask, s, NEG_INF), axis=-1)


def causal_block_attention(q, k, v, scale):
    B, S, H, d = q.shape
    nb = S // Q_BLOCK
    qb = jnp.swapaxes((q * scale).reshape(B, nb, Q_BLOCK, H, d), 0, 1)

    def one(args):
        i, q_blk = args
        p = causal_block_probs(q_blk, k, i * Q_BLOCK)
        return jnp.einsum('bhqs,bshd->bqhd', p.astype(v.dtype), v)

    o = lax.map(one, (jnp.arange(nb), qb))
    return jnp.swapaxes(o, 0, 1).reshape(B, S, H, v.shape[-1])


def mla_attention(c_q, c_kv, k_pe, cq_norm, ckv_norm, w_uq, w_ukv,
                  q_nope_norm, k_nope_norm, q_rope_norm, k_rope_norm, cos, sin):
    B, S, _ = c_q.shape
    q = (rms_norm(c_q, cq_norm) @ w_uq).reshape(B, S, MLA_HEADS, MLA_NOPE + MLA_ROPE)
    kv = (rms_norm(c_kv, ckv_norm) @ w_ukv).reshape(B, S, MLA_HEADS, MLA_NOPE + MLA_V)
    q_nope, q_pe = jnp.split(q, [MLA_NOPE], axis=-1)
    k_nope, v = jnp.split(kv, [MLA_NOPE], axis=-1)
    q_nope = rms_norm(q_nope, q_nope_norm)
    k_nope = rms_norm(k_nope, k_nope_norm)
    q_pe = apply_rope(rms_norm(q_pe, q_rope_norm), cos, sin)
    k_pe = apply_rope(rms_norm(k_pe.reshape(B, S, 1, MLA_ROPE), k_rope_norm), cos, sin)
    q = jnp.concatenate([q_nope, q_pe], axis=-1)
    k = jnp.concatenate([k_nope, jnp.broadcast_to(k_pe, (B, S, MLA_HEADS, MLA_ROPE))], axis=-1)
    o = causal_block_attention(q, k, v, (MLA_NOPE + MLA_ROPE) ** -0.5)
    return o.reshape(B, S, MLA_HEADS * MLA_V)


def differential_attention(dq, dk, dv, q_norm, k_norm, lq1, lk1, lq2, lk2, subln, lambda_init, cos, sin):
    B, S, _ = dq.shape
    f32 = jnp.float32
    q = apply_rope(rms_norm(dq.reshape(B, S, 2 * DIFF_HEADS, DIFF_QK), q_norm), cos, sin) * (DIFF_QK ** -0.5)
    k = apply_rope(rms_norm(dk.reshape(B, S, 2 * DIFF_HEADS, DIFF_QK), k_norm), cos, sin)
    q = q.reshape(B, S, DIFF_HEADS, 2, DIFF_QK)
    k = k.reshape(B, S, DIFF_HEADS, 2, DIFF_QK)
    k1, k2 = k[:, :, :, 0], k[:, :, :, 1]
    v = dv.reshape(B, S, DIFF_HEADS, DIFF_V)
    lam = (jnp.exp(jnp.sum(lq1.astype(f32) * lk1.astype(f32)))
           - jnp.exp(jnp.sum(lq2.astype(f32) * lk2.astype(f32))) + lambda_init)
    nb = S // Q_BLOCK
    qb = jnp.swapaxes(q.reshape(B, nb, Q_BLOCK, DIFF_HEADS, 2, DIFF_QK), 0, 1)

    def one(args):
        i, q_blk = args
        p1 = causal_block_probs(q_blk[:, :, :, 0], k1, i * Q_BLOCK)
        p2 = causal_block_probs(q_blk[:, :, :, 1], k2, i * Q_BLOCK)
        return jnp.einsum('bhqs,bshd->bqhd', (p1 - lam * p2).astype(v.dtype), v)

    o = jnp.swapaxes(lax.map(one, (jnp.arange(nb), qb)), 0, 1).reshape(B, S, DIFF_HEADS, DIFF_V)
    o = rms_norm(o, subln) * (1.0 - lambda_init)
    return o.reshape(B, S, DIFF_HEADS * DIFF_V)


def memory_cross_attention(h, mem_k, mem_v, w_q, q_norm, w_o):
    B, S, _ = h.shape
    q = rms_norm((h @ w_q).reshape(B, S, MEM_HEADS, MEM_HEAD_DIM), q_norm)
    s = jnp.einsum('bshd,bmhd->bhsm', q, mem_k).astype(jnp.float32) * (MEM_HEAD_DIM ** -0.5)
    p = jax.nn.softmax(s, axis=-1)
    o = jnp.einsum('bhsm,bmhd->bshd', p.astype(mem_v.dtype), mem_v).reshape(B, S, MEM_W)
    return o @ w_o


def setup_inputs(seed: int = 0) -> dict:
    key = jax.random.key(seed)
    ks = iter(jax.random.split(key, 64))
    f32 = jnp.float32

    def normal(shape, scale):
        return jax.random.normal(next(ks), shape, f32) * scale

    def gain(shape):
        return 1.0 + 0.02 * jax.random.normal(next(ks), shape, f32)

    D, F, E, O = D_MODEL, D_FF, N_EVEN, N_ODD
    x = normal((BATCH, SEQ, D), 1.0)
    mem = normal((BATCH, MEM_LEN, D), 1.0)
    positions = (jax.random.randint(next(ks), (BATCH, 1), 0, 4096) + jnp.arange(SEQ)[None, :]).astype(jnp.int32)
    return {
        'x': x, 'mem': mem, 'positions': positions,
        'ffn1_norm': gain((DEPTH, D)),
        'ffn1_w_gate': normal((DEPTH, D, F), D ** -0.5),
        'ffn1_w_up': normal((DEPTH, D, F), D ** -0.5),
        'ffn1_w_down': normal((DEPTH, F, D), F ** -0.5),
        'mix_norm': gain((DEPTH, D)),
        'ab_w_in': normal((E, D, AB_IN), D ** -0.5),
        'ab_w_out': normal((E, AB_OUT, D), AB_OUT ** -0.5),
        'swa_q_norm': gain((E, HEAD_DIM)),
        'swa_k_norm': gain((E, HEAD_DIM)),
        'swa_sinks': normal((E, SWA_HEADS), 1.0),
        'rwkv_mu': jax.random.uniform(next(ks), (E, RWKV_W), f32),
        'rwkv_w0': jax.random.uniform(next(ks), (E, RWKV_DIM), f32, -6.0, -1.0),
        'rwkv_w2': normal((E, DECAY_LORA, RWKV_DIM), 0.5 * DECAY_LORA ** -0.5),
        'rwkv_a0': normal((E, RWKV_DIM), 0.1),
        'rwkv_a2': normal((E, AAA_LORA, RWKV_DIM), AAA_LORA ** -0.5),
        'rwkv_g2': normal((E, GATE_LORA, RWKV_DIM), GATE_LORA ** -0.5),
        'rwkv_k_k': 0.85 + normal((E, RWKV_DIM), 0.02),
        'rwkv_k_a': gain((E, RWKV_DIM)),
        'rwkv_r_k': normal((E, RWKV_DIM), 0.1),
        'rwkv_gn_g': gain((E, RWKV_DIM)),
        'rwkv_gn_b': normal((E, RWKV_DIM), 0.01),
        'cd_w_in': normal((O, D, CD_IN), D ** -0.5),
        'cd_w_out': normal((O, CD_OUT, D), CD_OUT ** -0.5),
        'mla_cq_norm': gain((O, MLA_Q_RANK)),
        'mla_ckv_norm': gain((O, MLA_KV_RANK)),
        'mla_w_uq': normal((O, MLA_Q_RANK, MLA_HEADS * (MLA_NOPE + MLA_ROPE)), MLA_Q_RANK ** -0.5),
        'mla_w_ukv': normal((O, MLA_KV_RANK, MLA_HEADS * (MLA_NOPE + MLA_V)), MLA_KV_RANK ** -0.5),
        'mla_q_nope_norm': gain((O, MLA_NOPE)),
        'mla_k_nope_norm': gain((O, MLA_NOPE)),
        'mla_q_rope_norm': gain((O, MLA_ROPE)),
        'mla_k_rope_norm': gain((O, MLA_ROPE)),
        'diff_q_norm': gain((O, DIFF_QK)),
        'diff_k_norm': gain((O, DIFF_QK)),
        'diff_lq1': normal((O, DIFF_QK), 0.1),
        'diff_lk1': normal((O, DIFF_QK), 0.1),
        'diff_lq2': normal((O, DIFF_QK), 0.1),
        'diff_lk2': normal((O, DIFF_QK), 0.1),
        'diff_subln': gain((O, DIFF_V)),
        'memx_norm': gain((DEPTH, D)),
        'memx_w_q': normal((DEPTH, D, MEM_W), D ** -0.5),
        'memx_q_norm': gain((DEPTH, MEM_HEAD_DIM)),
        'memx_w_o': normal((DEPTH, MEM_W, D), MEM_W ** -0.5),
        'mem_norm': gain((D,)),
        'mem_w_kv': normal((D, 2 * MEM_W), D ** -0.5),
        'mem_k_norm': gain((MEM_HEAD_DIM,)),
        'ffn2_norm': gain((DEPTH, D)),
        'ffn2_w_gate': normal((DEPTH, D, F), D ** -0.5),
        'ffn2_w_up': normal((DEPTH, D, F), D ** -0.5),
        'ffn2_w_down': normal((DEPTH, F, D), F ** -0.5),
    }


def reference(x, mem, positions, ffn1_norm, ffn1_w_gate, ffn1_w_up, ffn1_w_down, mix_norm,
              ab_w_in, ab_w_out, swa_q_norm, swa_k_norm, swa_sinks,
              rwkv_mu, rwkv_w0, rwkv_w2, rwkv_a0, rwkv_a2, rwkv_g2, rwkv_k_k, rwkv_k_a, rwkv_r_k,
              rwkv_gn_g, rwkv_gn_b,
              cd_w_in, cd_w_out, mla_cq_norm, mla_ckv_norm, mla_w_uq, mla_w_ukv,
              mla_q_nope_norm, mla_k_nope_norm, mla_q_rope_norm, mla_k_rope_norm,
              diff_q_norm, diff_k_norm, diff_lq1, diff_lk1, diff_lq2, diff_lk2, diff_subln,
              memx_norm, memx_w_q, memx_q_norm, memx_w_o, mem_norm, mem_w_kv, mem_k_norm,
              ffn2_norm, ffn2_w_gate, ffn2_w_up, ffn2_w_down):
    B, S, _ = x.shape
    M = mem.shape[1]
    cos64, sin64 = rope_tables(positions, HEAD_DIM)
    cos32, sin32 = rope_tables(positions, MLA_ROPE)

    mem_k, mem_v = jnp.split(rms_norm(mem, mem_norm) @ mem_w_kv, 2, axis=-1)
    mem_k = rms_norm(mem_k.reshape(B, M, MEM_HEADS, MEM_HEAD_DIM), mem_k_norm)
    mem_v = mem_v.reshape(B, M, MEM_HEADS, MEM_HEAD_DIM)

    ab_split = [SWA_Q_W, SWA_Q_W + SWA_KV_W, SWA_Q_W + 2 * SWA_KV_W]
    c1 = MLA_Q_RANK
    c2 = c1 + MLA_KV_RANK
    c3 = c2 + MLA_ROPE
    c4 = c3 + DIFF_HEADS * 2 * DIFF_QK
    c5 = c4 + DIFF_HEADS * 2 * DIFF_QK
    cd_split = [c1, c2, c3, c4, c5]

    for layer in range(DEPTH):
        x = x + 0.5 * swiglu(rms_norm(x, ffn1_norm[layer]), ffn1_w_gate[layer], ffn1_w_up[layer], ffn1_w_down[layer])
        h = rms_norm(x, mix_norm[layer])
        j = layer // 2
        if layer % 2 == 0:
            u = h @ ab_w_in[j]
            qa, ka, va, rw = jnp.split(u, ab_split, axis=-1)
            q = apply_rope(rms_norm(qa.reshape(B, S, SWA_HEADS, HEAD_DIM), swa_q_norm[j]), cos64, sin64)
            k = apply_rope(rms_norm(ka.reshape(B, S, SWA_KV_HEADS, HEAD_DIM), swa_k_norm[j]), cos64, sin64)
            v = va.reshape(B, S, SWA_KV_HEADS, HEAD_DIM)
            y_a = sliding_window_sink_attention(q, k, v, swa_sinks[j]).reshape(B, S, SWA_Q_W)
            y_b = rwkv7_time_mix(rw, rwkv_mu[j], rwkv_w0[j], rwkv_w2[j], rwkv_a0[j], rwkv_a2[j], rwkv_g2[j],
                                 rwkv_k_k[j], rwkv_k_a[j], rwkv_r_k[j], rwkv_gn_g[j], rwkv_gn_b[j])
            mixed = jnp.concatenate([y_a, y_b], axis=-1) @ ab_w_out[j]
        else:
            u = h @ cd_w_in[j]
            c_q, c_kv, k_pe, dq, dk, dv = jnp.split(u, cd_split, axis=-1)
            y_c = mla_attention(c_q, c_kv, k_pe, mla_cq_norm[j], mla_ckv_norm[j], mla_w_uq[j], mla_w_ukv[j],
                                mla_q_nope_norm[j], mla_k_nope_norm[j], mla_q_rope_norm[j], mla_k_rope_norm[j],
                                cos32, sin32)
            lambda_init = 0.8 - 0.6 * math.exp(-0.3 * layer)
            y_d = differential_attention(dq, dk, dv, diff_q_norm[j], diff_k_norm[j], diff_lq1[j], diff_lk1[j],
                                         diff_lq2[j], diff_lk2[j], diff_subln[j], lambda_init, cos64, sin64)
            mixed = jnp.concatenate([y_c, y_d], axis=-1) @ cd_w_out[j]
        x = x + mixed
        x = x + memory_cross_attention(rms_norm(x, memx_norm[layer]), mem_k, mem_v,
                                       memx_w_q[layer], memx_q_norm[layer], memx_w_o[layer])
        x = x + 0.5 * swiglu(rms_norm(x, ffn2_norm[layer]), ffn2_w_gate[layer], ffn2_w_up[layer], ffn2_w_down[layer])
    return x
```

```python
import functools
import math

import jax
import jax.numpy as jnp
from jax import lax
from jax.experimental import pallas as pl
from jax.experimental.pallas import tpu as pltpu

F32 = jnp.float32
BF16 = jnp.bfloat16
HIGHEST = lax.Precision.HIGHEST

EPS = 1e-6
ROPE_THETA = 10000.0
NEG_INF = -1e30
LANES = 128

D_MODEL = 2048
D_FF = 5632
HEAD_DIM = 64
SWA_HEADS = 16
SWA_KV_HEADS = 4
SWA_BLOCK = 128
RWKV_DIM = 1024
RWKV_CHUNK = 64
RWKV_GN_EPS = 64e-5
DECAY_LORA, AAA_LORA, GATE_LORA = 64, 64, 160
MLA_HEADS, MLA_Q_RANK, MLA_KV_RANK, MLA_NOPE, MLA_ROPE, MLA_V = 16, 512, 256, 64, 32, 64
DIFF_HEADS, DIFF_QK, DIFF_V = 8, 64, 128
MEM_HEADS, MEM_HEAD_DIM = 4, 128
MEM_W = MEM_HEADS * MEM_HEAD_DIM
AB_IN_PAD = 5120
CD_IN_PAD = 4096

VMEM_LIMIT = 48 * 1024 * 1024


def _params(*sem):
    return pltpu.CompilerParams(dimension_semantics=sem, vmem_limit_bytes=VMEM_LIMIT)


def _dot(a, b):
    return jnp.dot(a, b, preferred_element_type=F32)


def _dot_t(a, b):
    return lax.dot_general(a, b, (((1,), (1,)), ((), ())), preferred_element_type=F32)


def _dot_0(a, b):
    return lax.dot_general(a, b, (((0,), (0,)), ((), ())), preferred_element_type=F32)


def _rms(x, g):
    return x * lax.rsqrt(jnp.mean(x * x, axis=-1, keepdims=True) + EPS) * g


def _seg_sum(x, e):
    hi = x.astype(BF16)
    lo = (x - hi.astype(F32)).astype(BF16)
    return _dot(hi, e) + _dot(lo, e)


def _tile_lanes(v, width):
    return v if v.shape[-1] == width else jnp.tile(v, (1, width // v.shape[-1]))


def _swap_halves(x, half):
    w = x.shape[-1]
    lane = lax.broadcasted_iota(jnp.int32, x.shape, x.ndim - 1)
    low = (lane & (2 * half - 1)) < half
    return jnp.where(low, pltpu.roll(x, w - half, x.ndim - 1), pltpu.roll(x, half, x.ndim - 1))


def _norm_matmul_kernel(x_ref, g_ref, w_ref, o_ref, xn_ref):
    @pl.when(pl.program_id(1) == 0)
    def _():
        xn_ref[...] = _rms(x_ref[...].astype(F32), g_ref[...]).astype(BF16)

    o_ref[...] = _dot(xn_ref[...], w_ref[...]).astype(o_ref.dtype)


def norm_matmul(x, g, w, *, k_blk=0, out_dtype=F32, tm=512, tn=512, name="norm_matmul"):
    t = x.shape[0]
    k, n = w.shape
    tm, tn = min(tm, t), min(tn, n)
    assert t % tm == 0 and n % tn == 0
    return pl.pallas_call(
        _norm_matmul_kernel,
        grid=(t // tm, n // tn),
        in_specs=[pl.BlockSpec((tm, k), lambda i, j: (i, k_blk)),
                  pl.BlockSpec((1, k), lambda i, j: (0, 0)),
                  pl.BlockSpec((k, tn), lambda i, j: (0, j))],
        out_specs=pl.BlockSpec((tm, tn), lambda i, j: (i, j)),
        out_shape=jax.ShapeDtypeStruct((t, n), out_dtype),
        scratch_shapes=[pltpu.VMEM((tm, k), BF16)],
        compiler_params=_params("parallel", "arbitrary"),
        name=name,
    )(x, g.reshape(1, k).astype(F32), w)


def _matmul_res_kernel(*refs, n_in):
    x_ref, o_ref = refs[2 * n_in], refs[2 * n_in + 1]
    acc = x_ref[...]
    for i in range(n_in):
        acc = acc + _dot(refs[i][...], refs[n_in + i][...])
    o_ref[...] = acc


def matmul_residual(x, a_list, w_list, *, tm=512, tn=512, name="matmul_residual"):
    t, n = x.shape
    tm, tn = min(tm, t), min(tn, n)
    assert t % tm == 0 and n % tn == 0
    n_in = len(a_list)
    in_specs = ([pl.BlockSpec((tm, a.shape[1]), lambda i, j: (i, 0)) for a in a_list]
                + [pl.BlockSpec((w.shape[0], tn), lambda i, j: (0, j)) for w in w_list]
                + [pl.BlockSpec((tm, tn), lambda i, j: (i, j))])
    return pl.pallas_call(
        functools.partial(_matmul_res_kernel, n_in=n_in),
        grid=(t // tm, n // tn),
        in_specs=in_specs,
        out_specs=pl.BlockSpec((tm, tn), lambda i, j: (i, j)),
        out_shape=jax.ShapeDtypeStruct((t, n), F32),
        compiler_params=_params("parallel", "arbitrary"),
        name=name,
    )(*a_list, *w_list, x)


def _ffn_kernel(x_ref, g_ref, wg_ref, wu_ref, wd_ref, o_ref, xn_ref, *, nf):
    f = pl.program_id(1)

    @pl.when(f == 0)
    def _():
        xn_ref[...] = _rms(x_ref[...], g_ref[...]).astype(BF16)
        o_ref[...] = jnp.zeros_like(o_ref)

    xn = xn_ref[...]
    a = _dot(xn, wg_ref[...])
    b = _dot(xn, wu_ref[...])
    h = (a * (1.0 / (1.0 + jnp.exp(-a))) * b).astype(BF16)
    o_ref[...] += _dot(h, wd_ref[...])

    @pl.when(f == nf - 1)
    def _():
        o_ref[...] = x_ref[...] + 0.5 * o_ref[...]


def ffn(x, g, wg, wu, wd, *, tm=512, tf=512, name="ffn"):
    t, d = x.shape
    ff = wg.shape[1]
    tm = min(tm, t)
    assert t % tm == 0 and ff % tf == 0
    nf = ff // tf
    return pl.pallas_call(
        functools.partial(_ffn_kernel, nf=nf),
        grid=(t // tm, nf),
        in_specs=[pl.BlockSpec((tm, d), lambda i, f: (i, 0)),
                  pl.BlockSpec((1, d), lambda i, f: (0, 0)),
                  pl.BlockSpec((d, tf), lambda i, f: (0, f)),
                  pl.BlockSpec((d, tf), lambda i, f: (0, f)),
                  pl.BlockSpec((tf, d), lambda i, f: (f, 0))],
        out_specs=pl.BlockSpec((tm, d), lambda i, f: (i, 0)),
        out_shape=jax.ShapeDtypeStruct((t, d), F32),
        scratch_shapes=[pltpu.VMEM((tm, d), BF16)],
        compiler_params=_params("parallel", "arbitrary"),
        name=name,
    )(x, g.reshape(1, d).astype(F32), wg, wu, wd)


def _swa_kernel(sink_ref, q_ref, kc_ref, kp_ref, vc_ref, vp_ref, cc_ref, sc_ref, cp_ref, sp_ref,
                gq_ref, gk_ref, o_ref):
    n = pl.program_id(1)
    blk = SWA_BLOCK
    group = SWA_HEADS // SWA_KV_HEADS
    q = q_ref[0]
    k = jnp.concatenate([kp_ref[0], kc_ref[0]], axis=0)
    v = jnp.concatenate([vp_ref[0], vc_ref[0]], axis=0)
    cos_q, sin_q = cc_ref[0], sc_ref[0]
    cos_k = jnp.concatenate([cp_ref[0], cos_q], axis=0)
    sin_k = jnp.concatenate([sp_ref[0], sin_q], axis=0)

    def rope(x, g, cos, sin):
        w = x.shape[1]
        xg = x * _tile_lanes(g, w)
        return xg * _tile_lanes(cos, w) + _swap_halves(xg, HEAD_DIM // 2) * _tile_lanes(sin, w)

    qr = rope(q, gq_ref[...], cos_q, sin_q)
    kr = rope(k, gk_ref[...], cos_k, sin_k)

    def inv_rms(x, h):
        xh = x[:, h * HEAD_DIM:(h + 1) * HEAD_DIM]
        return lax.rsqrt(jnp.mean(xh * xh, axis=-1, keepdims=True) + EPS)

    row = lax.broadcasted_iota(jnp.int32, (group * blk, 2 * blk), 0)
    col = lax.broadcasted_iota(jnp.int32, (group * blk, 2 * blk), 1)
    rel = (row & (blk - 1)) + blk - col
    valid = (rel >= 0) & (rel < blk) & ((n > 0) | (col >= blk))
    row1 = lax.broadcasted_iota(jnp.int32, (group * blk, 1), 0)
    scale = HEAD_DIM ** -0.5
    outs = []
    for g in range(SWA_KV_HEADS):
        sl = slice(g * HEAD_DIM, (g + 1) * HEAD_DIM)
        kg = (kr[:, sl] * inv_rms(k, g)).astype(BF16)
        vg = v[:, sl].astype(BF16)
        heads = range(g * group, (g + 1) * group)
        qs = jnp.concatenate(
            [qr[:, h * HEAD_DIM:(h + 1) * HEAD_DIM] * (inv_rms(q, h) * scale) for h in heads], axis=0).astype(BF16)
        s = jnp.where(valid, _dot_t(qs, kg), NEG_INF)
        sink = jnp.zeros((group * blk, 1), F32)
        for i, h in enumerate(heads):
            sink = jnp.where((row1 >= i * blk) & (row1 < (i + 1) * blk), sink_ref[h], sink)
        m = jnp.maximum(jnp.max(s, axis=-1, keepdims=True), sink)
        p = jnp.exp(s - m)
        den = jnp.sum(p, axis=-1, keepdims=True) + jnp.exp(sink - m)
        o = _dot(p.astype(BF16), vg) / den
        outs.extend(o[i * blk:(i + 1) * blk] for i in range(group))
    o_ref[0] = jnp.concatenate(outs, axis=-1).astype(o_ref.dtype)


def swa_attention(u, cos, sin, gq, gk, sinks):
    b, s, _ = u.shape
    nb = s // SWA_BLOCK
    qw, kw = SWA_HEADS * HEAD_DIM, SWA_KV_HEADS * HEAD_DIM
    cur = lambda c: (lambda i, n: (i, n, c))
    prev = lambda c: (lambda i, n: (i, jnp.maximum(n - 1, 0), c))
    tab = pl.BlockSpec((1, SWA_BLOCK, LANES), cur(0))
    tab_prev = pl.BlockSpec((1, SWA_BLOCK, LANES), prev(0))
    gain = pl.BlockSpec((1, LANES), lambda i, n: (0, 0))
    return pl.pallas_call(
        _swa_kernel,
        grid=(b, nb),
        in_specs=[pl.BlockSpec(memory_space=pltpu.SMEM),
                  pl.BlockSpec((1, SWA_BLOCK, qw), cur(0)),
                  pl.BlockSpec((1, SWA_BLOCK, kw), cur(qw // kw)),
                  pl.BlockSpec((1, SWA_BLOCK, kw), prev(qw // kw)),
                  pl.BlockSpec((1, SWA_BLOCK, kw), cur(qw // kw + 1)),
                  pl.BlockSpec((1, SWA_BLOCK, kw), prev(qw // kw + 1)),
                  tab, tab, tab_prev, tab_prev, gain, gain],
        out_specs=pl.BlockSpec((1, SWA_BLOCK, qw), cur(0)),
        out_shape=jax.ShapeDtypeStruct((b, s, qw), BF16),
        compiler_params=_params("parallel", "arbitrary"),
        name="swa_attention",
    )(sinks.astype(F32), u, u, u, u, u, cos, sin, cos, sin, gq, gk)


def _rwkv_prep_kernel(r_ref, k_ref, v_ref, l_ref, rp_ref, kp_ref, vp_ref, lp_ref,
                      mur_ref, muk_ref, muv_ref, mul_ref, w0_ref, w2_ref, a0_ref, a2_ref, g2_ref,
                      kk_ref, ka_ref, e_ref,
                      ro_ref, ko_ref, vo_ref, kko_ref, bo_ref, lwo_ref, go_ref):
    t = pl.program_id(1)

    def shifted(cur_ref, prev_ref, mu_ref):
        x = cur_ref[0]
        last = prev_ref[0][7:8, :] * jnp.where(t > 0, 1.0, 0.0)
        row = lax.broadcasted_iota(jnp.int32, x.shape, 0)
        prev = jnp.where(row == 0, last, pltpu.roll(x, 1, 0))
        return x + (prev - x) * mu_ref[...]

    r = shifted(r_ref, rp_ref, mur_ref)
    k = shifted(k_ref, kp_ref, muk_ref)
    v = shifted(v_ref, vp_ref, muv_ref)
    lo = shifted(l_ref, lp_ref, mul_ref)
    w_lo, a_lo, g_lo = lo[:, 0:LANES], lo[:, LANES:2 * LANES], lo[:, 2 * LANES:4 * LANES]
    hp = functools.partial(jnp.dot, precision=HIGHEST, preferred_element_type=F32)
    z = -(w0_ref[...] + hp(jnp.tanh(w_lo), w2_ref[...]))
    softplus = jnp.maximum(z, 0.0) + jnp.log(1.0 + jnp.exp(-jnp.abs(z)))
    w = -softplus - 0.5
    a = 1.0 / (1.0 + jnp.exp(-(a0_ref[...] + hp(a_lo, a2_ref[...]))))
    g = hp(1.0 / (1.0 + jnp.exp(-g_lo)), g2_ref[...])
    kk = k * kk_ref[...]
    kk = kk / jnp.maximum(jnp.sqrt(_seg_sum(kk * kk, e_ref[...])), 1e-12)
    ro_ref[0] = r
    ko_ref[0] = k * (1.0 + (a - 1.0) * ka_ref[...])
    vo_ref[0] = v
    kko_ref[0] = kk
    bo_ref[0] = kk * a
    lwo_ref[0] = -jnp.exp(w)
    go_ref[0] = g


def rwkv_prep(u, mu, w0, w2, a0, a2, g2, k_k, k_a, *, ts=512, tc=512):
    b, s, _ = u.shape
    ts = min(ts, s)
    nct = RWKV_DIM // tc
    base = (SWA_HEADS + 2 * SWA_KV_HEADS) * HEAD_DIM // tc
    lora_blk = base + 3 * nct

    def cur(off):
        return pl.BlockSpec((1, ts, tc), lambda i, t, c: (i, t, off + c))

    def prev(off):
        return pl.BlockSpec((1, 8, tc), lambda i, t, c: (i, jnp.maximum(t * (ts // 8) - 1, 0), off + c))

    lora_cur = pl.BlockSpec((1, ts, tc), lambda i, t, c: (i, t, lora_blk))
    lora_prev = pl.BlockSpec((1, 8, tc), lambda i, t, c: (i, jnp.maximum(t * (ts // 8) - 1, 0), lora_blk))
    vec = pl.BlockSpec((1, tc), lambda i, t, c: (0, c))
    vec0 = pl.BlockSpec((1, tc), lambda i, t, c: (0, 0))
    lora_w = lambda rows: pl.BlockSpec((rows, tc), lambda i, t, c: (0, c))
    pad_rows = lambda m, rows: jnp.pad(m, ((0, rows - m.shape[0]), (0, 0))).astype(F32)
    pad_cols = lambda vv, cols: jnp.pad(vv, (0, cols - vv.shape[0]))
    c3 = 3 * RWKV_DIM
    mu_l = jnp.concatenate([pad_cols(mu[c3:c3 + DECAY_LORA], LANES),
                            pad_cols(mu[c3 + DECAY_LORA:c3 + DECAY_LORA + AAA_LORA], LANES),
                            pad_cols(mu[c3 + DECAY_LORA + AAA_LORA:], 2 * LANES)])
    row = lambda vv: vv.reshape(1, -1).astype(F32)
    seg = (jnp.arange(tc)[:, None] // HEAD_DIM == jnp.arange(tc)[None, :] // HEAD_DIM).astype(BF16)
    out = jax.ShapeDtypeStruct((b, s, RWKV_DIM), F32)
    out_spec = pl.BlockSpec((1, ts, tc), lambda i, t, c: (i, t, c))
    return pl.pallas_call(
        _rwkv_prep_kernel,
        grid=(b, s // ts, nct),
        in_specs=[cur(base), cur(base + nct), cur(base + 2 * nct), lora_cur,
                  prev(base), prev(base + nct), prev(base + 2 * nct), lora_prev,
                  vec, vec, vec, vec0, vec, lora_w(LANES), vec, lora_w(LANES), lora_w(2 * LANES),
                  vec, vec, pl.BlockSpec((tc, tc), lambda i, t, c: (0, 0))],
        out_specs=[out_spec] * 7,
        out_shape=[out] * 7,
        compiler_params=_params("parallel", "parallel", "arbitrary"),
        name="rwkv_prep",
    )(u, u, u, u, u, u, u, u,
      row(mu[:RWKV_DIM]), row(mu[RWKV_DIM:2 * RWKV_DIM]), row(mu[2 * RWKV_DIM:c3]), row(mu_l),
      row(w0), pad_rows(w2, LANES), row(a0), pad_rows(a2, LANES), pad_rows(g2, 2 * LANES),
      row(k_k), row(k_a), seg)


def _rwkv_scan_kernel(r_ref, k_ref, v_ref, kk_ref, b_ref, lw_ref, g_ref, gng_ref, gnb_ref, rk_ref,
                      o_ref, st_ref, *, nchunk):
    c = RWKV_CHUNK
    hp = functools.partial(lax.dot_general, precision=HIGHEST, preferred_element_type=F32)
    mm = lambda x, y: hp(x, y, (((1,), (0,)), ((), ())))
    mm_t = lambda x, y: hp(x, y, (((1,), (1,)), ((), ())))
    mm_0 = lambda x, y: hp(x, y, (((0,), (0,)), ((), ())))
    st_ref[...] = jnp.zeros_like(st_ref)
    lane_c = lax.broadcasted_iota(jnp.int32, (c, LANES), 1)
    head0 = lane_c < HEAD_DIM
    ri = lax.broadcasted_iota(jnp.int32, (2 * c, 2 * c), 0)
    ci = lax.broadcasted_iota(jnp.int32, (2 * c, 2 * c), 1)
    eye = jnp.where(ri == ci, 1.0, 0.0)
    tril_c = jnp.where(lax.broadcasted_iota(jnp.int32, (c, c), 0) >= lax.broadcasted_iota(jnp.int32, (c, c), 1),
                       1.0, 0.0)
    stack = lambda x: jnp.concatenate([jnp.where(head0, x, 0.0), jnp.where(head0, 0.0, x)], axis=0)

    def seg_mean(x):
        m0 = jnp.sum(jnp.where(head0, x, 0.0), axis=-1, keepdims=True)
        m1 = jnp.sum(jnp.where(head0, 0.0, x), axis=-1, keepdims=True)
        return jnp.where(head0, m0, m1) * (1.0 / HEAD_DIM)

    @pl.loop(0, nchunk)
    def _(ic):
        sl = pl.ds(pl.multiple_of(ic * c, c), c)
        r, k, v = r_ref[0, sl, :], k_ref[0, sl, :], v_ref[0, sl, :]
        kk, b, lw, g = kk_ref[0, sl, :], b_ref[0, sl, :], lw_ref[0, sl, :], g_ref[0, sl, :]
        cum = mm(tril_c, lw)
        cum_end = cum[c - 1:c, :]
        e_neg = jnp.exp(-cum)
        e_end = jnp.exp(cum_end - cum)
        a_s = stack(-kk * jnp.exp(cum - lw))
        r_s = stack(r * jnp.exp(cum))
        b_s = stack(b * e_neg)
        k_s = stack(k * e_neg)
        bh_s = stack(b * e_end)
        kh_s = stack(k * e_end)
        v_s = stack(v)
        low = jnp.where(ri > ci, mm_t(a_s, b_s), 0.0)
        a_ak = jnp.where(ri > ci, mm_t(a_s, k_s), 0.0)
        a_rb = jnp.where(ri >= ci, mm_t(r_s, b_s), 0.0)
        a_rk = jnp.where(ri >= ci, mm_t(r_s, k_s), 0.0)
        inv = eye + low
        pw = low
        for _ in range(5):
            pw = mm(pw, pw)
            inv = inv + mm(inv, pw)
        p = mm(inv, a_s)
        qv = mm(inv, mm(a_ak, v_s))
        y1 = r_s + mm(a_rb, p)
        y0 = mm(a_rb, qv) + mm(a_rk, v_s)
        st = st_ref[...]
        y_st = mm_t(y1, st) + y0
        st_ref[...] = st * jnp.exp(cum_end) + mm(st, mm_0(p, bh_s)) + mm_0(qv, bh_s) + mm_0(v_s, kh_s)
        y = y_st[0:c] + y_st[c:2 * c]
        mean = seg_mean(y)
        var = seg_mean((y - mean) * (y - mean))
        yn = (y - mean) * lax.rsqrt(var + RWKV_GN_EPS) * gng_ref[...] + gnb_ref[...]
        bonus = seg_mean(r * k * rk_ref[...]) * float(HEAD_DIM) * v
        o_ref[0, sl, :] = ((yn + bonus) * g).astype(o_ref.dtype)


def rwkv_scan(r, k, v, kk, b, lw, g, gn_g, gn_b, r_k):
    bsz, s, _ = r.shape
    npair = RWKV_DIM // LANES
    seq = pl.BlockSpec((1, s, LANES), lambda i, p: (i, 0, p))
    vec = pl.BlockSpec((1, LANES), lambda i, p: (0, p))
    row = lambda vv: vv.reshape(1, -1).astype(F32)
    return pl.pallas_call(
        functools.partial(_rwkv_scan_kernel, nchunk=s // RWKV_CHUNK),
        grid=(bsz, npair),
        in_specs=[seq] * 7 + [vec] * 3,
        out_specs=seq,
        out_shape=jax.ShapeDtypeStruct((bsz, s, RWKV_DIM), BF16),
        scratch_shapes=[pltpu.VMEM((LANES, LANES), F32)],
        compiler_params=_params("parallel", "parallel"),
        name="rwkv_scan",
    )(r, k, v, kk, b, lw, g, row(gn_g), row(gn_b), row(r_k))


def _mla_prep_kernel(q_ref, kv_ref, pe_ref, cos_ref, sin_ref, e_ref, gq_ref, gkn_ref, gkp_ref, invn_ref,
                     qo_ref, ko_ref, vo_ref):
    cos, sin = cos_ref[0], sin_ref[0]
    half = MLA_ROPE // 2

    def rope(x, g):
        w = x.shape[1]
        xg = x * _tile_lanes(g, w)
        return xg * _tile_lanes(cos, w) + _swap_halves(xg, half) * _tile_lanes(sin, w)

    x = q_ref[0]
    w = x.shape[1]
    inv_n = _tile_lanes(invn_ref[...], w)
    inv = lax.rsqrt(_seg_sum(x * x, e_ref[...]) * inv_n + EPS)
    qo_ref[0] = (rope(x, gq_ref[...]) * inv * (MLA_NOPE + MLA_ROPE) ** -0.5).astype(qo_ref.dtype)
    kv = kv_ref[0]
    inv_k = lax.rsqrt(_seg_sum(kv * kv, e_ref[...]) * inv_n + EPS)
    k_nope = kv * inv_k * _tile_lanes(gkn_ref[...], w)
    pe = pe_ref[0]
    inv_pe = lax.rsqrt(jnp.sum(pe * pe, axis=-1, keepdims=True) * (1.0 / MLA_ROPE) + EPS)
    k_pe = rope(pe, gkp_ref[...]) * inv_pe
    ko_ref[0] = (k_nope + _tile_lanes(k_pe, w)).astype(ko_ref.dtype)
    vo_ref[0] = kv.astype(vo_ref.dtype)


def mla_prep(q_lat, kv, u, cos, sin, gq, gkn, gkp, *, ts=512, tc=512):
    b, s, wtot = q_lat.shape
    ts = min(ts, s)
    lane = jnp.arange(tc)
    same = (lane[:, None] // LANES == lane[None, :] // LANES)
    pos = lane % LANES
    nope = pos < MLA_NOPE
    pe = (pos >= MLA_NOPE) & (pos < MLA_NOPE + MLA_ROPE)
    seg = (same & ((nope[:, None] & nope[None, :]) | (pe[:, None] & pe[None, :]))).astype(BF16)
    p1 = jnp.arange(LANES)
    inv_n = jnp.where(p1 < MLA_NOPE, 1.0 / MLA_NOPE, jnp.where(p1 < MLA_NOPE + MLA_ROPE, 1.0 / MLA_ROPE, 0.0))
    blk = pl.BlockSpec((1, ts, tc), lambda i, t, c: (i, t, c))
    tab = pl.BlockSpec((1, ts, LANES), lambda i, t, c: (i, t, 0))
    vec = pl.BlockSpec((1, LANES), lambda i, t, c: (0, 0))
    pe_blk = (MLA_Q_RANK + MLA_KV_RANK) // LANES
    out = jax.ShapeDtypeStruct((b, s, wtot), BF16)
    return pl.pallas_call(
        _mla_prep_kernel,
        grid=(b, s // ts, wtot // tc),
        in_specs=[blk, blk, pl.BlockSpec((1, ts, LANES), lambda i, t, c: (i, t, pe_blk)), tab, tab,
                  pl.BlockSpec((tc, tc), lambda i, t, c: (0, 0)), vec, vec, vec, vec],
        out_specs=[blk] * 3,
        out_shape=[out] * 3,
        compiler_params=_params("parallel", "parallel", "arbitrary"),
        name="mla_prep",
    )(q_lat, kv, u, cos, sin, seg, gq, gkn, gkp, inv_n.reshape(1, LANES).astype(F32))


def _diff_prep_kernel(q_ref, k_ref, cos_ref, sin_ref, e_ref, gq_ref, gk_ref, qo_ref, ko_ref):
    cos, sin = cos_ref[0], sin_ref[0]

    def prep(x, g, scale):
        w = x.shape[1]
        xg = x * _tile_lanes(g, w)
        xr = xg * _tile_lanes(cos, w) + _swap_halves(xg, DIFF_QK // 2) * _tile_lanes(sin, w)
        inv = lax.rsqrt(_seg_sum(x * x, e_ref[...]) * (1.0 / DIFF_QK) + EPS)
        return xr * inv * scale

    qo_ref[0] = prep(q_ref[0], gq_ref[...], DIFF_QK ** -0.5).astype(qo_ref.dtype)
    ko_ref[0] = prep(k_ref[0], gk_ref[...], 1.0).astype(ko_ref.dtype)


def diff_prep(u, cos, sin, gq, gk, *, ts=512, tc=512):
    b, s, _ = u.shape
    ts = min(ts, s)
    wtot = 2 * DIFF_HEADS * DIFF_QK
    q_base = (CD_IN_PAD - 3 * wtot) // tc
    seg = (jnp.arange(tc)[:, None] // DIFF_QK == jnp.arange(tc)[None, :] // DIFF_QK).astype(BF16)
    blk = lambda off: pl.BlockSpec((1, ts, tc), lambda i, t, c: (i, t, off + c))
    tab = pl.BlockSpec((1, ts, LANES), lambda i, t, c: (i, t, 0))
    vec = pl.BlockSpec((1, LANES), lambda i, t, c: (0, 0))
    out = jax.ShapeDtypeStruct((b, s, wtot), BF16)
    return pl.pallas_call(
        _diff_prep_kernel,
        grid=(b, s // ts, wtot // tc),
        in_specs=[blk(q_base), blk(q_base + wtot // tc), tab, tab,
                  pl.BlockSpec((tc, tc), lambda i, t, c: (0, 0)), vec, vec],
        out_specs=[blk(0)] * 2,
        out_shape=[out] * 2,
        compiler_params=_params("parallel", "parallel", "arbitrary"),
        name="diff_prep",
    )(u, u, cos, sin, seg, gq, gk)


def _causal_attn_kernel(lam_ref, q_ref, k_ref, v_ref, g_ref, o_ref, *, n_sm, tq, out_scale):
    qi = pl.program_id(2)
    q = q_ref[0]
    if n_sm == 2:
        lane = lax.broadcasted_iota(jnp.int32, q.shape, 1)
        zero = jnp.zeros_like(q)
        qs = [jnp.where(lane < DIFF_QK, q, zero), jnp.where(lane < DIFF_QK, zero, q)]
    else:
        qs = [q]
    row = lax.broadcasted_iota(jnp.int32, (tq, tq), 0)
    col = lax.broadcasted_iota(jnp.int32, (tq, tq), 1)

    def step(j, carry, diagonal):
        kj = k_ref[0, pl.ds(pl.multiple_of(j * tq, tq), tq), :]
        vj = v_ref[0, pl.ds(pl.multiple_of(j * tq, tq), tq), :]
        new = []
        for i in range(n_sm):
            m, l, acc = carry[3 * i:3 * i + 3]
            s = _dot_t(qs[i], kj)
            if diagonal:
                s = jnp.where(col <= row, s, NEG_INF)
            m_new = jnp.maximum(m, jnp.max(s, axis=-1, keepdims=True))
            alpha = jnp.exp(m - m_new)
            p = jnp.exp(s - m_new)
            new += [m_new, alpha * l + jnp.sum(p, axis=-1, keepdims=True),
                    alpha * acc + _dot(p.astype(BF16), vj)]
        return tuple(new)

    init = (jnp.full((tq, 1), NEG_INF, F32), jnp.zeros((tq, 1), F32), jnp.zeros((tq, LANES), F32)) * n_sm
    carry = lax.fori_loop(0, qi, lambda j, cr: step(j, cr, False), init)
    carry = step(qi, carry, True)
    o = carry[2] / carry[1]
    if n_sm == 2:
        o = o - lam_ref[0] * (carry[5] / carry[4])
        o = _rms(o, g_ref[...]) * out_scale
    o_ref[0] = o.astype(o_ref.dtype)


def causal_attention(q, k, v, *, n_sm, lam=None, g=None, out_scale=1.0, tq=256, name="causal_attention"):
    b, s, wtot = q.shape
    tq = min(tq, s)
    heads = wtot // LANES
    lam = jnp.zeros((1,), F32) if lam is None else lam.reshape(1).astype(F32)
    g = jnp.ones((1, LANES), F32) if g is None else g.reshape(1, LANES).astype(F32)
    seq = pl.BlockSpec((1, s, LANES), lambda i, h, t: (i, 0, h))
    tile = pl.BlockSpec((1, tq, LANES), lambda i, h, t: (i, t, h))
    return pl.pallas_call(
        functools.partial(_causal_attn_kernel, n_sm=n_sm, tq=tq, out_scale=out_scale),
        grid=(b, heads, s // tq),
        in_specs=[pl.BlockSpec(memory_space=pltpu.SMEM), tile, seq, seq,
                  pl.BlockSpec((1, LANES), lambda i, h, t: (0, 0))],
        out_specs=tile,
        out_shape=jax.ShapeDtypeStruct((b, s, wtot), BF16),
        compiler_params=_params("parallel", "parallel", "arbitrary"),
        name=name,
    )(lam, q, k, v, g)


def _mem_attn_kernel(q_ref, kv_ref, gq_ref, gk_ref, o_ref):
    q = q_ref[0]
    kv = kv_ref[0]
    outs = []
    for h in range(MEM_HEADS):
        sl = slice(h * MEM_HEAD_DIM, (h + 1) * MEM_HEAD_DIM)
        qh = (_rms(q[:, sl], gq_ref[...]) * MEM_HEAD_DIM ** -0.5).astype(BF16)
        kh = _rms(kv[:, sl], gk_ref[...]).astype(BF16)
        vh = kv[:, MEM_W + h * MEM_HEAD_DIM:MEM_W + (h + 1) * MEM_HEAD_DIM].astype(BF16)
        s = _dot_t(qh, kh)
        p = jnp.exp(s - jnp.max(s, axis=-1, keepdims=True))
        outs.append(_dot(p.astype(BF16), vh) / jnp.sum(p, axis=-1, keepdims=True))
    o_ref[0] = jnp.concatenate(outs, axis=-1).astype(o_ref.dtype)


def mem_attention(q, mem_kv, gq, gk, *, tq=512):
    b, s, _ = q.shape
    m = mem_kv.shape[1]
    tq = min(tq, s)
    vec = pl.BlockSpec((1, MEM_HEAD_DIM), lambda i, t: (0, 0))
    return pl.pallas_call(
        _mem_attn_kernel,
        grid=(b, s // tq),
        in_specs=[pl.BlockSpec((1, tq, MEM_W), lambda i, t: (i, t, 0)),
                  pl.BlockSpec((1, m, 2 * MEM_W), lambda i, t: (i, 0, 0)), vec, vec],
        out_specs=pl.BlockSpec((1, tq, MEM_W), lambda i, t: (i, t, 0)),
        out_shape=jax.ShapeDtypeStruct((b, s, MEM_W), BF16),
        compiler_params=_params("parallel", "arbitrary"),
        name="mem_attention",
    )(q, mem_kv, gq.reshape(1, -1).astype(F32), gk.reshape(1, -1).astype(F32))


def _rope_tables(positions, dim, lead_ones, tail):
    inv = 1.0 / (ROPE_THETA ** (jnp.arange(0, dim, 2, dtype=F32) / dim))
    ang = positions.astype(F32)[..., None] * inv
    c, s = jnp.cos(ang), jnp.sin(ang)
    shape = positions.shape
    cos = jnp.concatenate([jnp.ones(shape + (lead_ones,), F32), c, c, jnp.ones(shape + (tail,), F32)], axis=-1)
    sin = jnp.concatenate([jnp.zeros(shape + (lead_ones,), F32), -s, s, jnp.zeros(shape + (tail,), F32)], axis=-1)
    reps = LANES // cos.shape[-1]
    return jnp.tile(cos, (1, 1, reps)), jnp.tile(sin, (1, 1, reps))


def _pad_cols(w, cols):
    return jnp.pad(w, ((0, 0), (0, cols - w.shape[1])))


def _ab_in_layout(w):
    c = (SWA_HEADS + 2 * SWA_KV_HEADS) * HEAD_DIM + 3 * RWKV_DIM
    return jnp.concatenate([w[:, :c], _pad_cols(w[:, c:c + DECAY_LORA], LANES),
                            _pad_cols(w[:, c + DECAY_LORA:c + DECAY_LORA + AAA_LORA], LANES),
                            _pad_cols(w[:, c + DECAY_LORA + AAA_LORA:], 2 * LANES)], axis=1)


def _cd_in_layout(w):
    c1 = MLA_Q_RANK + MLA_KV_RANK
    z = lambda n: jnp.zeros((w.shape[0], n), w.dtype)
    return jnp.concatenate([w[:, :c1], z(MLA_NOPE), w[:, c1:c1 + MLA_ROPE], z(LANES - MLA_NOPE - MLA_ROPE),
                            z(LANES), w[:, c1 + MLA_ROPE:]], axis=1)


def _head_slabs(w, per_head):
    k = w.shape[0]
    return jnp.pad(w.reshape(k, -1, per_head), ((0, 0), (0, 0), (0, LANES - per_head))).reshape(k, -1)


def _slab_vec(*parts):
    v = jnp.concatenate([p.astype(F32) for p in parts])
    return jnp.pad(v, (0, LANES - v.shape[0])).reshape(1, LANES)


def kernel(x, mem, positions, ffn1_norm, ffn1_w_gate, ffn1_w_up, ffn1_w_down, mix_norm, ab_w_in, ab_w_out, swa_q_norm, swa_k_norm, swa_sinks, rwkv_mu, rwkv_w0, rwkv_w2, rwkv_a0, rwkv_a2, rwkv_g2, rwkv_k_k, rwkv_k_a, rwkv_r_k, rwkv_gn_g, rwkv_gn_b, cd_w_in, cd_w_out, mla_cq_norm, mla_ckv_norm, mla_w_uq, mla_w_ukv, mla_q_nope_norm, mla_k_nope_norm, mla_q_rope_norm, mla_k_rope_norm, diff_q_norm, diff_k_norm, diff_lq1, diff_lk1, diff_lq2, diff_lk2, diff_subln, memx_norm, memx_w_q, memx_q_norm, memx_w_o, mem_norm, mem_w_kv, mem_k_norm, ffn2_norm, ffn2_w_gate, ffn2_w_up, ffn2_w_down):
    b, s, d = x.shape
    m = mem.shape[1]
    t = b * s
    depth = ffn1_norm.shape[0]
    bf = lambda w: w.astype(BF16)
    cos64, sin64 = _rope_tables(positions, HEAD_DIM, 0, 0)
    cos32, sin32 = _rope_tables(positions, MLA_ROPE, MLA_NOPE, LANES - MLA_NOPE - MLA_ROPE)

    mem_kv = norm_matmul(mem.reshape(b * m, d), mem_norm, bf(mem_w_kv), name="mem_kv").reshape(b, m, 2 * MEM_W)

    x = x.reshape(t, d)
    for layer in range(depth):
        j = layer // 2
        x = ffn(x, ffn1_norm[layer], bf(ffn1_w_gate[layer]), bf(ffn1_w_up[layer]), bf(ffn1_w_down[layer]),
                name="ffn1")
        if layer % 2 == 0:
            u = norm_matmul(x, mix_norm[layer], bf(_ab_in_layout(ab_w_in[j])), name="ab_in")
            u = u.reshape(b, s, AB_IN_PAD)
            y_a = swa_attention(u, cos64, sin64, _slab_vec(swa_q_norm[j], swa_q_norm[j]),
                                _slab_vec(swa_k_norm[j], swa_k_norm[j]), swa_sinks[j])
            prep = rwkv_prep(u, rwkv_mu[j], rwkv_w0[j], rwkv_w2[j], rwkv_a0[j], rwkv_a2[j], rwkv_g2[j],
                             rwkv_k_k[j], rwkv_k_a[j])
            y_b = rwkv_scan(*prep, rwkv_gn_g[j], rwkv_gn_b[j], rwkv_r_k[j])
            w_out = bf(ab_w_out[j])
            half = SWA_HEADS * HEAD_DIM
            x = matmul_residual(x, [y_a.reshape(t, -1), y_b.reshape(t, -1)], [w_out[:half], w_out[half:]],
                                name="ab_out")
        else:
            u = norm_matmul(x, mix_norm[layer], bf(_cd_in_layout(cd_w_in[j])), name="cd_in")
            q_lat = norm_matmul(u, mla_cq_norm[j], bf(_head_slabs(mla_w_uq[j], MLA_NOPE + MLA_ROPE)),
                                k_blk=0, name="mla_uq")
            kv = norm_matmul(u, mla_ckv_norm[j], bf(mla_w_ukv[j]), k_blk=MLA_Q_RANK // MLA_KV_RANK, name="mla_ukv")
            u = u.reshape(b, s, CD_IN_PAD)
            zero64 = jnp.zeros((MLA_NOPE,), F32)
            q_c, k_c, v_c = mla_prep(q_lat.reshape(b, s, -1), kv.reshape(b, s, -1), u, cos32, sin32,
                                     _slab_vec(mla_q_nope_norm[j], mla_q_rope_norm[j]),
                                     _slab_vec(mla_k_nope_norm[j]),
                                     _slab_vec(zero64, mla_k_rope_norm[j]))
            y_c = causal_attention(q_c, k_c, v_c, n_sm=1, name="mla_attention")
            q_d, k_d = diff_prep(u, cos64, sin64, _slab_vec(diff_q_norm[j], diff_q_norm[j]),
                                 _slab_vec(diff_k_norm[j], diff_k_norm[j]))
            lambda_init = 0.8 - 0.6 * math.exp(-0.3 * layer)
            lam = (jnp.exp(jnp.sum(diff_lq1[j].astype(F32) * diff_lk1[j].astype(F32)))
                   - jnp.exp(jnp.sum(diff_lq2[j].astype(F32) * diff_lk2[j].astype(F32))) + lambda_init)
            wd = 2 * DIFF_HEADS * DIFF_QK
            v_d = u[:, :, CD_IN_PAD - wd:].astype(BF16)
            y_d = causal_attention(q_d, k_d, v_d, n_sm=2, lam=lam, g=diff_subln[j],
                                   out_scale=1.0 - lambda_init, name="diff_attention")
            w_out = cd_w_out[j]
            n_c = MLA_HEADS * MLA_V
            w_c = jnp.pad(w_out[:n_c].reshape(MLA_HEADS, MLA_V, d), ((0, 0), (LANES - MLA_V, 0), (0, 0)))
            x = matmul_residual(x, [y_c.reshape(t, -1), y_d.reshape(t, -1)],
                                [bf(w_c.reshape(MLA_HEADS * LANES, d)), bf(w_out[n_c:])], name="cd_out")
        q_m = norm_matmul(x, memx_norm[layer], bf(memx_w_q[layer]), name="memx_q").reshape(b, s, MEM_W)
        o_m = mem_attention(q_m, mem_kv, memx_q_norm[layer], mem_k_norm)
        x = matmul_residual(x, [o_m.reshape(t, MEM_W)], [bf(memx_w_o[layer])], name="memx_out")
        x = ffn(x, ffn2_norm[layer], bf(ffn2_w_gate[layer]), bf(ffn2_w_up[layer]), bf(ffn2_w_down[layer]),
                name="ffn2")
    return x.reshape(b, s, d)
```

```python
import functools
import math

import jax
import jax.numpy as jnp
from jax import lax
from jax.experimental import pallas as pl
from jax.experimental.pallas import tpu as pltpu

F32 = jnp.float32
BF16 = jnp.bfloat16
HIGHEST = lax.Precision.HIGHEST

EPS = 1e-6
ROPE_THETA = 10000.0
NEG_INF = -1e30
ATTN_LOOKAHEAD = 4
LANES = 128

D_MODEL = 2048
D_FF = 5632
HEAD_DIM = 64
SWA_HEADS = 16
SWA_KV_HEADS = 4
SWA_BLOCK = 128
RWKV_DIM = 1024
RWKV_CHUNK = 64
RWKV_GN_EPS = 64e-5
DECAY_LORA, AAA_LORA, GATE_LORA = 64, 64, 160
MLA_HEADS, MLA_Q_RANK, MLA_KV_RANK, MLA_NOPE, MLA_ROPE, MLA_V = 16, 512, 256, 64, 32, 64
DIFF_HEADS, DIFF_QK, DIFF_V = 8, 64, 128
MEM_HEADS, MEM_HEAD_DIM = 4, 128
MEM_W = MEM_HEADS * MEM_HEAD_DIM
AB_IN_PAD = 5120
CD_IN_PAD = 4096

VMEM_LIMIT = 48 * 1024 * 1024


def _params(*sem):
    return pltpu.CompilerParams(dimension_semantics=sem, vmem_limit_bytes=VMEM_LIMIT)


def _dot(a, b):
    return jnp.dot(a, b, preferred_element_type=F32)


def _dot_t(a, b):
    return lax.dot_general(a, b, (((1,), (1,)), ((), ())), preferred_element_type=F32)


def _dot_0(a, b):
    return lax.dot_general(a, b, (((0,), (0,)), ((), ())), preferred_element_type=F32)


def _rms(x, g):
    return x * lax.rsqrt(jnp.mean(x * x, axis=-1, keepdims=True) + EPS) * g


def _seg_sum(x, e):
    hi = x.astype(BF16)
    lo = (x - hi.astype(F32)).astype(BF16)
    return _dot(hi, e) + _dot(lo, e)


def _tile_lanes(v, width):
    return v if v.shape[-1] == width else jnp.tile(v, (1, width // v.shape[-1]))


def _swap_halves(x, half):
    w = x.shape[-1]
    lane = lax.broadcasted_iota(jnp.int32, x.shape, x.ndim - 1)
    low = (lane & (2 * half - 1)) < half
    return jnp.where(low, pltpu.roll(x, w - half, x.ndim - 1), pltpu.roll(x, half, x.ndim - 1))


def _norm_matmul_kernel(x_ref, g_ref, w_ref, o_ref, xn_ref):
    @pl.when(pl.program_id(1) == 0)
    def _():
        xn_ref[...] = _rms(x_ref[...].astype(F32), g_ref[...]).astype(BF16)

    o_ref[...] = _dot(xn_ref[...], w_ref[...]).astype(o_ref.dtype)


def norm_matmul(x, g, w, *, k_blk=0, out_dtype=F32, tm=512, tn=512, name="norm_matmul"):
    t = x.shape[0]
    k, n = w.shape
    tm, tn = min(tm, t), min(tn, n)
    assert t % tm == 0 and n % tn == 0
    return pl.pallas_call(
        _norm_matmul_kernel,
        grid=(t // tm, n // tn),
        in_specs=[pl.BlockSpec((tm, k), lambda i, j: (i, k_blk)),
                  pl.BlockSpec((1, k), lambda i, j: (0, 0)),
                  pl.BlockSpec((k, tn), lambda i, j: (0, j))],
        out_specs=pl.BlockSpec((tm, tn), lambda i, j: (i, j)),
        out_shape=jax.ShapeDtypeStruct((t, n), out_dtype),
        scratch_shapes=[pltpu.VMEM((tm, k), BF16)],
        compiler_params=_params("parallel", "arbitrary"),
        name=name,
    )(x, g.reshape(1, k).astype(F32), w)


def _matmul_res_kernel(*refs, n_in):
    x_ref, o_ref = refs[2 * n_in], refs[2 * n_in + 1]
    acc = x_ref[...]
    for i in range(n_in):
        acc = acc + _dot(refs[i][...], refs[n_in + i][...])
    o_ref[...] = acc


def matmul_residual(x, a_list, w_list, *, tm=512, tn=512, name="matmul_residual"):
    t, n = x.shape
    tm, tn = min(tm, t), min(tn, n)
    assert t % tm == 0 and n % tn == 0
    n_in = len(a_list)
    in_specs = ([pl.BlockSpec((tm, a.shape[1]), lambda i, j: (i, 0)) for a in a_list]
                + [pl.BlockSpec((w.shape[0], tn), lambda i, j: (0, j)) for w in w_list]
                + [pl.BlockSpec((tm, tn), lambda i, j: (i, j))])
    return pl.pallas_call(
        functools.partial(_matmul_res_kernel, n_in=n_in),
        grid=(t // tm, n // tn),
        in_specs=in_specs,
        out_specs=pl.BlockSpec((tm, tn), lambda i, j: (i, j)),
        out_shape=jax.ShapeDtypeStruct((t, n), F32),
        compiler_params=_params("parallel", "arbitrary"),
        name=name,
    )(*a_list, *w_list, x)


def _ffn_kernel(x_ref, g_ref, wg_ref, wu_ref, wd_ref, o_ref, xn_ref, *, nf):
    f = pl.program_id(1)

    @pl.when(f == 0)
    def _():
        xn_ref[...] = _rms(x_ref[...], g_ref[...]).astype(BF16)
        o_ref[...] = jnp.zeros_like(o_ref)

    xn = xn_ref[...]
    a = _dot(xn, wg_ref[...])
    b = _dot(xn, wu_ref[...])
    h = (a * (1.0 / (1.0 + jnp.exp(-a))) * b).astype(BF16)
    o_ref[...] += _dot(h, wd_ref[...])

    @pl.when(f == nf - 1)
    def _():
        o_ref[...] = x_ref[...] + 0.5 * o_ref[...]


def ffn(x, g, wg, wu, wd, *, tm=512, tf=512, name="ffn"):
    t, d = x.shape
    ff = wg.shape[1]
    tm = min(tm, t)
    assert t % tm == 0 and ff % tf == 0
    nf = ff // tf
    return pl.pallas_call(
        functools.partial(_ffn_kernel, nf=nf),
        grid=(t // tm, nf),
        in_specs=[pl.BlockSpec((tm, d), lambda i, f: (i, 0)),
                  pl.BlockSpec((1, d), lambda i, f: (0, 0)),
                  pl.BlockSpec((d, tf), lambda i, f: (0, f)),
                  pl.BlockSpec((d, tf), lambda i, f: (0, f)),
                  pl.BlockSpec((tf, d), lambda i, f: (f, 0))],
        out_specs=pl.BlockSpec((tm, d), lambda i, f: (i, 0)),
        out_shape=jax.ShapeDtypeStruct((t, d), F32),
        scratch_shapes=[pltpu.VMEM((tm, d), BF16)],
        compiler_params=_params("parallel", "arbitrary"),
        name=name,
    )(x, g.reshape(1, d).astype(F32), wg, wu, wd)


def _swa_kernel(sink_ref, q_ref, kc_ref, kp_ref, vc_ref, vp_ref, cc_ref, sc_ref, cp_ref, sp_ref,
                gq_ref, gk_ref, o_ref):
    n = pl.program_id(1)
    blk = SWA_BLOCK
    group = SWA_HEADS // SWA_KV_HEADS
    q = q_ref[0]
    k = jnp.concatenate([kp_ref[0], kc_ref[0]], axis=0)
    v = jnp.concatenate([vp_ref[0], vc_ref[0]], axis=0)
    cos_q, sin_q = cc_ref[0], sc_ref[0]
    cos_k = jnp.concatenate([cp_ref[0], cos_q], axis=0)
    sin_k = jnp.concatenate([sp_ref[0], sin_q], axis=0)

    def rope(x, g, cos, sin):
        w = x.shape[1]
        xg = x * _tile_lanes(g, w)
        return xg * _tile_lanes(cos, w) + _swap_halves(xg, HEAD_DIM // 2) * _tile_lanes(sin, w)

    qr = rope(q, gq_ref[...], cos_q, sin_q)
    kr = rope(k, gk_ref[...], cos_k, sin_k)

    def inv_rms(x, h):
        xh = x[:, h * HEAD_DIM:(h + 1) * HEAD_DIM]
        return lax.rsqrt(jnp.mean(xh * xh, axis=-1, keepdims=True) + EPS)

    row = lax.broadcasted_iota(jnp.int32, (group * blk, 2 * blk), 0)
    col = lax.broadcasted_iota(jnp.int32, (group * blk, 2 * blk), 1)
    rel = (row & (blk - 1)) + blk - col
    valid = (rel >= 0) & (rel < blk) & ((n > 0) | (col >= blk))
    row1 = lax.broadcasted_iota(jnp.int32, (group * blk, 1), 0)
    scale = HEAD_DIM ** -0.5
    outs = []
    for g in range(SWA_KV_HEADS):
        sl = slice(g * HEAD_DIM, (g + 1) * HEAD_DIM)
        kg = (kr[:, sl] * inv_rms(k, g)).astype(BF16)
        vg = v[:, sl].astype(BF16)
        heads = range(g * group, (g + 1) * group)
        qs = jnp.concatenate(
            [qr[:, h * HEAD_DIM:(h + 1) * HEAD_DIM] * (inv_rms(q, h) * scale) for h in heads], axis=0).astype(BF16)
        s = jnp.where(valid, _dot_t(qs, kg), NEG_INF)
        sink = jnp.zeros((group * blk, 1), F32)
        for i, h in enumerate(heads):
            sink = jnp.where((row1 >= i * blk) & (row1 < (i + 1) * blk), sink_ref[h], sink)
        m = jnp.maximum(jnp.max(s, axis=-1, keepdims=True), sink)
        p = jnp.exp(s - m)
        den = jnp.sum(p, axis=-1, keepdims=True) + jnp.exp(sink - m)
        o = _dot(p.astype(BF16), vg) / den
        outs.extend(o[i * blk:(i + 1) * blk] for i in range(group))
    o_ref[0] = jnp.concatenate(outs, axis=-1).astype(o_ref.dtype)


def swa_attention(u, cos, sin, gq, gk, sinks):
    b, s, _ = u.shape
    nb = s // SWA_BLOCK
    qw, kw = SWA_HEADS * HEAD_DIM, SWA_KV_HEADS * HEAD_DIM
    cur = lambda c: (lambda i, n: (i, n, c))
    prev = lambda c: (lambda i, n: (i, jnp.maximum(n - 1, 0), c))
    tab = pl.BlockSpec((1, SWA_BLOCK, LANES), cur(0))
    tab_prev = pl.BlockSpec((1, SWA_BLOCK, LANES), prev(0))
    gain = pl.BlockSpec((1, LANES), lambda i, n: (0, 0))
    return pl.pallas_call(
        _swa_kernel,
        grid=(b, nb),
        in_specs=[pl.BlockSpec(memory_space=pltpu.SMEM),
                  pl.BlockSpec((1, SWA_BLOCK, qw), cur(0)),
                  pl.BlockSpec((1, SWA_BLOCK, kw), cur(qw // kw)),
                  pl.BlockSpec((1, SWA_BLOCK, kw), prev(qw // kw)),
                  pl.BlockSpec((1, SWA_BLOCK, kw), cur(qw // kw + 1)),
                  pl.BlockSpec((1, SWA_BLOCK, kw), prev(qw // kw + 1)),
                  tab, tab, tab_prev, tab_prev, gain, gain],
        out_specs=pl.BlockSpec((1, SWA_BLOCK, qw), cur(0)),
        out_shape=jax.ShapeDtypeStruct((b, s, qw), BF16),
        compiler_params=_params("parallel", "arbitrary"),
        name="swa_attention",
    )(sinks.astype(F32), u, u, u, u, u, cos, sin, cos, sin, gq, gk)


def _rwkv_prep_kernel(r_ref, k_ref, v_ref, l_ref, rp_ref, kp_ref, vp_ref, lp_ref,
                      mur_ref, muk_ref, muv_ref, mul_ref, w0_ref, w2_ref, a0_ref, a2_ref, g2_ref,
                      kk_ref, ka_ref, e_ref,
                      ro_ref, ko_ref, vo_ref, kko_ref, bo_ref, lwo_ref, go_ref):
    t = pl.program_id(1)

    def shifted(cur_ref, prev_ref, mu_ref):
        x = cur_ref[0]
        last = prev_ref[0][7:8, :] * jnp.where(t > 0, 1.0, 0.0)
        row = lax.broadcasted_iota(jnp.int32, x.shape, 0)
        prev = jnp.where(row == 0, last, pltpu.roll(x, 1, 0))
        return x + (prev - x) * mu_ref[...]

    r = shifted(r_ref, rp_ref, mur_ref)
    k = shifted(k_ref, kp_ref, muk_ref)
    v = shifted(v_ref, vp_ref, muv_ref)
    lo = shifted(l_ref, lp_ref, mul_ref)
    w_lo, a_lo, g_lo = lo[:, 0:LANES], lo[:, LANES:2 * LANES], lo[:, 2 * LANES:4 * LANES]
    hp = functools.partial(jnp.dot, precision=HIGHEST, preferred_element_type=F32)
    z = -(w0_ref[...] + hp(jnp.tanh(w_lo), w2_ref[...]))
    softplus = jnp.maximum(z, 0.0) + jnp.log(1.0 + jnp.exp(-jnp.abs(z)))
    w = -softplus - 0.5
    a = 1.0 / (1.0 + jnp.exp(-(a0_ref[...] + hp(a_lo, a2_ref[...]))))
    g = hp(1.0 / (1.0 + jnp.exp(-g_lo)), g2_ref[...])
    kk = k * kk_ref[...]
    kk = kk / jnp.maximum(jnp.sqrt(_seg_sum(kk * kk, e_ref[...])), 1e-12)
    ro_ref[0] = r
    ko_ref[0] = k * (1.0 + (a - 1.0) * ka_ref[...])
    vo_ref[0] = v
    kko_ref[0] = kk
    bo_ref[0] = kk * a
    lwo_ref[0] = -jnp.exp(w)
    go_ref[0] = g


def rwkv_prep(u, mu, w0, w2, a0, a2, g2, k_k, k_a, *, ts=512, tc=512):
    b, s, _ = u.shape
    ts = min(ts, s)
    nct = RWKV_DIM // tc
    base = (SWA_HEADS + 2 * SWA_KV_HEADS) * HEAD_DIM // tc
    lora_blk = base + 3 * nct

    def cur(off):
        return pl.BlockSpec((1, ts, tc), lambda i, t, c: (i, t, off + c))

    def prev(off):
        return pl.BlockSpec((1, 8, tc), lambda i, t, c: (i, jnp.maximum(t * (ts // 8) - 1, 0), off + c))

    lora_cur = pl.BlockSpec((1, ts, tc), lambda i, t, c: (i, t, lora_blk))
    lora_prev = pl.BlockSpec((1, 8, tc), lambda i, t, c: (i, jnp.maximum(t * (ts // 8) - 1, 0), lora_blk))
    vec = pl.BlockSpec((1, tc), lambda i, t, c: (0, c))
    vec0 = pl.BlockSpec((1, tc), lambda i, t, c: (0, 0))
    lora_w = lambda rows: pl.BlockSpec((rows, tc), lambda i, t, c: (0, c))
    pad_rows = lambda m, rows: jnp.pad(m, ((0, rows - m.shape[0]), (0, 0))).astype(F32)
    pad_cols = lambda vv, cols: jnp.pad(vv, (0, cols - vv.shape[0]))
    c3 = 3 * RWKV_DIM
    mu_l = jnp.concatenate([pad_cols(mu[c3:c3 + DECAY_LORA], LANES),
                            pad_cols(mu[c3 + DECAY_LORA:c3 + DECAY_LORA + AAA_LORA], LANES),
                            pad_cols(mu[c3 + DECAY_LORA + AAA_LORA:], 2 * LANES)])
    row = lambda vv: vv.reshape(1, -1).astype(F32)
    seg = (jnp.arange(tc)[:, None] // HEAD_DIM == jnp.arange(tc)[None, :] // HEAD_DIM).astype(BF16)
    out = jax.ShapeDtypeStruct((b, s, RWKV_DIM), F32)
    out_spec = pl.BlockSpec((1, ts, tc), lambda i, t, c: (i, t, c))
    return pl.pallas_call(
        _rwkv_prep_kernel,
        grid=(b, s // ts, nct),
        in_specs=[cur(base), cur(base + nct), cur(base + 2 * nct), lora_cur,
                  prev(base), prev(base + nct), prev(base + 2 * nct), lora_prev,
                  vec, vec, vec, vec0, vec, lora_w(LANES), vec, lora_w(LANES), lora_w(2 * LANES),
                  vec, vec, pl.BlockSpec((tc, tc), lambda i, t, c: (0, 0))],
        out_specs=[out_spec] * 7,
        out_shape=[out] * 7,
        compiler_params=_params("parallel", "parallel", "arbitrary"),
        name="rwkv_prep",
    )(u, u, u, u, u, u, u, u,
      row(mu[:RWKV_DIM]), row(mu[RWKV_DIM:2 * RWKV_DIM]), row(mu[2 * RWKV_DIM:c3]), row(mu_l),
      row(w0), pad_rows(w2, LANES), row(a0), pad_rows(a2, LANES), pad_rows(g2, 2 * LANES),
      row(k_k), row(k_a), seg)


def _mm(a, b, dims, passes):
    dn = (dims, ((), ()))
    dg = lambda x, y: lax.dot_general(x, y, dn, preferred_element_type=F32)
    ah = a.astype(BF16)
    bh = b.astype(BF16)
    if passes == 1:
        return dg(ah, bh)
    al = (a - ah.astype(F32)).astype(BF16)
    bl = (b - bh.astype(F32)).astype(BF16)
    return dg(ah, bh) + dg(ah, bl) + dg(al, bh)


_NN = ((1,), (0,))
_NT = ((1,), (1,))
_TN = ((0,), (0,))
P_SC, P_INV, P_PQ, P_OUT, P_ST = 1, 1, 1, 1, 1
RWKV_UNROLL = 8
RWKV_SCAN_UNROLL = 4


def _rwkv_scan_kernel(r_ref, k_ref, v_ref, kk_ref, b_ref, lw_ref, g_ref, gng_ref, gnb_ref, rk_ref,
                      o_ref, st_ref, y1_ref, y0_ref, n_ref, z_ref, dec_ref, *, nchunk):
    c = RWKV_CHUNK
    lane_c = lax.broadcasted_iota(jnp.int32, (c, LANES), 1)
    head0 = lane_c < HEAD_DIM
    ri = lax.broadcasted_iota(jnp.int32, (2 * c, 2 * c), 0)
    ci = lax.broadcasted_iota(jnp.int32, (2 * c, 2 * c), 1)
    eye = jnp.where(ri == ci, 1.0, 0.0)
    tril_c = jnp.where(lax.broadcasted_iota(jnp.int32, (c, c), 0) >= lax.broadcasted_iota(jnp.int32, (c, c), 1),
                       1.0, 0.0).astype(BF16)
    stack = lambda x: jnp.concatenate([jnp.where(head0, x, 0.0), jnp.where(head0, 0.0, x)], axis=0)

    def seg_mean(x):
        m0 = jnp.sum(jnp.where(head0, x, 0.0), axis=-1, keepdims=True)
        m1 = jnp.sum(jnp.where(head0, 0.0, x), axis=-1, keepdims=True)
        return jnp.where(head0, m0, m1) * (1.0 / HEAD_DIM)

    def build(ics):
        each = lambda f, *cols: [f(*args) for args in zip(*cols)]
        sls = [pl.ds(pl.multiple_of(ic * c, c), c) for ic in ics]
        load = lambda ref: [ref[0, sl, :] for sl in sls]
        r, k, v, kk, b, lw = (load(ref) for ref in (r_ref, k_ref, v_ref, kk_ref, b_ref, lw_ref))

        def running_sum(x):
            l1 = x.astype(BF16)
            rest = x - l1.astype(F32)
            l2 = rest.astype(BF16)
            l3 = (rest - l2.astype(F32)).astype(BF16)
            return _dot(tril_c, l1) + _dot(tril_c, l2) + _dot(tril_c, l3)

        cum = each(running_sum, lw)
        cum_end = [x[c - 1:c, :] for x in cum]
        e_neg = each(lambda x: jnp.exp(-x), cum)
        e_end = each(lambda x, xe: jnp.exp(xe - x), cum, cum_end)
        a_s = each(lambda kk_, x, l: stack(-kk_ * jnp.exp(x - l)), kk, cum, lw)
        r_s = each(lambda r_, x: stack(r_ * jnp.exp(x)), r, cum)
        b_s = each(lambda b_, e: stack(b_ * e), b, e_neg)
        k_s = each(lambda k_, e: stack(k_ * e), k, e_neg)
        bh_s = each(lambda b_, e: stack(b_ * e), b, e_end)
        kh_s = each(lambda k_, e: stack(k_ * e), k, e_end)
        v_s = each(stack, v)
        strict = lambda x, y: jnp.where(ri > ci, _mm(x, y, _NT, P_SC), 0.0)
        incl = lambda x, y: jnp.where(ri >= ci, _mm(x, y, _NT, P_SC), 0.0)
        low = each(strict, a_s, b_s)
        a_ak = each(strict, a_s, k_s)
        a_rb = each(incl, r_s, b_s)
        a_rk = each(incl, r_s, k_s)
        inv = [eye + x for x in low]
        pw = low
        for _ in range(5):
            pw = each(lambda x: _mm(x, x, _NN, P_INV), pw)
            inv = each(lambda t, x: t + _mm(t, x, _NN, P_INV), inv, pw)
        akv = each(lambda x, y: _mm(x, y, _NN, P_PQ), a_ak, v_s)
        p = each(lambda t, x: _mm(t, x, _NN, P_PQ), inv, a_s)
        qv = each(lambda t, x: _mm(t, x, _NN, P_PQ), inv, akv)
        y1 = each(lambda rs, x, y: rs + _mm(x, y, _NN, P_OUT), r_s, a_rb, p)
        y0 = each(lambda x, y, z, w: _mm(x, y, _NN, P_OUT) + _mm(z, w, _NN, P_OUT), a_rb, qv, a_rk, v_s)
        nn = each(lambda x, y: _mm(x, y, _TN, P_OUT), p, bh_s)
        zz = each(lambda x, y, z, w: _mm(x, y, _TN, P_OUT) + _mm(z, w, _TN, P_OUT), qv, bh_s, v_s, kh_s)
        for i, ic in enumerate(ics):
            y1_ref[ic] = y1[i]
            y0_ref[ic] = y0[i]
            n_ref[ic] = nn[i]
            z_ref[ic] = zz[i]
            dec_ref[ic] = jnp.broadcast_to(jnp.exp(cum_end[i]), (8, LANES))

    @pl.loop(0, nchunk // RWKV_UNROLL)
    def _(io):
        build([io * RWKV_UNROLL + i for i in range(RWKV_UNROLL)])

    st_ref[...] = jnp.zeros_like(st_ref)

    def emit(ic, st):
        sl = pl.ds(pl.multiple_of(ic * c, c), c)
        r, k, v, g = r_ref[0, sl, :], k_ref[0, sl, :], v_ref[0, sl, :], g_ref[0, sl, :]
        y_st = _mm(y1_ref[ic], st, _NT, P_ST) + y0_ref[ic]
        y = y_st[0:c] + y_st[c:2 * c]
        mean = seg_mean(y)
        var = seg_mean((y - mean) * (y - mean))
        yn = (y - mean) * lax.rsqrt(var + RWKV_GN_EPS) * gng_ref[...] + gnb_ref[...]
        bonus = seg_mean(r * k * rk_ref[...]) * float(HEAD_DIM) * v
        o_ref[0, sl, :] = ((yn + bonus) * g).astype(o_ref.dtype)
        return st * dec_ref[ic][0:1, :] + _mm(st, n_ref[ic], _NN, P_ST) + z_ref[ic]

    @pl.loop(0, nchunk // RWKV_SCAN_UNROLL)
    def _(io):
        st = st_ref[...]
        for i in range(RWKV_SCAN_UNROLL):
            st = emit(io * RWKV_SCAN_UNROLL + i, st)
        st_ref[...] = st


def rwkv_scan(r, k, v, kk, b, lw, g, gn_g, gn_b, r_k):
    bsz, s, _ = r.shape
    npair = RWKV_DIM // LANES
    nchunk = s // RWKV_CHUNK
    assert nchunk % RWKV_UNROLL == 0 and nchunk % RWKV_SCAN_UNROLL == 0
    seq = pl.BlockSpec((1, s, LANES), lambda i, p: (i, 0, p))
    vec = pl.BlockSpec((1, LANES), lambda i, p: (0, p))
    row = lambda vv: vv.reshape(1, -1).astype(F32)
    mat = pltpu.VMEM((nchunk, LANES, LANES), F32)
    return pl.pallas_call(
        functools.partial(_rwkv_scan_kernel, nchunk=nchunk),
        grid=(bsz, npair),
        in_specs=[seq] * 7 + [vec] * 3,
        out_specs=seq,
        out_shape=jax.ShapeDtypeStruct((bsz, s, RWKV_DIM), BF16),
        scratch_shapes=[pltpu.VMEM((LANES, LANES), F32), mat, mat, mat, mat,
                        pltpu.VMEM((nchunk, 8, LANES), F32)],
        compiler_params=_params("parallel", "parallel"),
        name="rwkv_scan",
    )(r, k, v, kk, b, lw, g, row(gn_g), row(gn_b), row(r_k))


def _mla_prep_kernel(q_ref, kv_ref, pe_ref, cos_ref, sin_ref, e_ref, gq_ref, gkn_ref, gkp_ref, invn_ref,
                     qo_ref, ko_ref, vo_ref):
    cos, sin = cos_ref[0], sin_ref[0]
    half = MLA_ROPE // 2

    def rope(x, g):
        w = x.shape[1]
        xg = x * _tile_lanes(g, w)
        return xg * _tile_lanes(cos, w) + _swap_halves(xg, half) * _tile_lanes(sin, w)

    x = q_ref[0]
    w = x.shape[1]
    inv_n = _tile_lanes(invn_ref[...], w)
    inv = lax.rsqrt(_seg_sum(x * x, e_ref[...]) * inv_n + EPS)
    qo_ref[0] = (rope(x, gq_ref[...]) * inv * (MLA_NOPE + MLA_ROPE) ** -0.5).astype(qo_ref.dtype)
    kv = kv_ref[0]
    inv_k = lax.rsqrt(_seg_sum(kv * kv, e_ref[...]) * inv_n + EPS)
    k_nope = kv * inv_k * _tile_lanes(gkn_ref[...], w)
    pe = pe_ref[0]
    inv_pe = lax.rsqrt(jnp.sum(pe * pe, axis=-1, keepdims=True) * (1.0 / MLA_ROPE) + EPS)
    k_pe = rope(pe, gkp_ref[...]) * inv_pe
    ko_ref[0] = (k_nope + _tile_lanes(k_pe, w)).astype(ko_ref.dtype)
    vo_ref[0] = kv.astype(vo_ref.dtype)


def mla_prep(q_lat, kv, u, cos, sin, gq, gkn, gkp, *, ts=512, tc=512):
    b, s, wtot = q_lat.shape
    ts = min(ts, s)
    lane = jnp.arange(tc)
    same = (lane[:, None] // LANES == lane[None, :] // LANES)
    pos = lane % LANES
    nope = pos < MLA_NOPE
    pe = (pos >= MLA_NOPE) & (pos < MLA_NOPE + MLA_ROPE)
    seg = (same & ((nope[:, None] & nope[None, :]) | (pe[:, None] & pe[None, :]))).astype(BF16)
    p1 = jnp.arange(LANES)
    inv_n = jnp.where(p1 < MLA_NOPE, 1.0 / MLA_NOPE, jnp.where(p1 < MLA_NOPE + MLA_ROPE, 1.0 / MLA_ROPE, 0.0))
    blk = pl.BlockSpec((1, ts, tc), lambda i, t, c: (i, t, c))
    tab = pl.BlockSpec((1, ts, LANES), lambda i, t, c: (i, t, 0))
    vec = pl.BlockSpec((1, LANES), lambda i, t, c: (0, 0))
    pe_blk = (MLA_Q_RANK + MLA_KV_RANK) // LANES
    out = jax.ShapeDtypeStruct((b, s, wtot), BF16)
    return pl.pallas_call(
        _mla_prep_kernel,
        grid=(b, s // ts, wtot // tc),
        in_specs=[blk, blk, pl.BlockSpec((1, ts, LANES), lambda i, t, c: (i, t, pe_blk)), tab, tab,
                  pl.BlockSpec((tc, tc), lambda i, t, c: (0, 0)), vec, vec, vec, vec],
        out_specs=[blk] * 3,
        out_shape=[out] * 3,
        compiler_params=_params("parallel", "parallel", "arbitrary"),
        name="mla_prep",
    )(q_lat, kv, u, cos, sin, seg, gq, gkn, gkp, inv_n.reshape(1, LANES).astype(F32))


def _diff_prep_kernel(q_ref, k_ref, cos_ref, sin_ref, e_ref, gq_ref, gk_ref, qo_ref, ko_ref):
    cos, sin = cos_ref[0], sin_ref[0]

    def prep(x, g, scale):
        w = x.shape[1]
        xg = x * _tile_lanes(g, w)
        xr = xg * _tile_lanes(cos, w) + _swap_halves(xg, DIFF_QK // 2) * _tile_lanes(sin, w)
        inv = lax.rsqrt(_seg_sum(x * x, e_ref[...]) * (1.0 / DIFF_QK) + EPS)
        return xr * inv * scale

    qo_ref[0] = prep(q_ref[0], gq_ref[...], DIFF_QK ** -0.5).astype(qo_ref.dtype)
    ko_ref[0] = prep(k_ref[0], gk_ref[...], 1.0).astype(ko_ref.dtype)


def diff_prep(u, cos, sin, gq, gk, *, ts=512, tc=512):
    b, s, _ = u.shape
    ts = min(ts, s)
    wtot = 2 * DIFF_HEADS * DIFF_QK
    q_base = (CD_IN_PAD - 3 * wtot) // tc
    seg = (jnp.arange(tc)[:, None] // DIFF_QK == jnp.arange(tc)[None, :] // DIFF_QK).astype(BF16)
    blk = lambda off: pl.BlockSpec((1, ts, tc), lambda i, t, c: (i, t, off + c))
    tab = pl.BlockSpec((1, ts, LANES), lambda i, t, c: (i, t, 0))
    vec = pl.BlockSpec((1, LANES), lambda i, t, c: (0, 0))
    out = jax.ShapeDtypeStruct((b, s, wtot), BF16)
    return pl.pallas_call(
        _diff_prep_kernel,
        grid=(b, s // ts, wtot // tc),
        in_specs=[blk(q_base), blk(q_base + wtot // tc), tab, tab,
                  pl.BlockSpec((tc, tc), lambda i, t, c: (0, 0)), vec, vec],
        out_specs=[blk(0)] * 2,
        out_shape=[out] * 2,
        compiler_params=_params("parallel", "parallel", "arbitrary"),
        name="diff_prep",
    )(u, u, cos, sin, seg, gq, gk)


def _causal_attn_kernel(lam_ref, q_ref, k_ref, vt_ref, g_ref, o_ref, *, n_sm, tq, out_scale):
    qi = pl.program_id(2)
    q = q_ref[0]
    hps = q.shape[1] // LANES
    slab = lambda x, h: x[:, h * LANES:(h + 1) * LANES]
    lane = lax.broadcasted_iota(jnp.int32, (tq, LANES), 1)
    qs, src = [], []
    for h in range(hps):
        qh = slab(q, h)
        if n_sm == 2:
            zero = jnp.zeros_like(qh)
            qs += [jnp.where(lane < DIFF_QK, qh, zero), jnp.where(lane < DIFF_QK, zero, qh)]
            src += [h, h]
        else:
            qs.append(qh)
            src.append(h)
    nch = len(qs)
    key_i = lax.broadcasted_iota(jnp.int32, (tq, tq), 0)
    qry_i = lax.broadcasted_iota(jnp.int32, (tq, tq), 1)

    def step(j, carry, diagonal):
        kj = k_ref[0, pl.ds(pl.multiple_of(j * tq, tq), tq), :]
        scores = lambda i: _dot_t(slab(kj, src[i]), qs[i])
        new = []
        ahead = [scores(i) for i in range(min(ATTN_LOOKAHEAD, nch))]
        for i in range(nch):
            s = ahead.pop(0)
            if i + ATTN_LOOKAHEAD < nch:
                ahead.append(scores(i + ATTN_LOOKAHEAD))
            if diagonal:
                s = jnp.where(key_i <= qry_i, s, NEG_INF)
            m, l, acc = carry[3 * i:3 * i + 3]
            m_new = jnp.maximum(m, jnp.max(s, axis=0, keepdims=True))
            alpha = jnp.exp(m - m_new)
            p = jnp.exp(s - m_new)
            new += [m_new, alpha * l + jnp.sum(p, axis=0, keepdims=True),
                    alpha * acc + _dot(vt_ref[0, src[i], j], p.astype(BF16))]
        return tuple(new)

    init = (jnp.full((1, tq), NEG_INF, F32), jnp.zeros((1, tq), F32), jnp.zeros((LANES, tq), F32)) * nch
    carry = lax.fori_loop(0, qi, lambda j, cr: step(j, cr, False), init)
    carry = step(qi, carry, True)
    outs = []
    for h in range(hps):
        c0 = 3 * n_sm * h
        o = carry[c0 + 2] / carry[c0 + 1]
        if n_sm == 2:
            o = o - lam_ref[0] * (carry[c0 + 5] / carry[c0 + 4])
            o = o * lax.rsqrt(jnp.mean(o * o, axis=0, keepdims=True) + EPS) * g_ref[...] * out_scale
        outs.append(o.T.astype(o_ref.dtype))
    o_ref[0] = jnp.concatenate(outs, axis=-1)


def causal_attention(q, k, v, *, n_sm, lam=None, g=None, out_scale=1.0, tq=256, hps=4,
                     name="causal_attention"):
    b, s, wtot = q.shape
    tq = min(tq, s)
    width = hps * LANES
    groups = wtot // width
    lam = jnp.zeros((1,), F32) if lam is None else lam.reshape(1).astype(F32)
    g = jnp.ones((LANES, 1), F32) if g is None else g.reshape(LANES, 1).astype(F32)
    vt = v.reshape(b, s // tq, tq, wtot // LANES, LANES).transpose(0, 3, 1, 4, 2)
    seq = pl.BlockSpec((1, s, width), lambda i, h, t: (i, 0, h))
    tile = pl.BlockSpec((1, tq, width), lambda i, h, t: (i, t, h))
    return pl.pallas_call(
        functools.partial(_causal_attn_kernel, n_sm=n_sm, tq=tq, out_scale=out_scale),
        grid=(b, groups, s // tq),
        in_specs=[pl.BlockSpec(memory_space=pltpu.SMEM), tile, seq,
                  pl.BlockSpec((1, hps, s // tq, LANES, tq), lambda i, h, t: (i, h, 0, 0, 0)),
                  pl.BlockSpec((LANES, 1), lambda i, h, t: (0, 0))],
        out_specs=tile,
        out_shape=jax.ShapeDtypeStruct((b, s, wtot), BF16),
        compiler_params=_params("parallel", "parallel", "arbitrary"),
        name=name,
    )(lam, q, k, vt, g)


def _mem_attn_kernel(q_ref, kv_ref, gq_ref, gk_ref, o_ref):
    q = q_ref[0]
    kv = kv_ref[0]
    outs = []
    for h in range(MEM_HEADS):
        sl = slice(h * MEM_HEAD_DIM, (h + 1) * MEM_HEAD_DIM)
        qh = (_rms(q[:, sl], gq_ref[...]) * MEM_HEAD_DIM ** -0.5).astype(BF16)
        kh = _rms(kv[:, sl], gk_ref[...]).astype(BF16)
        vh = kv[:, MEM_W + h * MEM_HEAD_DIM:MEM_W + (h + 1) * MEM_HEAD_DIM].astype(BF16)
        s = _dot_t(qh, kh)
        p = jnp.exp(s - jnp.max(s, axis=-1, keepdims=True))
        outs.append(_dot(p.astype(BF16), vh) / jnp.sum(p, axis=-1, keepdims=True))
    o_ref[0] = jnp.concatenate(outs, axis=-1).astype(o_ref.dtype)


def mem_attention(q, mem_kv, gq, gk, *, tq=512):
    b, s, _ = q.shape
    m = mem_kv.shape[1]
    tq = min(tq, s)
    vec = pl.BlockSpec((1, MEM_HEAD_DIM), lambda i, t: (0, 0))
    return pl.pallas_call(
        _mem_attn_kernel,
        grid=(b, s // tq),
        in_specs=[pl.BlockSpec((1, tq, MEM_W), lambda i, t: (i, t, 0)),
                  pl.BlockSpec((1, m, 2 * MEM_W), lambda i, t: (i, 0, 0)), vec, vec],
        out_specs=pl.BlockSpec((1, tq, MEM_W), lambda i, t: (i, t, 0)),
        out_shape=jax.ShapeDtypeStruct((b, s, MEM_W), BF16),
        compiler_params=_params("parallel", "arbitrary"),
        name="mem_attention",
    )(q, mem_kv, gq.reshape(1, -1).astype(F32), gk.reshape(1, -1).astype(F32))


def _rope_tables(positions, dim, lead_ones, tail):
    inv = 1.0 / (ROPE_THETA ** (jnp.arange(0, dim, 2, dtype=F32) / dim))
    ang = positions.astype(F32)[..., None] * inv
    c, s = jnp.cos(ang), jnp.sin(ang)
    shape = positions.shape
    cos = jnp.concatenate([jnp.ones(shape + (lead_ones,), F32), c, c, jnp.ones(shape + (tail,), F32)], axis=-1)
    sin = jnp.concatenate([jnp.zeros(shape + (lead_ones,), F32), -s, s, jnp.zeros(shape + (tail,), F32)], axis=-1)
    reps = LANES // cos.shape[-1]
    return jnp.tile(cos, (1, 1, reps)), jnp.tile(sin, (1, 1, reps))


def _pad_cols(w, cols):
    return jnp.pad(w, ((0, 0), (0, cols - w.shape[1])))


def _ab_in_layout(w):
    c = (SWA_HEADS + 2 * SWA_KV_HEADS) * HEAD_DIM + 3 * RWKV_DIM
    return jnp.concatenate([w[:, :c], _pad_cols(w[:, c:c + DECAY_LORA], LANES),
                            _pad_cols(w[:, c + DECAY_LORA:c + DECAY_LORA + AAA_LORA], LANES),
                            _pad_cols(w[:, c + DECAY_LORA + AAA_LORA:], 2 * LANES)], axis=1)


def _cd_in_layout(w):
    c1 = MLA_Q_RANK + MLA_KV_RANK
    z = lambda n: jnp.zeros((w.shape[0], n), w.dtype)
    return jnp.concatenate([w[:, :c1], z(MLA_NOPE), w[:, c1:c1 + MLA_ROPE], z(LANES - MLA_NOPE - MLA_ROPE),
                            z(LANES), w[:, c1 + MLA_ROPE:]], axis=1)


def _head_slabs(w, per_head):
    k = w.shape[0]
    return jnp.pad(w.reshape(k, -1, per_head), ((0, 0), (0, 0), (0, LANES - per_head))).reshape(k, -1)


def _slab_vec(*parts):
    v = jnp.concatenate([p.astype(F32) for p in parts])
    return jnp.pad(v, (0, LANES - v.shape[0])).reshape(1, LANES)


def kernel(x, mem, positions, ffn1_norm, ffn1_w_gate, ffn1_w_up, ffn1_w_down, mix_norm, ab_w_in, ab_w_out, swa_q_norm, swa_k_norm, swa_sinks, rwkv_mu, rwkv_w0, rwkv_w2, rwkv_a0, rwkv_a2, rwkv_g2, rwkv_k_k, rwkv_k_a, rwkv_r_k, rwkv_gn_g, rwkv_gn_b, cd_w_in, cd_w_out, mla_cq_norm, mla_ckv_norm, mla_w_uq, mla_w_ukv, mla_q_nope_norm, mla_k_nope_norm, mla_q_rope_norm, mla_k_rope_norm, diff_q_norm, diff_k_norm, diff_lq1, diff_lk1, diff_lq2, diff_lk2, diff_subln, memx_norm, memx_w_q, memx_q_norm, memx_w_o, mem_norm, mem_w_kv, mem_k_norm, ffn2_norm, ffn2_w_gate, ffn2_w_up, ffn2_w_down):
    b, s, d = x.shape
    m = mem.shape[1]
    t = b * s
    depth = ffn1_norm.shape[0]
    bf = lambda w: w.astype(BF16)
    cos64, sin64 = _rope_tables(positions, HEAD_DIM, 0, 0)
    cos32, sin32 = _rope_tables(positions, MLA_ROPE, MLA_NOPE, LANES - MLA_NOPE - MLA_ROPE)

    mem_kv = norm_matmul(mem.reshape(b * m, d), mem_norm, bf(mem_w_kv), name="mem_kv").reshape(b, m, 2 * MEM_W)

    x = x.reshape(t, d)
    for layer in range(depth):
        j = layer // 2
        x = ffn(x, ffn1_norm[layer], bf(ffn1_w_gate[layer]), bf(ffn1_w_up[layer]), bf(ffn1_w_down[layer]),
                name="ffn1")
        if layer % 2 == 0:
            u = norm_matmul(x, mix_norm[layer], bf(_ab_in_layout(ab_w_in[j])), name="ab_in")
            u = u.reshape(b, s, AB_IN_PAD)
            y_a = swa_attention(u, cos64, sin64, _slab_vec(swa_q_norm[j], swa_q_norm[j]),
                                _slab_vec(swa_k_norm[j], swa_k_norm[j]), swa_sinks[j])
            prep = rwkv_prep(u, rwkv_mu[j], rwkv_w0[j], rwkv_w2[j], rwkv_a0[j], rwkv_a2[j], rwkv_g2[j],
                             rwkv_k_k[j], rwkv_k_a[j])
            y_b = rwkv_scan(*prep, rwkv_gn_g[j], rwkv_gn_b[j], rwkv_r_k[j])
            w_out = bf(ab_w_out[j])
            half = SWA_HEADS * HEAD_DIM
            x = matmul_residual(x, [y_a.reshape(t, -1), y_b.reshape(t, -1)], [w_out[:half], w_out[half:]],
                                name="ab_out")
        else:
            u = norm_matmul(x, mix_norm[layer], bf(_cd_in_layout(cd_w_in[j])), name="cd_in")
            q_lat = norm_matmul(u, mla_cq_norm[j], bf(_head_slabs(mla_w_uq[j], MLA_NOPE + MLA_ROPE)),
                                k_blk=0, name="mla_uq")
            kv = norm_matmul(u, mla_ckv_norm[j], bf(mla_w_ukv[j]), k_blk=MLA_Q_RANK // MLA_KV_RANK, name="mla_ukv")
            u = u.reshape(b, s, CD_IN_PAD)
            zero64 = jnp.zeros((MLA_NOPE,), F32)
            q_c, k_c, v_c = mla_prep(q_lat.reshape(b, s, -1), kv.reshape(b, s, -1), u, cos32, sin32,
                                     _slab_vec(mla_q_nope_norm[j], mla_q_rope_norm[j]),
                                     _slab_vec(mla_k_nope_norm[j]),
                                     _slab_vec(zero64, mla_k_rope_norm[j]))
            y_c = causal_attention(q_c, k_c, v_c, n_sm=1, hps=8, name="mla_attention")
            q_d, k_d = diff_prep(u, cos64, sin64, _slab_vec(diff_q_norm[j], diff_q_norm[j]),
                                 _slab_vec(diff_k_norm[j], diff_k_norm[j]))
            lambda_init = 0.8 - 0.6 * math.exp(-0.3 * layer)
            lam = (jnp.exp(jnp.sum(diff_lq1[j].astype(F32) * diff_lk1[j].astype(F32)))
                   - jnp.exp(jnp.sum(diff_lq2[j].astype(F32) * diff_lk2[j].astype(F32))) + lambda_init)
            wd = 2 * DIFF_HEADS * DIFF_QK
            v_d = u[:, :, CD_IN_PAD - wd:].astype(BF16)
            y_d = causal_attention(q_d, k_d, v_d, n_sm=2, hps=4, lam=lam, g=diff_subln[j],
                                   out_scale=1.0 - lambda_init, name="diff_attention")
            w_out = cd_w_out[j]
            n_c = MLA_HEADS * MLA_V
            w_c = jnp.pad(w_out[:n_c].reshape(MLA_HEADS, MLA_V, d), ((0, 0), (LANES - MLA_V, 0), (0, 0)))
            x = matmul_residual(x, [y_c.reshape(t, -1), y_d.reshape(t, -1)],
                                [bf(w_c.reshape(MLA_HEADS * LANES, d)), bf(w_out[n_c:])], name="cd_out")
        q_m = norm_matmul(x, memx_norm[layer], bf(memx_w_q[layer]), name="memx_q").reshape(b, s, MEM_W)
        o_m = mem_attention(q_m, mem_kv, memx_q_norm[layer], mem_k_norm)
        x = matmul_residual(x, [o_m.reshape(t, MEM_W)], [bf(memx_w_o[layer])], name="memx_out")
        x = ffn(x, ffn2_norm[layer], bf(ffn2_w_gate[layer]), bf(ffn2_w_up[layer]), bf(ffn2_w_down[layer]),
                name="ffn2")
    return x.reshape(b, s, d)
```

```python
import functools
import math

import jax
import jax.numpy as jnp
from jax import lax
from jax.experimental import pallas as pl
from jax.experimental.pallas import tpu as pltpu

F32 = jnp.float32
BF16 = jnp.bfloat16
HIGHEST = lax.Precision.HIGHEST

EPS = 1e-6
ROPE_THETA = 10000.0
NEG_INF = -1e30
ATTN_LOOKAHEAD = 4
LANES = 128

D_MODEL = 2048
D_FF = 5632
HEAD_DIM = 64
SWA_HEADS = 16
SWA_KV_HEADS = 4
SWA_BLOCK = 128
RWKV_DIM = 1024
RWKV_CHUNK = 64
RWKV_GN_EPS = 64e-5
DECAY_LORA, AAA_LORA, GATE_LORA = 64, 64, 160
MLA_HEADS, MLA_Q_RANK, MLA_KV_RANK, MLA_NOPE, MLA_ROPE, MLA_V = 16, 512, 256, 64, 32, 64
DIFF_HEADS, DIFF_QK, DIFF_V = 8, 64, 128
MEM_HEADS, MEM_HEAD_DIM = 4, 128
MEM_W = MEM_HEADS * MEM_HEAD_DIM
AB_IN_PAD = 5120
CD_IN_PAD = 4096

VMEM_LIMIT = 48 * 1024 * 1024


def _params(*sem):
    return pltpu.CompilerParams(dimension_semantics=sem, vmem_limit_bytes=VMEM_LIMIT)


def _dot(a, b):
    return jnp.dot(a, b, preferred_element_type=F32)


def _dot_t(a, b):
    return lax.dot_general(a, b, (((1,), (1,)), ((), ())), preferred_element_type=F32)


def _dot_0(a, b):
    return lax.dot_general(a, b, (((0,), (0,)), ((), ())), preferred_element_type=F32)


def _rms(x, g):
    return x * lax.rsqrt(jnp.mean(x * x, axis=-1, keepdims=True) + EPS) * g


def _seg_sum(x, e):
    hi = x.astype(BF16)
    lo = (x - hi.astype(F32)).astype(BF16)
    return _dot(hi, e) + _dot(lo, e)


def _tile_lanes(v, width):
    return v if v.shape[-1] == width else jnp.tile(v, (1, width // v.shape[-1]))


def _swap_halves(x, half):
    w = x.shape[-1]
    lane = lax.broadcasted_iota(jnp.int32, x.shape, x.ndim - 1)
    low = (lane & (2 * half - 1)) < half
    return jnp.where(low, pltpu.roll(x, w - half, x.ndim - 1), pltpu.roll(x, half, x.ndim - 1))


def _norm_matmul_kernel(x_ref, g_ref, w_ref, o_ref, xn_ref):
    @pl.when(pl.program_id(1) == 0)
    def _():
        xn_ref[...] = _rms(x_ref[...].astype(F32), g_ref[...]).astype(BF16)

    o_ref[...] = _dot(xn_ref[...], w_ref[...]).astype(o_ref.dtype)


def norm_matmul(x, g, w, *, k_blk=0, out_dtype=F32, tm=1024, tn=512, name="norm_matmul"):
    t = x.shape[0]
    k, n = w.shape
    tm, tn = min(tm, t), min(tn, n)
    assert t % tm == 0 and n % tn == 0
    return pl.pallas_call(
        _norm_matmul_kernel,
        grid=(t // tm, n // tn),
        in_specs=[pl.BlockSpec((tm, k), lambda i, j: (i, k_blk)),
                  pl.BlockSpec((1, k), lambda i, j: (0, 0)),
                  pl.BlockSpec((k, tn), lambda i, j: (0, j))],
        out_specs=pl.BlockSpec((tm, tn), lambda i, j: (i, j)),
        out_shape=jax.ShapeDtypeStruct((t, n), out_dtype),
        scratch_shapes=[pltpu.VMEM((tm, k), BF16)],
        compiler_params=_params("parallel", "arbitrary"),
        name=name,
    )(x, g.reshape(1, k).astype(F32), w)


def _matmul_res_kernel(*refs, n_in):
    x_ref, o_ref = refs[2 * n_in], refs[2 * n_in + 1]
    acc = x_ref[...]
    for i in range(n_in):
        acc = acc + _dot(refs[i][...], refs[n_in + i][...])
    o_ref[...] = acc


def matmul_residual(x, a_list, w_list, *, tm=256, name="matmul_residual"):
    t, n = x.shape
    tm = min(tm, t)
    assert t % tm == 0
    n_in = len(a_list)
    in_specs = ([pl.BlockSpec((tm, a.shape[1]), lambda i: (i, 0)) for a in a_list]
                + [pl.BlockSpec(w.shape, lambda i: (0, 0)) for w in w_list]
                + [pl.BlockSpec((tm, n), lambda i: (i, 0))])
    return pl.pallas_call(
        functools.partial(_matmul_res_kernel, n_in=n_in),
        grid=(t // tm,),
        in_specs=in_specs,
        out_specs=pl.BlockSpec((tm, n), lambda i: (i, 0)),
        out_shape=jax.ShapeDtypeStruct((t, n), F32),
        compiler_params=_params("parallel"),
        name=name,
    )(*a_list, *w_list, x)


def _ffn_kernel(x_ref, g_ref, wg_ref, wu_ref, wd_ref, o_ref, xn_ref, *, nf):
    f = pl.program_id(1)

    @pl.when(f == 0)
    def _():
        xn_ref[...] = _rms(x_ref[...], g_ref[...]).astype(BF16)
        o_ref[...] = jnp.zeros_like(o_ref)

    xn = xn_ref[...]
    a = _dot(xn, wg_ref[...])
    b = _dot(xn, wu_ref[...])
    h = (a * (1.0 / (1.0 + jnp.exp(-a))) * b).astype(BF16)
    o_ref[...] += _dot(h, wd_ref[...])

    @pl.when(f == nf - 1)
    def _():
        o_ref[...] = x_ref[...] + 0.5 * o_ref[...]


def ffn(x, g, wg, wu, wd, *, tm=512, tf=512, name="ffn"):
    t, d = x.shape
    ff = wg.shape[1]
    tm = min(tm, t)
    assert t % tm == 0 and ff % tf == 0
    nf = ff // tf
    return pl.pallas_call(
        functools.partial(_ffn_kernel, nf=nf),
        grid=(t // tm, nf),
        in_specs=[pl.BlockSpec((tm, d), lambda i, f: (i, 0)),
                  pl.BlockSpec((1, d), lambda i, f: (0, 0)),
                  pl.BlockSpec((d, tf), lambda i, f: (0, f)),
                  pl.BlockSpec((d, tf), lambda i, f: (0, f)),
                  pl.BlockSpec((tf, d), lambda i, f: (f, 0))],
        out_specs=pl.BlockSpec((tm, d), lambda i, f: (i, 0)),
        out_shape=jax.ShapeDtypeStruct((t, d), F32),
        scratch_shapes=[pltpu.VMEM((tm, d), BF16)],
        compiler_params=_params("parallel", "arbitrary"),
        name=name,
    )(x, g.reshape(1, d).astype(F32), wg, wu, wd)


def _swa_kernel(sink_ref, q_ref, kc_ref, kp_ref, vc_ref, vp_ref, cc_ref, sc_ref, cp_ref, sp_ref,
                gq_ref, gk_ref, o_ref):
    n = pl.program_id(1)
    blk = SWA_BLOCK
    group = SWA_HEADS // SWA_KV_HEADS
    q = q_ref[0]
    k = jnp.concatenate([kp_ref[0], kc_ref[0]], axis=0)
    v = jnp.concatenate([vp_ref[0], vc_ref[0]], axis=0)
    cos_q, sin_q = cc_ref[0], sc_ref[0]
    cos_k = jnp.concatenate([cp_ref[0], cos_q], axis=0)
    sin_k = jnp.concatenate([sp_ref[0], sin_q], axis=0)

    def rope(x, g, cos, sin):
        w = x.shape[1]
        xg = x * _tile_lanes(g, w)
        return xg * _tile_lanes(cos, w) + _swap_halves(xg, HEAD_DIM // 2) * _tile_lanes(sin, w)

    qr = rope(q, gq_ref[...], cos_q, sin_q)
    kr = rope(k, gk_ref[...], cos_k, sin_k)

    def inv_rms(x, h):
        xh = x[:, h * HEAD_DIM:(h + 1) * HEAD_DIM]
        return lax.rsqrt(jnp.mean(xh * xh, axis=-1, keepdims=True) + EPS)

    row = lax.broadcasted_iota(jnp.int32, (group * blk, 2 * blk), 0)
    col = lax.broadcasted_iota(jnp.int32, (group * blk, 2 * blk), 1)
    rel = (row & (blk - 1)) + blk - col
    valid = (rel >= 0) & (rel < blk) & ((n > 0) | (col >= blk))
    row1 = lax.broadcasted_iota(jnp.int32, (group * blk, 1), 0)
    scale = HEAD_DIM ** -0.5
    outs = []
    for g in range(SWA_KV_HEADS):
        sl = slice(g * HEAD_DIM, (g + 1) * HEAD_DIM)
        kg = (kr[:, sl] * inv_rms(k, g)).astype(BF16)
        vg = v[:, sl].astype(BF16)
        heads = range(g * group, (g + 1) * group)
        qs = jnp.concatenate(
            [qr[:, h * HEAD_DIM:(h + 1) * HEAD_DIM] * (inv_rms(q, h) * scale) for h in heads], axis=0).astype(BF16)
        s = jnp.where(valid, _dot_t(qs, kg), NEG_INF)
        sink = jnp.zeros((group * blk, 1), F32)
        for i, h in enumerate(heads):
            sink = jnp.where((row1 >= i * blk) & (row1 < (i + 1) * blk), sink_ref[h], sink)
        m = jnp.maximum(jnp.max(s, axis=-1, keepdims=True), sink)
        p = jnp.exp(s - m)
        den = jnp.sum(p, axis=-1, keepdims=True) + jnp.exp(sink - m)
        o = _dot(p.astype(BF16), vg) / den
        outs.extend(o[i * blk:(i + 1) * blk] for i in range(group))
    o_ref[0] = jnp.concatenate(outs, axis=-1).astype(o_ref.dtype)


def swa_attention(u, cos, sin, gq, gk, sinks):
    b, s, _ = u.shape
    nb = s // SWA_BLOCK
    qw, kw = SWA_HEADS * HEAD_DIM, SWA_KV_HEADS * HEAD_DIM
    cur = lambda c: (lambda i, n: (i, n, c))
    prev = lambda c: (lambda i, n: (i, jnp.maximum(n - 1, 0), c))
    tab = pl.BlockSpec((1, SWA_BLOCK, LANES), cur(0))
    tab_prev = pl.BlockSpec((1, SWA_BLOCK, LANES), prev(0))
    gain = pl.BlockSpec((1, LANES), lambda i, n: (0, 0))
    return pl.pallas_call(
        _swa_kernel,
        grid=(b, nb),
        in_specs=[pl.BlockSpec(memory_space=pltpu.SMEM),
                  pl.BlockSpec((1, SWA_BLOCK, qw), cur(0)),
                  pl.BlockSpec((1, SWA_BLOCK, kw), cur(qw // kw)),
                  pl.BlockSpec((1, SWA_BLOCK, kw), prev(qw // kw)),
                  pl.BlockSpec((1, SWA_BLOCK, kw), cur(qw // kw + 1)),
                  pl.BlockSpec((1, SWA_BLOCK, kw), prev(qw // kw + 1)),
                  tab, tab, tab_prev, tab_prev, gain, gain],
        out_specs=pl.BlockSpec((1, SWA_BLOCK, qw), cur(0)),
        out_shape=jax.ShapeDtypeStruct((b, s, qw), BF16),
        compiler_params=_params("parallel", "arbitrary"),
        name="swa_attention",
    )(sinks.astype(F32), u, u, u, u, u, cos, sin, cos, sin, gq, gk)


def _rwkv_prep_kernel(r_ref, k_ref, v_ref, l_ref, rp_ref, kp_ref, vp_ref, lp_ref,
                      mur_ref, muk_ref, muv_ref, mul_ref, w0_ref, w2_ref, a0_ref, a2_ref, g2_ref,
                      kk_ref, ka_ref, e_ref,
                      ro_ref, ko_ref, vo_ref, kko_ref, bo_ref, lwo_ref, go_ref):
    t = pl.program_id(1)

    def shifted(cur_ref, prev_ref, mu_ref):
        x = cur_ref[0]
        last = prev_ref[0][7:8, :] * jnp.where(t > 0, 1.0, 0.0)
        row = lax.broadcasted_iota(jnp.int32, x.shape, 0)
        prev = jnp.where(row == 0, last, pltpu.roll(x, 1, 0))
        return x + (prev - x) * mu_ref[...]

    r = shifted(r_ref, rp_ref, mur_ref)
    k = shifted(k_ref, kp_ref, muk_ref)
    v = shifted(v_ref, vp_ref, muv_ref)
    lo = shifted(l_ref, lp_ref, mul_ref)
    w_lo, a_lo, g_lo = lo[:, 0:LANES], lo[:, LANES:2 * LANES], lo[:, 2 * LANES:4 * LANES]
    hp = functools.partial(jnp.dot, precision=HIGHEST, preferred_element_type=F32)
    z = -(w0_ref[...] + hp(jnp.tanh(w_lo), w2_ref[...]))
    softplus = jnp.maximum(z, 0.0) + jnp.log(1.0 + jnp.exp(-jnp.abs(z)))
    w = -softplus - 0.5
    a = 1.0 / (1.0 + jnp.exp(-(a0_ref[...] + hp(a_lo, a2_ref[...]))))
    g = hp(1.0 / (1.0 + jnp.exp(-g_lo)), g2_ref[...])
    kk = k * kk_ref[...]
    kk = kk / jnp.maximum(jnp.sqrt(_seg_sum(kk * kk, e_ref[...])), 1e-12)
    ro_ref[0] = r
    ko_ref[0] = k * (1.0 + (a - 1.0) * ka_ref[...])
    vo_ref[0] = v
    kko_ref[0] = kk
    bo_ref[0] = kk * a
    lwo_ref[0] = -jnp.exp(w)
    go_ref[0] = g


def rwkv_prep(u, mu, w0, w2, a0, a2, g2, k_k, k_a, *, ts=512, tc=512):
    b, s, _ = u.shape
    ts = min(ts, s)
    nct = RWKV_DIM // tc
    base = (SWA_HEADS + 2 * SWA_KV_HEADS) * HEAD_DIM // tc
    lora_blk = base + 3 * nct

    def cur(off):
        return pl.BlockSpec((1, ts, tc), lambda i, t, c: (i, t, off + c))

    def prev(off):
        return pl.BlockSpec((1, 8, tc), lambda i, t, c: (i, jnp.maximum(t * (ts // 8) - 1, 0), off + c))

    lora_cur = pl.BlockSpec((1, ts, tc), lambda i, t, c: (i, t, lora_blk))
    lora_prev = pl.BlockSpec((1, 8, tc), lambda i, t, c: (i, jnp.maximum(t * (ts // 8) - 1, 0), lora_blk))
    vec = pl.BlockSpec((1, tc), lambda i, t, c: (0, c))
    vec0 = pl.BlockSpec((1, tc), lambda i, t, c: (0, 0))
    lora_w = lambda rows: pl.BlockSpec((rows, tc), lambda i, t, c: (0, c))
    pad_rows = lambda m, rows: jnp.pad(m, ((0, rows - m.shape[0]), (0, 0))).astype(F32)
    pad_cols = lambda vv, cols: jnp.pad(vv, (0, cols - vv.shape[0]))
    c3 = 3 * RWKV_DIM
    mu_l = jnp.concatenate([pad_cols(mu[c3:c3 + DECAY_LORA], LANES),
                            pad_cols(mu[c3 + DECAY_LORA:c3 + DECAY_LORA + AAA_LORA], LANES),
                            pad_cols(mu[c3 + DECAY_LORA + AAA_LORA:], 2 * LANES)])
    row = lambda vv: vv.reshape(1, -1).astype(F32)
    seg = (jnp.arange(tc)[:, None] // HEAD_DIM == jnp.arange(tc)[None, :] // HEAD_DIM).astype(BF16)
    out = jax.ShapeDtypeStruct((b, s, RWKV_DIM), F32)
    out_spec = pl.BlockSpec((1, ts, tc), lambda i, t, c: (i, t, c))
    return pl.pallas_call(
        _rwkv_prep_kernel,
        grid=(b, s // ts, nct),
        in_specs=[cur(base), cur(base + nct), cur(base + 2 * nct), lora_cur,
                  prev(base), prev(base + nct), prev(base + 2 * nct), lora_prev,
                  vec, vec, vec, vec0, vec, lora_w(LANES), vec, lora_w(LANES), lora_w(2 * LANES),
                  vec, vec, pl.BlockSpec((tc, tc), lambda i, t, c: (0, 0))],
        out_specs=[out_spec] * 7,
        out_shape=[out] * 7,
        compiler_params=_params("parallel", "parallel", "arbitrary"),
        name="rwkv_prep",
    )(u, u, u, u, u, u, u, u,
      row(mu[:RWKV_DIM]), row(mu[RWKV_DIM:2 * RWKV_DIM]), row(mu[2 * RWKV_DIM:c3]), row(mu_l),
      row(w0), pad_rows(w2, LANES), row(a0), pad_rows(a2, LANES), pad_rows(g2, 2 * LANES),
      row(k_k), row(k_a), seg)


def _mm(a, b, dims, passes):
    dn = (dims, ((), ()))
    dg = lambda x, y: lax.dot_general(x, y, dn, preferred_element_type=F32)
    ah = a.astype(BF16)
    bh = b.astype(BF16)
    if passes == 1:
        return dg(ah, bh)
    al = (a - ah.astype(F32)).astype(BF16)
    bl = (b - bh.astype(F32)).astype(BF16)
    return dg(ah, bh) + dg(ah, bl) + dg(al, bh)


_NN = ((1,), (0,))
_NT = ((1,), (1,))
_TN = ((0,), (0,))
P_SC, P_INV, P_PQ, P_OUT, P_ST = 1, 1, 1, 1, 1
RWKV_UNROLL = 8
RWKV_SCAN_UNROLL = 4


def _rwkv_scan_kernel(r_ref, k_ref, v_ref, kk_ref, b_ref, lw_ref, g_ref, gng_ref, gnb_ref, rk_ref,
                      o_ref, st_ref, y1_ref, y0_ref, n_ref, z_ref, dec_ref, *, nchunk):
    c = RWKV_CHUNK
    lane_c = lax.broadcasted_iota(jnp.int32, (c, LANES), 1)
    head0 = lane_c < HEAD_DIM
    ri = lax.broadcasted_iota(jnp.int32, (2 * c, 2 * c), 0)
    ci = lax.broadcasted_iota(jnp.int32, (2 * c, 2 * c), 1)
    eye = jnp.where(ri == ci, 1.0, 0.0)
    tril_c = jnp.where(lax.broadcasted_iota(jnp.int32, (c, c), 0) >= lax.broadcasted_iota(jnp.int32, (c, c), 1),
                       1.0, 0.0).astype(BF16)
    stack = lambda x: jnp.concatenate([jnp.where(head0, x, 0.0), jnp.where(head0, 0.0, x)], axis=0)

    def seg_mean(x):
        m0 = jnp.sum(jnp.where(head0, x, 0.0), axis=-1, keepdims=True)
        m1 = jnp.sum(jnp.where(head0, 0.0, x), axis=-1, keepdims=True)
        return jnp.where(head0, m0, m1) * (1.0 / HEAD_DIM)

    def build(ics):
        each = lambda f, *cols: [f(*args) for args in zip(*cols)]
        sls = [pl.ds(pl.multiple_of(ic * c, c), c) for ic in ics]
        load = lambda ref: [ref[0, sl, :] for sl in sls]
        r, k, v, kk, b, lw = (load(ref) for ref in (r_ref, k_ref, v_ref, kk_ref, b_ref, lw_ref))

        def running_sum(x):
            l1 = x.astype(BF16)
            rest = x - l1.astype(F32)
            l2 = rest.astype(BF16)
            l3 = (rest - l2.astype(F32)).astype(BF16)
            return _dot(tril_c, l1) + _dot(tril_c, l2) + _dot(tril_c, l3)

        cum = each(running_sum, lw)
        cum_end = [x[c - 1:c, :] for x in cum]
        e_neg = each(lambda x: jnp.exp(-x), cum)
        e_end = each(lambda x, xe: jnp.exp(xe - x), cum, cum_end)
        a_s = each(lambda kk_, x, l: stack(-kk_ * jnp.exp(x - l)), kk, cum, lw)
        r_s = each(lambda r_, x: stack(r_ * jnp.exp(x)), r, cum)
        b_s = each(lambda b_, e: stack(b_ * e), b, e_neg)
        k_s = each(lambda k_, e: stack(k_ * e), k, e_neg)
        bh_s = each(lambda b_, e: stack(b_ * e), b, e_end)
        kh_s = each(lambda k_, e: stack(k_ * e), k, e_end)
        v_s = each(stack, v)
        strict = lambda x, y: jnp.where(ri > ci, _mm(x, y, _NT, P_SC), 0.0)
        incl = lambda x, y: jnp.where(ri >= ci, _mm(x, y, _NT, P_SC), 0.0)
        low = each(strict, a_s, b_s)
        a_ak = each(strict, a_s, k_s)
        a_rb = each(incl, r_s, b_s)
        a_rk = each(incl, r_s, k_s)
        inv = [eye + x for x in low]
        pw = low
        for _ in range(5):
            pw = each(lambda x: _mm(x, x, _NN, P_INV), pw)
            inv = each(lambda t, x: t + _mm(t, x, _NN, P_INV), inv, pw)
        akv = each(lambda x, y: _mm(x, y, _NN, P_PQ), a_ak, v_s)
        p = each(lambda t, x: _mm(t, x, _NN, P_PQ), inv, a_s)
        qv = each(lambda t, x: _mm(t, x, _NN, P_PQ), inv, akv)
        y1 = each(lambda rs, x, y: rs + _mm(x, y, _NN, P_OUT), r_s, a_rb, p)
        y0 = each(lambda x, y, z, w: _mm(x, y, _NN, P_OUT) + _mm(z, w, _NN, P_OUT), a_rb, qv, a_rk, v_s)
        nn = each(lambda x, y: _mm(x, y, _TN, P_OUT), p, bh_s)
        zz = each(lambda x, y, z, w: _mm(x, y, _TN, P_OUT) + _mm(z, w, _TN, P_OUT), qv, bh_s, v_s, kh_s)
        for i, ic in enumerate(ics):
            y1_ref[ic] = y1[i]
            y0_ref[ic] = y0[i]
            n_ref[ic] = nn[i]
            z_ref[ic] = zz[i]
            dec_ref[ic] = jnp.broadcast_to(jnp.exp(cum_end[i]), (8, LANES))

    @pl.loop(0, nchunk // RWKV_UNROLL)
    def _(io):
        build([io * RWKV_UNROLL + i for i in range(RWKV_UNROLL)])

    st_ref[...] = jnp.zeros_like(st_ref)

    def emit(ic, st):
        sl = pl.ds(pl.multiple_of(ic * c, c), c)
        r, k, v, g = r_ref[0, sl, :], k_ref[0, sl, :], v_ref[0, sl, :], g_ref[0, sl, :]
        y_st = _mm(y1_ref[ic], st, _NT, P_ST) + y0_ref[ic]
        y = y_st[0:c] + y_st[c:2 * c]
        mean = seg_mean(y)
        var = seg_mean((y - mean) * (y - mean))
        yn = (y - mean) * lax.rsqrt(var + RWKV_GN_EPS) * gng_ref[...] + gnb_ref[...]
        bonus = seg_mean(r * k * rk_ref[...]) * float(HEAD_DIM) * v
        o_ref[0, sl, :] = ((yn + bonus) * g).astype(o_ref.dtype)
        return st * dec_ref[ic][0:1, :] + _mm(st, n_ref[ic], _NN, P_ST) + z_ref[ic]

    @pl.loop(0, nchunk // RWKV_SCAN_UNROLL)
    def _(io):
        st = st_ref[...]
        for i in range(RWKV_SCAN_UNROLL):
            st = emit(io * RWKV_SCAN_UNROLL + i, st)
        st_ref[...] = st


def rwkv_scan(r, k, v, kk, b, lw, g, gn_g, gn_b, r_k):
    bsz, s, _ = r.shape
    npair = RWKV_DIM // LANES
    nchunk = s // RWKV_CHUNK
    assert nchunk % RWKV_UNROLL == 0 and nchunk % RWKV_SCAN_UNROLL == 0
    seq = pl.BlockSpec((1, s, LANES), lambda i, p: (i, 0, p))
    vec = pl.BlockSpec((1, LANES), lambda i, p: (0, p))
    row = lambda vv: vv.reshape(1, -1).astype(F32)
    mat = pltpu.VMEM((nchunk, LANES, LANES), F32)
    return pl.pallas_call(
        functools.partial(_rwkv_scan_kernel, nchunk=nchunk),
        grid=(bsz, npair),
        in_specs=[seq] * 7 + [vec] * 3,
        out_specs=seq,
        out_shape=jax.ShapeDtypeStruct((bsz, s, RWKV_DIM), BF16),
        scratch_shapes=[pltpu.VMEM((LANES, LANES), F32), mat, mat, mat, mat,
                        pltpu.VMEM((nchunk, 8, LANES), F32)],
        compiler_params=_params("parallel", "parallel"),
        name="rwkv_scan",
    )(r, k, v, kk, b, lw, g, row(gn_g), row(gn_b), row(r_k))


def _mla_prep_kernel(q_ref, kv_ref, pe_ref, cos_ref, sin_ref, e_ref, gq_ref, gkn_ref, gkp_ref, invn_ref,
                     qo_ref, ko_ref):
    cos, sin = cos_ref[0], sin_ref[0]
    half = MLA_ROPE // 2

    def rope(x, g):
        w = x.shape[1]
        xg = x * _tile_lanes(g, w)
        return xg * _tile_lanes(cos, w) + _swap_halves(xg, half) * _tile_lanes(sin, w)

    x = q_ref[0]
    w = x.shape[1]
    inv_n = _tile_lanes(invn_ref[...], w)
    inv = lax.rsqrt(_seg_sum(x * x, e_ref[...]) * inv_n + EPS)
    qo_ref[0] = (rope(x, gq_ref[...]) * inv * (MLA_NOPE + MLA_ROPE) ** -0.5).astype(qo_ref.dtype)
    kv = kv_ref[0]
    inv_k = lax.rsqrt(_seg_sum(kv * kv, e_ref[...]) * inv_n + EPS)
    k_nope = kv * inv_k * _tile_lanes(gkn_ref[...], w)
    pe = pe_ref[0]
    inv_pe = lax.rsqrt(jnp.sum(pe * pe, axis=-1, keepdims=True) * (1.0 / MLA_ROPE) + EPS)
    k_pe = rope(pe, gkp_ref[...]) * inv_pe
    ko_ref[0] = (k_nope + _tile_lanes(k_pe, w)).astype(ko_ref.dtype)


def mla_prep(q_lat, kv, u, cos, sin, gq, gkn, gkp, *, ts=512, tc=512):
    b, s, wtot = q_lat.shape
    ts = min(ts, s)
    lane = jnp.arange(tc)
    same = (lane[:, None] // LANES == lane[None, :] // LANES)
    pos = lane % LANES
    nope = pos < MLA_NOPE
    pe = (pos >= MLA_NOPE) & (pos < MLA_NOPE + MLA_ROPE)
    seg = (same & ((nope[:, None] & nope[None, :]) | (pe[:, None] & pe[None, :]))).astype(BF16)
    p1 = jnp.arange(LANES)
    inv_n = jnp.where(p1 < MLA_NOPE, 1.0 / MLA_NOPE, jnp.where(p1 < MLA_NOPE + MLA_ROPE, 1.0 / MLA_ROPE, 0.0))
    blk = pl.BlockSpec((1, ts, tc), lambda i, t, c: (i, t, c))
    tab = pl.BlockSpec((1, ts, LANES), lambda i, t, c: (i, t, 0))
    vec = pl.BlockSpec((1, LANES), lambda i, t, c: (0, 0))
    pe_blk = (MLA_Q_RANK + MLA_KV_RANK) // LANES
    out = jax.ShapeDtypeStruct((b, s, wtot), BF16)
    return pl.pallas_call(
        _mla_prep_kernel,
        grid=(b, s // ts, wtot // tc),
        in_specs=[blk, blk, pl.BlockSpec((1, ts, LANES), lambda i, t, c: (i, t, pe_blk)), tab, tab,
                  pl.BlockSpec((tc, tc), lambda i, t, c: (0, 0)), vec, vec, vec, vec],
        out_specs=[blk] * 2,
        out_shape=[out] * 2,
        compiler_params=_params("parallel", "parallel", "arbitrary"),
        name="mla_prep",
    )(q_lat, kv, u, cos, sin, seg, gq, gkn, gkp, inv_n.reshape(1, LANES).astype(F32))


def _diff_prep_kernel(q_ref, k_ref, cos_ref, sin_ref, e_ref, gq_ref, gk_ref, qo_ref, ko_ref):
    cos, sin = cos_ref[0], sin_ref[0]

    def prep(x, g, scale):
        w = x.shape[1]
        xg = x * _tile_lanes(g, w)
        xr = xg * _tile_lanes(cos, w) + _swap_halves(xg, DIFF_QK // 2) * _tile_lanes(sin, w)
        inv = lax.rsqrt(_seg_sum(x * x, e_ref[...]) * (1.0 / DIFF_QK) + EPS)
        return xr * inv * scale

    qo_ref[0] = prep(q_ref[0], gq_ref[...], DIFF_QK ** -0.5).astype(qo_ref.dtype)
    ko_ref[0] = prep(k_ref[0], gk_ref[...], 1.0).astype(ko_ref.dtype)


def diff_prep(u, cos, sin, gq, gk, *, ts=512, tc=512):
    b, s, _ = u.shape
    ts = min(ts, s)
    wtot = 2 * DIFF_HEADS * DIFF_QK
    q_base = (CD_IN_PAD - 3 * wtot) // tc
    seg = (jnp.arange(tc)[:, None] // DIFF_QK == jnp.arange(tc)[None, :] // DIFF_QK).astype(BF16)
    blk = lambda off: pl.BlockSpec((1, ts, tc), lambda i, t, c: (i, t, off + c))
    tab = pl.BlockSpec((1, ts, LANES), lambda i, t, c: (i, t, 0))
    vec = pl.BlockSpec((1, LANES), lambda i, t, c: (0, 0))
    out = jax.ShapeDtypeStruct((b, s, wtot), BF16)
    return pl.pallas_call(
        _diff_prep_kernel,
        grid=(b, s // ts, wtot // tc),
        in_specs=[blk(q_base), blk(q_base + wtot // tc), tab, tab,
                  pl.BlockSpec((tc, tc), lambda i, t, c: (0, 0)), vec, vec],
        out_specs=[blk(0)] * 2,
        out_shape=[out] * 2,
        compiler_params=_params("parallel", "parallel", "arbitrary"),
        name="diff_prep",
    )(u, u, cos, sin, seg, gq, gk)


def _causal_attn_kernel(lam_ref, q_ref, k_ref, vt_ref, g_ref, o_ref, *, n_sm, tq, out_scale):
    qi = pl.program_id(2)
    q = q_ref[0]
    hps = q.shape[1] // LANES
    slab = lambda x, h: x[:, h * LANES:(h + 1) * LANES]
    lane = lax.broadcasted_iota(jnp.int32, (tq, LANES), 1)
    qs, src = [], []
    for h in range(hps):
        qh = slab(q, h)
        if n_sm == 2:
            zero = jnp.zeros_like(qh)
            qs += [jnp.where(lane < DIFF_QK, qh, zero), jnp.where(lane < DIFF_QK, zero, qh)]
            src += [h, h]
        else:
            qs.append(qh)
            src.append(h)
    nch = len(qs)
    key_i = lax.broadcasted_iota(jnp.int32, (tq, tq), 0)
    qry_i = lax.broadcasted_iota(jnp.int32, (tq, tq), 1)

    def step(j, carry, diagonal):
        kj = k_ref[0, pl.ds(pl.multiple_of(j * tq, tq), tq), :]
        scores = lambda i: _dot_t(slab(kj, src[i]), qs[i])
        new = []
        ahead = [scores(i) for i in range(min(ATTN_LOOKAHEAD, nch))]
        for i in range(nch):
            s = ahead.pop(0)
            if i + ATTN_LOOKAHEAD < nch:
                ahead.append(scores(i + ATTN_LOOKAHEAD))
            if diagonal:
                s = jnp.where(key_i <= qry_i, s, NEG_INF)
            m, l, acc = carry[3 * i:3 * i + 3]
            m_new = jnp.maximum(m, jnp.max(s, axis=0, keepdims=True))
            alpha = jnp.exp(m - m_new)
            p = jnp.exp(s - m_new)
            new += [m_new, alpha * l + jnp.sum(p, axis=0, keepdims=True),
                    alpha * acc + _dot(vt_ref[0, src[i], j], p.astype(BF16))]
        return tuple(new)

    init = (jnp.full((1, tq), NEG_INF, F32), jnp.zeros((1, tq), F32),
            jnp.zeros((vt_ref.shape[3], tq), F32)) * nch
    carry = lax.fori_loop(0, qi, lambda j, cr: step(j, cr, False), init)
    carry = step(qi, carry, True)
    outs = []
    for h in range(hps):
        c0 = 3 * n_sm * h
        o = carry[c0 + 2] / carry[c0 + 1]
        if n_sm == 2:
            o = o - lam_ref[0] * (carry[c0 + 5] / carry[c0 + 4])
            o = o * lax.rsqrt(jnp.mean(o * o, axis=0, keepdims=True) + EPS) * g_ref[...] * out_scale
        outs.append(o.T.astype(o_ref.dtype))
    o_ref[0] = jnp.concatenate(outs, axis=-1)


def causal_attention(q, k, v, *, n_sm, lam=None, g=None, out_scale=1.0, tq=256, hps=4,
                     name="causal_attention"):
    b, s, wtot = q.shape
    tq = min(tq, s)
    width = hps * LANES
    groups = wtot // width
    ow = v.shape[3]
    lam = jnp.zeros((1,), F32) if lam is None else lam.reshape(1).astype(F32)
    g = jnp.ones((ow, 1), F32) if g is None else g.reshape(ow, 1).astype(F32)
    vt = v.reshape(b, s // tq, tq, wtot // LANES, ow).transpose(0, 3, 1, 4, 2)
    seq = pl.BlockSpec((1, s, width), lambda i, h, t: (i, 0, h))
    tile = pl.BlockSpec((1, tq, width), lambda i, h, t: (i, t, h))
    return pl.pallas_call(
        functools.partial(_causal_attn_kernel, n_sm=n_sm, tq=tq, out_scale=out_scale),
        grid=(b, groups, s // tq),
        in_specs=[pl.BlockSpec(memory_space=pltpu.SMEM), tile, seq,
                  pl.BlockSpec((1, hps, s // tq, ow, tq), lambda i, h, t: (i, h, 0, 0, 0)),
                  pl.BlockSpec((ow, 1), lambda i, h, t: (0, 0))],
        out_specs=pl.BlockSpec((1, tq, hps * ow), lambda i, h, t: (i, t, h)),
        out_shape=jax.ShapeDtypeStruct((b, s, groups * hps * ow), BF16),
        compiler_params=_params("parallel", "parallel", "arbitrary"),
        name=name,
    )(lam, q, k, vt, g)


def _mem_attn_kernel(q_ref, kv_ref, gq_ref, gk_ref, o_ref):
    q = q_ref[0]
    kv = kv_ref[0]
    outs = []
    for h in range(MEM_HEADS):
        sl = slice(h * MEM_HEAD_DIM, (h + 1) * MEM_HEAD_DIM)
        qh = (_rms(q[:, sl], gq_ref[...]) * MEM_HEAD_DIM ** -0.5).astype(BF16)
        kh = _rms(kv[:, sl], gk_ref[...]).astype(BF16)
        vh = kv[:, MEM_W + h * MEM_HEAD_DIM:MEM_W + (h + 1) * MEM_HEAD_DIM].astype(BF16)
        s = _dot_t(qh, kh)
        p = jnp.exp(s - jnp.max(s, axis=-1, keepdims=True))
        outs.append(_dot(p.astype(BF16), vh) / jnp.sum(p, axis=-1, keepdims=True))
    o_ref[0] = jnp.concatenate(outs, axis=-1).astype(o_ref.dtype)


def mem_attention(q, mem_kv, gq, gk, *, tq=512):
    b, s, _ = q.shape
    m = mem_kv.shape[1]
    tq = min(tq, s)
    vec = pl.BlockSpec((1, MEM_HEAD_DIM), lambda i, t: (0, 0))
    return pl.pallas_call(
        _mem_attn_kernel,
        grid=(b, s // tq),
        in_specs=[pl.BlockSpec((1, tq, MEM_W), lambda i, t: (i, t, 0)),
                  pl.BlockSpec((1, m, 2 * MEM_W), lambda i, t: (i, 0, 0)), vec, vec],
        out_specs=pl.BlockSpec((1, tq, MEM_W), lambda i, t: (i, t, 0)),
        out_shape=jax.ShapeDtypeStruct((b, s, MEM_W), BF16),
        compiler_params=_params("parallel", "arbitrary"),
        name="mem_attention",
    )(q, mem_kv, gq.reshape(1, -1).astype(F32), gk.reshape(1, -1).astype(F32))


def _rope_tables(positions, dim, lead_ones, tail):
    inv = 1.0 / (ROPE_THETA ** (jnp.arange(0, dim, 2, dtype=F32) / dim))
    ang = positions.astype(F32)[..., None] * inv
    c, s = jnp.cos(ang), jnp.sin(ang)
    shape = positions.shape
    cos = jnp.concatenate([jnp.ones(shape + (lead_ones,), F32), c, c, jnp.ones(shape + (tail,), F32)], axis=-1)
    sin = jnp.concatenate([jnp.zeros(shape + (lead_ones,), F32), -s, s, jnp.zeros(shape + (tail,), F32)], axis=-1)
    reps = LANES // cos.shape[-1]
    return jnp.tile(cos, (1, 1, reps)), jnp.tile(sin, (1, 1, reps))


def _pad_cols(w, cols):
    return jnp.pad(w, ((0, 0), (0, cols - w.shape[1])))


def _ab_in_layout(w):
    c = (SWA_HEADS + 2 * SWA_KV_HEADS) * HEAD_DIM + 3 * RWKV_DIM
    return jnp.concatenate([w[:, :c], _pad_cols(w[:, c:c + DECAY_LORA], LANES),
                            _pad_cols(w[:, c + DECAY_LORA:c + DECAY_LORA + AAA_LORA], LANES),
                            _pad_cols(w[:, c + DECAY_LORA + AAA_LORA:], 2 * LANES)], axis=1)


def _cd_in_layout(w):
    c1 = MLA_Q_RANK + MLA_KV_RANK
    z = lambda n: jnp.zeros((w.shape[0], n), w.dtype)
    return jnp.concatenate([w[:, :c1], z(MLA_NOPE), w[:, c1:c1 + MLA_ROPE], z(LANES - MLA_NOPE - MLA_ROPE),
                            z(LANES), w[:, c1 + MLA_ROPE:]], axis=1)


def _head_slabs(w, per_head):
    k = w.shape[0]
    return jnp.pad(w.reshape(k, -1, per_head), ((0, 0), (0, 0), (0, LANES - per_head))).reshape(k, -1)


def _slab_vec(*parts):
    v = jnp.concatenate([p.astype(F32) for p in parts])
    return jnp.pad(v, (0, LANES - v.shape[0])).reshape(1, LANES)


def kernel(x, mem, positions, ffn1_norm, ffn1_w_gate, ffn1_w_up, ffn1_w_down, mix_norm, ab_w_in, ab_w_out, swa_q_norm, swa_k_norm, swa_sinks, rwkv_mu, rwkv_w0, rwkv_w2, rwkv_a0, rwkv_a2, rwkv_g2, rwkv_k_k, rwkv_k_a, rwkv_r_k, rwkv_gn_g, rwkv_gn_b, cd_w_in, cd_w_out, mla_cq_norm, mla_ckv_norm, mla_w_uq, mla_w_ukv, mla_q_nope_norm, mla_k_nope_norm, mla_q_rope_norm, mla_k_rope_norm, diff_q_norm, diff_k_norm, diff_lq1, diff_lk1, diff_lq2, diff_lk2, diff_subln, memx_norm, memx_w_q, memx_q_norm, memx_w_o, mem_norm, mem_w_kv, mem_k_norm, ffn2_norm, ffn2_w_gate, ffn2_w_up, ffn2_w_down):
    b, s, d = x.shape
    m = mem.shape[1]
    t = b * s
    depth = ffn1_norm.shape[0]
    bf = lambda w: w.astype(BF16)
    cos64, sin64 = _rope_tables(positions, HEAD_DIM, 0, 0)
    cos32, sin32 = _rope_tables(positions, MLA_ROPE, MLA_NOPE, LANES - MLA_NOPE - MLA_ROPE)

    mem_kv = norm_matmul(mem.reshape(b * m, d), mem_norm, bf(mem_w_kv), name="mem_kv").reshape(b, m, 2 * MEM_W)

    x = x.reshape(t, d)
    for layer in range(depth):
        j = layer // 2
        x = ffn(x, ffn1_norm[layer], bf(ffn1_w_gate[layer]), bf(ffn1_w_up[layer]), bf(ffn1_w_down[layer]),
                name="ffn1")
        if layer % 2 == 0:
            u = norm_matmul(x, mix_norm[layer], bf(_ab_in_layout(ab_w_in[j])), name="ab_in")
            u = u.reshape(b, s, AB_IN_PAD)
            y_a = swa_attention(u, cos64, sin64, _slab_vec(swa_q_norm[j], swa_q_norm[j]),
                                _slab_vec(swa_k_norm[j], swa_k_norm[j]), swa_sinks[j])
            prep = rwkv_prep(u, rwkv_mu[j], rwkv_w0[j], rwkv_w2[j], rwkv_a0[j], rwkv_a2[j], rwkv_g2[j],
                             rwkv_k_k[j], rwkv_k_a[j])
            y_b = rwkv_scan(*prep, rwkv_gn_g[j], rwkv_gn_b[j], rwkv_r_k[j])
            w_out = bf(ab_w_out[j])
            half = SWA_HEADS * HEAD_DIM
            x = matmul_residual(x, [y_a.reshape(t, -1), y_b.reshape(t, -1)], [w_out[:half], w_out[half:]],
                                name="ab_out")
        else:
            u = norm_matmul(x, mix_norm[layer], bf(_cd_in_layout(cd_w_in[j])), name="cd_in")
            q_lat = norm_matmul(u, mla_cq_norm[j], bf(_head_slabs(mla_w_uq[j], MLA_NOPE + MLA_ROPE)),
                                k_blk=0, name="mla_uq")
            kv = norm_matmul(u, mla_ckv_norm[j], bf(mla_w_ukv[j]), k_blk=MLA_Q_RANK // MLA_KV_RANK, name="mla_ukv")
            u = u.reshape(b, s, CD_IN_PAD)
            zero64 = jnp.zeros((MLA_NOPE,), F32)
            q_c, k_c = mla_prep(q_lat.reshape(b, s, -1), kv.reshape(b, s, -1), u, cos32, sin32,
                                _slab_vec(mla_q_nope_norm[j], mla_q_rope_norm[j]),
                                _slab_vec(mla_k_nope_norm[j]),
                                _slab_vec(zero64, mla_k_rope_norm[j]))
            v_c = kv.reshape(b, s, MLA_HEADS, MLA_NOPE + MLA_V)[..., MLA_NOPE:].astype(BF16)
            y_c = causal_attention(q_c, k_c, v_c, n_sm=1, hps=8, name="mla_attention")
            q_d, k_d = diff_prep(u, cos64, sin64, _slab_vec(diff_q_norm[j], diff_q_norm[j]),
                                 _slab_vec(diff_k_norm[j], diff_k_norm[j]))
            lambda_init = 0.8 - 0.6 * math.exp(-0.3 * layer)
            lam = (jnp.exp(jnp.sum(diff_lq1[j].astype(F32) * diff_lk1[j].astype(F32)))
                   - jnp.exp(jnp.sum(diff_lq2[j].astype(F32) * diff_lk2[j].astype(F32))) + lambda_init)
            wd = 2 * DIFF_HEADS * DIFF_QK
            v_d = u[:, :, CD_IN_PAD - wd:].astype(BF16).reshape(b, s, DIFF_HEADS, DIFF_V)
            y_d = causal_attention(q_d, k_d, v_d, n_sm=2, hps=4, lam=lam, g=diff_subln[j],
                                   out_scale=1.0 - lambda_init, name="diff_attention")
            w_out = bf(cd_w_out[j])
            n_c = MLA_HEADS * MLA_V
            x = matmul_residual(x, [y_c.reshape(t, -1), y_d.reshape(t, -1)], [w_out[:n_c], w_out[n_c:]],
                                name="cd_out")
        q_m = norm_matmul(x, memx_norm[layer], bf(memx_w_q[layer]), name="memx_q").reshape(b, s, MEM_W)
        o_m = mem_attention(q_m, mem_kv, memx_q_norm[layer], mem_k_norm)
        x = matmul_residual(x, [o_m.reshape(t, MEM_W)], [bf(memx_w_o[layer])], name="memx_out")
        x = ffn(x, ffn2_norm[layer], bf(ffn2_w_gate[layer]), bf(ffn2_w_up[layer]), bf(ffn2_w_down[layer]),
                name="ffn2")
    return x.reshape(b, s, d)
```

```python
import functools
import math

import jax
import jax.numpy as jnp
from jax import lax
from jax.experimental import pallas as pl
from jax.experimental.pallas import tpu as pltpu

F32 = jnp.float32
BF16 = jnp.bfloat16
HIGHEST = lax.Precision.HIGHEST

EPS = 1e-6
ROPE_THETA = 10000.0
NEG_INF = -1e30
ATTN_LOOKAHEAD = 4
LANES = 128

D_MODEL = 2048
D_FF = 5632
HEAD_DIM = 64
SWA_HEADS = 16
SWA_KV_HEADS = 4
SWA_BLOCK = 128
RWKV_DIM = 1024
RWKV_CHUNK = 64
RWKV_GN_EPS = 64e-5
DECAY_LORA, AAA_LORA, GATE_LORA = 64, 64, 160
MLA_HEADS, MLA_Q_RANK, MLA_KV_RANK, MLA_NOPE, MLA_ROPE, MLA_V = 16, 512, 256, 64, 32, 64
DIFF_HEADS, DIFF_QK, DIFF_V = 8, 64, 128
MEM_HEADS, MEM_HEAD_DIM = 4, 128
MEM_W = MEM_HEADS * MEM_HEAD_DIM
AB_IN_PAD = 5120
CD_IN_PAD = 4096

VMEM_LIMIT = 48 * 1024 * 1024
FFN_VMEM_LIMIT = 56 * 1024 * 1024


def _params(*sem):
    return pltpu.CompilerParams(dimension_semantics=sem, vmem_limit_bytes=VMEM_LIMIT)


def _dot(a, b):
    return jnp.dot(a, b, preferred_element_type=F32)


def _dot_t(a, b):
    return lax.dot_general(a, b, (((1,), (1,)), ((), ())), preferred_element_type=F32)


def _dot_0(a, b):
    return lax.dot_general(a, b, (((0,), (0,)), ((), ())), preferred_element_type=F32)


def _rms(x, g):
    return x * lax.rsqrt(jnp.mean(x * x, axis=-1, keepdims=True) + EPS) * g


def _seg_sum(x, e):
    hi = x.astype(BF16)
    lo = (x - hi.astype(F32)).astype(BF16)
    return _dot(hi, e) + _dot(lo, e)


def _tile_lanes(v, width):
    return v if v.shape[-1] == width else jnp.tile(v, (1, width // v.shape[-1]))


def _swap_halves(x, half):
    w = x.shape[-1]
    lane = lax.broadcasted_iota(jnp.int32, x.shape, x.ndim - 1)
    low = (lane & (2 * half - 1)) < half
    return jnp.where(low, pltpu.roll(x, w - half, x.ndim - 1), pltpu.roll(x, half, x.ndim - 1))


def _norm_matmul_kernel(x_ref, g_ref, w_ref, o_ref, xn_ref):
    @pl.when(pl.program_id(1) == 0)
    def _():
        xn_ref[...] = _rms(x_ref[...].astype(F32), g_ref[...]).astype(BF16)

    o_ref[...] = _dot(xn_ref[...], w_ref[...]).astype(o_ref.dtype)


def norm_matmul(x, g, w, *, k_blk=0, out_dtype=F32, tm=1024, tn=512, name="norm_matmul"):
    t = x.shape[0]
    k, n = w.shape
    tm, tn = min(tm, t), min(tn, n)
    assert t % tm == 0 and n % tn == 0
    return pl.pallas_call(
        _norm_matmul_kernel,
        grid=(t // tm, n // tn),
        in_specs=[pl.BlockSpec((tm, k), lambda i, j: (i, k_blk)),
                  pl.BlockSpec((1, k), lambda i, j: (0, 0)),
                  pl.BlockSpec((k, tn), lambda i, j: (0, j))],
        out_specs=pl.BlockSpec((tm, tn), lambda i, j: (i, j)),
        out_shape=jax.ShapeDtypeStruct((t, n), out_dtype),
        scratch_shapes=[pltpu.VMEM((tm, k), BF16)],
        compiler_params=_params("parallel", "arbitrary"),
        name=name,
    )(x, g.reshape(1, k).astype(F32), w)


def _matmul_res_kernel(*refs, n_in):
    x_ref, o_ref = refs[2 * n_in], refs[2 * n_in + 1]
    acc = x_ref[...]
    for i in range(n_in):
        acc = acc + _dot(refs[i][...], refs[n_in + i][...])
    o_ref[...] = acc


def matmul_residual(x, a_list, w_list, *, tm=256, name="matmul_residual"):
    t, n = x.shape
    tm = min(tm, t)
    assert t % tm == 0
    n_in = len(a_list)
    in_specs = ([pl.BlockSpec((tm, a.shape[1]), lambda i: (i, 0)) for a in a_list]
                + [pl.BlockSpec(w.shape, lambda i: (0, 0)) for w in w_list]
                + [pl.BlockSpec((tm, n), lambda i: (i, 0))])
    return pl.pallas_call(
        functools.partial(_matmul_res_kernel, n_in=n_in),
        grid=(t // tm,),
        in_specs=in_specs,
        out_specs=pl.BlockSpec((tm, n), lambda i: (i, 0)),
        out_shape=jax.ShapeDtypeStruct((t, n), F32),
        compiler_params=_params("parallel"),
        name=name,
    )(*a_list, *w_list, x)


def _ffn_kernel(x_ref, g_ref, wg_ref, wu_ref, wd_ref, o_ref, xn_ref, *, nf):
    f = pl.program_id(1)

    @pl.when(f == 0)
    def _():
        x = x_ref[...]
        xn_ref[...] = _rms(x, g_ref[...]).astype(BF16)
        o_ref[...] = 2.0 * x

    xn = xn_ref[...]
    a = _dot(xn, wg_ref[...].astype(BF16))
    b = _dot(xn, wu_ref[...].astype(BF16))
    h = (a * (1.0 / (1.0 + jnp.exp(-a))) * b).astype(BF16)
    o_ref[...] += _dot(h, wd_ref[...].astype(BF16))

    @pl.when(f == nf - 1)
    def _():
        o_ref[...] = 0.5 * o_ref[...]


def ffn(x, g, wg, wu, wd, layer, *, tm=1024, tf=256, name="ffn"):
    t, d = x.shape
    ff = wg.shape[2]
    tm = min(tm, t)
    assert t % tm == 0 and ff % tf == 0
    nf = ff // tf
    return pl.pallas_call(
        functools.partial(_ffn_kernel, nf=nf),
        grid=(t // tm, nf),
        in_specs=[pl.BlockSpec((tm, d), lambda i, f: (i, 0), pipeline_mode=pl.Buffered(1)),
                  pl.BlockSpec((1, d), lambda i, f: (0, 0)),
                  pl.BlockSpec((None, d, tf), lambda i, f: (layer, 0, f)),
                  pl.BlockSpec((None, d, tf), lambda i, f: (layer, 0, f)),
                  pl.BlockSpec((None, tf, d), lambda i, f: (layer, f, 0))],
        out_specs=pl.BlockSpec((tm, d), lambda i, f: (i, 0)),
        out_shape=jax.ShapeDtypeStruct((t, d), F32),
        scratch_shapes=[pltpu.VMEM((tm, d), BF16)],
        compiler_params=pltpu.CompilerParams(dimension_semantics=("parallel", "arbitrary"),
                                             vmem_limit_bytes=FFN_VMEM_LIMIT),
        name=name,
    )(x, g.reshape(1, d).astype(F32), wg, wu, wd)


def _swa_kernel(sink_ref, q_ref, kc_ref, kp_ref, vc_ref, vp_ref, cc_ref, sc_ref, cp_ref, sp_ref,
                gq_ref, gk_ref, o_ref):
    n = pl.program_id(1)
    blk = SWA_BLOCK
    group = SWA_HEADS // SWA_KV_HEADS
    q = q_ref[0]
    k = jnp.concatenate([kp_ref[0], kc_ref[0]], axis=0)
    v = jnp.concatenate([vp_ref[0], vc_ref[0]], axis=0)
    cos_q, sin_q = cc_ref[0], sc_ref[0]
    cos_k = jnp.concatenate([cp_ref[0], cos_q], axis=0)
    sin_k = jnp.concatenate([sp_ref[0], sin_q], axis=0)

    def rope(x, g, cos, sin):
        w = x.shape[1]
        xg = x * _tile_lanes(g, w)
        return xg * _tile_lanes(cos, w) + _swap_halves(xg, HEAD_DIM // 2) * _tile_lanes(sin, w)

    qr = rope(q, gq_ref[...], cos_q, sin_q)
    kr = rope(k, gk_ref[...], cos_k, sin_k)

    def inv_rms(x, h):
        xh = x[:, h * HEAD_DIM:(h + 1) * HEAD_DIM]
        return lax.rsqrt(jnp.mean(xh * xh, axis=-1, keepdims=True) + EPS)

    row = lax.broadcasted_iota(jnp.int32, (group * blk, 2 * blk), 0)
    col = lax.broadcasted_iota(jnp.int32, (group * blk, 2 * blk), 1)
    rel = (row & (blk - 1)) + blk - col
    valid = (rel >= 0) & (rel < blk) & ((n > 0) | (col >= blk))
    row1 = lax.broadcasted_iota(jnp.int32, (group * blk, 1), 0)
    scale = HEAD_DIM ** -0.5
    outs = []
    for g in range(SWA_KV_HEADS):
        sl = slice(g * HEAD_DIM, (g + 1) * HEAD_DIM)
        kg = (kr[:, sl] * inv_rms(k, g)).astype(BF16)
        vg = v[:, sl].astype(BF16)
        heads = range(g * group, (g + 1) * group)
        qs = jnp.concatenate(
            [qr[:, h * HEAD_DIM:(h + 1) * HEAD_DIM] * (inv_rms(q, h) * scale) for h in heads], axis=0).astype(BF16)
        s = jnp.where(valid, _dot_t(qs, kg), NEG_INF)
        sink = jnp.zeros((group * blk, 1), F32)
        for i, h in enumerate(heads):
            sink = jnp.where((row1 >= i * blk) & (row1 < (i + 1) * blk), sink_ref[h], sink)
        m = jnp.maximum(jnp.max(s, axis=-1, keepdims=True), sink)
        p = jnp.exp(s - m)
        den = jnp.sum(p, axis=-1, keepdims=True) + jnp.exp(sink - m)
        o = _dot(p.astype(BF16), vg) / den
        outs.extend(o[i * blk:(i + 1) * blk] for i in range(group))
    o_ref[0] = jnp.concatenate(outs, axis=-1).astype(o_ref.dtype)


def swa_attention(u, cos, sin, gq, gk, sinks):
    b, s, _ = u.shape
    nb = s // SWA_BLOCK
    qw, kw = SWA_HEADS * HEAD_DIM, SWA_KV_HEADS * HEAD_DIM
    cur = lambda c: (lambda i, n: (i, n, c))
    prev = lambda c: (lambda i, n: (i, jnp.maximum(n - 1, 0), c))
    tab = pl.BlockSpec((1, SWA_BLOCK, LANES), cur(0))
    tab_prev = pl.BlockSpec((1, SWA_BLOCK, LANES), prev(0))
    gain = pl.BlockSpec((1, LANES), lambda i, n: (0, 0))
    return pl.pallas_call(
        _swa_kernel,
        grid=(b, nb),
        in_specs=[pl.BlockSpec(memory_space=pltpu.SMEM),
                  pl.BlockSpec((1, SWA_BLOCK, qw), cur(0)),
                  pl.BlockSpec((1, SWA_BLOCK, kw), cur(qw // kw)),
                  pl.BlockSpec((1, SWA_BLOCK, kw), prev(qw // kw)),
                  pl.BlockSpec((1, SWA_BLOCK, kw), cur(qw // kw + 1)),
                  pl.BlockSpec((1, SWA_BLOCK, kw), prev(qw // kw + 1)),
                  tab, tab, tab_prev, tab_prev, gain, gain],
        out_specs=pl.BlockSpec((1, SWA_BLOCK, qw), cur(0)),
        out_shape=jax.ShapeDtypeStruct((b, s, qw), BF16),
        compiler_params=_params("parallel", "arbitrary"),
        name="swa_attention",
    )(sinks.astype(F32), u, u, u, u, u, cos, sin, cos, sin, gq, gk)


def _rwkv_prep_kernel(r_ref, k_ref, v_ref, l_ref, rp_ref, kp_ref, vp_ref, lp_ref,
                      mur_ref, muk_ref, muv_ref, mul_ref, w0_ref, w2_ref, a0_ref, a2_ref, g2_ref,
                      kk_ref, ka_ref, e_ref,
                      ro_ref, ko_ref, vo_ref, kko_ref, bo_ref, lwo_ref, go_ref):
    t = pl.program_id(1)

    def shifted(cur_ref, prev_ref, mu_ref):
        x = cur_ref[0]
        last = prev_ref[0][7:8, :] * jnp.where(t > 0, 1.0, 0.0)
        row = lax.broadcasted_iota(jnp.int32, x.shape, 0)
        prev = jnp.where(row == 0, last, pltpu.roll(x, 1, 0))
        return x + (prev - x) * mu_ref[...]

    r = shifted(r_ref, rp_ref, mur_ref)
    k = shifted(k_ref, kp_ref, muk_ref)
    v = shifted(v_ref, vp_ref, muv_ref)
    lo = shifted(l_ref, lp_ref, mul_ref)
    w_lo, a_lo, g_lo = lo[:, 0:LANES], lo[:, LANES:2 * LANES], lo[:, 2 * LANES:4 * LANES]
    hp = functools.partial(jnp.dot, precision=HIGHEST, preferred_element_type=F32)
    z = -(w0_ref[...] + hp(jnp.tanh(w_lo), w2_ref[...]))
    softplus = jnp.maximum(z, 0.0) + jnp.log(1.0 + jnp.exp(-jnp.abs(z)))
    w = -softplus - 0.5
    a = 1.0 / (1.0 + jnp.exp(-(a0_ref[...] + hp(a_lo, a2_ref[...]))))
    g = hp(1.0 / (1.0 + jnp.exp(-g_lo)), g2_ref[...])
    kk = k * kk_ref[...]
    kk = kk / jnp.maximum(jnp.sqrt(_seg_sum(kk * kk, e_ref[...])), 1e-12)
    ro_ref[0] = r
    ko_ref[0] = k * (1.0 + (a - 1.0) * ka_ref[...])
    vo_ref[0] = v
    kko_ref[0] = kk
    bo_ref[0] = kk * a
    lwo_ref[0] = -jnp.exp(w)
    go_ref[0] = g


def rwkv_prep(u, mu, w0, w2, a0, a2, g2, k_k, k_a, *, ts=512, tc=512):
    b, s, _ = u.shape
    ts = min(ts, s)
    nct = RWKV_DIM // tc
    base = (SWA_HEADS + 2 * SWA_KV_HEADS) * HEAD_DIM // tc
    lora_blk = base + 3 * nct

    def cur(off):
        return pl.BlockSpec((1, ts, tc), lambda i, t, c: (i, t, off + c))

    def prev(off):
        return pl.BlockSpec((1, 8, tc), lambda i, t, c: (i, jnp.maximum(t * (ts // 8) - 1, 0), off + c))

    lora_cur = pl.BlockSpec((1, ts, tc), lambda i, t, c: (i, t, lora_blk))
    lora_prev = pl.BlockSpec((1, 8, tc), lambda i, t, c: (i, jnp.maximum(t * (ts // 8) - 1, 0), lora_blk))
    vec = pl.BlockSpec((1, tc), lambda i, t, c: (0, c))
    vec0 = pl.BlockSpec((1, tc), lambda i, t, c: (0, 0))
    lora_w = lambda rows: pl.BlockSpec((rows, tc), lambda i, t, c: (0, c))
    pad_rows = lambda m, rows: jnp.pad(m, ((0, rows - m.shape[0]), (0, 0))).astype(F32)
    pad_cols = lambda vv, cols: jnp.pad(vv, (0, cols - vv.shape[0]))
    c3 = 3 * RWKV_DIM
    mu_l = jnp.concatenate([pad_cols(mu[c3:c3 + DECAY_LORA], LANES),
                            pad_cols(mu[c3 + DECAY_LORA:c3 + DECAY_LORA + AAA_LORA], LANES),
                            pad_cols(mu[c3 + DECAY_LORA + AAA_LORA:], 2 * LANES)])
    row = lambda vv: vv.reshape(1, -1).astype(F32)
    seg = (jnp.arange(tc)[:, None] // HEAD_DIM == jnp.arange(tc)[None, :] // HEAD_DIM).astype(BF16)
    out = jax.ShapeDtypeStruct((b, s, RWKV_DIM), F32)
    out_spec = pl.BlockSpec((1, ts, tc), lambda i, t, c: (i, t, c))
    return pl.pallas_call(
        _rwkv_prep_kernel,
        grid=(b, s // ts, nct),
        in_specs=[cur(base), cur(base + nct), cur(base + 2 * nct), lora_cur,
                  prev(base), prev(base + nct), prev(base + 2 * nct), lora_prev,
                  vec, vec, vec, vec0, vec, lora_w(LANES), vec, lora_w(LANES), lora_w(2 * LANES),
                  vec, vec, pl.BlockSpec((tc, tc), lambda i, t, c: (0, 0))],
        out_specs=[out_spec] * 7,
        out_shape=[out] * 7,
        compiler_params=_params("parallel", "parallel", "arbitrary"),
        name="rwkv_prep",
    )(u, u, u, u, u, u, u, u,
      row(mu[:RWKV_DIM]), row(mu[RWKV_DIM:2 * RWKV_DIM]), row(mu[2 * RWKV_DIM:c3]), row(mu_l),
      row(w0), pad_rows(w2, LANES), row(a0), pad_rows(a2, LANES), pad_rows(g2, 2 * LANES),
      row(k_k), row(k_a), seg)


def _mm(a, b, dims, passes):
    dn = (dims, ((), ()))
    dg = lambda x, y: lax.dot_general(x, y, dn, preferred_element_type=F32)
    ah = a.astype(BF16)
    bh = b.astype(BF16)
    if passes == 1:
        return dg(ah, bh)
    al = (a - ah.astype(F32)).astype(BF16)
    bl = (b - bh.astype(F32)).astype(BF16)
    return dg(ah, bh) + dg(ah, bl) + dg(al, bh)


_NN = ((1,), (0,))
_NT = ((1,), (1,))
_TN = ((0,), (0,))
P_SC, P_INV, P_PQ, P_OUT, P_ST = 1, 1, 1, 1, 1
RWKV_UNROLL = 8
RWKV_SCAN_UNROLL = 4


def _rwkv_scan_kernel(r_ref, k_ref, v_ref, kk_ref, b_ref, lw_ref, g_ref, gng_ref, gnb_ref, rk_ref,
                      o_ref, st_ref, y1_ref, y0_ref, n_ref, z_ref, dec_ref, *, nchunk):
    c = RWKV_CHUNK
    lane_c = lax.broadcasted_iota(jnp.int32, (c, LANES), 1)
    head0 = lane_c < HEAD_DIM
    ri = lax.broadcasted_iota(jnp.int32, (2 * c, 2 * c), 0)
    ci = lax.broadcasted_iota(jnp.int32, (2 * c, 2 * c), 1)
    eye = jnp.where(ri == ci, 1.0, 0.0)
    tril_c = jnp.where(lax.broadcasted_iota(jnp.int32, (c, c), 0) >= lax.broadcasted_iota(jnp.int32, (c, c), 1),
                       1.0, 0.0).astype(BF16)
    stack = lambda x: jnp.concatenate([jnp.where(head0, x, 0.0), jnp.where(head0, 0.0, x)], axis=0)

    def seg_mean(x):
        m0 = jnp.sum(jnp.where(head0, x, 0.0), axis=-1, keepdims=True)
        m1 = jnp.sum(jnp.where(head0, 0.0, x), axis=-1, keepdims=True)
        return jnp.where(head0, m0, m1) * (1.0 / HEAD_DIM)

    def build(ics):
        each = lambda f, *cols: [f(*args) for args in zip(*cols)]
        sls = [pl.ds(pl.multiple_of(ic * c, c), c) for ic in ics]
        load = lambda ref: [ref[0, sl, :] for sl in sls]
        r, k, v, kk, b, lw = (load(ref) for ref in (r_ref, k_ref, v_ref, kk_ref, b_ref, lw_ref))

        def running_sum(x):
            l1 = x.astype(BF16)
            rest = x - l1.astype(F32)
            l2 = rest.astype(BF16)
            l3 = (rest - l2.astype(F32)).astype(BF16)
            return _dot(tril_c, l1) + _dot(tril_c, l2) + _dot(tril_c, l3)

        cum = each(running_sum, lw)
        cum_end = [x[c - 1:c, :] for x in cum]
        e_neg = each(lambda x: jnp.exp(-x), cum)
        e_end = each(lambda x, xe: jnp.exp(xe - x), cum, cum_end)
        a_s = each(lambda kk_, x, l: stack(-kk_ * jnp.exp(x - l)), kk, cum, lw)
        r_s = each(lambda r_, x: stack(r_ * jnp.exp(x)), r, cum)
        b_s = each(lambda b_, e: stack(b_ * e), b, e_neg)
        k_s = each(lambda k_, e: stack(k_ * e), k, e_neg)
        bh_s = each(lambda b_, e: stack(b_ * e), b, e_end)
        kh_s = each(lambda k_, e: stack(k_ * e), k, e_end)
        v_s = each(stack, v)
        strict = lambda x, y: jnp.where(ri > ci, _mm(x, y, _NT, P_SC), 0.0)
        incl = lambda x, y: jnp.where(ri >= ci, _mm(x, y, _NT, P_SC), 0.0)
        low = each(strict, a_s, b_s)
        a_ak = each(strict, a_s, k_s)
        a_rb = each(incl, r_s, b_s)
        a_rk = each(incl, r_s, k_s)
        inv = [eye + x for x in low]
        pw = low
        for _ in range(5):
            pw = each(lambda x: _mm(x, x, _NN, P_INV), pw)
            inv = each(lambda t, x: t + _mm(t, x, _NN, P_INV), inv, pw)
        akv = each(lambda x, y: _mm(x, y, _NN, P_PQ), a_ak, v_s)
        p = each(lambda t, x: _mm(t, x, _NN, P_PQ), inv, a_s)
        qv = each(lambda t, x: _mm(t, x, _NN, P_PQ), inv, akv)
        y1 = each(lambda rs, x, y: rs + _mm(x, y, _NN, P_OUT), r_s, a_rb, p)
        y0 = each(lambda x, y, z, w: _mm(x, y, _NN, P_OUT) + _mm(z, w, _NN, P_OUT), a_rb, qv, a_rk, v_s)
        nn = each(lambda x, y: _mm(x, y, _TN, P_OUT), p, bh_s)
        zz = each(lambda x, y, z, w: _mm(x, y, _TN, P_OUT) + _mm(z, w, _TN, P_OUT), qv, bh_s, v_s, kh_s)
        for i, ic in enumerate(ics):
            y1_ref[ic] = y1[i]
            y0_ref[ic] = y0[i]
            n_ref[ic] = nn[i]
            z_ref[ic] = zz[i]
            dec_ref[ic] = jnp.broadcast_to(jnp.exp(cum_end[i]), (8, LANES))

    @pl.loop(0, nchunk // RWKV_UNROLL)
    def _(io):
        build([io * RWKV_UNROLL + i for i in range(RWKV_UNROLL)])

    st_ref[...] = jnp.zeros_like(st_ref)

    def emit(ic, st):
        sl = pl.ds(pl.multiple_of(ic * c, c), c)
        r, k, v, g = r_ref[0, sl, :], k_ref[0, sl, :], v_ref[0, sl, :], g_ref[0, sl, :]
        y_st = _mm(y1_ref[ic], st, _NT, P_ST) + y0_ref[ic]
        y = y_st[0:c] + y_st[c:2 * c]
        mean = seg_mean(y)
        var = seg_mean((y - mean) * (y - mean))
        yn = (y - mean) * lax.rsqrt(var + RWKV_GN_EPS) * gng_ref[...] + gnb_ref[...]
        bonus = seg_mean(r * k * rk_ref[...]) * float(HEAD_DIM) * v
        o_ref[0, sl, :] = ((yn + bonus) * g).astype(o_ref.dtype)
        return st * dec_ref[ic][0:1, :] + _mm(st, n_ref[ic], _NN, P_ST) + z_ref[ic]

    @pl.loop(0, nchunk // RWKV_SCAN_UNROLL)
    def _(io):
        st = st_ref[...]
        for i in range(RWKV_SCAN_UNROLL):
            st = emit(io * RWKV_SCAN_UNROLL + i, st)
        st_ref[...] = st


def rwkv_scan(r, k, v, kk, b, lw, g, gn_g, gn_b, r_k):
    bsz, s, _ = r.shape
    npair = RWKV_DIM // LANES
    nchunk = s // RWKV_CHUNK
    assert nchunk % RWKV_UNROLL == 0 and nchunk % RWKV_SCAN_UNROLL == 0
    seq = pl.BlockSpec((1, s, LANES), lambda i, p: (i, 0, p))
    vec = pl.BlockSpec((1, LANES), lambda i, p: (0, p))
    row = lambda vv: vv.reshape(1, -1).astype(F32)
    mat = pltpu.VMEM((nchunk, LANES, LANES), F32)
    return pl.pallas_call(
        functools.partial(_rwkv_scan_kernel, nchunk=nchunk),
        grid=(bsz, npair),
        in_specs=[seq] * 7 + [vec] * 3,
        out_specs=seq,
        out_shape=jax.ShapeDtypeStruct((bsz, s, RWKV_DIM), BF16),
        scratch_shapes=[pltpu.VMEM((LANES, LANES), F32), mat, mat, mat, mat,
                        pltpu.VMEM((nchunk, 8, LANES), F32)],
        compiler_params=_params("parallel", "parallel"),
        name="rwkv_scan",
    )(r, k, v, kk, b, lw, g, row(gn_g), row(gn_b), row(r_k))


def _mla_prep_kernel(q_ref, kv_ref, pe_ref, cos_ref, sin_ref, e_ref, gq_ref, gkn_ref, gkp_ref, invn_ref,
                     qo_ref, ko_ref, vo_ref):
    cos, sin = cos_ref[0], sin_ref[0]
    half = MLA_ROPE // 2

    def rope(x, g):
        w = x.shape[1]
        xg = x * _tile_lanes(g, w)
        return xg * _tile_lanes(cos, w) + _swap_halves(xg, half) * _tile_lanes(sin, w)

    x = q_ref[0]
    w = x.shape[1]
    inv_n = _tile_lanes(invn_ref[...], w)
    inv = lax.rsqrt(_seg_sum(x * x, e_ref[...]) * inv_n + EPS)
    qo_ref[0] = (rope(x, gq_ref[...]) * inv * (MLA_NOPE + MLA_ROPE) ** -0.5).astype(qo_ref.dtype)
    kv = kv_ref[0]
    inv_k = lax.rsqrt(_seg_sum(kv * kv, e_ref[...]) * inv_n + EPS)
    k_nope = kv * inv_k * _tile_lanes(gkn_ref[...], w)
    pe = pe_ref[0]
    inv_pe = lax.rsqrt(jnp.sum(pe * pe, axis=-1, keepdims=True) * (1.0 / MLA_ROPE) + EPS)
    k_pe = rope(pe, gkp_ref[...]) * inv_pe
    ko_ref[0] = (k_nope + _tile_lanes(k_pe, w)).astype(ko_ref.dtype)
    vo_ref[0] = jnp.concatenate([kv[:, h * LANES + MLA_NOPE:(h + 1) * LANES] for h in range(w // LANES)],
                                axis=-1).astype(vo_ref.dtype)


def mla_prep(q_lat, kv, u, cos, sin, gq, gkn, gkp, *, ts=512, tc=512):
    b, s, wtot = q_lat.shape
    ts = min(ts, s)
    lane = jnp.arange(tc)
    same = (lane[:, None] // LANES == lane[None, :] // LANES)
    pos = lane % LANES
    nope = pos < MLA_NOPE
    pe = (pos >= MLA_NOPE) & (pos < MLA_NOPE + MLA_ROPE)
    seg = (same & ((nope[:, None] & nope[None, :]) | (pe[:, None] & pe[None, :]))).astype(BF16)
    p1 = jnp.arange(LANES)
    inv_n = jnp.where(p1 < MLA_NOPE, 1.0 / MLA_NOPE, jnp.where(p1 < MLA_NOPE + MLA_ROPE, 1.0 / MLA_ROPE, 0.0))
    blk = pl.BlockSpec((1, ts, tc), lambda i, t, c: (i, t, c))
    tab = pl.BlockSpec((1, ts, LANES), lambda i, t, c: (i, t, 0))
    vec = pl.BlockSpec((1, LANES), lambda i, t, c: (0, 0))
    pe_blk = (MLA_Q_RANK + MLA_KV_RANK) // LANES
    out = jax.ShapeDtypeStruct((b, s, wtot), BF16)
    return pl.pallas_call(
        _mla_prep_kernel,
        grid=(b, s // ts, wtot // tc),
        in_specs=[blk, blk, pl.BlockSpec((1, ts, LANES), lambda i, t, c: (i, t, pe_blk)), tab, tab,
                  pl.BlockSpec((tc, tc), lambda i, t, c: (0, 0)), vec, vec, vec, vec],
        out_specs=[blk, blk, pl.BlockSpec((1, ts, tc // LANES * MLA_V), lambda i, t, c: (i, t, c))],
        out_shape=[out, out, jax.ShapeDtypeStruct((b, s, wtot // LANES * MLA_V), BF16)],
        compiler_params=_params("parallel", "parallel", "arbitrary"),
        name="mla_prep",
    )(q_lat, kv, u, cos, sin, seg, gq, gkn, gkp, inv_n.reshape(1, LANES).astype(F32))


def _diff_prep_kernel(q_ref, k_ref, cos_ref, sin_ref, e_ref, gq_ref, gk_ref, qo_ref, ko_ref):
    cos, sin = cos_ref[0], sin_ref[0]

    def prep(x, g, scale):
        w = x.shape[1]
        xg = x * _tile_lanes(g, w)
        xr = xg * _tile_lanes(cos, w) + _swap_halves(xg, DIFF_QK // 2) * _tile_lanes(sin, w)
        inv = lax.rsqrt(_seg_sum(x * x, e_ref[...]) * (1.0 / DIFF_QK) + EPS)
        return xr * inv * scale

    qo_ref[0] = prep(q_ref[0], gq_ref[...], DIFF_QK ** -0.5).astype(qo_ref.dtype)
    ko_ref[0] = prep(k_ref[0], gk_ref[...], 1.0).astype(ko_ref.dtype)


def diff_prep(u, cos, sin, gq, gk, *, ts=512, tc=512):
    b, s, _ = u.shape
    ts = min(ts, s)
    wtot = 2 * DIFF_HEADS * DIFF_QK
    q_base = (CD_IN_PAD - 3 * wtot) // tc
    seg = (jnp.arange(tc)[:, None] // DIFF_QK == jnp.arange(tc)[None, :] // DIFF_QK).astype(BF16)
    blk = lambda off: pl.BlockSpec((1, ts, tc), lambda i, t, c: (i, t, off + c))
    tab = pl.BlockSpec((1, ts, LANES), lambda i, t, c: (i, t, 0))
    vec = pl.BlockSpec((1, LANES), lambda i, t, c: (0, 0))
    out = jax.ShapeDtypeStruct((b, s, wtot), BF16)
    return pl.pallas_call(
        _diff_prep_kernel,
        grid=(b, s // ts, wtot // tc),
        in_specs=[blk(q_base), blk(q_base + wtot // tc), tab, tab,
                  pl.BlockSpec((tc, tc), lambda i, t, c: (0, 0)), vec, vec],
        out_specs=[blk(0)] * 2,
        out_shape=[out] * 2,
        compiler_params=_params("parallel", "parallel", "arbitrary"),
        name="diff_prep",
    )(u, u, cos, sin, seg, gq, gk)


def _causal_attn_kernel(lam_ref, q_ref, k_ref, vt_ref, g_ref, o_ref, *, n_sm, tq, out_scale):
    qi = pl.program_id(2)
    q = q_ref[0]
    hps = q.shape[1] // LANES
    slab = lambda x, h: x[:, h * LANES:(h + 1) * LANES]
    lane = lax.broadcasted_iota(jnp.int32, (tq, LANES), 1)
    qs, src = [], []
    for h in range(hps):
        qh = slab(q, h)
        if n_sm == 2:
            zero = jnp.zeros_like(qh)
            qs += [jnp.where(lane < DIFF_QK, qh, zero), jnp.where(lane < DIFF_QK, zero, qh)]
            src += [h, h]
        else:
            qs.append(qh)
            src.append(h)
    nch = len(qs)
    key_i = lax.broadcasted_iota(jnp.int32, (tq, tq), 0)
    qry_i = lax.broadcasted_iota(jnp.int32, (tq, tq), 1)

    def step(j, carry, diagonal):
        kj = k_ref[0, pl.ds(pl.multiple_of(j * tq, tq), tq), :]
        scores = lambda i: _dot_t(slab(kj, src[i]), qs[i])
        new = []
        ahead = [scores(i) for i in range(min(ATTN_LOOKAHEAD, nch))]
        for i in range(nch):
            s = ahead.pop(0)
            if i + ATTN_LOOKAHEAD < nch:
                ahead.append(scores(i + ATTN_LOOKAHEAD))
            if diagonal:
                s = jnp.where(key_i <= qry_i, s, NEG_INF)
            m, l, acc = carry[3 * i:3 * i + 3]
            m_new = jnp.maximum(m, jnp.max(s, axis=0, keepdims=True))
            alpha = jnp.exp(m - m_new)
            p = jnp.exp(s - m_new)
            new += [m_new, alpha * l + jnp.sum(p, axis=0, keepdims=True),
                    alpha * acc + _dot(vt_ref[0, src[i], j], p.astype(BF16))]
        return tuple(new)

    init = (jnp.full((1, tq), NEG_INF, F32), jnp.zeros((1, tq), F32),
            jnp.zeros((vt_ref.shape[3], tq), F32)) * nch
    carry = lax.fori_loop(0, qi, lambda j, cr: step(j, cr, False), init)
    carry = step(qi, carry, True)
    outs = []
    for h in range(hps):
        c0 = 3 * n_sm * h
        o = carry[c0 + 2] / carry[c0 + 1]
        if n_sm == 2:
            o = o - lam_ref[0] * (carry[c0 + 5] / carry[c0 + 4])
            o = o * lax.rsqrt(jnp.mean(o * o, axis=0, keepdims=True) + EPS) * g_ref[...] * out_scale
        outs.append(o.T.astype(o_ref.dtype))
    o_ref[0] = jnp.concatenate(outs, axis=-1)


def causal_attention(q, k, v, *, n_sm, lam=None, g=None, out_scale=1.0, tq=256, hps=4,
                     name="causal_attention"):
    b, s, wtot = q.shape
    tq = min(tq, s)
    width = hps * LANES
    groups = wtot // width
    ow = v.shape[3]
    lam = jnp.zeros((1,), F32) if lam is None else lam.reshape(1).astype(F32)
    g = jnp.ones((ow, 1), F32) if g is None else g.reshape(ow, 1).astype(F32)
    vt = v.reshape(b, s // tq, tq, wtot // LANES, ow).transpose(0, 3, 1, 4, 2)
    seq = pl.BlockSpec((1, s, width), lambda i, h, t: (i, 0, h))
    tile = pl.BlockSpec((1, tq, width), lambda i, h, t: (i, t, h))
    return pl.pallas_call(
        functools.partial(_causal_attn_kernel, n_sm=n_sm, tq=tq, out_scale=out_scale),
        grid=(b, groups, s // tq),
        in_specs=[pl.BlockSpec(memory_space=pltpu.SMEM), tile, seq,
                  pl.BlockSpec((1, hps, s // tq, ow, tq), lambda i, h, t: (i, h, 0, 0, 0)),
                  pl.BlockSpec((ow, 1), lambda i, h, t: (0, 0))],
        out_specs=pl.BlockSpec((1, tq, hps * ow), lambda i, h, t: (i, t, h)),
        out_shape=jax.ShapeDtypeStruct((b, s, groups * hps * ow), BF16),
        compiler_params=_params("parallel", "parallel", "arbitrary"),
        name=name,
    )(lam, q, k, vt, g)


def _mem_attn_kernel(q_ref, kv_ref, gq_ref, gk_ref, o_ref):
    q = q_ref[0]
    kv = kv_ref[0]
    outs = []
    for h in range(MEM_HEADS):
        sl = slice(h * MEM_HEAD_DIM, (h + 1) * MEM_HEAD_DIM)
        qh = (_rms(q[:, sl], gq_ref[...]) * MEM_HEAD_DIM ** -0.5).astype(BF16)
        kh = _rms(kv[:, sl], gk_ref[...]).astype(BF16)
        vh = kv[:, MEM_W + h * MEM_HEAD_DIM:MEM_W + (h + 1) * MEM_HEAD_DIM].astype(BF16)
        s = _dot_t(qh, kh)
        p = jnp.exp(s - jnp.max(s, axis=-1, keepdims=True))
        outs.append(_dot(p.astype(BF16), vh) / jnp.sum(p, axis=-1, keepdims=True))
    o_ref[0] = jnp.concatenate(outs, axis=-1).astype(o_ref.dtype)


def mem_attention(q, mem_kv, gq, gk, *, tq=512):
    b, s, _ = q.shape
    m = mem_kv.shape[1]
    tq = min(tq, s)
    vec = pl.BlockSpec((1, MEM_HEAD_DIM), lambda i, t: (0, 0))
    return pl.pallas_call(
        _mem_attn_kernel,
        grid=(b, s // tq),
        in_specs=[pl.BlockSpec((1, tq, MEM_W), lambda i, t: (i, t, 0)),
                  pl.BlockSpec((1, m, 2 * MEM_W), lambda i, t: (i, 0, 0)), vec, vec],
        out_specs=pl.BlockSpec((1, tq, MEM_W), lambda i, t: (i, t, 0)),
        out_shape=jax.ShapeDtypeStruct((b, s, MEM_W), BF16),
        compiler_params=_params("parallel", "arbitrary"),
        name="mem_attention",
    )(q, mem_kv, gq.reshape(1, -1).astype(F32), gk.reshape(1, -1).astype(F32))


def _rope_tables(positions, dim, lead_ones, tail):
    inv = 1.0 / (ROPE_THETA ** (jnp.arange(0, dim, 2, dtype=F32) / dim))
    ang = positions.astype(F32)[..., None] * inv
    c, s = jnp.cos(ang), jnp.sin(ang)
    shape = positions.shape
    cos = jnp.concatenate([jnp.ones(shape + (lead_ones,), F32), c, c, jnp.ones(shape + (tail,), F32)], axis=-1)
    sin = jnp.concatenate([jnp.zeros(shape + (lead_ones,), F32), -s, s, jnp.zeros(shape + (tail,), F32)], axis=-1)
    reps = LANES // cos.shape[-1]
    return jnp.tile(cos, (1, 1, reps)), jnp.tile(sin, (1, 1, reps))


def _pad_cols(w, cols):
    return jnp.pad(w, ((0, 0), (0, cols - w.shape[1])))


def _ab_in_layout(w):
    c = (SWA_HEADS + 2 * SWA_KV_HEADS) * HEAD_DIM + 3 * RWKV_DIM
    return jnp.concatenate([w[:, :c], _pad_cols(w[:, c:c + DECAY_LORA], LANES),
                            _pad_cols(w[:, c + DECAY_LORA:c + DECAY_LORA + AAA_LORA], LANES),
                            _pad_cols(w[:, c + DECAY_LORA + AAA_LORA:], 2 * LANES)], axis=1)


def _cd_in_layout(w):
    c1 = MLA_Q_RANK + MLA_KV_RANK
    z = lambda n: jnp.zeros((w.shape[0], n), w.dtype)
    return jnp.concatenate([w[:, :c1], z(MLA_NOPE), w[:, c1:c1 + MLA_ROPE], z(LANES - MLA_NOPE - MLA_ROPE),
                            z(LANES), w[:, c1 + MLA_ROPE:]], axis=1)


def _head_slabs(w, per_head):
    k = w.shape[0]
    return jnp.pad(w.reshape(k, -1, per_head), ((0, 0), (0, 0), (0, LANES - per_head))).reshape(k, -1)


def _slab_vec(*parts):
    v = jnp.concatenate([p.astype(F32) for p in parts])
    return jnp.pad(v, (0, LANES - v.shape[0])).reshape(1, LANES)


def kernel(x, mem, positions, ffn1_norm, ffn1_w_gate, ffn1_w_up, ffn1_w_down, mix_norm, ab_w_in, ab_w_out, swa_q_norm, swa_k_norm, swa_sinks, rwkv_mu, rwkv_w0, rwkv_w2, rwkv_a0, rwkv_a2, rwkv_g2, rwkv_k_k, rwkv_k_a, rwkv_r_k, rwkv_gn_g, rwkv_gn_b, cd_w_in, cd_w_out, mla_cq_norm, mla_ckv_norm, mla_w_uq, mla_w_ukv, mla_q_nope_norm, mla_k_nope_norm, mla_q_rope_norm, mla_k_rope_norm, diff_q_norm, diff_k_norm, diff_lq1, diff_lk1, diff_lq2, diff_lk2, diff_subln, memx_norm, memx_w_q, memx_q_norm, memx_w_o, mem_norm, mem_w_kv, mem_k_norm, ffn2_norm, ffn2_w_gate, ffn2_w_up, ffn2_w_down):
    b, s, d = x.shape
    m = mem.shape[1]
    t = b * s
    depth = ffn1_norm.shape[0]
    bf = lambda w: w.astype(BF16)
    cos64, sin64 = _rope_tables(positions, HEAD_DIM, 0, 0)
    cos32, sin32 = _rope_tables(positions, MLA_ROPE, MLA_NOPE, LANES - MLA_NOPE - MLA_ROPE)

    mem_kv = norm_matmul(mem.reshape(b * m, d), mem_norm, bf(mem_w_kv), name="mem_kv").reshape(b, m, 2 * MEM_W)

    x = x.reshape(t, d)
    for layer in range(depth):
        j = layer // 2
        x = ffn(x, ffn1_norm[layer], ffn1_w_gate, ffn1_w_up, ffn1_w_down, layer, name="ffn1")
        if layer % 2 == 0:
            u = norm_matmul(x, mix_norm[layer], bf(_ab_in_layout(ab_w_in[j])), name="ab_in")
            u = u.reshape(b, s, AB_IN_PAD)
            y_a = swa_attention(u, cos64, sin64, _slab_vec(swa_q_norm[j], swa_q_norm[j]),
                                _slab_vec(swa_k_norm[j], swa_k_norm[j]), swa_sinks[j])
            prep = rwkv_prep(u, rwkv_mu[j], rwkv_w0[j], rwkv_w2[j], rwkv_a0[j], rwkv_a2[j], rwkv_g2[j],
                             rwkv_k_k[j], rwkv_k_a[j])
            y_b = rwkv_scan(*prep, rwkv_gn_g[j], rwkv_gn_b[j], rwkv_r_k[j])
            w_out = bf(ab_w_out[j])
            half = SWA_HEADS * HEAD_DIM
            x = matmul_residual(x, [y_a.reshape(t, -1), y_b.reshape(t, -1)], [w_out[:half], w_out[half:]],
                                name="ab_out")
        else:
            u = norm_matmul(x, mix_norm[layer], bf(_cd_in_layout(cd_w_in[j])), name="cd_in")
            q_lat = norm_matmul(u, mla_cq_norm[j], bf(_head_slabs(mla_w_uq[j], MLA_NOPE + MLA_ROPE)),
                                k_blk=0, name="mla_uq")
            kv = norm_matmul(u, mla_ckv_norm[j], bf(mla_w_ukv[j]), k_blk=MLA_Q_RANK // MLA_KV_RANK, name="mla_ukv")
            u = u.reshape(b, s, CD_IN_PAD)
            zero64 = jnp.zeros((MLA_NOPE,), F32)
            q_c, k_c, v_c = mla_prep(q_lat.reshape(b, s, -1), kv.reshape(b, s, -1), u, cos32, sin32,
                                     _slab_vec(mla_q_nope_norm[j], mla_q_rope_norm[j]),
                                     _slab_vec(mla_k_nope_norm[j]),
                                     _slab_vec(zero64, mla_k_rope_norm[j]))
            y_c = causal_attention(q_c, k_c, v_c.reshape(b, s, MLA_HEADS, MLA_V), n_sm=1, hps=8,
                                   name="mla_attention")
            q_d, k_d = diff_prep(u, cos64, sin64, _slab_vec(diff_q_norm[j], diff_q_norm[j]),
                                 _slab_vec(diff_k_norm[j], diff_k_norm[j]))
            lambda_init = 0.8 - 0.6 * math.exp(-0.3 * layer)
            lam = (jnp.exp(jnp.sum(diff_lq1[j].astype(F32) * diff_lk1[j].astype(F32)))
                   - jnp.exp(jnp.sum(diff_lq2[j].astype(F32) * diff_lk2[j].astype(F32))) + lambda_init)
            wd = 2 * DIFF_HEADS * DIFF_QK
            v_d = u[:, :, CD_IN_PAD - wd:].astype(BF16).reshape(b, s, DIFF_HEADS, DIFF_V)
            y_d = causal_attention(q_d, k_d, v_d, n_sm=2, hps=4, lam=lam, g=diff_subln[j],
                                   out_scale=1.0 - lambda_init, name="diff_attention")
            w_out = bf(cd_w_out[j])
            n_c = MLA_HEADS * MLA_V
            x = matmul_residual(x, [y_c.reshape(t, -1), y_d.reshape(t, -1)], [w_out[:n_c], w_out[n_c:]],
                                name="cd_out")
        q_m = norm_matmul(x, memx_norm[layer], bf(memx_w_q[layer]), name="memx_q").reshape(b, s, MEM_W)
        o_m = mem_attention(q_m, mem_kv, memx_q_norm[layer], mem_k_norm)
        x = matmul_residual(x, [o_m.reshape(t, MEM_W)], [bf(memx_w_o[layer])], name="memx_out")
        x = ffn(x, ffn2_norm[layer], ffn2_w_gate, ffn2_w_up, ffn2_w_down, layer, name="ffn2")
    return x.reshape(b, s, d)
```

```python
import functools
import math

import jax
import jax.numpy as jnp
from jax import lax
from jax.experimental import pallas as pl
from jax.experimental.pallas import tpu as pltpu

F32 = jnp.float32
BF16 = jnp.bfloat16
HIGHEST = lax.Precision.HIGHEST

EPS = 1e-6
ROPE_THETA = 10000.0
NEG_INF = -1e30
ATTN_LOOKAHEAD = 4
ATTN_TILE = 256
LANES = 128

D_MODEL = 2048
D_FF = 5632
HEAD_DIM = 64
SWA_HEADS = 16
SWA_KV_HEADS = 4
SWA_BLOCK = 128
RWKV_DIM = 1024
RWKV_CHUNK = 64
RWKV_GN_EPS = 64e-5
DECAY_LORA, AAA_LORA, GATE_LORA = 64, 64, 160
MLA_HEADS, MLA_Q_RANK, MLA_KV_RANK, MLA_NOPE, MLA_ROPE, MLA_V = 16, 512, 256, 64, 32, 64
DIFF_HEADS, DIFF_QK, DIFF_V = 8, 64, 128
MEM_HEADS, MEM_HEAD_DIM = 4, 128
MEM_W = MEM_HEADS * MEM_HEAD_DIM
AB_IN_PAD = 5120
CD_IN_PAD = 4096

VMEM_LIMIT = 48 * 1024 * 1024
FFN_VMEM_LIMIT = 56 * 1024 * 1024


def _params(*sem):
    return pltpu.CompilerParams(dimension_semantics=sem, vmem_limit_bytes=VMEM_LIMIT)


def _dot(a, b):
    return jnp.dot(a, b, preferred_element_type=F32)


def _dot_t(a, b):
    return lax.dot_general(a, b, (((1,), (1,)), ((), ())), preferred_element_type=F32)


def _dot_0(a, b):
    return lax.dot_general(a, b, (((0,), (0,)), ((), ())), preferred_element_type=F32)


def _rms(x, g):
    return x * lax.rsqrt(jnp.mean(x * x, axis=-1, keepdims=True) + EPS) * g


def _seg_sum(x, e):
    hi = x.astype(BF16)
    lo = (x - hi.astype(F32)).astype(BF16)
    return _dot(hi, e) + _dot(lo, e)


def _tile_lanes(v, width):
    return v if v.shape[-1] == width else jnp.tile(v, (1, width // v.shape[-1]))


def _store_transposed_tiles(vt_ref, v):
    for r in range(vt_ref.shape[1]):
        vt_ref[0, r] = v[r * ATTN_TILE:(r + 1) * ATTN_TILE, :].T.astype(vt_ref.dtype)


def _swap_halves(x, half):
    w = x.shape[-1]
    lane = lax.broadcasted_iota(jnp.int32, x.shape, x.ndim - 1)
    low = (lane & (2 * half - 1)) < half
    return jnp.where(low, pltpu.roll(x, w - half, x.ndim - 1), pltpu.roll(x, half, x.ndim - 1))


def _norm_matmul_kernel(x_ref, g_ref, w_ref, o_ref, xn_ref):
    @pl.when(pl.program_id(1) == 0)
    def _():
        xn_ref[...] = _rms(x_ref[...].astype(F32), g_ref[...]).astype(BF16)

    o_ref[...] = _dot(xn_ref[...], w_ref[...]).astype(o_ref.dtype)


def norm_matmul(x, g, w, *, k_blk=0, out_dtype=F32, tm=1024, tn=512, name="norm_matmul"):
    t = x.shape[0]
    k, n = w.shape
    tm, tn = min(tm, t), min(tn, n)
    assert t % tm == 0 and n % tn == 0
    return pl.pallas_call(
        _norm_matmul_kernel,
        grid=(t // tm, n // tn),
        in_specs=[pl.BlockSpec((tm, k), lambda i, j: (i, k_blk)),
                  pl.BlockSpec((1, k), lambda i, j: (0, 0)),
                  pl.BlockSpec((k, tn), lambda i, j: (0, j))],
        out_specs=pl.BlockSpec((tm, tn), lambda i, j: (i, j)),
        out_shape=jax.ShapeDtypeStruct((t, n), out_dtype),
        scratch_shapes=[pltpu.VMEM((tm, k), BF16)],
        compiler_params=_params("parallel", "arbitrary"),
        name=name,
    )(x, g.reshape(1, k).astype(F32), w)


def _matmul_res_kernel(*refs, n_in):
    x_ref, o_ref = refs[2 * n_in], refs[2 * n_in + 1]
    acc = x_ref[...]
    for i in range(n_in):
        acc = acc + _dot(refs[i][...], refs[n_in + i][...])
    o_ref[...] = acc


def matmul_residual(x, a_list, w_list, *, tm=256, name="matmul_residual"):
    t, n = x.shape
    tm = min(tm, t)
    assert t % tm == 0
    n_in = len(a_list)
    in_specs = ([pl.BlockSpec((tm, a.shape[1]), lambda i: (i, 0)) for a in a_list]
                + [pl.BlockSpec(w.shape, lambda i: (0, 0)) for w in w_list]
                + [pl.BlockSpec((tm, n), lambda i: (i, 0))])
    return pl.pallas_call(
        functools.partial(_matmul_res_kernel, n_in=n_in),
        grid=(t // tm,),
        in_specs=in_specs,
        out_specs=pl.BlockSpec((tm, n), lambda i: (i, 0)),
        out_shape=jax.ShapeDtypeStruct((t, n), F32),
        compiler_params=_params("parallel"),
        name=name,
    )(*a_list, *w_list, x)


def _ffn_kernel(x_ref, g_ref, wg_ref, wu_ref, wd_ref, o_ref, xn_ref, *, nf):
    f = pl.program_id(1)

    @pl.when(f == 0)
    def _():
        x = x_ref[...]
        xn_ref[...] = _rms(x, g_ref[...]).astype(BF16)
        o_ref[...] = 2.0 * x

    xn = xn_ref[...]
    a = _dot(xn, wg_ref[...].astype(BF16))
    b = _dot(xn, wu_ref[...].astype(BF16))
    h = (a * (1.0 / (1.0 + jnp.exp(-a))) * b).astype(BF16)
    o_ref[...] += _dot(h, wd_ref[...].astype(BF16))

    @pl.when(f == nf - 1)
    def _():
        o_ref[...] = 0.5 * o_ref[...]


def ffn(x, g, wg, wu, wd, layer, *, tm=1024, tf=256, name="ffn"):
    t, d = x.shape
    ff = wg.shape[2]
    tm = min(tm, t)
    assert t % tm == 0 and ff % tf == 0
    nf = ff // tf
    return pl.pallas_call(
        functools.partial(_ffn_kernel, nf=nf),
        grid=(t // tm, nf),
        in_specs=[pl.BlockSpec((tm, d), lambda i, f: (i, 0), pipeline_mode=pl.Buffered(1)),
                  pl.BlockSpec((1, d), lambda i, f: (0, 0)),
                  pl.BlockSpec((None, d, tf), lambda i, f: (layer, 0, f)),
                  pl.BlockSpec((None, d, tf), lambda i, f: (layer, 0, f)),
                  pl.BlockSpec((None, tf, d), lambda i, f: (layer, f, 0))],
        out_specs=pl.BlockSpec((tm, d), lambda i, f: (i, 0)),
        out_shape=jax.ShapeDtypeStruct((t, d), F32),
        scratch_shapes=[pltpu.VMEM((tm, d), BF16)],
        compiler_params=pltpu.CompilerParams(dimension_semantics=("parallel", "arbitrary"),
                                             vmem_limit_bytes=FFN_VMEM_LIMIT),
        name=name,
    )(x, g.reshape(1, d).astype(F32), wg, wu, wd)


def _swa_kernel(sink_ref, q_ref, kc_ref, kp_ref, vc_ref, vp_ref, cc_ref, sc_ref, cp_ref, sp_ref,
                gq_ref, gk_ref, o_ref):
    n = pl.program_id(1)
    blk = SWA_BLOCK
    group = SWA_HEADS // SWA_KV_HEADS
    q = q_ref[0]
    k = jnp.concatenate([kp_ref[0], kc_ref[0]], axis=0)
    v = jnp.concatenate([vp_ref[0], vc_ref[0]], axis=0)
    cos_q, sin_q = cc_ref[0], sc_ref[0]
    cos_k = jnp.concatenate([cp_ref[0], cos_q], axis=0)
    sin_k = jnp.concatenate([sp_ref[0], sin_q], axis=0)

    def rope(x, g, cos, sin):
        w = x.shape[1]
        xg = x * _tile_lanes(g, w)
        return xg * _tile_lanes(cos, w) + _swap_halves(xg, HEAD_DIM // 2) * _tile_lanes(sin, w)

    qr = rope(q, gq_ref[...], cos_q, sin_q)
    kr = rope(k, gk_ref[...], cos_k, sin_k)

    def inv_rms(x, h):
        xh = x[:, h * HEAD_DIM:(h + 1) * HEAD_DIM]
        return lax.rsqrt(jnp.mean(xh * xh, axis=-1, keepdims=True) + EPS)

    row = lax.broadcasted_iota(jnp.int32, (group * blk, 2 * blk), 0)
    col = lax.broadcasted_iota(jnp.int32, (group * blk, 2 * blk), 1)
    rel = (row & (blk - 1)) + blk - col
    valid = (rel >= 0) & (rel < blk) & ((n > 0) | (col >= blk))
    row1 = lax.broadcasted_iota(jnp.int32, (group * blk, 1), 0)
    scale = HEAD_DIM ** -0.5
    outs = []
    for g in range(SWA_KV_HEADS):
        sl = slice(g * HEAD_DIM, (g + 1) * HEAD_DIM)
        kg = (kr[:, sl] * inv_rms(k, g)).astype(BF16)
        vg = v[:, sl].astype(BF16)
        heads = range(g * group, (g + 1) * group)
        qs = jnp.concatenate(
            [qr[:, h * HEAD_DIM:(h + 1) * HEAD_DIM] * (inv_rms(q, h) * scale) for h in heads], axis=0).astype(BF16)
        s = jnp.where(valid, _dot_t(qs, kg), NEG_INF)
        sink = jnp.zeros((group * blk, 1), F32)
        for i, h in enumerate(heads):
            sink = jnp.where((row1 >= i * blk) & (row1 < (i + 1) * blk), sink_ref[h], sink)
        m = jnp.maximum(jnp.max(s, axis=-1, keepdims=True), sink)
        p = jnp.exp(s - m)
        den = jnp.sum(p, axis=-1, keepdims=True) + jnp.exp(sink - m)
        o = _dot(p.astype(BF16), vg) / den
        outs.extend(o[i * blk:(i + 1) * blk] for i in range(group))
    o_ref[0] = jnp.concatenate(outs, axis=-1).astype(o_ref.dtype)


def swa_attention(u, cos, sin, gq, gk, sinks):
    b, s, _ = u.shape
    nb = s // SWA_BLOCK
    qw, kw = SWA_HEADS * HEAD_DIM, SWA_KV_HEADS * HEAD_DIM
    cur = lambda c: (lambda i, n: (i, n, c))
    prev = lambda c: (lambda i, n: (i, jnp.maximum(n - 1, 0), c))
    tab = pl.BlockSpec((1, SWA_BLOCK, LANES), cur(0))
    tab_prev = pl.BlockSpec((1, SWA_BLOCK, LANES), prev(0))
    gain = pl.BlockSpec((1, LANES), lambda i, n: (0, 0))
    return pl.pallas_call(
        _swa_kernel,
        grid=(b, nb),
        in_specs=[pl.BlockSpec(memory_space=pltpu.SMEM),
                  pl.BlockSpec((1, SWA_BLOCK, qw), cur(0)),
                  pl.BlockSpec((1, SWA_BLOCK, kw), cur(qw // kw)),
                  pl.BlockSpec((1, SWA_BLOCK, kw), prev(qw // kw)),
                  pl.BlockSpec((1, SWA_BLOCK, kw), cur(qw // kw + 1)),
                  pl.BlockSpec((1, SWA_BLOCK, kw), prev(qw // kw + 1)),
                  tab, tab, tab_prev, tab_prev, gain, gain],
        out_specs=pl.BlockSpec((1, SWA_BLOCK, qw), cur(0)),
        out_shape=jax.ShapeDtypeStruct((b, s, qw), BF16),
        compiler_params=_params("parallel", "arbitrary"),
        name="swa_attention",
    )(sinks.astype(F32), u, u, u, u, u, cos, sin, cos, sin, gq, gk)


def _rwkv_prep_kernel(r_ref, k_ref, v_ref, l_ref, rp_ref, kp_ref, vp_ref, lp_ref,
                      mur_ref, muk_ref, muv_ref, mul_ref, w0_ref, w2_ref, a0_ref, a2_ref, g2_ref,
                      kk_ref, ka_ref, e_ref,
                      ro_ref, ko_ref, vo_ref, kko_ref, bo_ref, lwo_ref, go_ref):
    t = pl.program_id(1)

    def shifted(cur_ref, prev_ref, mu_ref):
        x = cur_ref[0]
        last = prev_ref[0][7:8, :] * jnp.where(t > 0, 1.0, 0.0)
        row = lax.broadcasted_iota(jnp.int32, x.shape, 0)
        prev = jnp.where(row == 0, last, pltpu.roll(x, 1, 0))
        return x + (prev - x) * mu_ref[...]

    r = shifted(r_ref, rp_ref, mur_ref)
    k = shifted(k_ref, kp_ref, muk_ref)
    v = shifted(v_ref, vp_ref, muv_ref)
    lo = shifted(l_ref, lp_ref, mul_ref)
    w_lo, a_lo, g_lo = lo[:, 0:LANES], lo[:, LANES:2 * LANES], lo[:, 2 * LANES:4 * LANES]
    hp = functools.partial(jnp.dot, precision=HIGHEST, preferred_element_type=F32)
    z = -(w0_ref[...] + hp(jnp.tanh(w_lo), w2_ref[...]))
    softplus = jnp.maximum(z, 0.0) + jnp.log(1.0 + jnp.exp(-jnp.abs(z)))
    w = -softplus - 0.5
    a = 1.0 / (1.0 + jnp.exp(-(a0_ref[...] + hp(a_lo, a2_ref[...]))))
    g = hp(1.0 / (1.0 + jnp.exp(-g_lo)), g2_ref[...])
    kk = k * kk_ref[...]
    kk = kk / jnp.maximum(jnp.sqrt(_seg_sum(kk * kk, e_ref[...])), 1e-12)
    ro_ref[0] = r
    ko_ref[0] = k * (1.0 + (a - 1.0) * ka_ref[...])
    vo_ref[0] = v
    kko_ref[0] = kk
    bo_ref[0] = kk * a
    lwo_ref[0] = -jnp.exp(w)
    go_ref[0] = g


def rwkv_prep(u, mu, w0, w2, a0, a2, g2, k_k, k_a, *, ts=512, tc=512):
    b, s, _ = u.shape
    ts = min(ts, s)
    nct = RWKV_DIM // tc
    base = (SWA_HEADS + 2 * SWA_KV_HEADS) * HEAD_DIM // tc
    lora_blk = base + 3 * nct

    def cur(off):
        return pl.BlockSpec((1, ts, tc), lambda i, t, c: (i, t, off + c))

    def prev(off):
        return pl.BlockSpec((1, 8, tc), lambda i, t, c: (i, jnp.maximum(t * (ts // 8) - 1, 0), off + c))

    lora_cur = pl.BlockSpec((1, ts, tc), lambda i, t, c: (i, t, lora_blk))
    lora_prev = pl.BlockSpec((1, 8, tc), lambda i, t, c: (i, jnp.maximum(t * (ts // 8) - 1, 0), lora_blk))
    vec = pl.BlockSpec((1, tc), lambda i, t, c: (0, c))
    vec0 = pl.BlockSpec((1, tc), lambda i, t, c: (0, 0))
    lora_w = lambda rows: pl.BlockSpec((rows, tc), lambda i, t, c: (0, c))
    pad_rows = lambda m, rows: jnp.pad(m, ((0, rows - m.shape[0]), (0, 0))).astype(F32)
    pad_cols = lambda vv, cols: jnp.pad(vv, (0, cols - vv.shape[0]))
    c3 = 3 * RWKV_DIM
    mu_l = jnp.concatenate([pad_cols(mu[c3:c3 + DECAY_LORA], LANES),
                            pad_cols(mu[c3 + DECAY_LORA:c3 + DECAY_LORA + AAA_LORA], LANES),
                            pad_cols(mu[c3 + DECAY_LORA + AAA_LORA:], 2 * LANES)])
    row = lambda vv: vv.reshape(1, -1).astype(F32)
    seg = (jnp.arange(tc)[:, None] // HEAD_DIM == jnp.arange(tc)[None, :] // HEAD_DIM).astype(BF16)
    out = jax.ShapeDtypeStruct((b, s, RWKV_DIM), F32)
    out_spec = pl.BlockSpec((1, ts, tc), lambda i, t, c: (i, t, c))
    return pl.pallas_call(
        _rwkv_prep_kernel,
        grid=(b, s // ts, nct),
        in_specs=[cur(base), cur(base + nct), cur(base + 2 * nct), lora_cur,
                  prev(base), prev(base + nct), prev(base + 2 * nct), lora_prev,
                  vec, vec, vec, vec0, vec, lora_w(LANES), vec, lora_w(LANES), lora_w(2 * LANES),
                  vec, vec, pl.BlockSpec((tc, tc), lambda i, t, c: (0, 0))],
        out_specs=[out_spec] * 7,
        out_shape=[out] * 7,
        compiler_params=_params("parallel", "parallel", "arbitrary"),
        name="rwkv_prep",
    )(u, u, u, u, u, u, u, u,
      row(mu[:RWKV_DIM]), row(mu[RWKV_DIM:2 * RWKV_DIM]), row(mu[2 * RWKV_DIM:c3]), row(mu_l),
      row(w0), pad_rows(w2, LANES), row(a0), pad_rows(a2, LANES), pad_rows(g2, 2 * LANES),
      row(k_k), row(k_a), seg)


def _mm(a, b, dims, passes):
    dn = (dims, ((), ()))
    dg = lambda x, y: lax.dot_general(x, y, dn, preferred_element_type=F32)
    ah = a.astype(BF16)
    bh = b.astype(BF16)
    if passes == 1:
        return dg(ah, bh)
    al = (a - ah.astype(F32)).astype(BF16)
    bl = (b - bh.astype(F32)).astype(BF16)
    return dg(ah, bh) + dg(ah, bl) + dg(al, bh)


_NN = ((1,), (0,))
_NT = ((1,), (1,))
_TN = ((0,), (0,))
P_SC, P_INV, P_PQ, P_OUT, P_ST = 1, 1, 1, 1, 1
RWKV_UNROLL = 8


def _rwkv_scan_kernel(r_ref, k_ref, v_ref, kk_ref, b_ref, lw_ref, g_ref, gng_ref, gnb_ref, rk_ref,
                      o_ref, st_ref, y1_ref, y0_ref, n_ref, z_ref, dec_ref, *, nchunk):
    c = RWKV_CHUNK
    lane_c = lax.broadcasted_iota(jnp.int32, (c, LANES), 1)
    head0 = lane_c < HEAD_DIM
    ri = lax.broadcasted_iota(jnp.int32, (2 * c, 2 * c), 0)
    ci = lax.broadcasted_iota(jnp.int32, (2 * c, 2 * c), 1)
    eye = jnp.where(ri == ci, 1.0, 0.0)
    tril_c = jnp.where(lax.broadcasted_iota(jnp.int32, (c, c), 0) >= lax.broadcasted_iota(jnp.int32, (c, c), 1),
                       1.0, 0.0).astype(BF16)
    stack = lambda x: jnp.concatenate([jnp.where(head0, x, 0.0), jnp.where(head0, 0.0, x)], axis=0)

    def seg_mean(x):
        m0 = jnp.sum(jnp.where(head0, x, 0.0), axis=-1, keepdims=True)
        m1 = jnp.sum(jnp.where(head0, 0.0, x), axis=-1, keepdims=True)
        return jnp.where(head0, m0, m1) * (1.0 / HEAD_DIM)

    def build(ics):
        each = lambda f, *cols: [f(*args) for args in zip(*cols)]
        sls = [pl.ds(pl.multiple_of(ic * c, c), c) for ic in ics]
        load = lambda ref: [ref[0, sl, :] for sl in sls]
        r, k, v, kk, b, lw = (load(ref) for ref in (r_ref, k_ref, v_ref, kk_ref, b_ref, lw_ref))

        def running_sum(x):
            l1 = x.astype(BF16)
            rest = x - l1.astype(F32)
            l2 = rest.astype(BF16)
            l3 = (rest - l2.astype(F32)).astype(BF16)
            return _dot(tril_c, l1) + _dot(tril_c, l2) + _dot(tril_c, l3)

        cum = each(running_sum, lw)
        yield
        cum_end = [x[c - 1:c, :] for x in cum]
        e_neg = each(lambda x: jnp.exp(-x), cum)
        e_end = each(lambda x, xe: jnp.exp(xe - x), cum, cum_end)
        a_s = each(lambda kk_, x, l: stack(-kk_ * jnp.exp(x - l)), kk, cum, lw)
        r_s = each(lambda r_, x: stack(r_ * jnp.exp(x)), r, cum)
        b_s = each(lambda b_, e: stack(b_ * e), b, e_neg)
        k_s = each(lambda k_, e: stack(k_ * e), k, e_neg)
        bh_s = each(lambda b_, e: stack(b_ * e), b, e_end)
        kh_s = each(lambda k_, e: stack(k_ * e), k, e_end)
        v_s = each(stack, v)
        strict = lambda x, y: jnp.where(ri > ci, _mm(x, y, _NT, P_SC), 0.0)
        incl = lambda x, y: jnp.where(ri >= ci, _mm(x, y, _NT, P_SC), 0.0)
        low = each(strict, a_s, b_s)
        a_ak = each(strict, a_s, k_s)
        a_rb = each(incl, r_s, b_s)
        a_rk = each(incl, r_s, k_s)
        yield
        inv = [eye + x for x in low]
        pw = low
        for _ in range(5):
            pw = each(lambda x: _mm(x, x, _NN, P_INV), pw)
            inv = each(lambda t, x: t + _mm(t, x, _NN, P_INV), inv, pw)
            yield
        akv = each(lambda x, y: _mm(x, y, _NN, P_PQ), a_ak, v_s)
        p = each(lambda t, x: _mm(t, x, _NN, P_PQ), inv, a_s)
        yield
        qv = each(lambda t, x: _mm(t, x, _NN, P_PQ), inv, akv)
        yield
        y1 = each(lambda rs, x, y: rs + _mm(x, y, _NN, P_OUT), r_s, a_rb, p)
        y0 = each(lambda x, y, z, w: _mm(x, y, _NN, P_OUT) + _mm(z, w, _NN, P_OUT), a_rb, qv, a_rk, v_s)
        nn = each(lambda x, y: _mm(x, y, _TN, P_OUT), p, bh_s)
        zz = each(lambda x, y, z, w: _mm(x, y, _TN, P_OUT) + _mm(z, w, _TN, P_OUT), qv, bh_s, v_s, kh_s)
        for i, ic in enumerate(ics):
            y1_ref[ic] = y1[i]
            y0_ref[ic] = y0[i]
            n_ref[ic] = nn[i]
            z_ref[ic] = zz[i]
            dec_ref[ic] = jnp.broadcast_to(jnp.exp(cum_end[i]), (8, LANES))

    def emit(ic, st):
        sl = pl.ds(pl.multiple_of(ic * c, c), c)
        r, k, v, g = r_ref[0, sl, :], k_ref[0, sl, :], v_ref[0, sl, :], g_ref[0, sl, :]
        y_st = _mm(y1_ref[ic], st, _NT, P_ST) + y0_ref[ic]
        y = y_st[0:c] + y_st[c:2 * c]
        mean = seg_mean(y)
        var = seg_mean((y - mean) * (y - mean))
        yn = (y - mean) * lax.rsqrt(var + RWKV_GN_EPS) * gng_ref[...] + gnb_ref[...]
        bonus = seg_mean(r * k * rk_ref[...]) * float(HEAD_DIM) * v
        o_ref[0, sl, :] = ((yn + bonus) * g).astype(o_ref.dtype)
        return st * dec_ref[ic][0:1, :] + _mm(st, n_ref[ic], _NN, P_ST) + z_ref[ic]

    def scan(ics):
        st = st_ref[...]
        for ic in ics:
            st = emit(ic, st)
            yield
        st_ref[...] = st

    def run(*gens):
        live = list(gens)
        while live:
            live = [gen for gen in live if next(gen, live) is not live]

    group = lambda io: [io * RWKV_UNROLL + i for i in range(RWKV_UNROLL)]
    ngroup = nchunk // RWKV_UNROLL
    st_ref[...] = jnp.zeros_like(st_ref)
    run(build(group(0)))

    @pl.loop(1, ngroup)
    def _(io):
        run(build(group(io)), scan(group(io - 1)))

    run(scan(group(ngroup - 1)))


def rwkv_scan(r, k, v, kk, b, lw, g, gn_g, gn_b, r_k):
    bsz, s, _ = r.shape
    npair = RWKV_DIM // LANES
    nchunk = s // RWKV_CHUNK
    assert nchunk % RWKV_UNROLL == 0
    seq = pl.BlockSpec((1, s, LANES), lambda i, p: (i, 0, p))
    vec = pl.BlockSpec((1, LANES), lambda i, p: (0, p))
    row = lambda vv: vv.reshape(1, -1).astype(F32)
    mat = pltpu.VMEM((nchunk, LANES, LANES), F32)
    return pl.pallas_call(
        functools.partial(_rwkv_scan_kernel, nchunk=nchunk),
        grid=(bsz, npair),
        in_specs=[seq] * 7 + [vec] * 3,
        out_specs=seq,
        out_shape=jax.ShapeDtypeStruct((bsz, s, RWKV_DIM), BF16),
        scratch_shapes=[pltpu.VMEM((LANES, LANES), F32), mat, mat, mat, mat,
                        pltpu.VMEM((nchunk, 8, LANES), F32)],
        compiler_params=_params("parallel", "parallel"),
        name="rwkv_scan",
    )(r, k, v, kk, b, lw, g, row(gn_g), row(gn_b), row(r_k))


def _mla_prep_kernel(q_ref, kv_ref, pe_ref, cos_ref, sin_ref, e_ref, gq_ref, gkn_ref, gkp_ref, invn_ref,
                     qo_ref, ko_ref, vo_ref):
    cos, sin = cos_ref[0], sin_ref[0]
    half = MLA_ROPE // 2

    def rope(x, g):
        w = x.shape[1]
        xg = x * _tile_lanes(g, w)
        return xg * _tile_lanes(cos, w) + _swap_halves(xg, half) * _tile_lanes(sin, w)

    x = q_ref[0]
    w = x.shape[1]
    inv_n = _tile_lanes(invn_ref[...], w)
    inv = lax.rsqrt(_seg_sum(x * x, e_ref[...]) * inv_n + EPS)
    qo_ref[0] = (rope(x, gq_ref[...]) * inv * (MLA_NOPE + MLA_ROPE) ** -0.5).astype(qo_ref.dtype)
    kv = kv_ref[0]
    inv_k = lax.rsqrt(_seg_sum(kv * kv, e_ref[...]) * inv_n + EPS)
    k_nope = kv * inv_k * _tile_lanes(gkn_ref[...], w)
    pe = pe_ref[0]
    inv_pe = lax.rsqrt(jnp.sum(pe * pe, axis=-1, keepdims=True) * (1.0 / MLA_ROPE) + EPS)
    k_pe = rope(pe, gkp_ref[...]) * inv_pe
    ko_ref[0] = (k_nope + _tile_lanes(k_pe, w)).astype(ko_ref.dtype)
    v = jnp.concatenate([kv[:, h * LANES + MLA_NOPE:(h + 1) * LANES] for h in range(w // LANES)], axis=-1)
    _store_transposed_tiles(vo_ref, v)


def mla_prep(q_lat, kv, u, cos, sin, gq, gkn, gkp, *, ts=512, tc=512):
    b, s, wtot = q_lat.shape
    ts = min(ts, s)
    lane = jnp.arange(tc)
    same = (lane[:, None] // LANES == lane[None, :] // LANES)
    pos = lane % LANES
    nope = pos < MLA_NOPE
    pe = (pos >= MLA_NOPE) & (pos < MLA_NOPE + MLA_ROPE)
    seg = (same & ((nope[:, None] & nope[None, :]) | (pe[:, None] & pe[None, :]))).astype(BF16)
    p1 = jnp.arange(LANES)
    inv_n = jnp.where(p1 < MLA_NOPE, 1.0 / MLA_NOPE, jnp.where(p1 < MLA_NOPE + MLA_ROPE, 1.0 / MLA_ROPE, 0.0))
    blk = pl.BlockSpec((1, ts, tc), lambda i, t, c: (i, t, c))
    tab = pl.BlockSpec((1, ts, LANES), lambda i, t, c: (i, t, 0))
    vec = pl.BlockSpec((1, LANES), lambda i, t, c: (0, 0))
    pe_blk = (MLA_Q_RANK + MLA_KV_RANK) // LANES
    out = jax.ShapeDtypeStruct((b, s, wtot), BF16)
    return pl.pallas_call(
        _mla_prep_kernel,
        grid=(b, s // ts, wtot // tc),
        in_specs=[blk, blk, pl.BlockSpec((1, ts, LANES), lambda i, t, c: (i, t, pe_blk)), tab, tab,
                  pl.BlockSpec((tc, tc), lambda i, t, c: (0, 0)), vec, vec, vec, vec],
        out_specs=[blk, blk, pl.BlockSpec((1, ts // ATTN_TILE, tc // LANES * MLA_V, ATTN_TILE),
                                          lambda i, t, c: (i, t, c, 0))],
        out_shape=[out, out, jax.ShapeDtypeStruct((b, s // ATTN_TILE, wtot // LANES * MLA_V, ATTN_TILE), BF16)],
        compiler_params=_params("parallel", "parallel", "arbitrary"),
        name="mla_prep",
    )(q_lat, kv, u, cos, sin, seg, gq, gkn, gkp, inv_n.reshape(1, LANES).astype(F32))


def _diff_prep_kernel(q_ref, k_ref, v_ref, cos_ref, sin_ref, e_ref, gq_ref, gk_ref, qo_ref, ko_ref, vo_ref):
    cos, sin = cos_ref[0], sin_ref[0]
    _store_transposed_tiles(vo_ref, v_ref[0])

    def prep(x, g, scale):
        w = x.shape[1]
        xg = x * _tile_lanes(g, w)
        xr = xg * _tile_lanes(cos, w) + _swap_halves(xg, DIFF_QK // 2) * _tile_lanes(sin, w)
        inv = lax.rsqrt(_seg_sum(x * x, e_ref[...]) * (1.0 / DIFF_QK) + EPS)
        return xr * inv * scale

    qo_ref[0] = prep(q_ref[0], gq_ref[...], DIFF_QK ** -0.5).astype(qo_ref.dtype)
    ko_ref[0] = prep(k_ref[0], gk_ref[...], 1.0).astype(ko_ref.dtype)


def diff_prep(u, cos, sin, gq, gk, *, ts=512, tc=512):
    b, s, _ = u.shape
    ts = min(ts, s)
    wtot = 2 * DIFF_HEADS * DIFF_QK
    q_base = (CD_IN_PAD - 3 * wtot) // tc
    seg = (jnp.arange(tc)[:, None] // DIFF_QK == jnp.arange(tc)[None, :] // DIFF_QK).astype(BF16)
    blk = lambda off: pl.BlockSpec((1, ts, tc), lambda i, t, c: (i, t, off + c))
    tab = pl.BlockSpec((1, ts, LANES), lambda i, t, c: (i, t, 0))
    vec = pl.BlockSpec((1, LANES), lambda i, t, c: (0, 0))
    out = jax.ShapeDtypeStruct((b, s, wtot), BF16)
    return pl.pallas_call(
        _diff_prep_kernel,
        grid=(b, s // ts, wtot // tc),
        in_specs=[blk(q_base), blk(q_base + wtot // tc), blk(q_base + 2 * wtot // tc), tab, tab,
                  pl.BlockSpec((tc, tc), lambda i, t, c: (0, 0)), vec, vec],
        out_specs=[blk(0), blk(0),
                   pl.BlockSpec((1, ts // ATTN_TILE, tc, ATTN_TILE), lambda i, t, c: (i, t, c, 0))],
        out_shape=[out, out, jax.ShapeDtypeStruct((b, s // ATTN_TILE, wtot, ATTN_TILE), BF16)],
        compiler_params=_params("parallel", "parallel", "arbitrary"),
        name="diff_prep",
    )(u, u, u, cos, sin, seg, gq, gk)


def _causal_attn_kernel(lam_ref, q_ref, k_ref, vt_ref, g_ref, o_ref, *, n_sm, tq, ow, out_scale):
    qi = pl.program_id(2)
    q = q_ref[0]
    hps = q.shape[1] // LANES
    slab = lambda x, h: x[:, h * LANES:(h + 1) * LANES]
    lane = lax.broadcasted_iota(jnp.int32, (tq, LANES), 1)
    qs, src = [], []
    for h in range(hps):
        qh = slab(q, h)
        if n_sm == 2:
            zero = jnp.zeros_like(qh)
            qs += [jnp.where(lane < DIFF_QK, qh, zero), jnp.where(lane < DIFF_QK, zero, qh)]
            src += [h, h]
        else:
            qs.append(qh)
            src.append(h)
    nch = len(qs)
    key_i = lax.broadcasted_iota(jnp.int32, (tq, tq), 0)
    qry_i = lax.broadcasted_iota(jnp.int32, (tq, tq), 1)

    def step(j, carry, diagonal):
        kj = k_ref[0, pl.ds(pl.multiple_of(j * tq, tq), tq), :]
        scores = lambda i: _dot_t(slab(kj, src[i]), qs[i])
        new = []
        ahead = [scores(i) for i in range(min(ATTN_LOOKAHEAD, nch))]
        for i in range(nch):
            s = ahead.pop(0)
            if i + ATTN_LOOKAHEAD < nch:
                ahead.append(scores(i + ATTN_LOOKAHEAD))
            if diagonal:
                s = jnp.where(key_i <= qry_i, s, NEG_INF)
            m, l, acc = carry[3 * i:3 * i + 3]
            m_new = jnp.maximum(m, jnp.max(s, axis=0, keepdims=True))
            alpha = jnp.exp(m - m_new)
            p = jnp.exp(s - m_new)
            new += [m_new, alpha * l + jnp.sum(p, axis=0, keepdims=True),
                    alpha * acc + _dot(vt_ref[0, j, src[i] * ow:(src[i] + 1) * ow, :], p.astype(BF16))]
        return tuple(new)

    init = (jnp.full((1, tq), NEG_INF, F32), jnp.zeros((1, tq), F32),
            jnp.zeros((ow, tq), F32)) * nch
    carry = lax.fori_loop(0, qi, lambda j, cr: step(j, cr, False), init)
    carry = step(qi, carry, True)
    outs = []
    for h in range(hps):
        c0 = 3 * n_sm * h
        o = carry[c0 + 2] / carry[c0 + 1]
        if n_sm == 2:
            o = o - lam_ref[0] * (carry[c0 + 5] / carry[c0 + 4])
            o = o * lax.rsqrt(jnp.mean(o * o, axis=0, keepdims=True) + EPS) * g_ref[...] * out_scale
        outs.append(o.T.astype(o_ref.dtype))
    o_ref[0] = jnp.concatenate(outs, axis=-1)


def causal_attention(q, k, vt, *, n_sm, lam=None, g=None, out_scale=1.0, hps=4, name="causal_attention"):
    b, s, wtot = q.shape
    tq = vt.shape[3]
    width = hps * LANES
    groups = wtot // width
    ow = vt.shape[2] // (wtot // LANES)
    lam = jnp.zeros((1,), F32) if lam is None else lam.reshape(1).astype(F32)
    g = jnp.ones((ow, 1), F32) if g is None else g.reshape(ow, 1).astype(F32)
    seq = pl.BlockSpec((1, s, width), lambda i, h, t: (i, 0, h))
    tile = pl.BlockSpec((1, tq, width), lambda i, h, t: (i, t, h))
    return pl.pallas_call(
        functools.partial(_causal_attn_kernel, n_sm=n_sm, tq=tq, ow=ow, out_scale=out_scale),
        grid=(b, groups, s // tq),
        in_specs=[pl.BlockSpec(memory_space=pltpu.SMEM), tile, seq,
                  pl.BlockSpec((1, s // tq, hps * ow, tq), lambda i, h, t: (i, 0, h, 0)),
                  pl.BlockSpec((ow, 1), lambda i, h, t: (0, 0))],
        out_specs=pl.BlockSpec((1, tq, hps * ow), lambda i, h, t: (i, t, h)),
        out_shape=jax.ShapeDtypeStruct((b, s, groups * hps * ow), BF16),
        compiler_params=_params("parallel", "parallel", "arbitrary"),
        name=name,
    )(lam, q, k, vt, g)


def _mem_attn_kernel(q_ref, kv_ref, gq_ref, gk_ref, o_ref):
    q = q_ref[0]
    kv = kv_ref[0]
    outs = []
    for h in range(MEM_HEADS):
        sl = slice(h * MEM_HEAD_DIM, (h + 1) * MEM_HEAD_DIM)
        qh = (_rms(q[:, sl], gq_ref[...]) * MEM_HEAD_DIM ** -0.5).astype(BF16)
        kh = _rms(kv[:, sl], gk_ref[...]).astype(BF16)
        vh = kv[:, MEM_W + h * MEM_HEAD_DIM:MEM_W + (h + 1) * MEM_HEAD_DIM].astype(BF16)
        s = _dot_t(qh, kh)
        p = jnp.exp(s - jnp.max(s, axis=-1, keepdims=True))
        outs.append(_dot(p.astype(BF16), vh) / jnp.sum(p, axis=-1, keepdims=True))
    o_ref[0] = jnp.concatenate(outs, axis=-1).astype(o_ref.dtype)


def mem_attention(q, mem_kv, gq, gk, *, tq=512):
    b, s, _ = q.shape
    m = mem_kv.shape[1]
    tq = min(tq, s)
    vec = pl.BlockSpec((1, MEM_HEAD_DIM), lambda i, t: (0, 0))
    return pl.pallas_call(
        _mem_attn_kernel,
        grid=(b, s // tq),
        in_specs=[pl.BlockSpec((1, tq, MEM_W), lambda i, t: (i, t, 0)),
                  pl.BlockSpec((1, m, 2 * MEM_W), lambda i, t: (i, 0, 0)), vec, vec],
        out_specs=pl.BlockSpec((1, tq, MEM_W), lambda i, t: (i, t, 0)),
        out_shape=jax.ShapeDtypeStruct((b, s, MEM_W), BF16),
        compiler_params=_params("parallel", "arbitrary"),
        name="mem_attention",
    )(q, mem_kv, gq.reshape(1, -1).astype(F32), gk.reshape(1, -1).astype(F32))


def _rope_tables(positions, dim, lead_ones, tail):
    inv = 1.0 / (ROPE_THETA ** (jnp.arange(0, dim, 2, dtype=F32) / dim))
    ang = positions.astype(F32)[..., None] * inv
    c, s = jnp.cos(ang), jnp.sin(ang)
    shape = positions.shape
    cos = jnp.concatenate([jnp.ones(shape + (lead_ones,), F32), c, c, jnp.ones(shape + (tail,), F32)], axis=-1)
    sin = jnp.concatenate([jnp.zeros(shape + (lead_ones,), F32), -s, s, jnp.zeros(shape + (tail,), F32)], axis=-1)
    reps = LANES // cos.shape[-1]
    return jnp.tile(cos, (1, 1, reps)), jnp.tile(sin, (1, 1, reps))


def _pad_cols(w, cols):
    return jnp.pad(w, ((0, 0), (0, cols - w.shape[1])))


def _ab_in_layout(w):
    c = (SWA_HEADS + 2 * SWA_KV_HEADS) * HEAD_DIM + 3 * RWKV_DIM
    return jnp.concatenate([w[:, :c], _pad_cols(w[:, c:c + DECAY_LORA], LANES),
                            _pad_cols(w[:, c + DECAY_LORA:c + DECAY_LORA + AAA_LORA], LANES),
                            _pad_cols(w[:, c + DECAY_LORA + AAA_LORA:], 2 * LANES)], axis=1)


def _cd_in_layout(w):
    c1 = MLA_Q_RANK + MLA_KV_RANK
    z = lambda n: jnp.zeros((w.shape[0], n), w.dtype)
    return jnp.concatenate([w[:, :c1], z(MLA_NOPE), w[:, c1:c1 + MLA_ROPE], z(LANES - MLA_NOPE - MLA_ROPE),
                            z(LANES), w[:, c1 + MLA_ROPE:]], axis=1)


def _head_slabs(w, per_head):
    k = w.shape[0]
    return jnp.pad(w.reshape(k, -1, per_head), ((0, 0), (0, 0), (0, LANES - per_head))).reshape(k, -1)


def _slab_vec(*parts):
    v = jnp.concatenate([p.astype(F32) for p in parts])
    return jnp.pad(v, (0, LANES - v.shape[0])).reshape(1, LANES)


def kernel(x, mem, positions, ffn1_norm, ffn1_w_gate, ffn1_w_up, ffn1_w_down, mix_norm, ab_w_in, ab_w_out, swa_q_norm, swa_k_norm, swa_sinks, rwkv_mu, rwkv_w0, rwkv_w2, rwkv_a0, rwkv_a2, rwkv_g2, rwkv_k_k, rwkv_k_a, rwkv_r_k, rwkv_gn_g, rwkv_gn_b, cd_w_in, cd_w_out, mla_cq_norm, mla_ckv_norm, mla_w_uq, mla_w_ukv, mla_q_nope_norm, mla_k_nope_norm, mla_q_rope_norm, mla_k_rope_norm, diff_q_norm, diff_k_norm, diff_lq1, diff_lk1, diff_lq2, diff_lk2, diff_subln, memx_norm, memx_w_q, memx_q_norm, memx_w_o, mem_norm, mem_w_kv, mem_k_norm, ffn2_norm, ffn2_w_gate, ffn2_w_up, ffn2_w_down):
    b, s, d = x.shape
    m = mem.shape[1]
    t = b * s
    depth = ffn1_norm.shape[0]
    bf = lambda w: w.astype(BF16)
    cos64, sin64 = _rope_tables(positions, HEAD_DIM, 0, 0)
    cos32, sin32 = _rope_tables(positions, MLA_ROPE, MLA_NOPE, LANES - MLA_NOPE - MLA_ROPE)

    mem_kv = norm_matmul(mem.reshape(b * m, d), mem_norm, bf(mem_w_kv), name="mem_kv").reshape(b, m, 2 * MEM_W)

    x = x.reshape(t, d)
    for layer in range(depth):
        j = layer // 2
        x = ffn(x, ffn1_norm[layer], ffn1_w_gate, ffn1_w_up, ffn1_w_down, layer, name="ffn1")
        if layer % 2 == 0:
            u = norm_matmul(x, mix_norm[layer], bf(_ab_in_layout(ab_w_in[j])), name="ab_in")
            u = u.reshape(b, s, AB_IN_PAD)
            y_a = swa_attention(u, cos64, sin64, _slab_vec(swa_q_norm[j], swa_q_norm[j]),
                                _slab_vec(swa_k_norm[j], swa_k_norm[j]), swa_sinks[j])
            prep = rwkv_prep(u, rwkv_mu[j], rwkv_w0[j], rwkv_w2[j], rwkv_a0[j], rwkv_a2[j], rwkv_g2[j],
                             rwkv_k_k[j], rwkv_k_a[j])
            y_b = rwkv_scan(*prep, rwkv_gn_g[j], rwkv_gn_b[j], rwkv_r_k[j])
            w_out = bf(ab_w_out[j])
            half = SWA_HEADS * HEAD_DIM
            x = matmul_residual(x, [y_a.reshape(t, -1), y_b.reshape(t, -1)], [w_out[:half], w_out[half:]],
                                name="ab_out")
        else:
            u = norm_matmul(x, mix_norm[layer], bf(_cd_in_layout(cd_w_in[j])), name="cd_in")
            q_lat = norm_matmul(u, mla_cq_norm[j], bf(_head_slabs(mla_w_uq[j], MLA_NOPE + MLA_ROPE)),
                                k_blk=0, name="mla_uq")
            kv = norm_matmul(u, mla_ckv_norm[j], bf(mla_w_ukv[j]), k_blk=MLA_Q_RANK // MLA_KV_RANK, name="mla_ukv")
            u = u.reshape(b, s, CD_IN_PAD)
            zero64 = jnp.zeros((MLA_NOPE,), F32)
            q_c, k_c, vt_c = mla_prep(q_lat.reshape(b, s, -1), kv.reshape(b, s, -1), u, cos32, sin32,
                                      _slab_vec(mla_q_nope_norm[j], mla_q_rope_norm[j]),
                                      _slab_vec(mla_k_nope_norm[j]),
                                      _slab_vec(zero64, mla_k_rope_norm[j]))
            y_c = causal_attention(q_c, k_c, vt_c, n_sm=1, hps=8, name="mla_attention")
            q_d, k_d, vt_d = diff_prep(u, cos64, sin64, _slab_vec(diff_q_norm[j], diff_q_norm[j]),
                                       _slab_vec(diff_k_norm[j], diff_k_norm[j]))
            lambda_init = 0.8 - 0.6 * math.exp(-0.3 * layer)
            lam = (jnp.exp(jnp.sum(diff_lq1[j].astype(F32) * diff_lk1[j].astype(F32)))
                   - jnp.exp(jnp.sum(diff_lq2[j].astype(F32) * diff_lk2[j].astype(F32))) + lambda_init)
            y_d = causal_attention(q_d, k_d, vt_d, n_sm=2, hps=4, lam=lam, g=diff_subln[j],
                                   out_scale=1.0 - lambda_init, name="diff_attention")
            w_out = bf(cd_w_out[j])
            n_c = MLA_HEADS * MLA_V
            x = matmul_residual(x, [y_c.reshape(t, -1), y_d.reshape(t, -1)], [w_out[:n_c], w_out[n_c:]],
                                name="cd_out")
        q_m = norm_matmul(x, memx_norm[layer], bf(memx_w_q[layer]), name="memx_q").reshape(b, s, MEM_W)
        o_m = mem_attention(q_m, mem_kv, memx_q_norm[layer], mem_k_norm)
        x = matmul_residual(x, [o_m.reshape(t, MEM_W)], [bf(memx_w_o[layer])], name="memx_out")
        x = ffn(x, ffn2_norm[layer], ffn2_w_gate, ffn2_w_up, ffn2_w_down, layer, name="ffn2")
    return x.reshape(b, s, d)
```

```python
import functools
import math

import jax
import jax.numpy as jnp
from jax import lax
from jax.experimental import pallas as pl
from jax.experimental.pallas import tpu as pltpu

F32 = jnp.float32
BF16 = jnp.bfloat16
HIGHEST = lax.Precision.HIGHEST

EPS = 1e-6
ROPE_THETA = 10000.0
NEG_INF = -1e30
ATTN_LOOKAHEAD = 4
ATTN_TILE = 256
LANES = 128

D_MODEL = 2048
D_FF = 5632
HEAD_DIM = 64
SWA_HEADS = 16
SWA_KV_HEADS = 4
SWA_BLOCK = 128
RWKV_DIM = 1024
RWKV_CHUNK = 64
RWKV_GN_EPS = 64e-5
DECAY_LORA, AAA_LORA, GATE_LORA = 64, 64, 160
MLA_HEADS, MLA_Q_RANK, MLA_KV_RANK, MLA_NOPE, MLA_ROPE, MLA_V = 16, 512, 256, 64, 32, 64
DIFF_HEADS, DIFF_QK, DIFF_V = 8, 64, 128
MEM_HEADS, MEM_HEAD_DIM = 4, 128
MEM_W = MEM_HEADS * MEM_HEAD_DIM
AB_IN_PAD = 5120
CD_IN_PAD = 4096

VMEM_LIMIT = 48 * 1024 * 1024
FFN_VMEM_LIMIT = 56 * 1024 * 1024


def _params(*sem):
    return pltpu.CompilerParams(dimension_semantics=sem, vmem_limit_bytes=VMEM_LIMIT)


def _dot(a, b):
    return jnp.dot(a, b, preferred_element_type=F32)


def _dot_t(a, b):
    return lax.dot_general(a, b, (((1,), (1,)), ((), ())), preferred_element_type=F32)


def _dot_0(a, b):
    return lax.dot_general(a, b, (((0,), (0,)), ((), ())), preferred_element_type=F32)


def _rms(x, g):
    return x * lax.rsqrt(jnp.mean(x * x, axis=-1, keepdims=True) + EPS) * g


def _seg_sum(x, e):
    hi = x.astype(BF16)
    lo = (x - hi.astype(F32)).astype(BF16)
    return _dot(hi, e) + _dot(lo, e)


def _tile_lanes(v, width):
    return v if v.shape[-1] == width else jnp.tile(v, (1, width // v.shape[-1]))


def _store_transposed_tiles(vt_ref, v):
    for r in range(vt_ref.shape[1]):
        vt_ref[0, r] = v[r * ATTN_TILE:(r + 1) * ATTN_TILE, :].T.astype(vt_ref.dtype)


def _swap_halves(x, half):
    w = x.shape[-1]
    lane = lax.broadcasted_iota(jnp.int32, x.shape, x.ndim - 1)
    low = (lane & (2 * half - 1)) < half
    return jnp.where(low, pltpu.roll(x, w - half, x.ndim - 1), pltpu.roll(x, half, x.ndim - 1))


def _norm_matmul_kernel(x_ref, g_ref, w_ref, o_ref, xn_ref):
    @pl.when(pl.program_id(1) == 0)
    def _():
        xn_ref[...] = _rms(x_ref[...].astype(F32), g_ref[...]).astype(BF16)

    o_ref[...] = _dot(xn_ref[...], w_ref[...]).astype(o_ref.dtype)


def norm_matmul(x, g, w, *, k_blk=0, out_dtype=F32, tm=1024, tn=512, name="norm_matmul"):
    t = x.shape[0]
    k, n = w.shape
    tm, tn = min(tm, t), min(tn, n)
    assert t % tm == 0 and n % tn == 0
    return pl.pallas_call(
        _norm_matmul_kernel,
        grid=(t // tm, n // tn),
        in_specs=[pl.BlockSpec((tm, k), lambda i, j: (i, k_blk)),
                  pl.BlockSpec((1, k), lambda i, j: (0, 0)),
                  pl.BlockSpec((k, tn), lambda i, j: (0, j))],
        out_specs=pl.BlockSpec((tm, tn), lambda i, j: (i, j)),
        out_shape=jax.ShapeDtypeStruct((t, n), out_dtype),
        scratch_shapes=[pltpu.VMEM((tm, k), BF16)],
        compiler_params=_params("parallel", "arbitrary"),
        name=name,
    )(x, g.reshape(1, k).astype(F32), w)


def _matmul_res_kernel(*refs, n_in):
    x_ref, o_ref = refs[2 * n_in], refs[2 * n_in + 1]
    acc = x_ref[...]
    for i in range(n_in):
        acc = acc + _dot(refs[i][...], refs[n_in + i][...])
    o_ref[...] = acc


def matmul_residual(x, a_list, w_list, *, tm=256, name="matmul_residual"):
    t, n = x.shape
    tm = min(tm, t)
    assert t % tm == 0
    n_in = len(a_list)
    in_specs = ([pl.BlockSpec((tm, a.shape[1]), lambda i: (i, 0)) for a in a_list]
                + [pl.BlockSpec(w.shape, lambda i: (0, 0)) for w in w_list]
                + [pl.BlockSpec((tm, n), lambda i: (i, 0))])
    return pl.pallas_call(
        functools.partial(_matmul_res_kernel, n_in=n_in),
        grid=(t // tm,),
        in_specs=in_specs,
        out_specs=pl.BlockSpec((tm, n), lambda i: (i, 0)),
        out_shape=jax.ShapeDtypeStruct((t, n), F32),
        compiler_params=_params("parallel"),
        name=name,
    )(*a_list, *w_list, x)


def _ffn_kernel(x_ref, g_ref, wg_ref, wu_ref, wd_ref, o_ref, xn_ref, *, nf):
    f = pl.program_id(1)

    @pl.when(f == 0)
    def _():
        x = x_ref[...]
        xn_ref[...] = _rms(x, g_ref[...]).astype(BF16)
        o_ref[...] = 2.0 * x

    xn = xn_ref[...]
    a = _dot(xn, wg_ref[...].astype(BF16))
    b = _dot(xn, wu_ref[...].astype(BF16))
    h = (a * (1.0 / (1.0 + jnp.exp(-a))) * b).astype(BF16)
    o_ref[...] += _dot(h, wd_ref[...].astype(BF16))

    @pl.when(f == nf - 1)
    def _():
        o_ref[...] = 0.5 * o_ref[...]


def ffn(x, g, wg, wu, wd, layer, *, tm=1024, tf=256, name="ffn"):
    t, d = x.shape
    ff = wg.shape[2]
    tm = min(tm, t)
    assert t % tm == 0 and ff % tf == 0
    nf = ff // tf
    return pl.pallas_call(
        functools.partial(_ffn_kernel, nf=nf),
        grid=(t // tm, nf),
        in_specs=[pl.BlockSpec((tm, d), lambda i, f: (i, 0), pipeline_mode=pl.Buffered(1)),
                  pl.BlockSpec((1, d), lambda i, f: (0, 0)),
                  pl.BlockSpec((None, d, tf), lambda i, f: (layer, 0, f)),
                  pl.BlockSpec((None, d, tf), lambda i, f: (layer, 0, f)),
                  pl.BlockSpec((None, tf, d), lambda i, f: (layer, f, 0))],
        out_specs=pl.BlockSpec((tm, d), lambda i, f: (i, 0)),
        out_shape=jax.ShapeDtypeStruct((t, d), F32),
        scratch_shapes=[pltpu.VMEM((tm, d), BF16)],
        compiler_params=pltpu.CompilerParams(dimension_semantics=("parallel", "arbitrary"),
                                             vmem_limit_bytes=FFN_VMEM_LIMIT),
        name=name,
    )(x, g.reshape(1, d).astype(F32), wg, wu, wd)


def _swa_kernel(sink_ref, q_ref, kc_ref, kp_ref, vc_ref, vp_ref, cc_ref, sc_ref, cp_ref, sp_ref,
                gq_ref, gk_ref, e_ref, o_ref):
    n = pl.program_id(1)
    blk = SWA_BLOCK
    group = SWA_HEADS // SWA_KV_HEADS
    q = q_ref[0]
    k = jnp.concatenate([kp_ref[0], kc_ref[0]], axis=0)
    v = jnp.concatenate([vp_ref[0], vc_ref[0]], axis=0)
    cos_q, sin_q = cc_ref[0], sc_ref[0]
    cos_k = jnp.concatenate([cp_ref[0], cos_q], axis=0)
    sin_k = jnp.concatenate([sp_ref[0], sin_q], axis=0)
    seg = e_ref[...]
    sw = seg.shape[0]

    def prep(x, g, cos, sin, scale):
        w = x.shape[1]
        xg = x * _tile_lanes(g, w)
        xr = xg * _tile_lanes(cos, w) + _swap_halves(xg, HEAD_DIM // 2) * _tile_lanes(sin, w)
        ss = jnp.concatenate([_seg_sum(x[:, i:i + sw] * x[:, i:i + sw], seg) for i in range(0, w, sw)], axis=-1)
        return xr * lax.rsqrt(ss * (1.0 / HEAD_DIM) + EPS) * scale

    qr = prep(q, gq_ref[...], cos_q, sin_q, HEAD_DIM ** -0.5).astype(BF16)
    kr = prep(k, gk_ref[...], cos_k, sin_k, 1.0)
    key_i = lax.broadcasted_iota(jnp.int32, (2 * blk, blk), 0)
    qry_i = lax.broadcasted_iota(jnp.int32, (2 * blk, blk), 1)
    rel = qry_i + blk - key_i
    valid = (rel >= 0) & (rel < blk) & ((n > 0) | (key_i >= blk))
    low = lax.broadcasted_iota(jnp.int32, (blk, LANES), 1) < HEAD_DIM
    zero = jnp.zeros((blk, LANES), BF16)
    k_dup, v_t = [], []
    for g in range(SWA_KV_HEADS):
        kg = kr[:, g * HEAD_DIM:(g + 1) * HEAD_DIM]
        k_dup.append(jnp.concatenate([kg, kg], axis=-1).astype(BF16))
        v_t.append(v[:, g * HEAD_DIM:(g + 1) * HEAD_DIM].T.astype(BF16))

    def scores(h):
        slab = qr[:, (h // 2) * LANES:(h // 2 + 1) * LANES]
        qh = jnp.where(low, slab, zero) if h % 2 == 0 else jnp.where(low, zero, slab)
        return _dot_t(k_dup[h // group], qh)

    ahead = [scores(h) for h in range(ATTN_LOOKAHEAD)]
    outs = []
    for h in range(SWA_HEADS):
        s = jnp.where(valid, ahead.pop(0), NEG_INF)
        if h + ATTN_LOOKAHEAD < SWA_HEADS:
            ahead.append(scores(h + ATTN_LOOKAHEAD))
        sink = sink_ref[h]
        m = jnp.maximum(jnp.max(s, axis=0, keepdims=True), sink)
        p = jnp.exp(s - m)
        den = jnp.sum(p, axis=0, keepdims=True) + jnp.exp(sink - m)
        outs.append(_dot(v_t[h // group], p.astype(BF16)) / den)
    slabs = [jnp.concatenate(outs[i:i + 2], axis=0).T for i in range(0, SWA_HEADS, 2)]
    o_ref[0] = jnp.concatenate(slabs, axis=-1).astype(o_ref.dtype)


def swa_attention(u, cos, sin, gq, gk, sinks):
    b, s, _ = u.shape
    nb = s // SWA_BLOCK
    qw, kw = SWA_HEADS * HEAD_DIM, SWA_KV_HEADS * HEAD_DIM
    cur = lambda c: (lambda i, n: (i, n, c))
    prev = lambda c: (lambda i, n: (i, jnp.maximum(n - 1, 0), c))
    tab = pl.BlockSpec((1, SWA_BLOCK, LANES), cur(0))
    tab_prev = pl.BlockSpec((1, SWA_BLOCK, LANES), prev(0))
    gain = pl.BlockSpec((1, LANES), lambda i, n: (0, 0))
    seg = (jnp.arange(kw)[:, None] // HEAD_DIM == jnp.arange(kw)[None, :] // HEAD_DIM).astype(BF16)
    return pl.pallas_call(
        _swa_kernel,
        grid=(b, nb),
        in_specs=[pl.BlockSpec(memory_space=pltpu.SMEM),
                  pl.BlockSpec((1, SWA_BLOCK, qw), cur(0)),
                  pl.BlockSpec((1, SWA_BLOCK, kw), cur(qw // kw)),
                  pl.BlockSpec((1, SWA_BLOCK, kw), prev(qw // kw)),
                  pl.BlockSpec((1, SWA_BLOCK, kw), cur(qw // kw + 1)),
                  pl.BlockSpec((1, SWA_BLOCK, kw), prev(qw // kw + 1)),
                  tab, tab, tab_prev, tab_prev, gain, gain, pl.BlockSpec((kw, kw), lambda i, n: (0, 0))],
        out_specs=pl.BlockSpec((1, SWA_BLOCK, qw), cur(0)),
        out_shape=jax.ShapeDtypeStruct((b, s, qw), BF16),
        compiler_params=_params("parallel", "arbitrary"),
        name="swa_attention",
    )(sinks.astype(F32), u, u, u, u, u, cos, sin, cos, sin, gq, gk, seg)


def _rwkv_prep_kernel(r_ref, k_ref, v_ref, l_ref, rp_ref, kp_ref, vp_ref, lp_ref,
                      mur_ref, muk_ref, muv_ref, mul_ref, w0_ref, w2_ref, a0_ref, a2_ref, g2_ref,
                      kk_ref, ka_ref, e_ref,
                      ro_ref, ko_ref, vo_ref, kko_ref, bo_ref, lwo_ref, go_ref):
    t = pl.program_id(1)

    def shifted(cur_ref, prev_ref, mu_ref):
        x = cur_ref[0]
        last = prev_ref[0][7:8, :] * jnp.where(t > 0, 1.0, 0.0)
        row = lax.broadcasted_iota(jnp.int32, x.shape, 0)
        prev = jnp.where(row == 0, last, pltpu.roll(x, 1, 0))
        return x + (prev - x) * mu_ref[...]

    r = shifted(r_ref, rp_ref, mur_ref)
    k = shifted(k_ref, kp_ref, muk_ref)
    v = shifted(v_ref, vp_ref, muv_ref)
    lo = shifted(l_ref, lp_ref, mul_ref)
    w_lo, a_lo, g_lo = lo[:, 0:LANES], lo[:, LANES:2 * LANES], lo[:, 2 * LANES:4 * LANES]
    hp = functools.partial(jnp.dot, precision=HIGHEST, preferred_element_type=F32)
    z = -(w0_ref[...] + hp(jnp.tanh(w_lo), w2_ref[...]))
    softplus = jnp.maximum(z, 0.0) + jnp.log(1.0 + jnp.exp(-jnp.abs(z)))
    w = -softplus - 0.5
    a = 1.0 / (1.0 + jnp.exp(-(a0_ref[...] + hp(a_lo, a2_ref[...]))))
    g = hp(1.0 / (1.0 + jnp.exp(-g_lo)), g2_ref[...])
    kk = k * kk_ref[...]
    kk = kk / jnp.maximum(jnp.sqrt(_seg_sum(kk * kk, e_ref[...])), 1e-12)
    ro_ref[0] = r
    ko_ref[0] = k * (1.0 + (a - 1.0) * ka_ref[...])
    vo_ref[0] = v
    kko_ref[0] = kk
    bo_ref[0] = kk * a
    lwo_ref[0] = -jnp.exp(w)
    go_ref[0] = g


def rwkv_prep(u, mu, w0, w2, a0, a2, g2, k_k, k_a, *, ts=512, tc=512):
    b, s, _ = u.shape
    ts = min(ts, s)
    nct = RWKV_DIM // tc
    base = (SWA_HEADS + 2 * SWA_KV_HEADS) * HEAD_DIM // tc
    lora_blk = base + 3 * nct

    def cur(off):
        return pl.BlockSpec((1, ts, tc), lambda i, t, c: (i, t, off + c))

    def prev(off):
        return pl.BlockSpec((1, 8, tc), lambda i, t, c: (i, jnp.maximum(t * (ts // 8) - 1, 0), off + c))

    lora_cur = pl.BlockSpec((1, ts, tc), lambda i, t, c: (i, t, lora_blk))
    lora_prev = pl.BlockSpec((1, 8, tc), lambda i, t, c: (i, jnp.maximum(t * (ts // 8) - 1, 0), lora_blk))
    vec = pl.BlockSpec((1, tc), lambda i, t, c: (0, c))
    vec0 = pl.BlockSpec((1, tc), lambda i, t, c: (0, 0))
    lora_w = lambda rows: pl.BlockSpec((rows, tc), lambda i, t, c: (0, c))
    pad_rows = lambda m, rows: jnp.pad(m, ((0, rows - m.shape[0]), (0, 0))).astype(F32)
    pad_cols = lambda vv, cols: jnp.pad(vv, (0, cols - vv.shape[0]))
    c3 = 3 * RWKV_DIM
    mu_l = jnp.concatenate([pad_cols(mu[c3:c3 + DECAY_LORA], LANES),
                            pad_cols(mu[c3 + DECAY_LORA:c3 + DECAY_LORA + AAA_LORA], LANES),
                            pad_cols(mu[c3 + DECAY_LORA + AAA_LORA:], 2 * LANES)])
    row = lambda vv: vv.reshape(1, -1).astype(F32)
    seg = (jnp.arange(tc)[:, None] // HEAD_DIM == jnp.arange(tc)[None, :] // HEAD_DIM).astype(BF16)
    out = jax.ShapeDtypeStruct((b, s, RWKV_DIM), F32)
    out_spec = pl.BlockSpec((1, ts, tc), lambda i, t, c: (i, t, c))
    return pl.pallas_call(
        _rwkv_prep_kernel,
        grid=(b, s // ts, nct),
        in_specs=[cur(base), cur(base + nct), cur(base + 2 * nct), lora_cur,
                  prev(base), prev(base + nct), prev(base + 2 * nct), lora_prev,
                  vec, vec, vec, vec0, vec, lora_w(LANES), vec, lora_w(LANES), lora_w(2 * LANES),
                  vec, vec, pl.BlockSpec((tc, tc), lambda i, t, c: (0, 0))],
        out_specs=[out_spec] * 7,
        out_shape=[out] * 7,
        compiler_params=_params("parallel", "parallel", "arbitrary"),
        name="rwkv_prep",
    )(u, u, u, u, u, u, u, u,
      row(mu[:RWKV_DIM]), row(mu[RWKV_DIM:2 * RWKV_DIM]), row(mu[2 * RWKV_DIM:c3]), row(mu_l),
      row(w0), pad_rows(w2, LANES), row(a0), pad_rows(a2, LANES), pad_rows(g2, 2 * LANES),
      row(k_k), row(k_a), seg)


def _mm(a, b, dims, passes):
    dn = (dims, ((), ()))
    dg = lambda x, y: lax.dot_general(x, y, dn, preferred_element_type=F32)
    ah = a.astype(BF16)
    bh = b.astype(BF16)
    if passes == 1:
        return dg(ah, bh)
    al = (a - ah.astype(F32)).astype(BF16)
    bl = (b - bh.astype(F32)).astype(BF16)
    return dg(ah, bh) + dg(ah, bl) + dg(al, bh)


_NN = ((1,), (0,))
_NT = ((1,), (1,))
_TN = ((0,), (0,))
P_SC, P_INV, P_PQ, P_OUT, P_ST = 1, 1, 1, 1, 1
RWKV_UNROLL = 8


def _rwkv_scan_kernel(r_ref, k_ref, v_ref, kk_ref, b_ref, lw_ref, g_ref, gng_ref, gnb_ref, rk_ref,
                      o_ref, st_ref, y1_ref, y0_ref, n_ref, z_ref, dec_ref, *, nchunk):
    c = RWKV_CHUNK
    lane_c = lax.broadcasted_iota(jnp.int32, (c, LANES), 1)
    head0 = lane_c < HEAD_DIM
    ri = lax.broadcasted_iota(jnp.int32, (2 * c, 2 * c), 0)
    ci = lax.broadcasted_iota(jnp.int32, (2 * c, 2 * c), 1)
    eye = jnp.where(ri == ci, 1.0, 0.0)
    tril_c = jnp.where(lax.broadcasted_iota(jnp.int32, (c, c), 0) >= lax.broadcasted_iota(jnp.int32, (c, c), 1),
                       1.0, 0.0).astype(BF16)
    stack = lambda x: jnp.concatenate([jnp.where(head0, x, 0.0), jnp.where(head0, 0.0, x)], axis=0)

    def seg_mean(x):
        m0 = jnp.sum(jnp.where(head0, x, 0.0), axis=-1, keepdims=True)
        m1 = jnp.sum(jnp.where(head0, 0.0, x), axis=-1, keepdims=True)
        return jnp.where(head0, m0, m1) * (1.0 / HEAD_DIM)

    def build(ics):
        each = lambda f, *cols: [f(*args) for args in zip(*cols)]
        sls = [pl.ds(pl.multiple_of(ic * c, c), c) for ic in ics]
        load = lambda ref: [ref[0, sl, :] for sl in sls]
        r, k, v, kk, b, lw = (load(ref) for ref in (r_ref, k_ref, v_ref, kk_ref, b_ref, lw_ref))

        def running_sum(x):
            l1 = x.astype(BF16)
            rest = x - l1.astype(F32)
            l2 = rest.astype(BF16)
            l3 = (rest - l2.astype(F32)).astype(BF16)
            return _dot(tril_c, l1) + _dot(tril_c, l2) + _dot(tril_c, l3)

        cum = each(running_sum, lw)
        yield
        cum_end = [x[c - 1:c, :] for x in cum]
        e_neg = each(lambda x: jnp.exp(-x), cum)
        e_end = each(lambda x, xe: jnp.exp(xe - x), cum, cum_end)
        a_s = each(lambda kk_, x, l: stack(-kk_ * jnp.exp(x - l)), kk, cum, lw)
        r_s = each(lambda r_, x: stack(r_ * jnp.exp(x)), r, cum)
        b_s = each(lambda b_, e: stack(b_ * e), b, e_neg)
        k_s = each(lambda k_, e: stack(k_ * e), k, e_neg)
        bh_s = each(lambda b_, e: stack(b_ * e), b, e_end)
        kh_s = each(lambda k_, e: stack(k_ * e), k, e_end)
        v_s = each(stack, v)
        strict = lambda x, y: jnp.where(ri > ci, _mm(x, y, _NT, P_SC), 0.0)
        incl = lambda x, y: jnp.where(ri >= ci, _mm(x, y, _NT, P_SC), 0.0)
        low = each(strict, a_s, b_s)
        a_ak = each(strict, a_s, k_s)
        a_rb = each(incl, r_s, b_s)
        a_rk = each(incl, r_s, k_s)
        yield
        inv = [eye + x for x in low]
        pw = low
        for _ in range(5):
            pw = each(lambda x: _mm(x, x, _NN, P_INV), pw)
            inv = each(lambda t, x: t + _mm(t, x, _NN, P_INV), inv, pw)
            yield
        akv = each(lambda x, y: _mm(x, y, _NN, P_PQ), a_ak, v_s)
        p = each(lambda t, x: _mm(t, x, _NN, P_PQ), inv, a_s)
        yield
        qv = each(lambda t, x: _mm(t, x, _NN, P_PQ), inv, akv)
        yield
        y1 = each(lambda rs, x, y: rs + _mm(x, y, _NN, P_OUT), r_s, a_rb, p)
        y0 = each(lambda x, y, z, w: _mm(x, y, _NN, P_OUT) + _mm(z, w, _NN, P_OUT), a_rb, qv, a_rk, v_s)
        nn = each(lambda x, y: _mm(x, y, _TN, P_OUT), p, bh_s)
        zz = each(lambda x, y, z, w: _mm(x, y, _TN, P_OUT) + _mm(z, w, _TN, P_OUT), qv, bh_s, v_s, kh_s)
        for i, ic in enumerate(ics):
            y1_ref[ic] = y1[i]
            y0_ref[ic] = y0[i]
            n_ref[ic] = nn[i]
            z_ref[ic] = zz[i]
            dec_ref[ic] = jnp.broadcast_to(jnp.exp(cum_end[i]), (8, LANES))

    def emit(ic, st):
        sl = pl.ds(pl.multiple_of(ic * c, c), c)
        r, k, v, g = r_ref[0, sl, :], k_ref[0, sl, :], v_ref[0, sl, :], g_ref[0, sl, :]
        y_st = _mm(y1_ref[ic], st, _NT, P_ST) + y0_ref[ic]
        y = y_st[0:c] + y_st[c:2 * c]
        mean = seg_mean(y)
        var = seg_mean((y - mean) * (y - mean))
        yn = (y - mean) * lax.rsqrt(var + RWKV_GN_EPS) * gng_ref[...] + gnb_ref[...]
        bonus = seg_mean(r * k * rk_ref[...]) * float(HEAD_DIM) * v
        o_ref[0, sl, :] = ((yn + bonus) * g).astype(o_ref.dtype)
        return st * dec_ref[ic][0:1, :] + _mm(st, n_ref[ic], _NN, P_ST) + z_ref[ic]

    def scan(ics):
        st = st_ref[...]
        for ic in ics:
            st = emit(ic, st)
            yield
        st_ref[...] = st

    def run(*gens):
        live = list(gens)
        while live:
            live = [gen for gen in live if next(gen, live) is not live]

    group = lambda io: [io * RWKV_UNROLL + i for i in range(RWKV_UNROLL)]
    ngroup = nchunk // RWKV_UNROLL
    st_ref[...] = jnp.zeros_like(st_ref)
    run(build(group(0)))

    @pl.loop(1, ngroup)
    def _(io):
        run(build(group(io)), scan(group(io - 1)))

    run(scan(group(ngroup - 1)))


def rwkv_scan(r, k, v, kk, b, lw, g, gn_g, gn_b, r_k):
    bsz, s, _ = r.shape
    npair = RWKV_DIM // LANES
    nchunk = s // RWKV_CHUNK
    assert nchunk % RWKV_UNROLL == 0
    seq = pl.BlockSpec((1, s, LANES), lambda i, p: (i, 0, p))
    vec = pl.BlockSpec((1, LANES), lambda i, p: (0, p))
    row = lambda vv: vv.reshape(1, -1).astype(F32)
    mat = pltpu.VMEM((nchunk, LANES, LANES), F32)
    return pl.pallas_call(
        functools.partial(_rwkv_scan_kernel, nchunk=nchunk),
        grid=(bsz, npair),
        in_specs=[seq] * 7 + [vec] * 3,
        out_specs=seq,
        out_shape=jax.ShapeDtypeStruct((bsz, s, RWKV_DIM), BF16),
        scratch_shapes=[pltpu.VMEM((LANES, LANES), F32), mat, mat, mat, mat,
                        pltpu.VMEM((nchunk, 8, LANES), F32)],
        compiler_params=_params("parallel", "parallel"),
        name="rwkv_scan",
    )(r, k, v, kk, b, lw, g, row(gn_g), row(gn_b), row(r_k))


def _mla_prep_kernel(cq_ref, ckv_ref, pe_ref, cos_ref, sin_ref, e_ref, gq_ref, gkn_ref, gkp_ref, invn_ref,
                     gcq_ref, gckv_ref, wuq_ref, wukv_ref, qo_ref, ko_ref, vo_ref):
    cos, sin = cos_ref[0], sin_ref[0]
    half = MLA_ROPE // 2

    def rope(x, g):
        w = x.shape[1]
        xg = x * _tile_lanes(g, w)
        return xg * _tile_lanes(cos, w) + _swap_halves(xg, half) * _tile_lanes(sin, w)

    x = _dot(_rms(cq_ref[0], gcq_ref[...]).astype(BF16), wuq_ref[...])
    w = x.shape[1]
    inv_n = _tile_lanes(invn_ref[...], w)
    inv = lax.rsqrt(_seg_sum(x * x, e_ref[...]) * inv_n + EPS)
    qo_ref[0] = (rope(x, gq_ref[...]) * inv * (MLA_NOPE + MLA_ROPE) ** -0.5).astype(qo_ref.dtype)
    kv = _dot(_rms(ckv_ref[0], gckv_ref[...]).astype(BF16), wukv_ref[...])
    inv_k = lax.rsqrt(_seg_sum(kv * kv, e_ref[...]) * inv_n + EPS)
    k_nope = kv * inv_k * _tile_lanes(gkn_ref[...], w)
    pe = pe_ref[0]
    inv_pe = lax.rsqrt(jnp.sum(pe * pe, axis=-1, keepdims=True) * (1.0 / MLA_ROPE) + EPS)
    k_pe = rope(pe, gkp_ref[...]) * inv_pe
    ko_ref[0] = (k_nope + _tile_lanes(k_pe, w)).astype(ko_ref.dtype)
    v = jnp.concatenate([kv[:, h * LANES + MLA_NOPE:(h + 1) * LANES] for h in range(w // LANES)], axis=-1)
    _store_transposed_tiles(vo_ref, v)


def mla_prep(u, cos, sin, gq, gkn, gkp, gcq, gckv, w_uq, w_ukv, *, ts=512, tc=512):
    b, s, _ = u.shape
    wtot = w_uq.shape[1]
    ts = min(ts, s)
    lane = jnp.arange(tc)
    same = (lane[:, None] // LANES == lane[None, :] // LANES)
    pos = lane % LANES
    nope = pos < MLA_NOPE
    pe = (pos >= MLA_NOPE) & (pos < MLA_NOPE + MLA_ROPE)
    seg = (same & ((nope[:, None] & nope[None, :]) | (pe[:, None] & pe[None, :]))).astype(BF16)
    p1 = jnp.arange(LANES)
    inv_n = jnp.where(p1 < MLA_NOPE, 1.0 / MLA_NOPE, jnp.where(p1 < MLA_NOPE + MLA_ROPE, 1.0 / MLA_ROPE, 0.0))
    blk = pl.BlockSpec((1, ts, tc), lambda i, t, c: (i, t, c))
    tab = pl.BlockSpec((1, ts, LANES), lambda i, t, c: (i, t, 0))
    vec = pl.BlockSpec((1, LANES), lambda i, t, c: (0, 0))
    pe_blk = (MLA_Q_RANK + MLA_KV_RANK) // LANES
    out = jax.ShapeDtypeStruct((b, s, wtot), BF16)
    return pl.pallas_call(
        _mla_prep_kernel,
        grid=(b, s // ts, wtot // tc),
        in_specs=[pl.BlockSpec((1, ts, MLA_Q_RANK), lambda i, t, c: (i, t, 0)),
                  pl.BlockSpec((1, ts, MLA_KV_RANK), lambda i, t, c: (i, t, MLA_Q_RANK // MLA_KV_RANK)),
                  pl.BlockSpec((1, ts, LANES), lambda i, t, c: (i, t, pe_blk)), tab, tab,
                  pl.BlockSpec((tc, tc), lambda i, t, c: (0, 0)), vec, vec, vec, vec,
                  pl.BlockSpec((1, MLA_Q_RANK), lambda i, t, c: (0, 0)),
                  pl.BlockSpec((1, MLA_KV_RANK), lambda i, t, c: (0, 0)),
                  pl.BlockSpec((MLA_Q_RANK, tc), lambda i, t, c: (0, c)),
                  pl.BlockSpec((MLA_KV_RANK, tc), lambda i, t, c: (0, c))],
        out_specs=[blk, blk, pl.BlockSpec((1, ts // ATTN_TILE, tc // LANES * MLA_V, ATTN_TILE),
                                          lambda i, t, c: (i, t, c, 0))],
        out_shape=[out, out, jax.ShapeDtypeStruct((b, s // ATTN_TILE, wtot // LANES * MLA_V, ATTN_TILE), BF16)],
        compiler_params=_params("parallel", "parallel", "arbitrary"),
        name="mla_prep",
    )(u, u, u, cos, sin, seg, gq, gkn, gkp, inv_n.reshape(1, LANES).astype(F32),
      gcq.reshape(1, -1).astype(F32), gckv.reshape(1, -1).astype(F32), w_uq, w_ukv)


def _diff_prep_kernel(q_ref, k_ref, v_ref, cos_ref, sin_ref, e_ref, gq_ref, gk_ref, qo_ref, ko_ref, vo_ref):
    cos, sin = cos_ref[0], sin_ref[0]
    _store_transposed_tiles(vo_ref, v_ref[0])

    def prep(x, g, scale):
        w = x.shape[1]
        xg = x * _tile_lanes(g, w)
        xr = xg * _tile_lanes(cos, w) + _swap_halves(xg, DIFF_QK // 2) * _tile_lanes(sin, w)
        inv = lax.rsqrt(_seg_sum(x * x, e_ref[...]) * (1.0 / DIFF_QK) + EPS)
        return xr * inv * scale

    qo_ref[0] = prep(q_ref[0], gq_ref[...], DIFF_QK ** -0.5).astype(qo_ref.dtype)
    ko_ref[0] = prep(k_ref[0], gk_ref[...], 1.0).astype(ko_ref.dtype)


def diff_prep(u, cos, sin, gq, gk, *, ts=512, tc=512):
    b, s, _ = u.shape
    ts = min(ts, s)
    wtot = 2 * DIFF_HEADS * DIFF_QK
    q_base = (CD_IN_PAD - 3 * wtot) // tc
    seg = (jnp.arange(tc)[:, None] // DIFF_QK == jnp.arange(tc)[None, :] // DIFF_QK).astype(BF16)
    blk = lambda off: pl.BlockSpec((1, ts, tc), lambda i, t, c: (i, t, off + c))
    tab = pl.BlockSpec((1, ts, LANES), lambda i, t, c: (i, t, 0))
    vec = pl.BlockSpec((1, LANES), lambda i, t, c: (0, 0))
    out = jax.ShapeDtypeStruct((b, s, wtot), BF16)
    return pl.pallas_call(
        _diff_prep_kernel,
        grid=(b, s // ts, wtot // tc),
        in_specs=[blk(q_base), blk(q_base + wtot // tc), blk(q_base + 2 * wtot // tc), tab, tab,
                  pl.BlockSpec((tc, tc), lambda i, t, c: (0, 0)), vec, vec],
        out_specs=[blk(0), blk(0),
                   pl.BlockSpec((1, ts // ATTN_TILE, tc, ATTN_TILE), lambda i, t, c: (i, t, c, 0))],
        out_shape=[out, out, jax.ShapeDtypeStruct((b, s // ATTN_TILE, wtot, ATTN_TILE), BF16)],
        compiler_params=_params("parallel", "parallel", "arbitrary"),
        name="diff_prep",
    )(u, u, u, cos, sin, seg, gq, gk)


def _causal_attn_kernel(lam_ref, q_ref, k_ref, vt_ref, g_ref, o_ref, *, n_sm, tq, ow, out_scale):
    qi = pl.program_id(2)
    q = q_ref[0]
    hps = q.shape[1] // LANES
    slab = lambda x, h: x[:, h * LANES:(h + 1) * LANES]
    lane = lax.broadcasted_iota(jnp.int32, (tq, LANES), 1)
    qs, src = [], []
    for h in range(hps):
        qh = slab(q, h)
        if n_sm == 2:
            zero = jnp.zeros_like(qh)
            qs += [jnp.where(lane < DIFF_QK, qh, zero), jnp.where(lane < DIFF_QK, zero, qh)]
            src += [h, h]
        else:
            qs.append(qh)
            src.append(h)
    nch = len(qs)
    key_i = lax.broadcasted_iota(jnp.int32, (tq, tq), 0)
    qry_i = lax.broadcasted_iota(jnp.int32, (tq, tq), 1)

    def step(j, carry, diagonal):
        kj = k_ref[0, pl.ds(pl.multiple_of(j * tq, tq), tq), :]
        scores = lambda i: _dot_t(slab(kj, src[i]), qs[i])
        new = []
        ahead = [scores(i) for i in range(min(ATTN_LOOKAHEAD, nch))]
        for i in range(nch):
            s = ahead.pop(0)
            if i + ATTN_LOOKAHEAD < nch:
                ahead.append(scores(i + ATTN_LOOKAHEAD))
            if diagonal:
                s = jnp.where(key_i <= qry_i, s, NEG_INF)
            m, l, acc = carry[3 * i:3 * i + 3]
            m_new = jnp.maximum(m, jnp.max(s, axis=0, keepdims=True))
            alpha = jnp.exp(m - m_new)
            p = jnp.exp(s - m_new)
            new += [m_new, alpha * l + jnp.sum(p, axis=0, keepdims=True),
                    alpha * acc + _dot(vt_ref[0, j, src[i] * ow:(src[i] + 1) * ow, :], p.astype(BF16))]
        return tuple(new)

    init = (jnp.full((1, tq), NEG_INF, F32), jnp.zeros((1, tq), F32),
            jnp.zeros((ow, tq), F32)) * nch
    carry = lax.fori_loop(0, qi, lambda j, cr: step(j, cr, False), init)
    carry = step(qi, carry, True)
    outs = []
    for h in range(hps):
        c0 = 3 * n_sm * h
        o = carry[c0 + 2] / carry[c0 + 1]
        if n_sm == 2:
            o = o - lam_ref[0] * (carry[c0 + 5] / carry[c0 + 4])
            o = o * lax.rsqrt(jnp.mean(o * o, axis=0, keepdims=True) + EPS) * g_ref[...] * out_scale
        outs.append(o.T.astype(o_ref.dtype))
    o_ref[0] = jnp.concatenate(outs, axis=-1)


def causal_attention(q, k, vt, *, n_sm, lam=None, g=None, out_scale=1.0, hps=4, name="causal_attention"):
    b, s, wtot = q.shape
    tq = vt.shape[3]
    width = hps * LANES
    groups = wtot // width
    ow = vt.shape[2] // (wtot // LANES)
    lam = jnp.zeros((1,), F32) if lam is None else lam.reshape(1).astype(F32)
    g = jnp.ones((ow, 1), F32) if g is None else g.reshape(ow, 1).astype(F32)
    seq = pl.BlockSpec((1, s, width), lambda i, h, t: (i, 0, h))
    tile = pl.BlockSpec((1, tq, width), lambda i, h, t: (i, t, h))
    return pl.pallas_call(
        functools.partial(_causal_attn_kernel, n_sm=n_sm, tq=tq, ow=ow, out_scale=out_scale),
        grid=(b, groups, s // tq),
        in_specs=[pl.BlockSpec(memory_space=pltpu.SMEM), tile, seq,
                  pl.BlockSpec((1, s // tq, hps * ow, tq), lambda i, h, t: (i, 0, h, 0)),
                  pl.BlockSpec((ow, 1), lambda i, h, t: (0, 0))],
        out_specs=pl.BlockSpec((1, tq, hps * ow), lambda i, h, t: (i, t, h)),
        out_shape=jax.ShapeDtypeStruct((b, s, groups * hps * ow), BF16),
        compiler_params=_params("parallel", "parallel", "arbitrary"),
        name=name,
    )(lam, q, k, vt, g)


def _memx_kernel(x_ref, g_ref, wq_ref, kv_ref, gq_ref, gk_ref, wo_ref, o_ref):
    x = x_ref[0]
    q = _dot(_rms(x, g_ref[...]).astype(BF16), wq_ref[...])
    kv = kv_ref[0]
    outs = []
    for h in range(MEM_HEADS):
        sl = slice(h * MEM_HEAD_DIM, (h + 1) * MEM_HEAD_DIM)
        qh = (_rms(q[:, sl], gq_ref[...]) * MEM_HEAD_DIM ** -0.5).astype(BF16)
        kh = _rms(kv[:, sl], gk_ref[...]).astype(BF16)
        vh = kv[:, MEM_W + h * MEM_HEAD_DIM:MEM_W + (h + 1) * MEM_HEAD_DIM].astype(BF16)
        s = _dot_t(qh, kh)
        p = jnp.exp(s - jnp.max(s, axis=-1, keepdims=True))
        outs.append(_dot(p.astype(BF16), vh) / jnp.sum(p, axis=-1, keepdims=True))
    o_ref[0] = x + _dot(jnp.concatenate(outs, axis=-1).astype(BF16), wo_ref[...])


def mem_cross_attention(x, mem_kv, g, wq, gq, gk, wo, *, tm=512):
    b, s, d = x.shape
    m = mem_kv.shape[1]
    tm = min(tm, s)
    const = lambda shape: pl.BlockSpec(shape, lambda i, t: (0,) * len(shape))
    tile = pl.BlockSpec((1, tm, d), lambda i, t: (i, t, 0))
    return pl.pallas_call(
        _memx_kernel,
        grid=(b, s // tm),
        in_specs=[tile, const((1, d)), const((d, MEM_W)),
                  pl.BlockSpec((1, m, 2 * MEM_W), lambda i, t: (i, 0, 0)),
                  const((1, MEM_HEAD_DIM)), const((1, MEM_HEAD_DIM)), const((MEM_W, d))],
        out_specs=tile,
        out_shape=jax.ShapeDtypeStruct((b, s, d), F32),
        compiler_params=_params("parallel", "arbitrary"),
        name="mem_cross_attention",
    )(x, g.reshape(1, d).astype(F32), wq, mem_kv, gq.reshape(1, -1).astype(F32),
      gk.reshape(1, -1).astype(F32), wo)


def _rope_tables(positions, dim, lead_ones, tail):
    inv = 1.0 / (ROPE_THETA ** (jnp.arange(0, dim, 2, dtype=F32) / dim))
    ang = positions.astype(F32)[..., None] * inv
    c, s = jnp.cos(ang), jnp.sin(ang)
    shape = positions.shape
    cos = jnp.concatenate([jnp.ones(shape + (lead_ones,), F32), c, c, jnp.ones(shape + (tail,), F32)], axis=-1)
    sin = jnp.concatenate([jnp.zeros(shape + (lead_ones,), F32), -s, s, jnp.zeros(shape + (tail,), F32)], axis=-1)
    reps = LANES // cos.shape[-1]
    return jnp.tile(cos, (1, 1, reps)), jnp.tile(sin, (1, 1, reps))


def _pad_cols(w, cols):
    return jnp.pad(w, ((0, 0), (0, cols - w.shape[1])))


def _ab_in_layout(w):
    c = (SWA_HEADS + 2 * SWA_KV_HEADS) * HEAD_DIM + 3 * RWKV_DIM
    return jnp.concatenate([w[:, :c], _pad_cols(w[:, c:c + DECAY_LORA], LANES),
                            _pad_cols(w[:, c + DECAY_LORA:c + DECAY_LORA + AAA_LORA], LANES),
                            _pad_cols(w[:, c + DECAY_LORA + AAA_LORA:], 2 * LANES)], axis=1)


def _cd_in_layout(w):
    c1 = MLA_Q_RANK + MLA_KV_RANK
    z = lambda n: jnp.zeros((w.shape[0], n), w.dtype)
    return jnp.concatenate([w[:, :c1], z(MLA_NOPE), w[:, c1:c1 + MLA_ROPE], z(LANES - MLA_NOPE - MLA_ROPE),
                            z(LANES), w[:, c1 + MLA_ROPE:]], axis=1)


def _head_slabs(w, per_head):
    k = w.shape[0]
    return jnp.pad(w.reshape(k, -1, per_head), ((0, 0), (0, 0), (0, LANES - per_head))).reshape(k, -1)


def _slab_vec(*parts):
    v = jnp.concatenate([p.astype(F32) for p in parts])
    return jnp.pad(v, (0, LANES - v.shape[0])).reshape(1, LANES)


def kernel(x, mem, positions, ffn1_norm, ffn1_w_gate, ffn1_w_up, ffn1_w_down, mix_norm, ab_w_in, ab_w_out, swa_q_norm, swa_k_norm, swa_sinks, rwkv_mu, rwkv_w0, rwkv_w2, rwkv_a0, rwkv_a2, rwkv_g2, rwkv_k_k, rwkv_k_a, rwkv_r_k, rwkv_gn_g, rwkv_gn_b, cd_w_in, cd_w_out, mla_cq_norm, mla_ckv_norm, mla_w_uq, mla_w_ukv, mla_q_nope_norm, mla_k_nope_norm, mla_q_rope_norm, mla_k_rope_norm, diff_q_norm, diff_k_norm, diff_lq1, diff_lk1, diff_lq2, diff_lk2, diff_subln, memx_norm, memx_w_q, memx_q_norm, memx_w_o, mem_norm, mem_w_kv, mem_k_norm, ffn2_norm, ffn2_w_gate, ffn2_w_up, ffn2_w_down):
    b, s, d = x.shape
    m = mem.shape[1]
    t = b * s
    depth = ffn1_norm.shape[0]
    bf = lambda w: w.astype(BF16)
    cos64, sin64 = _rope_tables(positions, HEAD_DIM, 0, 0)
    cos32, sin32 = _rope_tables(positions, MLA_ROPE, MLA_NOPE, LANES - MLA_NOPE - MLA_ROPE)

    mem_kv = norm_matmul(mem.reshape(b * m, d), mem_norm, bf(mem_w_kv), name="mem_kv").reshape(b, m, 2 * MEM_W)

    x = x.reshape(t, d)
    for layer in range(depth):
        j = layer // 2
        x = ffn(x, ffn1_norm[layer], ffn1_w_gate, ffn1_w_up, ffn1_w_down, layer, name="ffn1")
        if layer % 2 == 0:
            u = norm_matmul(x, mix_norm[layer], bf(_ab_in_layout(ab_w_in[j])), name="ab_in")
            u = u.reshape(b, s, AB_IN_PAD)
            y_a = swa_attention(u, cos64, sin64, _slab_vec(swa_q_norm[j], swa_q_norm[j]),
                                _slab_vec(swa_k_norm[j], swa_k_norm[j]), swa_sinks[j])
            prep = rwkv_prep(u, rwkv_mu[j], rwkv_w0[j], rwkv_w2[j], rwkv_a0[j], rwkv_a2[j], rwkv_g2[j],
                             rwkv_k_k[j], rwkv_k_a[j])
            y_b = rwkv_scan(*prep, rwkv_gn_g[j], rwkv_gn_b[j], rwkv_r_k[j])
            w_out = bf(ab_w_out[j])
            half = SWA_HEADS * HEAD_DIM
            x = matmul_residual(x, [y_a.reshape(t, -1), y_b.reshape(t, -1)], [w_out[:half], w_out[half:]],
                                name="ab_out")
        else:
            u = norm_matmul(x, mix_norm[layer], bf(_cd_in_layout(cd_w_in[j])), name="cd_in")
            u = u.reshape(b, s, CD_IN_PAD)
            zero64 = jnp.zeros((MLA_NOPE,), F32)
            q_c, k_c, vt_c = mla_prep(u, cos32, sin32,
                                      _slab_vec(mla_q_nope_norm[j], mla_q_rope_norm[j]),
                                      _slab_vec(mla_k_nope_norm[j]),
                                      _slab_vec(zero64, mla_k_rope_norm[j]),
                                      mla_cq_norm[j], mla_ckv_norm[j],
                                      bf(_head_slabs(mla_w_uq[j], MLA_NOPE + MLA_ROPE)), bf(mla_w_ukv[j]))
            y_c = causal_attention(q_c, k_c, vt_c, n_sm=1, hps=8, name="mla_attention")
            q_d, k_d, vt_d = diff_prep(u, cos64, sin64, _slab_vec(diff_q_norm[j], diff_q_norm[j]),
                                       _slab_vec(diff_k_norm[j], diff_k_norm[j]))
            lambda_init = 0.8 - 0.6 * math.exp(-0.3 * layer)
            lam = (jnp.exp(jnp.sum(diff_lq1[j].astype(F32) * diff_lk1[j].astype(F32)))
                   - jnp.exp(jnp.sum(diff_lq2[j].astype(F32) * diff_lk2[j].astype(F32))) + lambda_init)
            y_d = causal_attention(q_d, k_d, vt_d, n_sm=2, hps=4, lam=lam, g=diff_subln[j],
                                   out_scale=1.0 - lambda_init, name="diff_attention")
            w_out = bf(cd_w_out[j])
            n_c = MLA_HEADS * MLA_V
            x = matmul_residual(x, [y_c.reshape(t, -1), y_d.reshape(t, -1)], [w_out[:n_c], w_out[n_c:]],
                                name="cd_out")
        x = mem_cross_attention(x.reshape(b, s, d), mem_kv, memx_norm[layer], bf(memx_w_q[layer]),
                                memx_q_norm[layer], mem_k_norm, bf(memx_w_o[layer])).reshape(t, d)
        x = ffn(x, ffn2_norm[layer], ffn2_w_gate, ffn2_w_up, ffn2_w_down, layer, name="ffn2")
    return x.reshape(b, s, d)
```

```python
import functools
import math

import jax
import jax.numpy as jnp
from jax import lax
from jax.experimental import pallas as pl
from jax.experimental.pallas import tpu as pltpu

F32 = jnp.float32
BF16 = jnp.bfloat16

EPS = 1e-6
ROPE_THETA = 10000.0
NEG_INF = -1e30
ATTN_LOOKAHEAD = 4
ATTN_TILE = 256
LANES = 128

D_MODEL = 2048
D_FF = 5632
HEAD_DIM = 64
SWA_HEADS = 16
SWA_KV_HEADS = 4
SWA_BLOCK = 128
RWKV_DIM = 1024
RWKV_CHUNK = 64
RWKV_GN_EPS = 64e-5
DECAY_LORA, AAA_LORA, GATE_LORA = 64, 64, 160
MLA_HEADS, MLA_Q_RANK, MLA_KV_RANK, MLA_NOPE, MLA_ROPE, MLA_V = 16, 512, 256, 64, 32, 64
DIFF_HEADS, DIFF_QK, DIFF_V = 8, 64, 128
MEM_HEADS, MEM_HEAD_DIM = 4, 128
MEM_W = MEM_HEADS * MEM_HEAD_DIM
AB_IN_PAD = 5120
CD_IN_PAD = 4096

VMEM_LIMIT = 48 * 1024 * 1024
FFN_VMEM_LIMIT = 56 * 1024 * 1024


def _params(*sem):
    return pltpu.CompilerParams(dimension_semantics=sem, vmem_limit_bytes=VMEM_LIMIT)


def _dot(a, b):
    return jnp.dot(a, b, preferred_element_type=F32)


def _dot_t(a, b):
    return lax.dot_general(a, b, (((1,), (1,)), ((), ())), preferred_element_type=F32)


def _dot_0(a, b):
    return lax.dot_general(a, b, (((0,), (0,)), ((), ())), preferred_element_type=F32)


def _rms(x, g):
    return x * lax.rsqrt(jnp.mean(x * x, axis=-1, keepdims=True) + EPS) * g


def _seg_sum(x, e):
    hi = x.astype(BF16)
    lo = (x - hi.astype(F32)).astype(BF16)
    return _dot(hi, e) + _dot(lo, e)


def _tile_lanes(v, width):
    return v if v.shape[-1] == width else jnp.tile(v, (1, width // v.shape[-1]))


def _store_transposed_tiles(vt_ref, v):
    for r in range(vt_ref.shape[1]):
        vt_ref[0, r] = v[r * ATTN_TILE:(r + 1) * ATTN_TILE, :].T.astype(vt_ref.dtype)


def _swap_halves(x, half):
    w = x.shape[-1]
    lane = lax.broadcasted_iota(jnp.int32, x.shape, x.ndim - 1)
    low = (lane & (2 * half - 1)) < half
    return jnp.where(low, pltpu.roll(x, w - half, x.ndim - 1), pltpu.roll(x, half, x.ndim - 1))


def _norm_matmul_kernel(x_ref, g_ref, w_ref, o_ref, xn_ref):
    @pl.when(pl.program_id(1) == 0)
    def _():
        xn_ref[...] = _rms(x_ref[...].astype(F32), g_ref[...]).astype(BF16)

    o_ref[...] = _dot(xn_ref[...], w_ref[...]).astype(o_ref.dtype)


def norm_matmul(x, g, w, *, k_blk=0, out_dtype=F32, tm=1024, tn=512, name="norm_matmul"):
    t = x.shape[0]
    k, n = w.shape
    tm, tn = min(tm, t), min(tn, n)
    assert t % tm == 0 and n % tn == 0
    return pl.pallas_call(
        _norm_matmul_kernel,
        grid=(t // tm, n // tn),
        in_specs=[pl.BlockSpec((tm, k), lambda i, j: (i, k_blk)),
                  pl.BlockSpec((1, k), lambda i, j: (0, 0)),
                  pl.BlockSpec((k, tn), lambda i, j: (0, j))],
        out_specs=pl.BlockSpec((tm, tn), lambda i, j: (i, j)),
        out_shape=jax.ShapeDtypeStruct((t, n), out_dtype),
        scratch_shapes=[pltpu.VMEM((tm, k), BF16)],
        compiler_params=_params("parallel", "arbitrary"),
        name=name,
    )(x, g.reshape(1, k).astype(F32), w)


def _matmul_res_kernel(*refs, n_in):
    x_ref, o_ref = refs[2 * n_in], refs[2 * n_in + 1]
    acc = x_ref[...]
    for i in range(n_in):
        acc = acc + _dot(refs[i][...], refs[n_in + i][...])
    o_ref[...] = acc


def matmul_residual(x, a_list, w_list, *, tm=256, name="matmul_residual"):
    t, n = x.shape
    tm = min(tm, t)
    assert t % tm == 0
    n_in = len(a_list)
    in_specs = ([pl.BlockSpec((tm, a.shape[1]), lambda i: (i, 0)) for a in a_list]
                + [pl.BlockSpec(w.shape, lambda i: (0, 0)) for w in w_list]
                + [pl.BlockSpec((tm, n), lambda i: (i, 0))])
    return pl.pallas_call(
        functools.partial(_matmul_res_kernel, n_in=n_in),
        grid=(t // tm,),
        in_specs=in_specs,
        out_specs=pl.BlockSpec((tm, n), lambda i: (i, 0)),
        out_shape=jax.ShapeDtypeStruct((t, n), F32),
        compiler_params=_params("parallel"),
        name=name,
    )(*a_list, *w_list, x)


def _ffn_kernel(x_ref, g_ref, wg_ref, wu_ref, wd_ref, o_ref, xn_ref, *, nf):
    f = pl.program_id(1)

    @pl.when(f == 0)
    def _():
        x = x_ref[...]
        xn_ref[...] = _rms(x, g_ref[...]).astype(BF16)
        o_ref[...] = 2.0 * x

    xn = xn_ref[...]
    a = _dot(xn, wg_ref[...].astype(BF16))
    b = _dot(xn, wu_ref[...].astype(BF16))
    h = (a * (1.0 / (1.0 + jnp.exp(-a))) * b).astype(BF16)
    o_ref[...] += _dot(h, wd_ref[...].astype(BF16))

    @pl.when(f == nf - 1)
    def _():
        o_ref[...] = 0.5 * o_ref[...]


def ffn(x, g, wg, wu, wd, layer, *, tm=1024, tf=256, name="ffn"):
    t, d = x.shape
    ff = wg.shape[2]
    tm = min(tm, t)
    assert t % tm == 0 and ff % tf == 0
    nf = ff // tf
    return pl.pallas_call(
        functools.partial(_ffn_kernel, nf=nf),
        grid=(t // tm, nf),
        in_specs=[pl.BlockSpec((tm, d), lambda i, f: (i, 0), pipeline_mode=pl.Buffered(1)),
                  pl.BlockSpec((1, d), lambda i, f: (0, 0)),
                  pl.BlockSpec((None, d, tf), lambda i, f: (layer, 0, f)),
                  pl.BlockSpec((None, d, tf), lambda i, f: (layer, 0, f)),
                  pl.BlockSpec((None, tf, d), lambda i, f: (layer, f, 0))],
        out_specs=pl.BlockSpec((tm, d), lambda i, f: (i, 0)),
        out_shape=jax.ShapeDtypeStruct((t, d), F32),
        scratch_shapes=[pltpu.VMEM((tm, d), BF16)],
        compiler_params=pltpu.CompilerParams(dimension_semantics=("parallel", "arbitrary"),
                                             vmem_limit_bytes=FFN_VMEM_LIMIT),
        name=name,
    )(x, g.reshape(1, d).astype(F32), wg, wu, wd)


def _swa_kernel(sink_ref, q_ref, kc_ref, kp_ref, vc_ref, vp_ref, cc_ref, sc_ref, cp_ref, sp_ref,
                gq_ref, gk_ref, e_ref, o_ref):
    n = pl.program_id(1)
    blk = SWA_BLOCK
    group = SWA_HEADS // SWA_KV_HEADS
    q = q_ref[0]
    k = jnp.concatenate([kp_ref[0], kc_ref[0]], axis=0)
    v = jnp.concatenate([vp_ref[0], vc_ref[0]], axis=0)
    cos_q, sin_q = cc_ref[0], sc_ref[0]
    cos_k = jnp.concatenate([cp_ref[0], cos_q], axis=0)
    sin_k = jnp.concatenate([sp_ref[0], sin_q], axis=0)
    seg = e_ref[...]
    sw = seg.shape[0]

    def prep(x, g, cos, sin, scale):
        w = x.shape[1]
        xg = x * _tile_lanes(g, w)
        xr = xg * _tile_lanes(cos, w) + _swap_halves(xg, HEAD_DIM // 2) * _tile_lanes(sin, w)
        ss = jnp.concatenate([_seg_sum(x[:, i:i + sw] * x[:, i:i + sw], seg) for i in range(0, w, sw)], axis=-1)
        return xr * lax.rsqrt(ss * (1.0 / HEAD_DIM) + EPS) * scale

    qr = prep(q, gq_ref[...], cos_q, sin_q, HEAD_DIM ** -0.5).astype(BF16)
    kr = prep(k, gk_ref[...], cos_k, sin_k, 1.0)
    key_i = lax.broadcasted_iota(jnp.int32, (2 * blk, blk), 0)
    qry_i = lax.broadcasted_iota(jnp.int32, (2 * blk, blk), 1)
    rel = qry_i + blk - key_i
    valid = (rel >= 0) & (rel < blk) & ((n > 0) | (key_i >= blk))
    low = lax.broadcasted_iota(jnp.int32, (blk, LANES), 1) < HEAD_DIM
    zero = jnp.zeros((blk, LANES), BF16)
    k_dup, v_t = [], []
    for g in range(SWA_KV_HEADS):
        kg = kr[:, g * HEAD_DIM:(g + 1) * HEAD_DIM]
        k_dup.append(jnp.concatenate([kg, kg], axis=-1).astype(BF16))
        v_t.append(v[:, g * HEAD_DIM:(g + 1) * HEAD_DIM].T.astype(BF16))

    def scores(h):
        slab = qr[:, (h // 2) * LANES:(h // 2 + 1) * LANES]
        qh = jnp.where(low, slab, zero) if h % 2 == 0 else jnp.where(low, zero, slab)
        return _dot_t(k_dup[h // group], qh)

    ahead = [scores(h) for h in range(ATTN_LOOKAHEAD)]
    outs = []
    for h in range(SWA_HEADS):
        s = jnp.where(valid, ahead.pop(0), NEG_INF)
        if h + ATTN_LOOKAHEAD < SWA_HEADS:
            ahead.append(scores(h + ATTN_LOOKAHEAD))
        sink = sink_ref[h]
        m = jnp.maximum(jnp.max(s, axis=0, keepdims=True), sink)
        p = jnp.exp(s - m)
        den = jnp.sum(p, axis=0, keepdims=True) + jnp.exp(sink - m)
        outs.append(_dot(v_t[h // group], p.astype(BF16)) / den)
    slabs = [jnp.concatenate(outs[i:i + 2], axis=0).T for i in range(0, SWA_HEADS, 2)]
    o_ref[0] = jnp.concatenate(slabs, axis=-1).astype(o_ref.dtype)


def swa_attention(u, cos, sin, gq, gk, sinks):
    b, s, _ = u.shape
    nb = s // SWA_BLOCK
    qw, kw = SWA_HEADS * HEAD_DIM, SWA_KV_HEADS * HEAD_DIM
    cur = lambda c: (lambda i, n: (i, n, c))
    prev = lambda c: (lambda i, n: (i, jnp.maximum(n - 1, 0), c))
    tab = pl.BlockSpec((1, SWA_BLOCK, LANES), cur(0))
    tab_prev = pl.BlockSpec((1, SWA_BLOCK, LANES), prev(0))
    gain = pl.BlockSpec((1, LANES), lambda i, n: (0, 0))
    seg = (jnp.arange(kw)[:, None] // HEAD_DIM == jnp.arange(kw)[None, :] // HEAD_DIM).astype(BF16)
    return pl.pallas_call(
        _swa_kernel,
        grid=(b, nb),
        in_specs=[pl.BlockSpec(memory_space=pltpu.SMEM),
                  pl.BlockSpec((1, SWA_BLOCK, qw), cur(0)),
                  pl.BlockSpec((1, SWA_BLOCK, kw), cur(qw // kw)),
                  pl.BlockSpec((1, SWA_BLOCK, kw), prev(qw // kw)),
                  pl.BlockSpec((1, SWA_BLOCK, kw), cur(qw // kw + 1)),
                  pl.BlockSpec((1, SWA_BLOCK, kw), prev(qw // kw + 1)),
                  tab, tab, tab_prev, tab_prev, gain, gain, pl.BlockSpec((kw, kw), lambda i, n: (0, 0))],
        out_specs=pl.BlockSpec((1, SWA_BLOCK, qw), cur(0)),
        out_shape=jax.ShapeDtypeStruct((b, s, qw), BF16),
        compiler_params=_params("parallel", "arbitrary"),
        name="swa_attention",
    )(sinks.astype(F32), u, u, u, u, u, cos, sin, cos, sin, gq, gk, seg)


def _mm(a, b, dims, passes):
    dn = (dims, ((), ()))
    dg = lambda x, y: lax.dot_general(x, y, dn, preferred_element_type=F32)
    ah = a.astype(BF16)
    bh = b.astype(BF16)
    if passes == 1:
        return dg(ah, bh)
    al = (a - ah.astype(F32)).astype(BF16)
    bl = (b - bh.astype(F32)).astype(BF16)
    return dg(ah, bh) + dg(ah, bl) + dg(al, bh)


_NN = ((1,), (0,))
_NT = ((1,), (1,))
_TN = ((0,), (0,))
P_SC, P_INV, P_PQ, P_OUT, P_ST = 1, 1, 1, 1, 1
RWKV_UNROLL = 8


RWKV_PREP_ROWS = 256


def _rwkv_kernel(ur_ref, uk_ref, uv_ref, ul_ref, mur_ref, muk_ref, muv_ref, mul_ref, w0_ref, w2_ref, a0_ref,
                 a2_ref, g2_ref, kkw_ref, ka_ref, gng_ref, gnb_ref, rk_ref,
                 o_ref, r_ref, k_ref, v_ref, kk_ref, b_ref, lw_ref, g_ref,
                 st_ref, y1_ref, y0_ref, n_ref, z_ref, dec_ref, *, nchunk):
    c = RWKV_CHUNK
    lane_c = lax.broadcasted_iota(jnp.int32, (c, LANES), 1)
    head0 = lane_c < HEAD_DIM
    ri = lax.broadcasted_iota(jnp.int32, (2 * c, 2 * c), 0)
    ci = lax.broadcasted_iota(jnp.int32, (2 * c, 2 * c), 1)
    eye = jnp.where(ri == ci, 1.0, 0.0)
    tril_c = jnp.where(lax.broadcasted_iota(jnp.int32, (c, c), 0) >= lax.broadcasted_iota(jnp.int32, (c, c), 1),
                       1.0, 0.0).astype(BF16)
    stack = lambda x: jnp.concatenate([jnp.where(head0, x, 0.0), jnp.where(head0, 0.0, x)], axis=0)

    def seg_mean(x):
        first = lax.broadcasted_iota(jnp.int32, x.shape, 1) < HEAD_DIM
        m0 = jnp.sum(jnp.where(first, x, 0.0), axis=-1, keepdims=True)
        m1 = jnp.sum(jnp.where(first, 0.0, x), axis=-1, keepdims=True)
        return jnp.where(first, m0, m1) * (1.0 / HEAD_DIM)

    def prep(io, first_group):
        group_rows = RWKV_UNROLL * c
        for tix in range(group_rows // RWKV_PREP_ROWS):
            start = pl.multiple_of(io * group_rows + tix * RWKV_PREP_ROWS, RWKV_PREP_ROWS)
            rows = pl.ds(start, RWKV_PREP_ROWS)
            at_start = first_group and tix == 0

            def shifted(ref, mu_ref):
                x = ref[0, rows, :]
                if at_start:
                    last = jnp.zeros((1, x.shape[1]), F32)
                else:
                    last = ref[0, pl.ds(pl.multiple_of(start - 8, 8), 8), :][7:8, :]
                row = lax.broadcasted_iota(jnp.int32, x.shape, 0)
                prev = jnp.where(row == 0, last, pltpu.roll(x, 1, 0))
                return x + (prev - x) * mu_ref[...]

            r = shifted(ur_ref, mur_ref)
            k = shifted(uk_ref, muk_ref)
            v = shifted(uv_ref, muv_ref)
            lo = shifted(ul_ref, mul_ref)
            yield
            w_lo, a_lo, g_lo = lo[:, 0:LANES], lo[:, LANES:2 * LANES], lo[:, 2 * LANES:4 * LANES]
            z = -(w0_ref[...] + _mm(jnp.tanh(w_lo), w2_ref[...], _NN, 3))
            w = -(jnp.maximum(z, 0.0) + jnp.log(1.0 + jnp.exp(-jnp.abs(z)))) - 0.5
            a = 1.0 / (1.0 + jnp.exp(-(a0_ref[...] + _mm(a_lo, a2_ref[...], _NN, 1))))
            g = _mm(1.0 / (1.0 + jnp.exp(-g_lo)), g2_ref[...], _NN, 1)
            yield
            kk = k * kkw_ref[...]
            kk = kk / jnp.maximum(jnp.sqrt(seg_mean(kk * kk) * float(HEAD_DIM)), 1e-12)
            r_ref[rows, :] = r
            k_ref[rows, :] = k * (1.0 + (a - 1.0) * ka_ref[...])
            v_ref[rows, :] = v
            kk_ref[rows, :] = kk
            b_ref[rows, :] = kk * a
            lw_ref[rows, :] = -jnp.exp(w)
            g_ref[rows, :] = g
            yield

    def build(ics):
        each = lambda f, *cols: [f(*args) for args in zip(*cols)]
        sls = [pl.ds(pl.multiple_of(ic * c, c), c) for ic in ics]
        load = lambda ref: [ref[sl, :] for sl in sls]
        r, k, v, kk, b, lw = (load(ref) for ref in (r_ref, k_ref, v_ref, kk_ref, b_ref, lw_ref))

        def running_sum(x):
            l1 = x.astype(BF16)
            rest = x - l1.astype(F32)
            l2 = rest.astype(BF16)
            l3 = (rest - l2.astype(F32)).astype(BF16)
            return _dot(tril_c, l1) + _dot(tril_c, l2) + _dot(tril_c, l3)

        cum = each(running_sum, lw)
        yield
        cum_end = [x[c - 1:c, :] for x in cum]
        e_neg = each(lambda x: jnp.exp(-x), cum)
        e_end = each(lambda x, xe: jnp.exp(xe - x), cum, cum_end)
        a_s = each(lambda kk_, x, l: stack(-kk_ * jnp.exp(x - l)), kk, cum, lw)
        r_s = each(lambda r_, x: stack(r_ * jnp.exp(x)), r, cum)
        b_s = each(lambda b_, e: stack(b_ * e), b, e_neg)
        k_s = each(lambda k_, e: stack(k_ * e), k, e_neg)
        bh_s = each(lambda b_, e: stack(b_ * e), b, e_end)
        kh_s = each(lambda k_, e: stack(k_ * e), k, e_end)
        v_s = each(stack, v)
        n2 = 2 * c
        sc = each(lambda a_, r_, b_, k_: _mm(jnp.concatenate([a_, r_], axis=0),
                                             jnp.concatenate([b_, k_], axis=0), _NT, P_SC), a_s, r_s, b_s, k_s)
        low = [jnp.where(ri > ci, x[:n2, :n2], 0.0) for x in sc]
        a_ak = [jnp.where(ri > ci, x[:n2, n2:], 0.0) for x in sc]
        a_rb = [jnp.where(ri >= ci, x[n2:, :n2], 0.0) for x in sc]
        a_rk = [jnp.where(ri >= ci, x[n2:, n2:], 0.0) for x in sc]
        yield
        inv = [eye + x for x in low]
        pw = low
        for _ in range(5):
            pw = each(lambda x: _mm(x, x, _NN, P_INV), pw)
            inv = each(lambda t, x: t + _mm(t, x, _NN, P_INV), inv, pw)
            yield
        akv = each(lambda x, y: _mm(x, y, _NN, P_PQ), a_ak, v_s)
        yield
        pq = each(lambda t, x, y: _mm(t, jnp.concatenate([x, y], axis=1), _NN, P_PQ), inv, a_s, akv)
        yield
        yy = each(lambda x, y: _mm(x, y, _NN, P_OUT), a_rb, pq)
        y0b = each(lambda x, y: _mm(x, y, _NN, P_OUT), a_rk, v_s)
        nz = each(lambda x, y: _mm(x, y, _TN, P_OUT), pq, bh_s)
        zb = each(lambda x, y: _mm(x, y, _TN, P_OUT), v_s, kh_s)
        for i, ic in enumerate(ics):
            y1_ref[ic] = r_s[i] + yy[i][:, :LANES]
            y0_ref[ic] = yy[i][:, LANES:] + y0b[i]
            n_ref[ic] = nz[i][:LANES]
            z_ref[ic] = nz[i][LANES:] + zb[i]
            dec_ref[ic] = jnp.broadcast_to(jnp.exp(cum_end[i]), (8, LANES))

    def emit(ic, st):
        sl = pl.ds(pl.multiple_of(ic * c, c), c)
        r, k, v, g = r_ref[sl, :], k_ref[sl, :], v_ref[sl, :], g_ref[sl, :]
        y_st = _mm(y1_ref[ic], st, _NT, P_ST) + y0_ref[ic]
        y = y_st[0:c] + y_st[c:2 * c]
        mean = seg_mean(y)
        var = seg_mean((y - mean) * (y - mean))
        yn = (y - mean) * lax.rsqrt(var + RWKV_GN_EPS) * gng_ref[...] + gnb_ref[...]
        bonus = seg_mean(r * k * rk_ref[...]) * float(HEAD_DIM) * v
        o_ref[0, sl, :] = ((yn + bonus) * g).astype(o_ref.dtype)
        return st * dec_ref[ic][0:1, :] + _mm(st, n_ref[ic], _NN, P_ST) + z_ref[ic]

    def scan(ics):
        st = st_ref[...]
        for ic in ics:
            st = emit(ic, st)
            yield
        st_ref[...] = st

    def run(*gens):
        live = list(gens)
        while live:
            live = [gen for gen in live if next(gen, live) is not live]

    group = lambda io: [io * RWKV_UNROLL + i for i in range(RWKV_UNROLL)]
    ngroup = nchunk // RWKV_UNROLL
    st_ref[...] = jnp.zeros_like(st_ref)
    run(prep(0, True))
    if ngroup > 1:
        run(prep(1, False), build(group(0)))
    else:
        run(build(group(0)))

    @pl.loop(1, ngroup - 1)
    def _(io):
        run(prep(io + 1, False), build(group(io)), scan(group(io - 1)))

    if ngroup > 1:
        run(build(group(ngroup - 1)), scan(group(ngroup - 2)))
    run(scan(group(ngroup - 1)))


def rwkv_mix(u, mu, w0, w2, a0, a2, g2, k_k, k_a, r_k, gn_g, gn_b):
    bsz, s, _ = u.shape
    npair = RWKV_DIM // LANES
    nchunk = s // RWKV_CHUNK
    assert nchunk % RWKV_UNROLL == 0 and (RWKV_UNROLL * RWKV_CHUNK) % RWKV_PREP_ROWS == 0
    base = (SWA_HEADS + 2 * SWA_KV_HEADS) * HEAD_DIM // LANES
    lora_w = 4 * LANES
    slab = lambda off: pl.BlockSpec((1, s, LANES), lambda i, p: (i, 0, off + p))
    vec = pl.BlockSpec((1, LANES), lambda i, p: (0, p))
    cols = lambda rows: pl.BlockSpec((rows, LANES), lambda i, p: (0, p))
    row = lambda vv: vv.reshape(1, -1).astype(F32)
    pad_rows = lambda m, rows: jnp.pad(m, ((0, rows - m.shape[0]), (0, 0))).astype(F32)
    pad_cols = lambda vv, n: jnp.pad(vv, (0, n - vv.shape[0]))
    c3 = 3 * RWKV_DIM
    mu_l = jnp.concatenate([pad_cols(mu[c3:c3 + DECAY_LORA], LANES),
                            pad_cols(mu[c3 + DECAY_LORA:c3 + DECAY_LORA + AAA_LORA], LANES),
                            pad_cols(mu[c3 + DECAY_LORA + AAA_LORA:], 2 * LANES)])
    seq = pltpu.VMEM((s, LANES), F32)
    mat = pltpu.VMEM((nchunk, LANES, LANES), F32)
    return pl.pallas_call(
        functools.partial(_rwkv_kernel, nchunk=nchunk),
        grid=(bsz, npair),
        in_specs=[slab(base), slab(base + npair), slab(base + 2 * npair),
                  pl.BlockSpec((1, s, lora_w), lambda i, p: (i, 0, (base + 3 * npair) * LANES // lora_w)),
                  vec, vec, vec, pl.BlockSpec((1, lora_w), lambda i, p: (0, 0)),
                  vec, cols(LANES), vec, cols(LANES), cols(2 * LANES), vec, vec, vec, vec, vec],
        out_specs=pl.BlockSpec((1, s, LANES), lambda i, p: (i, 0, p)),
        out_shape=jax.ShapeDtypeStruct((bsz, s, RWKV_DIM), BF16),
        scratch_shapes=[seq] * 7 + [pltpu.VMEM((LANES, LANES), F32), mat, mat, mat, mat,
                                    pltpu.VMEM((nchunk, 8, LANES), F32)],
        compiler_params=_params("parallel", "arbitrary"),
        name="rwkv_mix",
    )(u, u, u, u, row(mu[:RWKV_DIM]), row(mu[RWKV_DIM:2 * RWKV_DIM]), row(mu[2 * RWKV_DIM:c3]), row(mu_l),
      row(w0), pad_rows(w2, LANES), row(a0), pad_rows(a2, LANES), pad_rows(g2, 2 * LANES),
      row(k_k), row(k_a), row(gn_g), row(gn_b), row(r_k))


def _mla_prep_kernel(cq_ref, ckv_ref, pe_ref, cos_ref, sin_ref, e_ref, gq_ref, gkn_ref, gkp_ref, invn_ref,
                     gcq_ref, gckv_ref, wuq_ref, wukv_ref, qo_ref, ko_ref, vo_ref):
    cos, sin = cos_ref[0], sin_ref[0]
    half = MLA_ROPE // 2

    def rope(x, g):
        w = x.shape[1]
        xg = x * _tile_lanes(g, w)
        return xg * _tile_lanes(cos, w) + _swap_halves(xg, half) * _tile_lanes(sin, w)

    x = _dot(_rms(cq_ref[0], gcq_ref[...]).astype(BF16), wuq_ref[...])
    w = x.shape[1]
    inv_n = _tile_lanes(invn_ref[...], w)
    inv = lax.rsqrt(_seg_sum(x * x, e_ref[...]) * inv_n + EPS)
    qo_ref[0] = (rope(x, gq_ref[...]) * inv * (MLA_NOPE + MLA_ROPE) ** -0.5).astype(qo_ref.dtype)
    kv = _dot(_rms(ckv_ref[0], gckv_ref[...]).astype(BF16), wukv_ref[...])
    inv_k = lax.rsqrt(_seg_sum(kv * kv, e_ref[...]) * inv_n + EPS)
    k_nope = kv * inv_k * _tile_lanes(gkn_ref[...], w)
    pe = pe_ref[0]
    inv_pe = lax.rsqrt(jnp.sum(pe * pe, axis=-1, keepdims=True) * (1.0 / MLA_ROPE) + EPS)
    k_pe = rope(pe, gkp_ref[...]) * inv_pe
    ko_ref[0] = (k_nope + _tile_lanes(k_pe, w)).astype(ko_ref.dtype)
    v = jnp.concatenate([kv[:, h * LANES + MLA_NOPE:(h + 1) * LANES] for h in range(w // LANES)], axis=-1)
    _store_transposed_tiles(vo_ref, v)


def mla_prep(u, cos, sin, gq, gkn, gkp, gcq, gckv, w_uq, w_ukv, *, ts=512, tc=512):
    b, s, _ = u.shape
    wtot = w_uq.shape[1]
    ts = min(ts, s)
    lane = jnp.arange(tc)
    same = (lane[:, None] // LANES == lane[None, :] // LANES)
    pos = lane % LANES
    nope = pos < MLA_NOPE
    pe = (pos >= MLA_NOPE) & (pos < MLA_NOPE + MLA_ROPE)
    seg = (same & ((nope[:, None] & nope[None, :]) | (pe[:, None] & pe[None, :]))).astype(BF16)
    p1 = jnp.arange(LANES)
    inv_n = jnp.where(p1 < MLA_NOPE, 1.0 / MLA_NOPE, jnp.where(p1 < MLA_NOPE + MLA_ROPE, 1.0 / MLA_ROPE, 0.0))
    blk = pl.BlockSpec((1, ts, tc), lambda i, t, c: (i, t, c))
    tab = pl.BlockSpec((1, ts, LANES), lambda i, t, c: (i, t, 0))
    vec = pl.BlockSpec((1, LANES), lambda i, t, c: (0, 0))
    pe_blk = (MLA_Q_RANK + MLA_KV_RANK) // LANES
    out = jax.ShapeDtypeStruct((b, s, wtot), BF16)
    return pl.pallas_call(
        _mla_prep_kernel,
        grid=(b, s // ts, wtot // tc),
        in_specs=[pl.BlockSpec((1, ts, MLA_Q_RANK), lambda i, t, c: (i, t, 0)),
                  pl.BlockSpec((1, ts, MLA_KV_RANK), lambda i, t, c: (i, t, MLA_Q_RANK // MLA_KV_RANK)),
                  pl.BlockSpec((1, ts, LANES), lambda i, t, c: (i, t, pe_blk)), tab, tab,
                  pl.BlockSpec((tc, tc), lambda i, t, c: (0, 0)), vec, vec, vec, vec,
                  pl.BlockSpec((1, MLA_Q_RANK), lambda i, t, c: (0, 0)),
                  pl.BlockSpec((1, MLA_KV_RANK), lambda i, t, c: (0, 0)),
                  pl.BlockSpec((MLA_Q_RANK, tc), lambda i, t, c: (0, c)),
                  pl.BlockSpec((MLA_KV_RANK, tc), lambda i, t, c: (0, c))],
        out_specs=[blk, blk, pl.BlockSpec((1, ts // ATTN_TILE, tc // LANES * MLA_V, ATTN_TILE),
                                          lambda i, t, c: (i, t, c, 0))],
        out_shape=[out, out, jax.ShapeDtypeStruct((b, s // ATTN_TILE, wtot // LANES * MLA_V, ATTN_TILE), BF16)],
        compiler_params=_params("parallel", "parallel", "arbitrary"),
        name="mla_prep",
    )(u, u, u, cos, sin, seg, gq, gkn, gkp, inv_n.reshape(1, LANES).astype(F32),
      gcq.reshape(1, -1).astype(F32), gckv.reshape(1, -1).astype(F32), w_uq, w_ukv)


def _diff_prep_kernel(q_ref, k_ref, v_ref, cos_ref, sin_ref, e_ref, gq_ref, gk_ref, qo_ref, ko_ref, vo_ref):
    cos, sin = cos_ref[0], sin_ref[0]
    _store_transposed_tiles(vo_ref, v_ref[0])

    def prep(x, g, scale):
        w = x.shape[1]
        xg = x * _tile_lanes(g, w)
        xr = xg * _tile_lanes(cos, w) + _swap_halves(xg, DIFF_QK // 2) * _tile_lanes(sin, w)
        inv = lax.rsqrt(_seg_sum(x * x, e_ref[...]) * (1.0 / DIFF_QK) + EPS)
        return xr * inv * scale

    qo_ref[0] = prep(q_ref[0], gq_ref[...], DIFF_QK ** -0.5).astype(qo_ref.dtype)
    ko_ref[0] = prep(k_ref[0], gk_ref[...], 1.0).astype(ko_ref.dtype)


def diff_prep(u, cos, sin, gq, gk, *, ts=512, tc=512):
    b, s, _ = u.shape
    ts = min(ts, s)
    wtot = 2 * DIFF_HEADS * DIFF_QK
    q_base = (CD_IN_PAD - 3 * wtot) // tc
    seg = (jnp.arange(tc)[:, None] // DIFF_QK == jnp.arange(tc)[None, :] // DIFF_QK).astype(BF16)
    blk = lambda off: pl.BlockSpec((1, ts, tc), lambda i, t, c: (i, t, off + c))
    tab = pl.BlockSpec((1, ts, LANES), lambda i, t, c: (i, t, 0))
    vec = pl.BlockSpec((1, LANES), lambda i, t, c: (0, 0))
    out = jax.ShapeDtypeStruct((b, s, wtot), BF16)
    return pl.pallas_call(
        _diff_prep_kernel,
        grid=(b, s // ts, wtot // tc),
        in_specs=[blk(q_base), blk(q_base + wtot // tc), blk(q_base + 2 * wtot // tc), tab, tab,
                  pl.BlockSpec((tc, tc), lambda i, t, c: (0, 0)), vec, vec],
        out_specs=[blk(0), blk(0),
                   pl.BlockSpec((1, ts // ATTN_TILE, tc, ATTN_TILE), lambda i, t, c: (i, t, c, 0))],
        out_shape=[out, out, jax.ShapeDtypeStruct((b, s // ATTN_TILE, wtot, ATTN_TILE), BF16)],
        compiler_params=_params("parallel", "parallel", "arbitrary"),
        name="diff_prep",
    )(u, u, u, cos, sin, seg, gq, gk)


def _causal_attn_kernel(lam_ref, q_ref, k_ref, vt_ref, g_ref, o_ref, *, n_sm, tq, ow, out_scale):
    qi = pl.program_id(2)
    q = q_ref[0]
    hps = q.shape[1] // LANES
    slab = lambda x, h: x[:, h * LANES:(h + 1) * LANES]
    lane = lax.broadcasted_iota(jnp.int32, (tq, LANES), 1)
    qs, src = [], []
    for h in range(hps):
        qh = slab(q, h)
        if n_sm == 2:
            zero = jnp.zeros_like(qh)
            qs += [jnp.where(lane < DIFF_QK, qh, zero), jnp.where(lane < DIFF_QK, zero, qh)]
            src += [h, h]
        else:
            qs.append(qh)
            src.append(h)
    nch = len(qs)
    key_i = lax.broadcasted_iota(jnp.int32, (tq, tq), 0)
    qry_i = lax.broadcasted_iota(jnp.int32, (tq, tq), 1)

    def step(j, carry, diagonal):
        kj = k_ref[0, pl.ds(pl.multiple_of(j * tq, tq), tq), :]
        scores = lambda i: _dot_t(slab(kj, src[i]), qs[i])
        new = []
        ahead = [scores(i) for i in range(min(ATTN_LOOKAHEAD, nch))]
        for i in range(nch):
            s = ahead.pop(0)
            if i + ATTN_LOOKAHEAD < nch:
                ahead.append(scores(i + ATTN_LOOKAHEAD))
            if diagonal:
                s = jnp.where(key_i <= qry_i, s, NEG_INF)
            m, l, acc = carry[3 * i:3 * i + 3]
            m_new = jnp.maximum(m, jnp.max(s, axis=0, keepdims=True))
            alpha = jnp.exp(m - m_new)
            p = jnp.exp(s - m_new)
            new += [m_new, alpha * l + jnp.sum(p, axis=0, keepdims=True),
                    alpha * acc + _dot(vt_ref[0, j, src[i] * ow:(src[i] + 1) * ow, :], p.astype(BF16))]
        return tuple(new)

    init = (jnp.full((1, tq), NEG_INF, F32), jnp.zeros((1, tq), F32),
            jnp.zeros((ow, tq), F32)) * nch
    carry = lax.fori_loop(0, qi, lambda j, cr: step(j, cr, False), init)
    carry = step(qi, carry, True)
    outs = []
    for h in range(hps):
        c0 = 3 * n_sm * h
        o = carry[c0 + 2] / carry[c0 + 1]
        if n_sm == 2:
            o = o - lam_ref[0] * (carry[c0 + 5] / carry[c0 + 4])
            o = o * lax.rsqrt(jnp.mean(o * o, axis=0, keepdims=True) + EPS) * g_ref[...] * out_scale
        outs.append(o.T.astype(o_ref.dtype))
    o_ref[0] = jnp.concatenate(outs, axis=-1)


def causal_attention(q, k, vt, *, n_sm, lam=None, g=None, out_scale=1.0, hps=4, name="causal_attention"):
    b, s, wtot = q.shape
    tq = vt.shape[3]
    width = hps * LANES
    groups = wtot // width
    ow = vt.shape[2] // (wtot // LANES)
    lam = jnp.zeros((1,), F32) if lam is None else lam.reshape(1).astype(F32)
    g = jnp.ones((ow, 1), F32) if g is None else g.reshape(ow, 1).astype(F32)
    seq = pl.BlockSpec((1, s, width), lambda i, h, t: (i, 0, h))
    tile = pl.BlockSpec((1, tq, width), lambda i, h, t: (i, t, h))
    return pl.pallas_call(
        functools.partial(_causal_attn_kernel, n_sm=n_sm, tq=tq, ow=ow, out_scale=out_scale),
        grid=(b, groups, s // tq),
        in_specs=[pl.BlockSpec(memory_space=pltpu.SMEM), tile, seq,
                  pl.BlockSpec((1, s // tq, hps * ow, tq), lambda i, h, t: (i, 0, h, 0)),
                  pl.BlockSpec((ow, 1), lambda i, h, t: (0, 0))],
        out_specs=pl.BlockSpec((1, tq, hps * ow), lambda i, h, t: (i, t, h)),
        out_shape=jax.ShapeDtypeStruct((b, s, groups * hps * ow), BF16),
        compiler_params=_params("parallel", "parallel", "arbitrary"),
        name=name,
    )(lam, q, k, vt, g)


def _memx_kernel(x_ref, g_ref, wq_ref, kv_ref, gq_ref, gk_ref, wo_ref, o_ref):
    x = x_ref[0]
    q = _dot(_rms(x, g_ref[...]).astype(BF16), wq_ref[...])
    kv = kv_ref[0]
    outs = []
    for h in range(MEM_HEADS):
        sl = slice(h * MEM_HEAD_DIM, (h + 1) * MEM_HEAD_DIM)
        qh = (_rms(q[:, sl], gq_ref[...]) * MEM_HEAD_DIM ** -0.5).astype(BF16)
        kh = _rms(kv[:, sl], gk_ref[...]).astype(BF16)
        vh = kv[:, MEM_W + h * MEM_HEAD_DIM:MEM_W + (h + 1) * MEM_HEAD_DIM].astype(BF16)
        s = _dot_t(qh, kh)
        p = jnp.exp(s - jnp.max(s, axis=-1, keepdims=True))
        outs.append(_dot(p.astype(BF16), vh) / jnp.sum(p, axis=-1, keepdims=True))
    o_ref[0] = x + _dot(jnp.concatenate(outs, axis=-1).astype(BF16), wo_ref[...])


def mem_cross_attention(x, mem_kv, g, wq, gq, gk, wo, *, tm=512):
    b, s, d = x.shape
    m = mem_kv.shape[1]
    tm = min(tm, s)
    const = lambda shape: pl.BlockSpec(shape, lambda i, t: (0,) * len(shape))
    tile = pl.BlockSpec((1, tm, d), lambda i, t: (i, t, 0))
    return pl.pallas_call(
        _memx_kernel,
        grid=(b, s // tm),
        in_specs=[tile, const((1, d)), const((d, MEM_W)),
                  pl.BlockSpec((1, m, 2 * MEM_W), lambda i, t: (i, 0, 0)),
                  const((1, MEM_HEAD_DIM)), const((1, MEM_HEAD_DIM)), const((MEM_W, d))],
        out_specs=tile,
        out_shape=jax.ShapeDtypeStruct((b, s, d), F32),
        compiler_params=_params("parallel", "arbitrary"),
        name="mem_cross_attention",
    )(x, g.reshape(1, d).astype(F32), wq, mem_kv, gq.reshape(1, -1).astype(F32),
      gk.reshape(1, -1).astype(F32), wo)


def _rope_tables(positions, dim, lead_ones, tail):
    inv = 1.0 / (ROPE_THETA ** (jnp.arange(0, dim, 2, dtype=F32) / dim))
    ang = positions.astype(F32)[..., None] * inv
    c, s = jnp.cos(ang), jnp.sin(ang)
    shape = positions.shape
    cos = jnp.concatenate([jnp.ones(shape + (lead_ones,), F32), c, c, jnp.ones(shape + (tail,), F32)], axis=-1)
    sin = jnp.concatenate([jnp.zeros(shape + (lead_ones,), F32), -s, s, jnp.zeros(shape + (tail,), F32)], axis=-1)
    reps = LANES // cos.shape[-1]
    return jnp.tile(cos, (1, 1, reps)), jnp.tile(sin, (1, 1, reps))


def _pad_cols(w, cols):
    return jnp.pad(w, ((0, 0), (0, cols - w.shape[1])))


def _ab_in_layout(w):
    c = (SWA_HEADS + 2 * SWA_KV_HEADS) * HEAD_DIM + 3 * RWKV_DIM
    return jnp.concatenate([w[:, :c], _pad_cols(w[:, c:c + DECAY_LORA], LANES),
                            _pad_cols(w[:, c + DECAY_LORA:c + DECAY_LORA + AAA_LORA], LANES),
                            _pad_cols(w[:, c + DECAY_LORA + AAA_LORA:], 2 * LANES)], axis=1)


def _cd_in_layout(w):
    c1 = MLA_Q_RANK + MLA_KV_RANK
    z = lambda n: jnp.zeros((w.shape[0], n), w.dtype)
    return jnp.concatenate([w[:, :c1], z(MLA_NOPE), w[:, c1:c1 + MLA_ROPE], z(LANES - MLA_NOPE - MLA_ROPE),
                            z(LANES), w[:, c1 + MLA_ROPE:]], axis=1)


def _head_slabs(w, per_head):
    k = w.shape[0]
    return jnp.pad(w.reshape(k, -1, per_head), ((0, 0), (0, 0), (0, LANES - per_head))).reshape(k, -1)


def _slab_vec(*parts):
    v = jnp.concatenate([p.astype(F32) for p in parts])
    return jnp.pad(v, (0, LANES - v.shape[0])).reshape(1, LANES)


def kernel(x, mem, positions, ffn1_norm, ffn1_w_gate, ffn1_w_up, ffn1_w_down, mix_norm, ab_w_in, ab_w_out, swa_q_norm, swa_k_norm, swa_sinks, rwkv_mu, rwkv_w0, rwkv_w2, rwkv_a0, rwkv_a2, rwkv_g2, rwkv_k_k, rwkv_k_a, rwkv_r_k, rwkv_gn_g, rwkv_gn_b, cd_w_in, cd_w_out, mla_cq_norm, mla_ckv_norm, mla_w_uq, mla_w_ukv, mla_q_nope_norm, mla_k_nope_norm, mla_q_rope_norm, mla_k_rope_norm, diff_q_norm, diff_k_norm, diff_lq1, diff_lk1, diff_lq2, diff_lk2, diff_subln, memx_norm, memx_w_q, memx_q_norm, memx_w_o, mem_norm, mem_w_kv, mem_k_norm, ffn2_norm, ffn2_w_gate, ffn2_w_up, ffn2_w_down):
    b, s, d = x.shape
    m = mem.shape[1]
    t = b * s
    depth = ffn1_norm.shape[0]
    bf = lambda w: w.astype(BF16)
    cos64, sin64 = _rope_tables(positions, HEAD_DIM, 0, 0)
    cos32, sin32 = _rope_tables(positions, MLA_ROPE, MLA_NOPE, LANES - MLA_NOPE - MLA_ROPE)

    mem_kv = norm_matmul(mem.reshape(b * m, d), mem_norm, bf(mem_w_kv), name="mem_kv").reshape(b, m, 2 * MEM_W)

    x = x.reshape(t, d)
    for layer in range(depth):
        j = layer // 2
        x = ffn(x, ffn1_norm[layer], ffn1_w_gate, ffn1_w_up, ffn1_w_down, layer, name="ffn1")
        if layer % 2 == 0:
            u = norm_matmul(x, mix_norm[layer], bf(_ab_in_layout(ab_w_in[j])), name="ab_in")
            u = u.reshape(b, s, AB_IN_PAD)
            y_a = swa_attention(u, cos64, sin64, _slab_vec(swa_q_norm[j], swa_q_norm[j]),
                                _slab_vec(swa_k_norm[j], swa_k_norm[j]), swa_sinks[j])
            y_b = rwkv_mix(u, rwkv_mu[j], rwkv_w0[j], rwkv_w2[j], rwkv_a0[j], rwkv_a2[j], rwkv_g2[j],
                           rwkv_k_k[j], rwkv_k_a[j], rwkv_r_k[j], rwkv_gn_g[j], rwkv_gn_b[j])
            w_out = bf(ab_w_out[j])
            half = SWA_HEADS * HEAD_DIM
            x = matmul_residual(x, [y_a.reshape(t, -1), y_b.reshape(t, -1)], [w_out[:half], w_out[half:]],
                                name="ab_out")
        else:
            u = norm_matmul(x, mix_norm[layer], bf(_cd_in_layout(cd_w_in[j])), name="cd_in")
            u = u.reshape(b, s, CD_IN_PAD)
            zero64 = jnp.zeros((MLA_NOPE,), F32)
            q_c, k_c, vt_c = mla_prep(u, cos32, sin32,
                                      _slab_vec(mla_q_nope_norm[j], mla_q_rope_norm[j]),
                                      _slab_vec(mla_k_nope_norm[j]),
                                      _slab_vec(zero64, mla_k_rope_norm[j]),
                                      mla_cq_norm[j], mla_ckv_norm[j],
                                      bf(_head_slabs(mla_w_uq[j], MLA_NOPE + MLA_ROPE)), bf(mla_w_ukv[j]))
            y_c = causal_attention(q_c, k_c, vt_c, n_sm=1, hps=8, name="mla_attention")
            q_d, k_d, vt_d = diff_prep(u, cos64, sin64, _slab_vec(diff_q_norm[j], diff_q_norm[j]),
                                       _slab_vec(diff_k_norm[j], diff_k_norm[j]))
            lambda_init = 0.8 - 0.6 * math.exp(-0.3 * layer)
            lam = (jnp.exp(jnp.sum(diff_lq1[j].astype(F32) * diff_lk1[j].astype(F32)))
                   - jnp.exp(jnp.sum(diff_lq2[j].astype(F32) * diff_lk2[j].astype(F32))) + lambda_init)
            y_d = causal_attention(q_d, k_d, vt_d, n_sm=2, hps=4, lam=lam, g=diff_subln[j],
                                   out_scale=1.0 - lambda_init, name="diff_attention")
            w_out = bf(cd_w_out[j])
            n_c = MLA_HEADS * MLA_V
            x = matmul_residual(x, [y_c.reshape(t, -1), y_d.reshape(t, -1)], [w_out[:n_c], w_out[n_c:]],
                                name="cd_out")
        x = mem_cross_attention(x.reshape(b, s, d), mem_kv, memx_norm[layer], bf(memx_w_q[layer]),
                                memx_q_norm[layer], mem_k_norm, bf(memx_w_o[layer])).reshape(t, d)
        x = ffn(x, ffn2_norm[layer], ffn2_w_gate, ffn2_w_up, ffn2_w_down, layer, name="ffn2")
    return x.reshape(b, s, d)
```

```python
import functools
import math

import jax
import jax.numpy as jnp
from jax import lax
from jax.experimental import pallas as pl
from jax.experimental.pallas import tpu as pltpu

F32 = jnp.float32
BF16 = jnp.bfloat16

EPS = 1e-6
ROPE_THETA = 10000.0
NEG_INF = -1e30
ATTN_LOOKAHEAD = 8
ATTN_TILE = 256
LANES = 128

D_MODEL = 2048
D_FF = 5632
HEAD_DIM = 64
SWA_HEADS = 16
SWA_KV_HEADS = 4
SWA_BLOCK = 128
RWKV_DIM = 1024
RWKV_CHUNK = 64
RWKV_GN_EPS = 64e-5
DECAY_LORA, AAA_LORA, GATE_LORA = 64, 64, 160
MLA_HEADS, MLA_Q_RANK, MLA_KV_RANK, MLA_NOPE, MLA_ROPE, MLA_V = 16, 512, 256, 64, 32, 64
DIFF_HEADS, DIFF_QK, DIFF_V = 8, 64, 128
MEM_HEADS, MEM_HEAD_DIM = 4, 128
MEM_W = MEM_HEADS * MEM_HEAD_DIM
AB_IN_PAD = 5120
CD_IN_PAD = 4096

VMEM_LIMIT = 48 * 1024 * 1024
FFN_VMEM_LIMIT = 60 * 1024 * 1024


def _params(*sem):
    return pltpu.CompilerParams(dimension_semantics=sem, vmem_limit_bytes=VMEM_LIMIT)


def _dot(a, b):
    return jnp.dot(a, b, preferred_element_type=F32)


def _dot_t(a, b):
    return lax.dot_general(a, b, (((1,), (1,)), ((), ())), preferred_element_type=F32)


def _dot_0(a, b):
    return lax.dot_general(a, b, (((0,), (0,)), ((), ())), preferred_element_type=F32)


def _rms(x, g):
    return x * lax.rsqrt(jnp.mean(x * x, axis=-1, keepdims=True) + EPS) * g


def _seg_sum(x, e):
    hi = x.astype(BF16)
    lo = (x - hi.astype(F32)).astype(BF16)
    return _dot(hi, e) + _dot(lo, e)


def _tile_lanes(v, width):
    return v if v.shape[-1] == width else jnp.tile(v, (1, width // v.shape[-1]))


def _store_transposed_tiles(vt_ref, v):
    for r in range(vt_ref.shape[1]):
        vt_ref[0, r] = v[r * ATTN_TILE:(r + 1) * ATTN_TILE, :].T.astype(vt_ref.dtype)


def _swap_halves(x, half):
    w = x.shape[-1]
    lane = lax.broadcasted_iota(jnp.int32, x.shape, x.ndim - 1)
    low = (lane & (2 * half - 1)) < half
    return jnp.where(low, pltpu.roll(x, w - half, x.ndim - 1), pltpu.roll(x, half, x.ndim - 1))


def _norm_matmul_kernel(x_ref, g_ref, w_ref, o_ref, xn_ref):
    @pl.when(pl.program_id(1) == 0)
    def _():
        xn_ref[...] = _rms(x_ref[...].astype(F32), g_ref[...]).astype(BF16)

    o_ref[...] = _dot(xn_ref[...], w_ref[...]).astype(o_ref.dtype)


def norm_matmul(x, g, w, *, k_blk=0, out_dtype=F32, tm=1024, tn=512, name="norm_matmul"):
    t = x.shape[0]
    k, n = w.shape
    tm, tn = min(tm, t), min(tn, n)
    assert t % tm == 0 and n % tn == 0
    return pl.pallas_call(
        _norm_matmul_kernel,
        grid=(t // tm, n // tn),
        in_specs=[pl.BlockSpec((tm, k), lambda i, j: (i, k_blk)),
                  pl.BlockSpec((1, k), lambda i, j: (0, 0)),
                  pl.BlockSpec((k, tn), lambda i, j: (0, j))],
        out_specs=pl.BlockSpec((tm, tn), lambda i, j: (i, j)),
        out_shape=jax.ShapeDtypeStruct((t, n), out_dtype),
        scratch_shapes=[pltpu.VMEM((tm, k), BF16)],
        compiler_params=_params("parallel", "arbitrary"),
        name=name,
    )(x, g.reshape(1, k).astype(F32), w)


def _matmul_res_kernel(*refs, n_in):
    x_ref, o_ref = refs[2 * n_in], refs[2 * n_in + 1]
    acc = x_ref[...]
    for i in range(n_in):
        acc = acc + _dot(refs[i][...], refs[n_in + i][...])
    o_ref[...] = acc


def matmul_residual(x, a_list, w_list, *, tm=256, name="matmul_residual"):
    t, n = x.shape
    tm = min(tm, t)
    assert t % tm == 0
    n_in = len(a_list)
    in_specs = ([pl.BlockSpec((tm, a.shape[1]), lambda i: (i, 0)) for a in a_list]
                + [pl.BlockSpec(w.shape, lambda i: (0, 0)) for w in w_list]
                + [pl.BlockSpec((tm, n), lambda i: (i, 0))])
    return pl.pallas_call(
        functools.partial(_matmul_res_kernel, n_in=n_in),
        grid=(t // tm,),
        in_specs=in_specs,
        out_specs=pl.BlockSpec((tm, n), lambda i: (i, 0)),
        out_shape=jax.ShapeDtypeStruct((t, n), F32),
        compiler_params=_params("parallel"),
        name=name,
    )(*a_list, *w_list, x)


def _ffn_kernel(x_ref, g_ref, wg_ref, wu_ref, wd_ref, o_ref, xn_ref, *, nf):
    f = pl.program_id(1)

    @pl.when(f == 0)
    def _():
        x = x_ref[...]
        xn_ref[...] = _rms(x, g_ref[...]).astype(BF16)
        o_ref[...] = 2.0 * x

    xn = xn_ref[...]
    a = _dot(xn, wg_ref[...].astype(BF16))
    b = _dot(xn, wu_ref[...].astype(BF16))
    h = (a * (1.0 / (1.0 + jnp.exp(-a))) * b).astype(BF16)
    o_ref[...] += _dot(h, wd_ref[...].astype(BF16))

    @pl.when(f == nf - 1)
    def _():
        o_ref[...] = 0.5 * o_ref[...]


def ffn(x, g, wg, wu, wd, layer, *, tm=1024, tf=256, name="ffn"):
    t, d = x.shape
    ff = wg.shape[2]
    tm = min(tm, t)
    assert t % tm == 0 and ff % tf == 0
    nf = ff // tf
    return pl.pallas_call(
        functools.partial(_ffn_kernel, nf=nf),
        grid=(t // tm, nf),
        in_specs=[pl.BlockSpec((tm, d), lambda i, f: (i, 0)),
                  pl.BlockSpec((1, d), lambda i, f: (0, 0)),
                  pl.BlockSpec((None, d, tf), lambda i, f: (layer, 0, f)),
                  pl.BlockSpec((None, d, tf), lambda i, f: (layer, 0, f)),
                  pl.BlockSpec((None, tf, d), lambda i, f: (layer, f, 0))],
        out_specs=pl.BlockSpec((tm, d), lambda i, f: (i, 0)),
        out_shape=jax.ShapeDtypeStruct((t, d), F32),
        scratch_shapes=[pltpu.VMEM((tm, d), BF16)],
        compiler_params=pltpu.CompilerParams(dimension_semantics=("parallel", "arbitrary"),
                                             vmem_limit_bytes=FFN_VMEM_LIMIT),
        name=name,
    )(x, g.reshape(1, d).astype(F32), wg, wu, wd)


def _swa_kernel(sink_ref, q_ref, kc_ref, kp_ref, vc_ref, vp_ref, cc_ref, sc_ref, cp_ref, sp_ref,
                gq_ref, gk_ref, e_ref, o_ref):
    n = pl.program_id(1)
    blk = SWA_BLOCK
    group = SWA_HEADS // SWA_KV_HEADS
    q = q_ref[0]
    k = jnp.concatenate([kp_ref[0], kc_ref[0]], axis=0)
    v = jnp.concatenate([vp_ref[0], vc_ref[0]], axis=0)
    cos_q, sin_q = cc_ref[0], sc_ref[0]
    cos_k = jnp.concatenate([cp_ref[0], cos_q], axis=0)
    sin_k = jnp.concatenate([sp_ref[0], sin_q], axis=0)
    seg = e_ref[...]
    sw = seg.shape[0]

    def prep(x, g, cos, sin, scale):
        w = x.shape[1]
        xg = x * _tile_lanes(g, w)
        xr = xg * _tile_lanes(cos, w) + _swap_halves(xg, HEAD_DIM // 2) * _tile_lanes(sin, w)
        ss = jnp.concatenate([_seg_sum(x[:, i:i + sw] * x[:, i:i + sw], seg) for i in range(0, w, sw)], axis=-1)
        return xr * lax.rsqrt(ss * (1.0 / HEAD_DIM) + EPS) * scale

    qr = prep(q, gq_ref[...], cos_q, sin_q, HEAD_DIM ** -0.5).astype(BF16)
    kr = prep(k, gk_ref[...], cos_k, sin_k, 1.0)
    key_i = lax.broadcasted_iota(jnp.int32, (2 * blk, blk), 0)
    qry_i = lax.broadcasted_iota(jnp.int32, (2 * blk, blk), 1)
    rel = qry_i + blk - key_i
    valid = (rel >= 0) & (rel < blk) & ((n > 0) | (key_i >= blk))
    low = lax.broadcasted_iota(jnp.int32, (blk, LANES), 1) < HEAD_DIM
    zero = jnp.zeros((blk, LANES), BF16)
    k_dup, v_t = [], []
    for g in range(SWA_KV_HEADS):
        kg = kr[:, g * HEAD_DIM:(g + 1) * HEAD_DIM]
        k_dup.append(jnp.concatenate([kg, kg], axis=-1).astype(BF16))
        v_t.append(v[:, g * HEAD_DIM:(g + 1) * HEAD_DIM].T.astype(BF16))

    def scores(h):
        slab = qr[:, (h // 2) * LANES:(h // 2 + 1) * LANES]
        qh = jnp.where(low, slab, zero) if h % 2 == 0 else jnp.where(low, zero, slab)
        return _dot_t(k_dup[h // group], qh)

    ahead = [scores(h) for h in range(ATTN_LOOKAHEAD)]
    outs = []
    for h in range(SWA_HEADS):
        s = jnp.where(valid, ahead.pop(0), NEG_INF)
        if h + ATTN_LOOKAHEAD < SWA_HEADS:
            ahead.append(scores(h + ATTN_LOOKAHEAD))
        sink = sink_ref[h]
        m = jnp.maximum(jnp.max(s, axis=0, keepdims=True), sink)
        p = jnp.exp(s - m)
        den = jnp.sum(p, axis=0, keepdims=True) + jnp.exp(sink - m)
        outs.append(_dot(v_t[h // group], p.astype(BF16)) / den)
    slabs = [jnp.concatenate(outs[i:i + 2], axis=0).T for i in range(0, SWA_HEADS, 2)]
    o_ref[0] = jnp.concatenate(slabs, axis=-1).astype(o_ref.dtype)


def swa_attention(u, cos, sin, gq, gk, sinks):
    b, s, _ = u.shape
    nb = s // SWA_BLOCK
    qw, kw = SWA_HEADS * HEAD_DIM, SWA_KV_HEADS * HEAD_DIM
    cur = lambda c: (lambda i, n: (i, n, c))
    prev = lambda c: (lambda i, n: (i, jnp.maximum(n - 1, 0), c))
    tab = pl.BlockSpec((1, SWA_BLOCK, LANES), cur(0))
    tab_prev = pl.BlockSpec((1, SWA_BLOCK, LANES), prev(0))
    gain = pl.BlockSpec((1, LANES), lambda i, n: (0, 0))
    seg = (jnp.arange(kw)[:, None] // HEAD_DIM == jnp.arange(kw)[None, :] // HEAD_DIM).astype(BF16)
    return pl.pallas_call(
        _swa_kernel,
        grid=(b, nb),
        in_specs=[pl.BlockSpec(memory_space=pltpu.SMEM),
                  pl.BlockSpec((1, SWA_BLOCK, qw), cur(0)),
                  pl.BlockSpec((1, SWA_BLOCK, kw), cur(qw // kw)),
                  pl.BlockSpec((1, SWA_BLOCK, kw), prev(qw // kw)),
                  pl.BlockSpec((1, SWA_BLOCK, kw), cur(qw // kw + 1)),
                  pl.BlockSpec((1, SWA_BLOCK, kw), prev(qw // kw + 1)),
                  tab, tab, tab_prev, tab_prev, gain, gain, pl.BlockSpec((kw, kw), lambda i, n: (0, 0))],
        out_specs=pl.BlockSpec((1, SWA_BLOCK, qw), cur(0)),
        out_shape=jax.ShapeDtypeStruct((b, s, qw), BF16),
        compiler_params=_params("parallel", "arbitrary"),
        name="swa_attention",
    )(sinks.astype(F32), u, u, u, u, u, cos, sin, cos, sin, gq, gk, seg)


def _mm(a, b, dims, passes):
    dn = (dims, ((), ()))
    dg = lambda x, y: lax.dot_general(x, y, dn, preferred_element_type=F32)
    ah = a.astype(BF16)
    bh = b.astype(BF16)
    if passes == 1:
        return dg(ah, bh)
    al = (a - ah.astype(F32)).astype(BF16)
    bl = (b - bh.astype(F32)).astype(BF16)
    return dg(ah, bh) + dg(ah, bl) + dg(al, bh)


_NN = ((1,), (0,))
_NT = ((1,), (1,))
_TN = ((0,), (0,))
P_SC, P_INV, P_PQ, P_OUT, P_ST = 1, 1, 1, 1, 1
RWKV_UNROLL = 8


RWKV_PREP_ROWS = 256


def _rwkv_kernel(ur_ref, uk_ref, uv_ref, ul_ref, mur_ref, muk_ref, muv_ref, mul_ref, w0_ref, w2_ref, a0_ref,
                 a2_ref, g2_ref, kkw_ref, ka_ref, gng_ref, gnb_ref, rk_ref,
                 o_ref, r_ref, k_ref, v_ref, kk_ref, b_ref, lw_ref, g_ref,
                 st_ref, y1_ref, y0_ref, n_ref, z_ref, dec_ref, *, nchunk):
    c = RWKV_CHUNK
    lane_c = lax.broadcasted_iota(jnp.int32, (c, LANES), 1)
    head0 = lane_c < HEAD_DIM
    ri = lax.broadcasted_iota(jnp.int32, (2 * c, 2 * c), 0)
    ci = lax.broadcasted_iota(jnp.int32, (2 * c, 2 * c), 1)
    eye = jnp.where(ri == ci, 1.0, 0.0)
    tril_c = jnp.where(lax.broadcasted_iota(jnp.int32, (c, c), 0) >= lax.broadcasted_iota(jnp.int32, (c, c), 1),
                       1.0, 0.0).astype(BF16)
    stack = lambda x: jnp.concatenate([jnp.where(head0, x, 0.0), jnp.where(head0, 0.0, x)], axis=0)

    def seg_mean(x):
        first = lax.broadcasted_iota(jnp.int32, x.shape, 1) < HEAD_DIM
        m0 = jnp.sum(jnp.where(first, x, 0.0), axis=-1, keepdims=True)
        m1 = jnp.sum(jnp.where(first, 0.0, x), axis=-1, keepdims=True)
        return jnp.where(first, m0, m1) * (1.0 / HEAD_DIM)

    def prep(io, first_group):
        group_rows = RWKV_UNROLL * c
        for tix in range(group_rows // RWKV_PREP_ROWS):
            start = pl.multiple_of(io * group_rows + tix * RWKV_PREP_ROWS, RWKV_PREP_ROWS)
            rows = pl.ds(start, RWKV_PREP_ROWS)
            at_start = first_group and tix == 0

            def shifted(ref, mu_ref):
                x = ref[0, rows, :]
                if at_start:
                    last = jnp.zeros((1, x.shape[1]), F32)
                else:
                    last = ref[0, pl.ds(pl.multiple_of(start - 8, 8), 8), :][7:8, :]
                row = lax.broadcasted_iota(jnp.int32, x.shape, 0)
                prev = jnp.where(row == 0, last, pltpu.roll(x, 1, 0))
                return x + (prev - x) * mu_ref[...]

            r = shifted(ur_ref, mur_ref)
            k = shifted(uk_ref, muk_ref)
            v = shifted(uv_ref, muv_ref)
            lo = shifted(ul_ref, mul_ref)
            yield
            w_lo, a_lo, g_lo = lo[:, 0:LANES], lo[:, LANES:2 * LANES], lo[:, 2 * LANES:4 * LANES]
            z = -(w0_ref[...] + _mm(jnp.tanh(w_lo), w2_ref[...], _NN, 3))
            w = -(jnp.maximum(z, 0.0) + jnp.log(1.0 + jnp.exp(-jnp.abs(z)))) - 0.5
            a = 1.0 / (1.0 + jnp.exp(-(a0_ref[...] + _mm(a_lo, a2_ref[...], _NN, 1))))
            g = _mm(1.0 / (1.0 + jnp.exp(-g_lo)), g2_ref[...], _NN, 1)
            yield
            kk = k * kkw_ref[...]
            kk = kk / jnp.maximum(jnp.sqrt(seg_mean(kk * kk) * float(HEAD_DIM)), 1e-12)
            r_ref[rows, :] = r
            k_ref[rows, :] = k * (1.0 + (a - 1.0) * ka_ref[...])
            v_ref[rows, :] = v
            kk_ref[rows, :] = kk
            b_ref[rows, :] = kk * a
            lw_ref[rows, :] = -jnp.exp(w)
            g_ref[rows, :] = g
            yield

    def build(ics):
        each = lambda f, *cols: [f(*args) for args in zip(*cols)]
        sls = [pl.ds(pl.multiple_of(ic * c, c), c) for ic in ics]
        load = lambda ref: [ref[sl, :] for sl in sls]
        r, k, v, kk, b, lw = (load(ref) for ref in (r_ref, k_ref, v_ref, kk_ref, b_ref, lw_ref))

        def running_sum(x):
            l1 = x.astype(BF16)
            rest = x - l1.astype(F32)
            l2 = rest.astype(BF16)
            l3 = (rest - l2.astype(F32)).astype(BF16)
            return _dot(tril_c, l1) + _dot(tril_c, l2) + _dot(tril_c, l3)

        cum = each(running_sum, lw)
        yield
        cum_end = [x[c - 1:c, :] for x in cum]
        e_neg = each(lambda x: jnp.exp(-x), cum)
        e_end = each(lambda x, xe: jnp.exp(xe - x), cum, cum_end)
        a_s = each(lambda kk_, x, l: stack(-kk_ * jnp.exp(x - l)), kk, cum, lw)
        r_s = each(lambda r_, x: stack(r_ * jnp.exp(x)), r, cum)
        b_s = each(lambda b_, e: stack(b_ * e), b, e_neg)
        k_s = each(lambda k_, e: stack(k_ * e), k, e_neg)
        bh_s = each(lambda b_, e: stack(b_ * e), b, e_end)
        kh_s = each(lambda k_, e: stack(k_ * e), k, e_end)
        v_s = each(stack, v)
        n2 = 2 * c
        sc = each(lambda a_, r_, b_, k_: _mm(jnp.concatenate([a_, r_], axis=0),
                                             jnp.concatenate([b_, k_], axis=0), _NT, P_SC), a_s, r_s, b_s, k_s)
        low = [jnp.where(ri > ci, x[:n2, :n2], 0.0) for x in sc]
        a_ak = [jnp.where(ri > ci, x[:n2, n2:], 0.0) for x in sc]
        a_rb = [jnp.where(ri >= ci, x[n2:, :n2], 0.0) for x in sc]
        a_rk = [jnp.where(ri >= ci, x[n2:, n2:], 0.0) for x in sc]
        yield
        inv = [eye + x for x in low]
        pw = low
        for _ in range(5):
            pw = each(lambda x: _mm(x, x, _NN, P_INV), pw)
            inv = each(lambda t, x: t + _mm(t, x, _NN, P_INV), inv, pw)
            yield
        akv = each(lambda x, y: _mm(x, y, _NN, P_PQ), a_ak, v_s)
        yield
        pq = each(lambda t, x, y: _mm(t, jnp.concatenate([x, y], axis=1), _NN, P_PQ), inv, a_s, akv)
        yield
        yy = each(lambda x, y: _mm(x, y, _NN, P_OUT), a_rb, pq)
        y0b = each(lambda x, y: _mm(x, y, _NN, P_OUT), a_rk, v_s)
        nz = each(lambda x, y: _mm(x, y, _TN, P_OUT), pq, bh_s)
        zb = each(lambda x, y: _mm(x, y, _TN, P_OUT), v_s, kh_s)
        for i, ic in enumerate(ics):
            y1_ref[ic] = r_s[i] + yy[i][:, :LANES]
            y0_ref[ic] = yy[i][:, LANES:] + y0b[i]
            n_ref[ic] = nz[i][:LANES]
            z_ref[ic] = nz[i][LANES:] + zb[i]
            dec_ref[ic] = jnp.broadcast_to(jnp.exp(cum_end[i]), (8, LANES))

    def emit(ic, st):
        sl = pl.ds(pl.multiple_of(ic * c, c), c)
        r, k, v, g = r_ref[sl, :], k_ref[sl, :], v_ref[sl, :], g_ref[sl, :]
        y_st = _mm(y1_ref[ic], st, _NT, P_ST) + y0_ref[ic]
        y = y_st[0:c] + y_st[c:2 * c]
        mean = seg_mean(y)
        var = seg_mean((y - mean) * (y - mean))
        yn = (y - mean) * lax.rsqrt(var + RWKV_GN_EPS) * gng_ref[...] + gnb_ref[...]
        bonus = seg_mean(r * k * rk_ref[...]) * float(HEAD_DIM) * v
        o_ref[0, sl, :] = ((yn + bonus) * g).astype(o_ref.dtype)
        return st * dec_ref[ic][0:1, :] + _mm(st, n_ref[ic], _NN, P_ST) + z_ref[ic]

    def scan(ics):
        st = st_ref[...]
        for ic in ics:
            st = emit(ic, st)
            yield
        st_ref[...] = st

    def run(*gens):
        live = list(gens)
        while live:
            live = [gen for gen in live if next(gen, live) is not live]

    group = lambda io: [io * RWKV_UNROLL + i for i in range(RWKV_UNROLL)]
    ngroup = nchunk // RWKV_UNROLL
    st_ref[...] = jnp.zeros_like(st_ref)
    run(prep(0, True))
    if ngroup > 1:
        run(prep(1, False), build(group(0)))
    else:
        run(build(group(0)))

    @pl.loop(1, ngroup - 1)
    def _(io):
        run(prep(io + 1, False), build(group(io)), scan(group(io - 1)))

    if ngroup > 1:
        run(build(group(ngroup - 1)), scan(group(ngroup - 2)))
    run(scan(group(ngroup - 1)))


def rwkv_mix(u, mu, w0, w2, a0, a2, g2, k_k, k_a, r_k, gn_g, gn_b):
    bsz, s, _ = u.shape
    npair = RWKV_DIM // LANES
    nchunk = s // RWKV_CHUNK
    assert nchunk % RWKV_UNROLL == 0 and (RWKV_UNROLL * RWKV_CHUNK) % RWKV_PREP_ROWS == 0
    base = (SWA_HEADS + 2 * SWA_KV_HEADS) * HEAD_DIM // LANES
    lora_w = 4 * LANES
    slab = lambda off: pl.BlockSpec((1, s, LANES), lambda i, p: (i, 0, off + p))
    vec = pl.BlockSpec((1, LANES), lambda i, p: (0, p))
    cols = lambda rows: pl.BlockSpec((rows, LANES), lambda i, p: (0, p))
    row = lambda vv: vv.reshape(1, -1).astype(F32)
    pad_rows = lambda m, rows: jnp.pad(m, ((0, rows - m.shape[0]), (0, 0))).astype(F32)
    pad_cols = lambda vv, n: jnp.pad(vv, (0, n - vv.shape[0]))
    c3 = 3 * RWKV_DIM
    mu_l = jnp.concatenate([pad_cols(mu[c3:c3 + DECAY_LORA], LANES),
                            pad_cols(mu[c3 + DECAY_LORA:c3 + DECAY_LORA + AAA_LORA], LANES),
                            pad_cols(mu[c3 + DECAY_LORA + AAA_LORA:], 2 * LANES)])
    seq = pltpu.VMEM((s, LANES), F32)
    mat = pltpu.VMEM((nchunk, LANES, LANES), F32)
    return pl.pallas_call(
        functools.partial(_rwkv_kernel, nchunk=nchunk),
        grid=(bsz, npair),
        in_specs=[slab(base), slab(base + npair), slab(base + 2 * npair),
                  pl.BlockSpec((1, s, lora_w), lambda i, p: (i, 0, (base + 3 * npair) * LANES // lora_w)),
                  vec, vec, vec, pl.BlockSpec((1, lora_w), lambda i, p: (0, 0)),
                  vec, cols(LANES), vec, cols(LANES), cols(2 * LANES), vec, vec, vec, vec, vec],
        out_specs=pl.BlockSpec((1, s, LANES), lambda i, p: (i, 0, p)),
        out_shape=jax.ShapeDtypeStruct((bsz, s, RWKV_DIM), BF16),
        scratch_shapes=[seq] * 7 + [pltpu.VMEM((LANES, LANES), F32), mat, mat, mat, mat,
                                    pltpu.VMEM((nchunk, 8, LANES), F32)],
        compiler_params=_params("parallel", "arbitrary"),
        name="rwkv_mix",
    )(u, u, u, u, row(mu[:RWKV_DIM]), row(mu[RWKV_DIM:2 * RWKV_DIM]), row(mu[2 * RWKV_DIM:c3]), row(mu_l),
      row(w0), pad_rows(w2, LANES), row(a0), pad_rows(a2, LANES), pad_rows(g2, 2 * LANES),
      row(k_k), row(k_a), row(gn_g), row(gn_b), row(r_k))


def _mla_prep_kernel(cq_ref, ckv_ref, pe_ref, cos_ref, sin_ref, e_ref, gq_ref, gkn_ref, gkp_ref, invn_ref,
                     gcq_ref, gckv_ref, wuq_ref, wukv_ref, qo_ref, ko_ref, vo_ref):
    cos, sin = cos_ref[0], sin_ref[0]
    half = MLA_ROPE // 2

    def rope(x, g):
        w = x.shape[1]
        xg = x * _tile_lanes(g, w)
        return xg * _tile_lanes(cos, w) + _swap_halves(xg, half) * _tile_lanes(sin, w)

    x = _dot(_rms(cq_ref[0], gcq_ref[...]).astype(BF16), wuq_ref[...])
    w = x.shape[1]
    inv_n = _tile_lanes(invn_ref[...], w)
    inv = lax.rsqrt(_seg_sum(x * x, e_ref[...]) * inv_n + EPS)
    qo_ref[0] = (rope(x, gq_ref[...]) * inv * (MLA_NOPE + MLA_ROPE) ** -0.5).astype(qo_ref.dtype)
    kv = _dot(_rms(ckv_ref[0], gckv_ref[...]).astype(BF16), wukv_ref[...])
    inv_k = lax.rsqrt(_seg_sum(kv * kv, e_ref[...]) * inv_n + EPS)
    k_nope = kv * inv_k * _tile_lanes(gkn_ref[...], w)
    pe = pe_ref[0]
    inv_pe = lax.rsqrt(jnp.sum(pe * pe, axis=-1, keepdims=True) * (1.0 / MLA_ROPE) + EPS)
    k_pe = rope(pe, gkp_ref[...]) * inv_pe
    ko_ref[0] = (k_nope + _tile_lanes(k_pe, w)).astype(ko_ref.dtype)
    v = jnp.concatenate([kv[:, h * LANES + MLA_NOPE:(h + 1) * LANES] for h in range(w // LANES)], axis=-1)
    _store_transposed_tiles(vo_ref, v)


def mla_prep(u, cos, sin, gq, gkn, gkp, gcq, gckv, w_uq, w_ukv, *, ts=512, tc=512):
    b, s, _ = u.shape
    wtot = w_uq.shape[1]
    ts = min(ts, s)
    lane = jnp.arange(tc)
    same = (lane[:, None] // LANES == lane[None, :] // LANES)
    pos = lane % LANES
    nope = pos < MLA_NOPE
    pe = (pos >= MLA_NOPE) & (pos < MLA_NOPE + MLA_ROPE)
    seg = (same & ((nope[:, None] & nope[None, :]) | (pe[:, None] & pe[None, :]))).astype(BF16)
    p1 = jnp.arange(LANES)
    inv_n = jnp.where(p1 < MLA_NOPE, 1.0 / MLA_NOPE, jnp.where(p1 < MLA_NOPE + MLA_ROPE, 1.0 / MLA_ROPE, 0.0))
    blk = pl.BlockSpec((1, ts, tc), lambda i, t, c: (i, t, c))
    tab = pl.BlockSpec((1, ts, LANES), lambda i, t, c: (i, t, 0))
    vec = pl.BlockSpec((1, LANES), lambda i, t, c: (0, 0))
    pe_blk = (MLA_Q_RANK + MLA_KV_RANK) // LANES
    out = jax.ShapeDtypeStruct((b, s, wtot), BF16)
    return pl.pallas_call(
        _mla_prep_kernel,
        grid=(b, s // ts, wtot // tc),
        in_specs=[pl.BlockSpec((1, ts, MLA_Q_RANK), lambda i, t, c: (i, t, 0)),
                  pl.BlockSpec((1, ts, MLA_KV_RANK), lambda i, t, c: (i, t, MLA_Q_RANK // MLA_KV_RANK)),
                  pl.BlockSpec((1, ts, LANES), lambda i, t, c: (i, t, pe_blk)), tab, tab,
                  pl.BlockSpec((tc, tc), lambda i, t, c: (0, 0)), vec, vec, vec, vec,
                  pl.BlockSpec((1, MLA_Q_RANK), lambda i, t, c: (0, 0)),
                  pl.BlockSpec((1, MLA_KV_RANK), lambda i, t, c: (0, 0)),
                  pl.BlockSpec((MLA_Q_RANK, tc), lambda i, t, c: (0, c)),
                  pl.BlockSpec((MLA_KV_RANK, tc), lambda i, t, c: (0, c))],
        out_specs=[blk, blk, pl.BlockSpec((1, ts // ATTN_TILE, tc // LANES * MLA_V, ATTN_TILE),
                                          lambda i, t, c: (i, t, c, 0))],
        out_shape=[out, out, jax.ShapeDtypeStruct((b, s // ATTN_TILE, wtot // LANES * MLA_V, ATTN_TILE), BF16)],
        compiler_params=_params("parallel", "parallel", "arbitrary"),
        name="mla_prep",
    )(u, u, u, cos, sin, seg, gq, gkn, gkp, inv_n.reshape(1, LANES).astype(F32),
      gcq.reshape(1, -1).astype(F32), gckv.reshape(1, -1).astype(F32), w_uq, w_ukv)


def _diff_prep_kernel(q_ref, k_ref, v_ref, cos_ref, sin_ref, e_ref, gq_ref, gk_ref, qo_ref, ko_ref, vo_ref):
    cos, sin = cos_ref[0], sin_ref[0]
    _store_transposed_tiles(vo_ref, v_ref[0])

    def prep(x, g, scale):
        w = x.shape[1]
        xg = x * _tile_lanes(g, w)
        xr = xg * _tile_lanes(cos, w) + _swap_halves(xg, DIFF_QK // 2) * _tile_lanes(sin, w)
        inv = lax.rsqrt(_seg_sum(x * x, e_ref[...]) * (1.0 / DIFF_QK) + EPS)
        return xr * inv * scale

    qo_ref[0] = prep(q_ref[0], gq_ref[...], DIFF_QK ** -0.5).astype(qo_ref.dtype)
    ko_ref[0] = prep(k_ref[0], gk_ref[...], 1.0).astype(ko_ref.dtype)


def diff_prep(u, cos, sin, gq, gk, *, ts=512, tc=512):
    b, s, _ = u.shape
    ts = min(ts, s)
    wtot = 2 * DIFF_HEADS * DIFF_QK
    q_base = (CD_IN_PAD - 3 * wtot) // tc
    seg = (jnp.arange(tc)[:, None] // DIFF_QK == jnp.arange(tc)[None, :] // DIFF_QK).astype(BF16)
    blk = lambda off: pl.BlockSpec((1, ts, tc), lambda i, t, c: (i, t, off + c))
    tab = pl.BlockSpec((1, ts, LANES), lambda i, t, c: (i, t, 0))
    vec = pl.BlockSpec((1, LANES), lambda i, t, c: (0, 0))
    out = jax.ShapeDtypeStruct((b, s, wtot), BF16)
    return pl.pallas_call(
        _diff_prep_kernel,
        grid=(b, s // ts, wtot // tc),
        in_specs=[blk(q_base), blk(q_base + wtot // tc), blk(q_base + 2 * wtot // tc), tab, tab,
                  pl.BlockSpec((tc, tc), lambda i, t, c: (0, 0)), vec, vec],
        out_specs=[blk(0), blk(0),
                   pl.BlockSpec((1, ts // ATTN_TILE, tc, ATTN_TILE), lambda i, t, c: (i, t, c, 0))],
        out_shape=[out, out, jax.ShapeDtypeStruct((b, s // ATTN_TILE, wtot, ATTN_TILE), BF16)],
        compiler_params=_params("parallel", "parallel", "arbitrary"),
        name="diff_prep",
    )(u, u, u, cos, sin, seg, gq, gk)


def _causal_attn_kernel(lam_ref, q_ref, k_ref, vt_ref, g_ref, o_ref, *, n_sm, tq, ow, out_scale):
    qi = pl.program_id(2)
    q = q_ref[0]
    hps = q.shape[1] // LANES
    slab = lambda x, h: x[:, h * LANES:(h + 1) * LANES]
    lane = lax.broadcasted_iota(jnp.int32, (tq, LANES), 1)
    qs, src = [], []
    for h in range(hps):
        qh = slab(q, h)
        if n_sm == 2:
            zero = jnp.zeros_like(qh)
            qs += [jnp.where(lane < DIFF_QK, qh, zero), jnp.where(lane < DIFF_QK, zero, qh)]
            src += [h, h]
        else:
            qs.append(qh)
            src.append(h)
    nch = len(qs)
    key_i = lax.broadcasted_iota(jnp.int32, (tq, tq), 0)
    qry_i = lax.broadcasted_iota(jnp.int32, (tq, tq), 1)

    def step(j, carry, diagonal):
        kj = k_ref[0, pl.ds(pl.multiple_of(j * tq, tq), tq), :]
        scores = lambda i: _dot_t(slab(kj, src[i]), qs[i])
        new = []
        ahead = [scores(i) for i in range(min(ATTN_LOOKAHEAD, nch))]
        for i in range(nch):
            s = ahead.pop(0)
            if i + ATTN_LOOKAHEAD < nch:
                ahead.append(scores(i + ATTN_LOOKAHEAD))
            if diagonal:
                s = jnp.where(key_i <= qry_i, s, NEG_INF)
            m, l, acc = carry[3 * i:3 * i + 3]
            m_new = jnp.maximum(m, jnp.max(s, axis=0, keepdims=True))
            alpha = jnp.exp(m - m_new)
            p = jnp.exp(s - m_new)
            new += [m_new, alpha * l + jnp.sum(p, axis=0, keepdims=True),
                    alpha * acc + _dot(vt_ref[0, j, src[i] * ow:(src[i] + 1) * ow, :], p.astype(BF16))]
        return tuple(new)

    init = (jnp.full((1, tq), NEG_INF, F32), jnp.zeros((1, tq), F32),
            jnp.zeros((ow, tq), F32)) * nch
    carry = lax.fori_loop(0, qi, lambda j, cr: step(j, cr, False), init)
    carry = step(qi, carry, True)
    outs = []
    for h in range(hps):
        c0 = 3 * n_sm * h
        o = carry[c0 + 2] / carry[c0 + 1]
        if n_sm == 2:
            o = o - lam_ref[0] * (carry[c0 + 5] / carry[c0 + 4])
            o = o * lax.rsqrt(jnp.mean(o * o, axis=0, keepdims=True) + EPS) * g_ref[...] * out_scale
        outs.append(o.T.astype(o_ref.dtype))
    o_ref[0] = jnp.concatenate(outs, axis=-1)


def causal_attention(q, k, vt, *, n_sm, lam=None, g=None, out_scale=1.0, hps=4, name="causal_attention"):
    b, s, wtot = q.shape
    tq = vt.shape[3]
    width = hps * LANES
    groups = wtot // width
    ow = vt.shape[2] // (wtot // LANES)
    lam = jnp.zeros((1,), F32) if lam is None else lam.reshape(1).astype(F32)
    g = jnp.ones((ow, 1), F32) if g is None else g.reshape(ow, 1).astype(F32)
    seq = pl.BlockSpec((1, s, width), lambda i, h, t: (i, 0, h))
    tile = pl.BlockSpec((1, tq, width), lambda i, h, t: (i, t, h))
    return pl.pallas_call(
        functools.partial(_causal_attn_kernel, n_sm=n_sm, tq=tq, ow=ow, out_scale=out_scale),
        grid=(b, groups, s // tq),
        in_specs=[pl.BlockSpec(memory_space=pltpu.SMEM), tile, seq,
                  pl.BlockSpec((1, s // tq, hps * ow, tq), lambda i, h, t: (i, 0, h, 0)),
                  pl.BlockSpec((ow, 1), lambda i, h, t: (0, 0))],
        out_specs=pl.BlockSpec((1, tq, hps * ow), lambda i, h, t: (i, t, h)),
        out_shape=jax.ShapeDtypeStruct((b, s, groups * hps * ow), BF16),
        compiler_params=_params("parallel", "parallel", "arbitrary"),
        name=name,
    )(lam, q, k, vt, g)


def _memx_kernel(x_ref, g_ref, wq_ref, kv_ref, gq_ref, gk_ref, wo_ref, o_ref):
    x = x_ref[0]
    q = _dot(_rms(x, g_ref[...]).astype(BF16), wq_ref[...])
    kv = kv_ref[0]
    outs = []
    for h in range(MEM_HEADS):
        sl = slice(h * MEM_HEAD_DIM, (h + 1) * MEM_HEAD_DIM)
        qh = (_rms(q[:, sl], gq_ref[...]) * MEM_HEAD_DIM ** -0.5).astype(BF16)
        kh = _rms(kv[:, sl], gk_ref[...]).astype(BF16)
        vh = kv[:, MEM_W + h * MEM_HEAD_DIM:MEM_W + (h + 1) * MEM_HEAD_DIM].astype(BF16)
        s = _dot_t(qh, kh)
        p = jnp.exp(s - jnp.max(s, axis=-1, keepdims=True))
        outs.append(_dot(p.astype(BF16), vh) / jnp.sum(p, axis=-1, keepdims=True))
    o_ref[0] = x + _dot(jnp.concatenate(outs, axis=-1).astype(BF16), wo_ref[...])


def mem_cross_attention(x, mem_kv, g, wq, gq, gk, wo, *, tm=512):
    b, s, d = x.shape
    m = mem_kv.shape[1]
    tm = min(tm, s)
    const = lambda shape: pl.BlockSpec(shape, lambda i, t: (0,) * len(shape))
    tile = pl.BlockSpec((1, tm, d), lambda i, t: (i, t, 0))
    return pl.pallas_call(
        _memx_kernel,
        grid=(b, s // tm),
        in_specs=[tile, const((1, d)), const((d, MEM_W)),
                  pl.BlockSpec((1, m, 2 * MEM_W), lambda i, t: (i, 0, 0)),
                  const((1, MEM_HEAD_DIM)), const((1, MEM_HEAD_DIM)), const((MEM_W, d))],
        out_specs=tile,
        out_shape=jax.ShapeDtypeStruct((b, s, d), F32),
        compiler_params=_params("parallel", "arbitrary"),
        name="mem_cross_attention",
    )(x, g.reshape(1, d).astype(F32), wq, mem_kv, gq.reshape(1, -1).astype(F32),
      gk.reshape(1, -1).astype(F32), wo)


def _rope_tables(positions, dim, lead_ones, tail):
    inv = 1.0 / (ROPE_THETA ** (jnp.arange(0, dim, 2, dtype=F32) / dim))
    ang = positions.astype(F32)[..., None] * inv
    c, s = jnp.cos(ang), jnp.sin(ang)
    shape = positions.shape
    cos = jnp.concatenate([jnp.ones(shape + (lead_ones,), F32), c, c, jnp.ones(shape + (tail,), F32)], axis=-1)
    sin = jnp.concatenate([jnp.zeros(shape + (lead_ones,), F32), -s, s, jnp.zeros(shape + (tail,), F32)], axis=-1)
    reps = LANES // cos.shape[-1]
    return jnp.tile(cos, (1, 1, reps)), jnp.tile(sin, (1, 1, reps))


def _pad_cols(w, cols):
    return jnp.pad(w, ((0, 0), (0, cols - w.shape[1])))


def _ab_in_layout(w):
    c = (SWA_HEADS + 2 * SWA_KV_HEADS) * HEAD_DIM + 3 * RWKV_DIM
    return jnp.concatenate([w[:, :c], _pad_cols(w[:, c:c + DECAY_LORA], LANES),
                            _pad_cols(w[:, c + DECAY_LORA:c + DECAY_LORA + AAA_LORA], LANES),
                            _pad_cols(w[:, c + DECAY_LORA + AAA_LORA:], 2 * LANES)], axis=1)


def _cd_in_layout(w):
    c1 = MLA_Q_RANK + MLA_KV_RANK
    z = lambda n: jnp.zeros((w.shape[0], n), w.dtype)
    return jnp.concatenate([w[:, :c1], z(MLA_NOPE), w[:, c1:c1 + MLA_ROPE], z(LANES - MLA_NOPE - MLA_ROPE),
                            z(LANES), w[:, c1 + MLA_ROPE:]], axis=1)


def _head_slabs(w, per_head):
    k = w.shape[0]
    return jnp.pad(w.reshape(k, -1, per_head), ((0, 0), (0, 0), (0, LANES - per_head))).reshape(k, -1)


def _slab_vec(*parts):
    v = jnp.concatenate([p.astype(F32) for p in parts])
    return jnp.pad(v, (0, LANES - v.shape[0])).reshape(1, LANES)


def kernel(x, mem, positions, ffn1_norm, ffn1_w_gate, ffn1_w_up, ffn1_w_down, mix_norm, ab_w_in, ab_w_out, swa_q_norm, swa_k_norm, swa_sinks, rwkv_mu, rwkv_w0, rwkv_w2, rwkv_a0, rwkv_a2, rwkv_g2, rwkv_k_k, rwkv_k_a, rwkv_r_k, rwkv_gn_g, rwkv_gn_b, cd_w_in, cd_w_out, mla_cq_norm, mla_ckv_norm, mla_w_uq, mla_w_ukv, mla_q_nope_norm, mla_k_nope_norm, mla_q_rope_norm, mla_k_rope_norm, diff_q_norm, diff_k_norm, diff_lq1, diff_lk1, diff_lq2, diff_lk2, diff_subln, memx_norm, memx_w_q, memx_q_norm, memx_w_o, mem_norm, mem_w_kv, mem_k_norm, ffn2_norm, ffn2_w_gate, ffn2_w_up, ffn2_w_down):
    b, s, d = x.shape
    m = mem.shape[1]
    t = b * s
    depth = ffn1_norm.shape[0]
    bf = lambda w: w.astype(BF16)
    cos64, sin64 = _rope_tables(positions, HEAD_DIM, 0, 0)
    cos32, sin32 = _rope_tables(positions, MLA_ROPE, MLA_NOPE, LANES - MLA_NOPE - MLA_ROPE)

    mem_kv = norm_matmul(mem.reshape(b * m, d), mem_norm, bf(mem_w_kv), name="mem_kv").reshape(b, m, 2 * MEM_W)

    x = x.reshape(t, d)
    for layer in range(depth):
        j = layer // 2
        x = ffn(x, ffn1_norm[layer], ffn1_w_gate, ffn1_w_up, ffn1_w_down, layer, name="ffn1")
        if layer % 2 == 0:
            u = norm_matmul(x, mix_norm[layer], _ab_in_layout(bf(ab_w_in[j])), name="ab_in")
            u = u.reshape(b, s, AB_IN_PAD)
            y_a = swa_attention(u, cos64, sin64, _slab_vec(swa_q_norm[j], swa_q_norm[j]),
                                _slab_vec(swa_k_norm[j], swa_k_norm[j]), swa_sinks[j])
            y_b = rwkv_mix(u, rwkv_mu[j], rwkv_w0[j], rwkv_w2[j], rwkv_a0[j], rwkv_a2[j], rwkv_g2[j],
                           rwkv_k_k[j], rwkv_k_a[j], rwkv_r_k[j], rwkv_gn_g[j], rwkv_gn_b[j])
            w_out = bf(ab_w_out[j])
            half = SWA_HEADS * HEAD_DIM
            x = matmul_residual(x, [y_a.reshape(t, -1), y_b.reshape(t, -1)], [w_out[:half], w_out[half:]],
                                name="ab_out")
        else:
            u = norm_matmul(x, mix_norm[layer], _cd_in_layout(bf(cd_w_in[j])), name="cd_in")
            u = u.reshape(b, s, CD_IN_PAD)
            zero64 = jnp.zeros((MLA_NOPE,), F32)
            q_c, k_c, vt_c = mla_prep(u, cos32, sin32,
                                      _slab_vec(mla_q_nope_norm[j], mla_q_rope_norm[j]),
                                      _slab_vec(mla_k_nope_norm[j]),
                                      _slab_vec(zero64, mla_k_rope_norm[j]),
                                      mla_cq_norm[j], mla_ckv_norm[j],
                                      bf(_head_slabs(mla_w_uq[j], MLA_NOPE + MLA_ROPE)), bf(mla_w_ukv[j]))
            y_c = causal_attention(q_c, k_c, vt_c, n_sm=1, hps=8, name="mla_attention")
            q_d, k_d, vt_d = diff_prep(u, cos64, sin64, _slab_vec(diff_q_norm[j], diff_q_norm[j]),
                                       _slab_vec(diff_k_norm[j], diff_k_norm[j]))
            lambda_init = 0.8 - 0.6 * math.exp(-0.3 * layer)
            lam = (jnp.exp(jnp.sum(diff_lq1[j].astype(F32) * diff_lk1[j].astype(F32)))
                   - jnp.exp(jnp.sum(diff_lq2[j].astype(F32) * diff_lk2[j].astype(F32))) + lambda_init)
            y_d = causal_attention(q_d, k_d, vt_d, n_sm=2, hps=4, lam=lam, g=diff_subln[j],
                                   out_scale=1.0 - lambda_init, name="diff_attention")
            w_out = bf(cd_w_out[j])
            n_c = MLA_HEADS * MLA_V
            x = matmul_residual(x, [y_c.reshape(t, -1), y_d.reshape(t, -1)], [w_out[:n_c], w_out[n_c:]],
                                name="cd_out")
        x = mem_cross_attention(x.reshape(b, s, d), mem_kv, memx_norm[layer], bf(memx_w_q[layer]),
                                memx_q_norm[layer], mem_k_norm, bf(memx_w_o[layer])).reshape(t, d)
        x = ffn(x, ffn2_norm[layer], ffn2_w_gate, ffn2_w_up, ffn2_w_down, layer, name="ffn2")
    return x.reshape(b, s, d)
```

```python
import functools
import math

import jax
import jax.numpy as jnp
from jax import lax
from jax.experimental import pallas as pl
from jax.experimental.pallas import tpu as pltpu

F32 = jnp.float32
BF16 = jnp.bfloat16

EPS = 1e-6
ROPE_THETA = 10000.0
NEG_INF = -1e30
LOG2E = math.log2(math.e)
ATTN_LOOKAHEAD = 8
ATTN_TILE = 256
SEG_W = 256
LANES = 128

D_MODEL = 2048
D_FF = 5632
HEAD_DIM = 64
SWA_HEADS = 16
SWA_KV_HEADS = 4
SWA_BLOCK = 128
RWKV_DIM = 1024
RWKV_CHUNK = 64
RWKV_GN_EPS = 64e-5
DECAY_LORA, AAA_LORA, GATE_LORA = 64, 64, 160
MLA_HEADS, MLA_Q_RANK, MLA_KV_RANK, MLA_NOPE, MLA_ROPE, MLA_V = 16, 512, 256, 64, 32, 64
DIFF_HEADS, DIFF_QK, DIFF_V = 8, 64, 128
MEM_HEADS, MEM_HEAD_DIM = 4, 128
MEM_W = MEM_HEADS * MEM_HEAD_DIM
AB_IN_PAD = 5120
CD_IN_PAD = 4096

VMEM_LIMIT = 48 * 1024 * 1024
FFN_VMEM_LIMIT = 60 * 1024 * 1024


def _params(*sem):
    return pltpu.CompilerParams(dimension_semantics=sem, vmem_limit_bytes=VMEM_LIMIT)


def _dot(a, b):
    return jnp.dot(a, b, preferred_element_type=F32)


def _dot_t(a, b):
    return lax.dot_general(a, b, (((1,), (1,)), ((), ())), preferred_element_type=F32)


def _dot_0(a, b):
    return lax.dot_general(a, b, (((0,), (0,)), ((), ())), preferred_element_type=F32)


def _rms(x, g):
    return x * lax.rsqrt(jnp.mean(x * x, axis=-1, keepdims=True) + EPS) * g


def _seg_sum(x, e):
    hi = x.astype(BF16)
    lo = (x - hi.astype(F32)).astype(BF16)
    w = e.shape[0]
    parts = [_dot(hi[:, i:i + w], e) + _dot(lo[:, i:i + w], e) for i in range(0, x.shape[1], w)]
    return parts[0] if len(parts) == 1 else jnp.concatenate(parts, axis=-1)


def _tile_lanes(v, width):
    return v if v.shape[-1] == width else jnp.tile(v, (1, width // v.shape[-1]))


def _store_transposed_tiles(vt_ref, v):
    for r in range(vt_ref.shape[1]):
        vt_ref[0, r] = v[r * ATTN_TILE:(r + 1) * ATTN_TILE, :].T.astype(vt_ref.dtype)


def _swap_halves(x, half):
    w = x.shape[-1]
    lane = lax.broadcasted_iota(jnp.int32, x.shape, x.ndim - 1)
    low = (lane & (2 * half - 1)) < half
    return jnp.where(low, pltpu.roll(x, w - half, x.ndim - 1), pltpu.roll(x, half, x.ndim - 1))


def _norm_matmul_kernel(x_ref, g_ref, w_ref, o_ref, xn_ref):
    @pl.when(pl.program_id(1) == 0)
    def _():
        xn_ref[...] = _rms(x_ref[...].astype(F32), g_ref[...]).astype(BF16)

    o_ref[...] = _dot(xn_ref[...], w_ref[...]).astype(o_ref.dtype)


def norm_matmul(x, g, w, *, k_blk=0, out_dtype=F32, tm=1024, tn=512, name="norm_matmul"):
    t = x.shape[0]
    k, n = w.shape
    tm, tn = min(tm, t), min(tn, n)
    assert t % tm == 0 and n % tn == 0
    return pl.pallas_call(
        _norm_matmul_kernel,
        grid=(t // tm, n // tn),
        in_specs=[pl.BlockSpec((tm, k), lambda i, j: (i, k_blk)),
                  pl.BlockSpec((1, k), lambda i, j: (0, 0)),
                  pl.BlockSpec((k, tn), lambda i, j: (0, j))],
        out_specs=pl.BlockSpec((tm, tn), lambda i, j: (i, j)),
        out_shape=jax.ShapeDtypeStruct((t, n), out_dtype),
        scratch_shapes=[pltpu.VMEM((tm, k), BF16)],
        compiler_params=_params("parallel", "arbitrary"),
        name=name,
    )(x, g.reshape(1, k).astype(F32), w)


def _matmul_res_kernel(*refs, n_in):
    x_ref, o_ref = refs[2 * n_in], refs[2 * n_in + 1]
    acc = x_ref[...]
    for i in range(n_in):
        acc = acc + _dot(refs[i][...], refs[n_in + i][...])
    o_ref[...] = acc


def matmul_residual(x, a_list, w_list, *, tm=256, name="matmul_residual"):
    t, n = x.shape
    tm = min(tm, t)
    assert t % tm == 0
    n_in = len(a_list)
    in_specs = ([pl.BlockSpec((tm, a.shape[1]), lambda i: (i, 0)) for a in a_list]
                + [pl.BlockSpec(w.shape, lambda i: (0, 0)) for w in w_list]
                + [pl.BlockSpec((tm, n), lambda i: (i, 0))])
    return pl.pallas_call(
        functools.partial(_matmul_res_kernel, n_in=n_in),
        grid=(t // tm,),
        in_specs=in_specs,
        out_specs=pl.BlockSpec((tm, n), lambda i: (i, 0)),
        out_shape=jax.ShapeDtypeStruct((t, n), F32),
        compiler_params=_params("parallel"),
        name=name,
    )(*a_list, *w_list, x)


def _ffn_kernel(x_ref, g_ref, wg_ref, wu_ref, wd_ref, o_ref, xn_ref, *, nf):
    f = pl.program_id(1)

    @pl.when(f == 0)
    def _():
        x = x_ref[...]
        xn_ref[...] = _rms(x, g_ref[...]).astype(BF16)
        o_ref[...] = 2.0 * x

    xn = xn_ref[...]
    a = _dot(xn, wg_ref[...].astype(BF16))
    b = _dot(xn, wu_ref[...].astype(BF16))
    h = (a * (1.0 / (1.0 + jnp.exp(-a))) * b).astype(BF16)
    o_ref[...] += _dot(h, wd_ref[...].astype(BF16))

    @pl.when(f == nf - 1)
    def _():
        o_ref[...] = 0.5 * o_ref[...]


def ffn(x, g, wg, wu, wd, layer, *, tm=1024, tf=256, name="ffn"):
    t, d = x.shape
    ff = wg.shape[2]
    tm = min(tm, t)
    assert t % tm == 0 and ff % tf == 0
    nf = ff // tf
    return pl.pallas_call(
        functools.partial(_ffn_kernel, nf=nf),
        grid=(t // tm, nf),
        in_specs=[pl.BlockSpec((tm, d), lambda i, f: (i, 0)),
                  pl.BlockSpec((1, d), lambda i, f: (0, 0)),
                  pl.BlockSpec((None, d, tf), lambda i, f: (layer, 0, f)),
                  pl.BlockSpec((None, d, tf), lambda i, f: (layer, 0, f)),
                  pl.BlockSpec((None, tf, d), lambda i, f: (layer, f, 0))],
        out_specs=pl.BlockSpec((tm, d), lambda i, f: (i, 0)),
        out_shape=jax.ShapeDtypeStruct((t, d), F32),
        scratch_shapes=[pltpu.VMEM((tm, d), BF16)],
        compiler_params=pltpu.CompilerParams(dimension_semantics=("parallel", "arbitrary"),
                                             vmem_limit_bytes=FFN_VMEM_LIMIT),
        name=name,
    )(x, g.reshape(1, d).astype(F32), wg, wu, wd)


def _swa_kernel(sink_ref, q_ref, kc_ref, kp_ref, vc_ref, vp_ref, cc_ref, sc_ref, cp_ref, sp_ref,
                gq_ref, gk_ref, e_ref, o_ref):
    n = pl.program_id(1)
    blk = SWA_BLOCK
    group = SWA_HEADS // SWA_KV_HEADS
    q = q_ref[0]
    k = jnp.concatenate([kp_ref[0], kc_ref[0]], axis=0)
    v = jnp.concatenate([vp_ref[0], vc_ref[0]], axis=0)
    cos_q, sin_q = cc_ref[0], sc_ref[0]
    cos_k = jnp.concatenate([cp_ref[0], cos_q], axis=0)
    sin_k = jnp.concatenate([sp_ref[0], sin_q], axis=0)
    def prep(x, g, cos, sin, scale):
        w = x.shape[1]
        xg = x * _tile_lanes(g, w)
        xr = xg * _tile_lanes(cos, w) + _swap_halves(xg, HEAD_DIM // 2) * _tile_lanes(sin, w)
        return xr * lax.rsqrt(_seg_sum(x * x, e_ref[...]) * (1.0 / HEAD_DIM) + EPS) * scale

    qr = prep(q, gq_ref[...], cos_q, sin_q, HEAD_DIM ** -0.5 * LOG2E).astype(BF16)
    kr = prep(k, gk_ref[...], cos_k, sin_k, 1.0)
    key_i = lax.broadcasted_iota(jnp.int32, (2 * blk, blk), 0)
    qry_i = lax.broadcasted_iota(jnp.int32, (2 * blk, blk), 1)
    rel = qry_i + blk - key_i
    valid = (rel >= 0) & (rel < blk) & ((n > 0) | (key_i >= blk))
    low = lax.broadcasted_iota(jnp.int32, (blk, LANES), 1) < HEAD_DIM
    zero = jnp.zeros((blk, LANES), BF16)
    k_dup, v_t = [], []
    for g in range(SWA_KV_HEADS):
        kg = kr[:, g * HEAD_DIM:(g + 1) * HEAD_DIM]
        k_dup.append(jnp.concatenate([kg, kg], axis=-1).astype(BF16))
        v_t.append(v[:, g * HEAD_DIM:(g + 1) * HEAD_DIM].T.astype(BF16))

    def scores(h):
        slab = qr[:, (h // 2) * LANES:(h // 2 + 1) * LANES]
        qh = jnp.where(low, slab, zero) if h % 2 == 0 else jnp.where(low, zero, slab)
        return _dot_t(k_dup[h // group], qh)

    ahead = [scores(h) for h in range(ATTN_LOOKAHEAD)]
    ones = jnp.ones((8, 2 * blk), BF16)
    outs = []
    for h in range(SWA_HEADS):
        s = jnp.where(valid, ahead.pop(0), NEG_INF)
        if h + ATTN_LOOKAHEAD < SWA_HEADS:
            ahead.append(scores(h + ATTN_LOOKAHEAD))
        sink = sink_ref[h] * LOG2E
        m = jnp.maximum(jnp.max(s, axis=0, keepdims=True), sink)
        p = jnp.exp2(s - m).astype(BF16)
        den = _dot(ones, p)[0:1] + jnp.exp2(sink - m)
        outs.append(_dot(v_t[h // group], p) / den)
    slabs = [jnp.concatenate(outs[i:i + 2], axis=0).T for i in range(0, SWA_HEADS, 2)]
    o_ref[0] = jnp.concatenate(slabs, axis=-1).astype(o_ref.dtype)


def swa_attention(u, cos, sin, gq, gk, sinks):
    b, s, _ = u.shape
    nb = s // SWA_BLOCK
    qw, kw = SWA_HEADS * HEAD_DIM, SWA_KV_HEADS * HEAD_DIM
    cur = lambda c: (lambda i, n: (i, n, c))
    prev = lambda c: (lambda i, n: (i, jnp.maximum(n - 1, 0), c))
    tab = pl.BlockSpec((1, SWA_BLOCK, LANES), cur(0))
    tab_prev = pl.BlockSpec((1, SWA_BLOCK, LANES), prev(0))
    gain = pl.BlockSpec((1, LANES), lambda i, n: (0, 0))
    seg = (jnp.arange(kw)[:, None] // HEAD_DIM == jnp.arange(kw)[None, :] // HEAD_DIM).astype(BF16)
    return pl.pallas_call(
        _swa_kernel,
        grid=(b, nb),
        in_specs=[pl.BlockSpec(memory_space=pltpu.SMEM),
                  pl.BlockSpec((1, SWA_BLOCK, qw), cur(0)),
                  pl.BlockSpec((1, SWA_BLOCK, kw), cur(qw // kw)),
                  pl.BlockSpec((1, SWA_BLOCK, kw), prev(qw // kw)),
                  pl.BlockSpec((1, SWA_BLOCK, kw), cur(qw // kw + 1)),
                  pl.BlockSpec((1, SWA_BLOCK, kw), prev(qw // kw + 1)),
                  tab, tab, tab_prev, tab_prev, gain, gain, pl.BlockSpec((kw, kw), lambda i, n: (0, 0))],
        out_specs=pl.BlockSpec((1, SWA_BLOCK, qw), cur(0)),
        out_shape=jax.ShapeDtypeStruct((b, s, qw), BF16),
        compiler_params=_params("parallel", "arbitrary"),
        name="swa_attention",
    )(sinks.astype(F32), u, u, u, u, u, cos, sin, cos, sin, gq, gk, seg)


def _mm(a, b, dims, passes):
    dn = (dims, ((), ()))
    dg = lambda x, y: lax.dot_general(x, y, dn, preferred_element_type=F32)
    ah = a.astype(BF16)
    bh = b.astype(BF16)
    if passes == 1:
        return dg(ah, bh)
    al = (a - ah.astype(F32)).astype(BF16)
    bl = (b - bh.astype(F32)).astype(BF16)
    return dg(ah, bh) + dg(ah, bl) + dg(al, bh)


_NN = ((1,), (0,))
_NT = ((1,), (1,))
_TN = ((0,), (0,))
P_SC, P_INV, P_PQ, P_OUT, P_ST = 1, 1, 1, 1, 1
RWKV_UNROLL = 8


RWKV_PREP_ROWS = 256


def _rwkv_kernel(ur_ref, uk_ref, uv_ref, ul_ref, mur_ref, muk_ref, muv_ref, mul_ref, w0_ref, w2_ref, a0_ref,
                 a2_ref, g2_ref, kkw_ref, ka_ref, gng_ref, gnb_ref, rk_ref,
                 o_ref, r_ref, k_ref, v_ref, kk_ref, b_ref, lw_ref, g_ref,
                 st_ref, y1_ref, y0_ref, n_ref, z_ref, dec_ref, *, nchunk):
    c = RWKV_CHUNK
    lane_c = lax.broadcasted_iota(jnp.int32, (c, LANES), 1)
    head0 = lane_c < HEAD_DIM
    ri = lax.broadcasted_iota(jnp.int32, (2 * c, 2 * c), 0)
    ci = lax.broadcasted_iota(jnp.int32, (2 * c, 2 * c), 1)
    eye = jnp.where(ri == ci, 1.0, 0.0)
    tril_c = jnp.where(lax.broadcasted_iota(jnp.int32, (c, c), 0) >= lax.broadcasted_iota(jnp.int32, (c, c), 1),
                       1.0, 0.0).astype(BF16)
    stack = lambda x: jnp.concatenate([jnp.where(head0, x, 0.0), jnp.where(head0, 0.0, x)], axis=0)

    def seg_mean(x):
        first = lax.broadcasted_iota(jnp.int32, x.shape, 1) < HEAD_DIM
        m0 = jnp.sum(jnp.where(first, x, 0.0), axis=-1, keepdims=True)
        m1 = jnp.sum(jnp.where(first, 0.0, x), axis=-1, keepdims=True)
        return jnp.where(first, m0, m1) * (1.0 / HEAD_DIM)

    def prep(io, first_group):
        group_rows = RWKV_UNROLL * c
        for tix in range(group_rows // RWKV_PREP_ROWS):
            start = pl.multiple_of(io * group_rows + tix * RWKV_PREP_ROWS, RWKV_PREP_ROWS)
            rows = pl.ds(start, RWKV_PREP_ROWS)
            at_start = first_group and tix == 0

            def shifted(ref, mu_ref):
                x = ref[0, rows, :]
                if at_start:
                    last = jnp.zeros((1, x.shape[1]), F32)
                else:
                    last = ref[0, pl.ds(pl.multiple_of(start - 8, 8), 8), :][7:8, :]
                row = lax.broadcasted_iota(jnp.int32, x.shape, 0)
                prev = jnp.where(row == 0, last, pltpu.roll(x, 1, 0))
                return x + (prev - x) * mu_ref[...]

            r = shifted(ur_ref, mur_ref)
            k = shifted(uk_ref, muk_ref)
            v = shifted(uv_ref, muv_ref)
            lo = shifted(ul_ref, mul_ref)
            yield
            w_lo, a_lo, g_lo = lo[:, 0:LANES], lo[:, LANES:2 * LANES], lo[:, 2 * LANES:4 * LANES]
            z = -(w0_ref[...] + _mm(jnp.tanh(w_lo), w2_ref[...], _NN, 3))
            w = -(jnp.maximum(z, 0.0) + jnp.log(1.0 + jnp.exp(-jnp.abs(z)))) - 0.5
            a = 1.0 / (1.0 + jnp.exp(-(a0_ref[...] + _mm(a_lo, a2_ref[...], _NN, 1))))
            g = _mm(1.0 / (1.0 + jnp.exp(-g_lo)), g2_ref[...], _NN, 1)
            yield
            kk = k * kkw_ref[...]
            kk = kk / jnp.maximum(jnp.sqrt(seg_mean(kk * kk) * float(HEAD_DIM)), 1e-12)
            r_ref[rows, :] = r
            k_ref[rows, :] = k * (1.0 + (a - 1.0) * ka_ref[...])
            v_ref[rows, :] = v
            kk_ref[rows, :] = kk
            b_ref[rows, :] = kk * a
            lw_ref[rows, :] = -jnp.exp(w)
            g_ref[rows, :] = g
            yield

    def build(ics):
        each = lambda f, *cols: [f(*args) for args in zip(*cols)]
        sls = [pl.ds(pl.multiple_of(ic * c, c), c) for ic in ics]
        load = lambda ref: [ref[sl, :] for sl in sls]
        r, k, v, kk, b, lw = (load(ref) for ref in (r_ref, k_ref, v_ref, kk_ref, b_ref, lw_ref))

        def running_sum(x):
            l1 = x.astype(BF16)
            rest = x - l1.astype(F32)
            l2 = rest.astype(BF16)
            l3 = (rest - l2.astype(F32)).astype(BF16)
            return _dot(tril_c, l1) + _dot(tril_c, l2) + _dot(tril_c, l3)

        cum = each(running_sum, lw)
        yield
        cum_end = [x[c - 1:c, :] for x in cum]
        e_neg = each(lambda x: jnp.exp(-x), cum)
        e_end = each(lambda x, xe: jnp.exp(xe - x), cum, cum_end)
        a_s = each(lambda kk_, x, l: stack(-kk_ * jnp.exp(x - l)), kk, cum, lw)
        r_s = each(lambda r_, x: stack(r_ * jnp.exp(x)), r, cum)
        b_s = each(lambda b_, e: stack(b_ * e), b, e_neg)
        k_s = each(lambda k_, e: stack(k_ * e), k, e_neg)
        bh_s = each(lambda b_, e: stack(b_ * e), b, e_end)
        kh_s = each(lambda k_, e: stack(k_ * e), k, e_end)
        v_s = each(stack, v)
        n2 = 2 * c
        sc = each(lambda a_, r_, b_, k_: _mm(jnp.concatenate([a_, r_], axis=0),
                                             jnp.concatenate([b_, k_], axis=0), _NT, P_SC), a_s, r_s, b_s, k_s)
        low = [jnp.where(ri > ci, x[:n2, :n2], 0.0) for x in sc]
        a_ak = [jnp.where(ri > ci, x[:n2, n2:], 0.0) for x in sc]
        a_rb = [jnp.where(ri >= ci, x[n2:, :n2], 0.0) for x in sc]
        a_rk = [jnp.where(ri >= ci, x[n2:, n2:], 0.0) for x in sc]
        yield
        inv = [eye + x for x in low]
        pw = low
        for _ in range(5):
            pw = each(lambda x: _mm(x, x, _NN, P_INV), pw)
            inv = each(lambda t, x: t + _mm(t, x, _NN, P_INV), inv, pw)
            yield
        akv = each(lambda x, y: _mm(x, y, _NN, P_PQ), a_ak, v_s)
        yield
        pq = each(lambda t, x, y: _mm(t, jnp.concatenate([x, y], axis=1), _NN, P_PQ), inv, a_s, akv)
        yield
        yy = each(lambda x, y: _mm(x, y, _NN, P_OUT), a_rb, pq)
        y0b = each(lambda x, y: _mm(x, y, _NN, P_OUT), a_rk, v_s)
        nz = each(lambda x, y: _mm(x, y, _TN, P_OUT), pq, bh_s)
        zb = each(lambda x, y: _mm(x, y, _TN, P_OUT), v_s, kh_s)
        for i, ic in enumerate(ics):
            y1_ref[ic] = r_s[i] + yy[i][:, :LANES]
            y0_ref[ic] = yy[i][:, LANES:] + y0b[i]
            n_ref[ic] = nz[i][:LANES]
            z_ref[ic] = nz[i][LANES:] + zb[i]
            dec_ref[ic] = jnp.broadcast_to(jnp.exp(cum_end[i]), (8, LANES))

    def emit(ic, st):
        sl = pl.ds(pl.multiple_of(ic * c, c), c)
        r, k, v, g = r_ref[sl, :], k_ref[sl, :], v_ref[sl, :], g_ref[sl, :]
        y_st = _mm(y1_ref[ic], st, _NT, P_ST) + y0_ref[ic]
        y = y_st[0:c] + y_st[c:2 * c]
        mean = seg_mean(y)
        var = seg_mean((y - mean) * (y - mean))
        yn = (y - mean) * lax.rsqrt(var + RWKV_GN_EPS) * gng_ref[...] + gnb_ref[...]
        bonus = seg_mean(r * k * rk_ref[...]) * float(HEAD_DIM) * v
        o_ref[0, sl, :] = ((yn + bonus) * g).astype(o_ref.dtype)
        return st * dec_ref[ic][0:1, :] + _mm(st, n_ref[ic], _NN, P_ST) + z_ref[ic]

    def scan(ics):
        st = st_ref[...]
        for ic in ics:
            st = emit(ic, st)
            yield
        st_ref[...] = st

    def run(*gens):
        live = list(gens)
        while live:
            live = [gen for gen in live if next(gen, live) is not live]

    group = lambda io: [io * RWKV_UNROLL + i for i in range(RWKV_UNROLL)]
    ngroup = nchunk // RWKV_UNROLL
    st_ref[...] = jnp.zeros_like(st_ref)
    run(prep(0, True))
    if ngroup > 1:
        run(prep(1, False), build(group(0)))
    else:
        run(build(group(0)))

    @pl.loop(1, ngroup - 1)
    def _(io):
        run(prep(io + 1, False), build(group(io)), scan(group(io - 1)))

    if ngroup > 1:
        run(build(group(ngroup - 1)), scan(group(ngroup - 2)))
    run(scan(group(ngroup - 1)))


def rwkv_mix(u, mu, w0, w2, a0, a2, g2, k_k, k_a, r_k, gn_g, gn_b):
    bsz, s, _ = u.shape
    npair = RWKV_DIM // LANES
    nchunk = s // RWKV_CHUNK
    assert nchunk % RWKV_UNROLL == 0 and (RWKV_UNROLL * RWKV_CHUNK) % RWKV_PREP_ROWS == 0
    base = (SWA_HEADS + 2 * SWA_KV_HEADS) * HEAD_DIM // LANES
    lora_w = 4 * LANES
    slab = lambda off: pl.BlockSpec((1, s, LANES), lambda i, p: (i, 0, off + p))
    vec = pl.BlockSpec((1, LANES), lambda i, p: (0, p))
    cols = lambda rows: pl.BlockSpec((rows, LANES), lambda i, p: (0, p))
    row = lambda vv: vv.reshape(1, -1).astype(F32)
    pad_rows = lambda m, rows: jnp.pad(m, ((0, rows - m.shape[0]), (0, 0))).astype(F32)
    pad_cols = lambda vv, n: jnp.pad(vv, (0, n - vv.shape[0]))
    c3 = 3 * RWKV_DIM
    mu_l = jnp.concatenate([pad_cols(mu[c3:c3 + DECAY_LORA], LANES),
                            pad_cols(mu[c3 + DECAY_LORA:c3 + DECAY_LORA + AAA_LORA], LANES),
                            pad_cols(mu[c3 + DECAY_LORA + AAA_LORA:], 2 * LANES)])
    seq = pltpu.VMEM((s, LANES), F32)
    mat = pltpu.VMEM((nchunk, LANES, LANES), F32)
    return pl.pallas_call(
        functools.partial(_rwkv_kernel, nchunk=nchunk),
        grid=(bsz, npair),
        in_specs=[slab(base), slab(base + npair), slab(base + 2 * npair),
                  pl.BlockSpec((1, s, lora_w), lambda i, p: (i, 0, (base + 3 * npair) * LANES // lora_w)),
                  vec, vec, vec, pl.BlockSpec((1, lora_w), lambda i, p: (0, 0)),
                  vec, cols(LANES), vec, cols(LANES), cols(2 * LANES), vec, vec, vec, vec, vec],
        out_specs=pl.BlockSpec((1, s, LANES), lambda i, p: (i, 0, p)),
        out_shape=jax.ShapeDtypeStruct((bsz, s, RWKV_DIM), BF16),
        scratch_shapes=[seq] * 7 + [pltpu.VMEM((LANES, LANES), F32), mat, mat, mat, mat,
                                    pltpu.VMEM((nchunk, 8, LANES), F32)],
        compiler_params=_params("parallel", "arbitrary"),
        name="rwkv_mix",
    )(u, u, u, u, row(mu[:RWKV_DIM]), row(mu[RWKV_DIM:2 * RWKV_DIM]), row(mu[2 * RWKV_DIM:c3]), row(mu_l),
      row(w0), pad_rows(w2, LANES), row(a0), pad_rows(a2, LANES), pad_rows(g2, 2 * LANES),
      row(k_k), row(k_a), row(gn_g), row(gn_b), row(r_k))


def _mla_prep_kernel(cq_ref, ckv_ref, pe_ref, cos_ref, sin_ref, e_ref, gq_ref, gkn_ref, gkp_ref, invn_ref,
                     gcq_ref, gckv_ref, wuq_ref, wukv_ref, qo_ref, ko_ref, vo_ref):
    cos, sin = cos_ref[0], sin_ref[0]
    half = MLA_ROPE // 2

    def rope(x, g):
        w = x.shape[1]
        xg = x * _tile_lanes(g, w)
        return xg * _tile_lanes(cos, w) + _swap_halves(xg, half) * _tile_lanes(sin, w)

    x = _dot(_rms(cq_ref[0], gcq_ref[...]).astype(BF16), wuq_ref[...])
    w = x.shape[1]
    inv_n = _tile_lanes(invn_ref[...], w)
    inv = lax.rsqrt(_seg_sum(x * x, e_ref[...]) * inv_n + EPS)
    qo_ref[0] = (rope(x, gq_ref[...]) * inv * ((MLA_NOPE + MLA_ROPE) ** -0.5 * LOG2E)).astype(qo_ref.dtype)
    kv = _dot(_rms(ckv_ref[0], gckv_ref[...]).astype(BF16), wukv_ref[...])
    inv_k = lax.rsqrt(_seg_sum(kv * kv, e_ref[...]) * inv_n + EPS)
    k_nope = kv * inv_k * _tile_lanes(gkn_ref[...], w)
    pe = pe_ref[0]
    inv_pe = lax.rsqrt(jnp.sum(pe * pe, axis=-1, keepdims=True) * (1.0 / MLA_ROPE) + EPS)
    k_pe = rope(pe, gkp_ref[...]) * inv_pe
    ko_ref[0] = (k_nope + _tile_lanes(k_pe, w)).astype(ko_ref.dtype)
    v = jnp.concatenate([kv[:, h * LANES + MLA_NOPE:(h + 1) * LANES] for h in range(w // LANES)], axis=-1)
    _store_transposed_tiles(vo_ref, v)


def mla_prep(u, cos, sin, gq, gkn, gkp, gcq, gckv, w_uq, w_ukv, *, ts=512, tc=512):
    b, s, _ = u.shape
    wtot = w_uq.shape[1]
    ts = min(ts, s)
    lane = jnp.arange(SEG_W)
    same = (lane[:, None] // LANES == lane[None, :] // LANES)
    pos = lane % LANES
    nope = pos < MLA_NOPE
    pe = (pos >= MLA_NOPE) & (pos < MLA_NOPE + MLA_ROPE)
    seg = (same & ((nope[:, None] & nope[None, :]) | (pe[:, None] & pe[None, :]))).astype(BF16)
    p1 = jnp.arange(LANES)
    inv_n = jnp.where(p1 < MLA_NOPE, 1.0 / MLA_NOPE, jnp.where(p1 < MLA_NOPE + MLA_ROPE, 1.0 / MLA_ROPE, 0.0))
    blk = pl.BlockSpec((1, ts, tc), lambda i, t, c: (i, t, c))
    tab = pl.BlockSpec((1, ts, LANES), lambda i, t, c: (i, t, 0))
    vec = pl.BlockSpec((1, LANES), lambda i, t, c: (0, 0))
    pe_blk = (MLA_Q_RANK + MLA_KV_RANK) // LANES
    out = jax.ShapeDtypeStruct((b, s, wtot), BF16)
    return pl.pallas_call(
        _mla_prep_kernel,
        grid=(b, s // ts, wtot // tc),
        in_specs=[pl.BlockSpec((1, ts, MLA_Q_RANK), lambda i, t, c: (i, t, 0)),
                  pl.BlockSpec((1, ts, MLA_KV_RANK), lambda i, t, c: (i, t, MLA_Q_RANK // MLA_KV_RANK)),
                  pl.BlockSpec((1, ts, LANES), lambda i, t, c: (i, t, pe_blk)), tab, tab,
                  pl.BlockSpec((SEG_W, SEG_W), lambda i, t, c: (0, 0)), vec, vec, vec, vec,
                  pl.BlockSpec((1, MLA_Q_RANK), lambda i, t, c: (0, 0)),
                  pl.BlockSpec((1, MLA_KV_RANK), lambda i, t, c: (0, 0)),
                  pl.BlockSpec((MLA_Q_RANK, tc), lambda i, t, c: (0, c)),
                  pl.BlockSpec((MLA_KV_RANK, tc), lambda i, t, c: (0, c))],
        out_specs=[blk, blk, pl.BlockSpec((1, ts // ATTN_TILE, tc // LANES * MLA_V, ATTN_TILE),
                                          lambda i, t, c: (i, t, c, 0))],
        out_shape=[out, out, jax.ShapeDtypeStruct((b, s // ATTN_TILE, wtot // LANES * MLA_V, ATTN_TILE), BF16)],
        compiler_params=_params("parallel", "parallel", "arbitrary"),
        name="mla_prep",
    )(u, u, u, cos, sin, seg, gq, gkn, gkp, inv_n.reshape(1, LANES).astype(F32),
      gcq.reshape(1, -1).astype(F32), gckv.reshape(1, -1).astype(F32), w_uq, w_ukv)


def _diff_prep_kernel(q_ref, k_ref, v_ref, cos_ref, sin_ref, e_ref, gq_ref, gk_ref, qo_ref, ko_ref, vo_ref):
    cos, sin = cos_ref[0], sin_ref[0]
    _store_transposed_tiles(vo_ref, v_ref[0])

    def prep(x, g, scale):
        w = x.shape[1]
        xg = x * _tile_lanes(g, w)
        xr = xg * _tile_lanes(cos, w) + _swap_halves(xg, DIFF_QK // 2) * _tile_lanes(sin, w)
        inv = lax.rsqrt(_seg_sum(x * x, e_ref[...]) * (1.0 / DIFF_QK) + EPS)
        return xr * inv * scale

    qo_ref[0] = prep(q_ref[0], gq_ref[...], DIFF_QK ** -0.5 * LOG2E).astype(qo_ref.dtype)
    ko_ref[0] = prep(k_ref[0], gk_ref[...], 1.0).astype(ko_ref.dtype)


def diff_prep(u, cos, sin, gq, gk, *, ts=512, tc=512):
    b, s, _ = u.shape
    ts = min(ts, s)
    wtot = 2 * DIFF_HEADS * DIFF_QK
    q_base = (CD_IN_PAD - 3 * wtot) // tc
    seg = (jnp.arange(SEG_W)[:, None] // DIFF_QK == jnp.arange(SEG_W)[None, :] // DIFF_QK).astype(BF16)
    blk = lambda off: pl.BlockSpec((1, ts, tc), lambda i, t, c: (i, t, off + c))
    tab = pl.BlockSpec((1, ts, LANES), lambda i, t, c: (i, t, 0))
    vec = pl.BlockSpec((1, LANES), lambda i, t, c: (0, 0))
    out = jax.ShapeDtypeStruct((b, s, wtot), BF16)
    return pl.pallas_call(
        _diff_prep_kernel,
        grid=(b, s // ts, wtot // tc),
        in_specs=[blk(q_base), blk(q_base + wtot // tc), blk(q_base + 2 * wtot // tc), tab, tab,
                  pl.BlockSpec((SEG_W, SEG_W), lambda i, t, c: (0, 0)), vec, vec],
        out_specs=[blk(0), blk(0),
                   pl.BlockSpec((1, ts // ATTN_TILE, tc, ATTN_TILE), lambda i, t, c: (i, t, c, 0))],
        out_shape=[out, out, jax.ShapeDtypeStruct((b, s // ATTN_TILE, wtot, ATTN_TILE), BF16)],
        compiler_params=_params("parallel", "parallel", "arbitrary"),
        name="diff_prep",
    )(u, u, u, cos, sin, seg, gq, gk)


def _causal_attn_kernel(lam_ref, q_ref, k_ref, vt_ref, g_ref, o_ref, *, n_sm, tq, ow, out_scale):
    qi = pl.program_id(2)
    q = q_ref[0]
    hps = q.shape[1] // LANES
    slab = lambda x, h: x[:, h * LANES:(h + 1) * LANES]
    lane = lax.broadcasted_iota(jnp.int32, (tq, LANES), 1)
    qs, src = [], []
    for h in range(hps):
        qh = slab(q, h)
        if n_sm == 2:
            zero = jnp.zeros_like(qh)
            qs += [jnp.where(lane < DIFF_QK, qh, zero), jnp.where(lane < DIFF_QK, zero, qh)]
            src += [h, h]
        else:
            qs.append(qh)
            src.append(h)
    nch = len(qs)
    key_i = lax.broadcasted_iota(jnp.int32, (tq, tq), 0)
    qry_i = lax.broadcasted_iota(jnp.int32, (tq, tq), 1)
    ones = jnp.ones((8, tq), BF16)

    def step(j, carry, diagonal):
        kj = k_ref[0, pl.ds(pl.multiple_of(j * tq, tq), tq), :]
        scores = lambda i: _dot_t(slab(kj, src[i]), qs[i])
        new = []
        ahead = [scores(i) for i in range(min(ATTN_LOOKAHEAD, nch))]
        for i in range(nch):
            s = ahead.pop(0)
            if i + ATTN_LOOKAHEAD < nch:
                ahead.append(scores(i + ATTN_LOOKAHEAD))
            if diagonal:
                s = jnp.where(key_i <= qry_i, s, NEG_INF)
            m, l, acc = carry[3 * i:3 * i + 3]
            m_new = jnp.maximum(m, jnp.max(s, axis=0, keepdims=True))
            alpha = jnp.exp2(m - m_new)
            p = jnp.exp2(s - m_new).astype(BF16)
            new += [m_new, alpha * l + _dot(ones, p)[0:1],
                    alpha * acc + _dot(vt_ref[0, j, src[i] * ow:(src[i] + 1) * ow, :], p)]
        return tuple(new)

    init = (jnp.full((1, tq), NEG_INF, F32), jnp.zeros((1, tq), F32),
            jnp.zeros((ow, tq), F32)) * nch
    carry = lax.fori_loop(0, qi, lambda j, cr: step(j, cr, False), init)
    carry = step(qi, carry, True)
    outs = []
    for h in range(hps):
        c0 = 3 * n_sm * h
        o = carry[c0 + 2] / carry[c0 + 1]
        if n_sm == 2:
            o = o - lam_ref[0] * (carry[c0 + 5] / carry[c0 + 4])
            o = o * lax.rsqrt(jnp.mean(o * o, axis=0, keepdims=True) + EPS) * g_ref[...] * out_scale
        outs.append(o.T.astype(o_ref.dtype))
    o_ref[0] = jnp.concatenate(outs, axis=-1)


def causal_attention(q, k, vt, *, n_sm, lam=None, g=None, out_scale=1.0, hps=4, name="causal_attention"):
    b, s, wtot = q.shape
    tq = vt.shape[3]
    width = hps * LANES
    groups = wtot // width
    ow = vt.shape[2] // (wtot // LANES)
    lam = jnp.zeros((1,), F32) if lam is None else lam.reshape(1).astype(F32)
    g = jnp.ones((ow, 1), F32) if g is None else g.reshape(ow, 1).astype(F32)
    seq = pl.BlockSpec((1, s, width), lambda i, h, t: (i, 0, h))
    tile = pl.BlockSpec((1, tq, width), lambda i, h, t: (i, t, h))
    return pl.pallas_call(
        functools.partial(_causal_attn_kernel, n_sm=n_sm, tq=tq, ow=ow, out_scale=out_scale),
        grid=(b, groups, s // tq),
        in_specs=[pl.BlockSpec(memory_space=pltpu.SMEM), tile, seq,
                  pl.BlockSpec((1, s // tq, hps * ow, tq), lambda i, h, t: (i, 0, h, 0)),
                  pl.BlockSpec((ow, 1), lambda i, h, t: (0, 0))],
        out_specs=pl.BlockSpec((1, tq, hps * ow), lambda i, h, t: (i, t, h)),
        out_shape=jax.ShapeDtypeStruct((b, s, groups * hps * ow), BF16),
        compiler_params=_params("parallel", "parallel", "arbitrary"),
        name=name,
    )(lam, q, k, vt, g)


def _memx_kernel(x_ref, g_ref, wq_ref, kv_ref, gq_ref, gk_ref, wo_ref, o_ref):
    x = x_ref[0]
    q = _dot(_rms(x, g_ref[...]).astype(BF16), wq_ref[...])
    kv = kv_ref[0]
    outs = []
    for h in range(MEM_HEADS):
        sl = slice(h * MEM_HEAD_DIM, (h + 1) * MEM_HEAD_DIM)
        qh = (_rms(q[:, sl], gq_ref[...]) * MEM_HEAD_DIM ** -0.5).astype(BF16)
        kh = _rms(kv[:, sl], gk_ref[...]).astype(BF16)
        vh = kv[:, MEM_W + h * MEM_HEAD_DIM:MEM_W + (h + 1) * MEM_HEAD_DIM].astype(BF16)
        s = _dot_t(qh, kh)
        p = jnp.exp(s - jnp.max(s, axis=-1, keepdims=True))
        outs.append(_dot(p.astype(BF16), vh) / jnp.sum(p, axis=-1, keepdims=True))
    o_ref[0] = x + _dot(jnp.concatenate(outs, axis=-1).astype(BF16), wo_ref[...])


def mem_cross_attention(x, mem_kv, g, wq, gq, gk, wo, *, tm=512):
    b, s, d = x.shape
    m = mem_kv.shape[1]
    tm = min(tm, s)
    const = lambda shape: pl.BlockSpec(shape, lambda i, t: (0,) * len(shape))
    tile = pl.BlockSpec((1, tm, d), lambda i, t: (i, t, 0))
    return pl.pallas_call(
        _memx_kernel,
        grid=(b, s // tm),
        in_specs=[tile, const((1, d)), const((d, MEM_W)),
                  pl.BlockSpec((1, m, 2 * MEM_W), lambda i, t: (i, 0, 0)),
                  const((1, MEM_HEAD_DIM)), const((1, MEM_HEAD_DIM)), const((MEM_W, d))],
        out_specs=tile,
        out_shape=jax.ShapeDtypeStruct((b, s, d), F32),
        compiler_params=_params("parallel", "arbitrary"),
        name="mem_cross_attention",
    )(x, g.reshape(1, d).astype(F32), wq, mem_kv, gq.reshape(1, -1).astype(F32),
      gk.reshape(1, -1).astype(F32), wo)


def _rope_tables(positions, dim, lead_ones, tail):
    inv = 1.0 / (ROPE_THETA ** (jnp.arange(0, dim, 2, dtype=F32) / dim))
    ang = positions.astype(F32)[..., None] * inv
    c, s = jnp.cos(ang), jnp.sin(ang)
    shape = positions.shape
    cos = jnp.concatenate([jnp.ones(shape + (lead_ones,), F32), c, c, jnp.ones(shape + (tail,), F32)], axis=-1)
    sin = jnp.concatenate([jnp.zeros(shape + (lead_ones,), F32), -s, s, jnp.zeros(shape + (tail,), F32)], axis=-1)
    reps = LANES // cos.shape[-1]
    return jnp.tile(cos, (1, 1, reps)), jnp.tile(sin, (1, 1, reps))


def _pad_cols(w, cols):
    return jnp.pad(w, ((0, 0), (0, cols - w.shape[1])))


def _ab_in_layout(w):
    c = (SWA_HEADS + 2 * SWA_KV_HEADS) * HEAD_DIM + 3 * RWKV_DIM
    return jnp.concatenate([w[:, :c], _pad_cols(w[:, c:c + DECAY_LORA], LANES),
                            _pad_cols(w[:, c + DECAY_LORA:c + DECAY_LORA + AAA_LORA], LANES),
                            _pad_cols(w[:, c + DECAY_LORA + AAA_LORA:], 2 * LANES)], axis=1)


def _cd_in_layout(w):
    c1 = MLA_Q_RANK + MLA_KV_RANK
    z = lambda n: jnp.zeros((w.shape[0], n), w.dtype)
    return jnp.concatenate([w[:, :c1], z(MLA_NOPE), w[:, c1:c1 + MLA_ROPE], z(LANES - MLA_NOPE - MLA_ROPE),
                            z(LANES), w[:, c1 + MLA_ROPE:]], axis=1)


def _head_slabs(w, per_head):
    k = w.shape[0]
    return jnp.pad(w.reshape(k, -1, per_head), ((0, 0), (0, 0), (0, LANES - per_head))).reshape(k, -1)


def _slab_vec(*parts):
    v = jnp.concatenate([p.astype(F32) for p in parts])
    return jnp.pad(v, (0, LANES - v.shape[0])).reshape(1, LANES)


def kernel(x, mem, positions, ffn1_norm, ffn1_w_gate, ffn1_w_up, ffn1_w_down, mix_norm, ab_w_in, ab_w_out, swa_q_norm, swa_k_norm, swa_sinks, rwkv_mu, rwkv_w0, rwkv_w2, rwkv_a0, rwkv_a2, rwkv_g2, rwkv_k_k, rwkv_k_a, rwkv_r_k, rwkv_gn_g, rwkv_gn_b, cd_w_in, cd_w_out, mla_cq_norm, mla_ckv_norm, mla_w_uq, mla_w_ukv, mla_q_nope_norm, mla_k_nope_norm, mla_q_rope_norm, mla_k_rope_norm, diff_q_norm, diff_k_norm, diff_lq1, diff_lk1, diff_lq2, diff_lk2, diff_subln, memx_norm, memx_w_q, memx_q_norm, memx_w_o, mem_norm, mem_w_kv, mem_k_norm, ffn2_norm, ffn2_w_gate, ffn2_w_up, ffn2_w_down):
    b, s, d = x.shape
    m = mem.shape[1]
    t = b * s
    depth = ffn1_norm.shape[0]
    bf = lambda w: w.astype(BF16)
    cos64, sin64 = _rope_tables(positions, HEAD_DIM, 0, 0)
    cos32, sin32 = _rope_tables(positions, MLA_ROPE, MLA_NOPE, LANES - MLA_NOPE - MLA_ROPE)

    mem_kv = norm_matmul(mem.reshape(b * m, d), mem_norm, bf(mem_w_kv), name="mem_kv").reshape(b, m, 2 * MEM_W)

    x = x.reshape(t, d)
    for layer in range(depth):
        j = layer // 2
        x = ffn(x, ffn1_norm[layer], ffn1_w_gate, ffn1_w_up, ffn1_w_down, layer, name="ffn1")
        if layer % 2 == 0:
            u = norm_matmul(x, mix_norm[layer], _ab_in_layout(bf(ab_w_in[j])), name="ab_in")
            u = u.reshape(b, s, AB_IN_PAD)
            y_a = swa_attention(u, cos64, sin64, _slab_vec(swa_q_norm[j], swa_q_norm[j]),
                                _slab_vec(swa_k_norm[j], swa_k_norm[j]), swa_sinks[j])
            y_b = rwkv_mix(u, rwkv_mu[j], rwkv_w0[j], rwkv_w2[j], rwkv_a0[j], rwkv_a2[j], rwkv_g2[j],
                           rwkv_k_k[j], rwkv_k_a[j], rwkv_r_k[j], rwkv_gn_g[j], rwkv_gn_b[j])
            w_out = bf(ab_w_out[j])
            half = SWA_HEADS * HEAD_DIM
            x = matmul_residual(x, [y_a.reshape(t, -1), y_b.reshape(t, -1)], [w_out[:half], w_out[half:]],
                                name="ab_out")
        else:
            u = norm_matmul(x, mix_norm[layer], _cd_in_layout(bf(cd_w_in[j])), name="cd_in")
            u = u.reshape(b, s, CD_IN_PAD)
            zero64 = jnp.zeros((MLA_NOPE,), F32)
            q_c, k_c, vt_c = mla_prep(u, cos32, sin32,
                                      _slab_vec(mla_q_nope_norm[j], mla_q_rope_norm[j]),
                                      _slab_vec(mla_k_nope_norm[j]),
                                      _slab_vec(zero64, mla_k_rope_norm[j]),
                                      mla_cq_norm[j], mla_ckv_norm[j],
                                      bf(_head_slabs(mla_w_uq[j], MLA_NOPE + MLA_ROPE)), bf(mla_w_ukv[j]))
            y_c = causal_attention(q_c, k_c, vt_c, n_sm=1, hps=8, name="mla_attention")
            q_d, k_d, vt_d = diff_prep(u, cos64, sin64, _slab_vec(diff_q_norm[j], diff_q_norm[j]),
                                       _slab_vec(diff_k_norm[j], diff_k_norm[j]))
            lambda_init = 0.8 - 0.6 * math.exp(-0.3 * layer)
            lam = (jnp.exp(jnp.sum(diff_lq1[j].astype(F32) * diff_lk1[j].astype(F32)))
                   - jnp.exp(jnp.sum(diff_lq2[j].astype(F32) * diff_lk2[j].astype(F32))) + lambda_init)
            y_d = causal_attention(q_d, k_d, vt_d, n_sm=2, hps=4, lam=lam, g=diff_subln[j],
                                   out_scale=1.0 - lambda_init, name="diff_attention")
            w_out = bf(cd_w_out[j])
            n_c = MLA_HEADS * MLA_V
            x = matmul_residual(x, [y_c.reshape(t, -1), y_d.reshape(t, -1)], [w_out[:n_c], w_out[n_c:]],
                                name="cd_out")
        x = mem_cross_attention(x.reshape(b, s, d), mem_kv, memx_norm[layer], bf(memx_w_q[layer]),
                                memx_q_norm[layer], mem_k_norm, bf(memx_w_o[layer])).reshape(t, d)
        x = ffn(x, ffn2_norm[layer], ffn2_w_gate, ffn2_w_up, ffn2_w_down, layer, name="ffn2")
    return x.reshape(b, s, d)
```

```python
import functools
import math

import jax
import jax.numpy as jnp
from jax import lax
from jax.experimental import pallas as pl
from jax.experimental.pallas import tpu as pltpu

F32 = jnp.float32
BF16 = jnp.bfloat16

EPS = 1e-6
ROPE_THETA = 10000.0
NEG_INF = -1e30
LOG2E = math.log2(math.e)
ATTN_LOOKAHEAD = 8
ATTN_TILE = 256
SEG_W = 256
LANES = 128

D_MODEL = 2048
D_FF = 5632
HEAD_DIM = 64
SWA_HEADS = 16
SWA_KV_HEADS = 4
SWA_BLOCK = 128
RWKV_DIM = 1024
RWKV_CHUNK = 64
RWKV_GN_EPS = 64e-5
DECAY_LORA, AAA_LORA, GATE_LORA = 64, 64, 160
MLA_HEADS, MLA_Q_RANK, MLA_KV_RANK, MLA_NOPE, MLA_ROPE, MLA_V = 16, 512, 256, 64, 32, 64
DIFF_HEADS, DIFF_QK, DIFF_V = 8, 64, 128
MEM_HEADS, MEM_HEAD_DIM = 4, 128
MEM_W = MEM_HEADS * MEM_HEAD_DIM
AB_IN_PAD = 5120
CD_IN_PAD = 4096

VMEM_LIMIT = 48 * 1024 * 1024
FFN_VMEM_LIMIT = 60 * 1024 * 1024


def _params(*sem):
    return pltpu.CompilerParams(dimension_semantics=sem, vmem_limit_bytes=VMEM_LIMIT)


def _dot(a, b):
    return jnp.dot(a, b, preferred_element_type=F32)


def _dot_t(a, b):
    return lax.dot_general(a, b, (((1,), (1,)), ((), ())), preferred_element_type=F32)


def _dot_0(a, b):
    return lax.dot_general(a, b, (((0,), (0,)), ((), ())), preferred_element_type=F32)


def _rms(x, g):
    return x * lax.rsqrt(jnp.mean(x * x, axis=-1, keepdims=True) + EPS) * g


def _seg_sum(x, e):
    hi = x.astype(BF16)
    lo = (x - hi.astype(F32)).astype(BF16)
    w = e.shape[0]
    parts = [_dot(hi[:, i:i + w], e) + _dot(lo[:, i:i + w], e) for i in range(0, x.shape[1], w)]
    return parts[0] if len(parts) == 1 else jnp.concatenate(parts, axis=-1)


def _tile_lanes(v, width):
    return v if v.shape[-1] == width else jnp.tile(v, (1, width // v.shape[-1]))


def _store_transposed_tiles(vt_ref, v):
    for r in range(vt_ref.shape[1]):
        vt_ref[0, r] = v[r * ATTN_TILE:(r + 1) * ATTN_TILE, :].T.astype(vt_ref.dtype)


def _swap_halves(x, half):
    w = x.shape[-1]
    lane = lax.broadcasted_iota(jnp.int32, x.shape, x.ndim - 1)
    low = (lane & (2 * half - 1)) < half
    return jnp.where(low, pltpu.roll(x, w - half, x.ndim - 1), pltpu.roll(x, half, x.ndim - 1))


def _norm_matmul_kernel(x_ref, g_ref, w_ref, o_ref, xn_ref):
    @pl.when(pl.program_id(1) == 0)
    def _():
        xn_ref[...] = _rms(x_ref[...].astype(F32), g_ref[...]).astype(BF16)

    o_ref[...] = _dot(xn_ref[...], w_ref[...]).astype(o_ref.dtype)


def norm_matmul(x, g, w, *, k_blk=0, out_dtype=F32, tm=1024, tn=512, name="norm_matmul"):
    t = x.shape[0]
    k, n = w.shape
    tm, tn = min(tm, t), min(tn, n)
    assert t % tm == 0 and n % tn == 0
    return pl.pallas_call(
        _norm_matmul_kernel,
        grid=(t // tm, n // tn),
        in_specs=[pl.BlockSpec((tm, k), lambda i, j: (i, k_blk)),
                  pl.BlockSpec((1, k), lambda i, j: (0, 0)),
                  pl.BlockSpec((k, tn), lambda i, j: (0, j))],
        out_specs=pl.BlockSpec((tm, tn), lambda i, j: (i, j)),
        out_shape=jax.ShapeDtypeStruct((t, n), out_dtype),
        scratch_shapes=[pltpu.VMEM((tm, k), BF16)],
        compiler_params=_params("parallel", "arbitrary"),
        name=name,
    )(x, g.reshape(1, k).astype(F32), w)


def _matmul_res_kernel(*refs, n_in):
    x_ref, o_ref = refs[2 * n_in], refs[2 * n_in + 1]
    acc = x_ref[...]
    for i in range(n_in):
        acc = acc + _dot(refs[i][...], refs[n_in + i][...])
    o_ref[...] = acc


def matmul_residual(x, a_list, w_list, *, tm=256, name="matmul_residual"):
    t, n = x.shape
    tm = min(tm, t)
    assert t % tm == 0
    n_in = len(a_list)
    in_specs = ([pl.BlockSpec((tm, a.shape[1]), lambda i: (i, 0)) for a in a_list]
                + [pl.BlockSpec(w.shape, lambda i: (0, 0)) for w in w_list]
                + [pl.BlockSpec((tm, n), lambda i: (i, 0))])
    return pl.pallas_call(
        functools.partial(_matmul_res_kernel, n_in=n_in),
        grid=(t // tm,),
        in_specs=in_specs,
        out_specs=pl.BlockSpec((tm, n), lambda i: (i, 0)),
        out_shape=jax.ShapeDtypeStruct((t, n), F32),
        compiler_params=_params("parallel"),
        name=name,
    )(*a_list, *w_list, x)


def _ffn_kernel(x_ref, g_ref, wg_ref, wu_ref, wd_ref, o_ref, xn_ref, *, nf):
    f = pl.program_id(1)

    @pl.when(f == 0)
    def _():
        x = x_ref[...]
        xn_ref[...] = _rms(x, g_ref[...]).astype(BF16)
        o_ref[...] = 2.0 * x

    xn = xn_ref[...]
    a = _dot(xn, wg_ref[...].astype(BF16))
    b = _dot(xn, wu_ref[...].astype(BF16))
    h = (a * (1.0 / (1.0 + jnp.exp(-a))) * b).astype(BF16)
    o_ref[...] += _dot(h, wd_ref[...].astype(BF16))

    @pl.when(f == nf - 1)
    def _():
        o_ref[...] = 0.5 * o_ref[...]


def ffn(x, g, wg, wu, wd, layer, *, tm=1024, tf=256, name="ffn"):
    t, d = x.shape
    ff = wg.shape[2]
    tm = min(tm, t)
    assert t % tm == 0 and ff % tf == 0
    nf = ff // tf
    return pl.pallas_call(
        functools.partial(_ffn_kernel, nf=nf),
        grid=(t // tm, nf),
        in_specs=[pl.BlockSpec((tm, d), lambda i, f: (i, 0)),
                  pl.BlockSpec((1, d), lambda i, f: (0, 0)),
                  pl.BlockSpec((None, d, tf), lambda i, f: (layer, 0, f)),
                  pl.BlockSpec((None, d, tf), lambda i, f: (layer, 0, f)),
                  pl.BlockSpec((None, tf, d), lambda i, f: (layer, f, 0))],
        out_specs=pl.BlockSpec((tm, d), lambda i, f: (i, 0)),
        out_shape=jax.ShapeDtypeStruct((t, d), F32),
        scratch_shapes=[pltpu.VMEM((tm, d), BF16)],
        compiler_params=pltpu.CompilerParams(dimension_semantics=("parallel", "arbitrary"),
                                             vmem_limit_bytes=FFN_VMEM_LIMIT),
        name=name,
    )(x, g.reshape(1, d).astype(F32), wg, wu, wd)


def _swa_kernel(sink_ref, q_ref, kc_ref, kp_ref, vc_ref, vp_ref, cc_ref, sc_ref, cp_ref, sp_ref,
                gq_ref, gk_ref, e_ref, o_ref):
    n = pl.program_id(1)
    blk = SWA_BLOCK
    group = SWA_HEADS // SWA_KV_HEADS
    q = q_ref[0]
    k = jnp.concatenate([kp_ref[0], kc_ref[0]], axis=0)
    v = jnp.concatenate([vp_ref[0], vc_ref[0]], axis=0)
    cos_q, sin_q = cc_ref[0], sc_ref[0]
    cos_k = jnp.concatenate([cp_ref[0], cos_q], axis=0)
    sin_k = jnp.concatenate([sp_ref[0], sin_q], axis=0)
    def prep(x, g, cos, sin, scale):
        w = x.shape[1]
        xg = x * _tile_lanes(g, w)
        xr = xg * _tile_lanes(cos, w) + _swap_halves(xg, HEAD_DIM // 2) * _tile_lanes(sin, w)
        return xr * lax.rsqrt(_seg_sum(x * x, e_ref[...]) * (1.0 / HEAD_DIM) + EPS) * scale

    qr = prep(q, gq_ref[...], cos_q, sin_q, HEAD_DIM ** -0.5 * LOG2E).astype(BF16)
    kr = prep(k, gk_ref[...], cos_k, sin_k, 1.0)
    key_i = lax.broadcasted_iota(jnp.int32, (2 * blk, blk), 0)
    qry_i = lax.broadcasted_iota(jnp.int32, (2 * blk, blk), 1)
    rel = qry_i + blk - key_i
    valid = (rel >= 0) & (rel < blk) & ((n > 0) | (key_i >= blk))
    low = lax.broadcasted_iota(jnp.int32, (blk, LANES), 1) < HEAD_DIM
    zero = jnp.zeros((blk, LANES), BF16)
    k_dup, v_t = [], []
    for g in range(SWA_KV_HEADS):
        kg = kr[:, g * HEAD_DIM:(g + 1) * HEAD_DIM]
        k_dup.append(jnp.concatenate([kg, kg], axis=-1).astype(BF16))
        v_t.append(v[:, g * HEAD_DIM:(g + 1) * HEAD_DIM].T.astype(BF16))

    def scores(h):
        slab = qr[:, (h // 2) * LANES:(h // 2 + 1) * LANES]
        qh = jnp.where(low, slab, zero) if h % 2 == 0 else jnp.where(low, zero, slab)
        return _dot_t(k_dup[h // group], qh)

    ahead = [scores(h) for h in range(ATTN_LOOKAHEAD)]
    ones = jnp.ones((8, 2 * blk), BF16)
    outs = []
    for h in range(SWA_HEADS):
        s = jnp.where(valid, ahead.pop(0), NEG_INF)
        if h + ATTN_LOOKAHEAD < SWA_HEADS:
            ahead.append(scores(h + ATTN_LOOKAHEAD))
        sink = sink_ref[h] * LOG2E
        m = jnp.maximum(jnp.max(s, axis=0, keepdims=True), sink)
        p = jnp.exp2(s - m).astype(BF16)
        den = _dot(ones, p)[0:1] + jnp.exp2(sink - m)
        outs.append(_dot(v_t[h // group], p) / den)
    slabs = [jnp.concatenate(outs[i:i + 2], axis=0).T for i in range(0, SWA_HEADS, 2)]
    o_ref[0] = jnp.concatenate(slabs, axis=-1).astype(o_ref.dtype)


def swa_attention(u, cos, sin, gq, gk, sinks):
    b, s, _ = u.shape
    nb = s // SWA_BLOCK
    qw, kw = SWA_HEADS * HEAD_DIM, SWA_KV_HEADS * HEAD_DIM
    cur = lambda c: (lambda i, n: (i, n, c))
    prev = lambda c: (lambda i, n: (i, jnp.maximum(n - 1, 0), c))
    tab = pl.BlockSpec((1, SWA_BLOCK, LANES), cur(0))
    tab_prev = pl.BlockSpec((1, SWA_BLOCK, LANES), prev(0))
    gain = pl.BlockSpec((1, LANES), lambda i, n: (0, 0))
    seg = (jnp.arange(kw)[:, None] // HEAD_DIM == jnp.arange(kw)[None, :] // HEAD_DIM).astype(BF16)
    return pl.pallas_call(
        _swa_kernel,
        grid=(b, nb),
        in_specs=[pl.BlockSpec(memory_space=pltpu.SMEM),
                  pl.BlockSpec((1, SWA_BLOCK, qw), cur(0)),
                  pl.BlockSpec((1, SWA_BLOCK, kw), cur(qw // kw)),
                  pl.BlockSpec((1, SWA_BLOCK, kw), prev(qw // kw)),
                  pl.BlockSpec((1, SWA_BLOCK, kw), cur(qw // kw + 1)),
                  pl.BlockSpec((1, SWA_BLOCK, kw), prev(qw // kw + 1)),
                  tab, tab, tab_prev, tab_prev, gain, gain, pl.BlockSpec((kw, kw), lambda i, n: (0, 0))],
        out_specs=pl.BlockSpec((1, SWA_BLOCK, qw), cur(0)),
        out_shape=jax.ShapeDtypeStruct((b, s, qw), BF16),
        compiler_params=_params("parallel", "arbitrary"),
        name="swa_attention",
    )(sinks.astype(F32), u, u, u, u, u, cos, sin, cos, sin, gq, gk, seg)


def _mm(a, b, dims, passes):
    dn = (dims, ((), ()))
    dg = lambda x, y: lax.dot_general(x, y, dn, preferred_element_type=F32)
    ah = a.astype(BF16)
    bh = b.astype(BF16)
    if passes == 1:
        return dg(ah, bh)
    al = (a - ah.astype(F32)).astype(BF16)
    bl = (b - bh.astype(F32)).astype(BF16)
    return dg(ah, bh) + dg(ah, bl) + dg(al, bh)


_NN = ((1,), (0,))
_NT = ((1,), (1,))
_TN = ((0,), (0,))
P_SC, P_INV, P_PQ, P_OUT, P_ST = 1, 1, 1, 1, 1
RWKV_UNROLL = 8


RWKV_PREP_ROWS = 256


def _rwkv_kernel(ur_ref, uk_ref, uv_ref, ul_ref, mur_ref, muk_ref, muv_ref, mul_ref, w0_ref, w2_ref, a0_ref,
                 a2_ref, g2_ref, kkw_ref, ka_ref, gng_ref, gnb_ref, rk_ref,
                 o_ref, r_ref, k_ref, v_ref, kk_ref, b_ref, lw_ref, g_ref,
                 st_ref, y1_ref, y0_ref, n_ref, z_ref, dec_ref, *, nchunk):
    c = RWKV_CHUNK
    lane_c = lax.broadcasted_iota(jnp.int32, (c, LANES), 1)
    head0 = lane_c < HEAD_DIM
    ri = lax.broadcasted_iota(jnp.int32, (2 * c, 2 * c), 0)
    ci = lax.broadcasted_iota(jnp.int32, (2 * c, 2 * c), 1)
    eye = jnp.where(ri == ci, 1.0, 0.0)
    tril_c = jnp.where(lax.broadcasted_iota(jnp.int32, (c, c), 0) >= lax.broadcasted_iota(jnp.int32, (c, c), 1),
                       1.0, 0.0).astype(BF16)
    stack = lambda x: jnp.concatenate([jnp.where(head0, x, 0.0), jnp.where(head0, 0.0, x)], axis=0)

    def seg_mean(x):
        first = lax.broadcasted_iota(jnp.int32, x.shape, 1) < HEAD_DIM
        m0 = jnp.sum(jnp.where(first, x, 0.0), axis=-1, keepdims=True)
        m1 = jnp.sum(jnp.where(first, 0.0, x), axis=-1, keepdims=True)
        return jnp.where(first, m0, m1) * (1.0 / HEAD_DIM)

    def prep(io, first_group):
        group_rows = RWKV_UNROLL * c
        for tix in range(group_rows // RWKV_PREP_ROWS):
            start = pl.multiple_of(io * group_rows + tix * RWKV_PREP_ROWS, RWKV_PREP_ROWS)
            rows = pl.ds(start, RWKV_PREP_ROWS)
            at_start = first_group and tix == 0

            def shifted(ref, mu_ref):
                x = ref[0, rows, :]
                if at_start:
                    last = jnp.zeros((1, x.shape[1]), F32)
                else:
                    last = ref[0, pl.ds(pl.multiple_of(start - 8, 8), 8), :][7:8, :]
                row = lax.broadcasted_iota(jnp.int32, x.shape, 0)
                prev = jnp.where(row == 0, last, pltpu.roll(x, 1, 0))
                return x + (prev - x) * mu_ref[...]

            r = shifted(ur_ref, mur_ref)
            k = shifted(uk_ref, muk_ref)
            v = shifted(uv_ref, muv_ref)
            lo = shifted(ul_ref, mul_ref)
            yield
            w_lo, a_lo, g_lo = lo[:, 0:LANES], lo[:, LANES:2 * LANES], lo[:, 2 * LANES:4 * LANES]
            z = -(w0_ref[...] + _mm(jnp.tanh(w_lo), w2_ref[...], _NN, 3))
            w = -(jnp.maximum(z, 0.0) + jnp.log(1.0 + jnp.exp(-jnp.abs(z)))) - 0.5
            a = 1.0 / (1.0 + jnp.exp(-(a0_ref[...] + _mm(a_lo, a2_ref[...], _NN, 1))))
            g = _mm(1.0 / (1.0 + jnp.exp(-g_lo)), g2_ref[...], _NN, 1)
            yield
            kk = k * kkw_ref[...]
            kk = kk / jnp.maximum(jnp.sqrt(seg_mean(kk * kk) * float(HEAD_DIM)), 1e-12)
            r_ref[rows, :] = r
            k_ref[rows, :] = k * (1.0 + (a - 1.0) * ka_ref[...])
            v_ref[rows, :] = v
            kk_ref[rows, :] = kk
            b_ref[rows, :] = kk * a
            lw_ref[rows, :] = -jnp.exp(w)
            g_ref[rows, :] = g
            yield

    def build(ics):
        each = lambda f, *cols: [f(*args) for args in zip(*cols)]
        sls = [pl.ds(pl.multiple_of(ic * c, c), c) for ic in ics]
        load = lambda ref: [ref[sl, :] for sl in sls]
        r, k, v, kk, b, lw = (load(ref) for ref in (r_ref, k_ref, v_ref, kk_ref, b_ref, lw_ref))

        def running_sum(x):
            l1 = x.astype(BF16)
            rest = x - l1.astype(F32)
            l2 = rest.astype(BF16)
            l3 = (rest - l2.astype(F32)).astype(BF16)
            return _dot(tril_c, l1) + _dot(tril_c, l2) + _dot(tril_c, l3)

        cum = each(running_sum, lw)
        yield
        cum_end = [x[c - 1:c, :] for x in cum]
        e_neg = each(lambda x: jnp.exp(-x), cum)
        e_end = each(lambda x, xe: jnp.exp(xe - x), cum, cum_end)
        a_s = each(lambda kk_, x, l: stack(-kk_ * jnp.exp(x - l)), kk, cum, lw)
        r_s = each(lambda r_, x: stack(r_ * jnp.exp(x)), r, cum)
        b_s = each(lambda b_, e: stack(b_ * e), b, e_neg)
        k_s = each(lambda k_, e: stack(k_ * e), k, e_neg)
        bh_s = each(lambda b_, e: stack(b_ * e), b, e_end)
        kh_s = each(lambda k_, e: stack(k_ * e), k, e_end)
        v_s = each(stack, v)
        n2 = 2 * c
        sc = each(lambda a_, r_, b_, k_: _mm(jnp.concatenate([a_, r_], axis=0),
                                             jnp.concatenate([b_, k_], axis=0), _NT, P_SC), a_s, r_s, b_s, k_s)
        low = [jnp.where(ri > ci, x[:n2, :n2], 0.0) for x in sc]
        a_ak = [jnp.where(ri > ci, x[:n2, n2:], 0.0) for x in sc]
        a_rb = [jnp.where(ri >= ci, x[n2:, :n2], 0.0) for x in sc]
        a_rk = [jnp.where(ri >= ci, x[n2:, n2:], 0.0) for x in sc]
        yield
        inv = [eye + x for x in low]
        pw = each(lambda x: _mm(x, x, _NN, P_INV), low)
        yield
        levels = 5
        for lvl in range(levels - 1):
            both = each(lambda t, x: _mm(jnp.concatenate([t, x], axis=0), x, _NN, P_INV), inv, pw)
            inv = each(lambda t, r: t + r[:n2], inv, both)
            pw = [r[n2:] for r in both]
            yield
        inv = each(lambda t, x: t + _mm(t, x, _NN, P_INV), inv, pw)
        yield
        akv = each(lambda x, y: _mm(x, y, _NN, P_PQ), a_ak, v_s)
        yield
        pq = each(lambda t, x, y: _mm(t, jnp.concatenate([x, y], axis=1), _NN, P_PQ), inv, a_s, akv)
        yield
        yy = each(lambda x, y: _mm(x, y, _NN, P_OUT), a_rb, pq)
        y0b = each(lambda x, y: _mm(x, y, _NN, P_OUT), a_rk, v_s)
        nz = each(lambda x, y: _mm(x, y, _TN, P_OUT), pq, bh_s)
        zb = each(lambda x, y: _mm(x, y, _TN, P_OUT), v_s, kh_s)
        for i, ic in enumerate(ics):
            y1_ref[ic] = r_s[i] + yy[i][:, :LANES]
            y0_ref[ic] = yy[i][:, LANES:] + y0b[i]
            n_ref[ic] = nz[i][:LANES]
            z_ref[ic] = nz[i][LANES:] + zb[i]
            dec_ref[ic] = jnp.broadcast_to(jnp.exp(cum_end[i]), (8, LANES))

    def emit(ic, st):
        sl = pl.ds(pl.multiple_of(ic * c, c), c)
        r, k, v, g = r_ref[sl, :], k_ref[sl, :], v_ref[sl, :], g_ref[sl, :]
        y_st = _mm(y1_ref[ic], st, _NT, P_ST) + y0_ref[ic]
        y = y_st[0:c] + y_st[c:2 * c]
        mean = seg_mean(y)
        var = seg_mean((y - mean) * (y - mean))
        yn = (y - mean) * lax.rsqrt(var + RWKV_GN_EPS) * gng_ref[...] + gnb_ref[...]
        bonus = seg_mean(r * k * rk_ref[...]) * float(HEAD_DIM) * v
        o_ref[0, sl, :] = ((yn + bonus) * g).astype(o_ref.dtype)
        return st * dec_ref[ic][0:1, :] + _mm(st, n_ref[ic], _NN, P_ST) + z_ref[ic]

    def scan(ics):
        st = st_ref[...]
        for ic in ics:
            st = emit(ic, st)
            yield
        st_ref[...] = st

    def run(*gens):
        live = list(gens)
        while live:
            live = [gen for gen in live if next(gen, live) is not live]

    group = lambda io: [io * RWKV_UNROLL + i for i in range(RWKV_UNROLL)]
    ngroup = nchunk // RWKV_UNROLL
    st_ref[...] = jnp.zeros_like(st_ref)
    run(prep(0, True))
    if ngroup > 1:
        run(prep(1, False), build(group(0)))
    else:
        run(build(group(0)))

    @pl.loop(1, ngroup - 1)
    def _(io):
        run(prep(io + 1, False), build(group(io)), scan(group(io - 1)))

    if ngroup > 1:
        run(build(group(ngroup - 1)), scan(group(ngroup - 2)))
    run(scan(group(ngroup - 1)))


def rwkv_mix(u, mu, w0, w2, a0, a2, g2, k_k, k_a, r_k, gn_g, gn_b):
    bsz, s, _ = u.shape
    npair = RWKV_DIM // LANES
    nchunk = s // RWKV_CHUNK
    assert nchunk % RWKV_UNROLL == 0 and (RWKV_UNROLL * RWKV_CHUNK) % RWKV_PREP_ROWS == 0
    base = (SWA_HEADS + 2 * SWA_KV_HEADS) * HEAD_DIM // LANES
    lora_w = 4 * LANES
    slab = lambda off: pl.BlockSpec((1, s, LANES), lambda i, p: (i, 0, off + p))
    vec = pl.BlockSpec((1, LANES), lambda i, p: (0, p))
    cols = lambda rows: pl.BlockSpec((rows, LANES), lambda i, p: (0, p))
    row = lambda vv: vv.reshape(1, -1).astype(F32)
    pad_rows = lambda m, rows: jnp.pad(m, ((0, rows - m.shape[0]), (0, 0))).astype(F32)
    pad_cols = lambda vv, n: jnp.pad(vv, (0, n - vv.shape[0]))
    c3 = 3 * RWKV_DIM
    mu_l = jnp.concatenate([pad_cols(mu[c3:c3 + DECAY_LORA], LANES),
                            pad_cols(mu[c3 + DECAY_LORA:c3 + DECAY_LORA + AAA_LORA], LANES),
                            pad_cols(mu[c3 + DECAY_LORA + AAA_LORA:], 2 * LANES)])
    seq = pltpu.VMEM((s, LANES), F32)
    mat = pltpu.VMEM((nchunk, LANES, LANES), F32)
    return pl.pallas_call(
        functools.partial(_rwkv_kernel, nchunk=nchunk),
        grid=(bsz, npair),
        in_specs=[slab(base), slab(base + npair), slab(base + 2 * npair),
                  pl.BlockSpec((1, s, lora_w), lambda i, p: (i, 0, (base + 3 * npair) * LANES // lora_w)),
                  vec, vec, vec, pl.BlockSpec((1, lora_w), lambda i, p: (0, 0)),
                  vec, cols(LANES), vec, cols(LANES), cols(2 * LANES), vec, vec, vec, vec, vec],
        out_specs=pl.BlockSpec((1, s, LANES), lambda i, p: (i, 0, p)),
        out_shape=jax.ShapeDtypeStruct((bsz, s, RWKV_DIM), BF16),
        scratch_shapes=[seq] * 7 + [pltpu.VMEM((LANES, LANES), F32), mat, mat, mat, mat,
                                    pltpu.VMEM((nchunk, 8, LANES), F32)],
        compiler_params=_params("parallel", "arbitrary"),
        name="rwkv_mix",
    )(u, u, u, u, row(mu[:RWKV_DIM]), row(mu[RWKV_DIM:2 * RWKV_DIM]), row(mu[2 * RWKV_DIM:c3]), row(mu_l),
      row(w0), pad_rows(w2, LANES), row(a0), pad_rows(a2, LANES), pad_rows(g2, 2 * LANES),
      row(k_k), row(k_a), row(gn_g), row(gn_b), row(r_k))


def _mla_prep_kernel(cq_ref, ckv_ref, pe_ref, cos_ref, sin_ref, e_ref, gq_ref, gkn_ref, gkp_ref, invn_ref,
                     gcq_ref, gckv_ref, wuq_ref, wukv_ref, qo_ref, ko_ref, vo_ref):
    cos, sin = cos_ref[0], sin_ref[0]
    half = MLA_ROPE // 2

    def rope(x, g):
        w = x.shape[1]
        xg = x * _tile_lanes(g, w)
        return xg * _tile_lanes(cos, w) + _swap_halves(xg, half) * _tile_lanes(sin, w)

    x = _dot(_rms(cq_ref[0], gcq_ref[...]).astype(BF16), wuq_ref[...])
    w = x.shape[1]
    inv_n = _tile_lanes(invn_ref[...], w)
    inv = lax.rsqrt(_seg_sum(x * x, e_ref[...]) * inv_n + EPS)
    qo_ref[0] = (rope(x, gq_ref[...]) * inv * ((MLA_NOPE + MLA_ROPE) ** -0.5 * LOG2E)).astype(qo_ref.dtype)
    kv = _dot(_rms(ckv_ref[0], gckv_ref[...]).astype(BF16), wukv_ref[...])
    inv_k = lax.rsqrt(_seg_sum(kv * kv, e_ref[...]) * inv_n + EPS)
    k_nope = kv * inv_k * _tile_lanes(gkn_ref[...], w)
    pe = pe_ref[0]
    inv_pe = lax.rsqrt(jnp.sum(pe * pe, axis=-1, keepdims=True) * (1.0 / MLA_ROPE) + EPS)
    k_pe = rope(pe, gkp_ref[...]) * inv_pe
    ko_ref[0] = (k_nope + _tile_lanes(k_pe, w)).astype(ko_ref.dtype)
    v = jnp.concatenate([kv[:, h * LANES + MLA_NOPE:(h + 1) * LANES] for h in range(w // LANES)], axis=-1)
    _store_transposed_tiles(vo_ref, v)


def mla_prep(u, cos, sin, gq, gkn, gkp, gcq, gckv, w_uq, w_ukv, *, ts=512, tc=512):
    b, s, _ = u.shape
    wtot = w_uq.shape[1]
    ts = min(ts, s)
    lane = jnp.arange(SEG_W)
    same = (lane[:, None] // LANES == lane[None, :] // LANES)
    pos = lane % LANES
    nope = pos < MLA_NOPE
    pe = (pos >= MLA_NOPE) & (pos < MLA_NOPE + MLA_ROPE)
    seg = (same & ((nope[:, None] & nope[None, :]) | (pe[:, None] & pe[None, :]))).astype(BF16)
    p1 = jnp.arange(LANES)
    inv_n = jnp.where(p1 < MLA_NOPE, 1.0 / MLA_NOPE, jnp.where(p1 < MLA_NOPE + MLA_ROPE, 1.0 / MLA_ROPE, 0.0))
    blk = pl.BlockSpec((1, ts, tc), lambda i, t, c: (i, t, c))
    tab = pl.BlockSpec((1, ts, LANES), lambda i, t, c: (i, t, 0))
    vec = pl.BlockSpec((1, LANES), lambda i, t, c: (0, 0))
    pe_blk = (MLA_Q_RANK + MLA_KV_RANK) // LANES
    out = jax.ShapeDtypeStruct((b, s, wtot), BF16)
    return pl.pallas_call(
        _mla_prep_kernel,
        grid=(b, s // ts, wtot // tc),
        in_specs=[pl.BlockSpec((1, ts, MLA_Q_RANK), lambda i, t, c: (i, t, 0)),
                  pl.BlockSpec((1, ts, MLA_KV_RANK), lambda i, t, c: (i, t, MLA_Q_RANK // MLA_KV_RANK)),
                  pl.BlockSpec((1, ts, LANES), lambda i, t, c: (i, t, pe_blk)), tab, tab,
                  pl.BlockSpec((SEG_W, SEG_W), lambda i, t, c: (0, 0)), vec, vec, vec, vec,
                  pl.BlockSpec((1, MLA_Q_RANK), lambda i, t, c: (0, 0)),
                  pl.BlockSpec((1, MLA_KV_RANK), lambda i, t, c: (0, 0)),
                  pl.BlockSpec((MLA_Q_RANK, tc), lambda i, t, c: (0, c)),
                  pl.BlockSpec((MLA_KV_RANK, tc), lambda i, t, c: (0, c))],
        out_specs=[blk, blk, pl.BlockSpec((1, ts // ATTN_TILE, tc // LANES * MLA_V, ATTN_TILE),
                                          lambda i, t, c: (i, t, c, 0))],
        out_shape=[out, out, jax.ShapeDtypeStruct((b, s // ATTN_TILE, wtot // LANES * MLA_V, ATTN_TILE), BF16)],
        compiler_params=_params("parallel", "parallel", "arbitrary"),
        name="mla_prep",
    )(u, u, u, cos, sin, seg, gq, gkn, gkp, inv_n.reshape(1, LANES).astype(F32),
      gcq.reshape(1, -1).astype(F32), gckv.reshape(1, -1).astype(F32), w_uq, w_ukv)


def _diff_prep_kernel(q_ref, k_ref, v_ref, cos_ref, sin_ref, e_ref, gq_ref, gk_ref, qo_ref, ko_ref, vo_ref):
    cos, sin = cos_ref[0], sin_ref[0]
    _store_transposed_tiles(vo_ref, v_ref[0])

    def prep(x, g, scale):
        w = x.shape[1]
        xg = x * _tile_lanes(g, w)
        xr = xg * _tile_lanes(cos, w) + _swap_halves(xg, DIFF_QK // 2) * _tile_lanes(sin, w)
        inv = lax.rsqrt(_seg_sum(x * x, e_ref[...]) * (1.0 / DIFF_QK) + EPS)
        return xr * inv * scale

    qo_ref[0] = prep(q_ref[0], gq_ref[...], DIFF_QK ** -0.5 * LOG2E).astype(qo_ref.dtype)
    ko_ref[0] = prep(k_ref[0], gk_ref[...], 1.0).astype(ko_ref.dtype)


def diff_prep(u, cos, sin, gq, gk, *, ts=512, tc=512):
    b, s, _ = u.shape
    ts = min(ts, s)
    wtot = 2 * DIFF_HEADS * DIFF_QK
    q_base = (CD_IN_PAD - 3 * wtot) // tc
    seg = (jnp.arange(SEG_W)[:, None] // DIFF_QK == jnp.arange(SEG_W)[None, :] // DIFF_QK).astype(BF16)
    blk = lambda off: pl.BlockSpec((1, ts, tc), lambda i, t, c: (i, t, off + c))
    tab = pl.BlockSpec((1, ts, LANES), lambda i, t, c: (i, t, 0))
    vec = pl.BlockSpec((1, LANES), lambda i, t, c: (0, 0))
    out = jax.ShapeDtypeStruct((b, s, wtot), BF16)
    return pl.pallas_call(
        _diff_prep_kernel,
        grid=(b, s // ts, wtot // tc),
        in_specs=[blk(q_base), blk(q_base + wtot // tc), blk(q_base + 2 * wtot // tc), tab, tab,
                  pl.BlockSpec((SEG_W, SEG_W), lambda i, t, c: (0, 0)), vec, vec],
        out_specs=[blk(0), blk(0),
                   pl.BlockSpec((1, ts // ATTN_TILE, tc, ATTN_TILE), lambda i, t, c: (i, t, c, 0))],
        out_shape=[out, out, jax.ShapeDtypeStruct((b, s // ATTN_TILE, wtot, ATTN_TILE), BF16)],
        compiler_params=_params("parallel", "parallel", "arbitrary"),
        name="diff_prep",
    )(u, u, u, cos, sin, seg, gq, gk)


def _causal_attn_kernel(lam_ref, q_ref, k_ref, vt_ref, g_ref, o_ref, *, n_sm, tq, ow, out_scale):
    qi = pl.program_id(2)
    q = q_ref[0]
    hps = q.shape[1] // LANES
    slab = lambda x, h: x[:, h * LANES:(h + 1) * LANES]
    lane = lax.broadcasted_iota(jnp.int32, (tq, LANES), 1)
    qs, src = [], []
    for h in range(hps):
        qh = slab(q, h)
        if n_sm == 2:
            zero = jnp.zeros_like(qh)
            qs += [jnp.where(lane < DIFF_QK, qh, zero), jnp.where(lane < DIFF_QK, zero, qh)]
            src += [h, h]
        else:
            qs.append(qh)
            src.append(h)
    nch = len(qs)
    key_i = lax.broadcasted_iota(jnp.int32, (tq, tq), 0)
    qry_i = lax.broadcasted_iota(jnp.int32, (tq, tq), 1)
    ones = jnp.ones((8, tq), BF16)

    def step(j, carry, diagonal):
        kj = k_ref[0, pl.ds(pl.multiple_of(j * tq, tq), tq), :]
        scores = lambda i: _dot_t(slab(kj, src[i]), qs[i])
        new = []
        ahead = [scores(i) for i in range(min(ATTN_LOOKAHEAD, nch))]
        for i in range(nch):
            s = ahead.pop(0)
            if i + ATTN_LOOKAHEAD < nch:
                ahead.append(scores(i + ATTN_LOOKAHEAD))
            if diagonal:
                s = jnp.where(key_i <= qry_i, s, NEG_INF)
            m, l, acc = carry[3 * i:3 * i + 3]
            m_new = jnp.maximum(m, jnp.max(s, axis=0, keepdims=True))
            alpha = jnp.exp2(m - m_new)
            p = jnp.exp2(s - m_new).astype(BF16)
            new += [m_new, alpha * l + _dot(ones, p)[0:1],
                    alpha * acc + _dot(vt_ref[0, j, src[i] * ow:(src[i] + 1) * ow, :], p)]
        return tuple(new)

    init = (jnp.full((1, tq), NEG_INF, F32), jnp.zeros((1, tq), F32),
            jnp.zeros((ow, tq), F32)) * nch
    carry = lax.fori_loop(0, qi, lambda j, cr: step(j, cr, False), init)
    carry = step(qi, carry, True)
    outs = []
    for h in range(hps):
        c0 = 3 * n_sm * h
        o = carry[c0 + 2] / carry[c0 + 1]
        if n_sm == 2:
            o = o - lam_ref[0] * (carry[c0 + 5] / carry[c0 + 4])
            o = o * lax.rsqrt(jnp.mean(o * o, axis=0, keepdims=True) + EPS) * g_ref[...] * out_scale
        outs.append(o.T.astype(o_ref.dtype))
    o_ref[0] = jnp.concatenate(outs, axis=-1)


def causal_attention(q, k, vt, *, n_sm, lam=None, g=None, out_scale=1.0, hps=4, name="causal_attention"):
    b, s, wtot = q.shape
    tq = vt.shape[3]
    width = hps * LANES
    groups = wtot // width
    ow = vt.shape[2] // (wtot // LANES)
    lam = jnp.zeros((1,), F32) if lam is None else lam.reshape(1).astype(F32)
    g = jnp.ones((ow, 1), F32) if g is None else g.reshape(ow, 1).astype(F32)
    seq = pl.BlockSpec((1, s, width), lambda i, h, t: (i, 0, h))
    tile = pl.BlockSpec((1, tq, width), lambda i, h, t: (i, t, h))
    return pl.pallas_call(
        functools.partial(_causal_attn_kernel, n_sm=n_sm, tq=tq, ow=ow, out_scale=out_scale),
        grid=(b, groups, s // tq),
        in_specs=[pl.BlockSpec(memory_space=pltpu.SMEM), tile, seq,
                  pl.BlockSpec((1, s // tq, hps * ow, tq), lambda i, h, t: (i, 0, h, 0)),
                  pl.BlockSpec((ow, 1), lambda i, h, t: (0, 0))],
        out_specs=pl.BlockSpec((1, tq, hps * ow), lambda i, h, t: (i, t, h)),
        out_shape=jax.ShapeDtypeStruct((b, s, groups * hps * ow), BF16),
        compiler_params=_params("parallel", "parallel", "arbitrary"),
        name=name,
    )(lam, q, k, vt, g)


def _memx_kernel(x_ref, g_ref, wq_ref, kv_ref, gq_ref, gk_ref, wo_ref, o_ref):
    x = x_ref[0]
    q = _dot(_rms(x, g_ref[...]).astype(BF16), wq_ref[...])
    kv = kv_ref[0]
    outs = []
    for h in range(MEM_HEADS):
        sl = slice(h * MEM_HEAD_DIM, (h + 1) * MEM_HEAD_DIM)
        qh = (_rms(q[:, sl], gq_ref[...]) * MEM_HEAD_DIM ** -0.5).astype(BF16)
        kh = _rms(kv[:, sl], gk_ref[...]).astype(BF16)
        vh = kv[:, MEM_W + h * MEM_HEAD_DIM:MEM_W + (h + 1) * MEM_HEAD_DIM].astype(BF16)
        s = _dot_t(qh, kh)
        p = jnp.exp(s - jnp.max(s, axis=-1, keepdims=True))
        outs.append(_dot(p.astype(BF16), vh) / jnp.sum(p, axis=-1, keepdims=True))
    o_ref[0] = x + _dot(jnp.concatenate(outs, axis=-1).astype(BF16), wo_ref[...])


def mem_cross_attention(x, mem_kv, g, wq, gq, gk, wo, *, tm=512):
    b, s, d = x.shape
    m = mem_kv.shape[1]
    tm = min(tm, s)
    const = lambda shape: pl.BlockSpec(shape, lambda i, t: (0,) * len(shape))
    tile = pl.BlockSpec((1, tm, d), lambda i, t: (i, t, 0))
    return pl.pallas_call(
        _memx_kernel,
        grid=(b, s // tm),
        in_specs=[tile, const((1, d)), const((d, MEM_W)),
                  pl.BlockSpec((1, m, 2 * MEM_W), lambda i, t: (i, 0, 0)),
                  const((1, MEM_HEAD_DIM)), const((1, MEM_HEAD_DIM)), const((MEM_W, d))],
        out_specs=tile,
        out_shape=jax.ShapeDtypeStruct((b, s, d), F32),
        compiler_params=_params("parallel", "arbitrary"),
        name="mem_cross_attention",
    )(x, g.reshape(1, d).astype(F32), wq, mem_kv, gq.reshape(1, -1).astype(F32),
      gk.reshape(1, -1).astype(F32), wo)


def _rope_tables(positions, dim, lead_ones, tail):
    inv = 1.0 / (ROPE_THETA ** (jnp.arange(0, dim, 2, dtype=F32) / dim))
    ang = positions.astype(F32)[..., None] * inv
    c, s = jnp.cos(ang), jnp.sin(ang)
    shape = positions.shape
    cos = jnp.concatenate([jnp.ones(shape + (lead_ones,), F32), c, c, jnp.ones(shape + (tail,), F32)], axis=-1)
    sin = jnp.concatenate([jnp.zeros(shape + (lead_ones,), F32), -s, s, jnp.zeros(shape + (tail,), F32)], axis=-1)
    reps = LANES // cos.shape[-1]
    return jnp.tile(cos, (1, 1, reps)), jnp.tile(sin, (1, 1, reps))


def _pad_cols(w, cols):
    return jnp.pad(w, ((0, 0), (0, cols - w.shape[1])))


def _ab_in_layout(w):
    c = (SWA_HEADS + 2 * SWA_KV_HEADS) * HEAD_DIM + 3 * RWKV_DIM
    return jnp.concatenate([w[:, :c], _pad_cols(w[:, c:c + DECAY_LORA], LANES),
                            _pad_cols(w[:, c + DECAY_LORA:c + DECAY_LORA + AAA_LORA], LANES),
                            _pad_cols(w[:, c + DECAY_LORA + AAA_LORA:], 2 * LANES)], axis=1)


def _cd_in_layout(w):
    c1 = MLA_Q_RANK + MLA_KV_RANK
    z = lambda n: jnp.zeros((w.shape[0], n), w.dtype)
    return jnp.concatenate([w[:, :c1], z(MLA_NOPE), w[:, c1:c1 + MLA_ROPE], z(LANES - MLA_NOPE - MLA_ROPE),
                            z(LANES), w[:, c1 + MLA_ROPE:]], axis=1)


def _head_slabs(w, per_head):
    k = w.shape[0]
    return jnp.pad(w.reshape(k, -1, per_head), ((0, 0), (0, 0), (0, LANES - per_head))).reshape(k, -1)


def _slab_vec(*parts):
    v = jnp.concatenate([p.astype(F32) for p in parts])
    return jnp.pad(v, (0, LANES - v.shape[0])).reshape(1, LANES)


def kernel(x, mem, positions, ffn1_norm, ffn1_w_gate, ffn1_w_up, ffn1_w_down, mix_norm, ab_w_in, ab_w_out, swa_q_norm, swa_k_norm, swa_sinks, rwkv_mu, rwkv_w0, rwkv_w2, rwkv_a0, rwkv_a2, rwkv_g2, rwkv_k_k, rwkv_k_a, rwkv_r_k, rwkv_gn_g, rwkv_gn_b, cd_w_in, cd_w_out, mla_cq_norm, mla_ckv_norm, mla_w_uq, mla_w_ukv, mla_q_nope_norm, mla_k_nope_norm, mla_q_rope_norm, mla_k_rope_norm, diff_q_norm, diff_k_norm, diff_lq1, diff_lk1, diff_lq2, diff_lk2, diff_subln, memx_norm, memx_w_q, memx_q_norm, memx_w_o, mem_norm, mem_w_kv, mem_k_norm, ffn2_norm, ffn2_w_gate, ffn2_w_up, ffn2_w_down):
    b, s, d = x.shape
    m = mem.shape[1]
    t = b * s
    depth = ffn1_norm.shape[0]
    bf = lambda w: w.astype(BF16)
    cos64, sin64 = _rope_tables(positions, HEAD_DIM, 0, 0)
    cos32, sin32 = _rope_tables(positions, MLA_ROPE, MLA_NOPE, LANES - MLA_NOPE - MLA_ROPE)

    mem_kv = norm_matmul(mem.reshape(b * m, d), mem_norm, bf(mem_w_kv), name="mem_kv").reshape(b, m, 2 * MEM_W)

    x = x.reshape(t, d)
    for layer in range(depth):
        j = layer // 2
        x = ffn(x, ffn1_norm[layer], ffn1_w_gate, ffn1_w_up, ffn1_w_down, layer, name="ffn1")
        if layer % 2 == 0:
            u = norm_matmul(x, mix_norm[layer], _ab_in_layout(bf(ab_w_in[j])), name="ab_in")
            u = u.reshape(b, s, AB_IN_PAD)
            y_a = swa_attention(u, cos64, sin64, _slab_vec(swa_q_norm[j], swa_q_norm[j]),
                                _slab_vec(swa_k_norm[j], swa_k_norm[j]), swa_sinks[j])
            y_b = rwkv_mix(u, rwkv_mu[j], rwkv_w0[j], rwkv_w2[j], rwkv_a0[j], rwkv_a2[j], rwkv_g2[j],
                           rwkv_k_k[j], rwkv_k_a[j], rwkv_r_k[j], rwkv_gn_g[j], rwkv_gn_b[j])
            w_out = bf(ab_w_out[j])
            half = SWA_HEADS * HEAD_DIM
            x = matmul_residual(x, [y_a.reshape(t, -1), y_b.reshape(t, -1)], [w_out[:half], w_out[half:]],
                                name="ab_out")
        else:
            u = norm_matmul(x, mix_norm[layer], _cd_in_layout(bf(cd_w_in[j])), name="cd_in")
            u = u.reshape(b, s, CD_IN_PAD)
            zero64 = jnp.zeros((MLA_NOPE,), F32)
            q_c, k_c, vt_c = mla_prep(u, cos32, sin32,
                                      _slab_vec(mla_q_nope_norm[j], mla_q_rope_norm[j]),
                                      _slab_vec(mla_k_nope_norm[j]),
                                      _slab_vec(zero64, mla_k_rope_norm[j]),
                                      mla_cq_norm[j], mla_ckv_norm[j],
                                      bf(_head_slabs(mla_w_uq[j], MLA_NOPE + MLA_ROPE)), bf(mla_w_ukv[j]))
            y_c = causal_attention(q_c, k_c, vt_c, n_sm=1, hps=8, name="mla_attention")
            q_d, k_d, vt_d = diff_prep(u, cos64, sin64, _slab_vec(diff_q_norm[j], diff_q_norm[j]),
                                       _slab_vec(diff_k_norm[j], diff_k_norm[j]))
            lambda_init = 0.8 - 0.6 * math.exp(-0.3 * layer)
            lam = (jnp.exp(jnp.sum(diff_lq1[j].astype(F32) * diff_lk1[j].astype(F32)))
                   - jnp.exp(jnp.sum(diff_lq2[j].astype(F32) * diff_lk2[j].astype(F32))) + lambda_init)
            y_d = causal_attention(q_d, k_d, vt_d, n_sm=2, hps=4, lam=lam, g=diff_subln[j],
                                   out_scale=1.0 - lambda_init, name="diff_attention")
            w_out = bf(cd_w_out[j])
            n_c = MLA_HEADS * MLA_V
            x = matmul_residual(x, [y_c.reshape(t, -1), y_d.reshape(t, -1)], [w_out[:n_c], w_out[n_c:]],
                                name="cd_out")
        x = mem_cross_attention(x.reshape(b, s, d), mem_kv, memx_norm[layer], bf(memx_w_q[layer]),
                                memx_q_norm[layer], mem_k_norm, bf(memx_w_o[layer])).reshape(t, d)
        x = ffn(x, ffn2_norm[layer], ffn2_w_gate, ffn2_w_up, ffn2_w_down, layer, name="ffn2")
    return x.reshape(b, s, d)
```

```python
import functools
import math

import jax
import jax.numpy as jnp
from jax import lax
from jax.experimental import pallas as pl
from jax.experimental.pallas import tpu as pltpu

F32 = jnp.float32
BF16 = jnp.bfloat16

EPS = 1e-6
ROPE_THETA = 10000.0
NEG_INF = -1e30
LOG2E = math.log2(math.e)
ATTN_LOOKAHEAD = 8
ATTN_TILE = 256
NORM_ROWS = 16
NORM_UNROLL = 4
SEG_W = 256
LANES = 128

D_MODEL = 2048
D_FF = 5632
HEAD_DIM = 64
SWA_HEADS = 16
SWA_KV_HEADS = 4
SWA_BLOCK = 128
RWKV_DIM = 1024
RWKV_CHUNK = 64
RWKV_GN_EPS = 64e-5
DECAY_LORA, AAA_LORA, GATE_LORA = 64, 64, 160
MLA_HEADS, MLA_Q_RANK, MLA_KV_RANK, MLA_NOPE, MLA_ROPE, MLA_V = 16, 512, 256, 64, 32, 64
DIFF_HEADS, DIFF_QK, DIFF_V = 8, 64, 128
MEM_HEADS, MEM_HEAD_DIM = 4, 128
MEM_W = MEM_HEADS * MEM_HEAD_DIM
AB_IN_PAD = 5120
CD_IN_PAD = 4096

VMEM_LIMIT = 48 * 1024 * 1024
FFN_VMEM_LIMIT = 60 * 1024 * 1024


def _params(*sem):
    return pltpu.CompilerParams(dimension_semantics=sem, vmem_limit_bytes=VMEM_LIMIT)


def _dot(a, b):
    return jnp.dot(a, b, preferred_element_type=F32)


def _dot_t(a, b):
    return lax.dot_general(a, b, (((1,), (1,)), ((), ())), preferred_element_type=F32)


def _dot_0(a, b):
    return lax.dot_general(a, b, (((0,), (0,)), ((), ())), preferred_element_type=F32)


def _rms(x, g):
    return x * lax.rsqrt(jnp.mean(x * x, axis=-1, keepdims=True) + EPS) * g


def _for_row_blocks(n_rows, body):
    step = NORM_ROWS * NORM_UNROLL
    assert n_rows % step == 0

    @pl.loop(0, n_rows // step)
    def _(r):
        for u in range(NORM_UNROLL):
            body(pl.ds(pl.multiple_of(r * step + u * NORM_ROWS, NORM_ROWS), NORM_ROWS))


def _seg_sum(x, e):
    hi = x.astype(BF16)
    lo = (x - hi.astype(F32)).astype(BF16)
    w = e.shape[0]
    parts = [_dot(hi[:, i:i + w], e) + _dot(lo[:, i:i + w], e) for i in range(0, x.shape[1], w)]
    return parts[0] if len(parts) == 1 else jnp.concatenate(parts, axis=-1)


def _tile_lanes(v, width):
    return v if v.shape[-1] == width else jnp.tile(v, (1, width // v.shape[-1]))


def _store_transposed_tiles(vt_ref, v):
    for r in range(vt_ref.shape[1]):
        vt_ref[0, r] = v[r * ATTN_TILE:(r + 1) * ATTN_TILE, :].T.astype(vt_ref.dtype)


def _swap_halves(x, half):
    w = x.shape[-1]
    lane = lax.broadcasted_iota(jnp.int32, x.shape, x.ndim - 1)
    low = (lane & (2 * half - 1)) < half
    return jnp.where(low, pltpu.roll(x, w - half, x.ndim - 1), pltpu.roll(x, half, x.ndim - 1))


def _norm_matmul_kernel(x_ref, g_ref, w_ref, o_ref, xn_ref):
    @pl.when(pl.program_id(1) == 0)
    def _():
        def norm_rows(rows):
            xn_ref[rows, :] = _rms(x_ref[rows, :].astype(F32), g_ref[...]).astype(BF16)

        _for_row_blocks(x_ref.shape[0], norm_rows)

    o_ref[...] = _dot(xn_ref[...], w_ref[...]).astype(o_ref.dtype)


def norm_matmul(x, g, w, *, k_blk=0, out_dtype=F32, tm=1024, tn=512, name="norm_matmul"):
    t = x.shape[0]
    k, n = w.shape
    tm, tn = min(tm, t), min(tn, n)
    assert t % tm == 0 and n % tn == 0
    return pl.pallas_call(
        _norm_matmul_kernel,
        grid=(t // tm, n // tn),
        in_specs=[pl.BlockSpec((tm, k), lambda i, j: (i, k_blk)),
                  pl.BlockSpec((1, k), lambda i, j: (0, 0)),
                  pl.BlockSpec((k, tn), lambda i, j: (0, j))],
        out_specs=pl.BlockSpec((tm, tn), lambda i, j: (i, j)),
        out_shape=jax.ShapeDtypeStruct((t, n), out_dtype),
        scratch_shapes=[pltpu.VMEM((tm, k), BF16)],
        compiler_params=_params("parallel", "arbitrary"),
        name=name,
    )(x, g.reshape(1, k).astype(F32), w)


def _matmul_res_kernel(*refs, n_in):
    x_ref, o_ref = refs[2 * n_in], refs[2 * n_in + 1]
    acc = x_ref[...]
    for i in range(n_in):
        acc = acc + _dot(refs[i][...], refs[n_in + i][...])
    o_ref[...] = acc


def matmul_residual(x, a_list, w_list, *, tm=256, name="matmul_residual"):
    t, n = x.shape
    tm = min(tm, t)
    assert t % tm == 0
    n_in = len(a_list)
    in_specs = ([pl.BlockSpec((tm, a.shape[1]), lambda i: (i, 0)) for a in a_list]
                + [pl.BlockSpec(w.shape, lambda i: (0, 0)) for w in w_list]
                + [pl.BlockSpec((tm, n), lambda i: (i, 0))])
    return pl.pallas_call(
        functools.partial(_matmul_res_kernel, n_in=n_in),
        grid=(t // tm,),
        in_specs=in_specs,
        out_specs=pl.BlockSpec((tm, n), lambda i: (i, 0)),
        out_shape=jax.ShapeDtypeStruct((t, n), F32),
        compiler_params=_params("parallel"),
        name=name,
    )(*a_list, *w_list, x)


def _ffn_kernel(x_ref, g_ref, wg_ref, wu_ref, wd_ref, o_ref, xn_ref):
    @pl.when(pl.program_id(1) == 0)
    def _():
        def norm_rows(rows):
            x = x_ref[rows, :]
            xn_ref[rows, :] = _rms(x, g_ref[...]).astype(BF16)
            o_ref[rows, :] = x

        _for_row_blocks(x_ref.shape[0], norm_rows)

    xn = xn_ref[...]
    a = _dot(xn, wg_ref[...].astype(BF16))
    b = _dot(xn, wu_ref[...].astype(BF16))
    h = (a * (0.5 / (1.0 + jnp.exp(-a))) * b).astype(BF16)
    o_ref[...] += _dot(h, wd_ref[...].astype(BF16))


def ffn(x, g, wg, wu, wd, layer, *, tm=1024, tf=256, name="ffn"):
    t, d = x.shape
    ff = wg.shape[2]
    tm = min(tm, t)
    assert t % tm == 0 and ff % tf == 0
    nf = ff // tf
    return pl.pallas_call(
        _ffn_kernel,
        grid=(t // tm, nf),
        in_specs=[pl.BlockSpec((tm, d), lambda i, f: (i, 0)),
                  pl.BlockSpec((1, d), lambda i, f: (0, 0)),
                  pl.BlockSpec((None, d, tf), lambda i, f: (layer, 0, f)),
                  pl.BlockSpec((None, d, tf), lambda i, f: (layer, 0, f)),
                  pl.BlockSpec((None, tf, d), lambda i, f: (layer, f, 0))],
        out_specs=pl.BlockSpec((tm, d), lambda i, f: (i, 0)),
        out_shape=jax.ShapeDtypeStruct((t, d), F32),
        scratch_shapes=[pltpu.VMEM((tm, d), BF16)],
        compiler_params=pltpu.CompilerParams(dimension_semantics=("parallel", "arbitrary"),
                                             vmem_limit_bytes=FFN_VMEM_LIMIT),
        name=name,
    )(x, g.reshape(1, d).astype(F32), wg, wu, wd)


def _swa_kernel(sink_ref, q_ref, kc_ref, kp_ref, vc_ref, vp_ref, cc_ref, sc_ref, cp_ref, sp_ref,
                gq_ref, gk_ref, e_ref, o_ref):
    n = pl.program_id(1)
    blk = SWA_BLOCK
    group = SWA_HEADS // SWA_KV_HEADS
    q = q_ref[0]
    k = jnp.concatenate([kp_ref[0], kc_ref[0]], axis=0)
    v = jnp.concatenate([vp_ref[0], vc_ref[0]], axis=0)
    cos_q, sin_q = cc_ref[0], sc_ref[0]
    cos_k = jnp.concatenate([cp_ref[0], cos_q], axis=0)
    sin_k = jnp.concatenate([sp_ref[0], sin_q], axis=0)
    def prep(x, g, cos, sin, scale):
        w = x.shape[1]
        xg = x * _tile_lanes(g, w)
        xr = xg * _tile_lanes(cos, w) + _swap_halves(xg, HEAD_DIM // 2) * _tile_lanes(sin, w)
        return xr * lax.rsqrt(_seg_sum(x * x, e_ref[...]) * (1.0 / HEAD_DIM) + EPS) * scale

    qr = prep(q, gq_ref[...], cos_q, sin_q, HEAD_DIM ** -0.5 * LOG2E).astype(BF16)
    kr = prep(k, gk_ref[...], cos_k, sin_k, 1.0)
    key_i = lax.broadcasted_iota(jnp.int32, (2 * blk, blk), 0)
    qry_i = lax.broadcasted_iota(jnp.int32, (2 * blk, blk), 1)
    rel = qry_i + blk - key_i
    valid = (rel >= 0) & (rel < blk) & ((n > 0) | (key_i >= blk))
    low = lax.broadcasted_iota(jnp.int32, (blk, LANES), 1) < HEAD_DIM
    zero = jnp.zeros((blk, LANES), BF16)
    k_dup, v_t = [], []
    for g in range(SWA_KV_HEADS):
        kg = kr[:, g * HEAD_DIM:(g + 1) * HEAD_DIM]
        k_dup.append(jnp.concatenate([kg, kg], axis=-1).astype(BF16))
        v_t.append(v[:, g * HEAD_DIM:(g + 1) * HEAD_DIM].T.astype(BF16))

    def scores(h):
        slab = qr[:, (h // 2) * LANES:(h // 2 + 1) * LANES]
        qh = jnp.where(low, slab, zero) if h % 2 == 0 else jnp.where(low, zero, slab)
        return _dot_t(k_dup[h // group], qh)

    ahead = [scores(h) for h in range(ATTN_LOOKAHEAD)]
    ones = jnp.ones((8, 2 * blk), BF16)
    outs = []
    for h in range(SWA_HEADS):
        s = jnp.where(valid, ahead.pop(0), NEG_INF)
        if h + ATTN_LOOKAHEAD < SWA_HEADS:
            ahead.append(scores(h + ATTN_LOOKAHEAD))
        sink = sink_ref[h] * LOG2E
        m = jnp.maximum(jnp.max(s, axis=0, keepdims=True), sink)
        p = jnp.exp2(s - m).astype(BF16)
        den = _dot(ones, p)[0:1] + jnp.exp2(sink - m)
        outs.append(_dot(v_t[h // group], p) / den)
    slabs = [jnp.concatenate(outs[i:i + 2], axis=0).T for i in range(0, SWA_HEADS, 2)]
    o_ref[0] = jnp.concatenate(slabs, axis=-1).astype(o_ref.dtype)


def swa_attention(u, cos, sin, gq, gk, sinks):
    b, s, _ = u.shape
    nb = s // SWA_BLOCK
    qw, kw = SWA_HEADS * HEAD_DIM, SWA_KV_HEADS * HEAD_DIM
    cur = lambda c: (lambda i, n: (i, n, c))
    prev = lambda c: (lambda i, n: (i, jnp.maximum(n - 1, 0), c))
    tab = pl.BlockSpec((1, SWA_BLOCK, LANES), cur(0))
    tab_prev = pl.BlockSpec((1, SWA_BLOCK, LANES), prev(0))
    gain = pl.BlockSpec((1, LANES), lambda i, n: (0, 0))
    seg = (jnp.arange(kw)[:, None] // HEAD_DIM == jnp.arange(kw)[None, :] // HEAD_DIM).astype(BF16)
    return pl.pallas_call(
        _swa_kernel,
        grid=(b, nb),
        in_specs=[pl.BlockSpec(memory_space=pltpu.SMEM),
                  pl.BlockSpec((1, SWA_BLOCK, qw), cur(0)),
                  pl.BlockSpec((1, SWA_BLOCK, kw), cur(qw // kw)),
                  pl.BlockSpec((1, SWA_BLOCK, kw), prev(qw // kw)),
                  pl.BlockSpec((1, SWA_BLOCK, kw), cur(qw // kw + 1)),
                  pl.BlockSpec((1, SWA_BLOCK, kw), prev(qw // kw + 1)),
                  tab, tab, tab_prev, tab_prev, gain, gain, pl.BlockSpec((kw, kw), lambda i, n: (0, 0))],
        out_specs=pl.BlockSpec((1, SWA_BLOCK, qw), cur(0)),
        out_shape=jax.ShapeDtypeStruct((b, s, qw), BF16),
        compiler_params=_params("parallel", "arbitrary"),
        name="swa_attention",
    )(sinks.astype(F32), u, u, u, u, u, cos, sin, cos, sin, gq, gk, seg)


def _mm(a, b, dims, passes):
    dn = (dims, ((), ()))
    dg = lambda x, y: lax.dot_general(x, y, dn, preferred_element_type=F32)
    ah = a.astype(BF16)
    bh = b.astype(BF16)
    if passes == 1:
        return dg(ah, bh)
    al = (a - ah.astype(F32)).astype(BF16)
    bl = (b - bh.astype(F32)).astype(BF16)
    return dg(ah, bh) + dg(ah, bl) + dg(al, bh)


_NN = ((1,), (0,))
_NT = ((1,), (1,))
_TN = ((0,), (0,))
P_SC, P_INV, P_PQ, P_OUT, P_ST = 1, 1, 1, 1, 1
RWKV_UNROLL = 8


RWKV_PREP_ROWS = 256


def _rwkv_kernel(ur_ref, uk_ref, uv_ref, ul_ref, mur_ref, muk_ref, muv_ref, mul_ref, w0_ref, w2_ref, a0_ref,
                 a2_ref, g2_ref, kkw_ref, ka_ref, gng_ref, gnb_ref, rk_ref,
                 o_ref, r_ref, k_ref, v_ref, kk_ref, b_ref, lw_ref, g_ref,
                 st_ref, y1_ref, y0_ref, n_ref, z_ref, dec_ref, *, nchunk):
    c = RWKV_CHUNK
    lane_c = lax.broadcasted_iota(jnp.int32, (c, LANES), 1)
    head0 = lane_c < HEAD_DIM
    ri = lax.broadcasted_iota(jnp.int32, (2 * c, 2 * c), 0)
    ci = lax.broadcasted_iota(jnp.int32, (2 * c, 2 * c), 1)
    eye = jnp.where(ri == ci, 1.0, 0.0)
    tril_c = jnp.where(lax.broadcasted_iota(jnp.int32, (c, c), 0) >= lax.broadcasted_iota(jnp.int32, (c, c), 1),
                       1.0, 0.0).astype(BF16)
    stack = lambda x: jnp.concatenate([jnp.where(head0, x, 0.0), jnp.where(head0, 0.0, x)], axis=0)

    def seg_mean(x):
        first = lax.broadcasted_iota(jnp.int32, x.shape, 1) < HEAD_DIM
        m0 = jnp.sum(jnp.where(first, x, 0.0), axis=-1, keepdims=True)
        m1 = jnp.sum(jnp.where(first, 0.0, x), axis=-1, keepdims=True)
        return jnp.where(first, m0, m1) * (1.0 / HEAD_DIM)

    def prep(io, first_group):
        group_rows = RWKV_UNROLL * c
        for tix in range(group_rows // RWKV_PREP_ROWS):
            start = pl.multiple_of(io * group_rows + tix * RWKV_PREP_ROWS, RWKV_PREP_ROWS)
            rows = pl.ds(start, RWKV_PREP_ROWS)
            at_start = first_group and tix == 0

            def shifted(ref, mu_ref):
                x = ref[0, rows, :]
                if at_start:
                    last = jnp.zeros((1, x.shape[1]), F32)
                else:
                    last = ref[0, pl.ds(pl.multiple_of(start - 8, 8), 8), :][7:8, :]
                row = lax.broadcasted_iota(jnp.int32, x.shape, 0)
                prev = jnp.where(row == 0, last, pltpu.roll(x, 1, 0))
                return x + (prev - x) * mu_ref[...]

            r = shifted(ur_ref, mur_ref)
            k = shifted(uk_ref, muk_ref)
            v = shifted(uv_ref, muv_ref)
            lo = shifted(ul_ref, mul_ref)
            yield
            w_lo, a_lo, g_lo = lo[:, 0:LANES], lo[:, LANES:2 * LANES], lo[:, 2 * LANES:4 * LANES]
            z = -(w0_ref[...] + _mm(jnp.tanh(w_lo), w2_ref[...], _NN, 3))
            w = -(jnp.maximum(z, 0.0) + jnp.log(1.0 + jnp.exp(-jnp.abs(z)))) - 0.5
            a = 1.0 / (1.0 + jnp.exp(-(a0_ref[...] + _mm(a_lo, a2_ref[...], _NN, 1))))
            g = _mm(1.0 / (1.0 + jnp.exp(-g_lo)), g2_ref[...], _NN, 1)
            yield
            kk = k * kkw_ref[...]
            kk = kk / jnp.maximum(jnp.sqrt(seg_mean(kk * kk) * float(HEAD_DIM)), 1e-12)
            r_ref[rows, :] = r
            k_ref[rows, :] = k * (1.0 + (a - 1.0) * ka_ref[...])
            v_ref[rows, :] = v
            kk_ref[rows, :] = kk
            b_ref[rows, :] = kk * a
            lw_ref[rows, :] = -jnp.exp(w)
            g_ref[rows, :] = g
            yield

    def build(ics):
        each = lambda f, *cols: [f(*args) for args in zip(*cols)]
        sls = [pl.ds(pl.multiple_of(ic * c, c), c) for ic in ics]
        load = lambda ref: [ref[sl, :] for sl in sls]
        r, k, v, kk, b, lw = (load(ref) for ref in (r_ref, k_ref, v_ref, kk_ref, b_ref, lw_ref))

        def running_sum(x):
            l1 = x.astype(BF16)
            rest = x - l1.astype(F32)
            l2 = rest.astype(BF16)
            l3 = (rest - l2.astype(F32)).astype(BF16)
            return _dot(tril_c, l1) + _dot(tril_c, l2) + _dot(tril_c, l3)

        cum = each(running_sum, lw)
        yield
        cum_end = [x[c - 1:c, :] for x in cum]
        e_neg = each(lambda x: jnp.exp(-x), cum)
        e_end = each(lambda x, xe: jnp.exp(xe - x), cum, cum_end)
        a_s = each(lambda kk_, x, l: stack(-kk_ * jnp.exp(x - l)), kk, cum, lw)
        r_s = each(lambda r_, x: stack(r_ * jnp.exp(x)), r, cum)
        b_s = each(lambda b_, e: stack(b_ * e), b, e_neg)
        k_s = each(lambda k_, e: stack(k_ * e), k, e_neg)
        bh_s = each(lambda b_, e: stack(b_ * e), b, e_end)
        kh_s = each(lambda k_, e: stack(k_ * e), k, e_end)
        v_s = each(stack, v)
        n2 = 2 * c
        sc = each(lambda a_, r_, b_, k_: _mm(jnp.concatenate([a_, r_], axis=0),
                                             jnp.concatenate([b_, k_], axis=0), _NT, P_SC), a_s, r_s, b_s, k_s)
        low = [jnp.where(ri > ci, x[:n2, :n2], 0.0) for x in sc]
        a_ak = [jnp.where(ri > ci, x[:n2, n2:], 0.0) for x in sc]
        a_rb = [jnp.where(ri >= ci, x[n2:, :n2], 0.0) for x in sc]
        a_rk = [jnp.where(ri >= ci, x[n2:, n2:], 0.0) for x in sc]
        yield
        inv = [eye + x for x in low]
        pw = each(lambda x: _mm(x, x, _NN, P_INV), low)
        yield
        levels = 5
        for lvl in range(levels - 1):
            both = each(lambda t, x: _mm(jnp.concatenate([t, x], axis=0), x, _NN, P_INV), inv, pw)
            inv = each(lambda t, r: t + r[:n2], inv, both)
            pw = [r[n2:] for r in both]
            yield
        inv = each(lambda t, x: t + _mm(t, x, _NN, P_INV), inv, pw)
        yield
        akv = each(lambda x, y: _mm(x, y, _NN, P_PQ), a_ak, v_s)
        yield
        pq = each(lambda t, x, y: _mm(t, jnp.concatenate([x, y], axis=1), _NN, P_PQ), inv, a_s, akv)
        yield
        yy = each(lambda x, y: _mm(x, y, _NN, P_OUT), a_rb, pq)
        y0b = each(lambda x, y: _mm(x, y, _NN, P_OUT), a_rk, v_s)
        nz = each(lambda x, y: _mm(x, y, _TN, P_OUT), pq, bh_s)
        zb = each(lambda x, y: _mm(x, y, _TN, P_OUT), v_s, kh_s)
        for i, ic in enumerate(ics):
            y1_ref[ic] = r_s[i] + yy[i][:, :LANES]
            y0_ref[ic] = yy[i][:, LANES:] + y0b[i]
            n_ref[ic] = nz[i][:LANES]
            z_ref[ic] = nz[i][LANES:] + zb[i]
            dec_ref[ic] = jnp.broadcast_to(jnp.exp(cum_end[i]), (8, LANES))

    def emit(ic, st):
        sl = pl.ds(pl.multiple_of(ic * c, c), c)
        r, k, v, g = r_ref[sl, :], k_ref[sl, :], v_ref[sl, :], g_ref[sl, :]
        y_st = _mm(y1_ref[ic], st, _NT, P_ST) + y0_ref[ic]
        y = y_st[0:c] + y_st[c:2 * c]
        mean = seg_mean(y)
        var = seg_mean((y - mean) * (y - mean))
        yn = (y - mean) * lax.rsqrt(var + RWKV_GN_EPS) * gng_ref[...] + gnb_ref[...]
        bonus = seg_mean(r * k * rk_ref[...]) * float(HEAD_DIM) * v
        o_ref[0, sl, :] = ((yn + bonus) * g).astype(o_ref.dtype)
        return st * dec_ref[ic][0:1, :] + _mm(st, n_ref[ic], _NN, P_ST) + z_ref[ic]

    def scan(ics):
        st = st_ref[...]
        for ic in ics:
            st = emit(ic, st)
            yield
        st_ref[...] = st

    def run(*gens):
        live = list(gens)
        while live:
            live = [gen for gen in live if next(gen, live) is not live]

    group = lambda io: [io * RWKV_UNROLL + i for i in range(RWKV_UNROLL)]
    ngroup = nchunk // RWKV_UNROLL
    st_ref[...] = jnp.zeros_like(st_ref)
    run(prep(0, True))
    if ngroup > 1:
        run(prep(1, False), build(group(0)))
    else:
        run(build(group(0)))

    @pl.loop(1, ngroup - 1)
    def _(io):
        run(prep(io + 1, False), build(group(io)), scan(group(io - 1)))

    if ngroup > 1:
        run(build(group(ngroup - 1)), scan(group(ngroup - 2)))
    run(scan(group(ngroup - 1)))


def rwkv_mix(u, mu, w0, w2, a0, a2, g2, k_k, k_a, r_k, gn_g, gn_b):
    bsz, s, _ = u.shape
    npair = RWKV_DIM // LANES
    nchunk = s // RWKV_CHUNK
    assert nchunk % RWKV_UNROLL == 0 and (RWKV_UNROLL * RWKV_CHUNK) % RWKV_PREP_ROWS == 0
    base = (SWA_HEADS + 2 * SWA_KV_HEADS) * HEAD_DIM // LANES
    lora_w = 4 * LANES
    slab = lambda off: pl.BlockSpec((1, s, LANES), lambda i, p: (i, 0, off + p))
    vec = pl.BlockSpec((1, LANES), lambda i, p: (0, p))
    cols = lambda rows: pl.BlockSpec((rows, LANES), lambda i, p: (0, p))
    row = lambda vv: vv.reshape(1, -1).astype(F32)
    pad_rows = lambda m, rows: jnp.pad(m, ((0, rows - m.shape[0]), (0, 0))).astype(F32)
    pad_cols = lambda vv, n: jnp.pad(vv, (0, n - vv.shape[0]))
    c3 = 3 * RWKV_DIM
    mu_l = jnp.concatenate([pad_cols(mu[c3:c3 + DECAY_LORA], LANES),
                            pad_cols(mu[c3 + DECAY_LORA:c3 + DECAY_LORA + AAA_LORA], LANES),
                            pad_cols(mu[c3 + DECAY_LORA + AAA_LORA:], 2 * LANES)])
    seq = pltpu.VMEM((s, LANES), F32)
    mat = pltpu.VMEM((nchunk, LANES, LANES), F32)
    return pl.pallas_call(
        functools.partial(_rwkv_kernel, nchunk=nchunk),
        grid=(bsz, npair),
        in_specs=[slab(base), slab(base + npair), slab(base + 2 * npair),
                  pl.BlockSpec((1, s, lora_w), lambda i, p: (i, 0, (base + 3 * npair) * LANES // lora_w)),
                  vec, vec, vec, pl.BlockSpec((1, lora_w), lambda i, p: (0, 0)),
                  vec, cols(LANES), vec, cols(LANES), cols(2 * LANES), vec, vec, vec, vec, vec],
        out_specs=pl.BlockSpec((1, s, LANES), lambda i, p: (i, 0, p)),
        out_shape=jax.ShapeDtypeStruct((bsz, s, RWKV_DIM), BF16),
        scratch_shapes=[seq] * 7 + [pltpu.VMEM((LANES, LANES), F32), mat, mat, mat, mat,
                                    pltpu.VMEM((nchunk, 8, LANES), F32)],
        compiler_params=_params("parallel", "arbitrary"),
        name="rwkv_mix",
    )(u, u, u, u, row(mu[:RWKV_DIM]), row(mu[RWKV_DIM:2 * RWKV_DIM]), row(mu[2 * RWKV_DIM:c3]), row(mu_l),
      row(w0), pad_rows(w2, LANES), row(a0), pad_rows(a2, LANES), pad_rows(g2, 2 * LANES),
      row(k_k), row(k_a), row(gn_g), row(gn_b), row(r_k))


def _mla_prep_kernel(cq_ref, ckv_ref, pe_ref, cos_ref, sin_ref, e_ref, gq_ref, gkn_ref, gkp_ref, invn_ref,
                     gcq_ref, gckv_ref, wuq_ref, wukv_ref, qo_ref, ko_ref, vo_ref):
    cos, sin = cos_ref[0], sin_ref[0]
    half = MLA_ROPE // 2

    def rope(x, g):
        w = x.shape[1]
        xg = x * _tile_lanes(g, w)
        return xg * _tile_lanes(cos, w) + _swap_halves(xg, half) * _tile_lanes(sin, w)

    x = _dot(_rms(cq_ref[0], gcq_ref[...]).astype(BF16), wuq_ref[...])
    w = x.shape[1]
    inv_n = _tile_lanes(invn_ref[...], w)
    inv = lax.rsqrt(_seg_sum(x * x, e_ref[...]) * inv_n + EPS)
    qo_ref[0] = (rope(x, gq_ref[...]) * inv * ((MLA_NOPE + MLA_ROPE) ** -0.5 * LOG2E)).astype(qo_ref.dtype)
    kv = _dot(_rms(ckv_ref[0], gckv_ref[...]).astype(BF16), wukv_ref[...])
    inv_k = lax.rsqrt(_seg_sum(kv * kv, e_ref[...]) * inv_n + EPS)
    k_nope = kv * inv_k * _tile_lanes(gkn_ref[...], w)
    pe = pe_ref[0]
    inv_pe = lax.rsqrt(jnp.sum(pe * pe, axis=-1, keepdims=True) * (1.0 / MLA_ROPE) + EPS)
    k_pe = rope(pe, gkp_ref[...]) * inv_pe
    ko_ref[0] = (k_nope + _tile_lanes(k_pe, w)).astype(ko_ref.dtype)
    v = jnp.concatenate([kv[:, h * LANES + MLA_NOPE:(h + 1) * LANES] for h in range(w // LANES)], axis=-1)
    _store_transposed_tiles(vo_ref, v)


def mla_prep(u, cos, sin, gq, gkn, gkp, gcq, gckv, w_uq, w_ukv, *, ts=512, tc=512):
    b, s, _ = u.shape
    wtot = w_uq.shape[1]
    ts = min(ts, s)
    lane = jnp.arange(SEG_W)
    same = (lane[:, None] // LANES == lane[None, :] // LANES)
    pos = lane % LANES
    nope = pos < MLA_NOPE
    pe = (pos >= MLA_NOPE) & (pos < MLA_NOPE + MLA_ROPE)
    seg = (same & ((nope[:, None] & nope[None, :]) | (pe[:, None] & pe[None, :]))).astype(BF16)
    p1 = jnp.arange(LANES)
    inv_n = jnp.where(p1 < MLA_NOPE, 1.0 / MLA_NOPE, jnp.where(p1 < MLA_NOPE + MLA_ROPE, 1.0 / MLA_ROPE, 0.0))
    blk = pl.BlockSpec((1, ts, tc), lambda i, t, c: (i, t, c))
    tab = pl.BlockSpec((1, ts, LANES), lambda i, t, c: (i, t, 0))
    vec = pl.BlockSpec((1, LANES), lambda i, t, c: (0, 0))
    pe_blk = (MLA_Q_RANK + MLA_KV_RANK) // LANES
    out = jax.ShapeDtypeStruct((b, s, wtot), BF16)
    return pl.pallas_call(
        _mla_prep_kernel,
        grid=(b, s // ts, wtot // tc),
        in_specs=[pl.BlockSpec((1, ts, MLA_Q_RANK), lambda i, t, c: (i, t, 0)),
                  pl.BlockSpec((1, ts, MLA_KV_RANK), lambda i, t, c: (i, t, MLA_Q_RANK // MLA_KV_RANK)),
                  pl.BlockSpec((1, ts, LANES), lambda i, t, c: (i, t, pe_blk)), tab, tab,
                  pl.BlockSpec((SEG_W, SEG_W), lambda i, t, c: (0, 0)), vec, vec, vec, vec,
                  pl.BlockSpec((1, MLA_Q_RANK), lambda i, t, c: (0, 0)),
                  pl.BlockSpec((1, MLA_KV_RANK), lambda i, t, c: (0, 0)),
                  pl.BlockSpec((MLA_Q_RANK, tc), lambda i, t, c: (0, c)),
                  pl.BlockSpec((MLA_KV_RANK, tc), lambda i, t, c: (0, c))],
        out_specs=[blk, blk, pl.BlockSpec((1, ts // ATTN_TILE, tc // LANES * MLA_V, ATTN_TILE),
                                          lambda i, t, c: (i, t, c, 0))],
        out_shape=[out, out, jax.ShapeDtypeStruct((b, s // ATTN_TILE, wtot // LANES * MLA_V, ATTN_TILE), BF16)],
        compiler_params=_params("parallel", "parallel", "arbitrary"),
        name="mla_prep",
    )(u, u, u, cos, sin, seg, gq, gkn, gkp, inv_n.reshape(1, LANES).astype(F32),
      gcq.reshape(1, -1).astype(F32), gckv.reshape(1, -1).astype(F32), w_uq, w_ukv)


def _diff_prep_kernel(q_ref, k_ref, v_ref, cos_ref, sin_ref, e_ref, gq_ref, gk_ref, qo_ref, ko_ref, vo_ref):
    cos, sin = cos_ref[0], sin_ref[0]
    _store_transposed_tiles(vo_ref, v_ref[0])

    def prep(x, g, scale):
        w = x.shape[1]
        xg = x * _tile_lanes(g, w)
        xr = xg * _tile_lanes(cos, w) + _swap_halves(xg, DIFF_QK // 2) * _tile_lanes(sin, w)
        inv = lax.rsqrt(_seg_sum(x * x, e_ref[...]) * (1.0 / DIFF_QK) + EPS)
        return xr * inv * scale

    qo_ref[0] = prep(q_ref[0], gq_ref[...], DIFF_QK ** -0.5 * LOG2E).astype(qo_ref.dtype)
    ko_ref[0] = prep(k_ref[0], gk_ref[...], 1.0).astype(ko_ref.dtype)


def diff_prep(u, cos, sin, gq, gk, *, ts=512, tc=512):
    b, s, _ = u.shape
    ts = min(ts, s)
    wtot = 2 * DIFF_HEADS * DIFF_QK
    q_base = (CD_IN_PAD - 3 * wtot) // tc
    seg = (jnp.arange(SEG_W)[:, None] // DIFF_QK == jnp.arange(SEG_W)[None, :] // DIFF_QK).astype(BF16)
    blk = lambda off: pl.BlockSpec((1, ts, tc), lambda i, t, c: (i, t, off + c))
    tab = pl.BlockSpec((1, ts, LANES), lambda i, t, c: (i, t, 0))
    vec = pl.BlockSpec((1, LANES), lambda i, t, c: (0, 0))
    out = jax.ShapeDtypeStruct((b, s, wtot), BF16)
    return pl.pallas_call(
        _diff_prep_kernel,
        grid=(b, s // ts, wtot // tc),
        in_specs=[blk(q_base), blk(q_base + wtot // tc), blk(q_base + 2 * wtot // tc), tab, tab,
                  pl.BlockSpec((SEG_W, SEG_W), lambda i, t, c: (0, 0)), vec, vec],
        out_specs=[blk(0), blk(0),
                   pl.BlockSpec((1, ts // ATTN_TILE, tc, ATTN_TILE), lambda i, t, c: (i, t, c, 0))],
        out_shape=[out, out, jax.ShapeDtypeStruct((b, s // ATTN_TILE, wtot, ATTN_TILE), BF16)],
        compiler_params=_params("parallel", "parallel", "arbitrary"),
        name="diff_prep",
    )(u, u, u, cos, sin, seg, gq, gk)


def _causal_attn_kernel(lam_ref, q_ref, k_ref, vt_ref, g_ref, o_ref, *, n_sm, tq, ow, out_scale):
    qi = pl.program_id(2)
    q = q_ref[0]
    hps = q.shape[1] // LANES
    slab = lambda x, h: x[:, h * LANES:(h + 1) * LANES]
    lane = lax.broadcasted_iota(jnp.int32, (tq, LANES), 1)
    qs, src = [], []
    for h in range(hps):
        qh = slab(q, h)
        if n_sm == 2:
            zero = jnp.zeros_like(qh)
            qs += [jnp.where(lane < DIFF_QK, qh, zero), jnp.where(lane < DIFF_QK, zero, qh)]
            src += [h, h]
        else:
            qs.append(qh)
            src.append(h)
    nch = len(qs)
    key_i = lax.broadcasted_iota(jnp.int32, (tq, tq), 0)
    qry_i = lax.broadcasted_iota(jnp.int32, (tq, tq), 1)
    ones = jnp.ones((8, tq), BF16)

    def step(j, carry, diagonal):
        kj = k_ref[0, pl.ds(pl.multiple_of(j * tq, tq), tq), :]
        scores = lambda i: _dot_t(slab(kj, src[i]), qs[i])
        new = []
        ahead = [scores(i) for i in range(min(ATTN_LOOKAHEAD, nch))]
        for i in range(nch):
            s = ahead.pop(0)
            if i + ATTN_LOOKAHEAD < nch:
                ahead.append(scores(i + ATTN_LOOKAHEAD))
            if diagonal:
                s = jnp.where(key_i <= qry_i, s, NEG_INF)
            m, l, acc = carry[3 * i:3 * i + 3]
            m_new = jnp.maximum(m, jnp.max(s, axis=0, keepdims=True))
            alpha = jnp.exp2(m - m_new)
            p = jnp.exp2(s - m_new).astype(BF16)
            new += [m_new, alpha * l + _dot(ones, p)[0:1],
                    alpha * acc + _dot(vt_ref[0, j, src[i] * ow:(src[i] + 1) * ow, :], p)]
        return tuple(new)

    init = (jnp.full((1, tq), NEG_INF, F32), jnp.zeros((1, tq), F32),
            jnp.zeros((ow, tq), F32)) * nch
    carry = lax.fori_loop(0, qi, lambda j, cr: step(j, cr, False), init)
    carry = step(qi, carry, True)
    outs = []
    for h in range(hps):
        c0 = 3 * n_sm * h
        o = carry[c0 + 2] / carry[c0 + 1]
        if n_sm == 2:
            o = o - lam_ref[0] * (carry[c0 + 5] / carry[c0 + 4])
            o = o * lax.rsqrt(jnp.mean(o * o, axis=0, keepdims=True) + EPS) * g_ref[...] * out_scale
        outs.append(o.T.astype(o_ref.dtype))
    o_ref[0] = jnp.concatenate(outs, axis=-1)


def causal_attention(q, k, vt, *, n_sm, lam=None, g=None, out_scale=1.0, hps=4, name="causal_attention"):
    b, s, wtot = q.shape
    tq = vt.shape[3]
    width = hps * LANES
    groups = wtot // width
    ow = vt.shape[2] // (wtot // LANES)
    lam = jnp.zeros((1,), F32) if lam is None else lam.reshape(1).astype(F32)
    g = jnp.ones((ow, 1), F32) if g is None else g.reshape(ow, 1).astype(F32)
    seq = pl.BlockSpec((1, s, width), lambda i, h, t: (i, 0, h))
    tile = pl.BlockSpec((1, tq, width), lambda i, h, t: (i, t, h))
    return pl.pallas_call(
        functools.partial(_causal_attn_kernel, n_sm=n_sm, tq=tq, ow=ow, out_scale=out_scale),
        grid=(b, groups, s // tq),
        in_specs=[pl.BlockSpec(memory_space=pltpu.SMEM), tile, seq,
                  pl.BlockSpec((1, s // tq, hps * ow, tq), lambda i, h, t: (i, 0, h, 0)),
                  pl.BlockSpec((ow, 1), lambda i, h, t: (0, 0))],
        out_specs=pl.BlockSpec((1, tq, hps * ow), lambda i, h, t: (i, t, h)),
        out_shape=jax.ShapeDtypeStruct((b, s, groups * hps * ow), BF16),
        compiler_params=_params("parallel", "parallel", "arbitrary"),
        name=name,
    )(lam, q, k, vt, g)


def _memx_kernel(x_ref, g_ref, wq_ref, kv_ref, gq_ref, gk_ref, wo_ref, o_ref):
    x = x_ref[0]
    q = _dot(_rms(x, g_ref[...]).astype(BF16), wq_ref[...])
    kv = kv_ref[0]
    outs = []
    for h in range(MEM_HEADS):
        sl = slice(h * MEM_HEAD_DIM, (h + 1) * MEM_HEAD_DIM)
        qh = (_rms(q[:, sl], gq_ref[...]) * MEM_HEAD_DIM ** -0.5).astype(BF16)
        kh = _rms(kv[:, sl], gk_ref[...]).astype(BF16)
        vh = kv[:, MEM_W + h * MEM_HEAD_DIM:MEM_W + (h + 1) * MEM_HEAD_DIM].astype(BF16)
        s = _dot_t(qh, kh)
        p = jnp.exp(s - jnp.max(s, axis=-1, keepdims=True))
        outs.append(_dot(p.astype(BF16), vh) / jnp.sum(p, axis=-1, keepdims=True))
    o_ref[0] = x + _dot(jnp.concatenate(outs, axis=-1).astype(BF16), wo_ref[...])


def mem_cross_attention(x, mem_kv, g, wq, gq, gk, wo, *, tm=512):
    b, s, d = x.shape
    m = mem_kv.shape[1]
    tm = min(tm, s)
    const = lambda shape: pl.BlockSpec(shape, lambda i, t: (0,) * len(shape))
    tile = pl.BlockSpec((1, tm, d), lambda i, t: (i, t, 0))
    return pl.pallas_call(
        _memx_kernel,
        grid=(b, s // tm),
        in_specs=[tile, const((1, d)), const((d, MEM_W)),
                  pl.BlockSpec((1, m, 2 * MEM_W), lambda i, t: (i, 0, 0)),
                  const((1, MEM_HEAD_DIM)), const((1, MEM_HEAD_DIM)), const((MEM_W, d))],
        out_specs=tile,
        out_shape=jax.ShapeDtypeStruct((b, s, d), F32),
        compiler_params=_params("parallel", "arbitrary"),
        name="mem_cross_attention",
    )(x, g.reshape(1, d).astype(F32), wq, mem_kv, gq.reshape(1, -1).astype(F32),
      gk.reshape(1, -1).astype(F32), wo)


def _rope_tables(positions, dim, lead_ones, tail):
    inv = 1.0 / (ROPE_THETA ** (jnp.arange(0, dim, 2, dtype=F32) / dim))
    ang = positions.astype(F32)[..., None] * inv
    c, s = jnp.cos(ang), jnp.sin(ang)
    shape = positions.shape
    cos = jnp.concatenate([jnp.ones(shape + (lead_ones,), F32), c, c, jnp.ones(shape + (tail,), F32)], axis=-1)
    sin = jnp.concatenate([jnp.zeros(shape + (lead_ones,), F32), -s, s, jnp.zeros(shape + (tail,), F32)], axis=-1)
    reps = LANES // cos.shape[-1]
    return jnp.tile(cos, (1, 1, reps)), jnp.tile(sin, (1, 1, reps))


def _pad_cols(w, cols):
    return jnp.pad(w, ((0, 0), (0, cols - w.shape[1])))


def _ab_in_layout(w):
    c = (SWA_HEADS + 2 * SWA_KV_HEADS) * HEAD_DIM + 3 * RWKV_DIM
    return jnp.concatenate([w[:, :c], _pad_cols(w[:, c:c + DECAY_LORA], LANES),
                            _pad_cols(w[:, c + DECAY_LORA:c + DECAY_LORA + AAA_LORA], LANES),
                            _pad_cols(w[:, c + DECAY_LORA + AAA_LORA:], 2 * LANES)], axis=1)


def _cd_in_layout(w):
    c1 = MLA_Q_RANK + MLA_KV_RANK
    z = lambda n: jnp.zeros((w.shape[0], n), w.dtype)
    return jnp.concatenate([w[:, :c1], z(MLA_NOPE), w[:, c1:c1 + MLA_ROPE], z(LANES - MLA_NOPE - MLA_ROPE),
                            z(LANES), w[:, c1 + MLA_ROPE:]], axis=1)


def _head_slabs(w, per_head):
    k = w.shape[0]
    return jnp.pad(w.reshape(k, -1, per_head), ((0, 0), (0, 0), (0, LANES - per_head))).reshape(k, -1)


def _slab_vec(*parts):
    v = jnp.concatenate([p.astype(F32) for p in parts])
    return jnp.pad(v, (0, LANES - v.shape[0])).reshape(1, LANES)


def kernel(x, mem, positions, ffn1_norm, ffn1_w_gate, ffn1_w_up, ffn1_w_down, mix_norm, ab_w_in, ab_w_out, swa_q_norm, swa_k_norm, swa_sinks, rwkv_mu, rwkv_w0, rwkv_w2, rwkv_a0, rwkv_a2, rwkv_g2, rwkv_k_k, rwkv_k_a, rwkv_r_k, rwkv_gn_g, rwkv_gn_b, cd_w_in, cd_w_out, mla_cq_norm, mla_ckv_norm, mla_w_uq, mla_w_ukv, mla_q_nope_norm, mla_k_nope_norm, mla_q_rope_norm, mla_k_rope_norm, diff_q_norm, diff_k_norm, diff_lq1, diff_lk1, diff_lq2, diff_lk2, diff_subln, memx_norm, memx_w_q, memx_q_norm, memx_w_o, mem_norm, mem_w_kv, mem_k_norm, ffn2_norm, ffn2_w_gate, ffn2_w_up, ffn2_w_down):
    b, s, d = x.shape
    m = mem.shape[1]
    t = b * s
    depth = ffn1_norm.shape[0]
    bf = lambda w: w.astype(BF16)
    cos64, sin64 = _rope_tables(positions, HEAD_DIM, 0, 0)
    cos32, sin32 = _rope_tables(positions, MLA_ROPE, MLA_NOPE, LANES - MLA_NOPE - MLA_ROPE)

    mem_kv = norm_matmul(mem.reshape(b * m, d), mem_norm, bf(mem_w_kv), name="mem_kv").reshape(b, m, 2 * MEM_W)

    x = x.reshape(t, d)
    for layer in range(depth):
        j = layer // 2
        x = ffn(x, ffn1_norm[layer], ffn1_w_gate, ffn1_w_up, ffn1_w_down, layer, name="ffn1")
        if layer % 2 == 0:
            u = norm_matmul(x, mix_norm[layer], _ab_in_layout(bf(ab_w_in[j])), name="ab_in")
            u = u.reshape(b, s, AB_IN_PAD)
            y_a = swa_attention(u, cos64, sin64, _slab_vec(swa_q_norm[j], swa_q_norm[j]),
                                _slab_vec(swa_k_norm[j], swa_k_norm[j]), swa_sinks[j])
            y_b = rwkv_mix(u, rwkv_mu[j], rwkv_w0[j], rwkv_w2[j], rwkv_a0[j], rwkv_a2[j], rwkv_g2[j],
                           rwkv_k_k[j], rwkv_k_a[j], rwkv_r_k[j], rwkv_gn_g[j], rwkv_gn_b[j])
            w_out = bf(ab_w_out[j])
            half = SWA_HEADS * HEAD_DIM
            x = matmul_residual(x, [y_a.reshape(t, -1), y_b.reshape(t, -1)], [w_out[:half], w_out[half:]],
                                name="ab_out")
        else:
            u = norm_matmul(x, mix_norm[layer], _cd_in_layout(bf(cd_w_in[j])), name="cd_in")
            u = u.reshape(b, s, CD_IN_PAD)
            zero64 = jnp.zeros((MLA_NOPE,), F32)
            q_c, k_c, vt_c = mla_prep(u, cos32, sin32,
                                      _slab_vec(mla_q_nope_norm[j], mla_q_rope_norm[j]),
                                      _slab_vec(mla_k_nope_norm[j]),
                                      _slab_vec(zero64, mla_k_rope_norm[j]),
                                      mla_cq_norm[j], mla_ckv_norm[j],
                                      bf(_head_slabs(mla_w_uq[j], MLA_NOPE + MLA_ROPE)), bf(mla_w_ukv[j]))
            y_c = causal_attention(q_c, k_c, vt_c, n_sm=1, hps=8, name="mla_attention")
            q_d, k_d, vt_d = diff_prep(u, cos64, sin64, _slab_vec(diff_q_norm[j], diff_q_norm[j]),
                                       _slab_vec(diff_k_norm[j], diff_k_norm[j]))
            lambda_init = 0.8 - 0.6 * math.exp(-0.3 * layer)
            lam = (jnp.exp(jnp.sum(diff_lq1[j].astype(F32) * diff_lk1[j].astype(F32)))
                   - jnp.exp(jnp.sum(diff_lq2[j].astype(F32) * diff_lk2[j].astype(F32))) + lambda_init)
            y_d = causal_attention(q_d, k_d, vt_d, n_sm=2, hps=4, lam=lam, g=diff_subln[j],
                                   out_scale=1.0 - lambda_init, name="diff_attention")
            w_out = bf(cd_w_out[j])
            n_c = MLA_HEADS * MLA_V
            x = matmul_residual(x, [y_c.reshape(t, -1), y_d.reshape(t, -1)], [w_out[:n_c], w_out[n_c:]],
                                name="cd_out")
        x = mem_cross_attention(x.reshape(b, s, d), mem_kv, memx_norm[layer], bf(memx_w_q[layer]),
                                memx_q_norm[layer], mem_k_norm, bf(memx_w_o[layer])).reshape(t, d)
        x = ffn(x, ffn2_norm[layer], ffn2_w_gate, ffn2_w_up, ffn2_w_down, layer, name="ffn2")
    return x.reshape(b, s, d)
```

```python
import functools
import math

import jax
import jax.numpy as jnp
from jax import lax
from jax.experimental import pallas as pl
from jax.experimental.pallas import tpu as pltpu

F32 = jnp.float32
BF16 = jnp.bfloat16

EPS = 1e-6
ROPE_THETA = 10000.0
NEG_INF = -1e30
LOG2E = math.log2(math.e)
ATTN_LOOKAHEAD = 8
ATTN_TILE = 256
SEG_W = 256
LANES = 128

D_MODEL = 2048
D_FF = 5632
HEAD_DIM = 64
SWA_HEADS = 16
SWA_KV_HEADS = 4
SWA_BLOCK = 128
RWKV_DIM = 1024
RWKV_CHUNK = 64
RWKV_GN_EPS = 64e-5
DECAY_LORA, AAA_LORA, GATE_LORA = 64, 64, 160
MLA_HEADS, MLA_Q_RANK, MLA_KV_RANK, MLA_NOPE, MLA_ROPE, MLA_V = 16, 512, 256, 64, 32, 64
DIFF_HEADS, DIFF_QK, DIFF_V = 8, 64, 128
MEM_HEADS, MEM_HEAD_DIM = 4, 128
MEM_W = MEM_HEADS * MEM_HEAD_DIM
AB_IN_PAD = 5120
CD_IN_PAD = 4096

VMEM_LIMIT = 48 * 1024 * 1024
FFN_VMEM_LIMIT = 60 * 1024 * 1024


def _params(*sem):
    return pltpu.CompilerParams(dimension_semantics=sem, vmem_limit_bytes=VMEM_LIMIT)


def _dot(a, b):
    return jnp.dot(a, b, preferred_element_type=F32)


def _dot_t(a, b):
    return lax.dot_general(a, b, (((1,), (1,)), ((), ())), preferred_element_type=F32)


def _dot_0(a, b):
    return lax.dot_general(a, b, (((0,), (0,)), ((), ())), preferred_element_type=F32)


def _rms(x, g):
    return x * lax.rsqrt(jnp.mean(x * x, axis=-1, keepdims=True) + EPS) * g


def _seg_sum(x, e):
    hi = x.astype(BF16)
    lo = (x - hi.astype(F32)).astype(BF16)
    w = e.shape[0]
    parts = [_dot(hi[:, i:i + w], e) + _dot(lo[:, i:i + w], e) for i in range(0, x.shape[1], w)]
    return parts[0] if len(parts) == 1 else jnp.concatenate(parts, axis=-1)


def _tile_lanes(v, width):
    return v if v.shape[-1] == width else jnp.tile(v, (1, width // v.shape[-1]))


def _store_transposed_tiles(vt_ref, v):
    for r in range(vt_ref.shape[1]):
        vt_ref[0, r] = v[r * ATTN_TILE:(r + 1) * ATTN_TILE, :].T.astype(vt_ref.dtype)


def _swap_halves(x, half):
    w = x.shape[-1]
    lane = lax.broadcasted_iota(jnp.int32, x.shape, x.ndim - 1)
    low = (lane & (2 * half - 1)) < half
    return jnp.where(low, pltpu.roll(x, w - half, x.ndim - 1), pltpu.roll(x, half, x.ndim - 1))


def _norm_matmul_kernel(x_ref, g_ref, w_ref, o_ref, xn_ref):
    @pl.when(pl.program_id(1) == 0)
    def _():
        xn_ref[...] = _rms(x_ref[...].astype(F32), g_ref[...]).astype(BF16)

    o_ref[...] = _dot(xn_ref[...], w_ref[...]).astype(o_ref.dtype)


def norm_matmul(x, g, w, *, k_blk=0, out_dtype=F32, tm=1024, tn=512, name="norm_matmul"):
    t = x.shape[0]
    k, n = w.shape
    tm, tn = min(tm, t), min(tn, n)
    assert t % tm == 0 and n % tn == 0
    return pl.pallas_call(
        _norm_matmul_kernel,
        grid=(t // tm, n // tn),
        in_specs=[pl.BlockSpec((tm, k), lambda i, j: (i, k_blk)),
                  pl.BlockSpec((1, k), lambda i, j: (0, 0)),
                  pl.BlockSpec((k, tn), lambda i, j: (0, j))],
        out_specs=pl.BlockSpec((tm, tn), lambda i, j: (i, j)),
        out_shape=jax.ShapeDtypeStruct((t, n), out_dtype),
        scratch_shapes=[pltpu.VMEM((tm, k), BF16)],
        compiler_params=_params("parallel", "arbitrary"),
        name=name,
    )(x, g.reshape(1, k).astype(F32), w)


def _matmul_res_kernel(*refs, n_in):
    x_ref, o_ref = refs[2 * n_in], refs[2 * n_in + 1]
    acc = x_ref[...]
    for i in range(n_in):
        acc = acc + _dot(refs[i][...], refs[n_in + i][...])
    o_ref[...] = acc


def matmul_residual(x, a_list, w_list, *, tm=256, name="matmul_residual"):
    t, n = x.shape
    tm = min(tm, t)
    assert t % tm == 0
    n_in = len(a_list)
    in_specs = ([pl.BlockSpec((tm, a.shape[1]), lambda i: (i, 0)) for a in a_list]
                + [pl.BlockSpec(w.shape, lambda i: (0, 0)) for w in w_list]
                + [pl.BlockSpec((tm, n), lambda i: (i, 0))])
    return pl.pallas_call(
        functools.partial(_matmul_res_kernel, n_in=n_in),
        grid=(t // tm,),
        in_specs=in_specs,
        out_specs=pl.BlockSpec((tm, n), lambda i: (i, 0)),
        out_shape=jax.ShapeDtypeStruct((t, n), F32),
        compiler_params=_params("parallel"),
        name=name,
    )(*a_list, *w_list, x)


def _ffn_kernel(x_ref, g_ref, wg_ref, wu_ref, wd_ref, o_ref, xn_ref):
    @pl.when(pl.program_id(1) == 0)
    def _():
        x = x_ref[...]
        xn_ref[...] = _rms(x, g_ref[...]).astype(BF16)
        o_ref[...] = x

    xn = xn_ref[...]
    a = _dot(xn, wg_ref[...].astype(BF16))
    b = _dot(xn, wu_ref[...].astype(BF16))
    h = (a * (0.5 / (1.0 + jnp.exp(-a))) * b).astype(BF16)
    o_ref[...] += _dot(h, wd_ref[...].astype(BF16))


def ffn(x, g, wg, wu, wd, layer, *, tm=1024, tf=256, name="ffn"):
    t, d = x.shape
    ff = wg.shape[2]
    tm = min(tm, t)
    assert t % tm == 0 and ff % tf == 0
    nf = ff // tf
    return pl.pallas_call(
        _ffn_kernel,
        grid=(t // tm, nf),
        in_specs=[pl.BlockSpec((tm, d), lambda i, f: (i, 0)),
                  pl.BlockSpec((1, d), lambda i, f: (0, 0)),
                  pl.BlockSpec((None, d, tf), lambda i, f: (layer, 0, f)),
                  pl.BlockSpec((None, d, tf), lambda i, f: (layer, 0, f)),
                  pl.BlockSpec((None, tf, d), lambda i, f: (layer, f, 0))],
        out_specs=pl.BlockSpec((tm, d), lambda i, f: (i, 0)),
        out_shape=jax.ShapeDtypeStruct((t, d), F32),
        scratch_shapes=[pltpu.VMEM((tm, d), BF16)],
        compiler_params=pltpu.CompilerParams(dimension_semantics=("parallel", "arbitrary"),
                                             vmem_limit_bytes=FFN_VMEM_LIMIT),
        name=name,
    )(x, g.reshape(1, d).astype(F32), wg, wu, wd)


def _swa_kernel(sink_ref, q_ref, kc_ref, kp_ref, vc_ref, vp_ref, cc_ref, sc_ref, cp_ref, sp_ref,
                gq_ref, gk_ref, e_ref, o_ref):
    n = pl.program_id(1)
    blk = SWA_BLOCK
    group = SWA_HEADS // SWA_KV_HEADS
    q = q_ref[0]
    k = jnp.concatenate([kp_ref[0], kc_ref[0]], axis=0)
    v = jnp.concatenate([vp_ref[0], vc_ref[0]], axis=0)
    cos_q, sin_q = cc_ref[0], sc_ref[0]
    cos_k = jnp.concatenate([cp_ref[0], cos_q], axis=0)
    sin_k = jnp.concatenate([sp_ref[0], sin_q], axis=0)
    def prep(x, g, cos, sin, scale):
        w = x.shape[1]
        xg = x * _tile_lanes(g, w)
        xr = xg * _tile_lanes(cos, w) + _swap_halves(xg, HEAD_DIM // 2) * _tile_lanes(sin, w)
        return xr * lax.rsqrt(_seg_sum(x * x, e_ref[...]) * (1.0 / HEAD_DIM) + EPS) * scale

    qr = prep(q, gq_ref[...], cos_q, sin_q, HEAD_DIM ** -0.5 * LOG2E).astype(BF16)
    kr = prep(k, gk_ref[...], cos_k, sin_k, 1.0)
    key_i = lax.broadcasted_iota(jnp.int32, (2 * blk, blk), 0)
    qry_i = lax.broadcasted_iota(jnp.int32, (2 * blk, blk), 1)
    rel = qry_i + blk - key_i
    valid = (rel >= 0) & (rel < blk) & ((n > 0) | (key_i >= blk))
    low = lax.broadcasted_iota(jnp.int32, (blk, LANES), 1) < HEAD_DIM
    zero = jnp.zeros((blk, LANES), BF16)
    k_dup, v_t = [], []
    for g in range(SWA_KV_HEADS):
        kg = kr[:, g * HEAD_DIM:(g + 1) * HEAD_DIM]
        k_dup.append(jnp.concatenate([kg, kg], axis=-1).astype(BF16))
        v_t.append(v[:, g * HEAD_DIM:(g + 1) * HEAD_DIM].T.astype(BF16))

    def scores(h):
        slab = qr[:, (h // 2) * LANES:(h // 2 + 1) * LANES]
        qh = jnp.where(low, slab, zero) if h % 2 == 0 else jnp.where(low, zero, slab)
        return _dot_t(k_dup[h // group], qh)

    ahead = [scores(h) for h in range(ATTN_LOOKAHEAD)]
    ones = jnp.ones((8, 2 * blk), BF16)
    outs = []
    for h in range(SWA_HEADS):
        s = jnp.where(valid, ahead.pop(0), NEG_INF)
        if h + ATTN_LOOKAHEAD < SWA_HEADS:
            ahead.append(scores(h + ATTN_LOOKAHEAD))
        sink = sink_ref[h] * LOG2E
        m = jnp.maximum(jnp.max(s, axis=0, keepdims=True), sink)
        p = jnp.exp2(s - m).astype(BF16)
        den = _dot(ones, p)[0:1] + jnp.exp2(sink - m)
        outs.append(_dot(v_t[h // group], p) / den)
    slabs = [jnp.concatenate(outs[i:i + 2], axis=0).T for i in range(0, SWA_HEADS, 2)]
    o_ref[0] = jnp.concatenate(slabs, axis=-1).astype(o_ref.dtype)


def swa_attention(u, cos, sin, gq, gk, sinks):
    b, s, _ = u.shape
    nb = s // SWA_BLOCK
    qw, kw = SWA_HEADS * HEAD_DIM, SWA_KV_HEADS * HEAD_DIM
    cur = lambda c: (lambda i, n: (i, n, c))
    prev = lambda c: (lambda i, n: (i, jnp.maximum(n - 1, 0), c))
    tab = pl.BlockSpec((1, SWA_BLOCK, LANES), cur(0))
    tab_prev = pl.BlockSpec((1, SWA_BLOCK, LANES), prev(0))
    gain = pl.BlockSpec((1, LANES), lambda i, n: (0, 0))
    seg = (jnp.arange(kw)[:, None] // HEAD_DIM == jnp.arange(kw)[None, :] // HEAD_DIM).astype(BF16)
    return pl.pallas_call(
        _swa_kernel,
        grid=(b, nb),
        in_specs=[pl.BlockSpec(memory_space=pltpu.SMEM),
                  pl.BlockSpec((1, SWA_BLOCK, qw), cur(0)),
                  pl.BlockSpec((1, SWA_BLOCK, kw), cur(qw // kw)),
                  pl.BlockSpec((1, SWA_BLOCK, kw), prev(qw // kw)),
                  pl.BlockSpec((1, SWA_BLOCK, kw), cur(qw // kw + 1)),
                  pl.BlockSpec((1, SWA_BLOCK, kw), prev(qw // kw + 1)),
                  tab, tab, tab_prev, tab_prev, gain, gain, pl.BlockSpec((kw, kw), lambda i, n: (0, 0))],
        out_specs=pl.BlockSpec((1, SWA_BLOCK, qw), cur(0)),
        out_shape=jax.ShapeDtypeStruct((b, s, qw), BF16),
        compiler_params=_params("parallel", "arbitrary"),
        name="swa_attention",
    )(sinks.astype(F32), u, u, u, u, u, cos, sin, cos, sin, gq, gk, seg)


def _mm(a, b, dims, passes):
    dn = (dims, ((), ()))
    dg = lambda x, y: lax.dot_general(x, y, dn, preferred_element_type=F32)
    ah = a.astype(BF16)
    bh = b.astype(BF16)
    if passes == 1:
        return dg(ah, bh)
    al = (a - ah.astype(F32)).astype(BF16)
    bl = (b - bh.astype(F32)).astype(BF16)
    return dg(ah, bh) + dg(ah, bl) + dg(al, bh)


_NN = ((1,), (0,))
_NT = ((1,), (1,))
_TN = ((0,), (0,))
P_SC, P_INV, P_PQ, P_OUT, P_ST = 1, 1, 1, 1, 1
RWKV_UNROLL = 8


RWKV_PREP_ROWS = 256


def _rwkv_kernel(ur_ref, uk_ref, uv_ref, ul_ref, mur_ref, muk_ref, muv_ref, mul_ref, w0_ref, w2_ref, a0_ref,
                 a2_ref, g2_ref, kkw_ref, ka_ref, gng_ref, gnb_ref, rk_ref,
                 o_ref, r_ref, k_ref, v_ref, kk_ref, b_ref, lw_ref, g_ref,
                 st_ref, y1_ref, y0_ref, n_ref, z_ref, dec_ref, *, nchunk):
    c = RWKV_CHUNK
    lane_c = lax.broadcasted_iota(jnp.int32, (c, LANES), 1)
    head0 = lane_c < HEAD_DIM
    ri = lax.broadcasted_iota(jnp.int32, (2 * c, 2 * c), 0)
    ci = lax.broadcasted_iota(jnp.int32, (2 * c, 2 * c), 1)
    eye = jnp.where(ri == ci, 1.0, 0.0)
    tril_c = jnp.where(lax.broadcasted_iota(jnp.int32, (c, c), 0) >= lax.broadcasted_iota(jnp.int32, (c, c), 1),
                       1.0, 0.0).astype(BF16)
    stack = lambda x: jnp.concatenate([jnp.where(head0, x, 0.0), jnp.where(head0, 0.0, x)], axis=0)

    def seg_mean(x):
        first = lax.broadcasted_iota(jnp.int32, x.shape, 1) < HEAD_DIM
        m0 = jnp.sum(jnp.where(first, x, 0.0), axis=-1, keepdims=True)
        m1 = jnp.sum(jnp.where(first, 0.0, x), axis=-1, keepdims=True)
        return jnp.where(first, m0, m1) * (1.0 / HEAD_DIM)

    def prep(io, first_group):
        group_rows = RWKV_UNROLL * c
        for tix in range(group_rows // RWKV_PREP_ROWS):
            start = pl.multiple_of(io * group_rows + tix * RWKV_PREP_ROWS, RWKV_PREP_ROWS)
            rows = pl.ds(start, RWKV_PREP_ROWS)
            at_start = first_group and tix == 0

            def shifted(ref, mu_ref):
                x = ref[0, rows, :]
                if at_start:
                    last = jnp.zeros((1, x.shape[1]), F32)
                else:
                    last = ref[0, pl.ds(pl.multiple_of(start - 8, 8), 8), :][7:8, :]
                row = lax.broadcasted_iota(jnp.int32, x.shape, 0)
                prev = jnp.where(row == 0, last, pltpu.roll(x, 1, 0))
                return x + (prev - x) * mu_ref[...]

            r = shifted(ur_ref, mur_ref)
            k = shifted(uk_ref, muk_ref)
            v = shifted(uv_ref, muv_ref)
            lo = shifted(ul_ref, mul_ref)
            yield
            w_lo, a_lo, g_lo = lo[:, 0:LANES], lo[:, LANES:2 * LANES], lo[:, 2 * LANES:4 * LANES]
            z = -(w0_ref[...] + _mm(jnp.tanh(w_lo), w2_ref[...], _NN, 3))
            w = -(jnp.maximum(z, 0.0) + jnp.log(1.0 + jnp.exp(-jnp.abs(z)))) - 0.5
            a = 1.0 / (1.0 + jnp.exp(-(a0_ref[...] + _mm(a_lo, a2_ref[...], _NN, 1))))
            g = _mm(1.0 / (1.0 + jnp.exp(-g_lo)), g2_ref[...], _NN, 1)
            yield
            kk = k * kkw_ref[...]
            kk = kk / jnp.maximum(jnp.sqrt(seg_mean(kk * kk) * float(HEAD_DIM)), 1e-12)
            r_ref[rows, :] = r
            k_ref[rows, :] = k * (1.0 + (a - 1.0) * ka_ref[...])
            v_ref[rows, :] = v
            kk_ref[rows, :] = kk
            b_ref[rows, :] = kk * a
            lw_ref[rows, :] = -jnp.exp(w)
            g_ref[rows, :] = g
            yield

    def build(ics):
        each = lambda f, *cols: [f(*args) for args in zip(*cols)]
        sls = [pl.ds(pl.multiple_of(ic * c, c), c) for ic in ics]
        load = lambda ref: [ref[sl, :] for sl in sls]
        r, k, v, kk, b, lw = (load(ref) for ref in (r_ref, k_ref, v_ref, kk_ref, b_ref, lw_ref))

        def running_sum(x):
            l1 = x.astype(BF16)
            rest = x - l1.astype(F32)
            l2 = rest.astype(BF16)
            l3 = (rest - l2.astype(F32)).astype(BF16)
            return _dot(tril_c, l1) + _dot(tril_c, l2) + _dot(tril_c, l3)

        cum = each(running_sum, lw)
        yield
        cum_end = [x[c - 1:c, :] for x in cum]
        e_neg = each(lambda x: jnp.exp(-x), cum)
        e_end = each(lambda x, xe: jnp.exp(xe - x), cum, cum_end)
        a_s = each(lambda kk_, x, l: stack(-kk_ * jnp.exp(x - l)), kk, cum, lw)
        r_s = each(lambda r_, x: stack(r_ * jnp.exp(x)), r, cum)
        b_s = each(lambda b_, e: stack(b_ * e), b, e_neg)
        k_s = each(lambda k_, e: stack(k_ * e), k, e_neg)
        bh_s = each(lambda b_, e: stack(b_ * e), b, e_end)
        kh_s = each(lambda k_, e: stack(k_ * e), k, e_end)
        v_s = each(stack, v)
        n2 = 2 * c
        sc = each(lambda a_, r_, b_, k_: _mm(jnp.concatenate([a_, r_], axis=0),
                                             jnp.concatenate([b_, k_], axis=0), _NT, P_SC), a_s, r_s, b_s, k_s)
        low = [jnp.where(ri > ci, x[:n2, :n2], 0.0) for x in sc]
        a_ak = [jnp.where(ri > ci, x[:n2, n2:], 0.0) for x in sc]
        a_rb = [jnp.where(ri >= ci, x[n2:, :n2], 0.0) for x in sc]
        a_rk = [jnp.where(ri >= ci, x[n2:, n2:], 0.0) for x in sc]
        yield
        inv = [eye + x for x in low]
        pw = each(lambda x: _mm(x, x, _NN, P_INV), low)
        yield
        levels = 5
        for lvl in range(levels - 1):
            both = each(lambda t, x: _mm(jnp.concatenate([t, x], axis=0), x, _NN, P_INV), inv, pw)
            inv = each(lambda t, r: t + r[:n2], inv, both)
            pw = [r[n2:] for r in both]
            yield
        inv = each(lambda t, x: t + _mm(t, x, _NN, P_INV), inv, pw)
        yield
        akv = each(lambda x, y: _mm(x, y, _NN, P_PQ), a_ak, v_s)
        yield
        pq = each(lambda t, x, y: _mm(t, jnp.concatenate([x, y], axis=1), _NN, P_PQ), inv, a_s, akv)
        yield
        yy = each(lambda x, y: _mm(x, y, _NN, P_OUT), a_rb, pq)
        y0b = each(lambda x, y: _mm(x, y, _NN, P_OUT), a_rk, v_s)
        nz = each(lambda x, y: _mm(x, y, _TN, P_OUT), pq, bh_s)
        zb = each(lambda x, y: _mm(x, y, _TN, P_OUT), v_s, kh_s)
        for i, ic in enumerate(ics):
            y1_ref[ic] = r_s[i] + yy[i][:, :LANES]
            y0_ref[ic] = yy[i][:, LANES:] + y0b[i]
            n_ref[ic] = nz[i][:LANES]
            z_ref[ic] = nz[i][LANES:] + zb[i]
            dec_ref[ic] = jnp.broadcast_to(jnp.exp(cum_end[i]), (8, LANES))

    def emit(ic, st):
        sl = pl.ds(pl.multiple_of(ic * c, c), c)
        r, k, v, g = r_ref[sl, :], k_ref[sl, :], v_ref[sl, :], g_ref[sl, :]
        y_st = _mm(y1_ref[ic], st, _NT, P_ST) + y0_ref[ic]
        y = y_st[0:c] + y_st[c:2 * c]
        mean = seg_mean(y)
        var = seg_mean((y - mean) * (y - mean))
        yn = (y - mean) * lax.rsqrt(var + RWKV_GN_EPS) * gng_ref[...] + gnb_ref[...]
        bonus = seg_mean(r * k * rk_ref[...]) * float(HEAD_DIM) * v
        o_ref[0, sl, :] = ((yn + bonus) * g).astype(o_ref.dtype)
        return st * dec_ref[ic][0:1, :] + _mm(st, n_ref[ic], _NN, P_ST) + z_ref[ic]

    def scan(ics):
        st = st_ref[...]
        for ic in ics:
            st = emit(ic, st)
            yield
        st_ref[...] = st

    def run(*gens):
        live = list(gens)
        while live:
            live = [gen for gen in live if next(gen, live) is not live]

    group = lambda io: [io * RWKV_UNROLL + i for i in range(RWKV_UNROLL)]
    ngroup = nchunk // RWKV_UNROLL
    st_ref[...] = jnp.zeros_like(st_ref)
    run(prep(0, True))
    if ngroup > 1:
        run(prep(1, False), build(group(0)))
    else:
        run(build(group(0)))

    @pl.loop(1, ngroup - 1)
    def _(io):
        run(prep(io + 1, False), build(group(io)), scan(group(io - 1)))

    if ngroup > 1:
        run(build(group(ngroup - 1)), scan(group(ngroup - 2)))
    run(scan(group(ngroup - 1)))


def rwkv_mix(u, mu, w0, w2, a0, a2, g2, k_k, k_a, r_k, gn_g, gn_b):
    bsz, s, _ = u.shape
    npair = RWKV_DIM // LANES
    nchunk = s // RWKV_CHUNK
    assert nchunk % RWKV_UNROLL == 0 and (RWKV_UNROLL * RWKV_CHUNK) % RWKV_PREP_ROWS == 0
    base = (SWA_HEADS + 2 * SWA_KV_HEADS) * HEAD_DIM // LANES
    lora_w = 4 * LANES
    slab = lambda off: pl.BlockSpec((1, s, LANES), lambda i, p: (i, 0, off + p))
    vec = pl.BlockSpec((1, LANES), lambda i, p: (0, p))
    cols = lambda rows: pl.BlockSpec((rows, LANES), lambda i, p: (0, p))
    row = lambda vv: vv.reshape(1, -1).astype(F32)
    pad_rows = lambda m, rows: jnp.pad(m, ((0, rows - m.shape[0]), (0, 0))).astype(F32)
    pad_cols = lambda vv, n: jnp.pad(vv, (0, n - vv.shape[0]))
    c3 = 3 * RWKV_DIM
    mu_l = jnp.concatenate([pad_cols(mu[c3:c3 + DECAY_LORA], LANES),
                            pad_cols(mu[c3 + DECAY_LORA:c3 + DECAY_LORA + AAA_LORA], LANES),
                            pad_cols(mu[c3 + DECAY_LORA + AAA_LORA:], 2 * LANES)])
    seq = pltpu.VMEM((s, LANES), F32)
    mat = pltpu.VMEM((nchunk, LANES, LANES), F32)
    return pl.pallas_call(
        functools.partial(_rwkv_kernel, nchunk=nchunk),
        grid=(bsz, npair),
        in_specs=[slab(base), slab(base + npair), slab(base + 2 * npair),
                  pl.BlockSpec((1, s, lora_w), lambda i, p: (i, 0, (base + 3 * npair) * LANES // lora_w)),
                  vec, vec, vec, pl.BlockSpec((1, lora_w), lambda i, p: (0, 0)),
                  vec, cols(LANES), vec, cols(LANES), cols(2 * LANES), vec, vec, vec, vec, vec],
        out_specs=pl.BlockSpec((1, s, LANES), lambda i, p: (i, 0, p)),
        out_shape=jax.ShapeDtypeStruct((bsz, s, RWKV_DIM), BF16),
        scratch_shapes=[seq] * 7 + [pltpu.VMEM((LANES, LANES), F32), mat, mat, mat, mat,
                                    pltpu.VMEM((nchunk, 8, LANES), F32)],
        compiler_params=_params("parallel", "arbitrary"),
        name="rwkv_mix",
    )(u, u, u, u, row(mu[:RWKV_DIM]), row(mu[RWKV_DIM:2 * RWKV_DIM]), row(mu[2 * RWKV_DIM:c3]), row(mu_l),
      row(w0), pad_rows(w2, LANES), row(a0), pad_rows(a2, LANES), pad_rows(g2, 2 * LANES),
      row(k_k), row(k_a), row(gn_g), row(gn_b), row(r_k))


def _mla_prep_kernel(cq_ref, ckv_ref, pe_ref, cos_ref, sin_ref, e_ref, gq_ref, gkn_ref, gkp_ref, invn_ref,
                     gcq_ref, gckv_ref, wuq_ref, wukv_ref, qo_ref, ko_ref, vo_ref):
    cos, sin = cos_ref[0], sin_ref[0]
    half = MLA_ROPE // 2

    def rope(x, g):
        w = x.shape[1]
        xg = x * _tile_lanes(g, w)
        return xg * _tile_lanes(cos, w) + _swap_halves(xg, half) * _tile_lanes(sin, w)

    x = _dot(_rms(cq_ref[0], gcq_ref[...]).astype(BF16), wuq_ref[...])
    w = x.shape[1]
    inv_n = _tile_lanes(invn_ref[...], w)
    inv = lax.rsqrt(_seg_sum(x * x, e_ref[...]) * inv_n + EPS)
    qo_ref[0] = (rope(x, gq_ref[...]) * inv * ((MLA_NOPE + MLA_ROPE) ** -0.5 * LOG2E)).astype(qo_ref.dtype)
    kv = _dot(_rms(ckv_ref[0], gckv_ref[...]).astype(BF16), wukv_ref[...])
    inv_k = lax.rsqrt(_seg_sum(kv * kv, e_ref[...]) * inv_n + EPS)
    k_nope = kv * inv_k * _tile_lanes(gkn_ref[...], w)
    pe = pe_ref[0]
    inv_pe = lax.rsqrt(jnp.sum(pe * pe, axis=-1, keepdims=True) * (1.0 / MLA_ROPE) + EPS)
    k_pe = rope(pe, gkp_ref[...]) * inv_pe
    ko_ref[0] = (k_nope + _tile_lanes(k_pe, w)).astype(ko_ref.dtype)
    v = jnp.concatenate([kv[:, h * LANES + MLA_NOPE:(h + 1) * LANES] for h in range(w // LANES)], axis=-1)
    _store_transposed_tiles(vo_ref, v)


def mla_prep(u, cos, sin, gq, gkn, gkp, gcq, gckv, w_uq, w_ukv, *, ts=512, tc=512):
    b, s, _ = u.shape
    wtot = w_uq.shape[1]
    ts = min(ts, s)
    lane = jnp.arange(SEG_W)
    same = (lane[:, None] // LANES == lane[None, :] // LANES)
    pos = lane % LANES
    nope = pos < MLA_NOPE
    pe = (pos >= MLA_NOPE) & (pos < MLA_NOPE + MLA_ROPE)
    seg = (same & ((nope[:, None] & nope[None, :]) | (pe[:, None] & pe[None, :]))).astype(BF16)
    p1 = jnp.arange(LANES)
    inv_n = jnp.where(p1 < MLA_NOPE, 1.0 / MLA_NOPE, jnp.where(p1 < MLA_NOPE + MLA_ROPE, 1.0 / MLA_ROPE, 0.0))
    blk = pl.BlockSpec((1, ts, tc), lambda i, t, c: (i, t, c))
    tab = pl.BlockSpec((1, ts, LANES), lambda i, t, c: (i, t, 0))
    vec = pl.BlockSpec((1, LANES), lambda i, t, c: (0, 0))
    pe_blk = (MLA_Q_RANK + MLA_KV_RANK) // LANES
    out = jax.ShapeDtypeStruct((b, s, wtot), BF16)
    return pl.pallas_call(
        _mla_prep_kernel,
        grid=(b, s // ts, wtot // tc),
        in_specs=[pl.BlockSpec((1, ts, MLA_Q_RANK), lambda i, t, c: (i, t, 0)),
                  pl.BlockSpec((1, ts, MLA_KV_RANK), lambda i, t, c: (i, t, MLA_Q_RANK // MLA_KV_RANK)),
                  pl.BlockSpec((1, ts, LANES), lambda i, t, c: (i, t, pe_blk)), tab, tab,
                  pl.BlockSpec((SEG_W, SEG_W), lambda i, t, c: (0, 0)), vec, vec, vec, vec,
                  pl.BlockSpec((1, MLA_Q_RANK), lambda i, t, c: (0, 0)),
                  pl.BlockSpec((1, MLA_KV_RANK), lambda i, t, c: (0, 0)),
                  pl.BlockSpec((MLA_Q_RANK, tc), lambda i, t, c: (0, c)),
                  pl.BlockSpec((MLA_KV_RANK, tc), lambda i, t, c: (0, c))],
        out_specs=[blk, blk, pl.BlockSpec((1, ts // ATTN_TILE, tc // LANES * MLA_V, ATTN_TILE),
                                          lambda i, t, c: (i, t, c, 0))],
        out_shape=[out, out, jax.ShapeDtypeStruct((b, s // ATTN_TILE, wtot // LANES * MLA_V, ATTN_TILE), BF16)],
        compiler_params=_params("parallel", "parallel", "arbitrary"),
        name="mla_prep",
    )(u, u, u, cos, sin, seg, gq, gkn, gkp, inv_n.reshape(1, LANES).astype(F32),
      gcq.reshape(1, -1).astype(F32), gckv.reshape(1, -1).astype(F32), w_uq, w_ukv)


def _diff_prep_kernel(q_ref, k_ref, v_ref, cos_ref, sin_ref, e_ref, gq_ref, gk_ref, qo_ref, ko_ref, vo_ref):
    cos, sin = cos_ref[0], sin_ref[0]
    _store_transposed_tiles(vo_ref, v_ref[0])

    def prep(x, g, scale):
        w = x.shape[1]
        xg = x * _tile_lanes(g, w)
        xr = xg * _tile_lanes(cos, w) + _swap_halves(xg, DIFF_QK // 2) * _tile_lanes(sin, w)
        inv = lax.rsqrt(_seg_sum(x * x, e_ref[...]) * (1.0 / DIFF_QK) + EPS)
        return xr * inv * scale

    qo_ref[0] = prep(q_ref[0], gq_ref[...], DIFF_QK ** -0.5 * LOG2E).astype(qo_ref.dtype)
    ko_ref[0] = prep(k_ref[0], gk_ref[...], 1.0).astype(ko_ref.dtype)


def diff_prep(u, cos, sin, gq, gk, *, ts=512, tc=512):
    b, s, _ = u.shape
    ts = min(ts, s)
    wtot = 2 * DIFF_HEADS * DIFF_QK
    q_base = (CD_IN_PAD - 3 * wtot) // tc
    seg = (jnp.arange(SEG_W)[:, None] // DIFF_QK == jnp.arange(SEG_W)[None, :] // DIFF_QK).astype(BF16)
    blk = lambda off: pl.BlockSpec((1, ts, tc), lambda i, t, c: (i, t, off + c))
    tab = pl.BlockSpec((1, ts, LANES), lambda i, t, c: (i, t, 0))
    vec = pl.BlockSpec((1, LANES), lambda i, t, c: (0, 0))
    out = jax.ShapeDtypeStruct((b, s, wtot), BF16)
    return pl.pallas_call(
        _diff_prep_kernel,
        grid=(b, s // ts, wtot // tc),
        in_specs=[blk(q_base), blk(q_base + wtot // tc), blk(q_base + 2 * wtot // tc), tab, tab,
                  pl.BlockSpec((SEG_W, SEG_W), lambda i, t, c: (0, 0)), vec, vec],
        out_specs=[blk(0), blk(0),
                   pl.BlockSpec((1, ts // ATTN_TILE, tc, ATTN_TILE), lambda i, t, c: (i, t, c, 0))],
        out_shape=[out, out, jax.ShapeDtypeStruct((b, s // ATTN_TILE, wtot, ATTN_TILE), BF16)],
        compiler_params=_params("parallel", "parallel", "arbitrary"),
        name="diff_prep",
    )(u, u, u, cos, sin, seg, gq, gk)


def _causal_attn_kernel(lam_ref, q_ref, k_ref, vt_ref, g_ref, o_ref, *, n_sm, tq, ow, out_scale):
    qi = pl.program_id(2)
    q = q_ref[0]
    hps = q.shape[1] // LANES
    slab = lambda x, h: x[:, h * LANES:(h + 1) * LANES]
    lane = lax.broadcasted_iota(jnp.int32, (tq, LANES), 1)
    qs, src = [], []
    for h in range(hps):
        qh = slab(q, h)
        if n_sm == 2:
            zero = jnp.zeros_like(qh)
            qs += [jnp.where(lane < DIFF_QK, qh, zero), jnp.where(lane < DIFF_QK, zero, qh)]
            src += [h, h]
        else:
            qs.append(qh)
            src.append(h)
    nch = len(qs)
    key_i = lax.broadcasted_iota(jnp.int32, (tq, tq), 0)
    qry_i = lax.broadcasted_iota(jnp.int32, (tq, tq), 1)
    ones = jnp.ones((8, tq), BF16)

    def step(j, carry, diagonal):
        kj = k_ref[0, pl.ds(pl.multiple_of(j * tq, tq), tq), :]
        scores = lambda i: _dot_t(slab(kj, src[i]), qs[i])
        new = []
        ahead = [scores(i) for i in range(min(ATTN_LOOKAHEAD, nch))]
        for i in range(nch):
            s = ahead.pop(0)
            if i + ATTN_LOOKAHEAD < nch:
                ahead.append(scores(i + ATTN_LOOKAHEAD))
            if diagonal:
                s = jnp.where(key_i <= qry_i, s, NEG_INF)
            m, l, acc = carry[3 * i:3 * i + 3]
            m_new = jnp.maximum(m, jnp.max(s, axis=0, keepdims=True))
            alpha = jnp.exp2(m - m_new)
            p = jnp.exp2(s - m_new).astype(BF16)
            new += [m_new, alpha * l + _dot(ones, p)[0:1],
                    alpha * acc + _dot(vt_ref[0, j, src[i] * ow:(src[i] + 1) * ow, :], p)]
        return tuple(new)

    init = (jnp.full((1, tq), NEG_INF, F32), jnp.zeros((1, tq), F32),
            jnp.zeros((ow, tq), F32)) * nch
    carry = lax.fori_loop(0, qi, lambda j, cr: step(j, cr, False), init)
    carry = step(qi, carry, True)
    outs = []
    for h in range(hps):
        c0 = 3 * n_sm * h
        o = carry[c0 + 2] / carry[c0 + 1]
        if n_sm == 2:
            o = o - lam_ref[0] * (carry[c0 + 5] / carry[c0 + 4])
            o = o * lax.rsqrt(jnp.mean(o * o, axis=0, keepdims=True) + EPS) * g_ref[...] * out_scale
        outs.append(o.T.astype(o_ref.dtype))
    o_ref[0] = jnp.concatenate(outs, axis=-1)


def causal_attention(q, k, vt, *, n_sm, lam=None, g=None, out_scale=1.0, hps=4, name="causal_attention"):
    b, s, wtot = q.shape
    tq = vt.shape[3]
    width = hps * LANES
    groups = wtot // width
    ow = vt.shape[2] // (wtot // LANES)
    lam = jnp.zeros((1,), F32) if lam is None else lam.reshape(1).astype(F32)
    g = jnp.ones((ow, 1), F32) if g is None else g.reshape(ow, 1).astype(F32)
    seq = pl.BlockSpec((1, s, width), lambda i, h, t: (i, 0, h))
    tile = pl.BlockSpec((1, tq, width), lambda i, h, t: (i, t, h))
    return pl.pallas_call(
        functools.partial(_causal_attn_kernel, n_sm=n_sm, tq=tq, ow=ow, out_scale=out_scale),
        grid=(b, groups, s // tq),
        in_specs=[pl.BlockSpec(memory_space=pltpu.SMEM), tile, seq,
                  pl.BlockSpec((1, s // tq, hps * ow, tq), lambda i, h, t: (i, 0, h, 0)),
                  pl.BlockSpec((ow, 1), lambda i, h, t: (0, 0))],
        out_specs=pl.BlockSpec((1, tq, hps * ow), lambda i, h, t: (i, t, h)),
        out_shape=jax.ShapeDtypeStruct((b, s, groups * hps * ow), BF16),
        compiler_params=_params("parallel", "parallel", "arbitrary"),
        name=name,
    )(lam, q, k, vt, g)


def _memx_kernel(x_ref, g_ref, wq_ref, kv_ref, gq_ref, gk_ref, wo_ref, o_ref):
    x = x_ref[0]
    q = _dot(_rms(x, g_ref[...]).astype(BF16), wq_ref[...])
    kv = kv_ref[0]
    outs = []
    for h in range(MEM_HEADS):
        sl = slice(h * MEM_HEAD_DIM, (h + 1) * MEM_HEAD_DIM)
        qh = (_rms(q[:, sl], gq_ref[...]) * MEM_HEAD_DIM ** -0.5).astype(BF16)
        kh = _rms(kv[:, sl], gk_ref[...]).astype(BF16)
        vh = kv[:, MEM_W + h * MEM_HEAD_DIM:MEM_W + (h + 1) * MEM_HEAD_DIM].astype(BF16)
        s = _dot_t(qh, kh)
        p = jnp.exp(s - jnp.max(s, axis=-1, keepdims=True))
        outs.append(_dot(p.astype(BF16), vh) / jnp.sum(p, axis=-1, keepdims=True))
    o_ref[0] = x + _dot(jnp.concatenate(outs, axis=-1).astype(BF16), wo_ref[...])


def mem_cross_attention(x, mem_kv, g, wq, gq, gk, wo, *, tm=512):
    b, s, d = x.shape
    m = mem_kv.shape[1]
    tm = min(tm, s)
    const = lambda shape: pl.BlockSpec(shape, lambda i, t: (0,) * len(shape))
    tile = pl.BlockSpec((1, tm, d), lambda i, t: (i, t, 0))
    return pl.pallas_call(
        _memx_kernel,
        grid=(b, s // tm),
        in_specs=[tile, const((1, d)), const((d, MEM_W)),
                  pl.BlockSpec((1, m, 2 * MEM_W), lambda i, t: (i, 0, 0)),
                  const((1, MEM_HEAD_DIM)), const((1, MEM_HEAD_DIM)), const((MEM_W, d))],
        out_specs=tile,
        out_shape=jax.ShapeDtypeStruct((b, s, d), F32),
        compiler_params=_params("parallel", "arbitrary"),
        name="mem_cross_attention",
    )(x, g.reshape(1, d).astype(F32), wq, mem_kv, gq.reshape(1, -1).astype(F32),
      gk.reshape(1, -1).astype(F32), wo)


def _rope_tables(positions, dim, lead_ones, tail):
    inv = 1.0 / (ROPE_THETA ** (jnp.arange(0, dim, 2, dtype=F32) / dim))
    ang = positions.astype(F32)[..., None] * inv
    c, s = jnp.cos(ang), jnp.sin(ang)
    shape = positions.shape
    cos = jnp.concatenate([jnp.ones(shape + (lead_ones,), F32), c, c, jnp.ones(shape + (tail,), F32)], axis=-1)
    sin = jnp.concatenate([jnp.zeros(shape + (lead_ones,), F32), -s, s, jnp.zeros(shape + (tail,), F32)], axis=-1)
    reps = LANES // cos.shape[-1]
    return jnp.tile(cos, (1, 1, reps)), jnp.tile(sin, (1, 1, reps))


def _pad_cols(w, cols):
    return jnp.pad(w, ((0, 0), (0, cols - w.shape[1])))


def _ab_in_layout(w):
    c = (SWA_HEADS + 2 * SWA_KV_HEADS) * HEAD_DIM + 3 * RWKV_DIM
    return jnp.concatenate([w[:, :c], _pad_cols(w[:, c:c + DECAY_LORA], LANES),
                            _pad_cols(w[:, c + DECAY_LORA:c + DECAY_LORA + AAA_LORA], LANES),
                            _pad_cols(w[:, c + DECAY_LORA + AAA_LORA:], 2 * LANES)], axis=1)


def _cd_in_layout(w):
    c1 = MLA_Q_RANK + MLA_KV_RANK
    z = lambda n: jnp.zeros((w.shape[0], n), w.dtype)
    return jnp.concatenate([w[:, :c1], z(MLA_NOPE), w[:, c1:c1 + MLA_ROPE], z(LANES - MLA_NOPE - MLA_ROPE),
                            z(LANES), w[:, c1 + MLA_ROPE:]], axis=1)


def _head_slabs(w, per_head):
    k = w.shape[0]
    return jnp.pad(w.reshape(k, -1, per_head), ((0, 0), (0, 0), (0, LANES - per_head))).reshape(k, -1)


def _slab_vec(*parts):
    v = jnp.concatenate([p.astype(F32) for p in parts])
    return jnp.pad(v, (0, LANES - v.shape[0])).reshape(1, LANES)


def kernel(x, mem, positions, ffn1_norm, ffn1_w_gate, ffn1_w_up, ffn1_w_down, mix_norm, ab_w_in, ab_w_out, swa_q_norm, swa_k_norm, swa_sinks, rwkv_mu, rwkv_w0, rwkv_w2, rwkv_a0, rwkv_a2, rwkv_g2, rwkv_k_k, rwkv_k_a, rwkv_r_k, rwkv_gn_g, rwkv_gn_b, cd_w_in, cd_w_out, mla_cq_norm, mla_ckv_norm, mla_w_uq, mla_w_ukv, mla_q_nope_norm, mla_k_nope_norm, mla_q_rope_norm, mla_k_rope_norm, diff_q_norm, diff_k_norm, diff_lq1, diff_lk1, diff_lq2, diff_lk2, diff_subln, memx_norm, memx_w_q, memx_q_norm, memx_w_o, mem_norm, mem_w_kv, mem_k_norm, ffn2_norm, ffn2_w_gate, ffn2_w_up, ffn2_w_down):
    b, s, d = x.shape
    m = mem.shape[1]
    t = b * s
    depth = ffn1_norm.shape[0]
    bf = lambda w: w.astype(BF16)
    cos64, sin64 = _rope_tables(positions, HEAD_DIM, 0, 0)
    cos32, sin32 = _rope_tables(positions, MLA_ROPE, MLA_NOPE, LANES - MLA_NOPE - MLA_ROPE)

    mem_kv = norm_matmul(mem.reshape(b * m, d), mem_norm, bf(mem_w_kv), name="mem_kv").reshape(b, m, 2 * MEM_W)

    x = x.reshape(t, d)
    for layer in range(depth):
        j = layer // 2
        x = ffn(x, ffn1_norm[layer], ffn1_w_gate, ffn1_w_up, ffn1_w_down, layer, name="ffn1")
        if layer % 2 == 0:
            u = norm_matmul(x, mix_norm[layer], _ab_in_layout(bf(ab_w_in[j])), name="ab_in")
            u = u.reshape(b, s, AB_IN_PAD)
            y_a = swa_attention(u, cos64, sin64, _slab_vec(swa_q_norm[j], swa_q_norm[j]),
                                _slab_vec(swa_k_norm[j], swa_k_norm[j]), swa_sinks[j])
            y_b = rwkv_mix(u, rwkv_mu[j], rwkv_w0[j], rwkv_w2[j], rwkv_a0[j], rwkv_a2[j], rwkv_g2[j],
                           rwkv_k_k[j], rwkv_k_a[j], rwkv_r_k[j], rwkv_gn_g[j], rwkv_gn_b[j])
            w_out = bf(ab_w_out[j])
            half = SWA_HEADS * HEAD_DIM
            x = matmul_residual(x, [y_a.reshape(t, -1), y_b.reshape(t, -1)], [w_out[:half], w_out[half:]],
                                name="ab_out")
        else:
            u = norm_matmul(x, mix_norm[layer], _cd_in_layout(bf(cd_w_in[j])), name="cd_in")
            u = u.reshape(b, s, CD_IN_PAD)
            zero64 = jnp.zeros((MLA_NOPE,), F32)
            q_c, k_c, vt_c = mla_prep(u, cos32, sin32,
                                      _slab_vec(mla_q_nope_norm[j], mla_q_rope_norm[j]),
                                      _slab_vec(mla_k_nope_norm[j]),
                                      _slab_vec(zero64, mla_k_rope_norm[j]),
                                      mla_cq_norm[j], mla_ckv_norm[j],
                                      bf(_head_slabs(mla_w_uq[j], MLA_NOPE + MLA_ROPE)), bf(mla_w_ukv[j]))
            y_c = causal_attention(q_c, k_c, vt_c, n_sm=1, hps=8, name="mla_attention")
            q_d, k_d, vt_d = diff_prep(u, cos64, sin64, _slab_vec(diff_q_norm[j], diff_q_norm[j]),
                                       _slab_vec(diff_k_norm[j], diff_k_norm[j]))
            lambda_init = 0.8 - 0.6 * math.exp(-0.3 * layer)
            lam = (jnp.exp(jnp.sum(diff_lq1[j].astype(F32) * diff_lk1[j].astype(F32)))
                   - jnp.exp(jnp.sum(diff_lq2[j].astype(F32) * diff_lk2[j].astype(F32))) + lambda_init)
            y_d = causal_attention(q_d, k_d, vt_d, n_sm=2, hps=4, lam=lam, g=diff_subln[j],
                                   out_scale=1.0 - lambda_init, name="diff_attention")
            w_out = bf(cd_w_out[j])
            n_c = MLA_HEADS * MLA_V
            x = matmul_residual(x, [y_c.reshape(t, -1), y_d.reshape(t, -1)], [w_out[:n_c], w_out[n_c:]],
                                name="cd_out")
        x = mem_cross_attention(x.reshape(b, s, d), mem_kv, memx_norm[layer], bf(memx_w_q[layer]),
                                memx_q_norm[layer], mem_k_norm, bf(memx_w_o[layer])).reshape(t, d)
        x = ffn(x, ffn2_norm[layer], ffn2_w_gate, ffn2_w_up, ffn2_w_down, layer, name="ffn2")
    return x.reshape(b, s, d)
```

```python
import functools
import math

import jax
import jax.numpy as jnp
from jax import lax
from jax.experimental import pallas as pl
from jax.experimental.pallas import tpu as pltpu

F32 = jnp.float32
BF16 = jnp.bfloat16

EPS = 1e-6
ROPE_THETA = 10000.0
NEG_INF = -1e30
LOG2E = math.log2(math.e)
ATTN_LOOKAHEAD = 8
ATTN_TILE = 256
SEG_W = 256
LANES = 128

D_MODEL = 2048
D_FF = 5632
HEAD_DIM = 64
SWA_HEADS = 16
SWA_KV_HEADS = 4
SWA_BLOCK = 128
RWKV_DIM = 1024
RWKV_CHUNK = 64
RWKV_GN_EPS = 64e-5
DECAY_LORA, AAA_LORA, GATE_LORA = 64, 64, 160
MLA_HEADS, MLA_Q_RANK, MLA_KV_RANK, MLA_NOPE, MLA_ROPE, MLA_V = 16, 512, 256, 64, 32, 64
DIFF_HEADS, DIFF_QK, DIFF_V = 8, 64, 128
MEM_HEADS, MEM_HEAD_DIM = 4, 128
MEM_W = MEM_HEADS * MEM_HEAD_DIM
AB_IN_PAD = 5120
CD_IN_PAD = 4096

VMEM_LIMIT = 48 * 1024 * 1024
FFN_VMEM_LIMIT = 60 * 1024 * 1024


def _params(*sem):
    return pltpu.CompilerParams(dimension_semantics=sem, vmem_limit_bytes=VMEM_LIMIT)


def _dot(a, b):
    return jnp.dot(a, b, preferred_element_type=F32)


def _dot_t(a, b):
    return lax.dot_general(a, b, (((1,), (1,)), ((), ())), preferred_element_type=F32)


def _dot_0(a, b):
    return lax.dot_general(a, b, (((0,), (0,)), ((), ())), preferred_element_type=F32)


def _rms(x, g):
    return x * lax.rsqrt(jnp.mean(x * x, axis=-1, keepdims=True) + EPS) * g


def _seg_sum(x, e):
    hi = x.astype(BF16)
    lo = (x - hi.astype(F32)).astype(BF16)
    w = e.shape[0]
    parts = [_dot(hi[:, i:i + w], e) + _dot(lo[:, i:i + w], e) for i in range(0, x.shape[1], w)]
    return parts[0] if len(parts) == 1 else jnp.concatenate(parts, axis=-1)


def _tile_lanes(v, width):
    return v if v.shape[-1] == width else jnp.tile(v, (1, width // v.shape[-1]))


def _store_transposed_tiles(vt_ref, v):
    for r in range(vt_ref.shape[1]):
        vt_ref[0, r] = v[r * ATTN_TILE:(r + 1) * ATTN_TILE, :].T.astype(vt_ref.dtype)


def _swap_halves(x, half):
    w = x.shape[-1]
    lane = lax.broadcasted_iota(jnp.int32, x.shape, x.ndim - 1)
    low = (lane & (2 * half - 1)) < half
    return jnp.where(low, pltpu.roll(x, w - half, x.ndim - 1), pltpu.roll(x, half, x.ndim - 1))


def _norm_matmul_kernel(x_ref, g_ref, w_ref, o_ref, xn_ref):
    @pl.when(pl.program_id(1) == 0)
    def _():
        xn_ref[...] = _rms(x_ref[...].astype(F32), g_ref[...]).astype(BF16)

    o_ref[...] = _dot(xn_ref[...], w_ref[...]).astype(o_ref.dtype)


def norm_matmul(x, g, w, *, k_blk=0, out_dtype=F32, tm=1024, tn=512, name="norm_matmul"):
    t = x.shape[0]
    k, n = w.shape
    tm, tn = min(tm, t), min(tn, n)
    assert t % tm == 0 and n % tn == 0
    return pl.pallas_call(
        _norm_matmul_kernel,
        grid=(t // tm, n // tn),
        in_specs=[pl.BlockSpec((tm, k), lambda i, j: (i, k_blk)),
                  pl.BlockSpec((1, k), lambda i, j: (0, 0)),
                  pl.BlockSpec((k, tn), lambda i, j: (0, j))],
        out_specs=pl.BlockSpec((tm, tn), lambda i, j: (i, j)),
        out_shape=jax.ShapeDtypeStruct((t, n), out_dtype),
        scratch_shapes=[pltpu.VMEM((tm, k), BF16)],
        compiler_params=_params("parallel", "arbitrary"),
        name=name,
    )(x, g.reshape(1, k).astype(F32), w)


def _ffn_kernel(x_ref, g_ref, wg_ref, wu_ref, wd_ref, o_ref, xn_ref):
    @pl.when(pl.program_id(1) == 0)
    def _():
        x = x_ref[...]
        xn_ref[...] = _rms(x, g_ref[...]).astype(BF16)
        o_ref[...] = x

    xn = xn_ref[...]
    a = _dot(xn, wg_ref[...].astype(BF16))
    b = _dot(xn, wu_ref[...].astype(BF16))
    h = (a * (0.5 / (1.0 + jnp.exp(-a))) * b).astype(BF16)
    o_ref[...] += _dot(h, wd_ref[...].astype(BF16))


def ffn(x, g, wg, wu, wd, layer, *, tm=1024, tf=256, name="ffn"):
    t, d = x.shape
    ff = wg.shape[2]
    tm = min(tm, t)
    assert t % tm == 0 and ff % tf == 0
    nf = ff // tf
    return pl.pallas_call(
        _ffn_kernel,
        grid=(t // tm, nf),
        in_specs=[pl.BlockSpec((tm, d), lambda i, f: (i, 0)),
                  pl.BlockSpec((1, d), lambda i, f: (0, 0)),
                  pl.BlockSpec((None, d, tf), lambda i, f: (layer, 0, f)),
                  pl.BlockSpec((None, d, tf), lambda i, f: (layer, 0, f)),
                  pl.BlockSpec((None, tf, d), lambda i, f: (layer, f, 0))],
        out_specs=pl.BlockSpec((tm, d), lambda i, f: (i, 0)),
        out_shape=jax.ShapeDtypeStruct((t, d), F32),
        scratch_shapes=[pltpu.VMEM((tm, d), BF16)],
        compiler_params=pltpu.CompilerParams(dimension_semantics=("parallel", "arbitrary"),
                                             vmem_limit_bytes=FFN_VMEM_LIMIT),
        name=name,
    )(x, g.reshape(1, d).astype(F32), wg, wu, wd)


def _swa_kernel(sink_ref, q_ref, kc_ref, kp_ref, vc_ref, vp_ref, cc_ref, sc_ref, cp_ref, sp_ref,
                gq_ref, gk_ref, e_ref, o_ref):
    n = pl.program_id(1)
    blk = SWA_BLOCK
    group = SWA_HEADS // SWA_KV_HEADS
    q = q_ref[0]
    k = jnp.concatenate([kp_ref[0], kc_ref[0]], axis=0)
    v = jnp.concatenate([vp_ref[0], vc_ref[0]], axis=0)
    cos_q, sin_q = cc_ref[0], sc_ref[0]
    cos_k = jnp.concatenate([cp_ref[0], cos_q], axis=0)
    sin_k = jnp.concatenate([sp_ref[0], sin_q], axis=0)
    def prep(x, g, cos, sin, scale):
        w = x.shape[1]
        xg = x * _tile_lanes(g, w)
        xr = xg * _tile_lanes(cos, w) + _swap_halves(xg, HEAD_DIM // 2) * _tile_lanes(sin, w)
        return xr * lax.rsqrt(_seg_sum(x * x, e_ref[...]) * (1.0 / HEAD_DIM) + EPS) * scale

    qr = prep(q, gq_ref[...], cos_q, sin_q, HEAD_DIM ** -0.5 * LOG2E).astype(BF16)
    kr = prep(k, gk_ref[...], cos_k, sin_k, 1.0)
    key_i = lax.broadcasted_iota(jnp.int32, (2 * blk, blk), 0)
    qry_i = lax.broadcasted_iota(jnp.int32, (2 * blk, blk), 1)
    rel = qry_i + blk - key_i
    valid = (rel >= 0) & (rel < blk) & ((n > 0) | (key_i >= blk))
    low = lax.broadcasted_iota(jnp.int32, (blk, LANES), 1) < HEAD_DIM
    zero = jnp.zeros((blk, LANES), BF16)
    k_dup, v_t = [], []
    for g in range(SWA_KV_HEADS):
        kg = kr[:, g * HEAD_DIM:(g + 1) * HEAD_DIM]
        k_dup.append(jnp.concatenate([kg, kg], axis=-1).astype(BF16))
        v_t.append(v[:, g * HEAD_DIM:(g + 1) * HEAD_DIM].T.astype(BF16))

    def scores(h):
        slab = qr[:, (h // 2) * LANES:(h // 2 + 1) * LANES]
        qh = jnp.where(low, slab, zero) if h % 2 == 0 else jnp.where(low, zero, slab)
        return _dot_t(k_dup[h // group], qh)

    ahead = [scores(h) for h in range(ATTN_LOOKAHEAD)]
    ones = jnp.ones((8, 2 * blk), BF16)
    outs = []
    for h in range(SWA_HEADS):
        s = jnp.where(valid, ahead.pop(0), NEG_INF)
        if h + ATTN_LOOKAHEAD < SWA_HEADS:
            ahead.append(scores(h + ATTN_LOOKAHEAD))
        sink = sink_ref[h] * LOG2E
        m = jnp.maximum(jnp.max(s, axis=0, keepdims=True), sink)
        p = jnp.exp2(s - m).astype(BF16)
        den = _dot(ones, p)[0:1] + jnp.exp2(sink - m)
        outs.append(_dot(v_t[h // group], p) / den)
    slabs = [jnp.concatenate(outs[i:i + 2], axis=0).T for i in range(0, SWA_HEADS, 2)]
    o_ref[0] = jnp.concatenate(slabs, axis=-1).astype(o_ref.dtype)


def swa_attention(u, cos, sin, gq, gk, sinks):
    b, s, _ = u.shape
    nb = s // SWA_BLOCK
    qw, kw = SWA_HEADS * HEAD_DIM, SWA_KV_HEADS * HEAD_DIM
    cur = lambda c: (lambda i, n: (i, n, c))
    prev = lambda c: (lambda i, n: (i, jnp.maximum(n - 1, 0), c))
    tab = pl.BlockSpec((1, SWA_BLOCK, LANES), cur(0))
    tab_prev = pl.BlockSpec((1, SWA_BLOCK, LANES), prev(0))
    gain = pl.BlockSpec((1, LANES), lambda i, n: (0, 0))
    seg = (jnp.arange(kw)[:, None] // HEAD_DIM == jnp.arange(kw)[None, :] // HEAD_DIM).astype(BF16)
    return pl.pallas_call(
        _swa_kernel,
        grid=(b, nb),
        in_specs=[pl.BlockSpec(memory_space=pltpu.SMEM),
                  pl.BlockSpec((1, SWA_BLOCK, qw), cur(0)),
                  pl.BlockSpec((1, SWA_BLOCK, kw), cur(qw // kw)),
                  pl.BlockSpec((1, SWA_BLOCK, kw), prev(qw // kw)),
                  pl.BlockSpec((1, SWA_BLOCK, kw), cur(qw // kw + 1)),
                  pl.BlockSpec((1, SWA_BLOCK, kw), prev(qw // kw + 1)),
                  tab, tab, tab_prev, tab_prev, gain, gain, pl.BlockSpec((kw, kw), lambda i, n: (0, 0))],
        out_specs=pl.BlockSpec((1, SWA_BLOCK, qw), cur(0)),
        out_shape=jax.ShapeDtypeStruct((b, s, qw), BF16),
        compiler_params=_params("parallel", "arbitrary"),
        name="swa_attention",
    )(sinks.astype(F32), u, u, u, u, u, cos, sin, cos, sin, gq, gk, seg)


def _mm(a, b, dims, passes):
    dn = (dims, ((), ()))
    dg = lambda x, y: lax.dot_general(x, y, dn, preferred_element_type=F32)
    ah = a.astype(BF16)
    bh = b.astype(BF16)
    if passes == 1:
        return dg(ah, bh)
    al = (a - ah.astype(F32)).astype(BF16)
    bl = (b - bh.astype(F32)).astype(BF16)
    return dg(ah, bh) + dg(ah, bl) + dg(al, bh)


_NN = ((1,), (0,))
_NT = ((1,), (1,))
_TN = ((0,), (0,))
P_SC, P_INV, P_PQ, P_OUT, P_ST = 1, 1, 1, 1, 1
RWKV_UNROLL = 8


RWKV_PREP_ROWS = 256


def _rwkv_kernel(ur_ref, uk_ref, uv_ref, ul_ref, mur_ref, muk_ref, muv_ref, mul_ref, w0_ref, w2_ref, a0_ref,
                 a2_ref, g2_ref, kkw_ref, ka_ref, gng_ref, gnb_ref, rk_ref,
                 o_ref, r_ref, k_ref, v_ref, kk_ref, b_ref, lw_ref, g_ref,
                 st_ref, y1_ref, y0_ref, n_ref, z_ref, dec_ref, *, nchunk):
    c = RWKV_CHUNK
    lane_c = lax.broadcasted_iota(jnp.int32, (c, LANES), 1)
    head0 = lane_c < HEAD_DIM
    ri = lax.broadcasted_iota(jnp.int32, (2 * c, 2 * c), 0)
    ci = lax.broadcasted_iota(jnp.int32, (2 * c, 2 * c), 1)
    eye = jnp.where(ri == ci, 1.0, 0.0)
    tril_c = jnp.where(lax.broadcasted_iota(jnp.int32, (c, c), 0) >= lax.broadcasted_iota(jnp.int32, (c, c), 1),
                       1.0, 0.0).astype(BF16)
    stack = lambda x: jnp.concatenate([jnp.where(head0, x, 0.0), jnp.where(head0, 0.0, x)], axis=0)

    def seg_mean(x):
        first = lax.broadcasted_iota(jnp.int32, x.shape, 1) < HEAD_DIM
        m0 = jnp.sum(jnp.where(first, x, 0.0), axis=-1, keepdims=True)
        m1 = jnp.sum(jnp.where(first, 0.0, x), axis=-1, keepdims=True)
        return jnp.where(first, m0, m1) * (1.0 / HEAD_DIM)

    def prep(io, first_group):
        group_rows = RWKV_UNROLL * c
        for tix in range(group_rows // RWKV_PREP_ROWS):
            start = pl.multiple_of(io * group_rows + tix * RWKV_PREP_ROWS, RWKV_PREP_ROWS)
            rows = pl.ds(start, RWKV_PREP_ROWS)
            at_start = first_group and tix == 0

            def shifted(ref, mu_ref):
                x = ref[0, rows, :]
                if at_start:
                    last = jnp.zeros((1, x.shape[1]), F32)
                else:
                    last = ref[0, pl.ds(pl.multiple_of(start - 8, 8), 8), :][7:8, :]
                row = lax.broadcasted_iota(jnp.int32, x.shape, 0)
                prev = jnp.where(row == 0, last, pltpu.roll(x, 1, 0))
                return x + (prev - x) * mu_ref[...]

            r = shifted(ur_ref, mur_ref)
            k = shifted(uk_ref, muk_ref)
            v = shifted(uv_ref, muv_ref)
            lo = shifted(ul_ref, mul_ref)
            yield
            w_lo, a_lo, g_lo = lo[:, 0:LANES], lo[:, LANES:2 * LANES], lo[:, 2 * LANES:4 * LANES]
            z = -(w0_ref[...] + _mm(jnp.tanh(w_lo), w2_ref[...], _NN, 3))
            w = -(jnp.maximum(z, 0.0) + jnp.log(1.0 + jnp.exp(-jnp.abs(z)))) - 0.5
            a = 1.0 / (1.0 + jnp.exp(-(a0_ref[...] + _mm(a_lo, a2_ref[...], _NN, 1))))
            g = _mm(1.0 / (1.0 + jnp.exp(-g_lo)), g2_ref[...], _NN, 1)
            yield
            kk = k * kkw_ref[...]
            kk = kk / jnp.maximum(jnp.sqrt(seg_mean(kk * kk) * float(HEAD_DIM)), 1e-12)
            r_ref[rows, :] = r
            k_ref[rows, :] = k * (1.0 + (a - 1.0) * ka_ref[...])
            v_ref[rows, :] = v
            kk_ref[rows, :] = kk
            b_ref[rows, :] = kk * a
            lw_ref[rows, :] = -jnp.exp(w)
            g_ref[rows, :] = g
            yield

    def build(ics):
        each = lambda f, *cols: [f(*args) for args in zip(*cols)]
        sls = [pl.ds(pl.multiple_of(ic * c, c), c) for ic in ics]
        load = lambda ref: [ref[sl, :] for sl in sls]
        r, k, v, kk, b, lw = (load(ref) for ref in (r_ref, k_ref, v_ref, kk_ref, b_ref, lw_ref))

        def running_sum(x):
            l1 = x.astype(BF16)
            rest = x - l1.astype(F32)
            l2 = rest.astype(BF16)
            l3 = (rest - l2.astype(F32)).astype(BF16)
            return _dot(tril_c, l1) + _dot(tril_c, l2) + _dot(tril_c, l3)

        cum = each(running_sum, lw)
        yield
        cum_end = [x[c - 1:c, :] for x in cum]
        e_neg = each(lambda x: jnp.exp(-x), cum)
        e_end = each(lambda x, xe: jnp.exp(xe - x), cum, cum_end)
        a_s = each(lambda kk_, x, l: stack(-kk_ * jnp.exp(x - l)), kk, cum, lw)
        r_s = each(lambda r_, x: stack(r_ * jnp.exp(x)), r, cum)
        b_s = each(lambda b_, e: stack(b_ * e), b, e_neg)
        k_s = each(lambda k_, e: stack(k_ * e), k, e_neg)
        bh_s = each(lambda b_, e: stack(b_ * e), b, e_end)
        kh_s = each(lambda k_, e: stack(k_ * e), k, e_end)
        v_s = each(stack, v)
        n2 = 2 * c
        sc = each(lambda a_, r_, b_, k_: _mm(jnp.concatenate([a_, r_], axis=0),
                                             jnp.concatenate([b_, k_], axis=0), _NT, P_SC), a_s, r_s, b_s, k_s)
        low = [jnp.where(ri > ci, x[:n2, :n2], 0.0) for x in sc]
        a_ak = [jnp.where(ri > ci, x[:n2, n2:], 0.0) for x in sc]
        a_rb = [jnp.where(ri >= ci, x[n2:, :n2], 0.0) for x in sc]
        a_rk = [jnp.where(ri >= ci, x[n2:, n2:], 0.0) for x in sc]
        yield
        inv = [eye + x for x in low]
        pw = each(lambda x: _mm(x, x, _NN, P_INV), low)
        yield
        levels = 5
        for lvl in range(levels - 1):
            both = each(lambda t, x: _mm(jnp.concatenate([t, x], axis=0), x, _NN, P_INV), inv, pw)
            inv = each(lambda t, r: t + r[:n2], inv, both)
            pw = [r[n2:] for r in both]
            yield
        inv = each(lambda t, x: t + _mm(t, x, _NN, P_INV), inv, pw)
        yield
        akv = each(lambda x, y: _mm(x, y, _NN, P_PQ), a_ak, v_s)
        yield
        pq = each(lambda t, x, y: _mm(t, jnp.concatenate([x, y], axis=1), _NN, P_PQ), inv, a_s, akv)
        yield
        yy = each(lambda x, y: _mm(x, y, _NN, P_OUT), a_rb, pq)
        y0b = each(lambda x, y: _mm(x, y, _NN, P_OUT), a_rk, v_s)
        nz = each(lambda x, y: _mm(x, y, _TN, P_OUT), pq, bh_s)
        zb = each(lambda x, y: _mm(x, y, _TN, P_OUT), v_s, kh_s)
        for i, ic in enumerate(ics):
            y1_ref[ic] = r_s[i] + yy[i][:, :LANES]
            y0_ref[ic] = yy[i][:, LANES:] + y0b[i]
            n_ref[ic] = nz[i][:LANES]
            z_ref[ic] = nz[i][LANES:] + zb[i]
            dec_ref[ic] = jnp.broadcast_to(jnp.exp(cum_end[i]), (8, LANES))

    def emit(ic, st):
        sl = pl.ds(pl.multiple_of(ic * c, c), c)
        r, k, v, g = r_ref[sl, :], k_ref[sl, :], v_ref[sl, :], g_ref[sl, :]
        y_st = _mm(y1_ref[ic], st, _NT, P_ST) + y0_ref[ic]
        y = y_st[0:c] + y_st[c:2 * c]
        mean = seg_mean(y)
        var = seg_mean((y - mean) * (y - mean))
        yn = (y - mean) * lax.rsqrt(var + RWKV_GN_EPS) * gng_ref[...] + gnb_ref[...]
        bonus = seg_mean(r * k * rk_ref[...]) * float(HEAD_DIM) * v
        o_ref[0, sl, :] = ((yn + bonus) * g).astype(o_ref.dtype)
        return st * dec_ref[ic][0:1, :] + _mm(st, n_ref[ic], _NN, P_ST) + z_ref[ic]

    def scan(ics):
        st = st_ref[...]
        for ic in ics:
            st = emit(ic, st)
            yield
        st_ref[...] = st

    def run(*gens):
        live = list(gens)
        while live:
            live = [gen for gen in live if next(gen, live) is not live]

    group = lambda io: [io * RWKV_UNROLL + i for i in range(RWKV_UNROLL)]
    ngroup = nchunk // RWKV_UNROLL
    st_ref[...] = jnp.zeros_like(st_ref)
    run(prep(0, True))
    if ngroup > 1:
        run(prep(1, False), build(group(0)))
    else:
        run(build(group(0)))

    @pl.loop(1, ngroup - 1)
    def _(io):
        run(prep(io + 1, False), build(group(io)), scan(group(io - 1)))

    if ngroup > 1:
        run(build(group(ngroup - 1)), scan(group(ngroup - 2)))
    run(scan(group(ngroup - 1)))


def rwkv_mix(u, mu, w0, w2, a0, a2, g2, k_k, k_a, r_k, gn_g, gn_b):
    bsz, s, _ = u.shape
    npair = RWKV_DIM // LANES
    nchunk = s // RWKV_CHUNK
    assert nchunk % RWKV_UNROLL == 0 and (RWKV_UNROLL * RWKV_CHUNK) % RWKV_PREP_ROWS == 0
    base = (SWA_HEADS + 2 * SWA_KV_HEADS) * HEAD_DIM // LANES
    lora_w = 4 * LANES
    slab = lambda off: pl.BlockSpec((1, s, LANES), lambda i, p: (i, 0, off + p))
    vec = pl.BlockSpec((1, LANES), lambda i, p: (0, p))
    cols = lambda rows: pl.BlockSpec((rows, LANES), lambda i, p: (0, p))
    row = lambda vv: vv.reshape(1, -1).astype(F32)
    pad_rows = lambda m, rows: jnp.pad(m, ((0, rows - m.shape[0]), (0, 0))).astype(F32)
    pad_cols = lambda vv, n: jnp.pad(vv, (0, n - vv.shape[0]))
    c3 = 3 * RWKV_DIM
    mu_l = jnp.concatenate([pad_cols(mu[c3:c3 + DECAY_LORA], LANES),
                            pad_cols(mu[c3 + DECAY_LORA:c3 + DECAY_LORA + AAA_LORA], LANES),
                            pad_cols(mu[c3 + DECAY_LORA + AAA_LORA:], 2 * LANES)])
    seq = pltpu.VMEM((s, LANES), F32)
    mat = pltpu.VMEM((nchunk, LANES, LANES), F32)
    return pl.pallas_call(
        functools.partial(_rwkv_kernel, nchunk=nchunk),
        grid=(bsz, npair),
        in_specs=[slab(base), slab(base + npair), slab(base + 2 * npair),
                  pl.BlockSpec((1, s, lora_w), lambda i, p: (i, 0, (base + 3 * npair) * LANES // lora_w)),
                  vec, vec, vec, pl.BlockSpec((1, lora_w), lambda i, p: (0, 0)),
                  vec, cols(LANES), vec, cols(LANES), cols(2 * LANES), vec, vec, vec, vec, vec],
        out_specs=pl.BlockSpec((1, s, LANES), lambda i, p: (i, 0, p)),
        out_shape=jax.ShapeDtypeStruct((bsz, s, RWKV_DIM), BF16),
        scratch_shapes=[seq] * 7 + [pltpu.VMEM((LANES, LANES), F32), mat, mat, mat, mat,
                                    pltpu.VMEM((nchunk, 8, LANES), F32)],
        compiler_params=_params("parallel", "arbitrary"),
        name="rwkv_mix",
    )(u, u, u, u, row(mu[:RWKV_DIM]), row(mu[RWKV_DIM:2 * RWKV_DIM]), row(mu[2 * RWKV_DIM:c3]), row(mu_l),
      row(w0), pad_rows(w2, LANES), row(a0), pad_rows(a2, LANES), pad_rows(g2, 2 * LANES),
      row(k_k), row(k_a), row(gn_g), row(gn_b), row(r_k))


def _mla_prep_kernel(cq_ref, ckv_ref, pe_ref, cos_ref, sin_ref, e_ref, gq_ref, gkn_ref, gkp_ref, invn_ref,
                     gcq_ref, gckv_ref, wuq_ref, wukv_ref, qo_ref, ko_ref, vo_ref):
    cos, sin = cos_ref[0], sin_ref[0]
    half = MLA_ROPE // 2

    def rope(x, g):
        w = x.shape[1]
        xg = x * _tile_lanes(g, w)
        return xg * _tile_lanes(cos, w) + _swap_halves(xg, half) * _tile_lanes(sin, w)

    x = _dot(_rms(cq_ref[0], gcq_ref[...]).astype(BF16), wuq_ref[...])
    w = x.shape[1]
    inv_n = _tile_lanes(invn_ref[...], w)
    inv = lax.rsqrt(_seg_sum(x * x, e_ref[...]) * inv_n + EPS)
    qo_ref[0] = (rope(x, gq_ref[...]) * inv * ((MLA_NOPE + MLA_ROPE) ** -0.5 * LOG2E)).astype(qo_ref.dtype)
    kv = _dot(_rms(ckv_ref[0], gckv_ref[...]).astype(BF16), wukv_ref[...])
    inv_k = lax.rsqrt(_seg_sum(kv * kv, e_ref[...]) * inv_n + EPS)
    k_nope = kv * inv_k * _tile_lanes(gkn_ref[...], w)
    pe = pe_ref[0]
    inv_pe = lax.rsqrt(jnp.sum(pe * pe, axis=-1, keepdims=True) * (1.0 / MLA_ROPE) + EPS)
    k_pe = rope(pe, gkp_ref[...]) * inv_pe
    ko_ref[0] = (k_nope + _tile_lanes(k_pe, w)).astype(ko_ref.dtype)
    v = jnp.concatenate([kv[:, h * LANES + MLA_NOPE:(h + 1) * LANES] for h in range(w // LANES)], axis=-1)
    _store_transposed_tiles(vo_ref, v)


def mla_prep(u, cos, sin, gq, gkn, gkp, gcq, gckv, w_uq, w_ukv, *, ts=512, tc=512):
    b, s, _ = u.shape
    wtot = w_uq.shape[1]
    ts = min(ts, s)
    lane = jnp.arange(SEG_W)
    same = (lane[:, None] // LANES == lane[None, :] // LANES)
    pos = lane % LANES
    nope = pos < MLA_NOPE
    pe = (pos >= MLA_NOPE) & (pos < MLA_NOPE + MLA_ROPE)
    seg = (same & ((nope[:, None] & nope[None, :]) | (pe[:, None] & pe[None, :]))).astype(BF16)
    p1 = jnp.arange(LANES)
    inv_n = jnp.where(p1 < MLA_NOPE, 1.0 / MLA_NOPE, jnp.where(p1 < MLA_NOPE + MLA_ROPE, 1.0 / MLA_ROPE, 0.0))
    blk = pl.BlockSpec((1, ts, tc), lambda i, t, c: (i, t, c))
    tab = pl.BlockSpec((1, ts, LANES), lambda i, t, c: (i, t, 0))
    vec = pl.BlockSpec((1, LANES), lambda i, t, c: (0, 0))
    pe_blk = (MLA_Q_RANK + MLA_KV_RANK) // LANES
    out = jax.ShapeDtypeStruct((b, s, wtot), BF16)
    return pl.pallas_call(
        _mla_prep_kernel,
        grid=(b, s // ts, wtot // tc),
        in_specs=[pl.BlockSpec((1, ts, MLA_Q_RANK), lambda i, t, c: (i, t, 0)),
                  pl.BlockSpec((1, ts, MLA_KV_RANK), lambda i, t, c: (i, t, MLA_Q_RANK // MLA_KV_RANK)),
                  pl.BlockSpec((1, ts, LANES), lambda i, t, c: (i, t, pe_blk)), tab, tab,
                  pl.BlockSpec((SEG_W, SEG_W), lambda i, t, c: (0, 0)), vec, vec, vec, vec,
                  pl.BlockSpec((1, MLA_Q_RANK), lambda i, t, c: (0, 0)),
                  pl.BlockSpec((1, MLA_KV_RANK), lambda i, t, c: (0, 0)),
                  pl.BlockSpec((MLA_Q_RANK, tc), lambda i, t, c: (0, c)),
                  pl.BlockSpec((MLA_KV_RANK, tc), lambda i, t, c: (0, c))],
        out_specs=[blk, blk, pl.BlockSpec((1, ts // ATTN_TILE, tc // LANES * MLA_V, ATTN_TILE),
                                          lambda i, t, c: (i, t, c, 0))],
        out_shape=[out, out, jax.ShapeDtypeStruct((b, s // ATTN_TILE, wtot // LANES * MLA_V, ATTN_TILE), BF16)],
        compiler_params=_params("parallel", "parallel", "arbitrary"),
        name="mla_prep",
    )(u, u, u, cos, sin, seg, gq, gkn, gkp, inv_n.reshape(1, LANES).astype(F32),
      gcq.reshape(1, -1).astype(F32), gckv.reshape(1, -1).astype(F32), w_uq, w_ukv)


def _diff_prep_kernel(q_ref, k_ref, v_ref, cos_ref, sin_ref, e_ref, gq_ref, gk_ref, qo_ref, ko_ref, vo_ref):
    cos, sin = cos_ref[0], sin_ref[0]
    _store_transposed_tiles(vo_ref, v_ref[0])

    def prep(x, g, scale):
        w = x.shape[1]
        xg = x * _tile_lanes(g, w)
        xr = xg * _tile_lanes(cos, w) + _swap_halves(xg, DIFF_QK // 2) * _tile_lanes(sin, w)
        inv = lax.rsqrt(_seg_sum(x * x, e_ref[...]) * (1.0 / DIFF_QK) + EPS)
        return xr * inv * scale

    qo_ref[0] = prep(q_ref[0], gq_ref[...], DIFF_QK ** -0.5 * LOG2E).astype(qo_ref.dtype)
    ko_ref[0] = prep(k_ref[0], gk_ref[...], 1.0).astype(ko_ref.dtype)


def diff_prep(u, cos, sin, gq, gk, *, ts=512, tc=512):
    b, s, _ = u.shape
    ts = min(ts, s)
    wtot = 2 * DIFF_HEADS * DIFF_QK
    q_base = (CD_IN_PAD - 3 * wtot) // tc
    seg = (jnp.arange(SEG_W)[:, None] // DIFF_QK == jnp.arange(SEG_W)[None, :] // DIFF_QK).astype(BF16)
    blk = lambda off: pl.BlockSpec((1, ts, tc), lambda i, t, c: (i, t, off + c))
    tab = pl.BlockSpec((1, ts, LANES), lambda i, t, c: (i, t, 0))
    vec = pl.BlockSpec((1, LANES), lambda i, t, c: (0, 0))
    out = jax.ShapeDtypeStruct((b, s, wtot), BF16)
    return pl.pallas_call(
        _diff_prep_kernel,
        grid=(b, s // ts, wtot // tc),
        in_specs=[blk(q_base), blk(q_base + wtot // tc), blk(q_base + 2 * wtot // tc), tab, tab,
                  pl.BlockSpec((SEG_W, SEG_W), lambda i, t, c: (0, 0)), vec, vec],
        out_specs=[blk(0), blk(0),
                   pl.BlockSpec((1, ts // ATTN_TILE, tc, ATTN_TILE), lambda i, t, c: (i, t, c, 0))],
        out_shape=[out, out, jax.ShapeDtypeStruct((b, s // ATTN_TILE, wtot, ATTN_TILE), BF16)],
        compiler_params=_params("parallel", "parallel", "arbitrary"),
        name="diff_prep",
    )(u, u, u, cos, sin, seg, gq, gk)


def _causal_attn_kernel(lam_ref, q_ref, k_ref, vt_ref, g_ref, o_ref, *, n_sm, tq, ow, out_scale):
    qi = pl.program_id(2)
    q = q_ref[0]
    hps = q.shape[1] // LANES
    slab = lambda x, h: x[:, h * LANES:(h + 1) * LANES]
    lane = lax.broadcasted_iota(jnp.int32, (tq, LANES), 1)
    qs, src = [], []
    for h in range(hps):
        qh = slab(q, h)
        if n_sm == 2:
            zero = jnp.zeros_like(qh)
            qs += [jnp.where(lane < DIFF_QK, qh, zero), jnp.where(lane < DIFF_QK, zero, qh)]
            src += [h, h]
        else:
            qs.append(qh)
            src.append(h)
    nch = len(qs)
    key_i = lax.broadcasted_iota(jnp.int32, (tq, tq), 0)
    qry_i = lax.broadcasted_iota(jnp.int32, (tq, tq), 1)
    ones = jnp.ones((8, tq), BF16)

    def step(j, carry, diagonal):
        kj = k_ref[0, pl.ds(pl.multiple_of(j * tq, tq), tq), :]
        scores = lambda i: _dot_t(slab(kj, src[i]), qs[i])
        new = []
        ahead = [scores(i) for i in range(min(ATTN_LOOKAHEAD, nch))]
        for i in range(nch):
            s = ahead.pop(0)
            if i + ATTN_LOOKAHEAD < nch:
                ahead.append(scores(i + ATTN_LOOKAHEAD))
            if diagonal:
                s = jnp.where(key_i <= qry_i, s, NEG_INF)
            m, l, acc = carry[3 * i:3 * i + 3]
            m_new = jnp.maximum(m, jnp.max(s, axis=0, keepdims=True))
            alpha = jnp.exp2(m - m_new)
            p = jnp.exp2(s - m_new).astype(BF16)
            new += [m_new, alpha * l + _dot(ones, p)[0:1],
                    alpha * acc + _dot(vt_ref[0, j, src[i] * ow:(src[i] + 1) * ow, :], p)]
        return tuple(new)

    init = (jnp.full((1, tq), NEG_INF, F32), jnp.zeros((1, tq), F32),
            jnp.zeros((ow, tq), F32)) * nch
    carry = lax.fori_loop(0, qi, lambda j, cr: step(j, cr, False), init)
    carry = step(qi, carry, True)
    outs = []
    for h in range(hps):
        c0 = 3 * n_sm * h
        o = carry[c0 + 2] / carry[c0 + 1]
        if n_sm == 2:
            o = o - lam_ref[0] * (carry[c0 + 5] / carry[c0 + 4])
            o = o * lax.rsqrt(jnp.mean(o * o, axis=0, keepdims=True) + EPS) * g_ref[...] * out_scale
        outs.append(o.T.astype(o_ref.dtype))
    o_ref[0] = jnp.concatenate(outs, axis=-1)


def causal_attention(q, k, vt, *, n_sm, lam=None, g=None, out_scale=1.0, hps=4, name="causal_attention"):
    b, s, wtot = q.shape
    tq = vt.shape[3]
    width = hps * LANES
    groups = wtot // width
    ow = vt.shape[2] // (wtot // LANES)
    lam = jnp.zeros((1,), F32) if lam is None else lam.reshape(1).astype(F32)
    g = jnp.ones((ow, 1), F32) if g is None else g.reshape(ow, 1).astype(F32)
    seq = pl.BlockSpec((1, s, width), lambda i, h, t: (i, 0, h))
    tile = pl.BlockSpec((1, tq, width), lambda i, h, t: (i, t, h))
    return pl.pallas_call(
        functools.partial(_causal_attn_kernel, n_sm=n_sm, tq=tq, ow=ow, out_scale=out_scale),
        grid=(b, groups, s // tq),
        in_specs=[pl.BlockSpec(memory_space=pltpu.SMEM), tile, seq,
                  pl.BlockSpec((1, s // tq, hps * ow, tq), lambda i, h, t: (i, 0, h, 0)),
                  pl.BlockSpec((ow, 1), lambda i, h, t: (0, 0))],
        out_specs=pl.BlockSpec((1, tq, hps * ow), lambda i, h, t: (i, t, h)),
        out_shape=jax.ShapeDtypeStruct((b, s, groups * hps * ow), BF16),
        compiler_params=_params("parallel", "parallel", "arbitrary"),
        name=name,
    )(lam, q, k, vt, g)


def _memx_kernel(x_ref, ya_ref, yb_ref, wa_ref, wb_ref, g_ref, wq_ref, kv_ref, gq_ref, gk_ref, wo_ref, o_ref):
    x = x_ref[0] + _dot(ya_ref[0], wa_ref[...]) + _dot(yb_ref[0], wb_ref[...])
    q = _dot(_rms(x, g_ref[...]).astype(BF16), wq_ref[...])
    kv = kv_ref[0]
    outs = []
    for h in range(MEM_HEADS):
        sl = slice(h * MEM_HEAD_DIM, (h + 1) * MEM_HEAD_DIM)
        qh = (_rms(q[:, sl], gq_ref[...]) * MEM_HEAD_DIM ** -0.5).astype(BF16)
        kh = _rms(kv[:, sl], gk_ref[...]).astype(BF16)
        vh = kv[:, MEM_W + h * MEM_HEAD_DIM:MEM_W + (h + 1) * MEM_HEAD_DIM].astype(BF16)
        s = _dot_t(qh, kh)
        p = jnp.exp(s - jnp.max(s, axis=-1, keepdims=True))
        outs.append(_dot(p.astype(BF16), vh) / jnp.sum(p, axis=-1, keepdims=True))
    o_ref[0] = x + _dot(jnp.concatenate(outs, axis=-1).astype(BF16), wo_ref[...])


def mix_out_mem_attention(x, ya, yb, wa, wb, mem_kv, g, wq, gq, gk, wo, *, tm=512):
    b, s, d = x.shape
    m = mem_kv.shape[1]
    tm = min(tm, s)
    const = lambda shape: pl.BlockSpec(shape, lambda i, t: (0,) * len(shape), pipeline_mode=pl.Buffered(1))
    tile = lambda w: pl.BlockSpec((1, tm, w), lambda i, t: (i, t, 0))
    return pl.pallas_call(
        _memx_kernel,
        grid=(b, s // tm),
        in_specs=[tile(d), tile(ya.shape[2]), tile(yb.shape[2]), const(wa.shape), const(wb.shape),
                  const((1, d)), const((d, MEM_W)),
                  pl.BlockSpec((1, m, 2 * MEM_W), lambda i, t: (i, 0, 0)),
                  const((1, MEM_HEAD_DIM)), const((1, MEM_HEAD_DIM)), const((MEM_W, d))],
        out_specs=tile(d),
        out_shape=jax.ShapeDtypeStruct((b, s, d), F32),
        compiler_params=_params("parallel", "arbitrary"),
        name="mix_out_mem_attention",
    )(x, ya, yb, wa, wb, g.reshape(1, d).astype(F32), wq, mem_kv, gq.reshape(1, -1).astype(F32),
      gk.reshape(1, -1).astype(F32), wo)


def _rope_tables(positions, dim, lead_ones, tail):
    inv = 1.0 / (ROPE_THETA ** (jnp.arange(0, dim, 2, dtype=F32) / dim))
    ang = positions.astype(F32)[..., None] * inv
    c, s = jnp.cos(ang), jnp.sin(ang)
    shape = positions.shape
    cos = jnp.concatenate([jnp.ones(shape + (lead_ones,), F32), c, c, jnp.ones(shape + (tail,), F32)], axis=-1)
    sin = jnp.concatenate([jnp.zeros(shape + (lead_ones,), F32), -s, s, jnp.zeros(shape + (tail,), F32)], axis=-1)
    reps = LANES // cos.shape[-1]
    return jnp.tile(cos, (1, 1, reps)), jnp.tile(sin, (1, 1, reps))


def _pad_cols(w, cols):
    return jnp.pad(w, ((0, 0), (0, cols - w.shape[1])))


def _ab_in_layout(w):
    c = (SWA_HEADS + 2 * SWA_KV_HEADS) * HEAD_DIM + 3 * RWKV_DIM
    return jnp.concatenate([w[:, :c], _pad_cols(w[:, c:c + DECAY_LORA], LANES),
                            _pad_cols(w[:, c + DECAY_LORA:c + DECAY_LORA + AAA_LORA], LANES),
                            _pad_cols(w[:, c + DECAY_LORA + AAA_LORA:], 2 * LANES)], axis=1)


def _cd_in_layout(w):
    c1 = MLA_Q_RANK + MLA_KV_RANK
    z = lambda n: jnp.zeros((w.shape[0], n), w.dtype)
    return jnp.concatenate([w[:, :c1], z(MLA_NOPE), w[:, c1:c1 + MLA_ROPE], z(LANES - MLA_NOPE - MLA_ROPE),
                            z(LANES), w[:, c1 + MLA_ROPE:]], axis=1)


def _head_slabs(w, per_head):
    k = w.shape[0]
    return jnp.pad(w.reshape(k, -1, per_head), ((0, 0), (0, 0), (0, LANES - per_head))).reshape(k, -1)


def _slab_vec(*parts):
    v = jnp.concatenate([p.astype(F32) for p in parts])
    return jnp.pad(v, (0, LANES - v.shape[0])).reshape(1, LANES)


def kernel(x, mem, positions, ffn1_norm, ffn1_w_gate, ffn1_w_up, ffn1_w_down, mix_norm, ab_w_in, ab_w_out, swa_q_norm, swa_k_norm, swa_sinks, rwkv_mu, rwkv_w0, rwkv_w2, rwkv_a0, rwkv_a2, rwkv_g2, rwkv_k_k, rwkv_k_a, rwkv_r_k, rwkv_gn_g, rwkv_gn_b, cd_w_in, cd_w_out, mla_cq_norm, mla_ckv_norm, mla_w_uq, mla_w_ukv, mla_q_nope_norm, mla_k_nope_norm, mla_q_rope_norm, mla_k_rope_norm, diff_q_norm, diff_k_norm, diff_lq1, diff_lk1, diff_lq2, diff_lk2, diff_subln, memx_norm, memx_w_q, memx_q_norm, memx_w_o, mem_norm, mem_w_kv, mem_k_norm, ffn2_norm, ffn2_w_gate, ffn2_w_up, ffn2_w_down):
    b, s, d = x.shape
    m = mem.shape[1]
    t = b * s
    depth = ffn1_norm.shape[0]
    bf = lambda w: w.astype(BF16)
    cos64, sin64 = _rope_tables(positions, HEAD_DIM, 0, 0)
    cos32, sin32 = _rope_tables(positions, MLA_ROPE, MLA_NOPE, LANES - MLA_NOPE - MLA_ROPE)

    mem_kv = norm_matmul(mem.reshape(b * m, d), mem_norm, bf(mem_w_kv), name="mem_kv").reshape(b, m, 2 * MEM_W)

    x = x.reshape(t, d)
    for layer in range(depth):
        j = layer // 2
        x = ffn(x, ffn1_norm[layer], ffn1_w_gate, ffn1_w_up, ffn1_w_down, layer, name="ffn1")
        if layer % 2 == 0:
            u = norm_matmul(x, mix_norm[layer], _ab_in_layout(bf(ab_w_in[j])), name="ab_in")
            u = u.reshape(b, s, AB_IN_PAD)
            y_a = swa_attention(u, cos64, sin64, _slab_vec(swa_q_norm[j], swa_q_norm[j]),
                                _slab_vec(swa_k_norm[j], swa_k_norm[j]), swa_sinks[j])
            y_b = rwkv_mix(u, rwkv_mu[j], rwkv_w0[j], rwkv_w2[j], rwkv_a0[j], rwkv_a2[j], rwkv_g2[j],
                           rwkv_k_k[j], rwkv_k_a[j], rwkv_r_k[j], rwkv_gn_g[j], rwkv_gn_b[j])
            mixed, w_out, split = (y_a, y_b), bf(ab_w_out[j]), SWA_HEADS * HEAD_DIM
        else:
            u = norm_matmul(x, mix_norm[layer], _cd_in_layout(bf(cd_w_in[j])), name="cd_in")
            u = u.reshape(b, s, CD_IN_PAD)
            zero64 = jnp.zeros((MLA_NOPE,), F32)
            q_c, k_c, vt_c = mla_prep(u, cos32, sin32,
                                      _slab_vec(mla_q_nope_norm[j], mla_q_rope_norm[j]),
                                      _slab_vec(mla_k_nope_norm[j]),
                                      _slab_vec(zero64, mla_k_rope_norm[j]),
                                      mla_cq_norm[j], mla_ckv_norm[j],
                                      bf(_head_slabs(mla_w_uq[j], MLA_NOPE + MLA_ROPE)), bf(mla_w_ukv[j]))
            y_c = causal_attention(q_c, k_c, vt_c, n_sm=1, hps=8, name="mla_attention")
            q_d, k_d, vt_d = diff_prep(u, cos64, sin64, _slab_vec(diff_q_norm[j], diff_q_norm[j]),
                                       _slab_vec(diff_k_norm[j], diff_k_norm[j]))
            lambda_init = 0.8 - 0.6 * math.exp(-0.3 * layer)
            lam = (jnp.exp(jnp.sum(diff_lq1[j].astype(F32) * diff_lk1[j].astype(F32)))
                   - jnp.exp(jnp.sum(diff_lq2[j].astype(F32) * diff_lk2[j].astype(F32))) + lambda_init)
            y_d = causal_attention(q_d, k_d, vt_d, n_sm=2, hps=4, lam=lam, g=diff_subln[j],
                                   out_scale=1.0 - lambda_init, name="diff_attention")
            mixed, w_out, split = (y_c, y_d), bf(cd_w_out[j]), MLA_HEADS * MLA_V
        x = mix_out_mem_attention(x.reshape(b, s, d), mixed[0], mixed[1], w_out[:split], w_out[split:], mem_kv,
                                  memx_norm[layer], bf(memx_w_q[layer]), memx_q_norm[layer], mem_k_norm,
                                  bf(memx_w_o[layer])).reshape(t, d)
        x = ffn(x, ffn2_norm[layer], ffn2_w_gate, ffn2_w_up, ffn2_w_down, layer, name="ffn2")
    return x.reshape(b, s, d)
```

```python
import functools
import math

import jax
import jax.numpy as jnp
from jax import lax
from jax.experimental import pallas as pl
from jax.experimental.pallas import tpu as pltpu

F32 = jnp.float32
BF16 = jnp.bfloat16

EPS = 1e-6
ROPE_THETA = 10000.0
NEG_INF = -1e30
LOG2E = math.log2(math.e)
ATTN_LOOKAHEAD = 8
ATTN_TILE = 256
SEG_W = 256
LANES = 128

D_MODEL = 2048
D_FF = 5632
HEAD_DIM = 64
SWA_HEADS = 16
SWA_KV_HEADS = 4
SWA_BLOCK = 128
RWKV_DIM = 1024
RWKV_CHUNK = 64
RWKV_GN_EPS = 64e-5
DECAY_LORA, AAA_LORA, GATE_LORA = 64, 64, 160
MLA_HEADS, MLA_Q_RANK, MLA_KV_RANK, MLA_NOPE, MLA_ROPE, MLA_V = 16, 512, 256, 64, 32, 64
DIFF_HEADS, DIFF_QK, DIFF_V = 8, 64, 128
MEM_HEADS, MEM_HEAD_DIM = 4, 128
MEM_W = MEM_HEADS * MEM_HEAD_DIM
AB_IN_PAD = 5120
CD_IN_PAD = 4096

VMEM_LIMIT = 48 * 1024 * 1024
FFN_VMEM_LIMIT = 60 * 1024 * 1024


def _params(*sem):
    return pltpu.CompilerParams(dimension_semantics=sem, vmem_limit_bytes=VMEM_LIMIT)


def _dot(a, b):
    return jnp.dot(a, b, preferred_element_type=F32)


def _dot_t(a, b):
    return lax.dot_general(a, b, (((1,), (1,)), ((), ())), preferred_element_type=F32)


def _dot_0(a, b):
    return lax.dot_general(a, b, (((0,), (0,)), ((), ())), preferred_element_type=F32)


def _rms(x, g):
    return x * lax.rsqrt(jnp.mean(x * x, axis=-1, keepdims=True) + EPS) * g


def _seg_sum(x, e):
    xb = x.astype(BF16)
    w = e.shape[0]
    parts = [_dot(xb[:, i:i + w], e) for i in range(0, x.shape[1], w)]
    return parts[0] if len(parts) == 1 else jnp.concatenate(parts, axis=-1)


def _tile_lanes(v, width):
    return v if v.shape[-1] == width else jnp.tile(v, (1, width // v.shape[-1]))


def _store_transposed_tiles(vt_ref, v):
    for r in range(vt_ref.shape[1]):
        vt_ref[0, r] = v[r * ATTN_TILE:(r + 1) * ATTN_TILE, :].T.astype(vt_ref.dtype)


def _swap_halves(x, half):
    w = x.shape[-1]
    lane = lax.broadcasted_iota(jnp.int32, x.shape, x.ndim - 1)
    low = (lane & (2 * half - 1)) < half
    return jnp.where(low, pltpu.roll(x, w - half, x.ndim - 1), pltpu.roll(x, half, x.ndim - 1))


def _norm_matmul_kernel(x_ref, g_ref, w_ref, o_ref, xn_ref):
    @pl.when(pl.program_id(1) == 0)
    def _():
        xn_ref[...] = _rms(x_ref[...].astype(F32), g_ref[...]).astype(BF16)

    o_ref[...] = _dot(xn_ref[...], w_ref[...]).astype(o_ref.dtype)


def norm_matmul(x, g, w, *, k_blk=0, out_dtype=F32, tm=1024, tn=512, name="norm_matmul"):
    t = x.shape[0]
    k, n = w.shape
    tm, tn = min(tm, t), min(tn, n)
    assert t % tm == 0 and n % tn == 0
    return pl.pallas_call(
        _norm_matmul_kernel,
        grid=(t // tm, n // tn),
        in_specs=[pl.BlockSpec((tm, k), lambda i, j: (i, k_blk)),
                  pl.BlockSpec((1, k), lambda i, j: (0, 0)),
                  pl.BlockSpec((k, tn), lambda i, j: (0, j))],
        out_specs=pl.BlockSpec((tm, tn), lambda i, j: (i, j)),
        out_shape=jax.ShapeDtypeStruct((t, n), out_dtype),
        scratch_shapes=[pltpu.VMEM((tm, k), BF16)],
        compiler_params=_params("parallel", "arbitrary"),
        name=name,
    )(x, g.reshape(1, k).astype(F32), w)


def _ffn_kernel(x_ref, g_ref, wg_ref, wu_ref, wd_ref, o_ref, xn_ref):
    @pl.when(pl.program_id(1) == 0)
    def _():
        x = x_ref[...]
        xn_ref[...] = _rms(x, g_ref[...]).astype(BF16)
        o_ref[...] = x

    xn = xn_ref[...]
    a = _dot(xn, wg_ref[...].astype(BF16))
    b = _dot(xn, wu_ref[...].astype(BF16))
    h = (a * (0.5 / (1.0 + jnp.exp(-a))) * b).astype(BF16)
    o_ref[...] += _dot(h, wd_ref[...].astype(BF16))


def ffn(x, g, wg, wu, wd, layer, *, tm=1024, tf=256, name="ffn"):
    t, d = x.shape
    ff = wg.shape[2]
    tm = min(tm, t)
    assert t % tm == 0 and ff % tf == 0
    nf = ff // tf
    return pl.pallas_call(
        _ffn_kernel,
        grid=(t // tm, nf),
        in_specs=[pl.BlockSpec((tm, d), lambda i, f: (i, 0)),
                  pl.BlockSpec((1, d), lambda i, f: (0, 0)),
                  pl.BlockSpec((None, d, tf), lambda i, f: (layer, 0, f)),
                  pl.BlockSpec((None, d, tf), lambda i, f: (layer, 0, f)),
                  pl.BlockSpec((None, tf, d), lambda i, f: (layer, f, 0))],
        out_specs=pl.BlockSpec((tm, d), lambda i, f: (i, 0)),
        out_shape=jax.ShapeDtypeStruct((t, d), F32),
        scratch_shapes=[pltpu.VMEM((tm, d), BF16)],
        compiler_params=pltpu.CompilerParams(dimension_semantics=("parallel", "arbitrary"),
                                             vmem_limit_bytes=FFN_VMEM_LIMIT),
        name=name,
    )(x, g.reshape(1, d).astype(F32), wg, wu, wd)


def _swa_kernel(sink_ref, q_ref, kc_ref, kp_ref, vc_ref, vp_ref, cc_ref, sc_ref, cp_ref, sp_ref,
                gq_ref, gk_ref, e_ref, o_ref):
    n = pl.program_id(1)
    blk = SWA_BLOCK
    group = SWA_HEADS // SWA_KV_HEADS
    q = q_ref[0]
    k = jnp.concatenate([kp_ref[0], kc_ref[0]], axis=0)
    v = jnp.concatenate([vp_ref[0], vc_ref[0]], axis=0)
    cos_q, sin_q = cc_ref[0], sc_ref[0]
    cos_k = jnp.concatenate([cp_ref[0], cos_q], axis=0)
    sin_k = jnp.concatenate([sp_ref[0], sin_q], axis=0)
    def prep(x, g, cos, sin, scale):
        w = x.shape[1]
        xg = x * _tile_lanes(g, w)
        xr = xg * _tile_lanes(cos, w) + _swap_halves(xg, HEAD_DIM // 2) * _tile_lanes(sin, w)
        return xr * lax.rsqrt(_seg_sum(x * x, e_ref[...]) * (1.0 / HEAD_DIM) + EPS) * scale

    qr = prep(q, gq_ref[...], cos_q, sin_q, HEAD_DIM ** -0.5 * LOG2E).astype(BF16)
    kr = prep(k, gk_ref[...], cos_k, sin_k, 1.0)
    key_i = lax.broadcasted_iota(jnp.int32, (2 * blk, blk), 0)
    qry_i = lax.broadcasted_iota(jnp.int32, (2 * blk, blk), 1)
    rel = qry_i + blk - key_i
    valid = (rel >= 0) & (rel < blk) & ((n > 0) | (key_i >= blk))
    low = lax.broadcasted_iota(jnp.int32, (blk, LANES), 1) < HEAD_DIM
    zero = jnp.zeros((blk, LANES), BF16)
    k_dup, v_t = [], []
    for g in range(SWA_KV_HEADS):
        kg = kr[:, g * HEAD_DIM:(g + 1) * HEAD_DIM]
        k_dup.append(jnp.concatenate([kg, kg], axis=-1).astype(BF16))
        v_t.append(v[:, g * HEAD_DIM:(g + 1) * HEAD_DIM].T.astype(BF16))

    def scores(h):
        slab = qr[:, (h // 2) * LANES:(h // 2 + 1) * LANES]
        qh = jnp.where(low, slab, zero) if h % 2 == 0 else jnp.where(low, zero, slab)
        return _dot_t(k_dup[h // group], qh)

    ahead = [scores(h) for h in range(ATTN_LOOKAHEAD)]
    ones = jnp.ones((8, 2 * blk), BF16)
    outs = []
    for h in range(SWA_HEADS):
        s = jnp.where(valid, ahead.pop(0), NEG_INF)
        if h + ATTN_LOOKAHEAD < SWA_HEADS:
            ahead.append(scores(h + ATTN_LOOKAHEAD))
        sink = sink_ref[h] * LOG2E
        m = jnp.maximum(jnp.max(s, axis=0, keepdims=True), sink)
        p = jnp.exp2(s - m).astype(BF16)
        den = _dot(ones, p)[0:1] + jnp.exp2(sink - m)
        outs.append(_dot(v_t[h // group], p) / den)
    slabs = [jnp.concatenate(outs[i:i + 2], axis=0).T for i in range(0, SWA_HEADS, 2)]
    o_ref[0] = jnp.concatenate(slabs, axis=-1).astype(o_ref.dtype)


def swa_attention(u, cos, sin, gq, gk, sinks):
    b, s, _ = u.shape
    nb = s // SWA_BLOCK
    qw, kw = SWA_HEADS * HEAD_DIM, SWA_KV_HEADS * HEAD_DIM
    cur = lambda c: (lambda i, n: (i, n, c))
    prev = lambda c: (lambda i, n: (i, jnp.maximum(n - 1, 0), c))
    tab = pl.BlockSpec((1, SWA_BLOCK, LANES), cur(0))
    tab_prev = pl.BlockSpec((1, SWA_BLOCK, LANES), prev(0))
    gain = pl.BlockSpec((1, LANES), lambda i, n: (0, 0))
    seg = (jnp.arange(kw)[:, None] // HEAD_DIM == jnp.arange(kw)[None, :] // HEAD_DIM).astype(BF16)
    return pl.pallas_call(
        _swa_kernel,
        grid=(b, nb),
        in_specs=[pl.BlockSpec(memory_space=pltpu.SMEM),
                  pl.BlockSpec((1, SWA_BLOCK, qw), cur(0)),
                  pl.BlockSpec((1, SWA_BLOCK, kw), cur(qw // kw)),
                  pl.BlockSpec((1, SWA_BLOCK, kw), prev(qw // kw)),
                  pl.BlockSpec((1, SWA_BLOCK, kw), cur(qw // kw + 1)),
                  pl.BlockSpec((1, SWA_BLOCK, kw), prev(qw // kw + 1)),
                  tab, tab, tab_prev, tab_prev, gain, gain, pl.BlockSpec((kw, kw), lambda i, n: (0, 0))],
        out_specs=pl.BlockSpec((1, SWA_BLOCK, qw), cur(0)),
        out_shape=jax.ShapeDtypeStruct((b, s, qw), BF16),
        compiler_params=_params("parallel", "arbitrary"),
        name="swa_attention",
    )(sinks.astype(F32), u, u, u, u, u, cos, sin, cos, sin, gq, gk, seg)


def _mm(a, b, dims, passes):
    dn = (dims, ((), ()))
    dg = lambda x, y: lax.dot_general(x, y, dn, preferred_element_type=F32)
    ah = a.astype(BF16)
    bh = b.astype(BF16)
    if passes == 1:
        return dg(ah, bh)
    al = (a - ah.astype(F32)).astype(BF16)
    bl = (b - bh.astype(F32)).astype(BF16)
    return dg(ah, bh) + dg(ah, bl) + dg(al, bh)


_NN = ((1,), (0,))
_NT = ((1,), (1,))
_TN = ((0,), (0,))
P_SC, P_INV, P_PQ, P_OUT, P_ST = 1, 1, 1, 1, 1
RWKV_UNROLL = 8


RWKV_PREP_ROWS = 256


def _rwkv_kernel(ur_ref, uk_ref, uv_ref, ul_ref, mur_ref, muk_ref, muv_ref, mul_ref, w0_ref, w2_ref, a0_ref,
                 a2_ref, g2_ref, kkw_ref, ka_ref, gng_ref, gnb_ref, rk_ref,
                 o_ref, r_ref, k_ref, v_ref, kk_ref, b_ref, lw_ref, g_ref,
                 st_ref, y1_ref, y0_ref, n_ref, z_ref, dec_ref, *, nchunk):
    c = RWKV_CHUNK
    lane_c = lax.broadcasted_iota(jnp.int32, (c, LANES), 1)
    head0 = lane_c < HEAD_DIM
    ri = lax.broadcasted_iota(jnp.int32, (2 * c, 2 * c), 0)
    ci = lax.broadcasted_iota(jnp.int32, (2 * c, 2 * c), 1)
    eye = jnp.where(ri == ci, 1.0, 0.0)
    tril_c = jnp.where(lax.broadcasted_iota(jnp.int32, (c, c), 0) >= lax.broadcasted_iota(jnp.int32, (c, c), 1),
                       1.0, 0.0).astype(BF16)
    stack = lambda x: jnp.concatenate([jnp.where(head0, x, 0.0), jnp.where(head0, 0.0, x)], axis=0)

    def seg_mean(x):
        first = lax.broadcasted_iota(jnp.int32, x.shape, 1) < HEAD_DIM
        m0 = jnp.sum(jnp.where(first, x, 0.0), axis=-1, keepdims=True)
        m1 = jnp.sum(jnp.where(first, 0.0, x), axis=-1, keepdims=True)
        return jnp.where(first, m0, m1) * (1.0 / HEAD_DIM)

    def prep(io, first_group):
        group_rows = RWKV_UNROLL * c
        for tix in range(group_rows // RWKV_PREP_ROWS):
            start = pl.multiple_of(io * group_rows + tix * RWKV_PREP_ROWS, RWKV_PREP_ROWS)
            rows = pl.ds(start, RWKV_PREP_ROWS)
            at_start = first_group and tix == 0

            def shifted(ref, mu_ref):
                x = ref[0, rows, :]
                if at_start:
                    last = jnp.zeros((1, x.shape[1]), F32)
                else:
                    last = ref[0, pl.ds(pl.multiple_of(start - 8, 8), 8), :][7:8, :]
                row = lax.broadcasted_iota(jnp.int32, x.shape, 0)
                prev = jnp.where(row == 0, last, pltpu.roll(x, 1, 0))
                return x + (prev - x) * mu_ref[...]

            r = shifted(ur_ref, mur_ref)
            k = shifted(uk_ref, muk_ref)
            v = shifted(uv_ref, muv_ref)
            lo = shifted(ul_ref, mul_ref)
            yield
            w_lo, a_lo, g_lo = lo[:, 0:LANES], lo[:, LANES:2 * LANES], lo[:, 2 * LANES:4 * LANES]
            z = -(w0_ref[...] + _mm(jnp.tanh(w_lo), w2_ref[...], _NN, 3))
            w = -(jnp.maximum(z, 0.0) + jnp.log(1.0 + jnp.exp(-jnp.abs(z)))) - 0.5
            a = 1.0 / (1.0 + jnp.exp(-(a0_ref[...] + _mm(a_lo, a2_ref[...], _NN, 1))))
            g = _mm(1.0 / (1.0 + jnp.exp(-g_lo)), g2_ref[...], _NN, 1)
            yield
            kk = k * kkw_ref[...]
            kk = kk / jnp.maximum(jnp.sqrt(seg_mean(kk * kk) * float(HEAD_DIM)), 1e-12)
            r_ref[rows, :] = r
            k_ref[rows, :] = k * (1.0 + (a - 1.0) * ka_ref[...])
            v_ref[rows, :] = v
            kk_ref[rows, :] = kk
            b_ref[rows, :] = kk * a
            lw_ref[rows, :] = -jnp.exp(w)
            g_ref[rows, :] = g
            yield

    def build(ics):
        each = lambda f, *cols: [f(*args) for args in zip(*cols)]
        sls = [pl.ds(pl.multiple_of(ic * c, c), c) for ic in ics]
        load = lambda ref: [ref[sl, :] for sl in sls]
        r, k, v, kk, b, lw = (load(ref) for ref in (r_ref, k_ref, v_ref, kk_ref, b_ref, lw_ref))

        def running_sum(x):
            l1 = x.astype(BF16)
            rest = x - l1.astype(F32)
            l2 = rest.astype(BF16)
            l3 = (rest - l2.astype(F32)).astype(BF16)
            return _dot(tril_c, l1) + _dot(tril_c, l2) + _dot(tril_c, l3)

        cum = each(running_sum, lw)
        yield
        cum_end = [x[c - 1:c, :] for x in cum]
        e_neg = each(lambda x: jnp.exp(-x), cum)
        e_end = each(lambda x, xe: jnp.exp(xe - x), cum, cum_end)
        a_s = each(lambda kk_, x, l: stack(-kk_ * jnp.exp(x - l)), kk, cum, lw)
        r_s = each(lambda r_, x: stack(r_ * jnp.exp(x)), r, cum)
        b_s = each(lambda b_, e: stack(b_ * e), b, e_neg)
        k_s = each(lambda k_, e: stack(k_ * e), k, e_neg)
        bh_s = each(lambda b_, e: stack(b_ * e), b, e_end)
        kh_s = each(lambda k_, e: stack(k_ * e), k, e_end)
        v_s = each(stack, v)
        n2 = 2 * c
        sc = each(lambda a_, r_, b_, k_: _mm(jnp.concatenate([a_, r_], axis=0),
                                             jnp.concatenate([b_, k_], axis=0), _NT, P_SC), a_s, r_s, b_s, k_s)
        low = [jnp.where(ri > ci, x[:n2, :n2], 0.0) for x in sc]
        a_ak = [jnp.where(ri > ci, x[:n2, n2:], 0.0) for x in sc]
        a_rb = [jnp.where(ri >= ci, x[n2:, :n2], 0.0) for x in sc]
        a_rk = [jnp.where(ri >= ci, x[n2:, n2:], 0.0) for x in sc]
        yield
        inv = [eye + x for x in low]
        pw = each(lambda x: _mm(x, x, _NN, P_INV), low)
        yield
        levels = 5
        for lvl in range(levels - 1):
            both = each(lambda t, x: _mm(jnp.concatenate([t, x], axis=0), x, _NN, P_INV), inv, pw)
            inv = each(lambda t, r: t + r[:n2], inv, both)
            pw = [r[n2:] for r in both]
            yield
        inv = each(lambda t, x: t + _mm(t, x, _NN, P_INV), inv, pw)
        yield
        akv = each(lambda x, y: _mm(x, y, _NN, P_PQ), a_ak, v_s)
        yield
        pq = each(lambda t, x, y: _mm(t, jnp.concatenate([x, y], axis=1), _NN, P_PQ), inv, a_s, akv)
        yield
        yy = each(lambda x, y: _mm(x, y, _NN, P_OUT), a_rb, pq)
        y0b = each(lambda x, y: _mm(x, y, _NN, P_OUT), a_rk, v_s)
        nz = each(lambda x, y: _mm(x, y, _TN, P_OUT), pq, bh_s)
        zb = each(lambda x, y: _mm(x, y, _TN, P_OUT), v_s, kh_s)
        for i, ic in enumerate(ics):
            y1_ref[ic] = r_s[i] + yy[i][:, :LANES]
            y0_ref[ic] = yy[i][:, LANES:] + y0b[i]
            n_ref[ic] = nz[i][:LANES]
            z_ref[ic] = nz[i][LANES:] + zb[i]
            dec_ref[ic] = jnp.broadcast_to(jnp.exp(cum_end[i]), (8, LANES))

    def emit(ic, st):
        sl = pl.ds(pl.multiple_of(ic * c, c), c)
        r, k, v, g = r_ref[sl, :], k_ref[sl, :], v_ref[sl, :], g_ref[sl, :]
        y_st = _mm(y1_ref[ic], st, _NT, P_ST) + y0_ref[ic]
        y = y_st[0:c] + y_st[c:2 * c]
        mean = seg_mean(y)
        var = seg_mean((y - mean) * (y - mean))
        yn = (y - mean) * lax.rsqrt(var + RWKV_GN_EPS) * gng_ref[...] + gnb_ref[...]
        bonus = seg_mean(r * k * rk_ref[...]) * float(HEAD_DIM) * v
        o_ref[0, sl, :] = ((yn + bonus) * g).astype(o_ref.dtype)
        return st * dec_ref[ic][0:1, :] + _mm(st, n_ref[ic], _NN, P_ST) + z_ref[ic]

    def scan(ics):
        st = st_ref[...]
        for ic in ics:
            st = emit(ic, st)
            yield
        st_ref[...] = st

    def run(*gens):
        live = list(gens)
        while live:
            live = [gen for gen in live if next(gen, live) is not live]

    group = lambda io: [io * RWKV_UNROLL + i for i in range(RWKV_UNROLL)]
    ngroup = nchunk // RWKV_UNROLL
    st_ref[...] = jnp.zeros_like(st_ref)
    run(prep(0, True))
    if ngroup > 1:
        run(prep(1, False), build(group(0)))
    else:
        run(build(group(0)))

    @pl.loop(1, ngroup - 1)
    def _(io):
        run(prep(io + 1, False), build(group(io)), scan(group(io - 1)))

    if ngroup > 1:
        run(build(group(ngroup - 1)), scan(group(ngroup - 2)))
    run(scan(group(ngroup - 1)))


def rwkv_mix(u, mu, w0, w2, a0, a2, g2, k_k, k_a, r_k, gn_g, gn_b):
    bsz, s, _ = u.shape
    npair = RWKV_DIM // LANES
    nchunk = s // RWKV_CHUNK
    assert nchunk % RWKV_UNROLL == 0 and (RWKV_UNROLL * RWKV_CHUNK) % RWKV_PREP_ROWS == 0
    base = (SWA_HEADS + 2 * SWA_KV_HEADS) * HEAD_DIM // LANES
    lora_w = 4 * LANES
    slab = lambda off: pl.BlockSpec((1, s, LANES), lambda i, p: (i, 0, off + p))
    vec = pl.BlockSpec((1, LANES), lambda i, p: (0, p))
    cols = lambda rows: pl.BlockSpec((rows, LANES), lambda i, p: (0, p))
    row = lambda vv: vv.reshape(1, -1).astype(F32)
    pad_rows = lambda m, rows: jnp.pad(m, ((0, rows - m.shape[0]), (0, 0))).astype(F32)
    pad_cols = lambda vv, n: jnp.pad(vv, (0, n - vv.shape[0]))
    c3 = 3 * RWKV_DIM
    mu_l = jnp.concatenate([pad_cols(mu[c3:c3 + DECAY_LORA], LANES),
                            pad_cols(mu[c3 + DECAY_LORA:c3 + DECAY_LORA + AAA_LORA], LANES),
                            pad_cols(mu[c3 + DECAY_LORA + AAA_LORA:], 2 * LANES)])
    seq = pltpu.VMEM((s, LANES), F32)
    mat = pltpu.VMEM((nchunk, LANES, LANES), F32)
    return pl.pallas_call(
        functools.partial(_rwkv_kernel, nchunk=nchunk),
        grid=(bsz, npair),
        in_specs=[slab(base), slab(base + npair), slab(base + 2 * npair),
                  pl.BlockSpec((1, s, lora_w), lambda i, p: (i, 0, (base + 3 * npair) * LANES // lora_w)),
                  vec, vec, vec, pl.BlockSpec((1, lora_w), lambda i, p: (0, 0)),
                  vec, cols(LANES), vec, cols(LANES), cols(2 * LANES), vec, vec, vec, vec, vec],
        out_specs=pl.BlockSpec((1, s, LANES), lambda i, p: (i, 0, p)),
        out_shape=jax.ShapeDtypeStruct((bsz, s, RWKV_DIM), BF16),
        scratch_shapes=[seq] * 7 + [pltpu.VMEM((LANES, LANES), F32), mat, mat, mat, mat,
                                    pltpu.VMEM((nchunk, 8, LANES), F32)],
        compiler_params=_params("parallel", "arbitrary"),
        name="rwkv_mix",
    )(u, u, u, u, row(mu[:RWKV_DIM]), row(mu[RWKV_DIM:2 * RWKV_DIM]), row(mu[2 * RWKV_DIM:c3]), row(mu_l),
      row(w0), pad_rows(w2, LANES), row(a0), pad_rows(a2, LANES), pad_rows(g2, 2 * LANES),
      row(k_k), row(k_a), row(gn_g), row(gn_b), row(r_k))


def _mla_prep_kernel(cq_ref, ckv_ref, pe_ref, cos_ref, sin_ref, e_ref, gq_ref, gkn_ref, gkp_ref, invn_ref,
                     gcq_ref, gckv_ref, wuq_ref, wukv_ref, qo_ref, ko_ref, vo_ref, cqn_ref, ckvn_ref, kpe_ref):
    cos, sin = cos_ref[0], sin_ref[0]
    half = MLA_ROPE // 2

    def rope(x, g):
        w = x.shape[1]
        xg = x * _tile_lanes(g, w)
        return xg * _tile_lanes(cos, w) + _swap_halves(xg, half) * _tile_lanes(sin, w)

    @pl.when(pl.program_id(2) == 0)
    def _():
        cqn_ref[...] = _rms(cq_ref[0], gcq_ref[...]).astype(BF16)
        ckvn_ref[...] = _rms(ckv_ref[0], gckv_ref[...]).astype(BF16)
        pe = pe_ref[0]
        inv_pe = lax.rsqrt(jnp.sum(pe * pe, axis=-1, keepdims=True) * (1.0 / MLA_ROPE) + EPS)
        kpe_ref[...] = rope(pe, gkp_ref[...]) * inv_pe

    x = _dot(cqn_ref[...], wuq_ref[...])
    w = x.shape[1]
    inv_n = _tile_lanes(invn_ref[...], w)
    inv = lax.rsqrt(_seg_sum(x * x, e_ref[...]) * inv_n + EPS)
    qo_ref[0] = (rope(x, gq_ref[...]) * inv * ((MLA_NOPE + MLA_ROPE) ** -0.5 * LOG2E)).astype(qo_ref.dtype)
    kv = _dot(ckvn_ref[...], wukv_ref[...])
    inv_k = lax.rsqrt(_seg_sum(kv * kv, e_ref[...]) * inv_n + EPS)
    k_nope = kv * inv_k * _tile_lanes(gkn_ref[...], w)
    ko_ref[0] = (k_nope + _tile_lanes(kpe_ref[...], w)).astype(ko_ref.dtype)
    v = jnp.concatenate([kv[:, h * LANES + MLA_NOPE:(h + 1) * LANES] for h in range(w // LANES)], axis=-1)
    _store_transposed_tiles(vo_ref, v)


def mla_prep(u, cos, sin, gq, gkn, gkp, gcq, gckv, w_uq, w_ukv, *, ts=512, tc=512):
    b, s, _ = u.shape
    wtot = w_uq.shape[1]
    ts = min(ts, s)
    lane = jnp.arange(SEG_W)
    same = (lane[:, None] // LANES == lane[None, :] // LANES)
    pos = lane % LANES
    nope = pos < MLA_NOPE
    pe = (pos >= MLA_NOPE) & (pos < MLA_NOPE + MLA_ROPE)
    seg = (same & ((nope[:, None] & nope[None, :]) | (pe[:, None] & pe[None, :]))).astype(BF16)
    p1 = jnp.arange(LANES)
    inv_n = jnp.where(p1 < MLA_NOPE, 1.0 / MLA_NOPE, jnp.where(p1 < MLA_NOPE + MLA_ROPE, 1.0 / MLA_ROPE, 0.0))
    blk = pl.BlockSpec((1, ts, tc), lambda i, t, c: (i, t, c))
    tab = pl.BlockSpec((1, ts, LANES), lambda i, t, c: (i, t, 0))
    vec = pl.BlockSpec((1, LANES), lambda i, t, c: (0, 0))
    pe_blk = (MLA_Q_RANK + MLA_KV_RANK) // LANES
    out = jax.ShapeDtypeStruct((b, s, wtot), BF16)
    return pl.pallas_call(
        _mla_prep_kernel,
        grid=(b, s // ts, wtot // tc),
        in_specs=[pl.BlockSpec((1, ts, MLA_Q_RANK), lambda i, t, c: (i, t, 0)),
                  pl.BlockSpec((1, ts, MLA_KV_RANK), lambda i, t, c: (i, t, MLA_Q_RANK // MLA_KV_RANK)),
                  pl.BlockSpec((1, ts, LANES), lambda i, t, c: (i, t, pe_blk)), tab, tab,
                  pl.BlockSpec((SEG_W, SEG_W), lambda i, t, c: (0, 0)), vec, vec, vec, vec,
                  pl.BlockSpec((1, MLA_Q_RANK), lambda i, t, c: (0, 0)),
                  pl.BlockSpec((1, MLA_KV_RANK), lambda i, t, c: (0, 0)),
                  pl.BlockSpec((MLA_Q_RANK, tc), lambda i, t, c: (0, c)),
                  pl.BlockSpec((MLA_KV_RANK, tc), lambda i, t, c: (0, c))],
        out_specs=[blk, blk, pl.BlockSpec((1, ts // ATTN_TILE, tc // LANES * MLA_V, ATTN_TILE),
                                          lambda i, t, c: (i, t, c, 0))],
        out_shape=[out, out, jax.ShapeDtypeStruct((b, s // ATTN_TILE, wtot // LANES * MLA_V, ATTN_TILE), BF16)],
        scratch_shapes=[pltpu.VMEM((ts, MLA_Q_RANK), BF16), pltpu.VMEM((ts, MLA_KV_RANK), BF16),
                        pltpu.VMEM((ts, LANES), F32)],
        compiler_params=_params("parallel", "parallel", "arbitrary"),
        name="mla_prep",
    )(u, u, u, cos, sin, seg, gq, gkn, gkp, inv_n.reshape(1, LANES).astype(F32),
      gcq.reshape(1, -1).astype(F32), gckv.reshape(1, -1).astype(F32), w_uq, w_ukv)


def _diff_prep_kernel(q_ref, k_ref, v_ref, cos_ref, sin_ref, e_ref, gq_ref, gk_ref, qo_ref, ko_ref, vo_ref):
    cos, sin = cos_ref[0], sin_ref[0]
    _store_transposed_tiles(vo_ref, v_ref[0])

    def prep(x, g, scale):
        w = x.shape[1]
        xg = x * _tile_lanes(g, w)
        xr = xg * _tile_lanes(cos, w) + _swap_halves(xg, DIFF_QK // 2) * _tile_lanes(sin, w)
        inv = lax.rsqrt(_seg_sum(x * x, e_ref[...]) * (1.0 / DIFF_QK) + EPS)
        return xr * inv * scale

    qo_ref[0] = prep(q_ref[0], gq_ref[...], DIFF_QK ** -0.5 * LOG2E).astype(qo_ref.dtype)
    ko_ref[0] = prep(k_ref[0], gk_ref[...], 1.0).astype(ko_ref.dtype)


def diff_prep(u, cos, sin, gq, gk, *, ts=512, tc=512):
    b, s, _ = u.shape
    ts = min(ts, s)
    wtot = 2 * DIFF_HEADS * DIFF_QK
    q_base = (CD_IN_PAD - 3 * wtot) // tc
    seg = (jnp.arange(SEG_W)[:, None] // DIFF_QK == jnp.arange(SEG_W)[None, :] // DIFF_QK).astype(BF16)
    blk = lambda off: pl.BlockSpec((1, ts, tc), lambda i, t, c: (i, t, off + c))
    tab = pl.BlockSpec((1, ts, LANES), lambda i, t, c: (i, t, 0))
    vec = pl.BlockSpec((1, LANES), lambda i, t, c: (0, 0))
    out = jax.ShapeDtypeStruct((b, s, wtot), BF16)
    return pl.pallas_call(
        _diff_prep_kernel,
        grid=(b, s // ts, wtot // tc),
        in_specs=[blk(q_base), blk(q_base + wtot // tc), blk(q_base + 2 * wtot // tc), tab, tab,
                  pl.BlockSpec((SEG_W, SEG_W), lambda i, t, c: (0, 0)), vec, vec],
        out_specs=[blk(0), blk(0),
                   pl.BlockSpec((1, ts // ATTN_TILE, tc, ATTN_TILE), lambda i, t, c: (i, t, c, 0))],
        out_shape=[out, out, jax.ShapeDtypeStruct((b, s // ATTN_TILE, wtot, ATTN_TILE), BF16)],
        compiler_params=_params("parallel", "parallel", "arbitrary"),
        name="diff_prep",
    )(u, u, u, cos, sin, seg, gq, gk)


def _causal_attn_kernel(lam_ref, q_ref, k_ref, vt_ref, g_ref, o_ref, *, n_sm, tq, ow, out_scale):
    qi = pl.program_id(2)
    q = q_ref[0]
    hps = q.shape[1] // LANES
    slab = lambda x, h: x[:, h * LANES:(h + 1) * LANES]
    lane = lax.broadcasted_iota(jnp.int32, (tq, LANES), 1)
    qs, src = [], []
    for h in range(hps):
        qh = slab(q, h)
        if n_sm == 2:
            zero = jnp.zeros_like(qh)
            qs += [jnp.where(lane < DIFF_QK, qh, zero), jnp.where(lane < DIFF_QK, zero, qh)]
            src += [h, h]
        else:
            qs.append(qh)
            src.append(h)
    nch = len(qs)
    key_i = lax.broadcasted_iota(jnp.int32, (tq, tq), 0)
    qry_i = lax.broadcasted_iota(jnp.int32, (tq, tq), 1)
    ones = jnp.ones((8, tq), BF16)

    def step(j, carry, diagonal):
        kj = k_ref[0, pl.ds(pl.multiple_of(j * tq, tq), tq), :]
        scores = lambda i: _dot_t(slab(kj, src[i]), qs[i])
        new = []
        ahead = [scores(i) for i in range(min(ATTN_LOOKAHEAD, nch))]
        for i in range(nch):
            s = ahead.pop(0)
            if i + ATTN_LOOKAHEAD < nch:
                ahead.append(scores(i + ATTN_LOOKAHEAD))
            if diagonal:
                s = jnp.where(key_i <= qry_i, s, NEG_INF)
            m, l, acc = carry[3 * i:3 * i + 3]
            m_new = jnp.maximum(m, jnp.max(s, axis=0, keepdims=True))
            alpha = jnp.exp2(m - m_new)
            p = jnp.exp2(s - m_new).astype(BF16)
            new += [m_new, alpha * l + _dot(ones, p)[0:1],
                    alpha * acc + _dot(vt_ref[0, j, src[i] * ow:(src[i] + 1) * ow, :], p)]
        return tuple(new)

    init = (jnp.full((1, tq), NEG_INF, F32), jnp.zeros((1, tq), F32),
            jnp.zeros((ow, tq), F32)) * nch
    carry = lax.fori_loop(0, qi, lambda j, cr: step(j, cr, False), init)
    carry = step(qi, carry, True)
    outs = []
    for h in range(hps):
        c0 = 3 * n_sm * h
        o = carry[c0 + 2] / carry[c0 + 1]
        if n_sm == 2:
            o = o - lam_ref[0] * (carry[c0 + 5] / carry[c0 + 4])
            o = o * lax.rsqrt(jnp.mean(o * o, axis=0, keepdims=True) + EPS) * g_ref[...] * out_scale
        outs.append(o.T.astype(o_ref.dtype))
    o_ref[0] = jnp.concatenate(outs, axis=-1)


def causal_attention(q, k, vt, *, n_sm, lam=None, g=None, out_scale=1.0, hps=4, name="causal_attention"):
    b, s, wtot = q.shape
    tq = vt.shape[3]
    width = hps * LANES
    groups = wtot // width
    ow = vt.shape[2] // (wtot // LANES)
    lam = jnp.zeros((1,), F32) if lam is None else lam.reshape(1).astype(F32)
    g = jnp.ones((ow, 1), F32) if g is None else g.reshape(ow, 1).astype(F32)
    seq = pl.BlockSpec((1, s, width), lambda i, h, t: (i, 0, h))
    tile = pl.BlockSpec((1, tq, width), lambda i, h, t: (i, t, h))
    return pl.pallas_call(
        functools.partial(_causal_attn_kernel, n_sm=n_sm, tq=tq, ow=ow, out_scale=out_scale),
        grid=(b, groups, s // tq),
        in_specs=[pl.BlockSpec(memory_space=pltpu.SMEM), tile, seq,
                  pl.BlockSpec((1, s // tq, hps * ow, tq), lambda i, h, t: (i, 0, h, 0)),
                  pl.BlockSpec((ow, 1), lambda i, h, t: (0, 0))],
        out_specs=pl.BlockSpec((1, tq, hps * ow), lambda i, h, t: (i, t, h)),
        out_shape=jax.ShapeDtypeStruct((b, s, groups * hps * ow), BF16),
        compiler_params=_params("parallel", "parallel", "arbitrary"),
        name=name,
    )(lam, q, k, vt, g)


def _memx_kernel(x_ref, ya_ref, yb_ref, wa_ref, wb_ref, g_ref, wq_ref, kv_ref, gq_ref, gk_ref, wo_ref, o_ref):
    x = x_ref[0] + _dot(ya_ref[0], wa_ref[...]) + _dot(yb_ref[0], wb_ref[...])
    q = _dot(_rms(x, g_ref[...]).astype(BF16), wq_ref[...])
    kv = kv_ref[0]
    outs = []
    for h in range(MEM_HEADS):
        sl = slice(h * MEM_HEAD_DIM, (h + 1) * MEM_HEAD_DIM)
        qh = (_rms(q[:, sl], gq_ref[...]) * MEM_HEAD_DIM ** -0.5).astype(BF16)
        kh = _rms(kv[:, sl], gk_ref[...]).astype(BF16)
        vh = kv[:, MEM_W + h * MEM_HEAD_DIM:MEM_W + (h + 1) * MEM_HEAD_DIM].astype(BF16)
        s = _dot_t(qh, kh)
        p = jnp.exp(s - jnp.max(s, axis=-1, keepdims=True))
        outs.append(_dot(p.astype(BF16), vh) / jnp.sum(p, axis=-1, keepdims=True))
    o_ref[0] = x + _dot(jnp.concatenate(outs, axis=-1).astype(BF16), wo_ref[...])


def mix_out_mem_attention(x, ya, yb, wa, wb, mem_kv, g, wq, gq, gk, wo, *, tm=512):
    b, s, d = x.shape
    m = mem_kv.shape[1]
    tm = min(tm, s)
    const = lambda shape: pl.BlockSpec(shape, lambda i, t: (0,) * len(shape), pipeline_mode=pl.Buffered(1))
    tile = lambda w: pl.BlockSpec((1, tm, w), lambda i, t: (i, t, 0))
    return pl.pallas_call(
        _memx_kernel,
        grid=(b, s // tm),
        in_specs=[tile(d), tile(ya.shape[2]), tile(yb.shape[2]), const(wa.shape), const(wb.shape),
                  const((1, d)), const((d, MEM_W)),
                  pl.BlockSpec((1, m, 2 * MEM_W), lambda i, t: (i, 0, 0)),
                  const((1, MEM_HEAD_DIM)), const((1, MEM_HEAD_DIM)), const((MEM_W, d))],
        out_specs=tile(d),
        out_shape=jax.ShapeDtypeStruct((b, s, d), F32),
        compiler_params=_params("parallel", "arbitrary"),
        name="mix_out_mem_attention",
    )(x, ya, yb, wa, wb, g.reshape(1, d).astype(F32), wq, mem_kv, gq.reshape(1, -1).astype(F32),
      gk.reshape(1, -1).astype(F32), wo)


def _rope_tables(positions, dim, lead_ones, tail):
    inv = 1.0 / (ROPE_THETA ** (jnp.arange(0, dim, 2, dtype=F32) / dim))
    ang = positions.astype(F32)[..., None] * inv
    c, s = jnp.cos(ang), jnp.sin(ang)
    shape = positions.shape
    cos = jnp.concatenate([jnp.ones(shape + (lead_ones,), F32), c, c, jnp.ones(shape + (tail,), F32)], axis=-1)
    sin = jnp.concatenate([jnp.zeros(shape + (lead_ones,), F32), -s, s, jnp.zeros(shape + (tail,), F32)], axis=-1)
    reps = LANES // cos.shape[-1]
    return jnp.tile(cos, (1, 1, reps)), jnp.tile(sin, (1, 1, reps))


def _pad_cols(w, cols):
    return jnp.pad(w, ((0, 0), (0, cols - w.shape[1])))


def _ab_in_layout(w):
    c = (SWA_HEADS + 2 * SWA_KV_HEADS) * HEAD_DIM + 3 * RWKV_DIM
    return jnp.concatenate([w[:, :c], _pad_cols(w[:, c:c + DECAY_LORA], LANES),
                            _pad_cols(w[:, c + DECAY_LORA:c + DECAY_LORA + AAA_LORA], LANES),
                            _pad_cols(w[:, c + DECAY_LORA + AAA_LORA:], 2 * LANES)], axis=1)


def _cd_in_layout(w):
    c1 = MLA_Q_RANK + MLA_KV_RANK
    z = lambda n: jnp.zeros((w.shape[0], n), w.dtype)
    return jnp.concatenate([w[:, :c1], z(MLA_NOPE), w[:, c1:c1 + MLA_ROPE], z(LANES - MLA_NOPE - MLA_ROPE),
                            z(LANES), w[:, c1 + MLA_ROPE:]], axis=1)


def _head_slabs(w, per_head):
    k = w.shape[0]
    return jnp.pad(w.reshape(k, -1, per_head), ((0, 0), (0, 0), (0, LANES - per_head))).reshape(k, -1)


def _slab_vec(*parts):
    v = jnp.concatenate([p.astype(F32) for p in parts])
    return jnp.pad(v, (0, LANES - v.shape[0])).reshape(1, LANES)


def kernel(x, mem, positions, ffn1_norm, ffn1_w_gate, ffn1_w_up, ffn1_w_down, mix_norm, ab_w_in, ab_w_out, swa_q_norm, swa_k_norm, swa_sinks, rwkv_mu, rwkv_w0, rwkv_w2, rwkv_a0, rwkv_a2, rwkv_g2, rwkv_k_k, rwkv_k_a, rwkv_r_k, rwkv_gn_g, rwkv_gn_b, cd_w_in, cd_w_out, mla_cq_norm, mla_ckv_norm, mla_w_uq, mla_w_ukv, mla_q_nope_norm, mla_k_nope_norm, mla_q_rope_norm, mla_k_rope_norm, diff_q_norm, diff_k_norm, diff_lq1, diff_lk1, diff_lq2, diff_lk2, diff_subln, memx_norm, memx_w_q, memx_q_norm, memx_w_o, mem_norm, mem_w_kv, mem_k_norm, ffn2_norm, ffn2_w_gate, ffn2_w_up, ffn2_w_down):
    b, s, d = x.shape
    m = mem.shape[1]
    t = b * s
    depth = ffn1_norm.shape[0]
    bf = lambda w: w.astype(BF16)
    cos64, sin64 = _rope_tables(positions, HEAD_DIM, 0, 0)
    cos32, sin32 = _rope_tables(positions, MLA_ROPE, MLA_NOPE, LANES - MLA_NOPE - MLA_ROPE)

    mem_kv = norm_matmul(mem.reshape(b * m, d), mem_norm, bf(mem_w_kv), name="mem_kv").reshape(b, m, 2 * MEM_W)

    x = x.reshape(t, d)
    for layer in range(depth):
        j = layer // 2
        x = ffn(x, ffn1_norm[layer], ffn1_w_gate, ffn1_w_up, ffn1_w_down, layer, name="ffn1")
        if layer % 2 == 0:
            u = norm_matmul(x, mix_norm[layer], _ab_in_layout(bf(ab_w_in[j])), name="ab_in")
            u = u.reshape(b, s, AB_IN_PAD)
            y_a = swa_attention(u, cos64, sin64, _slab_vec(swa_q_norm[j], swa_q_norm[j]),
                                _slab_vec(swa_k_norm[j], swa_k_norm[j]), swa_sinks[j])
            y_b = rwkv_mix(u, rwkv_mu[j], rwkv_w0[j], rwkv_w2[j], rwkv_a0[j], rwkv_a2[j], rwkv_g2[j],
                           rwkv_k_k[j], rwkv_k_a[j], rwkv_r_k[j], rwkv_gn_g[j], rwkv_gn_b[j])
            mixed, w_out, split = (y_a, y_b), bf(ab_w_out[j]), SWA_HEADS * HEAD_DIM
        else:
            u = norm_matmul(x, mix_norm[layer], _cd_in_layout(bf(cd_w_in[j])), name="cd_in")
            u = u.reshape(b, s, CD_IN_PAD)
            zero64 = jnp.zeros((MLA_NOPE,), F32)
            q_c, k_c, vt_c = mla_prep(u, cos32, sin32,
                                      _slab_vec(mla_q_nope_norm[j], mla_q_rope_norm[j]),
                                      _slab_vec(mla_k_nope_norm[j]),
                                      _slab_vec(zero64, mla_k_rope_norm[j]),
                                      mla_cq_norm[j], mla_ckv_norm[j],
                                      bf(_head_slabs(mla_w_uq[j], MLA_NOPE + MLA_ROPE)), bf(mla_w_ukv[j]))
            y_c = causal_attention(q_c, k_c, vt_c, n_sm=1, hps=8, name="mla_attention")
            q_d, k_d, vt_d = diff_prep(u, cos64, sin64, _slab_vec(diff_q_norm[j], diff_q_norm[j]),
                                       _slab_vec(diff_k_norm[j], diff_k_norm[j]))
            lambda_init = 0.8 - 0.6 * math.exp(-0.3 * layer)
            lam = (jnp.exp(jnp.sum(diff_lq1[j].astype(F32) * diff_lk1[j].astype(F32)))
                   - jnp.exp(jnp.sum(diff_lq2[j].astype(F32) * diff_lk2[j].astype(F32))) + lambda_init)
            y_d = causal_attention(q_d, k_d, vt_d, n_sm=2, hps=4, lam=lam, g=diff_subln[j],
                                   out_scale=1.0 - lambda_init, name="diff_attention")
            mixed, w_out, split = (y_c, y_d), bf(cd_w_out[j]), MLA_HEADS * MLA_V
        x = mix_out_mem_attention(x.reshape(b, s, d), mixed[0], mixed[1], w_out[:split], w_out[split:], mem_kv,
                                  memx_norm[layer], bf(memx_w_q[layer]), memx_q_norm[layer], mem_k_norm,
                                  bf(memx_w_o[layer])).reshape(t, d)
        x = ffn(x, ffn2_norm[layer], ffn2_w_gate, ffn2_w_up, ffn2_w_down, layer, name="ffn2")
    return x.reshape(b, s, d)
```

```python
import functools
import math

import jax
import jax.numpy as jnp
from jax import lax
from jax.experimental import pallas as pl
from jax.experimental.pallas import tpu as pltpu

F32 = jnp.float32
BF16 = jnp.bfloat16

EPS = 1e-6
ROPE_THETA = 10000.0
NEG_INF = -1e30
LOG2E = math.log2(math.e)
ATTN_LOOKAHEAD = 8
ATTN_TILE = 256
SEG_W = 256
LANES = 128

D_MODEL = 2048
D_FF = 5632
HEAD_DIM = 64
SWA_HEADS = 16
SWA_KV_HEADS = 4
SWA_BLOCK = 128
RWKV_DIM = 1024
RWKV_CHUNK = 64
RWKV_GN_EPS = 64e-5
DECAY_LORA, AAA_LORA, GATE_LORA = 64, 64, 160
MLA_HEADS, MLA_Q_RANK, MLA_KV_RANK, MLA_NOPE, MLA_ROPE, MLA_V = 16, 512, 256, 64, 32, 64
DIFF_HEADS, DIFF_QK, DIFF_V = 8, 64, 128
MEM_HEADS, MEM_HEAD_DIM = 4, 128
MEM_W = MEM_HEADS * MEM_HEAD_DIM
AB_IN_PAD = 5120
CD_IN_PAD = 4096

VMEM_LIMIT = 48 * 1024 * 1024
FFN_VMEM_LIMIT = 60 * 1024 * 1024


def _params(*sem):
    return pltpu.CompilerParams(dimension_semantics=sem, vmem_limit_bytes=VMEM_LIMIT)


def _dot(a, b):
    return jnp.dot(a, b, preferred_element_type=F32)


def _dot_t(a, b):
    return lax.dot_general(a, b, (((1,), (1,)), ((), ())), preferred_element_type=F32)


def _dot_0(a, b):
    return lax.dot_general(a, b, (((0,), (0,)), ((), ())), preferred_element_type=F32)


def _rms(x, g):
    return x * lax.rsqrt(jnp.mean(x * x, axis=-1, keepdims=True) + EPS) * g


def _seg_sum(x, e):
    xb = x.astype(BF16)
    w = e.shape[0]
    parts = [_dot(xb[:, i:i + w], e) for i in range(0, x.shape[1], w)]
    return parts[0] if len(parts) == 1 else jnp.concatenate(parts, axis=-1)


def _tile_lanes(v, width):
    return v if v.shape[-1] == width else jnp.tile(v, (1, width // v.shape[-1]))


def _store_transposed_tiles(vt_ref, v):
    for r in range(vt_ref.shape[1]):
        vt_ref[0, r] = v[r * ATTN_TILE:(r + 1) * ATTN_TILE, :].T.astype(vt_ref.dtype)


def _swap_halves(x, half):
    w = x.shape[-1]
    lane = lax.broadcasted_iota(jnp.int32, x.shape, x.ndim - 1)
    low = (lane & (2 * half - 1)) < half
    return jnp.where(low, pltpu.roll(x, w - half, x.ndim - 1), pltpu.roll(x, half, x.ndim - 1))


def _norm_matmul_kernel(x_ref, g_ref, *refs, starts, counts):
    w_refs, o_ref, xn_ref = refs[:len(starts)], refs[len(starts)], refs[len(starts) + 1]
    j = pl.program_id(1)

    @pl.when(j == 0)
    def _():
        xn_ref[...] = _rms(x_ref[...], g_ref[...]).astype(BF16)

    for w_ref, start, count in zip(w_refs, starts, counts):
        @pl.when((j >= start) & (j < start + count))
        def _(w_ref=w_ref):
            o_ref[...] = _dot(xn_ref[...], w_ref[...].astype(BF16))


def norm_matmul(x, g, segments, *, tm=1024, tn=512, name="norm_matmul"):
    t, k = x.shape
    tm = min(tm, t)
    assert t % tm == 0
    counts = [seg[3] for seg in segments]
    starts = [sum(counts[:i]) for i in range(len(counts))]

    def w_spec(w, layer, first, start, count):
        col = lambda j: first + jnp.clip(j - start, 0, count - 1)
        if w.ndim == 3:
            return pl.BlockSpec((None, k, tn), lambda i, j: (layer, 0, col(j)))
        return pl.BlockSpec((k, tn), lambda i, j: (0, col(j)))

    return pl.pallas_call(
        functools.partial(_norm_matmul_kernel, starts=tuple(starts), counts=tuple(counts)),
        grid=(t // tm, sum(counts)),
        in_specs=[pl.BlockSpec((tm, k), lambda i, j: (i, 0)),
                  pl.BlockSpec((1, k), lambda i, j: (0, 0))]
                 + [w_spec(w, layer, first, start, count)
                    for (w, layer, first, count), start in zip(segments, starts)],
        out_specs=pl.BlockSpec((tm, tn), lambda i, j: (i, j)),
        out_shape=jax.ShapeDtypeStruct((t, sum(counts) * tn), F32),
        scratch_shapes=[pltpu.VMEM((tm, k), BF16)],
        compiler_params=_params("parallel", "arbitrary"),
        name=name,
    )(x, g.reshape(1, k).astype(F32), *[seg[0] for seg in segments])


def _ffn_kernel(x_ref, g_ref, wg_ref, wu_ref, wd_ref, o_ref, xn_ref):
    @pl.when(pl.program_id(1) == 0)
    def _():
        x = x_ref[...]
        xn_ref[...] = _rms(x, g_ref[...]).astype(BF16)
        o_ref[...] = x

    xn = xn_ref[...]
    a = _dot(xn, wg_ref[...].astype(BF16))
    b = _dot(xn, wu_ref[...].astype(BF16))
    h = (a * (0.5 / (1.0 + jnp.exp(-a))) * b).astype(BF16)
    o_ref[...] += _dot(h, wd_ref[...].astype(BF16))


def ffn(x, g, wg, wu, wd, layer, *, tm=1024, tf=256, name="ffn"):
    t, d = x.shape
    ff = wg.shape[2]
    tm = min(tm, t)
    assert t % tm == 0 and ff % tf == 0
    nf = ff // tf
    return pl.pallas_call(
        _ffn_kernel,
        grid=(t // tm, nf),
        in_specs=[pl.BlockSpec((tm, d), lambda i, f: (i, 0)),
                  pl.BlockSpec((1, d), lambda i, f: (0, 0)),
                  pl.BlockSpec((None, d, tf), lambda i, f: (layer, 0, f)),
                  pl.BlockSpec((None, d, tf), lambda i, f: (layer, 0, f)),
                  pl.BlockSpec((None, tf, d), lambda i, f: (layer, f, 0))],
        out_specs=pl.BlockSpec((tm, d), lambda i, f: (i, 0)),
        out_shape=jax.ShapeDtypeStruct((t, d), F32),
        scratch_shapes=[pltpu.VMEM((tm, d), BF16)],
        compiler_params=pltpu.CompilerParams(dimension_semantics=("parallel", "arbitrary"),
                                             vmem_limit_bytes=FFN_VMEM_LIMIT),
        name=name,
    )(x, g.reshape(1, d).astype(F32), wg, wu, wd)


def _swa_kernel(sink_ref, q_ref, kc_ref, kp_ref, vc_ref, vp_ref, cc_ref, sc_ref, cp_ref, sp_ref,
                gq_ref, gk_ref, e_ref, o_ref):
    n = pl.program_id(1)
    blk = SWA_BLOCK
    group = SWA_HEADS // SWA_KV_HEADS
    q = q_ref[0]
    k = jnp.concatenate([kp_ref[0], kc_ref[0]], axis=0)
    v = jnp.concatenate([vp_ref[0], vc_ref[0]], axis=0)
    cos_q, sin_q = cc_ref[0], sc_ref[0]
    cos_k = jnp.concatenate([cp_ref[0], cos_q], axis=0)
    sin_k = jnp.concatenate([sp_ref[0], sin_q], axis=0)
    def prep(x, g, cos, sin, scale):
        w = x.shape[1]
        xg = x * _tile_lanes(g, w)
        xr = xg * _tile_lanes(cos, w) + _swap_halves(xg, HEAD_DIM // 2) * _tile_lanes(sin, w)
        return xr * lax.rsqrt(_seg_sum(x * x, e_ref[...]) * (1.0 / HEAD_DIM) + EPS) * scale

    qr = prep(q, gq_ref[...], cos_q, sin_q, HEAD_DIM ** -0.5 * LOG2E).astype(BF16)
    kr = prep(k, gk_ref[...], cos_k, sin_k, 1.0)
    key_i = lax.broadcasted_iota(jnp.int32, (2 * blk, blk), 0)
    qry_i = lax.broadcasted_iota(jnp.int32, (2 * blk, blk), 1)
    rel = qry_i + blk - key_i
    valid = (rel >= 0) & (rel < blk) & ((n > 0) | (key_i >= blk))
    low = lax.broadcasted_iota(jnp.int32, (blk, LANES), 1) < HEAD_DIM
    zero = jnp.zeros((blk, LANES), BF16)
    k_dup, v_t = [], []
    for g in range(SWA_KV_HEADS):
        kg = kr[:, g * HEAD_DIM:(g + 1) * HEAD_DIM]
        k_dup.append(jnp.concatenate([kg, kg], axis=-1).astype(BF16))
        v_t.append(v[:, g * HEAD_DIM:(g + 1) * HEAD_DIM].T.astype(BF16))

    def scores(h):
        slab = qr[:, (h // 2) * LANES:(h // 2 + 1) * LANES]
        qh = jnp.where(low, slab, zero) if h % 2 == 0 else jnp.where(low, zero, slab)
        return _dot_t(k_dup[h // group], qh)

    ahead = [scores(h) for h in range(ATTN_LOOKAHEAD)]
    ones = jnp.ones((8, 2 * blk), BF16)
    outs = []
    for h in range(SWA_HEADS):
        s = jnp.where(valid, ahead.pop(0), NEG_INF)
        if h + ATTN_LOOKAHEAD < SWA_HEADS:
            ahead.append(scores(h + ATTN_LOOKAHEAD))
        sink = sink_ref[h] * LOG2E
        m = jnp.maximum(jnp.max(s, axis=0, keepdims=True), sink)
        p = jnp.exp2(s - m).astype(BF16)
        den = _dot(ones, p)[0:1] + jnp.exp2(sink - m)
        outs.append(_dot(v_t[h // group], p) / den)
    slabs = [jnp.concatenate(outs[i:i + 2], axis=0).T for i in range(0, SWA_HEADS, 2)]
    o_ref[0] = jnp.concatenate(slabs, axis=-1).astype(o_ref.dtype)


def swa_attention(u, cos, sin, gq, gk, sinks):
    b, s, _ = u.shape
    nb = s // SWA_BLOCK
    qw, kw = SWA_HEADS * HEAD_DIM, SWA_KV_HEADS * HEAD_DIM
    cur = lambda c: (lambda i, n: (i, n, c))
    prev = lambda c: (lambda i, n: (i, jnp.maximum(n - 1, 0), c))
    tab = pl.BlockSpec((1, SWA_BLOCK, LANES), cur(0))
    tab_prev = pl.BlockSpec((1, SWA_BLOCK, LANES), prev(0))
    gain = pl.BlockSpec((1, LANES), lambda i, n: (0, 0))
    seg = (jnp.arange(kw)[:, None] // HEAD_DIM == jnp.arange(kw)[None, :] // HEAD_DIM).astype(BF16)
    return pl.pallas_call(
        _swa_kernel,
        grid=(b, nb),
        in_specs=[pl.BlockSpec(memory_space=pltpu.SMEM),
                  pl.BlockSpec((1, SWA_BLOCK, qw), cur(0)),
                  pl.BlockSpec((1, SWA_BLOCK, kw), cur(qw // kw)),
                  pl.BlockSpec((1, SWA_BLOCK, kw), prev(qw // kw)),
                  pl.BlockSpec((1, SWA_BLOCK, kw), cur(qw // kw + 1)),
                  pl.BlockSpec((1, SWA_BLOCK, kw), prev(qw // kw + 1)),
                  tab, tab, tab_prev, tab_prev, gain, gain, pl.BlockSpec((kw, kw), lambda i, n: (0, 0))],
        out_specs=pl.BlockSpec((1, SWA_BLOCK, qw), cur(0)),
        out_shape=jax.ShapeDtypeStruct((b, s, qw), BF16),
        compiler_params=_params("parallel", "arbitrary"),
        name="swa_attention",
    )(sinks.astype(F32), u, u, u, u, u, cos, sin, cos, sin, gq, gk, seg)


def _mm(a, b, dims, passes):
    dn = (dims, ((), ()))
    dg = lambda x, y: lax.dot_general(x, y, dn, preferred_element_type=F32)
    ah = a.astype(BF16)
    bh = b.astype(BF16)
    if passes == 1:
        return dg(ah, bh)
    al = (a - ah.astype(F32)).astype(BF16)
    bl = (b - bh.astype(F32)).astype(BF16)
    return dg(ah, bh) + dg(ah, bl) + dg(al, bh)


_NN = ((1,), (0,))
_NT = ((1,), (1,))
_TN = ((0,), (0,))
P_SC, P_INV, P_PQ, P_OUT, P_ST = 1, 1, 1, 1, 1
RWKV_UNROLL = 8


RWKV_PREP_ROWS = 256


def _rwkv_kernel(ur_ref, uk_ref, uv_ref, ul_ref, mur_ref, muk_ref, muv_ref, mul_ref, w0_ref, w2_ref, a0_ref,
                 a2_ref, g2_ref, kkw_ref, ka_ref, gng_ref, gnb_ref, rk_ref,
                 o_ref, r_ref, k_ref, v_ref, kk_ref, b_ref, lw_ref, g_ref,
                 st_ref, y1_ref, y0_ref, n_ref, z_ref, dec_ref, *, nchunk):
    c = RWKV_CHUNK
    lane_c = lax.broadcasted_iota(jnp.int32, (c, LANES), 1)
    head0 = lane_c < HEAD_DIM
    ri = lax.broadcasted_iota(jnp.int32, (2 * c, 2 * c), 0)
    ci = lax.broadcasted_iota(jnp.int32, (2 * c, 2 * c), 1)
    eye = jnp.where(ri == ci, 1.0, 0.0)
    tril_c = jnp.where(lax.broadcasted_iota(jnp.int32, (c, c), 0) >= lax.broadcasted_iota(jnp.int32, (c, c), 1),
                       1.0, 0.0).astype(BF16)
    stack = lambda x: jnp.concatenate([jnp.where(head0, x, 0.0), jnp.where(head0, 0.0, x)], axis=0)

    def seg_mean(x):
        first = lax.broadcasted_iota(jnp.int32, x.shape, 1) < HEAD_DIM
        m0 = jnp.sum(jnp.where(first, x, 0.0), axis=-1, keepdims=True)
        m1 = jnp.sum(jnp.where(first, 0.0, x), axis=-1, keepdims=True)
        return jnp.where(first, m0, m1) * (1.0 / HEAD_DIM)

    def prep(io, first_group):
        group_rows = RWKV_UNROLL * c
        for tix in range(group_rows // RWKV_PREP_ROWS):
            start = pl.multiple_of(io * group_rows + tix * RWKV_PREP_ROWS, RWKV_PREP_ROWS)
            rows = pl.ds(start, RWKV_PREP_ROWS)
            at_start = first_group and tix == 0

            def shifted(ref, mu_ref):
                x = ref[0, rows, :]
                if at_start:
                    last = jnp.zeros((1, x.shape[1]), F32)
                else:
                    last = ref[0, pl.ds(pl.multiple_of(start - 8, 8), 8), :][7:8, :]
                row = lax.broadcasted_iota(jnp.int32, x.shape, 0)
                prev = jnp.where(row == 0, last, pltpu.roll(x, 1, 0))
                return x + (prev - x) * mu_ref[...]

            r = shifted(ur_ref, mur_ref)
            k = shifted(uk_ref, muk_ref)
            v = shifted(uv_ref, muv_ref)
            lo = shifted(ul_ref, mul_ref)
            yield
            w_lo, a_lo, g_lo = lo[:, 0:LANES], lo[:, LANES:2 * LANES], lo[:, 2 * LANES:4 * LANES]
            z = -(w0_ref[...] + _mm(jnp.tanh(w_lo), w2_ref[...], _NN, 3))
            w = -(jnp.maximum(z, 0.0) + jnp.log(1.0 + jnp.exp(-jnp.abs(z)))) - 0.5
            a = 1.0 / (1.0 + jnp.exp(-(a0_ref[...] + _mm(a_lo, a2_ref[...], _NN, 1))))
            g = _mm(1.0 / (1.0 + jnp.exp(-g_lo)), g2_ref[...], _NN, 1)
            yield
            kk = k * kkw_ref[...]
            kk = kk / jnp.maximum(jnp.sqrt(seg_mean(kk * kk) * float(HEAD_DIM)), 1e-12)
            r_ref[rows, :] = r
            k_ref[rows, :] = k * (1.0 + (a - 1.0) * ka_ref[...])
            v_ref[rows, :] = v
            kk_ref[rows, :] = kk
            b_ref[rows, :] = kk * a
            lw_ref[rows, :] = -jnp.exp(w)
            g_ref[rows, :] = g
            yield

    def build(ics):
        each = lambda f, *cols: [f(*args) for args in zip(*cols)]
        sls = [pl.ds(pl.multiple_of(ic * c, c), c) for ic in ics]
        load = lambda ref: [ref[sl, :] for sl in sls]
        r, k, v, kk, b, lw = (load(ref) for ref in (r_ref, k_ref, v_ref, kk_ref, b_ref, lw_ref))

        def running_sum(x):
            l1 = x.astype(BF16)
            rest = x - l1.astype(F32)
            l2 = rest.astype(BF16)
            l3 = (rest - l2.astype(F32)).astype(BF16)
            return _dot(tril_c, l1) + _dot(tril_c, l2) + _dot(tril_c, l3)

        cum = each(running_sum, lw)
        yield
        cum_end = [x[c - 1:c, :] for x in cum]
        e_neg = each(lambda x: jnp.exp(-x), cum)
        e_end = each(lambda x, xe: jnp.exp(xe - x), cum, cum_end)
        a_s = each(lambda kk_, x, l: stack(-kk_ * jnp.exp(x - l)), kk, cum, lw)
        r_s = each(lambda r_, x: stack(r_ * jnp.exp(x)), r, cum)
        b_s = each(lambda b_, e: stack(b_ * e), b, e_neg)
        k_s = each(lambda k_, e: stack(k_ * e), k, e_neg)
        bh_s = each(lambda b_, e: stack(b_ * e), b, e_end)
        kh_s = each(lambda k_, e: stack(k_ * e), k, e_end)
        v_s = each(stack, v)
        n2 = 2 * c
        sc = each(lambda a_, r_, b_, k_: _mm(jnp.concatenate([a_, r_], axis=0),
                                             jnp.concatenate([b_, k_], axis=0), _NT, P_SC), a_s, r_s, b_s, k_s)
        low = [jnp.where(ri > ci, x[:n2, :n2], 0.0) for x in sc]
        a_ak = [jnp.where(ri > ci, x[:n2, n2:], 0.0) for x in sc]
        a_rb = [jnp.where(ri >= ci, x[n2:, :n2], 0.0) for x in sc]
        a_rk = [jnp.where(ri >= ci, x[n2:, n2:], 0.0) for x in sc]
        yield
        inv = [eye + x for x in low]
        pw = each(lambda x: _mm(x, x, _NN, P_INV), low)
        yield
        levels = 5
        for lvl in range(levels - 1):
            both = each(lambda t, x: _mm(jnp.concatenate([t, x], axis=0), x, _NN, P_INV), inv, pw)
            inv = each(lambda t, r: t + r[:n2], inv, both)
            pw = [r[n2:] for r in both]
            yield
        inv = each(lambda t, x: t + _mm(t, x, _NN, P_INV), inv, pw)
        yield
        akv = each(lambda x, y: _mm(x, y, _NN, P_PQ), a_ak, v_s)
        yield
        pq = each(lambda t, x, y: _mm(t, jnp.concatenate([x, y], axis=1), _NN, P_PQ), inv, a_s, akv)
        yield
        yy = each(lambda x, y: _mm(x, y, _NN, P_OUT), a_rb, pq)
        y0b = each(lambda x, y: _mm(x, y, _NN, P_OUT), a_rk, v_s)
        nz = each(lambda x, y: _mm(x, y, _TN, P_OUT), pq, bh_s)
        zb = each(lambda x, y: _mm(x, y, _TN, P_OUT), v_s, kh_s)
        for i, ic in enumerate(ics):
            y1_ref[ic] = r_s[i] + yy[i][:, :LANES]
            y0_ref[ic] = yy[i][:, LANES:] + y0b[i]
            n_ref[ic] = nz[i][:LANES]
            z_ref[ic] = nz[i][LANES:] + zb[i]
            dec_ref[ic] = jnp.broadcast_to(jnp.exp(cum_end[i]), (8, LANES))

    def emit(ic, st):
        sl = pl.ds(pl.multiple_of(ic * c, c), c)
        r, k, v, g = r_ref[sl, :], k_ref[sl, :], v_ref[sl, :], g_ref[sl, :]
        y_st = _mm(y1_ref[ic], st, _NT, P_ST) + y0_ref[ic]
        y = y_st[0:c] + y_st[c:2 * c]
        mean = seg_mean(y)
        var = seg_mean((y - mean) * (y - mean))
        yn = (y - mean) * lax.rsqrt(var + RWKV_GN_EPS) * gng_ref[...] + gnb_ref[...]
        bonus = seg_mean(r * k * rk_ref[...]) * float(HEAD_DIM) * v
        o_ref[0, sl, :] = ((yn + bonus) * g).astype(o_ref.dtype)
        return st * dec_ref[ic][0:1, :] + _mm(st, n_ref[ic], _NN, P_ST) + z_ref[ic]

    def scan(ics):
        st = st_ref[...]
        for ic in ics:
            st = emit(ic, st)
            yield
        st_ref[...] = st

    def run(*gens):
        live = list(gens)
        while live:
            live = [gen for gen in live if next(gen, live) is not live]

    group = lambda io: [io * RWKV_UNROLL + i for i in range(RWKV_UNROLL)]
    ngroup = nchunk // RWKV_UNROLL
    st_ref[...] = jnp.zeros_like(st_ref)
    run(prep(0, True))
    if ngroup > 1:
        run(prep(1, False), build(group(0)))
    else:
        run(build(group(0)))

    @pl.loop(1, ngroup - 1)
    def _(io):
        run(prep(io + 1, False), build(group(io)), scan(group(io - 1)))

    if ngroup > 1:
        run(build(group(ngroup - 1)), scan(group(ngroup - 2)))
    run(scan(group(ngroup - 1)))


def rwkv_mix(u, mu, w0, w2, a0, a2, g2, k_k, k_a, r_k, gn_g, gn_b):
    bsz, s, _ = u.shape
    npair = RWKV_DIM // LANES
    nchunk = s // RWKV_CHUNK
    assert nchunk % RWKV_UNROLL == 0 and (RWKV_UNROLL * RWKV_CHUNK) % RWKV_PREP_ROWS == 0
    base = (SWA_HEADS + 2 * SWA_KV_HEADS) * HEAD_DIM // LANES
    lora_w = 4 * LANES
    slab = lambda off: pl.BlockSpec((1, s, LANES), lambda i, p: (i, 0, off + p))
    vec = pl.BlockSpec((1, LANES), lambda i, p: (0, p))
    cols = lambda rows: pl.BlockSpec((rows, LANES), lambda i, p: (0, p))
    row = lambda vv: vv.reshape(1, -1).astype(F32)
    pad_rows = lambda m, rows: jnp.pad(m, ((0, rows - m.shape[0]), (0, 0))).astype(F32)
    pad_cols = lambda vv, n: jnp.pad(vv, (0, n - vv.shape[0]))
    c3 = 3 * RWKV_DIM
    mu_l = jnp.concatenate([pad_cols(mu[c3:c3 + DECAY_LORA], LANES),
                            pad_cols(mu[c3 + DECAY_LORA:c3 + DECAY_LORA + AAA_LORA], LANES),
                            pad_cols(mu[c3 + DECAY_LORA + AAA_LORA:], 2 * LANES)])
    seq = pltpu.VMEM((s, LANES), F32)
    mat = pltpu.VMEM((nchunk, LANES, LANES), F32)
    return pl.pallas_call(
        functools.partial(_rwkv_kernel, nchunk=nchunk),
        grid=(bsz, npair),
        in_specs=[slab(base), slab(base + npair), slab(base + 2 * npair),
                  pl.BlockSpec((1, s, lora_w), lambda i, p: (i, 0, (base + 3 * npair) * LANES // lora_w)),
                  vec, vec, vec, pl.BlockSpec((1, lora_w), lambda i, p: (0, 0)),
                  vec, cols(LANES), vec, cols(LANES), cols(2 * LANES), vec, vec, vec, vec, vec],
        out_specs=pl.BlockSpec((1, s, LANES), lambda i, p: (i, 0, p)),
        out_shape=jax.ShapeDtypeStruct((bsz, s, RWKV_DIM), BF16),
        scratch_shapes=[seq] * 7 + [pltpu.VMEM((LANES, LANES), F32), mat, mat, mat, mat,
                                    pltpu.VMEM((nchunk, 8, LANES), F32)],
        compiler_params=_params("parallel", "arbitrary"),
        name="rwkv_mix",
    )(u, u, u, u, row(mu[:RWKV_DIM]), row(mu[RWKV_DIM:2 * RWKV_DIM]), row(mu[2 * RWKV_DIM:c3]), row(mu_l),
      row(w0), pad_rows(w2, LANES), row(a0), pad_rows(a2, LANES), pad_rows(g2, 2 * LANES),
      row(k_k), row(k_a), row(gn_g), row(gn_b), row(r_k))


def _mla_prep_kernel(cq_ref, ckv_ref, pe_ref, cos_ref, sin_ref, e_ref, gq_ref, gkn_ref, gkp_ref, invn_ref,
                     gcq_ref, gckv_ref, wuq_ref, wukv_ref, qo_ref, ko_ref, vo_ref, cqn_ref, ckvn_ref, kpe_ref):
    cos, sin = cos_ref[0], sin_ref[0]
    half = MLA_ROPE // 2

    def rope(x, g):
        w = x.shape[1]
        xg = x * _tile_lanes(g, w)
        return xg * _tile_lanes(cos, w) + _swap_halves(xg, half) * _tile_lanes(sin, w)

    @pl.when(pl.program_id(2) == 0)
    def _():
        cqn_ref[...] = _rms(cq_ref[0], gcq_ref[...]).astype(BF16)
        ckvn_ref[...] = _rms(ckv_ref[0], gckv_ref[...]).astype(BF16)
        pe = pe_ref[0]
        inv_pe = lax.rsqrt(jnp.sum(pe * pe, axis=-1, keepdims=True) * (1.0 / MLA_ROPE) + EPS)
        kpe_ref[...] = rope(pe, gkp_ref[...]) * inv_pe

    x = _dot(cqn_ref[...], wuq_ref[...])
    w = x.shape[1]
    inv_n = _tile_lanes(invn_ref[...], w)
    inv = lax.rsqrt(_seg_sum(x * x, e_ref[...]) * inv_n + EPS)
    qo_ref[0] = (rope(x, gq_ref[...]) * inv * ((MLA_NOPE + MLA_ROPE) ** -0.5 * LOG2E)).astype(qo_ref.dtype)
    kv = _dot(ckvn_ref[...], wukv_ref[...])
    inv_k = lax.rsqrt(_seg_sum(kv * kv, e_ref[...]) * inv_n + EPS)
    k_nope = kv * inv_k * _tile_lanes(gkn_ref[...], w)
    ko_ref[0] = (k_nope + _tile_lanes(kpe_ref[...], w)).astype(ko_ref.dtype)
    v = jnp.concatenate([kv[:, h * LANES + MLA_NOPE:(h + 1) * LANES] for h in range(w // LANES)], axis=-1)
    _store_transposed_tiles(vo_ref, v)


def mla_prep(u, cos, sin, gq, gkn, gkp, gcq, gckv, w_uq, w_ukv, *, ts=512, tc=512):
    b, s, _ = u.shape
    wtot = w_uq.shape[1]
    ts = min(ts, s)
    lane = jnp.arange(SEG_W)
    same = (lane[:, None] // LANES == lane[None, :] // LANES)
    pos = lane % LANES
    nope = pos < MLA_NOPE
    pe = (pos >= MLA_NOPE) & (pos < MLA_NOPE + MLA_ROPE)
    seg = (same & ((nope[:, None] & nope[None, :]) | (pe[:, None] & pe[None, :]))).astype(BF16)
    p1 = jnp.arange(LANES)
    inv_n = jnp.where(p1 < MLA_NOPE, 1.0 / MLA_NOPE, jnp.where(p1 < MLA_NOPE + MLA_ROPE, 1.0 / MLA_ROPE, 0.0))
    blk = pl.BlockSpec((1, ts, tc), lambda i, t, c: (i, t, c))
    tab = pl.BlockSpec((1, ts, LANES), lambda i, t, c: (i, t, 0))
    vec = pl.BlockSpec((1, LANES), lambda i, t, c: (0, 0))
    pe_blk = (MLA_Q_RANK + MLA_KV_RANK) // LANES
    out = jax.ShapeDtypeStruct((b, s, wtot), BF16)
    return pl.pallas_call(
        _mla_prep_kernel,
        grid=(b, s // ts, wtot // tc),
        in_specs=[pl.BlockSpec((1, ts, MLA_Q_RANK), lambda i, t, c: (i, t, 0)),
                  pl.BlockSpec((1, ts, MLA_KV_RANK), lambda i, t, c: (i, t, MLA_Q_RANK // MLA_KV_RANK)),
                  pl.BlockSpec((1, ts, LANES), lambda i, t, c: (i, t, pe_blk)), tab, tab,
                  pl.BlockSpec((SEG_W, SEG_W), lambda i, t, c: (0, 0)), vec, vec, vec, vec,
                  pl.BlockSpec((1, MLA_Q_RANK), lambda i, t, c: (0, 0)),
                  pl.BlockSpec((1, MLA_KV_RANK), lambda i, t, c: (0, 0)),
                  pl.BlockSpec((MLA_Q_RANK, tc), lambda i, t, c: (0, c)),
                  pl.BlockSpec((MLA_KV_RANK, tc), lambda i, t, c: (0, c))],
        out_specs=[blk, blk, pl.BlockSpec((1, ts // ATTN_TILE, tc // LANES * MLA_V, ATTN_TILE),
                                          lambda i, t, c: (i, t, c, 0))],
        out_shape=[out, out, jax.ShapeDtypeStruct((b, s // ATTN_TILE, wtot // LANES * MLA_V, ATTN_TILE), BF16)],
        scratch_shapes=[pltpu.VMEM((ts, MLA_Q_RANK), BF16), pltpu.VMEM((ts, MLA_KV_RANK), BF16),
                        pltpu.VMEM((ts, LANES), F32)],
        compiler_params=_params("parallel", "parallel", "arbitrary"),
        name="mla_prep",
    )(u, u, u, cos, sin, seg, gq, gkn, gkp, inv_n.reshape(1, LANES).astype(F32),
      gcq.reshape(1, -1).astype(F32), gckv.reshape(1, -1).astype(F32), w_uq, w_ukv)


def _diff_prep_kernel(q_ref, k_ref, v_ref, cos_ref, sin_ref, e_ref, gq_ref, gk_ref, qo_ref, ko_ref, vo_ref):
    cos, sin = cos_ref[0], sin_ref[0]
    _store_transposed_tiles(vo_ref, v_ref[0])

    def prep(x, g, scale):
        w = x.shape[1]
        xg = x * _tile_lanes(g, w)
        xr = xg * _tile_lanes(cos, w) + _swap_halves(xg, DIFF_QK // 2) * _tile_lanes(sin, w)
        inv = lax.rsqrt(_seg_sum(x * x, e_ref[...]) * (1.0 / DIFF_QK) + EPS)
        return xr * inv * scale

    qo_ref[0] = prep(q_ref[0], gq_ref[...], DIFF_QK ** -0.5 * LOG2E).astype(qo_ref.dtype)
    ko_ref[0] = prep(k_ref[0], gk_ref[...], 1.0).astype(ko_ref.dtype)


def diff_prep(u, cos, sin, gq, gk, *, ts=512, tc=512):
    b, s, _ = u.shape
    ts = min(ts, s)
    wtot = 2 * DIFF_HEADS * DIFF_QK
    q_base = (CD_IN_PAD - 3 * wtot) // tc
    seg = (jnp.arange(SEG_W)[:, None] // DIFF_QK == jnp.arange(SEG_W)[None, :] // DIFF_QK).astype(BF16)
    blk = lambda off: pl.BlockSpec((1, ts, tc), lambda i, t, c: (i, t, off + c))
    tab = pl.BlockSpec((1, ts, LANES), lambda i, t, c: (i, t, 0))
    vec = pl.BlockSpec((1, LANES), lambda i, t, c: (0, 0))
    out = jax.ShapeDtypeStruct((b, s, wtot), BF16)
    return pl.pallas_call(
        _diff_prep_kernel,
        grid=(b, s // ts, wtot // tc),
        in_specs=[blk(q_base), blk(q_base + wtot // tc), blk(q_base + 2 * wtot // tc), tab, tab,
                  pl.BlockSpec((SEG_W, SEG_W), lambda i, t, c: (0, 0)), vec, vec],
        out_specs=[blk(0), blk(0),
                   pl.BlockSpec((1, ts // ATTN_TILE, tc, ATTN_TILE), lambda i, t, c: (i, t, c, 0))],
        out_shape=[out, out, jax.ShapeDtypeStruct((b, s // ATTN_TILE, wtot, ATTN_TILE), BF16)],
        compiler_params=_params("parallel", "parallel", "arbitrary"),
        name="diff_prep",
    )(u, u, u, cos, sin, seg, gq, gk)


def _causal_attn_kernel(lam_ref, q_ref, k_ref, vt_ref, g_ref, o_ref, *, n_sm, tq, ow, out_scale):
    qi = pl.program_id(2)
    q = q_ref[0]
    hps = q.shape[1] // LANES
    slab = lambda x, h: x[:, h * LANES:(h + 1) * LANES]
    lane = lax.broadcasted_iota(jnp.int32, (tq, LANES), 1)
    qs, src = [], []
    for h in range(hps):
        qh = slab(q, h)
        if n_sm == 2:
            zero = jnp.zeros_like(qh)
            qs += [jnp.where(lane < DIFF_QK, qh, zero), jnp.where(lane < DIFF_QK, zero, qh)]
            src += [h, h]
        else:
            qs.append(qh)
            src.append(h)
    nch = len(qs)
    key_i = lax.broadcasted_iota(jnp.int32, (tq, tq), 0)
    qry_i = lax.broadcasted_iota(jnp.int32, (tq, tq), 1)
    ones = jnp.ones((8, tq), BF16)

    def step(j, carry, diagonal):
        kj = k_ref[0, pl.ds(pl.multiple_of(j * tq, tq), tq), :]
        scores = lambda i: _dot_t(slab(kj, src[i]), qs[i])
        new = []
        ahead = [scores(i) for i in range(min(ATTN_LOOKAHEAD, nch))]
        for i in range(nch):
            s = ahead.pop(0)
            if i + ATTN_LOOKAHEAD < nch:
                ahead.append(scores(i + ATTN_LOOKAHEAD))
            if diagonal:
                s = jnp.where(key_i <= qry_i, s, NEG_INF)
            m, l, acc = carry[3 * i:3 * i + 3]
            m_new = jnp.maximum(m, jnp.max(s, axis=0, keepdims=True))
            alpha = jnp.exp2(m - m_new)
            p = jnp.exp2(s - m_new).astype(BF16)
            new += [m_new, alpha * l + _dot(ones, p)[0:1],
                    alpha * acc + _dot(vt_ref[0, j, src[i] * ow:(src[i] + 1) * ow, :], p)]
        return tuple(new)

    init = (jnp.full((1, tq), NEG_INF, F32), jnp.zeros((1, tq), F32),
            jnp.zeros((ow, tq), F32)) * nch
    carry = lax.fori_loop(0, qi, lambda j, cr: step(j, cr, False), init)
    carry = step(qi, carry, True)
    outs = []
    for h in range(hps):
        c0 = 3 * n_sm * h
        o = carry[c0 + 2] / carry[c0 + 1]
        if n_sm == 2:
            o = o - lam_ref[0] * (carry[c0 + 5] / carry[c0 + 4])
            o = o * lax.rsqrt(jnp.mean(o * o, axis=0, keepdims=True) + EPS) * g_ref[...] * out_scale
        outs.append(o.T.astype(o_ref.dtype))
    o_ref[0] = jnp.concatenate(outs, axis=-1)


def causal_attention(q, k, vt, *, n_sm, lam=None, g=None, out_scale=1.0, hps=4, name="causal_attention"):
    b, s, wtot = q.shape
    tq = vt.shape[3]
    width = hps * LANES
    groups = wtot // width
    ow = vt.shape[2] // (wtot // LANES)
    lam = jnp.zeros((1,), F32) if lam is None else lam.reshape(1).astype(F32)
    g = jnp.ones((ow, 1), F32) if g is None else g.reshape(ow, 1).astype(F32)
    seq = pl.BlockSpec((1, s, width), lambda i, h, t: (i, 0, h))
    tile = pl.BlockSpec((1, tq, width), lambda i, h, t: (i, t, h))
    return pl.pallas_call(
        functools.partial(_causal_attn_kernel, n_sm=n_sm, tq=tq, ow=ow, out_scale=out_scale),
        grid=(b, groups, s // tq),
        in_specs=[pl.BlockSpec(memory_space=pltpu.SMEM), tile, seq,
                  pl.BlockSpec((1, s // tq, hps * ow, tq), lambda i, h, t: (i, 0, h, 0)),
                  pl.BlockSpec((ow, 1), lambda i, h, t: (0, 0))],
        out_specs=pl.BlockSpec((1, tq, hps * ow), lambda i, h, t: (i, t, h)),
        out_shape=jax.ShapeDtypeStruct((b, s, groups * hps * ow), BF16),
        compiler_params=_params("parallel", "parallel", "arbitrary"),
        name=name,
    )(lam, q, k, vt, g)


def _memx_kernel(x_ref, ya_ref, yb_ref, wa_ref, wb_ref, g_ref, wq_ref, kv_ref, gq_ref, gk_ref, wo_ref, o_ref):
    x = x_ref[0] + _dot(ya_ref[0], wa_ref[...]) + _dot(yb_ref[0], wb_ref[...])
    q = _dot(_rms(x, g_ref[...]).astype(BF16), wq_ref[...])
    kv = kv_ref[0]
    outs = []
    for h in range(MEM_HEADS):
        sl = slice(h * MEM_HEAD_DIM, (h + 1) * MEM_HEAD_DIM)
        qh = (_rms(q[:, sl], gq_ref[...]) * MEM_HEAD_DIM ** -0.5).astype(BF16)
        kh = _rms(kv[:, sl], gk_ref[...]).astype(BF16)
        vh = kv[:, MEM_W + h * MEM_HEAD_DIM:MEM_W + (h + 1) * MEM_HEAD_DIM].astype(BF16)
        s = _dot_t(qh, kh)
        p = jnp.exp(s - jnp.max(s, axis=-1, keepdims=True))
        outs.append(_dot(p.astype(BF16), vh) / jnp.sum(p, axis=-1, keepdims=True))
    o_ref[0] = x + _dot(jnp.concatenate(outs, axis=-1).astype(BF16), wo_ref[...])


def mix_out_mem_attention(x, ya, yb, wa, wb, mem_kv, g, wq, gq, gk, wo, *, tm=512):
    b, s, d = x.shape
    m = mem_kv.shape[1]
    tm = min(tm, s)
    const = lambda shape: pl.BlockSpec(shape, lambda i, t: (0,) * len(shape), pipeline_mode=pl.Buffered(1))
    tile = lambda w: pl.BlockSpec((1, tm, w), lambda i, t: (i, t, 0))
    return pl.pallas_call(
        _memx_kernel,
        grid=(b, s // tm),
        in_specs=[tile(d), tile(ya.shape[2]), tile(yb.shape[2]), const(wa.shape), const(wb.shape),
                  const((1, d)), const((d, MEM_W)),
                  pl.BlockSpec((1, m, 2 * MEM_W), lambda i, t: (i, 0, 0)),
                  const((1, MEM_HEAD_DIM)), const((1, MEM_HEAD_DIM)), const((MEM_W, d))],
        out_specs=tile(d),
        out_shape=jax.ShapeDtypeStruct((b, s, d), F32),
        compiler_params=_params("parallel", "arbitrary"),
        name="mix_out_mem_attention",
    )(x, ya, yb, wa, wb, g.reshape(1, d).astype(F32), wq, mem_kv, gq.reshape(1, -1).astype(F32),
      gk.reshape(1, -1).astype(F32), wo)


def _rope_tables(positions, dim, lead_ones, tail):
    inv = 1.0 / (ROPE_THETA ** (jnp.arange(0, dim, 2, dtype=F32) / dim))
    ang = positions.astype(F32)[..., None] * inv
    c, s = jnp.cos(ang), jnp.sin(ang)
    shape = positions.shape
    cos = jnp.concatenate([jnp.ones(shape + (lead_ones,), F32), c, c, jnp.ones(shape + (tail,), F32)], axis=-1)
    sin = jnp.concatenate([jnp.zeros(shape + (lead_ones,), F32), -s, s, jnp.zeros(shape + (tail,), F32)], axis=-1)
    reps = LANES // cos.shape[-1]
    return jnp.tile(cos, (1, 1, reps)), jnp.tile(sin, (1, 1, reps))


def _pad_cols(w, cols):
    return jnp.pad(w, ((0, 0), (0, cols - w.shape[1])))


IN_TILE = 512


def _ab_in_segments(w_all, j):
    c = (SWA_HEADS + 2 * SWA_KV_HEADS) * HEAD_DIM + 3 * RWKV_DIM
    w = w_all[j]
    lora = jnp.concatenate([_pad_cols(w[:, c:c + DECAY_LORA], LANES),
                            _pad_cols(w[:, c + DECAY_LORA:c + DECAY_LORA + AAA_LORA], LANES),
                            _pad_cols(w[:, c + DECAY_LORA + AAA_LORA:], 2 * LANES)], axis=1).astype(BF16)
    return [(w_all, j, 0, c // IN_TILE), (lora, None, 0, 1)]


def _cd_in_segments(w_all, j):
    c1 = MLA_Q_RANK + MLA_KV_RANK
    w = w_all[j]
    z = lambda n: jnp.zeros((w.shape[0], n), w.dtype)
    mid = jnp.concatenate([w[:, MLA_Q_RANK:c1], z(MLA_NOPE), w[:, c1:c1 + MLA_ROPE],
                           z(LANES - MLA_NOPE - MLA_ROPE), z(LANES)], axis=1).astype(BF16)
    rest = w[:, c1 + MLA_ROPE:].astype(BF16)
    return [(w_all, j, 0, MLA_Q_RANK // IN_TILE), (mid, None, 0, 1), (rest, None, 0, rest.shape[1] // IN_TILE)]


def _head_slabs(w, per_head):
    k = w.shape[0]
    return jnp.pad(w.reshape(k, -1, per_head), ((0, 0), (0, 0), (0, LANES - per_head))).reshape(k, -1)


def _slab_vec(*parts):
    v = jnp.concatenate([p.astype(F32) for p in parts])
    return jnp.pad(v, (0, LANES - v.shape[0])).reshape(1, LANES)


def kernel(x, mem, positions, ffn1_norm, ffn1_w_gate, ffn1_w_up, ffn1_w_down, mix_norm, ab_w_in, ab_w_out, swa_q_norm, swa_k_norm, swa_sinks, rwkv_mu, rwkv_w0, rwkv_w2, rwkv_a0, rwkv_a2, rwkv_g2, rwkv_k_k, rwkv_k_a, rwkv_r_k, rwkv_gn_g, rwkv_gn_b, cd_w_in, cd_w_out, mla_cq_norm, mla_ckv_norm, mla_w_uq, mla_w_ukv, mla_q_nope_norm, mla_k_nope_norm, mla_q_rope_norm, mla_k_rope_norm, diff_q_norm, diff_k_norm, diff_lq1, diff_lk1, diff_lq2, diff_lk2, diff_subln, memx_norm, memx_w_q, memx_q_norm, memx_w_o, mem_norm, mem_w_kv, mem_k_norm, ffn2_norm, ffn2_w_gate, ffn2_w_up, ffn2_w_down):
    b, s, d = x.shape
    m = mem.shape[1]
    t = b * s
    depth = ffn1_norm.shape[0]
    bf = lambda w: w.astype(BF16)
    cos64, sin64 = _rope_tables(positions, HEAD_DIM, 0, 0)
    cos32, sin32 = _rope_tables(positions, MLA_ROPE, MLA_NOPE, LANES - MLA_NOPE - MLA_ROPE)

    mem_kv = norm_matmul(mem.reshape(b * m, d), mem_norm, [(mem_w_kv, None, 0, 2 * MEM_W // IN_TILE)],
                         tn=IN_TILE, name="mem_kv").reshape(b, m, 2 * MEM_W)

    x = x.reshape(t, d)
    for layer in range(depth):
        j = layer // 2
        x = ffn(x, ffn1_norm[layer], ffn1_w_gate, ffn1_w_up, ffn1_w_down, layer, name="ffn1")
        if layer % 2 == 0:
            u = norm_matmul(x, mix_norm[layer], _ab_in_segments(ab_w_in, j), tn=IN_TILE, name="ab_in")
            u = u.reshape(b, s, AB_IN_PAD)
            y_a = swa_attention(u, cos64, sin64, _slab_vec(swa_q_norm[j], swa_q_norm[j]),
                                _slab_vec(swa_k_norm[j], swa_k_norm[j]), swa_sinks[j])
            y_b = rwkv_mix(u, rwkv_mu[j], rwkv_w0[j], rwkv_w2[j], rwkv_a0[j], rwkv_a2[j], rwkv_g2[j],
                           rwkv_k_k[j], rwkv_k_a[j], rwkv_r_k[j], rwkv_gn_g[j], rwkv_gn_b[j])
            mixed, w_out, split = (y_a, y_b), bf(ab_w_out[j]), SWA_HEADS * HEAD_DIM
        else:
            u = norm_matmul(x, mix_norm[layer], _cd_in_segments(cd_w_in, j), tn=IN_TILE, name="cd_in")
            u = u.reshape(b, s, CD_IN_PAD)
            zero64 = jnp.zeros((MLA_NOPE,), F32)
            q_c, k_c, vt_c = mla_prep(u, cos32, sin32,
                                      _slab_vec(mla_q_nope_norm[j], mla_q_rope_norm[j]),
                                      _slab_vec(mla_k_nope_norm[j]),
                                      _slab_vec(zero64, mla_k_rope_norm[j]),
                                      mla_cq_norm[j], mla_ckv_norm[j],
                                      bf(_head_slabs(mla_w_uq[j], MLA_NOPE + MLA_ROPE)), bf(mla_w_ukv[j]))
            y_c = causal_attention(q_c, k_c, vt_c, n_sm=1, hps=8, name="mla_attention")
            q_d, k_d, vt_d = diff_prep(u, cos64, sin64, _slab_vec(diff_q_norm[j], diff_q_norm[j]),
                                       _slab_vec(diff_k_norm[j], diff_k_norm[j]))
            lambda_init = 0.8 - 0.6 * math.exp(-0.3 * layer)
            lam = (jnp.exp(jnp.sum(diff_lq1[j].astype(F32) * diff_lk1[j].astype(F32)))
                   - jnp.exp(jnp.sum(diff_lq2[j].astype(F32) * diff_lk2[j].astype(F32))) + lambda_init)
            y_d = causal_attention(q_d, k_d, vt_d, n_sm=2, hps=4, lam=lam, g=diff_subln[j],
                                   out_scale=1.0 - lambda_init, name="diff_attention")
            mixed, w_out, split = (y_c, y_d), bf(cd_w_out[j]), MLA_HEADS * MLA_V
        x = mix_out_mem_attention(x.reshape(b, s, d), mixed[0], mixed[1], w_out[:split], w_out[split:], mem_kv,
                                  memx_norm[layer], bf(memx_w_q[layer]), memx_q_norm[layer], mem_k_norm,
                                  bf(memx_w_o[layer])).reshape(t, d)
        x = ffn(x, ffn2_norm[layer], ffn2_w_gate, ffn2_w_up, ffn2_w_down, layer, name="ffn2")
    return x.reshape(b, s, d)
```

```python
import functools
import math

import jax
import jax.numpy as jnp
from jax import lax
from jax.experimental import pallas as pl
from jax.experimental.pallas import tpu as pltpu

F32 = jnp.float32
BF16 = jnp.bfloat16

EPS = 1e-6
ROPE_THETA = 10000.0
NEG_INF = -1e30
LOG2E = math.log2(math.e)
ATTN_LOOKAHEAD = 8
ATTN_TILE = 256
SEG_W = 256
LANES = 128

D_MODEL = 2048
D_FF = 5632
HEAD_DIM = 64
SWA_HEADS = 16
SWA_KV_HEADS = 4
SWA_BLOCK = 128
RWKV_DIM = 1024
RWKV_CHUNK = 64
RWKV_GN_EPS = 64e-5
DECAY_LORA, AAA_LORA, GATE_LORA = 64, 64, 160
MLA_HEADS, MLA_Q_RANK, MLA_KV_RANK, MLA_NOPE, MLA_ROPE, MLA_V = 16, 512, 256, 64, 32, 64
DIFF_HEADS, DIFF_QK, DIFF_V = 8, 64, 128
MEM_HEADS, MEM_HEAD_DIM = 4, 128
MEM_W = MEM_HEADS * MEM_HEAD_DIM
AB_IN_PAD = 5120
CD_IN_PAD = 4096

VMEM_LIMIT = 48 * 1024 * 1024
FFN_VMEM_LIMIT = 60 * 1024 * 1024


def _params(*sem):
    return pltpu.CompilerParams(dimension_semantics=sem, vmem_limit_bytes=VMEM_LIMIT)


def _dot(a, b):
    return jnp.dot(a, b, preferred_element_type=F32)


def _dot_t(a, b):
    return lax.dot_general(a, b, (((1,), (1,)), ((), ())), preferred_element_type=F32)


def _dot_0(a, b):
    return lax.dot_general(a, b, (((0,), (0,)), ((), ())), preferred_element_type=F32)


def _rms(x, g):
    return x * lax.rsqrt(jnp.mean(x * x, axis=-1, keepdims=True) + EPS) * g


def _seg_sum(x, e):
    xb = x.astype(BF16)
    w = e.shape[0]
    parts = [_dot(xb[:, i:i + w], e) for i in range(0, x.shape[1], w)]
    return parts[0] if len(parts) == 1 else jnp.concatenate(parts, axis=-1)


def _tile_lanes(v, width):
    return v if v.shape[-1] == width else jnp.tile(v, (1, width // v.shape[-1]))


def _store_transposed_tiles(vt_ref, v):
    for r in range(vt_ref.shape[1]):
        vt_ref[0, r] = v[r * ATTN_TILE:(r + 1) * ATTN_TILE, :].T.astype(vt_ref.dtype)


def _swap_halves(x, half):
    w = x.shape[-1]
    lane = lax.broadcasted_iota(jnp.int32, x.shape, x.ndim - 1)
    low = (lane & (2 * half - 1)) < half
    return jnp.where(low, pltpu.roll(x, w - half, x.ndim - 1), pltpu.roll(x, half, x.ndim - 1))


def _norm_matmul_kernel(x_ref, g_ref, *refs, starts, counts):
    w_refs, o_ref, xn_ref = refs[:len(starts)], refs[len(starts)], refs[len(starts) + 1]
    j = pl.program_id(1)

    @pl.when(j == 0)
    def _():
        xn_ref[...] = _rms(x_ref[...], g_ref[...]).astype(BF16)

    for w_ref, start, count in zip(w_refs, starts, counts):
        @pl.when((j >= start) & (j < start + count))
        def _(w_ref=w_ref):
            o_ref[...] = _dot_t(xn_ref[...], w_ref[...])


def norm_matmul(x, g, segments, *, tm=1024, tn=512, name="norm_matmul"):
    t, k = x.shape
    tm = min(tm, t)
    assert t % tm == 0 and all(w.shape[0] % tn == 0 and w.shape[1] == k for w in segments)
    counts = [w.shape[0] // tn for w in segments]
    starts = [sum(counts[:i]) for i in range(len(counts))]
    w_spec = lambda start, count: pl.BlockSpec((tn, k), lambda i, j: (jnp.clip(j - start, 0, count - 1), 0))
    return pl.pallas_call(
        functools.partial(_norm_matmul_kernel, starts=tuple(starts), counts=tuple(counts)),
        grid=(t // tm, sum(counts)),
        in_specs=[pl.BlockSpec((tm, k), lambda i, j: (i, 0)),
                  pl.BlockSpec((1, k), lambda i, j: (0, 0))]
                 + [w_spec(start, count) for start, count in zip(starts, counts)],
        out_specs=pl.BlockSpec((tm, tn), lambda i, j: (i, j)),
        out_shape=jax.ShapeDtypeStruct((t, sum(counts) * tn), F32),
        scratch_shapes=[pltpu.VMEM((tm, k), BF16)],
        compiler_params=_params("parallel", "arbitrary"),
        name=name,
    )(x, g.reshape(1, k).astype(F32), *segments)


def _ffn_kernel(x_ref, g_ref, wg_ref, wu_ref, wd_ref, o_ref, xn_ref):
    @pl.when(pl.program_id(1) == 0)
    def _():
        x = x_ref[...]
        xn_ref[...] = _rms(x, g_ref[...]).astype(BF16)
        o_ref[...] = x

    xn = xn_ref[...]
    a = _dot(xn, wg_ref[...].astype(BF16))
    b = _dot(xn, wu_ref[...].astype(BF16))
    h = (a * (0.5 / (1.0 + jnp.exp(-a))) * b).astype(BF16)
    o_ref[...] += _dot(h, wd_ref[...].astype(BF16))


def ffn(x, g, wg, wu, wd, layer, *, tm=1024, tf=256, name="ffn"):
    t, d = x.shape
    ff = wg.shape[2]
    tm = min(tm, t)
    assert t % tm == 0 and ff % tf == 0
    nf = ff // tf
    return pl.pallas_call(
        _ffn_kernel,
        grid=(t // tm, nf),
        in_specs=[pl.BlockSpec((tm, d), lambda i, f: (i, 0)),
                  pl.BlockSpec((1, d), lambda i, f: (0, 0)),
                  pl.BlockSpec((None, d, tf), lambda i, f: (layer, 0, f)),
                  pl.BlockSpec((None, d, tf), lambda i, f: (layer, 0, f)),
                  pl.BlockSpec((None, tf, d), lambda i, f: (layer, f, 0))],
        out_specs=pl.BlockSpec((tm, d), lambda i, f: (i, 0)),
        out_shape=jax.ShapeDtypeStruct((t, d), F32),
        scratch_shapes=[pltpu.VMEM((tm, d), BF16)],
        compiler_params=pltpu.CompilerParams(dimension_semantics=("parallel", "arbitrary"),
                                             vmem_limit_bytes=FFN_VMEM_LIMIT),
        name=name,
    )(x, g.reshape(1, d).astype(F32), wg, wu, wd)


def _swa_kernel(sink_ref, q_ref, kc_ref, kp_ref, vc_ref, vp_ref, cc_ref, sc_ref, cp_ref, sp_ref,
                gq_ref, gk_ref, e_ref, o_ref):
    n = pl.program_id(1)
    blk = SWA_BLOCK
    group = SWA_HEADS // SWA_KV_HEADS
    q = q_ref[0]
    k = jnp.concatenate([kp_ref[0], kc_ref[0]], axis=0)
    v = jnp.concatenate([vp_ref[0], vc_ref[0]], axis=0)
    cos_q, sin_q = cc_ref[0], sc_ref[0]
    cos_k = jnp.concatenate([cp_ref[0], cos_q], axis=0)
    sin_k = jnp.concatenate([sp_ref[0], sin_q], axis=0)
    def prep(x, g, cos, sin, scale):
        w = x.shape[1]
        xg = x * _tile_lanes(g, w)
        xr = xg * _tile_lanes(cos, w) + _swap_halves(xg, HEAD_DIM // 2) * _tile_lanes(sin, w)
        return xr * lax.rsqrt(_seg_sum(x * x, e_ref[...]) * (1.0 / HEAD_DIM) + EPS) * scale

    qr = prep(q, gq_ref[...], cos_q, sin_q, HEAD_DIM ** -0.5 * LOG2E).astype(BF16)
    kr = prep(k, gk_ref[...], cos_k, sin_k, 1.0)
    key_i = lax.broadcasted_iota(jnp.int32, (2 * blk, blk), 0)
    qry_i = lax.broadcasted_iota(jnp.int32, (2 * blk, blk), 1)
    rel = qry_i + blk - key_i
    valid = (rel >= 0) & (rel < blk) & ((n > 0) | (key_i >= blk))
    low = lax.broadcasted_iota(jnp.int32, (blk, LANES), 1) < HEAD_DIM
    zero = jnp.zeros((blk, LANES), BF16)
    k_dup, v_t = [], []
    for g in range(SWA_KV_HEADS):
        kg = kr[:, g * HEAD_DIM:(g + 1) * HEAD_DIM]
        k_dup.append(jnp.concatenate([kg, kg], axis=-1).astype(BF16))
        v_t.append(v[:, g * HEAD_DIM:(g + 1) * HEAD_DIM].T.astype(BF16))

    def scores(h):
        slab = qr[:, (h // 2) * LANES:(h // 2 + 1) * LANES]
        qh = jnp.where(low, slab, zero) if h % 2 == 0 else jnp.where(low, zero, slab)
        return _dot_t(k_dup[h // group], qh)

    ahead = [scores(h) for h in range(ATTN_LOOKAHEAD)]
    ones = jnp.ones((8, 2 * blk), BF16)
    outs = []
    for h in range(SWA_HEADS):
        s = jnp.where(valid, ahead.pop(0), NEG_INF)
        if h + ATTN_LOOKAHEAD < SWA_HEADS:
            ahead.append(scores(h + ATTN_LOOKAHEAD))
        sink = sink_ref[h] * LOG2E
        m = jnp.maximum(jnp.max(s, axis=0, keepdims=True), sink)
        p = jnp.exp2(s - m).astype(BF16)
        den = _dot(ones, p)[0:1] + jnp.exp2(sink - m)
        outs.append(_dot(v_t[h // group], p) / den)
    slabs = [jnp.concatenate(outs[i:i + 2], axis=0).T for i in range(0, SWA_HEADS, 2)]
    o_ref[0] = jnp.concatenate(slabs, axis=-1).astype(o_ref.dtype)


def swa_attention(u, cos, sin, gq, gk, sinks):
    b, s, _ = u.shape
    nb = s // SWA_BLOCK
    qw, kw = SWA_HEADS * HEAD_DIM, SWA_KV_HEADS * HEAD_DIM
    cur = lambda c: (lambda i, n: (i, n, c))
    prev = lambda c: (lambda i, n: (i, jnp.maximum(n - 1, 0), c))
    tab = pl.BlockSpec((1, SWA_BLOCK, LANES), cur(0))
    tab_prev = pl.BlockSpec((1, SWA_BLOCK, LANES), prev(0))
    gain = pl.BlockSpec((1, LANES), lambda i, n: (0, 0))
    seg = (jnp.arange(kw)[:, None] // HEAD_DIM == jnp.arange(kw)[None, :] // HEAD_DIM).astype(BF16)
    return pl.pallas_call(
        _swa_kernel,
        grid=(b, nb),
        in_specs=[pl.BlockSpec(memory_space=pltpu.SMEM),
                  pl.BlockSpec((1, SWA_BLOCK, qw), cur(0)),
                  pl.BlockSpec((1, SWA_BLOCK, kw), cur(qw // kw)),
                  pl.BlockSpec((1, SWA_BLOCK, kw), prev(qw // kw)),
                  pl.BlockSpec((1, SWA_BLOCK, kw), cur(qw // kw + 1)),
                  pl.BlockSpec((1, SWA_BLOCK, kw), prev(qw // kw + 1)),
                  tab, tab, tab_prev, tab_prev, gain, gain, pl.BlockSpec((kw, kw), lambda i, n: (0, 0))],
        out_specs=pl.BlockSpec((1, SWA_BLOCK, qw), cur(0)),
        out_shape=jax.ShapeDtypeStruct((b, s, qw), BF16),
        compiler_params=_params("parallel", "arbitrary"),
        name="swa_attention",
    )(sinks.astype(F32), u, u, u, u, u, cos, sin, cos, sin, gq, gk, seg)


def _mm(a, b, dims, passes):
    dn = (dims, ((), ()))
    dg = lambda x, y: lax.dot_general(x, y, dn, preferred_element_type=F32)
    ah = a.astype(BF16)
    bh = b.astype(BF16)
    if passes == 1:
        return dg(ah, bh)
    al = (a - ah.astype(F32)).astype(BF16)
    bl = (b - bh.astype(F32)).astype(BF16)
    return dg(ah, bh) + dg(ah, bl) + dg(al, bh)


_NN = ((1,), (0,))
_NT = ((1,), (1,))
_TN = ((0,), (0,))
P_SC, P_INV, P_PQ, P_OUT, P_ST = 1, 1, 1, 1, 1
RWKV_UNROLL = 8


RWKV_PREP_ROWS = 256


def _rwkv_kernel(ur_ref, uk_ref, uv_ref, ul_ref, mur_ref, muk_ref, muv_ref, mul_ref, w0_ref, w2_ref, a0_ref,
                 a2_ref, g2_ref, kkw_ref, ka_ref, gng_ref, gnb_ref, rk_ref,
                 o_ref, r_ref, k_ref, v_ref, kk_ref, b_ref, lw_ref, g_ref,
                 st_ref, y1_ref, y0_ref, n_ref, z_ref, dec_ref, *, nchunk):
    c = RWKV_CHUNK
    lane_c = lax.broadcasted_iota(jnp.int32, (c, LANES), 1)
    head0 = lane_c < HEAD_DIM
    ri = lax.broadcasted_iota(jnp.int32, (2 * c, 2 * c), 0)
    ci = lax.broadcasted_iota(jnp.int32, (2 * c, 2 * c), 1)
    eye = jnp.where(ri == ci, 1.0, 0.0)
    tril_c = jnp.where(lax.broadcasted_iota(jnp.int32, (c, c), 0) >= lax.broadcasted_iota(jnp.int32, (c, c), 1),
                       1.0, 0.0).astype(BF16)
    stack = lambda x: jnp.concatenate([jnp.where(head0, x, 0.0), jnp.where(head0, 0.0, x)], axis=0)

    def seg_mean(x):
        first = lax.broadcasted_iota(jnp.int32, x.shape, 1) < HEAD_DIM
        m0 = jnp.sum(jnp.where(first, x, 0.0), axis=-1, keepdims=True)
        m1 = jnp.sum(jnp.where(first, 0.0, x), axis=-1, keepdims=True)
        return jnp.where(first, m0, m1) * (1.0 / HEAD_DIM)

    def prep(io, first_group):
        group_rows = RWKV_UNROLL * c
        for tix in range(group_rows // RWKV_PREP_ROWS):
            start = pl.multiple_of(io * group_rows + tix * RWKV_PREP_ROWS, RWKV_PREP_ROWS)
            rows = pl.ds(start, RWKV_PREP_ROWS)
            at_start = first_group and tix == 0

            def shifted(ref, mu_ref):
                x = ref[0, rows, :]
                if at_start:
                    last = jnp.zeros((1, x.shape[1]), F32)
                else:
                    last = ref[0, pl.ds(pl.multiple_of(start - 8, 8), 8), :][7:8, :]
                row = lax.broadcasted_iota(jnp.int32, x.shape, 0)
                prev = jnp.where(row == 0, last, pltpu.roll(x, 1, 0))
                return x + (prev - x) * mu_ref[...]

            r = shifted(ur_ref, mur_ref)
            k = shifted(uk_ref, muk_ref)
            v = shifted(uv_ref, muv_ref)
            lo = shifted(ul_ref, mul_ref)
            yield
            w_lo, a_lo, g_lo = lo[:, 0:LANES], lo[:, LANES:2 * LANES], lo[:, 2 * LANES:4 * LANES]
            z = -(w0_ref[...] + _mm(jnp.tanh(w_lo), w2_ref[...], _NN, 3))
            w = -(jnp.maximum(z, 0.0) + jnp.log(1.0 + jnp.exp(-jnp.abs(z)))) - 0.5
            a = 1.0 / (1.0 + jnp.exp(-(a0_ref[...] + _mm(a_lo, a2_ref[...], _NN, 1))))
            g = _mm(1.0 / (1.0 + jnp.exp(-g_lo)), g2_ref[...], _NN, 1)
            yield
            kk = k * kkw_ref[...]
            kk = kk / jnp.maximum(jnp.sqrt(seg_mean(kk * kk) * float(HEAD_DIM)), 1e-12)
            r_ref[rows, :] = r
            k_ref[rows, :] = k * (1.0 + (a - 1.0) * ka_ref[...])
            v_ref[rows, :] = v
            kk_ref[rows, :] = kk
            b_ref[rows, :] = kk * a
            lw_ref[rows, :] = -jnp.exp(w)
            g_ref[rows, :] = g
            yield

    def build(ics):
        each = lambda f, *cols: [f(*args) for args in zip(*cols)]
        sls = [pl.ds(pl.multiple_of(ic * c, c), c) for ic in ics]
        load = lambda ref: [ref[sl, :] for sl in sls]
        r, k, v, kk, b, lw = (load(ref) for ref in (r_ref, k_ref, v_ref, kk_ref, b_ref, lw_ref))

        def running_sum(x):
            l1 = x.astype(BF16)
            rest = x - l1.astype(F32)
            l2 = rest.astype(BF16)
            l3 = (rest - l2.astype(F32)).astype(BF16)
            return _dot(tril_c, l1) + _dot(tril_c, l2) + _dot(tril_c, l3)

        cum = each(running_sum, lw)
        yield
        cum_end = [x[c - 1:c, :] for x in cum]
        e_neg = each(lambda x: jnp.exp(-x), cum)
        e_end = each(lambda x, xe: jnp.exp(xe - x), cum, cum_end)
        a_s = each(lambda kk_, x, l: stack(-kk_ * jnp.exp(x - l)), kk, cum, lw)
        r_s = each(lambda r_, x: stack(r_ * jnp.exp(x)), r, cum)
        b_s = each(lambda b_, e: stack(b_ * e), b, e_neg)
        k_s = each(lambda k_, e: stack(k_ * e), k, e_neg)
        bh_s = each(lambda b_, e: stack(b_ * e), b, e_end)
        kh_s = each(lambda k_, e: stack(k_ * e), k, e_end)
        v_s = each(stack, v)
        n2 = 2 * c
        sc = each(lambda a_, r_, b_, k_: _mm(jnp.concatenate([a_, r_], axis=0),
                                             jnp.concatenate([b_, k_], axis=0), _NT, P_SC), a_s, r_s, b_s, k_s)
        low = [jnp.where(ri > ci, x[:n2, :n2], 0.0) for x in sc]
        a_ak = [jnp.where(ri > ci, x[:n2, n2:], 0.0) for x in sc]
        a_rb = [jnp.where(ri >= ci, x[n2:, :n2], 0.0) for x in sc]
        a_rk = [jnp.where(ri >= ci, x[n2:, n2:], 0.0) for x in sc]
        yield
        inv = [eye + x for x in low]
        pw = each(lambda x: _mm(x, x, _NN, P_INV), low)
        yield
        levels = 5
        for lvl in range(levels - 1):
            both = each(lambda t, x: _mm(jnp.concatenate([t, x], axis=0), x, _NN, P_INV), inv, pw)
            inv = each(lambda t, r: t + r[:n2], inv, both)
            pw = [r[n2:] for r in both]
            yield
        inv = each(lambda t, x: t + _mm(t, x, _NN, P_INV), inv, pw)
        yield
        akv = each(lambda x, y: _mm(x, y, _NN, P_PQ), a_ak, v_s)
        yield
        pq = each(lambda t, x, y: _mm(t, jnp.concatenate([x, y], axis=1), _NN, P_PQ), inv, a_s, akv)
        yield
        yy = each(lambda x, y: _mm(x, y, _NN, P_OUT), a_rb, pq)
        y0b = each(lambda x, y: _mm(x, y, _NN, P_OUT), a_rk, v_s)
        nz = each(lambda x, y: _mm(x, y, _TN, P_OUT), pq, bh_s)
        zb = each(lambda x, y: _mm(x, y, _TN, P_OUT), v_s, kh_s)
        for i, ic in enumerate(ics):
            y1_ref[ic] = r_s[i] + yy[i][:, :LANES]
            y0_ref[ic] = yy[i][:, LANES:] + y0b[i]
            n_ref[ic] = nz[i][:LANES]
            z_ref[ic] = nz[i][LANES:] + zb[i]
            dec_ref[ic] = jnp.broadcast_to(jnp.exp(cum_end[i]), (8, LANES))

    def emit(ic, st):
        sl = pl.ds(pl.multiple_of(ic * c, c), c)
        r, k, v, g = r_ref[sl, :], k_ref[sl, :], v_ref[sl, :], g_ref[sl, :]
        y_st = _mm(y1_ref[ic], st, _NT, P_ST) + y0_ref[ic]
        y = y_st[0:c] + y_st[c:2 * c]
        mean = seg_mean(y)
        var = seg_mean((y - mean) * (y - mean))
        yn = (y - mean) * lax.rsqrt(var + RWKV_GN_EPS) * gng_ref[...] + gnb_ref[...]
        bonus = seg_mean(r * k * rk_ref[...]) * float(HEAD_DIM) * v
        o_ref[0, sl, :] = ((yn + bonus) * g).astype(o_ref.dtype)
        return st * dec_ref[ic][0:1, :] + _mm(st, n_ref[ic], _NN, P_ST) + z_ref[ic]

    def scan(ics):
        st = st_ref[...]
        for ic in ics:
            st = emit(ic, st)
            yield
        st_ref[...] = st

    def run(*gens):
        live = list(gens)
        while live:
            live = [gen for gen in live if next(gen, live) is not live]

    group = lambda io: [io * RWKV_UNROLL + i for i in range(RWKV_UNROLL)]
    ngroup = nchunk // RWKV_UNROLL
    st_ref[...] = jnp.zeros_like(st_ref)
    run(prep(0, True))
    if ngroup > 1:
        run(prep(1, False), build(group(0)))
    else:
        run(build(group(0)))

    @pl.loop(1, ngroup - 1)
    def _(io):
        run(prep(io + 1, False), build(group(io)), scan(group(io - 1)))

    if ngroup > 1:
        run(build(group(ngroup - 1)), scan(group(ngroup - 2)))
    run(scan(group(ngroup - 1)))


def rwkv_mix(u, mu, w0, w2, a0, a2, g2, k_k, k_a, r_k, gn_g, gn_b):
    bsz, s, _ = u.shape
    npair = RWKV_DIM // LANES
    nchunk = s // RWKV_CHUNK
    assert nchunk % RWKV_UNROLL == 0 and (RWKV_UNROLL * RWKV_CHUNK) % RWKV_PREP_ROWS == 0
    base = (SWA_HEADS + 2 * SWA_KV_HEADS) * HEAD_DIM // LANES
    lora_w = 4 * LANES
    slab = lambda off: pl.BlockSpec((1, s, LANES), lambda i, p: (i, 0, off + p))
    vec = pl.BlockSpec((1, LANES), lambda i, p: (0, p))
    cols = lambda rows: pl.BlockSpec((rows, LANES), lambda i, p: (0, p))
    row = lambda vv: vv.reshape(1, -1).astype(F32)
    pad_rows = lambda m, rows: jnp.pad(m, ((0, rows - m.shape[0]), (0, 0))).astype(F32)
    pad_cols = lambda vv, n: jnp.pad(vv, (0, n - vv.shape[0]))
    c3 = 3 * RWKV_DIM
    mu_l = jnp.concatenate([pad_cols(mu[c3:c3 + DECAY_LORA], LANES),
                            pad_cols(mu[c3 + DECAY_LORA:c3 + DECAY_LORA + AAA_LORA], LANES),
                            pad_cols(mu[c3 + DECAY_LORA + AAA_LORA:], 2 * LANES)])
    seq = pltpu.VMEM((s, LANES), F32)
    mat = pltpu.VMEM((nchunk, LANES, LANES), F32)
    return pl.pallas_call(
        functools.partial(_rwkv_kernel, nchunk=nchunk),
        grid=(bsz, npair),
        in_specs=[slab(base), slab(base + npair), slab(base + 2 * npair),
                  pl.BlockSpec((1, s, lora_w), lambda i, p: (i, 0, (base + 3 * npair) * LANES // lora_w)),
                  vec, vec, vec, pl.BlockSpec((1, lora_w), lambda i, p: (0, 0)),
                  vec, cols(LANES), vec, cols(LANES), cols(2 * LANES), vec, vec, vec, vec, vec],
        out_specs=pl.BlockSpec((1, s, LANES), lambda i, p: (i, 0, p)),
        out_shape=jax.ShapeDtypeStruct((bsz, s, RWKV_DIM), BF16),
        scratch_shapes=[seq] * 7 + [pltpu.VMEM((LANES, LANES), F32), mat, mat, mat, mat,
                                    pltpu.VMEM((nchunk, 8, LANES), F32)],
        compiler_params=_params("parallel", "arbitrary"),
        name="rwkv_mix",
    )(u, u, u, u, row(mu[:RWKV_DIM]), row(mu[RWKV_DIM:2 * RWKV_DIM]), row(mu[2 * RWKV_DIM:c3]), row(mu_l),
      row(w0), pad_rows(w2, LANES), row(a0), pad_rows(a2, LANES), pad_rows(g2, 2 * LANES),
      row(k_k), row(k_a), row(gn_g), row(gn_b), row(r_k))


def _mla_prep_kernel(cq_ref, ckv_ref, pe_ref, cos_ref, sin_ref, e_ref, gq_ref, gkn_ref, gkp_ref, invn_ref,
                     gcq_ref, gckv_ref, wuq_ref, wukv_ref, qo_ref, ko_ref, vo_ref, cqn_ref, ckvn_ref, kpe_ref):
    cos, sin = cos_ref[0], sin_ref[0]
    half = MLA_ROPE // 2

    def rope(x, g):
        w = x.shape[1]
        xg = x * _tile_lanes(g, w)
        return xg * _tile_lanes(cos, w) + _swap_halves(xg, half) * _tile_lanes(sin, w)

    @pl.when(pl.program_id(2) == 0)
    def _():
        cqn_ref[...] = _rms(cq_ref[0], gcq_ref[...]).astype(BF16)
        ckvn_ref[...] = _rms(ckv_ref[0], gckv_ref[...]).astype(BF16)
        pe = pe_ref[0]
        inv_pe = lax.rsqrt(jnp.sum(pe * pe, axis=-1, keepdims=True) * (1.0 / MLA_ROPE) + EPS)
        kpe_ref[...] = rope(pe, gkp_ref[...]) * inv_pe

    x = _dot(cqn_ref[...], wuq_ref[...])
    w = x.shape[1]
    inv_n = _tile_lanes(invn_ref[...], w)
    inv = lax.rsqrt(_seg_sum(x * x, e_ref[...]) * inv_n + EPS)
    qo_ref[0] = (rope(x, gq_ref[...]) * inv * ((MLA_NOPE + MLA_ROPE) ** -0.5 * LOG2E)).astype(qo_ref.dtype)
    kv = _dot(ckvn_ref[...], wukv_ref[...])
    inv_k = lax.rsqrt(_seg_sum(kv * kv, e_ref[...]) * inv_n + EPS)
    k_nope = kv * inv_k * _tile_lanes(gkn_ref[...], w)
    ko_ref[0] = (k_nope + _tile_lanes(kpe_ref[...], w)).astype(ko_ref.dtype)
    v = jnp.concatenate([kv[:, h * LANES + MLA_NOPE:(h + 1) * LANES] for h in range(w // LANES)], axis=-1)
    _store_transposed_tiles(vo_ref, v)


def mla_prep(u, cos, sin, gq, gkn, gkp, gcq, gckv, w_uq, w_ukv, *, ts=512, tc=512):
    b, s, _ = u.shape
    wtot = w_uq.shape[1]
    ts = min(ts, s)
    lane = jnp.arange(SEG_W)
    same = (lane[:, None] // LANES == lane[None, :] // LANES)
    pos = lane % LANES
    nope = pos < MLA_NOPE
    pe = (pos >= MLA_NOPE) & (pos < MLA_NOPE + MLA_ROPE)
    seg = (same & ((nope[:, None] & nope[None, :]) | (pe[:, None] & pe[None, :]))).astype(BF16)
    p1 = jnp.arange(LANES)
    inv_n = jnp.where(p1 < MLA_NOPE, 1.0 / MLA_NOPE, jnp.where(p1 < MLA_NOPE + MLA_ROPE, 1.0 / MLA_ROPE, 0.0))
    blk = pl.BlockSpec((1, ts, tc), lambda i, t, c: (i, t, c))
    tab = pl.BlockSpec((1, ts, LANES), lambda i, t, c: (i, t, 0))
    vec = pl.BlockSpec((1, LANES), lambda i, t, c: (0, 0))
    pe_blk = (MLA_Q_RANK + MLA_KV_RANK) // LANES
    out = jax.ShapeDtypeStruct((b, s, wtot), BF16)
    return pl.pallas_call(
        _mla_prep_kernel,
        grid=(b, s // ts, wtot // tc),
        in_specs=[pl.BlockSpec((1, ts, MLA_Q_RANK), lambda i, t, c: (i, t, 0)),
                  pl.BlockSpec((1, ts, MLA_KV_RANK), lambda i, t, c: (i, t, MLA_Q_RANK // MLA_KV_RANK)),
                  pl.BlockSpec((1, ts, LANES), lambda i, t, c: (i, t, pe_blk)), tab, tab,
                  pl.BlockSpec((SEG_W, SEG_W), lambda i, t, c: (0, 0)), vec, vec, vec, vec,
                  pl.BlockSpec((1, MLA_Q_RANK), lambda i, t, c: (0, 0)),
                  pl.BlockSpec((1, MLA_KV_RANK), lambda i, t, c: (0, 0)),
                  pl.BlockSpec((MLA_Q_RANK, tc), lambda i, t, c: (0, c)),
                  pl.BlockSpec((MLA_KV_RANK, tc), lambda i, t, c: (0, c))],
        out_specs=[blk, blk, pl.BlockSpec((1, ts // ATTN_TILE, tc // LANES * MLA_V, ATTN_TILE),
                                          lambda i, t, c: (i, t, c, 0))],
        out_shape=[out, out, jax.ShapeDtypeStruct((b, s // ATTN_TILE, wtot // LANES * MLA_V, ATTN_TILE), BF16)],
        scratch_shapes=[pltpu.VMEM((ts, MLA_Q_RANK), BF16), pltpu.VMEM((ts, MLA_KV_RANK), BF16),
                        pltpu.VMEM((ts, LANES), F32)],
        compiler_params=_params("parallel", "parallel", "arbitrary"),
        name="mla_prep",
    )(u, u, u, cos, sin, seg, gq, gkn, gkp, inv_n.reshape(1, LANES).astype(F32),
      gcq.reshape(1, -1).astype(F32), gckv.reshape(1, -1).astype(F32), w_uq, w_ukv)


def _diff_prep_kernel(q_ref, k_ref, v_ref, cos_ref, sin_ref, e_ref, gq_ref, gk_ref, qo_ref, ko_ref, vo_ref):
    cos, sin = cos_ref[0], sin_ref[0]
    _store_transposed_tiles(vo_ref, v_ref[0])

    def prep(x, g, scale):
        w = x.shape[1]
        xg = x * _tile_lanes(g, w)
        xr = xg * _tile_lanes(cos, w) + _swap_halves(xg, DIFF_QK // 2) * _tile_lanes(sin, w)
        inv = lax.rsqrt(_seg_sum(x * x, e_ref[...]) * (1.0 / DIFF_QK) + EPS)
        return xr * inv * scale

    qo_ref[0] = prep(q_ref[0], gq_ref[...], DIFF_QK ** -0.5 * LOG2E).astype(qo_ref.dtype)
    ko_ref[0] = prep(k_ref[0], gk_ref[...], 1.0).astype(ko_ref.dtype)


def diff_prep(u, cos, sin, gq, gk, *, ts=512, tc=512):
    b, s, _ = u.shape
    ts = min(ts, s)
    wtot = 2 * DIFF_HEADS * DIFF_QK
    q_base = (CD_IN_PAD - 3 * wtot) // tc
    seg = (jnp.arange(SEG_W)[:, None] // DIFF_QK == jnp.arange(SEG_W)[None, :] // DIFF_QK).astype(BF16)
    blk = lambda off: pl.BlockSpec((1, ts, tc), lambda i, t, c: (i, t, off + c))
    tab = pl.BlockSpec((1, ts, LANES), lambda i, t, c: (i, t, 0))
    vec = pl.BlockSpec((1, LANES), lambda i, t, c: (0, 0))
    out = jax.ShapeDtypeStruct((b, s, wtot), BF16)
    return pl.pallas_call(
        _diff_prep_kernel,
        grid=(b, s // ts, wtot // tc),
        in_specs=[blk(q_base), blk(q_base + wtot // tc), blk(q_base + 2 * wtot // tc), tab, tab,
                  pl.BlockSpec((SEG_W, SEG_W), lambda i, t, c: (0, 0)), vec, vec],
        out_specs=[blk(0), blk(0),
                   pl.BlockSpec((1, ts // ATTN_TILE, tc, ATTN_TILE), lambda i, t, c: (i, t, c, 0))],
        out_shape=[out, out, jax.ShapeDtypeStruct((b, s // ATTN_TILE, wtot, ATTN_TILE), BF16)],
        compiler_params=_params("parallel", "parallel", "arbitrary"),
        name="diff_prep",
    )(u, u, u, cos, sin, seg, gq, gk)


def _causal_attn_kernel(lam_ref, q_ref, k_ref, vt_ref, g_ref, o_ref, *, n_sm, tq, ow, out_scale):
    qi = pl.program_id(2)
    q = q_ref[0]
    hps = q.shape[1] // LANES
    slab = lambda x, h: x[:, h * LANES:(h + 1) * LANES]
    lane = lax.broadcasted_iota(jnp.int32, (tq, LANES), 1)
    qs, src = [], []
    for h in range(hps):
        qh = slab(q, h)
        if n_sm == 2:
            zero = jnp.zeros_like(qh)
            qs += [jnp.where(lane < DIFF_QK, qh, zero), jnp.where(lane < DIFF_QK, zero, qh)]
            src += [h, h]
        else:
            qs.append(qh)
            src.append(h)
    nch = len(qs)
    key_i = lax.broadcasted_iota(jnp.int32, (tq, tq), 0)
    qry_i = lax.broadcasted_iota(jnp.int32, (tq, tq), 1)
    ones = jnp.ones((8, tq), BF16)

    def step(j, carry, diagonal):
        kj = k_ref[0, pl.ds(pl.multiple_of(j * tq, tq), tq), :]
        scores = lambda i: _dot_t(slab(kj, src[i]), qs[i])
        new = []
        ahead = [scores(i) for i in range(min(ATTN_LOOKAHEAD, nch))]
        for i in range(nch):
            s = ahead.pop(0)
            if i + ATTN_LOOKAHEAD < nch:
                ahead.append(scores(i + ATTN_LOOKAHEAD))
            if diagonal:
                s = jnp.where(key_i <= qry_i, s, NEG_INF)
            m, l, acc = carry[3 * i:3 * i + 3]
            m_new = jnp.maximum(m, jnp.max(s, axis=0, keepdims=True))
            alpha = jnp.exp2(m - m_new)
            p = jnp.exp2(s - m_new).astype(BF16)
            new += [m_new, alpha * l + _dot(ones, p)[0:1],
                    alpha * acc + _dot(vt_ref[0, j, src[i] * ow:(src[i] + 1) * ow, :], p)]
        return tuple(new)

    init = (jnp.full((1, tq), NEG_INF, F32), jnp.zeros((1, tq), F32),
            jnp.zeros((ow, tq), F32)) * nch
    carry = lax.fori_loop(0, qi, lambda j, cr: step(j, cr, False), init)
    carry = step(qi, carry, True)
    outs = []
    for h in range(hps):
        c0 = 3 * n_sm * h
        o = carry[c0 + 2] / carry[c0 + 1]
        if n_sm == 2:
            o = o - lam_ref[0] * (carry[c0 + 5] / carry[c0 + 4])
            o = o * lax.rsqrt(jnp.mean(o * o, axis=0, keepdims=True) + EPS) * g_ref[...] * out_scale
        outs.append(o.T.astype(o_ref.dtype))
    o_ref[0] = jnp.concatenate(outs, axis=-1)


def causal_attention(q, k, vt, *, n_sm, lam=None, g=None, out_scale=1.0, hps=4, name="causal_attention"):
    b, s, wtot = q.shape
    tq = vt.shape[3]
    width = hps * LANES
    groups = wtot // width
    ow = vt.shape[2] // (wtot // LANES)
    lam = jnp.zeros((1,), F32) if lam is None else lam.reshape(1).astype(F32)
    g = jnp.ones((ow, 1), F32) if g is None else g.reshape(ow, 1).astype(F32)
    seq = pl.BlockSpec((1, s, width), lambda i, h, t: (i, 0, h))
    tile = pl.BlockSpec((1, tq, width), lambda i, h, t: (i, t, h))
    return pl.pallas_call(
        functools.partial(_causal_attn_kernel, n_sm=n_sm, tq=tq, ow=ow, out_scale=out_scale),
        grid=(b, groups, s // tq),
        in_specs=[pl.BlockSpec(memory_space=pltpu.SMEM), tile, seq,
                  pl.BlockSpec((1, s // tq, hps * ow, tq), lambda i, h, t: (i, 0, h, 0)),
                  pl.BlockSpec((ow, 1), lambda i, h, t: (0, 0))],
        out_specs=pl.BlockSpec((1, tq, hps * ow), lambda i, h, t: (i, t, h)),
        out_shape=jax.ShapeDtypeStruct((b, s, groups * hps * ow), BF16),
        compiler_params=_params("parallel", "parallel", "arbitrary"),
        name=name,
    )(lam, q, k, vt, g)


def _memx_kernel(x_ref, ya_ref, yb_ref, wa_ref, wb_ref, g_ref, wq_ref, kv_ref, gq_ref, gk_ref, wo_ref, o_ref):
    x = x_ref[0] + _dot(ya_ref[0], wa_ref[...]) + _dot(yb_ref[0], wb_ref[...])
    q = _dot(_rms(x, g_ref[...]).astype(BF16), wq_ref[...])
    kv = kv_ref[0]
    outs = []
    for h in range(MEM_HEADS):
        sl = slice(h * MEM_HEAD_DIM, (h + 1) * MEM_HEAD_DIM)
        qh = (_rms(q[:, sl], gq_ref[...]) * MEM_HEAD_DIM ** -0.5).astype(BF16)
        kh = _rms(kv[:, sl], gk_ref[...]).astype(BF16)
        vh = kv[:, MEM_W + h * MEM_HEAD_DIM:MEM_W + (h + 1) * MEM_HEAD_DIM].astype(BF16)
        s = _dot_t(qh, kh)
        p = jnp.exp(s - jnp.max(s, axis=-1, keepdims=True))
        outs.append(_dot(p.astype(BF16), vh) / jnp.sum(p, axis=-1, keepdims=True))
    o_ref[0] = x + _dot(jnp.concatenate(outs, axis=-1).astype(BF16), wo_ref[...])


def mix_out_mem_attention(x, ya, yb, wa, wb, mem_kv, g, wq, gq, gk, wo, *, tm=512):
    b, s, d = x.shape
    m = mem_kv.shape[1]
    tm = min(tm, s)
    const = lambda shape: pl.BlockSpec(shape, lambda i, t: (0,) * len(shape), pipeline_mode=pl.Buffered(1))
    tile = lambda w: pl.BlockSpec((1, tm, w), lambda i, t: (i, t, 0))
    return pl.pallas_call(
        _memx_kernel,
        grid=(b, s // tm),
        in_specs=[tile(d), tile(ya.shape[2]), tile(yb.shape[2]), const(wa.shape), const(wb.shape),
                  const((1, d)), const((d, MEM_W)),
                  pl.BlockSpec((1, m, 2 * MEM_W), lambda i, t: (i, 0, 0)),
                  const((1, MEM_HEAD_DIM)), const((1, MEM_HEAD_DIM)), const((MEM_W, d))],
        out_specs=tile(d),
        out_shape=jax.ShapeDtypeStruct((b, s, d), F32),
        compiler_params=_params("parallel", "arbitrary"),
        name="mix_out_mem_attention",
    )(x, ya, yb, wa, wb, g.reshape(1, d).astype(F32), wq, mem_kv, gq.reshape(1, -1).astype(F32),
      gk.reshape(1, -1).astype(F32), wo)


def _rope_tables(positions, dim, lead_ones, tail):
    inv = 1.0 / (ROPE_THETA ** (jnp.arange(0, dim, 2, dtype=F32) / dim))
    ang = positions.astype(F32)[..., None] * inv
    c, s = jnp.cos(ang), jnp.sin(ang)
    shape = positions.shape
    cos = jnp.concatenate([jnp.ones(shape + (lead_ones,), F32), c, c, jnp.ones(shape + (tail,), F32)], axis=-1)
    sin = jnp.concatenate([jnp.zeros(shape + (lead_ones,), F32), -s, s, jnp.zeros(shape + (tail,), F32)], axis=-1)
    reps = LANES // cos.shape[-1]
    return jnp.tile(cos, (1, 1, reps)), jnp.tile(sin, (1, 1, reps))


IN_TILE = 512


def _pad_rows(w, rows):
    return jnp.pad(w, ((0, rows - w.shape[0]), (0, 0)))


def _ab_in_segments(w):
    c = (SWA_HEADS + 2 * SWA_KV_HEADS) * HEAD_DIM + 3 * RWKV_DIM
    wt = w.T.astype(BF16)
    lora = jnp.concatenate([_pad_rows(wt[c:c + DECAY_LORA], LANES),
                            _pad_rows(wt[c + DECAY_LORA:c + DECAY_LORA + AAA_LORA], LANES),
                            _pad_rows(wt[c + DECAY_LORA + AAA_LORA:], 2 * LANES)], axis=0)
    return [wt[:c], lora]


def _cd_in_segments(w):
    c1 = MLA_Q_RANK + MLA_KV_RANK
    wt = w.T.astype(BF16)
    z = lambda n: jnp.zeros((n, wt.shape[1]), wt.dtype)
    mid = jnp.concatenate([wt[MLA_Q_RANK:c1], z(MLA_NOPE), wt[c1:c1 + MLA_ROPE],
                           z(LANES - MLA_NOPE - MLA_ROPE), z(LANES)], axis=0)
    return [wt[:MLA_Q_RANK], mid, wt[c1 + MLA_ROPE:]]


def _head_slabs(w, per_head):
    k = w.shape[0]
    return jnp.pad(w.reshape(k, -1, per_head), ((0, 0), (0, 0), (0, LANES - per_head))).reshape(k, -1)


def _slab_vec(*parts):
    v = jnp.concatenate([p.astype(F32) for p in parts])
    return jnp.pad(v, (0, LANES - v.shape[0])).reshape(1, LANES)


def kernel(x, mem, positions, ffn1_norm, ffn1_w_gate, ffn1_w_up, ffn1_w_down, mix_norm, ab_w_in, ab_w_out, swa_q_norm, swa_k_norm, swa_sinks, rwkv_mu, rwkv_w0, rwkv_w2, rwkv_a0, rwkv_a2, rwkv_g2, rwkv_k_k, rwkv_k_a, rwkv_r_k, rwkv_gn_g, rwkv_gn_b, cd_w_in, cd_w_out, mla_cq_norm, mla_ckv_norm, mla_w_uq, mla_w_ukv, mla_q_nope_norm, mla_k_nope_norm, mla_q_rope_norm, mla_k_rope_norm, diff_q_norm, diff_k_norm, diff_lq1, diff_lk1, diff_lq2, diff_lk2, diff_subln, memx_norm, memx_w_q, memx_q_norm, memx_w_o, mem_norm, mem_w_kv, mem_k_norm, ffn2_norm, ffn2_w_gate, ffn2_w_up, ffn2_w_down):
    b, s, d = x.shape
    m = mem.shape[1]
    t = b * s
    depth = ffn1_norm.shape[0]
    bf = lambda w: w.astype(BF16)
    cos64, sin64 = _rope_tables(positions, HEAD_DIM, 0, 0)
    cos32, sin32 = _rope_tables(positions, MLA_ROPE, MLA_NOPE, LANES - MLA_NOPE - MLA_ROPE)

    mem_kv = norm_matmul(mem.reshape(b * m, d), mem_norm, [bf(mem_w_kv).T],
                         tn=IN_TILE, name="mem_kv").reshape(b, m, 2 * MEM_W)

    x = x.reshape(t, d)
    for layer in range(depth):
        j = layer // 2
        x = ffn(x, ffn1_norm[layer], ffn1_w_gate, ffn1_w_up, ffn1_w_down, layer, name="ffn1")
        if layer % 2 == 0:
            u = norm_matmul(x, mix_norm[layer], _ab_in_segments(ab_w_in[j]), tn=IN_TILE, name="ab_in")
            u = u.reshape(b, s, AB_IN_PAD)
            y_a = swa_attention(u, cos64, sin64, _slab_vec(swa_q_norm[j], swa_q_norm[j]),
                                _slab_vec(swa_k_norm[j], swa_k_norm[j]), swa_sinks[j])
            y_b = rwkv_mix(u, rwkv_mu[j], rwkv_w0[j], rwkv_w2[j], rwkv_a0[j], rwkv_a2[j], rwkv_g2[j],
                           rwkv_k_k[j], rwkv_k_a[j], rwkv_r_k[j], rwkv_gn_g[j], rwkv_gn_b[j])
            mixed, w_out, split = (y_a, y_b), bf(ab_w_out[j]), SWA_HEADS * HEAD_DIM
        else:
            u = norm_matmul(x, mix_norm[layer], _cd_in_segments(cd_w_in[j]), tn=IN_TILE, name="cd_in")
            u = u.reshape(b, s, CD_IN_PAD)
            zero64 = jnp.zeros((MLA_NOPE,), F32)
            q_c, k_c, vt_c = mla_prep(u, cos32, sin32,
                                      _slab_vec(mla_q_nope_norm[j], mla_q_rope_norm[j]),
                                      _slab_vec(mla_k_nope_norm[j]),
                                      _slab_vec(zero64, mla_k_rope_norm[j]),
                                      mla_cq_norm[j], mla_ckv_norm[j],
                                      bf(_head_slabs(mla_w_uq[j], MLA_NOPE + MLA_ROPE)), bf(mla_w_ukv[j]))
            y_c = causal_attention(q_c, k_c, vt_c, n_sm=1, hps=8, name="mla_attention")
            q_d, k_d, vt_d = diff_prep(u, cos64, sin64, _slab_vec(diff_q_norm[j], diff_q_norm[j]),
                                       _slab_vec(diff_k_norm[j], diff_k_norm[j]))
            lambda_init = 0.8 - 0.6 * math.exp(-0.3 * layer)
            lam = (jnp.exp(jnp.sum(diff_lq1[j].astype(F32) * diff_lk1[j].astype(F32)))
                   - jnp.exp(jnp.sum(diff_lq2[j].astype(F32) * diff_lk2[j].astype(F32))) + lambda_init)
            y_d = causal_attention(q_d, k_d, vt_d, n_sm=2, hps=4, lam=lam, g=diff_subln[j],
                                   out_scale=1.0 - lambda_init, name="diff_attention")
            mixed, w_out, split = (y_c, y_d), bf(cd_w_out[j]), MLA_HEADS * MLA_V
        x = mix_out_mem_attention(x.reshape(b, s, d), mixed[0], mixed[1], w_out[:split], w_out[split:], mem_kv,
                                  memx_norm[layer], bf(memx_w_q[layer]), memx_q_norm[layer], mem_k_norm,
                                  bf(memx_w_o[layer])).reshape(t, d)
        x = ffn(x, ffn2_norm[layer], ffn2_w_gate, ffn2_w_up, ffn2_w_down, layer, name="ffn2")
    return x.reshape(b, s, d)
```

```python
import functools
import math

import jax
import jax.numpy as jnp
from jax import lax
from jax.experimental import pallas as pl
from jax.experimental.pallas import tpu as pltpu

F32 = jnp.float32
BF16 = jnp.bfloat16

EPS = 1e-6
ROPE_THETA = 10000.0
NEG_INF = -1e30
LOG2E = math.log2(math.e)
ATTN_LOOKAHEAD = 8
ATTN_TILE = 256
SEG_W = 256
LANES = 128

D_MODEL = 2048
D_FF = 5632
HEAD_DIM = 64
SWA_HEADS = 16
SWA_KV_HEADS = 4
SWA_BLOCK = 128
RWKV_DIM = 1024
RWKV_CHUNK = 64
RWKV_GN_EPS = 64e-5
DECAY_LORA, AAA_LORA, GATE_LORA = 64, 64, 160
MLA_HEADS, MLA_Q_RANK, MLA_KV_RANK, MLA_NOPE, MLA_ROPE, MLA_V = 16, 512, 256, 64, 32, 64
DIFF_HEADS, DIFF_QK, DIFF_V = 8, 64, 128
MEM_HEADS, MEM_HEAD_DIM = 4, 128
MEM_W = MEM_HEADS * MEM_HEAD_DIM
AB_IN_PAD = 5120
CD_IN_PAD = 4096

VMEM_LIMIT = 48 * 1024 * 1024
FFN_VMEM_LIMIT = 60 * 1024 * 1024


def _params(*sem):
    return pltpu.CompilerParams(dimension_semantics=sem, vmem_limit_bytes=VMEM_LIMIT)


def _dot(a, b):
    return jnp.dot(a, b, preferred_element_type=F32)


def _dot_t(a, b):
    return lax.dot_general(a, b, (((1,), (1,)), ((), ())), preferred_element_type=F32)


def _dot_0(a, b):
    return lax.dot_general(a, b, (((0,), (0,)), ((), ())), preferred_element_type=F32)


def _rms(x, g):
    return x * lax.rsqrt(jnp.mean(x * x, axis=-1, keepdims=True) + EPS) * g


def _seg_sum(x, e):
    xb = x.astype(BF16)
    w = e.shape[0]
    parts = [_dot(xb[:, i:i + w], e) for i in range(0, x.shape[1], w)]
    return parts[0] if len(parts) == 1 else jnp.concatenate(parts, axis=-1)


def _tile_lanes(v, width):
    return v if v.shape[-1] == width else jnp.tile(v, (1, width // v.shape[-1]))


def _store_transposed_tiles(vt_ref, v):
    for r in range(vt_ref.shape[1]):
        vt_ref[0, r] = v[r * ATTN_TILE:(r + 1) * ATTN_TILE, :].T.astype(vt_ref.dtype)


def _swap_halves(x, half):
    w = x.shape[-1]
    lane = lax.broadcasted_iota(jnp.int32, x.shape, x.ndim - 1)
    low = (lane & (2 * half - 1)) < half
    return jnp.where(low, pltpu.roll(x, w - half, x.ndim - 1), pltpu.roll(x, half, x.ndim - 1))


def _norm_matmul_kernel(x_ref, g_ref, *refs, starts, counts):
    w_refs, o_ref, xn_ref = refs[:len(starts)], refs[len(starts)], refs[len(starts) + 1]
    j = pl.program_id(1)

    @pl.when(j == 0)
    def _():
        xn_ref[...] = _rms(x_ref[...], g_ref[...]).astype(BF16)

    for w_ref, start, count in zip(w_refs, starts, counts):
        @pl.when((j >= start) & (j < start + count))
        def _(w_ref=w_ref):
            o_ref[...] = _dot_t(xn_ref[...], w_ref[...])


def norm_matmul(x, g, segments, *, tm=1024, tn=512, name="norm_matmul"):
    t, k = x.shape
    tm = min(tm, t)
    assert t % tm == 0 and all(w.shape[0] % tn == 0 and w.shape[1] == k for w in segments)
    counts = [w.shape[0] // tn for w in segments]
    starts = [sum(counts[:i]) for i in range(len(counts))]
    w_spec = lambda start, count: pl.BlockSpec((tn, k), lambda i, j: (jnp.clip(j - start, 0, count - 1), 0))
    return pl.pallas_call(
        functools.partial(_norm_matmul_kernel, starts=tuple(starts), counts=tuple(counts)),
        grid=(t // tm, sum(counts)),
        in_specs=[pl.BlockSpec((tm, k), lambda i, j: (i, 0)),
                  pl.BlockSpec((1, k), lambda i, j: (0, 0))]
                 + [w_spec(start, count) for start, count in zip(starts, counts)],
        out_specs=pl.BlockSpec((tm, tn), lambda i, j: (i, j)),
        out_shape=jax.ShapeDtypeStruct((t, sum(counts) * tn), F32),
        scratch_shapes=[pltpu.VMEM((tm, k), BF16)],
        compiler_params=_params("parallel", "arbitrary"),
        name=name,
    )(x, g.reshape(1, k).astype(F32), *segments)


def _ffn_kernel(x_ref, g_ref, wg_ref, wu_ref, wd_ref, o_ref, xn_ref):
    @pl.when(pl.program_id(1) == 0)
    def _():
        x = x_ref[...]
        xn_ref[...] = _rms(x, g_ref[...]).astype(BF16)
        o_ref[...] = x

    xn = xn_ref[...]
    a = _dot(xn, wg_ref[...].astype(BF16))
    b = _dot(xn, wu_ref[...].astype(BF16))
    h = (a * (0.5 / (1.0 + jnp.exp(-a))) * b).astype(BF16)
    o_ref[...] += _dot(h, wd_ref[...].astype(BF16))


def ffn(x, g, wg, wu, wd, layer, *, tm=1024, tf=256, name="ffn"):
    t, d = x.shape
    ff = wg.shape[2]
    tm = min(tm, t)
    assert t % tm == 0 and ff % tf == 0
    nf = ff // tf
    return pl.pallas_call(
        _ffn_kernel,
        grid=(t // tm, nf),
        in_specs=[pl.BlockSpec((tm, d), lambda i, f: (i, 0)),
                  pl.BlockSpec((1, d), lambda i, f: (0, 0)),
                  pl.BlockSpec((None, d, tf), lambda i, f: (layer, 0, f)),
                  pl.BlockSpec((None, d, tf), lambda i, f: (layer, 0, f)),
                  pl.BlockSpec((None, tf, d), lambda i, f: (layer, f, 0))],
        out_specs=pl.BlockSpec((tm, d), lambda i, f: (i, 0)),
        out_shape=jax.ShapeDtypeStruct((t, d), F32),
        scratch_shapes=[pltpu.VMEM((tm, d), BF16)],
        compiler_params=pltpu.CompilerParams(dimension_semantics=("parallel", "arbitrary"),
                                             vmem_limit_bytes=FFN_VMEM_LIMIT),
        name=name,
    )(x, g.reshape(1, d).astype(F32), wg, wu, wd)


def _swa_kernel(sink_ref, q_ref, kc_ref, kp_ref, vc_ref, vp_ref, cc_ref, sc_ref, cp_ref, sp_ref,
                gq_ref, gk_ref, e_ref, o_ref):
    n = pl.program_id(1)
    blk = SWA_BLOCK
    group = SWA_HEADS // SWA_KV_HEADS
    q = q_ref[0]
    k = jnp.concatenate([kp_ref[0], kc_ref[0]], axis=0)
    v = jnp.concatenate([vp_ref[0], vc_ref[0]], axis=0)
    cos_q, sin_q = cc_ref[0], sc_ref[0]
    cos_k = jnp.concatenate([cp_ref[0], cos_q], axis=0)
    sin_k = jnp.concatenate([sp_ref[0], sin_q], axis=0)
    def prep(x, g, cos, sin, scale):
        w = x.shape[1]
        xg = x * _tile_lanes(g, w)
        xr = xg * _tile_lanes(cos, w) + _swap_halves(xg, HEAD_DIM // 2) * _tile_lanes(sin, w)
        return xr * lax.rsqrt(_seg_sum(x * x, e_ref[...]) * (1.0 / HEAD_DIM) + EPS) * scale

    qr = prep(q, gq_ref[...], cos_q, sin_q, HEAD_DIM ** -0.5 * LOG2E).astype(BF16)
    kr = prep(k, gk_ref[...], cos_k, sin_k, 1.0)
    key_i = lax.broadcasted_iota(jnp.int32, (2 * blk, blk), 0)
    qry_i = lax.broadcasted_iota(jnp.int32, (2 * blk, blk), 1)
    rel = qry_i + blk - key_i
    valid = (rel >= 0) & (rel < blk) & ((n > 0) | (key_i >= blk))
    low = lax.broadcasted_iota(jnp.int32, (blk, LANES), 1) < HEAD_DIM
    zero = jnp.zeros((blk, LANES), BF16)
    k_dup, v_t = [], []
    for g in range(SWA_KV_HEADS):
        kg = kr[:, g * HEAD_DIM:(g + 1) * HEAD_DIM]
        k_dup.append(jnp.concatenate([kg, kg], axis=-1).astype(BF16))
        v_t.append(v[:, g * HEAD_DIM:(g + 1) * HEAD_DIM].T.astype(BF16))

    def scores(h):
        slab = qr[:, (h // 2) * LANES:(h // 2 + 1) * LANES]
        qh = jnp.where(low, slab, zero) if h % 2 == 0 else jnp.where(low, zero, slab)
        return _dot_t(k_dup[h // group], qh)

    ahead = [scores(h) for h in range(ATTN_LOOKAHEAD)]
    ones = jnp.ones((8, 2 * blk), BF16)
    outs = []
    for h in range(SWA_HEADS):
        s = jnp.where(valid, ahead.pop(0), NEG_INF)
        if h + ATTN_LOOKAHEAD < SWA_HEADS:
            ahead.append(scores(h + ATTN_LOOKAHEAD))
        sink = sink_ref[h] * LOG2E
        m = jnp.maximum(jnp.max(s, axis=0, keepdims=True), sink)
        p = jnp.exp2(s - m).astype(BF16)
        den = _dot(ones, p)[0:1] + jnp.exp2(sink - m)
        outs.append(_dot(v_t[h // group], p) / den)
    slabs = [jnp.concatenate(outs[i:i + 2], axis=0).T for i in range(0, SWA_HEADS, 2)]
    o_ref[0] = jnp.concatenate(slabs, axis=-1).astype(o_ref.dtype)


def swa_attention(u, cos, sin, gq, gk, sinks):
    b, s, _ = u.shape
    nb = s // SWA_BLOCK
    qw, kw = SWA_HEADS * HEAD_DIM, SWA_KV_HEADS * HEAD_DIM
    cur = lambda c: (lambda i, n: (i, n, c))
    prev = lambda c: (lambda i, n: (i, jnp.maximum(n - 1, 0), c))
    tab = pl.BlockSpec((1, SWA_BLOCK, LANES), cur(0))
    tab_prev = pl.BlockSpec((1, SWA_BLOCK, LANES), prev(0))
    gain = pl.BlockSpec((1, LANES), lambda i, n: (0, 0))
    seg = (jnp.arange(kw)[:, None] // HEAD_DIM == jnp.arange(kw)[None, :] // HEAD_DIM).astype(BF16)
    return pl.pallas_call(
        _swa_kernel,
        grid=(b, nb),
        in_specs=[pl.BlockSpec(memory_space=pltpu.SMEM),
                  pl.BlockSpec((1, SWA_BLOCK, qw), cur(0)),
                  pl.BlockSpec((1, SWA_BLOCK, kw), cur(qw // kw)),
                  pl.BlockSpec((1, SWA_BLOCK, kw), prev(qw // kw)),
                  pl.BlockSpec((1, SWA_BLOCK, kw), cur(qw // kw + 1)),
                  pl.BlockSpec((1, SWA_BLOCK, kw), prev(qw // kw + 1)),
                  tab, tab, tab_prev, tab_prev, gain, gain, pl.BlockSpec((kw, kw), lambda i, n: (0, 0))],
        out_specs=pl.BlockSpec((1, SWA_BLOCK, qw), cur(0)),
        out_shape=jax.ShapeDtypeStruct((b, s, qw), BF16),
        compiler_params=_params("parallel", "arbitrary"),
        name="swa_attention",
    )(sinks.astype(F32), u, u, u, u, u, cos, sin, cos, sin, gq, gk, seg)


def _mm(a, b, dims, passes):
    dn = (dims, ((), ()))
    dg = lambda x, y: lax.dot_general(x, y, dn, preferred_element_type=F32)
    ah = a.astype(BF16)
    bh = b.astype(BF16)
    if passes == 1:
        return dg(ah, bh)
    al = (a - ah.astype(F32)).astype(BF16)
    bl = (b - bh.astype(F32)).astype(BF16)
    return dg(ah, bh) + dg(ah, bl) + dg(al, bh)


_NN = ((1,), (0,))
_NT = ((1,), (1,))
_TN = ((0,), (0,))
P_SC, P_INV, P_PQ, P_OUT, P_ST = 1, 1, 1, 1, 1
RWKV_UNROLL = 8


RWKV_PREP_ROWS = 256


def _rwkv_kernel(ur_ref, uk_ref, uv_ref, ul_ref, mur_ref, muk_ref, muv_ref, mul_ref, w0_ref, w2_ref, a0_ref,
                 a2_ref, g2_ref, kkw_ref, ka_ref, gng_ref, gnb_ref, rk_ref,
                 o_ref, r_ref, k_ref, v_ref, kk_ref, b_ref, lw_ref, g_ref,
                 st_ref, y1_ref, y0_ref, n_ref, z_ref, dec_ref, *, nchunk):
    c = RWKV_CHUNK
    lane_c = lax.broadcasted_iota(jnp.int32, (c, LANES), 1)
    head0 = lane_c < HEAD_DIM
    ri = lax.broadcasted_iota(jnp.int32, (2 * c, 2 * c), 0)
    ci = lax.broadcasted_iota(jnp.int32, (2 * c, 2 * c), 1)
    eye = jnp.where(ri == ci, 1.0, 0.0)
    tril_c = jnp.where(lax.broadcasted_iota(jnp.int32, (c, c), 0) >= lax.broadcasted_iota(jnp.int32, (c, c), 1),
                       1.0, 0.0).astype(BF16)
    stack = lambda x: jnp.concatenate([jnp.where(head0, x, 0.0), jnp.where(head0, 0.0, x)], axis=0)

    def seg_mean(x):
        first = lax.broadcasted_iota(jnp.int32, x.shape, 1) < HEAD_DIM
        m0 = jnp.sum(jnp.where(first, x, 0.0), axis=-1, keepdims=True)
        m1 = jnp.sum(jnp.where(first, 0.0, x), axis=-1, keepdims=True)
        return jnp.where(first, m0, m1) * (1.0 / HEAD_DIM)

    def prep(io, first_group):
        group_rows = RWKV_UNROLL * c
        for tix in range(group_rows // RWKV_PREP_ROWS):
            start = pl.multiple_of(io * group_rows + tix * RWKV_PREP_ROWS, RWKV_PREP_ROWS)
            rows = pl.ds(start, RWKV_PREP_ROWS)
            at_start = first_group and tix == 0

            def shifted(ref, mu_ref):
                x = ref[0, rows, :]
                if at_start:
                    last = jnp.zeros((1, x.shape[1]), F32)
                else:
                    last = ref[0, pl.ds(pl.multiple_of(start - 8, 8), 8), :][7:8, :]
                row = lax.broadcasted_iota(jnp.int32, x.shape, 0)
                prev = jnp.where(row == 0, last, pltpu.roll(x, 1, 0))
                return x + (prev - x) * mu_ref[...]

            r = shifted(ur_ref, mur_ref)
            k = shifted(uk_ref, muk_ref)
            v = shifted(uv_ref, muv_ref)
            lo = shifted(ul_ref, mul_ref)
            yield
            w_lo, a_lo, g_lo = lo[:, 0:LANES], lo[:, LANES:2 * LANES], lo[:, 2 * LANES:4 * LANES]
            z = -(w0_ref[...] + _mm(jnp.tanh(w_lo), w2_ref[...], _NN, 3))
            w = -(jnp.maximum(z, 0.0) + jnp.log(1.0 + jnp.exp(-jnp.abs(z)))) - 0.5
            a = 1.0 / (1.0 + jnp.exp(-(a0_ref[...] + _mm(a_lo, a2_ref[...], _NN, 1))))
            g = _mm(1.0 / (1.0 + jnp.exp(-g_lo)), g2_ref[...], _NN, 1)
            yield
            kk = k * kkw_ref[...]
            kk = kk / jnp.maximum(jnp.sqrt(seg_mean(kk * kk) * float(HEAD_DIM)), 1e-12)
            r_ref[rows, :] = r
            k_ref[rows, :] = k * (1.0 + (a - 1.0) * ka_ref[...])
            v_ref[rows, :] = v
            kk_ref[rows, :] = kk
            b_ref[rows, :] = kk * a
            lw_ref[rows, :] = -jnp.exp(w)
            g_ref[rows, :] = g
            yield

    def build(ics):
        each = lambda f, *cols: [f(*args) for args in zip(*cols)]
        sls = [pl.ds(pl.multiple_of(ic * c, c), c) for ic in ics]
        load = lambda ref: [ref[sl, :] for sl in sls]
        r, k, v, kk, b, lw = (load(ref) for ref in (r_ref, k_ref, v_ref, kk_ref, b_ref, lw_ref))

        def running_sum(x):
            l1 = x.astype(BF16)
            rest = x - l1.astype(F32)
            l2 = rest.astype(BF16)
            l3 = (rest - l2.astype(F32)).astype(BF16)
            return _dot(tril_c, l1) + _dot(tril_c, l2) + _dot(tril_c, l3)

        cum = each(running_sum, lw)
        yield
        cum_end = [x[c - 1:c, :] for x in cum]
        e_neg = each(lambda x: jnp.exp(-x), cum)
        e_end = each(lambda x, xe: jnp.exp(xe - x), cum, cum_end)
        a_s = each(lambda kk_, x, l: stack(-kk_ * jnp.exp(x - l)), kk, cum, lw)
        r_s = each(lambda r_, x: stack(r_ * jnp.exp(x)), r, cum)
        b_s = each(lambda b_, e: stack(b_ * e), b, e_neg)
        k_s = each(lambda k_, e: stack(k_ * e), k, e_neg)
        bh_s = each(lambda b_, e: stack(b_ * e), b, e_end)
        kh_s = each(lambda k_, e: stack(k_ * e), k, e_end)
        v_s = each(stack, v)
        n2 = 2 * c
        sc = each(lambda a_, r_, b_, k_: _mm(jnp.concatenate([a_, r_], axis=0),
                                             jnp.concatenate([b_, k_], axis=0), _NT, P_SC), a_s, r_s, b_s, k_s)
        low = [jnp.where(ri > ci, x[:n2, :n2], 0.0) for x in sc]
        a_ak = [jnp.where(ri > ci, x[:n2, n2:], 0.0) for x in sc]
        a_rb = [jnp.where(ri >= ci, x[n2:, :n2], 0.0) for x in sc]
        a_rk = [jnp.where(ri >= ci, x[n2:, n2:], 0.0) for x in sc]
        yield
        inv = [eye + x for x in low]
        pw = each(lambda x: _mm(x, x, _NN, P_INV), low)
        yield
        levels = 5
        for lvl in range(levels - 1):
            both = each(lambda t, x: _mm(jnp.concatenate([t, x], axis=0), x, _NN, P_INV), inv, pw)
            inv = each(lambda t, r: t + r[:n2], inv, both)
            pw = [r[n2:] for r in both]
            yield
        inv = each(lambda t, x: t + _mm(t, x, _NN, P_INV), inv, pw)
        yield
        akv = each(lambda x, y: _mm(x, y, _NN, P_PQ), a_ak, v_s)
        yield
        pq = each(lambda t, x, y: _mm(t, jnp.concatenate([x, y], axis=1), _NN, P_PQ), inv, a_s, akv)
        yield
        yy = each(lambda x, y: _mm(x, y, _NN, P_OUT), a_rb, pq)
        y0b = each(lambda x, y: _mm(x, y, _NN, P_OUT), a_rk, v_s)
        nz = each(lambda x, y: _mm(x, y, _TN, P_OUT), pq, bh_s)
        zb = each(lambda x, y: _mm(x, y, _TN, P_OUT), v_s, kh_s)
        for i, ic in enumerate(ics):
            y1_ref[ic] = r_s[i] + yy[i][:, :LANES]
            y0_ref[ic] = yy[i][:, LANES:] + y0b[i]
            n_ref[ic] = nz[i][:LANES]
            z_ref[ic] = nz[i][LANES:] + zb[i]
            dec_ref[ic] = jnp.broadcast_to(jnp.exp(cum_end[i]), (8, LANES))

    def emit(ic, st):
        sl = pl.ds(pl.multiple_of(ic * c, c), c)
        r, k, v, g = r_ref[sl, :], k_ref[sl, :], v_ref[sl, :], g_ref[sl, :]
        y_st = _mm(y1_ref[ic], st, _NT, P_ST) + y0_ref[ic]
        y = y_st[0:c] + y_st[c:2 * c]
        mean = seg_mean(y)
        var = seg_mean((y - mean) * (y - mean))
        yn = (y - mean) * lax.rsqrt(var + RWKV_GN_EPS) * gng_ref[...] + gnb_ref[...]
        bonus = seg_mean(r * k * rk_ref[...]) * float(HEAD_DIM) * v
        o_ref[0, sl, :] = ((yn + bonus) * g).astype(o_ref.dtype)
        return st * dec_ref[ic][0:1, :] + _mm(st, n_ref[ic], _NN, P_ST) + z_ref[ic]

    def scan(ics):
        st = st_ref[...]
        for ic in ics:
            st = emit(ic, st)
            yield
        st_ref[...] = st

    def run(*gens):
        live = list(gens)
        while live:
            live = [gen for gen in live if next(gen, live) is not live]

    group = lambda io: [io * RWKV_UNROLL + i for i in range(RWKV_UNROLL)]
    ngroup = nchunk // RWKV_UNROLL
    st_ref[...] = jnp.zeros_like(st_ref)
    run(prep(0, True))
    if ngroup > 1:
        run(prep(1, False), build(group(0)))
    else:
        run(build(group(0)))

    @pl.loop(1, ngroup - 1)
    def _(io):
        run(prep(io + 1, False), build(group(io)), scan(group(io - 1)))

    if ngroup > 1:
        run(build(group(ngroup - 1)), scan(group(ngroup - 2)))
    run(scan(group(ngroup - 1)))


def rwkv_mix(u, mu, w0, w2, a0, a2, g2, k_k, k_a, r_k, gn_g, gn_b):
    bsz, s, _ = u.shape
    npair = RWKV_DIM // LANES
    nchunk = s // RWKV_CHUNK
    assert nchunk % RWKV_UNROLL == 0 and (RWKV_UNROLL * RWKV_CHUNK) % RWKV_PREP_ROWS == 0
    base = (SWA_HEADS + 2 * SWA_KV_HEADS) * HEAD_DIM // LANES
    lora_w = 4 * LANES
    slab = lambda off: pl.BlockSpec((1, s, LANES), lambda i, p: (i, 0, off + p))
    vec = pl.BlockSpec((1, LANES), lambda i, p: (0, p))
    cols = lambda rows: pl.BlockSpec((rows, LANES), lambda i, p: (0, p))
    row = lambda vv: vv.reshape(1, -1).astype(F32)
    pad_rows = lambda m, rows: jnp.pad(m, ((0, rows - m.shape[0]), (0, 0))).astype(F32)
    pad_cols = lambda vv, n: jnp.pad(vv, (0, n - vv.shape[0]))
    c3 = 3 * RWKV_DIM
    mu_l = jnp.concatenate([pad_cols(mu[c3:c3 + DECAY_LORA], LANES),
                            pad_cols(mu[c3 + DECAY_LORA:c3 + DECAY_LORA + AAA_LORA], LANES),
                            pad_cols(mu[c3 + DECAY_LORA + AAA_LORA:], 2 * LANES)])
    seq = pltpu.VMEM((s, LANES), F32)
    mat = pltpu.VMEM((nchunk, LANES, LANES), F32)
    return pl.pallas_call(
        functools.partial(_rwkv_kernel, nchunk=nchunk),
        grid=(bsz, npair),
        in_specs=[slab(base), slab(base + npair), slab(base + 2 * npair),
                  pl.BlockSpec((1, s, lora_w), lambda i, p: (i, 0, (base + 3 * npair) * LANES // lora_w)),
                  vec, vec, vec, pl.BlockSpec((1, lora_w), lambda i, p: (0, 0)),
                  vec, cols(LANES), vec, cols(LANES), cols(2 * LANES), vec, vec, vec, vec, vec],
        out_specs=pl.BlockSpec((1, s, LANES), lambda i, p: (i, 0, p)),
        out_shape=jax.ShapeDtypeStruct((bsz, s, RWKV_DIM), BF16),
        scratch_shapes=[seq] * 7 + [pltpu.VMEM((LANES, LANES), F32), mat, mat, mat, mat,
                                    pltpu.VMEM((nchunk, 8, LANES), F32)],
        compiler_params=_params("parallel", "arbitrary"),
        name="rwkv_mix",
    )(u, u, u, u, row(mu[:RWKV_DIM]), row(mu[RWKV_DIM:2 * RWKV_DIM]), row(mu[2 * RWKV_DIM:c3]), row(mu_l),
      row(w0), pad_rows(w2, LANES), row(a0), pad_rows(a2, LANES), pad_rows(g2, 2 * LANES),
      row(k_k), row(k_a), row(gn_g), row(gn_b), row(r_k))


def _mla_prep_kernel(cq_ref, ckv_ref, pe_ref, cos_ref, sin_ref, e_ref, gq_ref, gkn_ref, gkp_ref, invn_ref,
                     gcq_ref, gckv_ref, wuq_ref, wukv_ref, qo_ref, ko_ref, vo_ref, cqn_ref, ckvn_ref, kpe_ref):
    cos, sin = cos_ref[0], sin_ref[0]
    half = MLA_ROPE // 2

    def rope(x, g):
        w = x.shape[1]
        xg = x * _tile_lanes(g, w)
        return xg * _tile_lanes(cos, w) + _swap_halves(xg, half) * _tile_lanes(sin, w)

    @pl.when(pl.program_id(2) == 0)
    def _():
        cqn_ref[...] = _rms(cq_ref[0], gcq_ref[...]).astype(BF16)
        ckvn_ref[...] = _rms(ckv_ref[0], gckv_ref[...]).astype(BF16)
        pe = pe_ref[0]
        inv_pe = lax.rsqrt(jnp.sum(pe * pe, axis=-1, keepdims=True) * (1.0 / MLA_ROPE) + EPS)
        kpe_ref[...] = rope(pe, gkp_ref[...]) * inv_pe

    x = _dot(cqn_ref[...], wuq_ref[...])
    w = x.shape[1]
    inv_n = _tile_lanes(invn_ref[...], w)
    inv = lax.rsqrt(_seg_sum(x * x, e_ref[...]) * inv_n + EPS)
    qo_ref[0] = (rope(x, gq_ref[...]) * inv * ((MLA_NOPE + MLA_ROPE) ** -0.5 * LOG2E)).astype(qo_ref.dtype)
    kv = _dot(ckvn_ref[...], wukv_ref[...])
    inv_k = lax.rsqrt(_seg_sum(kv * kv, e_ref[...]) * inv_n + EPS)
    k_nope = kv * inv_k * _tile_lanes(gkn_ref[...], w)
    ko_ref[0] = (k_nope + _tile_lanes(kpe_ref[...], w)).astype(ko_ref.dtype)
    v = jnp.concatenate([kv[:, h * LANES + MLA_NOPE:(h + 1) * LANES] for h in range(w // LANES)], axis=-1)
    _store_transposed_tiles(vo_ref, v)


def mla_prep(u, cos, sin, gq, gkn, gkp, gcq, gckv, w_uq, w_ukv, *, ts=512, tc=512):
    b, s, _ = u.shape
    wtot = w_uq.shape[1]
    ts = min(ts, s)
    lane = jnp.arange(SEG_W)
    same = (lane[:, None] // LANES == lane[None, :] // LANES)
    pos = lane % LANES
    nope = pos < MLA_NOPE
    pe = (pos >= MLA_NOPE) & (pos < MLA_NOPE + MLA_ROPE)
    seg = (same & ((nope[:, None] & nope[None, :]) | (pe[:, None] & pe[None, :]))).astype(BF16)
    p1 = jnp.arange(LANES)
    inv_n = jnp.where(p1 < MLA_NOPE, 1.0 / MLA_NOPE, jnp.where(p1 < MLA_NOPE + MLA_ROPE, 1.0 / MLA_ROPE, 0.0))
    blk = pl.BlockSpec((1, ts, tc), lambda i, t, c: (i, t, c))
    tab = pl.BlockSpec((1, ts, LANES), lambda i, t, c: (i, t, 0))
    vec = pl.BlockSpec((1, LANES), lambda i, t, c: (0, 0))
    pe_blk = (MLA_Q_RANK + MLA_KV_RANK) // LANES
    out = jax.ShapeDtypeStruct((b, s, wtot), BF16)
    return pl.pallas_call(
        _mla_prep_kernel,
        grid=(b, s // ts, wtot // tc),
        in_specs=[pl.BlockSpec((1, ts, MLA_Q_RANK), lambda i, t, c: (i, t, 0)),
                  pl.BlockSpec((1, ts, MLA_KV_RANK), lambda i, t, c: (i, t, MLA_Q_RANK // MLA_KV_RANK)),
                  pl.BlockSpec((1, ts, LANES), lambda i, t, c: (i, t, pe_blk)), tab, tab,
                  pl.BlockSpec((SEG_W, SEG_W), lambda i, t, c: (0, 0)), vec, vec, vec, vec,
                  pl.BlockSpec((1, MLA_Q_RANK), lambda i, t, c: (0, 0)),
                  pl.BlockSpec((1, MLA_KV_RANK), lambda i, t, c: (0, 0)),
                  pl.BlockSpec((MLA_Q_RANK, tc), lambda i, t, c: (0, c)),
                  pl.BlockSpec((MLA_KV_RANK, tc), lambda i, t, c: (0, c))],
        out_specs=[blk, blk, pl.BlockSpec((1, ts // ATTN_TILE, tc // LANES * MLA_V, ATTN_TILE),
                                          lambda i, t, c: (i, t, c, 0))],
        out_shape=[out, out, jax.ShapeDtypeStruct((b, s // ATTN_TILE, wtot // LANES * MLA_V, ATTN_TILE), BF16)],
        scratch_shapes=[pltpu.VMEM((ts, MLA_Q_RANK), BF16), pltpu.VMEM((ts, MLA_KV_RANK), BF16),
                        pltpu.VMEM((ts, LANES), F32)],
        compiler_params=_params("parallel", "parallel", "arbitrary"),
        name="mla_prep",
    )(u, u, u, cos, sin, seg, gq, gkn, gkp, inv_n.reshape(1, LANES).astype(F32),
      gcq.reshape(1, -1).astype(F32), gckv.reshape(1, -1).astype(F32), w_uq, w_ukv)


def _diff_prep_kernel(q_ref, k_ref, v_ref, cos_ref, sin_ref, e_ref, gq_ref, gk_ref, qo_ref, ko_ref, vo_ref):
    cos, sin = cos_ref[0], sin_ref[0]
    _store_transposed_tiles(vo_ref, v_ref[0])

    def prep(x, g, scale):
        w = x.shape[1]
        xg = x * _tile_lanes(g, w)
        xr = xg * _tile_lanes(cos, w) + _swap_halves(xg, DIFF_QK // 2) * _tile_lanes(sin, w)
        inv = lax.rsqrt(_seg_sum(x * x, e_ref[...]) * (1.0 / DIFF_QK) + EPS)
        return xr * inv * scale

    qo_ref[0] = prep(q_ref[0], gq_ref[...], DIFF_QK ** -0.5 * LOG2E).astype(qo_ref.dtype)
    ko_ref[0] = prep(k_ref[0], gk_ref[...], 1.0).astype(ko_ref.dtype)


def diff_prep(u, cos, sin, gq, gk, *, ts=512, tc=512):
    b, s, _ = u.shape
    ts = min(ts, s)
    wtot = 2 * DIFF_HEADS * DIFF_QK
    q_base = (CD_IN_PAD - 3 * wtot) // tc
    seg = (jnp.arange(SEG_W)[:, None] // DIFF_QK == jnp.arange(SEG_W)[None, :] // DIFF_QK).astype(BF16)
    blk = lambda off: pl.BlockSpec((1, ts, tc), lambda i, t, c: (i, t, off + c))
    tab = pl.BlockSpec((1, ts, LANES), lambda i, t, c: (i, t, 0))
    vec = pl.BlockSpec((1, LANES), lambda i, t, c: (0, 0))
    out = jax.ShapeDtypeStruct((b, s, wtot), BF16)
    return pl.pallas_call(
        _diff_prep_kernel,
        grid=(b, s // ts, wtot // tc),
        in_specs=[blk(q_base), blk(q_base + wtot // tc), blk(q_base + 2 * wtot // tc), tab, tab,
                  pl.BlockSpec((SEG_W, SEG_W), lambda i, t, c: (0, 0)), vec, vec],
        out_specs=[blk(0), blk(0),
                   pl.BlockSpec((1, ts // ATTN_TILE, tc, ATTN_TILE), lambda i, t, c: (i, t, c, 0))],
        out_shape=[out, out, jax.ShapeDtypeStruct((b, s // ATTN_TILE, wtot, ATTN_TILE), BF16)],
        compiler_params=_params("parallel", "parallel", "arbitrary"),
        name="diff_prep",
    )(u, u, u, cos, sin, seg, gq, gk)


def _causal_attn_kernel(lam_ref, q_ref, k_ref, vt_ref, g_ref, o_ref, *, n_sm, tq, ow, out_scale):
    qi = pl.program_id(2)
    q = q_ref[0]
    hps = q.shape[1] // LANES
    slab = lambda x, h: x[:, h * LANES:(h + 1) * LANES]
    lane = lax.broadcasted_iota(jnp.int32, (tq, LANES), 1)
    qs, src = [], []
    for h in range(hps):
        qh = slab(q, h)
        if n_sm == 2:
            zero = jnp.zeros_like(qh)
            qs += [jnp.where(lane < DIFF_QK, qh, zero), jnp.where(lane < DIFF_QK, zero, qh)]
            src += [h, h]
        else:
            qs.append(qh)
            src.append(h)
    nch = len(qs)
    key_i = lax.broadcasted_iota(jnp.int32, (tq, tq), 0)
    qry_i = lax.broadcasted_iota(jnp.int32, (tq, tq), 1)
    ones = jnp.ones((8, tq), BF16)

    def step(j, carry, diagonal):
        kj = k_ref[0, pl.ds(pl.multiple_of(j * tq, tq), tq), :]
        scores = lambda i: _dot_t(slab(kj, src[i]), qs[i])
        new = []
        ahead = [scores(i) for i in range(min(ATTN_LOOKAHEAD, nch))]
        for i in range(nch):
            s = ahead.pop(0)
            if i + ATTN_LOOKAHEAD < nch:
                ahead.append(scores(i + ATTN_LOOKAHEAD))
            if diagonal:
                s = jnp.where(key_i <= qry_i, s, NEG_INF)
            m, l, acc = carry[3 * i:3 * i + 3]
            m_new = jnp.maximum(m, jnp.max(s, axis=0, keepdims=True))
            alpha = jnp.exp2(m - m_new)
            p = jnp.exp2(s - m_new).astype(BF16)
            new += [m_new, alpha * l + _dot(ones, p)[0:1],
                    alpha * acc + _dot(vt_ref[0, j, src[i] * ow:(src[i] + 1) * ow, :], p)]
        return tuple(new)

    init = (jnp.full((1, tq), NEG_INF, F32), jnp.zeros((1, tq), F32),
            jnp.zeros((ow, tq), F32)) * nch
    carry = lax.fori_loop(0, qi, lambda j, cr: step(j, cr, False), init)
    carry = step(qi, carry, True)
    outs = []
    for h in range(hps):
        c0 = 3 * n_sm * h
        o = carry[c0 + 2] / carry[c0 + 1]
        if n_sm == 2:
            o = o - lam_ref[0] * (carry[c0 + 5] / carry[c0 + 4])
            o = o * lax.rsqrt(jnp.mean(o * o, axis=0, keepdims=True) + EPS) * g_ref[...] * out_scale
        outs.append(o.T.astype(o_ref.dtype))
    o_ref[0] = jnp.concatenate(outs, axis=-1)


def causal_attention(q, k, vt, *, n_sm, lam=None, g=None, out_scale=1.0, hps=4, name="causal_attention"):
    b, s, wtot = q.shape
    tq = vt.shape[3]
    width = hps * LANES
    groups = wtot // width
    ow = vt.shape[2] // (wtot // LANES)
    lam = jnp.zeros((1,), F32) if lam is None else lam.reshape(1).astype(F32)
    g = jnp.ones((ow, 1), F32) if g is None else g.reshape(ow, 1).astype(F32)
    seq = pl.BlockSpec((1, s, width), lambda i, h, t: (i, 0, h))
    tile = pl.BlockSpec((1, tq, width), lambda i, h, t: (i, t, h))
    return pl.pallas_call(
        functools.partial(_causal_attn_kernel, n_sm=n_sm, tq=tq, ow=ow, out_scale=out_scale),
        grid=(b, groups, s // tq),
        in_specs=[pl.BlockSpec(memory_space=pltpu.SMEM), tile, seq,
                  pl.BlockSpec((1, s // tq, hps * ow, tq), lambda i, h, t: (i, 0, h, 0)),
                  pl.BlockSpec((ow, 1), lambda i, h, t: (0, 0))],
        out_specs=pl.BlockSpec((1, tq, hps * ow), lambda i, h, t: (i, t, h)),
        out_shape=jax.ShapeDtypeStruct((b, s, groups * hps * ow), BF16),
        compiler_params=_params("parallel", "parallel", "arbitrary"),
        name=name,
    )(lam, q, k, vt, g)


def _memx_kernel(x_ref, ya_ref, yb_ref, wa_ref, wb_ref, g_ref, wq_ref, kv_ref, gq_ref, gk_ref, wo_ref, o_ref):
    x = x_ref[0] + _dot(ya_ref[0], wa_ref[...]) + _dot(yb_ref[0], wb_ref[...])
    q = _dot(_rms(x, g_ref[...]).astype(BF16), wq_ref[...])
    kv = kv_ref[0]
    outs = []
    for h in range(MEM_HEADS):
        sl = slice(h * MEM_HEAD_DIM, (h + 1) * MEM_HEAD_DIM)
        qh = (_rms(q[:, sl], gq_ref[...]) * MEM_HEAD_DIM ** -0.5).astype(BF16)
        kh = _rms(kv[:, sl], gk_ref[...]).astype(BF16)
        vh = kv[:, MEM_W + h * MEM_HEAD_DIM:MEM_W + (h + 1) * MEM_HEAD_DIM].astype(BF16)
        s = _dot_t(qh, kh)
        p = jnp.exp(s - jnp.max(s, axis=-1, keepdims=True))
        outs.append(_dot(p.astype(BF16), vh) / jnp.sum(p, axis=-1, keepdims=True))
    o_ref[0] = x + _dot(jnp.concatenate(outs, axis=-1).astype(BF16), wo_ref[...])


def mix_out_mem_attention(x, ya, yb, wa, wb, mem_kv, g, wq, gq, gk, wo, *, tm=512):
    b, s, d = x.shape
    m = mem_kv.shape[1]
    tm = min(tm, s)
    const = lambda shape: pl.BlockSpec(shape, lambda i, t: (0,) * len(shape), pipeline_mode=pl.Buffered(1))
    tile = lambda w: pl.BlockSpec((1, tm, w), lambda i, t: (i, t, 0))
    return pl.pallas_call(
        _memx_kernel,
        grid=(b, s // tm),
        in_specs=[tile(d), tile(ya.shape[2]), tile(yb.shape[2]), const(wa.shape), const(wb.shape),
                  const((1, d)), const((d, MEM_W)),
                  pl.BlockSpec((1, m, 2 * MEM_W), lambda i, t: (i, 0, 0)),
                  const((1, MEM_HEAD_DIM)), const((1, MEM_HEAD_DIM)), const((MEM_W, d))],
        out_specs=tile(d),
        out_shape=jax.ShapeDtypeStruct((b, s, d), F32),
        compiler_params=_params("parallel", "arbitrary"),
        name="mix_out_mem_attention",
    )(x, ya, yb, wa, wb, g.reshape(1, d).astype(F32), wq, mem_kv, gq.reshape(1, -1).astype(F32),
      gk.reshape(1, -1).astype(F32), wo)


def _rope_tables(positions, dim, lead_ones, tail):
    inv = 1.0 / (ROPE_THETA ** (jnp.arange(0, dim, 2, dtype=F32) / dim))
    half = dim // 2
    reps = LANES // (lead_ones + dim + tail)
    freq = jnp.tile(jnp.concatenate([jnp.zeros((lead_ones,), F32), inv, inv, jnp.zeros((tail,), F32)]), reps)
    sign = jnp.tile(jnp.concatenate([jnp.zeros((lead_ones,), F32), -jnp.ones((half,), F32),
                                     jnp.ones((half,), F32), jnp.zeros((tail,), F32)]), reps)
    ang = positions.astype(F32)[..., None] * freq
    return jnp.cos(ang), jnp.sin(ang) * sign


IN_TILE = 512


def _pad_rows(w, rows):
    return jnp.pad(w, ((0, rows - w.shape[0]), (0, 0)))


def _ab_in_segments(w):
    c = (SWA_HEADS + 2 * SWA_KV_HEADS) * HEAD_DIM + 3 * RWKV_DIM
    wt = w.T.astype(BF16)
    lora = jnp.concatenate([_pad_rows(wt[c:c + DECAY_LORA], LANES),
                            _pad_rows(wt[c + DECAY_LORA:c + DECAY_LORA + AAA_LORA], LANES),
                            _pad_rows(wt[c + DECAY_LORA + AAA_LORA:], 2 * LANES)], axis=0)
    return [wt[:c], lora]


def _cd_in_segments(w):
    c1 = MLA_Q_RANK + MLA_KV_RANK
    wt = w.T.astype(BF16)
    z = lambda n: jnp.zeros((n, wt.shape[1]), wt.dtype)
    mid = jnp.concatenate([wt[MLA_Q_RANK:c1], z(MLA_NOPE), wt[c1:c1 + MLA_ROPE],
                           z(LANES - MLA_NOPE - MLA_ROPE), z(LANES)], axis=0)
    return [wt[:MLA_Q_RANK], mid, wt[c1 + MLA_ROPE:]]


def _head_slabs(w, per_head):
    k = w.shape[0]
    return jnp.pad(w.reshape(k, -1, per_head), ((0, 0), (0, 0), (0, LANES - per_head))).reshape(k, -1)


def _slab_vec(*parts):
    v = jnp.concatenate([p.astype(F32) for p in parts])
    return jnp.pad(v, (0, LANES - v.shape[0])).reshape(1, LANES)


def kernel(x, mem, positions, ffn1_norm, ffn1_w_gate, ffn1_w_up, ffn1_w_down, mix_norm, ab_w_in, ab_w_out, swa_q_norm, swa_k_norm, swa_sinks, rwkv_mu, rwkv_w0, rwkv_w2, rwkv_a0, rwkv_a2, rwkv_g2, rwkv_k_k, rwkv_k_a, rwkv_r_k, rwkv_gn_g, rwkv_gn_b, cd_w_in, cd_w_out, mla_cq_norm, mla_ckv_norm, mla_w_uq, mla_w_ukv, mla_q_nope_norm, mla_k_nope_norm, mla_q_rope_norm, mla_k_rope_norm, diff_q_norm, diff_k_norm, diff_lq1, diff_lk1, diff_lq2, diff_lk2, diff_subln, memx_norm, memx_w_q, memx_q_norm, memx_w_o, mem_norm, mem_w_kv, mem_k_norm, ffn2_norm, ffn2_w_gate, ffn2_w_up, ffn2_w_down):
    b, s, d = x.shape
    m = mem.shape[1]
    t = b * s
    depth = ffn1_norm.shape[0]
    bf = lambda w: w.astype(BF16)
    cos64, sin64 = _rope_tables(positions, HEAD_DIM, 0, 0)
    cos32, sin32 = _rope_tables(positions, MLA_ROPE, MLA_NOPE, LANES - MLA_NOPE - MLA_ROPE)

    mem_kv = norm_matmul(mem.reshape(b * m, d), mem_norm, [bf(mem_w_kv).T],
                         tn=IN_TILE, name="mem_kv").reshape(b, m, 2 * MEM_W)

    x = x.reshape(t, d)
    for layer in range(depth):
        j = layer // 2
        x = ffn(x, ffn1_norm[layer], ffn1_w_gate, ffn1_w_up, ffn1_w_down, layer, name="ffn1")
        if layer % 2 == 0:
            u = norm_matmul(x, mix_norm[layer], _ab_in_segments(ab_w_in[j]), tn=IN_TILE, name="ab_in")
            u = u.reshape(b, s, AB_IN_PAD)
            y_a = swa_attention(u, cos64, sin64, _slab_vec(swa_q_norm[j], swa_q_norm[j]),
                                _slab_vec(swa_k_norm[j], swa_k_norm[j]), swa_sinks[j])
            y_b = rwkv_mix(u, rwkv_mu[j], rwkv_w0[j], rwkv_w2[j], rwkv_a0[j], rwkv_a2[j], rwkv_g2[j],
                           rwkv_k_k[j], rwkv_k_a[j], rwkv_r_k[j], rwkv_gn_g[j], rwkv_gn_b[j])
            mixed, w_out, split = (y_a, y_b), bf(ab_w_out[j]), SWA_HEADS * HEAD_DIM
        else:
            u = norm_matmul(x, mix_norm[layer], _cd_in_segments(cd_w_in[j]), tn=IN_TILE, name="cd_in")
            u = u.reshape(b, s, CD_IN_PAD)
            zero64 = jnp.zeros((MLA_NOPE,), F32)
            q_c, k_c, vt_c = mla_prep(u, cos32, sin32,
                                      _slab_vec(mla_q_nope_norm[j], mla_q_rope_norm[j]),
                                      _slab_vec(mla_k_nope_norm[j]),
                                      _slab_vec(zero64, mla_k_rope_norm[j]),
                                      mla_cq_norm[j], mla_ckv_norm[j],
                                      bf(_head_slabs(mla_w_uq[j], MLA_NOPE + MLA_ROPE)), bf(mla_w_ukv[j]))
            y_c = causal_attention(q_c, k_c, vt_c, n_sm=1, hps=16, name="mla_attention")
            q_d, k_d, vt_d = diff_prep(u, cos64, sin64, _slab_vec(diff_q_norm[j], diff_q_norm[j]),
                                       _slab_vec(diff_k_norm[j], diff_k_norm[j]))
            lambda_init = 0.8 - 0.6 * math.exp(-0.3 * layer)
            lam = (jnp.exp(jnp.sum(diff_lq1[j].astype(F32) * diff_lk1[j].astype(F32)))
                   - jnp.exp(jnp.sum(diff_lq2[j].astype(F32) * diff_lk2[j].astype(F32))) + lambda_init)
            y_d = causal_attention(q_d, k_d, vt_d, n_sm=2, hps=8, lam=lam, g=diff_subln[j],
                                   out_scale=1.0 - lambda_init, name="diff_attention")
            mixed, w_out, split = (y_c, y_d), bf(cd_w_out[j]), MLA_HEADS * MLA_V
        x = mix_out_mem_attention(x.reshape(b, s, d), mixed[0], mixed[1], w_out[:split], w_out[split:], mem_kv,
                                  memx_norm[layer], bf(memx_w_q[layer]), memx_q_norm[layer], mem_k_norm,
                                  bf(memx_w_o[layer])).reshape(t, d)
        x = ffn(x, ffn2_norm[layer], ffn2_w_gate, ffn2_w_up, ffn2_w_down, layer, name="ffn2")
    return x.reshape(b, s, d)
```

```python
import functools
import math

import jax
import jax.numpy as jnp
from jax import lax
from jax.experimental import pallas as pl
from jax.experimental.pallas import tpu as pltpu

F32 = jnp.float32
BF16 = jnp.bfloat16

EPS = 1e-6
ROPE_THETA = 10000.0
NEG_INF = -1e30
LOG2E = math.log2(math.e)
ATTN_LOOKAHEAD = 8
SWA_LOOKAHEAD = 4
ATTN_TILE = 256
SEG_W = 256
LANES = 128

D_MODEL = 2048
D_FF = 5632
HEAD_DIM = 64
SWA_HEADS = 16
SWA_KV_HEADS = 4
SWA_BLOCK = 128
RWKV_DIM = 1024
RWKV_CHUNK = 64
RWKV_GN_EPS = 64e-5
DECAY_LORA, AAA_LORA, GATE_LORA = 64, 64, 160
MLA_HEADS, MLA_Q_RANK, MLA_KV_RANK, MLA_NOPE, MLA_ROPE, MLA_V = 16, 512, 256, 64, 32, 64
DIFF_HEADS, DIFF_QK, DIFF_V = 8, 64, 128
MEM_HEADS, MEM_HEAD_DIM = 4, 128
MEM_W = MEM_HEADS * MEM_HEAD_DIM
AB_IN_PAD = 5120
CD_IN_PAD = 4096

VMEM_LIMIT = 48 * 1024 * 1024
FFN_VMEM_LIMIT = 60 * 1024 * 1024


def _params(*sem):
    return pltpu.CompilerParams(dimension_semantics=sem, vmem_limit_bytes=VMEM_LIMIT)


def _dot(a, b):
    return jnp.dot(a, b, preferred_element_type=F32)


def _dot_t(a, b):
    return lax.dot_general(a, b, (((1,), (1,)), ((), ())), preferred_element_type=F32)


def _dot_0(a, b):
    return lax.dot_general(a, b, (((0,), (0,)), ((), ())), preferred_element_type=F32)


def _rms(x, g):
    return x * lax.rsqrt(jnp.mean(x * x, axis=-1, keepdims=True) + EPS) * g


def _seg_sum(x, e):
    xb = x.astype(BF16)
    w = e.shape[0]
    parts = [_dot(xb[:, i:i + w], e) for i in range(0, x.shape[1], w)]
    return parts[0] if len(parts) == 1 else jnp.concatenate(parts, axis=-1)


def _tile_lanes(v, width):
    return v if v.shape[-1] == width else jnp.tile(v, (1, width // v.shape[-1]))


def _store_transposed_tiles(vt_ref, v):
    for r in range(vt_ref.shape[1]):
        vt_ref[0, r] = v[r * ATTN_TILE:(r + 1) * ATTN_TILE, :].T.astype(vt_ref.dtype)


def _swap_halves(x, half):
    w = x.shape[-1]
    lane = lax.broadcasted_iota(jnp.int32, x.shape, x.ndim - 1)
    low = (lane & (2 * half - 1)) < half
    return jnp.where(low, pltpu.roll(x, w - half, x.ndim - 1), pltpu.roll(x, half, x.ndim - 1))


def _norm_matmul_kernel(x_ref, g_ref, *refs, starts, counts):
    w_refs, o_ref, xn_ref = refs[:len(starts)], refs[len(starts)], refs[len(starts) + 1]
    j = pl.program_id(1)

    @pl.when(j == 0)
    def _():
        xn_ref[...] = _rms(x_ref[...], g_ref[...]).astype(BF16)

    for w_ref, start, count in zip(w_refs, starts, counts):
        @pl.when((j >= start) & (j < start + count))
        def _(w_ref=w_ref):
            o_ref[...] = _dot_t(xn_ref[...], w_ref[...])


def norm_matmul(x, g, segments, *, tm=1024, tn=512, name="norm_matmul"):
    t, k = x.shape
    tm = min(tm, t)
    assert t % tm == 0 and all(w.shape[0] % tn == 0 and w.shape[1] == k for w in segments)
    counts = [w.shape[0] // tn for w in segments]
    starts = [sum(counts[:i]) for i in range(len(counts))]
    w_spec = lambda start, count: pl.BlockSpec((tn, k), lambda i, j: (jnp.clip(j - start, 0, count - 1), 0))
    return pl.pallas_call(
        functools.partial(_norm_matmul_kernel, starts=tuple(starts), counts=tuple(counts)),
        grid=(t // tm, sum(counts)),
        in_specs=[pl.BlockSpec((tm, k), lambda i, j: (i, 0)),
                  pl.BlockSpec((1, k), lambda i, j: (0, 0))]
                 + [w_spec(start, count) for start, count in zip(starts, counts)],
        out_specs=pl.BlockSpec((tm, tn), lambda i, j: (i, j)),
        out_shape=jax.ShapeDtypeStruct((t, sum(counts) * tn), F32),
        scratch_shapes=[pltpu.VMEM((tm, k), BF16)],
        compiler_params=_params("parallel", "arbitrary"),
        name=name,
    )(x, g.reshape(1, k).astype(F32), *segments)


def _ffn_kernel(x_ref, g_ref, wg_ref, wu_ref, wd_ref, o_ref, xn_ref):
    @pl.when(pl.program_id(1) == 0)
    def _():
        x = x_ref[...]
        xn_ref[...] = _rms(x, g_ref[...]).astype(BF16)
        o_ref[...] = x

    xn = xn_ref[...]
    a = _dot(xn, wg_ref[...].astype(BF16))
    b = _dot(xn, wu_ref[...].astype(BF16))
    h = (a * (0.5 / (1.0 + jnp.exp(-a))) * b).astype(BF16)
    o_ref[...] += _dot(h, wd_ref[...].astype(BF16))


def ffn(x, g, wg, wu, wd, layer, *, tm=1024, tf=256, name="ffn"):
    t, d = x.shape
    ff = wg.shape[2]
    tm = min(tm, t)
    assert t % tm == 0 and ff % tf == 0
    nf = ff // tf
    return pl.pallas_call(
        _ffn_kernel,
        grid=(t // tm, nf),
        in_specs=[pl.BlockSpec((tm, d), lambda i, f: (i, 0)),
                  pl.BlockSpec((1, d), lambda i, f: (0, 0)),
                  pl.BlockSpec((None, d, tf), lambda i, f: (layer, 0, f)),
                  pl.BlockSpec((None, d, tf), lambda i, f: (layer, 0, f)),
                  pl.BlockSpec((None, tf, d), lambda i, f: (layer, f, 0))],
        out_specs=pl.BlockSpec((tm, d), lambda i, f: (i, 0)),
        out_shape=jax.ShapeDtypeStruct((t, d), F32),
        scratch_shapes=[pltpu.VMEM((tm, d), BF16)],
        compiler_params=pltpu.CompilerParams(dimension_semantics=("parallel", "arbitrary"),
                                             vmem_limit_bytes=FFN_VMEM_LIMIT),
        name=name,
    )(x, g.reshape(1, d).astype(F32), wg, wu, wd)


def _swa_kernel(sink_ref, q_ref, kc_ref, kp_ref, vc_ref, vp_ref, cc_ref, sc_ref, cp_ref, sp_ref,
                gq_ref, gk_ref, e_ref, o_ref):
    n = pl.program_id(1)
    blk = SWA_BLOCK
    group = SWA_HEADS // SWA_KV_HEADS
    q = q_ref[0]
    k = jnp.concatenate([kp_ref[0], kc_ref[0]], axis=0)
    v = jnp.concatenate([vp_ref[0], vc_ref[0]], axis=0)
    cos_q, sin_q = cc_ref[0], sc_ref[0]
    cos_k = jnp.concatenate([cp_ref[0], cos_q], axis=0)
    sin_k = jnp.concatenate([sp_ref[0], sin_q], axis=0)
    def prep(x, g, cos, sin, scale):
        w = x.shape[1]
        xg = x * _tile_lanes(g, w)
        xr = xg * _tile_lanes(cos, w) + _swap_halves(xg, HEAD_DIM // 2) * _tile_lanes(sin, w)
        return xr * lax.rsqrt(_seg_sum(x * x, e_ref[...]) * (1.0 / HEAD_DIM) + EPS) * scale

    qr = prep(q, gq_ref[...], cos_q, sin_q, HEAD_DIM ** -0.5 * LOG2E).astype(BF16)
    kr = prep(k, gk_ref[...], cos_k, sin_k, 1.0)
    key_i = lax.broadcasted_iota(jnp.int32, (2 * blk, blk), 0)
    qry_i = lax.broadcasted_iota(jnp.int32, (2 * blk, blk), 1)
    rel = qry_i + blk - key_i
    valid = (rel >= 0) & (rel < blk) & ((n > 0) | (key_i >= blk))
    low = lax.broadcasted_iota(jnp.int32, (blk, LANES), 1) < HEAD_DIM
    zero = jnp.zeros((blk, LANES), BF16)
    k_dup, v_t = [], []
    for g in range(SWA_KV_HEADS):
        kg = kr[:, g * HEAD_DIM:(g + 1) * HEAD_DIM]
        k_dup.append(jnp.concatenate([kg, kg], axis=-1).astype(BF16))
        v_t.append(v[:, g * HEAD_DIM:(g + 1) * HEAD_DIM].T.astype(BF16))

    def scores(h):
        slab = qr[:, (h // 2) * LANES:(h // 2 + 1) * LANES]
        qh = jnp.where(low, slab, zero) if h % 2 == 0 else jnp.where(low, zero, slab)
        return _dot_t(k_dup[h // group], qh)

    ahead = [scores(h) for h in range(SWA_LOOKAHEAD)]
    ones = jnp.ones((8, 2 * blk), BF16)
    outs = []
    for h in range(SWA_HEADS):
        s = jnp.where(valid, ahead.pop(0), NEG_INF)
        if h + SWA_LOOKAHEAD < SWA_HEADS:
            ahead.append(scores(h + SWA_LOOKAHEAD))
        sink = sink_ref[h] * LOG2E
        m = jnp.maximum(jnp.max(s, axis=0, keepdims=True), sink)
        p = jnp.exp2(s - m).astype(BF16)
        den = _dot(ones, p)[0:1] + jnp.exp2(sink - m)
        outs.append(_dot(v_t[h // group], p) / den)
    slabs = [jnp.concatenate(outs[i:i + 2], axis=0).T for i in range(0, SWA_HEADS, 2)]
    o_ref[0] = jnp.concatenate(slabs, axis=-1).astype(o_ref.dtype)


def swa_attention(u, cos, sin, gq, gk, sinks):
    b, s, _ = u.shape
    nb = s // SWA_BLOCK
    qw, kw = SWA_HEADS * HEAD_DIM, SWA_KV_HEADS * HEAD_DIM
    cur = lambda c: (lambda i, n: (i, n, c))
    prev = lambda c: (lambda i, n: (i, jnp.maximum(n - 1, 0), c))
    tab = pl.BlockSpec((1, SWA_BLOCK, LANES), cur(0))
    tab_prev = pl.BlockSpec((1, SWA_BLOCK, LANES), prev(0))
    gain = pl.BlockSpec((1, LANES), lambda i, n: (0, 0))
    seg = (jnp.arange(kw)[:, None] // HEAD_DIM == jnp.arange(kw)[None, :] // HEAD_DIM).astype(BF16)
    return pl.pallas_call(
        _swa_kernel,
        grid=(b, nb),
        in_specs=[pl.BlockSpec(memory_space=pltpu.SMEM),
                  pl.BlockSpec((1, SWA_BLOCK, qw), cur(0)),
                  pl.BlockSpec((1, SWA_BLOCK, kw), cur(qw // kw)),
                  pl.BlockSpec((1, SWA_BLOCK, kw), prev(qw // kw)),
                  pl.BlockSpec((1, SWA_BLOCK, kw), cur(qw // kw + 1)),
                  pl.BlockSpec((1, SWA_BLOCK, kw), prev(qw // kw + 1)),
                  tab, tab, tab_prev, tab_prev, gain, gain, pl.BlockSpec((kw, kw), lambda i, n: (0, 0))],
        out_specs=pl.BlockSpec((1, SWA_BLOCK, qw), cur(0)),
        out_shape=jax.ShapeDtypeStruct((b, s, qw), BF16),
        compiler_params=_params("parallel", "arbitrary"),
        name="swa_attention",
    )(sinks.astype(F32), u, u, u, u, u, cos, sin, cos, sin, gq, gk, seg)


def _mm(a, b, dims, passes):
    dn = (dims, ((), ()))
    dg = lambda x, y: lax.dot_general(x, y, dn, preferred_element_type=F32)
    ah = a.astype(BF16)
    bh = b.astype(BF16)
    if passes == 1:
        return dg(ah, bh)
    al = (a - ah.astype(F32)).astype(BF16)
    bl = (b - bh.astype(F32)).astype(BF16)
    return dg(ah, bh) + dg(ah, bl) + dg(al, bh)


_NN = ((1,), (0,))
_NT = ((1,), (1,))
_TN = ((0,), (0,))
P_SC, P_INV, P_PQ, P_OUT, P_ST = 1, 1, 1, 1, 1
RWKV_UNROLL = 8


RWKV_PREP_ROWS = 256


def _rwkv_kernel(ur_ref, uk_ref, uv_ref, ul_ref, mur_ref, muk_ref, muv_ref, mul_ref, w0_ref, w2_ref, a0_ref,
                 a2_ref, g2_ref, kkw_ref, ka_ref, gng_ref, gnb_ref, rk_ref,
                 o_ref, r_ref, k_ref, v_ref, kk_ref, b_ref, lw_ref, g_ref,
                 st_ref, y1_ref, y0_ref, n_ref, z_ref, dec_ref, *, nchunk):
    c = RWKV_CHUNK
    lane_c = lax.broadcasted_iota(jnp.int32, (c, LANES), 1)
    head0 = lane_c < HEAD_DIM
    ri = lax.broadcasted_iota(jnp.int32, (2 * c, 2 * c), 0)
    ci = lax.broadcasted_iota(jnp.int32, (2 * c, 2 * c), 1)
    eye = jnp.where(ri == ci, 1.0, 0.0)
    tril_c = jnp.where(lax.broadcasted_iota(jnp.int32, (c, c), 0) >= lax.broadcasted_iota(jnp.int32, (c, c), 1),
                       1.0, 0.0).astype(BF16)
    stack = lambda x: jnp.concatenate([jnp.where(head0, x, 0.0), jnp.where(head0, 0.0, x)], axis=0)

    def seg_mean(x):
        first = lax.broadcasted_iota(jnp.int32, x.shape, 1) < HEAD_DIM
        m0 = jnp.sum(jnp.where(first, x, 0.0), axis=-1, keepdims=True)
        m1 = jnp.sum(jnp.where(first, 0.0, x), axis=-1, keepdims=True)
        return jnp.where(first, m0, m1) * (1.0 / HEAD_DIM)

    def prep(io, first_group):
        group_rows = RWKV_UNROLL * c
        for tix in range(group_rows // RWKV_PREP_ROWS):
            start = pl.multiple_of(io * group_rows + tix * RWKV_PREP_ROWS, RWKV_PREP_ROWS)
            rows = pl.ds(start, RWKV_PREP_ROWS)
            at_start = first_group and tix == 0

            def shifted(ref, mu_ref):
                x = ref[0, rows, :]
                if at_start:
                    last = jnp.zeros((1, x.shape[1]), F32)
                else:
                    last = ref[0, pl.ds(pl.multiple_of(start - 8, 8), 8), :][7:8, :]
                row = lax.broadcasted_iota(jnp.int32, x.shape, 0)
                prev = jnp.where(row == 0, last, pltpu.roll(x, 1, 0))
                return x + (prev - x) * mu_ref[...]

            r = shifted(ur_ref, mur_ref)
            k = shifted(uk_ref, muk_ref)
            v = shifted(uv_ref, muv_ref)
            lo = shifted(ul_ref, mul_ref)
            yield
            w_lo, a_lo, g_lo = lo[:, 0:LANES], lo[:, LANES:2 * LANES], lo[:, 2 * LANES:4 * LANES]
            z = -(w0_ref[...] + _mm(jnp.tanh(w_lo), w2_ref[...], _NN, 3))
            w = -(jnp.maximum(z, 0.0) + jnp.log(1.0 + jnp.exp(-jnp.abs(z)))) - 0.5
            a = 1.0 / (1.0 + jnp.exp(-(a0_ref[...] + _mm(a_lo, a2_ref[...], _NN, 1))))
            g = _mm(1.0 / (1.0 + jnp.exp(-g_lo)), g2_ref[...], _NN, 1)
            yield
            kk = k * kkw_ref[...]
            kk = kk / jnp.maximum(jnp.sqrt(seg_mean(kk * kk) * float(HEAD_DIM)), 1e-12)
            r_ref[rows, :] = r
            k_ref[rows, :] = k * (1.0 + (a - 1.0) * ka_ref[...])
            v_ref[rows, :] = v
            kk_ref[rows, :] = kk
            b_ref[rows, :] = kk * a
            lw_ref[rows, :] = -jnp.exp(w)
            g_ref[rows, :] = g
            yield

    def build(ics):
        each = lambda f, *cols: [f(*args) for args in zip(*cols)]
        sls = [pl.ds(pl.multiple_of(ic * c, c), c) for ic in ics]
        load = lambda ref: [ref[sl, :] for sl in sls]
        r, k, v, kk, b, lw = (load(ref) for ref in (r_ref, k_ref, v_ref, kk_ref, b_ref, lw_ref))

        def running_sum(x):
            l1 = x.astype(BF16)
            rest = x - l1.astype(F32)
            l2 = rest.astype(BF16)
            l3 = (rest - l2.astype(F32)).astype(BF16)
            return _dot(tril_c, l1) + _dot(tril_c, l2) + _dot(tril_c, l3)

        cum = each(running_sum, lw)
        yield
        cum_end = [x[c - 1:c, :] for x in cum]
        e_neg = each(lambda x: jnp.exp(-x), cum)
        e_end = each(lambda x, xe: jnp.exp(xe - x), cum, cum_end)
        a_s = each(lambda kk_, x, l: stack(-kk_ * jnp.exp(x - l)), kk, cum, lw)
        r_s = each(lambda r_, x: stack(r_ * jnp.exp(x)), r, cum)
        b_s = each(lambda b_, e: stack(b_ * e), b, e_neg)
        k_s = each(lambda k_, e: stack(k_ * e), k, e_neg)
        bh_s = each(lambda b_, e: stack(b_ * e), b, e_end)
        kh_s = each(lambda k_, e: stack(k_ * e), k, e_end)
        v_s = each(stack, v)
        n2 = 2 * c
        sc = each(lambda a_, r_, b_, k_: _mm(jnp.concatenate([a_, r_], axis=0),
                                             jnp.concatenate([b_, k_], axis=0), _NT, P_SC), a_s, r_s, b_s, k_s)
        low = [jnp.where(ri > ci, x[:n2, :n2], 0.0) for x in sc]
        a_ak = [jnp.where(ri > ci, x[:n2, n2:], 0.0) for x in sc]
        a_rb = [jnp.where(ri >= ci, x[n2:, :n2], 0.0) for x in sc]
        a_rk = [jnp.where(ri >= ci, x[n2:, n2:], 0.0) for x in sc]
        yield
        inv = [eye + x for x in low]
        pw = each(lambda x: _mm(x, x, _NN, P_INV), low)
        yield
        levels = 5
        for lvl in range(levels - 1):
            both = each(lambda t, x: _mm(jnp.concatenate([t, x], axis=0), x, _NN, P_INV), inv, pw)
            inv = each(lambda t, r: t + r[:n2], inv, both)
            pw = [r[n2:] for r in both]
            yield
        inv = each(lambda t, x: t + _mm(t, x, _NN, P_INV), inv, pw)
        yield
        akv = each(lambda x, y: _mm(x, y, _NN, P_PQ), a_ak, v_s)
        yield
        pq = each(lambda t, x, y: _mm(t, jnp.concatenate([x, y], axis=1), _NN, P_PQ), inv, a_s, akv)
        yield
        yy = each(lambda x, y: _mm(x, y, _NN, P_OUT), a_rb, pq)
        y0b = each(lambda x, y: _mm(x, y, _NN, P_OUT), a_rk, v_s)
        nz = each(lambda x, y: _mm(x, y, _TN, P_OUT), pq, bh_s)
        zb = each(lambda x, y: _mm(x, y, _TN, P_OUT), v_s, kh_s)
        for i, ic in enumerate(ics):
            y1_ref[ic] = r_s[i] + yy[i][:, :LANES]
            y0_ref[ic] = yy[i][:, LANES:] + y0b[i]
            n_ref[ic] = nz[i][:LANES]
            z_ref[ic] = nz[i][LANES:] + zb[i]
            dec_ref[ic] = jnp.broadcast_to(jnp.exp(cum_end[i]), (8, LANES))

    def emit(ic, st):
        sl = pl.ds(pl.multiple_of(ic * c, c), c)
        r, k, v, g = r_ref[sl, :], k_ref[sl, :], v_ref[sl, :], g_ref[sl, :]
        y_st = _mm(y1_ref[ic], st, _NT, P_ST) + y0_ref[ic]
        y = y_st[0:c] + y_st[c:2 * c]
        mean = seg_mean(y)
        var = seg_mean((y - mean) * (y - mean))
        yn = (y - mean) * lax.rsqrt(var + RWKV_GN_EPS) * gng_ref[...] + gnb_ref[...]
        bonus = seg_mean(r * k * rk_ref[...]) * float(HEAD_DIM) * v
        o_ref[0, sl, :] = ((yn + bonus) * g).astype(o_ref.dtype)
        return st * dec_ref[ic][0:1, :] + _mm(st, n_ref[ic], _NN, P_ST) + z_ref[ic]

    def scan(ics):
        st = st_ref[...]
        for ic in ics:
            st = emit(ic, st)
            yield
        st_ref[...] = st

    def run(*gens):
        live = list(gens)
        while live:
            live = [gen for gen in live if next(gen, live) is not live]

    group = lambda io: [io * RWKV_UNROLL + i for i in range(RWKV_UNROLL)]
    ngroup = nchunk // RWKV_UNROLL
    st_ref[...] = jnp.zeros_like(st_ref)
    run(prep(0, True))
    if ngroup > 1:
        run(prep(1, False), build(group(0)))
    else:
        run(build(group(0)))

    @pl.loop(1, ngroup - 1)
    def _(io):
        run(prep(io + 1, False), build(group(io)), scan(group(io - 1)))

    if ngroup > 1:
        run(build(group(ngroup - 1)), scan(group(ngroup - 2)))
    run(scan(group(ngroup - 1)))


def rwkv_mix(u, mu, w0, w2, a0, a2, g2, k_k, k_a, r_k, gn_g, gn_b):
    bsz, s, _ = u.shape
    npair = RWKV_DIM // LANES
    nchunk = s // RWKV_CHUNK
    assert nchunk % RWKV_UNROLL == 0 and (RWKV_UNROLL * RWKV_CHUNK) % RWKV_PREP_ROWS == 0
    base = (SWA_HEADS + 2 * SWA_KV_HEADS) * HEAD_DIM // LANES
    lora_w = 4 * LANES
    slab = lambda off: pl.BlockSpec((1, s, LANES), lambda i, p: (i, 0, off + p))
    vec = pl.BlockSpec((1, LANES), lambda i, p: (0, p))
    cols = lambda rows: pl.BlockSpec((rows, LANES), lambda i, p: (0, p))
    row = lambda vv: vv.reshape(1, -1).astype(F32)
    pad_rows = lambda m, rows: jnp.pad(m, ((0, rows - m.shape[0]), (0, 0))).astype(F32)
    pad_cols = lambda vv, n: jnp.pad(vv, (0, n - vv.shape[0]))
    c3 = 3 * RWKV_DIM
    mu_l = jnp.concatenate([pad_cols(mu[c3:c3 + DECAY_LORA], LANES),
                            pad_cols(mu[c3 + DECAY_LORA:c3 + DECAY_LORA + AAA_LORA], LANES),
                            pad_cols(mu[c3 + DECAY_LORA + AAA_LORA:], 2 * LANES)])
    seq = pltpu.VMEM((s, LANES), F32)
    mat = pltpu.VMEM((nchunk, LANES, LANES), F32)
    return pl.pallas_call(
        functools.partial(_rwkv_kernel, nchunk=nchunk),
        grid=(bsz, npair),
        in_specs=[slab(base), slab(base + npair), slab(base + 2 * npair),
                  pl.BlockSpec((1, s, lora_w), lambda i, p: (i, 0, (base + 3 * npair) * LANES // lora_w)),
                  vec, vec, vec, pl.BlockSpec((1, lora_w), lambda i, p: (0, 0)),
                  vec, cols(LANES), vec, cols(LANES), cols(2 * LANES), vec, vec, vec, vec, vec],
        out_specs=pl.BlockSpec((1, s, LANES), lambda i, p: (i, 0, p)),
        out_shape=jax.ShapeDtypeStruct((bsz, s, RWKV_DIM), BF16),
        scratch_shapes=[seq] * 7 + [pltpu.VMEM((LANES, LANES), F32), mat, mat, mat, mat,
                                    pltpu.VMEM((nchunk, 8, LANES), F32)],
        compiler_params=_params("parallel", "arbitrary"),
        name="rwkv_mix",
    )(u, u, u, u, row(mu[:RWKV_DIM]), row(mu[RWKV_DIM:2 * RWKV_DIM]), row(mu[2 * RWKV_DIM:c3]), row(mu_l),
      row(w0), pad_rows(w2, LANES), row(a0), pad_rows(a2, LANES), pad_rows(g2, 2 * LANES),
      row(k_k), row(k_a), row(gn_g), row(gn_b), row(r_k))


def _mla_prep_kernel(cq_ref, ckv_ref, pe_ref, cos_ref, sin_ref, e_ref, gq_ref, gkn_ref, gkp_ref, invn_ref,
                     gcq_ref, gckv_ref, wuq_ref, wukv_ref, qo_ref, ko_ref, vo_ref, cqn_ref, ckvn_ref, kpe_ref):
    cos, sin = cos_ref[0], sin_ref[0]
    half = MLA_ROPE // 2

    def rope(x, g):
        w = x.shape[1]
        xg = x * _tile_lanes(g, w)
        return xg * _tile_lanes(cos, w) + _swap_halves(xg, half) * _tile_lanes(sin, w)

    @pl.when(pl.program_id(2) == 0)
    def _():
        cqn_ref[...] = _rms(cq_ref[0], gcq_ref[...]).astype(BF16)
        ckvn_ref[...] = _rms(ckv_ref[0], gckv_ref[...]).astype(BF16)
        pe = pe_ref[0]
        inv_pe = lax.rsqrt(jnp.sum(pe * pe, axis=-1, keepdims=True) * (1.0 / MLA_ROPE) + EPS)
        kpe_ref[...] = rope(pe, gkp_ref[...]) * inv_pe

    x = _dot(cqn_ref[...], wuq_ref[...])
    w = x.shape[1]
    inv_n = _tile_lanes(invn_ref[...], w)
    inv = lax.rsqrt(_seg_sum(x * x, e_ref[...]) * inv_n + EPS)
    qo_ref[0] = (rope(x, gq_ref[...]) * inv * ((MLA_NOPE + MLA_ROPE) ** -0.5 * LOG2E)).astype(qo_ref.dtype)
    kv = _dot(ckvn_ref[...], wukv_ref[...])
    inv_k = lax.rsqrt(_seg_sum(kv * kv, e_ref[...]) * inv_n + EPS)
    k_nope = kv * inv_k * _tile_lanes(gkn_ref[...], w)
    ko_ref[0] = (k_nope + _tile_lanes(kpe_ref[...], w)).astype(ko_ref.dtype)
    v = jnp.concatenate([kv[:, h * LANES + MLA_NOPE:(h + 1) * LANES] for h in range(w // LANES)], axis=-1)
    _store_transposed_tiles(vo_ref, v)


def mla_prep(u, cos, sin, gq, gkn, gkp, gcq, gckv, w_uq, w_ukv, *, ts=512, tc=512):
    b, s, _ = u.shape
    wtot = w_uq.shape[1]
    ts = min(ts, s)
    lane = jnp.arange(SEG_W)
    same = (lane[:, None] // LANES == lane[None, :] // LANES)
    pos = lane % LANES
    nope = pos < MLA_NOPE
    pe = (pos >= MLA_NOPE) & (pos < MLA_NOPE + MLA_ROPE)
    seg = (same & ((nope[:, None] & nope[None, :]) | (pe[:, None] & pe[None, :]))).astype(BF16)
    p1 = jnp.arange(LANES)
    inv_n = jnp.where(p1 < MLA_NOPE, 1.0 / MLA_NOPE, jnp.where(p1 < MLA_NOPE + MLA_ROPE, 1.0 / MLA_ROPE, 0.0))
    blk = pl.BlockSpec((1, ts, tc), lambda i, t, c: (i, t, c))
    tab = pl.BlockSpec((1, ts, LANES), lambda i, t, c: (i, t, 0))
    vec = pl.BlockSpec((1, LANES), lambda i, t, c: (0, 0))
    pe_blk = (MLA_Q_RANK + MLA_KV_RANK) // LANES
    out = jax.ShapeDtypeStruct((b, s, wtot), BF16)
    return pl.pallas_call(
        _mla_prep_kernel,
        grid=(b, s // ts, wtot // tc),
        in_specs=[pl.BlockSpec((1, ts, MLA_Q_RANK), lambda i, t, c: (i, t, 0)),
                  pl.BlockSpec((1, ts, MLA_KV_RANK), lambda i, t, c: (i, t, MLA_Q_RANK // MLA_KV_RANK)),
                  pl.BlockSpec((1, ts, LANES), lambda i, t, c: (i, t, pe_blk)), tab, tab,
                  pl.BlockSpec((SEG_W, SEG_W), lambda i, t, c: (0, 0)), vec, vec, vec, vec,
                  pl.BlockSpec((1, MLA_Q_RANK), lambda i, t, c: (0, 0)),
                  pl.BlockSpec((1, MLA_KV_RANK), lambda i, t, c: (0, 0)),
                  pl.BlockSpec((MLA_Q_RANK, tc), lambda i, t, c: (0, c)),
                  pl.BlockSpec((MLA_KV_RANK, tc), lambda i, t, c: (0, c))],
        out_specs=[blk, blk, pl.BlockSpec((1, ts // ATTN_TILE, tc // LANES * MLA_V, ATTN_TILE),
                                          lambda i, t, c: (i, t, c, 0))],
        out_shape=[out, out, jax.ShapeDtypeStruct((b, s // ATTN_TILE, wtot // LANES * MLA_V, ATTN_TILE), BF16)],
        scratch_shapes=[pltpu.VMEM((ts, MLA_Q_RANK), BF16), pltpu.VMEM((ts, MLA_KV_RANK), BF16),
                        pltpu.VMEM((ts, LANES), F32)],
        compiler_params=_params("parallel", "parallel", "arbitrary"),
        name="mla_prep",
    )(u, u, u, cos, sin, seg, gq, gkn, gkp, inv_n.reshape(1, LANES).astype(F32),
      gcq.reshape(1, -1).astype(F32), gckv.reshape(1, -1).astype(F32), w_uq, w_ukv)


def _diff_prep_kernel(q_ref, k_ref, v_ref, cos_ref, sin_ref, e_ref, gq_ref, gk_ref, qo_ref, ko_ref, vo_ref):
    cos, sin = cos_ref[0], sin_ref[0]
    _store_transposed_tiles(vo_ref, v_ref[0])

    def prep(x, g, scale):
        w = x.shape[1]
        xg = x * _tile_lanes(g, w)
        xr = xg * _tile_lanes(cos, w) + _swap_halves(xg, DIFF_QK // 2) * _tile_lanes(sin, w)
        inv = lax.rsqrt(_seg_sum(x * x, e_ref[...]) * (1.0 / DIFF_QK) + EPS)
        return xr * inv * scale

    qo_ref[0] = prep(q_ref[0], gq_ref[...], DIFF_QK ** -0.5 * LOG2E).astype(qo_ref.dtype)
    ko_ref[0] = prep(k_ref[0], gk_ref[...], 1.0).astype(ko_ref.dtype)


def diff_prep(u, cos, sin, gq, gk, *, ts=512, tc=512):
    b, s, _ = u.shape
    ts = min(ts, s)
    wtot = 2 * DIFF_HEADS * DIFF_QK
    q_base = (CD_IN_PAD - 3 * wtot) // tc
    seg = (jnp.arange(SEG_W)[:, None] // DIFF_QK == jnp.arange(SEG_W)[None, :] // DIFF_QK).astype(BF16)
    blk = lambda off: pl.BlockSpec((1, ts, tc), lambda i, t, c: (i, t, off + c))
    tab = pl.BlockSpec((1, ts, LANES), lambda i, t, c: (i, t, 0))
    vec = pl.BlockSpec((1, LANES), lambda i, t, c: (0, 0))
    out = jax.ShapeDtypeStruct((b, s, wtot), BF16)
    return pl.pallas_call(
        _diff_prep_kernel,
        grid=(b, s // ts, wtot // tc),
        in_specs=[blk(q_base), blk(q_base + wtot // tc), blk(q_base + 2 * wtot // tc), tab, tab,
                  pl.BlockSpec((SEG_W, SEG_W), lambda i, t, c: (0, 0)), vec, vec],
        out_specs=[blk(0), blk(0),
                   pl.BlockSpec((1, ts // ATTN_TILE, tc, ATTN_TILE), lambda i, t, c: (i, t, c, 0))],
        out_shape=[out, out, jax.ShapeDtypeStruct((b, s // ATTN_TILE, wtot, ATTN_TILE), BF16)],
        compiler_params=_params("parallel", "parallel", "arbitrary"),
        name="diff_prep",
    )(u, u, u, cos, sin, seg, gq, gk)


def _causal_attn_kernel(lam_ref, q_ref, k_ref, vt_ref, g_ref, o_ref, *, n_sm, tq, ow, out_scale):
    qi = pl.program_id(2)
    q = q_ref[0]
    hps = q.shape[1] // LANES
    slab = lambda x, h: x[:, h * LANES:(h + 1) * LANES]
    lane = lax.broadcasted_iota(jnp.int32, (tq, LANES), 1)
    qs, src = [], []
    for h in range(hps):
        qh = slab(q, h)
        if n_sm == 2:
            zero = jnp.zeros_like(qh)
            qs += [jnp.where(lane < DIFF_QK, qh, zero), jnp.where(lane < DIFF_QK, zero, qh)]
            src += [h, h]
        else:
            qs.append(qh)
            src.append(h)
    nch = len(qs)
    key_i = lax.broadcasted_iota(jnp.int32, (tq, tq), 0)
    qry_i = lax.broadcasted_iota(jnp.int32, (tq, tq), 1)
    ones = jnp.ones((8, tq), BF16)

    def step(j, carry, diagonal):
        kj = k_ref[0, pl.ds(pl.multiple_of(j * tq, tq), tq), :]
        scores = lambda i: _dot_t(slab(kj, src[i]), qs[i])
        new = []
        ahead = [scores(i) for i in range(min(ATTN_LOOKAHEAD, nch))]
        for i in range(nch):
            s = ahead.pop(0)
            if i + ATTN_LOOKAHEAD < nch:
                ahead.append(scores(i + ATTN_LOOKAHEAD))
            if diagonal:
                s = jnp.where(key_i <= qry_i, s, NEG_INF)
            m, l, acc = carry[3 * i:3 * i + 3]
            m_new = jnp.maximum(m, jnp.max(s, axis=0, keepdims=True))
            alpha = jnp.exp2(m - m_new)
            p = jnp.exp2(s - m_new).astype(BF16)
            new += [m_new, alpha * l + _dot(ones, p)[0:1],
                    alpha * acc + _dot(vt_ref[0, j, src[i] * ow:(src[i] + 1) * ow, :], p)]
        return tuple(new)

    init = (jnp.full((1, tq), NEG_INF, F32), jnp.zeros((1, tq), F32),
            jnp.zeros((ow, tq), F32)) * nch
    carry = lax.fori_loop(0, qi, lambda j, cr: step(j, cr, False), init)
    carry = step(qi, carry, True)
    outs = []
    for h in range(hps):
        c0 = 3 * n_sm * h
        o = carry[c0 + 2] / carry[c0 + 1]
        if n_sm == 2:
            o = o - lam_ref[0] * (carry[c0 + 5] / carry[c0 + 4])
            o = o * lax.rsqrt(jnp.mean(o * o, axis=0, keepdims=True) + EPS) * g_ref[...] * out_scale
        outs.append(o.T.astype(o_ref.dtype))
    o_ref[0] = jnp.concatenate(outs, axis=-1)


def causal_attention(q, k, vt, *, n_sm, lam=None, g=None, out_scale=1.0, hps=4, name="causal_attention"):
    b, s, wtot = q.shape
    tq = vt.shape[3]
    width = hps * LANES
    groups = wtot // width
    ow = vt.shape[2] // (wtot // LANES)
    lam = jnp.zeros((1,), F32) if lam is None else lam.reshape(1).astype(F32)
    g = jnp.ones((ow, 1), F32) if g is None else g.reshape(ow, 1).astype(F32)
    seq = pl.BlockSpec((1, s, width), lambda i, h, t: (i, 0, h))
    tile = pl.BlockSpec((1, tq, width), lambda i, h, t: (i, t, h))
    return pl.pallas_call(
        functools.partial(_causal_attn_kernel, n_sm=n_sm, tq=tq, ow=ow, out_scale=out_scale),
        grid=(b, groups, s // tq),
        in_specs=[pl.BlockSpec(memory_space=pltpu.SMEM), tile, seq,
                  pl.BlockSpec((1, s // tq, hps * ow, tq), lambda i, h, t: (i, 0, h, 0)),
                  pl.BlockSpec((ow, 1), lambda i, h, t: (0, 0))],
        out_specs=pl.BlockSpec((1, tq, hps * ow), lambda i, h, t: (i, t, h)),
        out_shape=jax.ShapeDtypeStruct((b, s, groups * hps * ow), BF16),
        compiler_params=_params("parallel", "parallel", "arbitrary"),
        name=name,
    )(lam, q, k, vt, g)


def _memx_kernel(x_ref, ya_ref, yb_ref, wa_ref, wb_ref, g_ref, wq_ref, kv_ref, gq_ref, gk_ref, wo_ref, o_ref):
    x = x_ref[0] + _dot(ya_ref[0], wa_ref[...]) + _dot(yb_ref[0], wb_ref[...])
    q = _dot(_rms(x, g_ref[...]).astype(BF16), wq_ref[...])
    kv = kv_ref[0]
    outs = []
    for h in range(MEM_HEADS):
        sl = slice(h * MEM_HEAD_DIM, (h + 1) * MEM_HEAD_DIM)
        qh = (_rms(q[:, sl], gq_ref[...]) * MEM_HEAD_DIM ** -0.5).astype(BF16)
        kh = _rms(kv[:, sl], gk_ref[...]).astype(BF16)
        vh = kv[:, MEM_W + h * MEM_HEAD_DIM:MEM_W + (h + 1) * MEM_HEAD_DIM].astype(BF16)
        s = _dot_t(qh, kh)
        p = jnp.exp(s - jnp.max(s, axis=-1, keepdims=True))
        outs.append(_dot(p.astype(BF16), vh) / jnp.sum(p, axis=-1, keepdims=True))
    o_ref[0] = x + _dot(jnp.concatenate(outs, axis=-1).astype(BF16), wo_ref[...])


def mix_out_mem_attention(x, ya, yb, wa, wb, mem_kv, g, wq, gq, gk, wo, *, tm=512):
    b, s, d = x.shape
    m = mem_kv.shape[1]
    tm = min(tm, s)
    const = lambda shape: pl.BlockSpec(shape, lambda i, t: (0,) * len(shape), pipeline_mode=pl.Buffered(1))
    tile = lambda w: pl.BlockSpec((1, tm, w), lambda i, t: (i, t, 0))
    return pl.pallas_call(
        _memx_kernel,
        grid=(b, s // tm),
        in_specs=[tile(d), tile(ya.shape[2]), tile(yb.shape[2]), const(wa.shape), const(wb.shape),
                  const((1, d)), const((d, MEM_W)),
                  pl.BlockSpec((1, m, 2 * MEM_W), lambda i, t: (i, 0, 0)),
                  const((1, MEM_HEAD_DIM)), const((1, MEM_HEAD_DIM)), const((MEM_W, d))],
        out_specs=tile(d),
        out_shape=jax.ShapeDtypeStruct((b, s, d), F32),
        compiler_params=_params("parallel", "arbitrary"),
        name="mix_out_mem_attention",
    )(x, ya, yb, wa, wb, g.reshape(1, d).astype(F32), wq, mem_kv, gq.reshape(1, -1).astype(F32),
      gk.reshape(1, -1).astype(F32), wo)


def _rope_tables(positions, dim, lead_ones, tail):
    c, s = _rope_cos_sin(positions, dim)
    shape = positions.shape
    cos = jnp.concatenate([jnp.ones(shape + (lead_ones,), F32), c, c, jnp.ones(shape + (tail,), F32)], axis=-1)
    sin = jnp.concatenate([jnp.zeros(shape + (lead_ones,), F32), -s, s, jnp.zeros(shape + (tail,), F32)], axis=-1)
    reps = LANES // cos.shape[-1]
    return jnp.tile(cos, (1, 1, reps)), jnp.tile(sin, (1, 1, reps))


def _rope_cos_sin(positions, dim):
    inv = 1.0 / (ROPE_THETA ** (jnp.arange(0, HEAD_DIM, 2, dtype=F32) / HEAD_DIM))
    ang = positions.astype(F32)[..., None] * inv
    step = HEAD_DIM // dim
    return jnp.cos(ang)[..., ::step], jnp.sin(ang)[..., ::step]


IN_TILE = 512


def _pad_rows(w, rows):
    return jnp.pad(w, ((0, rows - w.shape[0]), (0, 0)))


def _ab_in_segments(w):
    c = (SWA_HEADS + 2 * SWA_KV_HEADS) * HEAD_DIM + 3 * RWKV_DIM
    wt = w.T.astype(BF16)
    lora = jnp.concatenate([_pad_rows(wt[c:c + DECAY_LORA], LANES),
                            _pad_rows(wt[c + DECAY_LORA:c + DECAY_LORA + AAA_LORA], LANES),
                            _pad_rows(wt[c + DECAY_LORA + AAA_LORA:], 2 * LANES)], axis=0)
    return [wt[:c], lora]


def _cd_in_segments(w):
    c1 = MLA_Q_RANK + MLA_KV_RANK
    wt = w.T.astype(BF16)
    z = lambda n: jnp.zeros((n, wt.shape[1]), wt.dtype)
    mid = jnp.concatenate([wt[MLA_Q_RANK:c1], z(MLA_NOPE), wt[c1:c1 + MLA_ROPE],
                           z(LANES - MLA_NOPE - MLA_ROPE), z(LANES)], axis=0)
    return [wt[:MLA_Q_RANK], mid, wt[c1 + MLA_ROPE:]]


def _head_slabs(w, per_head):
    k = w.shape[0]
    return jnp.pad(w.reshape(k, -1, per_head), ((0, 0), (0, 0), (0, LANES - per_head))).reshape(k, -1)


def _slab_vec(*parts):
    v = jnp.concatenate([p.astype(F32) for p in parts])
    return jnp.pad(v, (0, LANES - v.shape[0])).reshape(1, LANES)


def kernel(x, mem, positions, ffn1_norm, ffn1_w_gate, ffn1_w_up, ffn1_w_down, mix_norm, ab_w_in, ab_w_out, swa_q_norm, swa_k_norm, swa_sinks, rwkv_mu, rwkv_w0, rwkv_w2, rwkv_a0, rwkv_a2, rwkv_g2, rwkv_k_k, rwkv_k_a, rwkv_r_k, rwkv_gn_g, rwkv_gn_b, cd_w_in, cd_w_out, mla_cq_norm, mla_ckv_norm, mla_w_uq, mla_w_ukv, mla_q_nope_norm, mla_k_nope_norm, mla_q_rope_norm, mla_k_rope_norm, diff_q_norm, diff_k_norm, diff_lq1, diff_lk1, diff_lq2, diff_lk2, diff_subln, memx_norm, memx_w_q, memx_q_norm, memx_w_o, mem_norm, mem_w_kv, mem_k_norm, ffn2_norm, ffn2_w_gate, ffn2_w_up, ffn2_w_down):
    b, s, d = x.shape
    m = mem.shape[1]
    t = b * s
    depth = ffn1_norm.shape[0]
    bf = lambda w: w.astype(BF16)
    cos64, sin64 = _rope_tables(positions, HEAD_DIM, 0, 0)
    cos32, sin32 = _rope_tables(positions, MLA_ROPE, MLA_NOPE, LANES - MLA_NOPE - MLA_ROPE)

    mem_kv = norm_matmul(mem.reshape(b * m, d), mem_norm, [bf(mem_w_kv).T],
                         tn=IN_TILE, name="mem_kv").reshape(b, m, 2 * MEM_W)

    x = x.reshape(t, d)
    for layer in range(depth):
        j = layer // 2
        x = ffn(x, ffn1_norm[layer], ffn1_w_gate, ffn1_w_up, ffn1_w_down, layer, name="ffn1")
        if layer % 2 == 0:
            u = norm_matmul(x, mix_norm[layer], _ab_in_segments(ab_w_in[j]), tn=IN_TILE, name="ab_in")
            u = u.reshape(b, s, AB_IN_PAD)
            y_a = swa_attention(u, cos64, sin64, _slab_vec(swa_q_norm[j], swa_q_norm[j]),
                                _slab_vec(swa_k_norm[j], swa_k_norm[j]), swa_sinks[j])
            y_b = rwkv_mix(u, rwkv_mu[j], rwkv_w0[j], rwkv_w2[j], rwkv_a0[j], rwkv_a2[j], rwkv_g2[j],
                           rwkv_k_k[j], rwkv_k_a[j], rwkv_r_k[j], rwkv_gn_g[j], rwkv_gn_b[j])
            mixed, w_out, split = (y_a, y_b), bf(ab_w_out[j]), SWA_HEADS * HEAD_DIM
        else:
            u = norm_matmul(x, mix_norm[layer], _cd_in_segments(cd_w_in[j]), tn=IN_TILE, name="cd_in")
            u = u.reshape(b, s, CD_IN_PAD)
            zero64 = jnp.zeros((MLA_NOPE,), F32)
            q_c, k_c, vt_c = mla_prep(u, cos32, sin32,
                                      _slab_vec(mla_q_nope_norm[j], mla_q_rope_norm[j]),
                                      _slab_vec(mla_k_nope_norm[j]),
                                      _slab_vec(zero64, mla_k_rope_norm[j]),
                                      mla_cq_norm[j], mla_ckv_norm[j],
                                      bf(_head_slabs(mla_w_uq[j], MLA_NOPE + MLA_ROPE)), bf(mla_w_ukv[j]))
            y_c = causal_attention(q_c, k_c, vt_c, n_sm=1, hps=16, name="mla_attention")
            q_d, k_d, vt_d = diff_prep(u, cos64, sin64, _slab_vec(diff_q_norm[j], diff_q_norm[j]),
                                       _slab_vec(diff_k_norm[j], diff_k_norm[j]))
            lambda_init = 0.8 - 0.6 * math.exp(-0.3 * layer)
            lam = (jnp.exp(jnp.sum(diff_lq1[j].astype(F32) * diff_lk1[j].astype(F32)))
                   - jnp.exp(jnp.sum(diff_lq2[j].astype(F32) * diff_lk2[j].astype(F32))) + lambda_init)
            y_d = causal_attention(q_d, k_d, vt_d, n_sm=2, hps=8, lam=lam, g=diff_subln[j],
                                   out_scale=1.0 - lambda_init, name="diff_attention")
            mixed, w_out, split = (y_c, y_d), bf(cd_w_out[j]), MLA_HEADS * MLA_V
        x = mix_out_mem_attention(x.reshape(b, s, d), mixed[0], mixed[1], w_out[:split], w_out[split:], mem_kv,
                                  memx_norm[layer], bf(memx_w_q[layer]), memx_q_norm[layer], mem_k_norm,
                                  bf(memx_w_o[layer])).reshape(t, d)
        x = ffn(x, ffn2_norm[layer], ffn2_w_gate, ffn2_w_up, ffn2_w_down, layer, name="ffn2")
    return x.reshape(b, s, d)
```

```python
import functools
import math

import jax
import jax.numpy as jnp
from jax import lax
from jax.experimental import pallas as pl
from jax.experimental.pallas import tpu as pltpu

F32 = jnp.float32
BF16 = jnp.bfloat16

EPS = 1e-6
ROPE_THETA = 10000.0
NEG_INF = -1e30
LOG2E = math.log2(math.e)
ATTN_LOOKAHEAD = 8
SWA_LOOKAHEAD = 4
ATTN_TILE = 256
SEG_W = 256
LANES = 128

D_MODEL = 2048
D_FF = 5632
HEAD_DIM = 64
SWA_HEADS = 16
SWA_KV_HEADS = 4
SWA_BLOCK = 128
RWKV_DIM = 1024
RWKV_CHUNK = 64
RWKV_GN_EPS = 64e-5
DECAY_LORA, AAA_LORA, GATE_LORA = 64, 64, 160
MLA_HEADS, MLA_Q_RANK, MLA_KV_RANK, MLA_NOPE, MLA_ROPE, MLA_V = 16, 512, 256, 64, 32, 64
DIFF_HEADS, DIFF_QK, DIFF_V = 8, 64, 128
MEM_HEADS, MEM_HEAD_DIM = 4, 128
MEM_W = MEM_HEADS * MEM_HEAD_DIM
AB_IN_PAD = 5120
CD_IN_PAD = 4096

VMEM_LIMIT = 48 * 1024 * 1024
FFN_VMEM_LIMIT = 60 * 1024 * 1024


def _params(*sem):
    return pltpu.CompilerParams(dimension_semantics=sem, vmem_limit_bytes=VMEM_LIMIT)


def _dot(a, b):
    return jnp.dot(a, b, preferred_element_type=F32)


def _dot_t(a, b):
    return lax.dot_general(a, b, (((1,), (1,)), ((), ())), preferred_element_type=F32)


def _dot_0(a, b):
    return lax.dot_general(a, b, (((0,), (0,)), ((), ())), preferred_element_type=F32)


def _rms(x, g):
    return x * lax.rsqrt(jnp.mean(x * x, axis=-1, keepdims=True) + EPS) * g


def _seg_sum(x, e):
    xb = x.astype(BF16)
    w = e.shape[0]
    parts = [_dot(xb[:, i:i + w], e) for i in range(0, x.shape[1], w)]
    return parts[0] if len(parts) == 1 else jnp.concatenate(parts, axis=-1)


def _tile_lanes(v, width):
    return v if v.shape[-1] == width else jnp.tile(v, (1, width // v.shape[-1]))


def _store_transposed_tiles(vt_ref, v):
    for r in range(vt_ref.shape[1]):
        vt_ref[0, r] = v[r * ATTN_TILE:(r + 1) * ATTN_TILE, :].T.astype(vt_ref.dtype)


def _swap_halves(x, half):
    w = x.shape[-1]
    lane = lax.broadcasted_iota(jnp.int32, x.shape, x.ndim - 1)
    low = (lane & (2 * half - 1)) < half
    return jnp.where(low, pltpu.roll(x, w - half, x.ndim - 1), pltpu.roll(x, half, x.ndim - 1))


def _norm_matmul_kernel(x_ref, g_ref, *refs, starts, counts):
    w_refs, o_ref, xn_ref = refs[:len(starts)], refs[len(starts)], refs[len(starts) + 1]
    j = pl.program_id(1)

    @pl.when(j == 0)
    def _():
        xn_ref[...] = _rms(x_ref[...], g_ref[...]).astype(BF16)

    for w_ref, start, count in zip(w_refs, starts, counts):
        @pl.when((j >= start) & (j < start + count))
        def _(w_ref=w_ref):
            o_ref[...] = _dot_t(xn_ref[...], w_ref[...])


def norm_matmul(x, g, segments, *, tm=1024, tn=512, name="norm_matmul"):
    t, k = x.shape
    tm = min(tm, t)
    assert t % tm == 0 and all(w.shape[0] % tn == 0 and w.shape[1] == k for w in segments)
    counts = [w.shape[0] // tn for w in segments]
    starts = [sum(counts[:i]) for i in range(len(counts))]
    w_spec = lambda start, count: pl.BlockSpec((tn, k), lambda i, j: (jnp.clip(j - start, 0, count - 1), 0))
    return pl.pallas_call(
        functools.partial(_norm_matmul_kernel, starts=tuple(starts), counts=tuple(counts)),
        grid=(t // tm, sum(counts)),
        in_specs=[pl.BlockSpec((tm, k), lambda i, j: (i, 0)),
                  pl.BlockSpec((1, k), lambda i, j: (0, 0))]
                 + [w_spec(start, count) for start, count in zip(starts, counts)],
        out_specs=pl.BlockSpec((tm, tn), lambda i, j: (i, j)),
        out_shape=jax.ShapeDtypeStruct((t, sum(counts) * tn), F32),
        scratch_shapes=[pltpu.VMEM((tm, k), BF16)],
        compiler_params=_params("parallel", "arbitrary"),
        name=name,
    )(x, g.reshape(1, k).astype(F32), *segments)


def _ffn_kernel(x_ref, g_ref, wg_ref, wu_ref, wd_ref, o_ref, xn_ref):
    @pl.when(pl.program_id(1) == 0)
    def _():
        x = x_ref[...]
        xn_ref[...] = _rms(x, g_ref[...]).astype(BF16)
        o_ref[...] = x

    xn = xn_ref[...]
    a = _dot(xn, wg_ref[...].astype(BF16))
    b = _dot(xn, wu_ref[...].astype(BF16))
    h = (a * (0.5 / (1.0 + jnp.exp(-a))) * b).astype(BF16)
    o_ref[...] += _dot(h, wd_ref[...].astype(BF16))


def ffn(x, g, wg, wu, wd, layer, *, tm=1024, tf=256, name="ffn"):
    t, d = x.shape
    ff = wg.shape[2]
    tm = min(tm, t)
    assert t % tm == 0 and ff % tf == 0
    nf = ff // tf
    return pl.pallas_call(
        _ffn_kernel,
        grid=(t // tm, nf),
        in_specs=[pl.BlockSpec((tm, d), lambda i, f: (i, 0)),
                  pl.BlockSpec((1, d), lambda i, f: (0, 0)),
                  pl.BlockSpec((None, d, tf), lambda i, f: (layer, 0, f)),
                  pl.BlockSpec((None, d, tf), lambda i, f: (layer, 0, f)),
                  pl.BlockSpec((None, tf, d), lambda i, f: (layer, f, 0))],
        out_specs=pl.BlockSpec((tm, d), lambda i, f: (i, 0)),
        out_shape=jax.ShapeDtypeStruct((t, d), F32),
        scratch_shapes=[pltpu.VMEM((tm, d), BF16)],
        compiler_params=pltpu.CompilerParams(dimension_semantics=("parallel", "arbitrary"),
                                             vmem_limit_bytes=FFN_VMEM_LIMIT),
        name=name,
    )(x, g.reshape(1, d).astype(F32), wg, wu, wd)


def _swa_kernel(sink_ref, q_ref, kc_ref, kp_ref, vc_ref, vp_ref, cc_ref, sc_ref, cp_ref, sp_ref,
                gq_ref, gk_ref, e_ref, o_ref):
    n = pl.program_id(1)
    blk = SWA_BLOCK
    group = SWA_HEADS // SWA_KV_HEADS
    q = q_ref[0]
    k = jnp.concatenate([kp_ref[0], kc_ref[0]], axis=0)
    v = jnp.concatenate([vp_ref[0], vc_ref[0]], axis=0)
    cos_q, sin_q = cc_ref[0], sc_ref[0]
    cos_k = jnp.concatenate([cp_ref[0], cos_q], axis=0)
    sin_k = jnp.concatenate([sp_ref[0], sin_q], axis=0)
    def prep(x, g, cos, sin, scale):
        w = x.shape[1]
        xg = x * _tile_lanes(g, w)
        xr = xg * _tile_lanes(cos, w) + _swap_halves(xg, HEAD_DIM // 2) * _tile_lanes(sin, w)
        return xr * lax.rsqrt(_seg_sum(x * x, e_ref[...]) * (1.0 / HEAD_DIM) + EPS) * scale

    qr = prep(q, gq_ref[...], cos_q, sin_q, HEAD_DIM ** -0.5 * LOG2E).astype(BF16)
    kr = prep(k, gk_ref[...], cos_k, sin_k, 1.0)
    key_i = lax.broadcasted_iota(jnp.int32, (2 * blk, blk), 0)
    qry_i = lax.broadcasted_iota(jnp.int32, (2 * blk, blk), 1)
    rel = qry_i + blk - key_i
    valid = (rel >= 0) & (rel < blk) & ((n > 0) | (key_i >= blk))
    low = lax.broadcasted_iota(jnp.int32, (blk, LANES), 1) < HEAD_DIM
    zero = jnp.zeros((blk, LANES), BF16)
    k_dup, v_t = [], []
    for g in range(SWA_KV_HEADS):
        kg = kr[:, g * HEAD_DIM:(g + 1) * HEAD_DIM]
        k_dup.append(jnp.concatenate([kg, kg], axis=-1).astype(BF16))
        v_t.append(v[:, g * HEAD_DIM:(g + 1) * HEAD_DIM].T.astype(BF16))

    def scores(h):
        slab = qr[:, (h // 2) * LANES:(h // 2 + 1) * LANES]
        qh = jnp.where(low, slab, zero) if h % 2 == 0 else jnp.where(low, zero, slab)
        return _dot_t(k_dup[h // group], qh)

    ahead = [scores(h) for h in range(SWA_LOOKAHEAD)]
    ones = jnp.ones((8, 2 * blk), BF16)
    outs = []
    for h in range(SWA_HEADS):
        s = jnp.where(valid, ahead.pop(0), NEG_INF)
        if h + SWA_LOOKAHEAD < SWA_HEADS:
            ahead.append(scores(h + SWA_LOOKAHEAD))
        sink = sink_ref[h] * LOG2E
        m = jnp.maximum(jnp.max(s, axis=0, keepdims=True), sink)
        p = jnp.exp2(s - m).astype(BF16)
        den = _dot(ones, p)[0:1] + jnp.exp2(sink - m)
        outs.append(_dot(v_t[h // group], p) / den)
    slabs = [jnp.concatenate(outs[i:i + 2], axis=0).T for i in range(0, SWA_HEADS, 2)]
    o_ref[0] = jnp.concatenate(slabs, axis=-1).astype(o_ref.dtype)


def swa_attention(u, cos, sin, gq, gk, sinks):
    b, s, _ = u.shape
    nb = s // SWA_BLOCK
    qw, kw = SWA_HEADS * HEAD_DIM, SWA_KV_HEADS * HEAD_DIM
    cur = lambda c: (lambda i, n: (i, n, c))
    prev = lambda c: (lambda i, n: (i, jnp.maximum(n - 1, 0), c))
    tab = pl.BlockSpec((1, SWA_BLOCK, LANES), cur(0))
    tab_prev = pl.BlockSpec((1, SWA_BLOCK, LANES), prev(0))
    gain = pl.BlockSpec((1, LANES), lambda i, n: (0, 0))
    seg = (jnp.arange(kw)[:, None] // HEAD_DIM == jnp.arange(kw)[None, :] // HEAD_DIM).astype(BF16)
    return pl.pallas_call(
        _swa_kernel,
        grid=(b, nb),
        in_specs=[pl.BlockSpec(memory_space=pltpu.SMEM),
                  pl.BlockSpec((1, SWA_BLOCK, qw), cur(0)),
                  pl.BlockSpec((1, SWA_BLOCK, kw), cur(qw // kw)),
                  pl.BlockSpec((1, SWA_BLOCK, kw), prev(qw // kw)),
                  pl.BlockSpec((1, SWA_BLOCK, kw), cur(qw // kw + 1)),
                  pl.BlockSpec((1, SWA_BLOCK, kw), prev(qw // kw + 1)),
                  tab, tab, tab_prev, tab_prev, gain, gain, pl.BlockSpec((kw, kw), lambda i, n: (0, 0))],
        out_specs=pl.BlockSpec((1, SWA_BLOCK, qw), cur(0)),
        out_shape=jax.ShapeDtypeStruct((b, s, qw), BF16),
        compiler_params=_params("parallel", "arbitrary"),
        name="swa_attention",
    )(sinks.astype(F32), u, u, u, u, u, cos, sin, cos, sin, gq, gk, seg)


def _mm(a, b, dims, passes):
    dn = (dims, ((), ()))
    dg = lambda x, y: lax.dot_general(x, y, dn, preferred_element_type=F32)
    ah = a.astype(BF16)
    bh = b.astype(BF16)
    if passes == 1:
        return dg(ah, bh)
    al = (a - ah.astype(F32)).astype(BF16)
    bl = (b - bh.astype(F32)).astype(BF16)
    return dg(ah, bh) + dg(ah, bl) + dg(al, bh)


_NN = ((1,), (0,))
_NT = ((1,), (1,))
_TN = ((0,), (0,))
P_SC, P_INV, P_PQ, P_OUT, P_ST = 1, 1, 1, 1, 1
RWKV_UNROLL = 8


RWKV_PREP_ROWS = 256


def _rwkv_kernel(ur_ref, uk_ref, uv_ref, ul_ref, mur_ref, muk_ref, muv_ref, mul_ref, w0_ref, w2_ref, a0_ref,
                 a2_ref, g2_ref, kkw_ref, ka_ref, gng_ref, gnb_ref, rk_ref,
                 o_ref, r_ref, k_ref, v_ref, kk_ref, b_ref, lw_ref, g_ref,
                 st_ref, y1_ref, y0_ref, n_ref, z_ref, dec_ref, *, nchunk):
    c = RWKV_CHUNK
    lane_c = lax.broadcasted_iota(jnp.int32, (c, LANES), 1)
    head0 = lane_c < HEAD_DIM
    ri = lax.broadcasted_iota(jnp.int32, (2 * c, 2 * c), 0)
    ci = lax.broadcasted_iota(jnp.int32, (2 * c, 2 * c), 1)
    eye = jnp.where(ri == ci, 1.0, 0.0)
    tril_c = jnp.where(lax.broadcasted_iota(jnp.int32, (c, c), 0) >= lax.broadcasted_iota(jnp.int32, (c, c), 1),
                       1.0, 0.0).astype(BF16)
    stack = lambda x: jnp.concatenate([jnp.where(head0, x, 0.0), jnp.where(head0, 0.0, x)], axis=0)

    def seg_mean(x):
        first = lax.broadcasted_iota(jnp.int32, x.shape, 1) < HEAD_DIM
        m0 = jnp.sum(jnp.where(first, x, 0.0), axis=-1, keepdims=True)
        m1 = jnp.sum(jnp.where(first, 0.0, x), axis=-1, keepdims=True)
        return jnp.where(first, m0, m1) * (1.0 / HEAD_DIM)

    def prep(io, first_group):
        group_rows = RWKV_UNROLL * c
        for tix in range(group_rows // RWKV_PREP_ROWS):
            start = pl.multiple_of(io * group_rows + tix * RWKV_PREP_ROWS, RWKV_PREP_ROWS)
            rows = pl.ds(start, RWKV_PREP_ROWS)
            at_start = first_group and tix == 0

            def shifted(ref, mu_ref):
                x = ref[0, rows, :]
                if at_start:
                    last = jnp.zeros((1, x.shape[1]), F32)
                else:
                    last = ref[0, pl.ds(pl.multiple_of(start - 8, 8), 8), :][7:8, :]
                row = lax.broadcasted_iota(jnp.int32, x.shape, 0)
                prev = jnp.where(row == 0, last, pltpu.roll(x, 1, 0))
                return x + (prev - x) * mu_ref[...]

            r = shifted(ur_ref, mur_ref)
            k = shifted(uk_ref, muk_ref)
            v = shifted(uv_ref, muv_ref)
            lo = shifted(ul_ref, mul_ref)
            yield
            w_lo, a_lo, g_lo = lo[:, 0:LANES], lo[:, LANES:2 * LANES], lo[:, 2 * LANES:4 * LANES]
            z = -(w0_ref[...] + _mm(jnp.tanh(w_lo), w2_ref[...], _NN, 3))
            w = -(jnp.maximum(z, 0.0) + jnp.log(1.0 + jnp.exp(-jnp.abs(z)))) - 0.5
            a = 1.0 / (1.0 + jnp.exp(-(a0_ref[...] + _mm(a_lo, a2_ref[...], _NN, 1))))
            g = _mm(1.0 / (1.0 + jnp.exp(-g_lo)), g2_ref[...], _NN, 1)
            yield
            kk = k * kkw_ref[...]
            kk = kk / jnp.maximum(jnp.sqrt(seg_mean(kk * kk) * float(HEAD_DIM)), 1e-12)
            r_ref[rows, :] = r
            k_ref[rows, :] = k * (1.0 + (a - 1.0) * ka_ref[...])
            v_ref[rows, :] = v
            kk_ref[rows, :] = kk
            b_ref[rows, :] = kk * a
            lw_ref[rows, :] = -jnp.exp(w)
            g_ref[rows, :] = g
            yield

    def build(ics):
        each = lambda f, *cols: [f(*args) for args in zip(*cols)]
        sls = [pl.ds(pl.multiple_of(ic * c, c), c) for ic in ics]
        load = lambda ref: [ref[sl, :] for sl in sls]
        r, k, v, kk, b, lw = (load(ref) for ref in (r_ref, k_ref, v_ref, kk_ref, b_ref, lw_ref))

        def running_sum(x):
            l1 = x.astype(BF16)
            rest = x - l1.astype(F32)
            l2 = rest.astype(BF16)
            l3 = (rest - l2.astype(F32)).astype(BF16)
            return _dot(tril_c, l1) + _dot(tril_c, l2) + _dot(tril_c, l3)

        cum = each(running_sum, lw)
        yield
        cum_end = [x[c - 1:c, :] for x in cum]
        e_neg = each(lambda x: jnp.exp(-x), cum)
        e_end = each(lambda x, xe: jnp.exp(xe - x), cum, cum_end)
        a_s = each(lambda kk_, x, l: stack(-kk_ * jnp.exp(x - l)), kk, cum, lw)
        r_s = each(lambda r_, x: stack(r_ * jnp.exp(x)), r, cum)
        b_s = each(lambda b_, e: stack(b_ * e), b, e_neg)
        k_s = each(lambda k_, e: stack(k_ * e), k, e_neg)
        bh_s = each(lambda b_, e: stack(b_ * e), b, e_end)
        kh_s = each(lambda k_, e: stack(k_ * e), k, e_end)
        v_s = each(stack, v)
        n2 = 2 * c
        sc = each(lambda a_, r_, b_, k_: _mm(jnp.concatenate([a_, r_], axis=0),
                                             jnp.concatenate([b_, k_], axis=0), _NT, P_SC), a_s, r_s, b_s, k_s)
        low = [jnp.where(ri > ci, x[:n2, :n2], 0.0) for x in sc]
        a_ak = [jnp.where(ri > ci, x[:n2, n2:], 0.0) for x in sc]
        a_rb = [jnp.where(ri >= ci, x[n2:, :n2], 0.0) for x in sc]
        a_rk = [jnp.where(ri >= ci, x[n2:, n2:], 0.0) for x in sc]
        yield
        inv = [eye + x for x in low]
        pw = each(lambda x: _mm(x, x, _NN, P_INV), low)
        yield
        levels = 5
        for lvl in range(levels - 1):
            both = each(lambda t, x: _mm(jnp.concatenate([t, x], axis=0), x, _NN, P_INV), inv, pw)
            inv = each(lambda t, r: t + r[:n2], inv, both)
            pw = [r[n2:] for r in both]
            yield
        inv = each(lambda t, x: t + _mm(t, x, _NN, P_INV), inv, pw)
        yield
        akv = each(lambda x, y: _mm(x, y, _NN, P_PQ), a_ak, v_s)
        yield
        pq = each(lambda t, x, y: _mm(t, jnp.concatenate([x, y], axis=1), _NN, P_PQ), inv, a_s, akv)
        yield
        yy = each(lambda x, y: _mm(x, y, _NN, P_OUT), a_rb, pq)
        y0b = each(lambda x, y: _mm(x, y, _NN, P_OUT), a_rk, v_s)
        nz = each(lambda x, y: _mm(x, y, _TN, P_OUT), pq, bh_s)
        zb = each(lambda x, y: _mm(x, y, _TN, P_OUT), v_s, kh_s)
        for i, ic in enumerate(ics):
            y1_ref[ic] = r_s[i] + yy[i][:, :LANES]
            y0_ref[ic] = yy[i][:, LANES:] + y0b[i]
            n_ref[ic] = nz[i][:LANES]
            z_ref[ic] = nz[i][LANES:] + zb[i]
            dec_ref[ic] = jnp.broadcast_to(jnp.exp(cum_end[i]), (8, LANES))

    def emit(ic, st):
        sl = pl.ds(pl.multiple_of(ic * c, c), c)
        r, k, v, g = r_ref[sl, :], k_ref[sl, :], v_ref[sl, :], g_ref[sl, :]
        y_st = _mm(y1_ref[ic], st, _NT, P_ST) + y0_ref[ic]
        y = y_st[0:c] + y_st[c:2 * c]
        mean = seg_mean(y)
        var = seg_mean((y - mean) * (y - mean))
        yn = (y - mean) * lax.rsqrt(var + RWKV_GN_EPS) * gng_ref[...] + gnb_ref[...]
        bonus = seg_mean(r * k * rk_ref[...]) * float(HEAD_DIM) * v
        o_ref[0, sl, :] = ((yn + bonus) * g).astype(o_ref.dtype)
        return st * dec_ref[ic][0:1, :] + _mm(st, n_ref[ic], _NN, P_ST) + z_ref[ic]

    def scan(ics):
        st = st_ref[...]
        for ic in ics:
            st = emit(ic, st)
            yield
        st_ref[...] = st

    def run(*gens):
        live = list(gens)
        while live:
            live = [gen for gen in live if next(gen, live) is not live]

    group = lambda io: [io * RWKV_UNROLL + i for i in range(RWKV_UNROLL)]
    ngroup = nchunk // RWKV_UNROLL
    st_ref[...] = jnp.zeros_like(st_ref)
    run(prep(0, True))
    if ngroup > 1:
        run(prep(1, False), build(group(0)))
    else:
        run(build(group(0)))

    @pl.loop(1, ngroup - 1)
    def _(io):
        run(prep(io + 1, False), build(group(io)), scan(group(io - 1)))

    if ngroup > 1:
        run(build(group(ngroup - 1)), scan(group(ngroup - 2)))
    run(scan(group(ngroup - 1)))


def rwkv_mix(u, mu, w0, w2, a0, a2, g2, k_k, k_a, r_k, gn_g, gn_b):
    bsz, s, _ = u.shape
    npair = RWKV_DIM // LANES
    nchunk = s // RWKV_CHUNK
    assert nchunk % RWKV_UNROLL == 0 and (RWKV_UNROLL * RWKV_CHUNK) % RWKV_PREP_ROWS == 0
    base = (SWA_HEADS + 2 * SWA_KV_HEADS) * HEAD_DIM // LANES
    lora_w = 4 * LANES
    slab = lambda off: pl.BlockSpec((1, s, LANES), lambda i, p: (i, 0, off + p))
    vec = pl.BlockSpec((1, LANES), lambda i, p: (0, p))
    cols = lambda rows: pl.BlockSpec((rows, LANES), lambda i, p: (0, p))
    row = lambda vv: vv.reshape(1, -1).astype(F32)
    pad_rows = lambda m, rows: jnp.pad(m, ((0, rows - m.shape[0]), (0, 0))).astype(F32)
    pad_cols = lambda vv, n: jnp.pad(vv, (0, n - vv.shape[0]))
    c3 = 3 * RWKV_DIM
    mu_l = jnp.concatenate([pad_cols(mu[c3:c3 + DECAY_LORA], LANES),
                            pad_cols(mu[c3 + DECAY_LORA:c3 + DECAY_LORA + AAA_LORA], LANES),
                            pad_cols(mu[c3 + DECAY_LORA + AAA_LORA:], 2 * LANES)])
    seq = pltpu.VMEM((s, LANES), F32)
    mat = pltpu.VMEM((nchunk, LANES, LANES), F32)
    return pl.pallas_call(
        functools.partial(_rwkv_kernel, nchunk=nchunk),
        grid=(bsz, npair),
        in_specs=[slab(base), slab(base + npair), slab(base + 2 * npair),
                  pl.BlockSpec((1, s, lora_w), lambda i, p: (i, 0, (base + 3 * npair) * LANES // lora_w)),
                  vec, vec, vec, pl.BlockSpec((1, lora_w), lambda i, p: (0, 0)),
                  vec, cols(LANES), vec, cols(LANES), cols(2 * LANES), vec, vec, vec, vec, vec],
        out_specs=pl.BlockSpec((1, s, LANES), lambda i, p: (i, 0, p)),
        out_shape=jax.ShapeDtypeStruct((bsz, s, RWKV_DIM), BF16),
        scratch_shapes=[seq] * 7 + [pltpu.VMEM((LANES, LANES), F32), mat, mat, mat, mat,
                                    pltpu.VMEM((nchunk, 8, LANES), F32)],
        compiler_params=_params("parallel", "arbitrary"),
        name="rwkv_mix",
    )(u, u, u, u, row(mu[:RWKV_DIM]), row(mu[RWKV_DIM:2 * RWKV_DIM]), row(mu[2 * RWKV_DIM:c3]), row(mu_l),
      row(w0), pad_rows(w2, LANES), row(a0), pad_rows(a2, LANES), pad_rows(g2, 2 * LANES),
      row(k_k), row(k_a), row(gn_g), row(gn_b), row(r_k))


def _mla_prep_kernel(cq_ref, ckv_ref, pe_ref, cos_ref, sin_ref, e_ref, gq_ref, gkn_ref, gkp_ref, invn_ref,
                     gcq_ref, gckv_ref, wuq_ref, wukv_ref, qo_ref, ko_ref, vo_ref, cqn_ref, ckvn_ref, kpe_ref):
    cos, sin = cos_ref[0], sin_ref[0]
    half = MLA_ROPE // 2

    def rope(x, g):
        w = x.shape[1]
        xg = x * _tile_lanes(g, w)
        return xg * _tile_lanes(cos, w) + _swap_halves(xg, half) * _tile_lanes(sin, w)

    @pl.when(pl.program_id(2) == 0)
    def _():
        cqn_ref[...] = _rms(cq_ref[0], gcq_ref[...]).astype(BF16)
        ckvn_ref[...] = _rms(ckv_ref[0], gckv_ref[...]).astype(BF16)
        pe = pe_ref[0]
        inv_pe = lax.rsqrt(jnp.sum(pe * pe, axis=-1, keepdims=True) * (1.0 / MLA_ROPE) + EPS)
        kpe_ref[...] = rope(pe, gkp_ref[...]) * inv_pe

    x = _dot(cqn_ref[...], wuq_ref[...])
    w = x.shape[1]
    inv_n = _tile_lanes(invn_ref[...], w)
    inv = lax.rsqrt(_seg_sum(x * x, e_ref[...]) * inv_n + EPS)
    qo_ref[0] = (rope(x, gq_ref[...]) * inv * ((MLA_NOPE + MLA_ROPE) ** -0.5 * LOG2E)).astype(qo_ref.dtype)
    kv = _dot(ckvn_ref[...], wukv_ref[...])
    inv_k = lax.rsqrt(_seg_sum(kv * kv, e_ref[...]) * inv_n + EPS)
    k_nope = kv * inv_k * _tile_lanes(gkn_ref[...], w)
    ko_ref[0] = (k_nope + _tile_lanes(kpe_ref[...], w)).astype(ko_ref.dtype)
    v = jnp.concatenate([kv[:, h * LANES + MLA_NOPE:(h + 1) * LANES] for h in range(w // LANES)], axis=-1)
    _store_transposed_tiles(vo_ref, v)


def mla_prep(u, cos, sin, gq, gkn, gkp, gcq, gckv, w_uq, w_ukv, *, ts=512, tc=512):
    b, s, _ = u.shape
    wtot = w_uq.shape[1]
    ts = min(ts, s)
    lane = jnp.arange(SEG_W)
    same = (lane[:, None] // LANES == lane[None, :] // LANES)
    pos = lane % LANES
    nope = pos < MLA_NOPE
    pe = (pos >= MLA_NOPE) & (pos < MLA_NOPE + MLA_ROPE)
    seg = (same & ((nope[:, None] & nope[None, :]) | (pe[:, None] & pe[None, :]))).astype(BF16)
    p1 = jnp.arange(LANES)
    inv_n = jnp.where(p1 < MLA_NOPE, 1.0 / MLA_NOPE, jnp.where(p1 < MLA_NOPE + MLA_ROPE, 1.0 / MLA_ROPE, 0.0))
    blk = pl.BlockSpec((1, ts, tc), lambda i, t, c: (i, t, c))
    tab = pl.BlockSpec((1, ts, LANES), lambda i, t, c: (i, t, 0))
    vec = pl.BlockSpec((1, LANES), lambda i, t, c: (0, 0))
    pe_blk = (MLA_Q_RANK + MLA_KV_RANK) // LANES
    out = jax.ShapeDtypeStruct((b, s, wtot), BF16)
    return pl.pallas_call(
        _mla_prep_kernel,
        grid=(b, s // ts, wtot // tc),
        in_specs=[pl.BlockSpec((1, ts, MLA_Q_RANK), lambda i, t, c: (i, t, 0)),
                  pl.BlockSpec((1, ts, MLA_KV_RANK), lambda i, t, c: (i, t, MLA_Q_RANK // MLA_KV_RANK)),
                  pl.BlockSpec((1, ts, LANES), lambda i, t, c: (i, t, pe_blk)), tab, tab,
                  pl.BlockSpec((SEG_W, SEG_W), lambda i, t, c: (0, 0)), vec, vec, vec, vec,
                  pl.BlockSpec((1, MLA_Q_RANK), lambda i, t, c: (0, 0)),
                  pl.BlockSpec((1, MLA_KV_RANK), lambda i, t, c: (0, 0)),
                  pl.BlockSpec((MLA_Q_RANK, tc), lambda i, t, c: (0, c)),
                  pl.BlockSpec((MLA_KV_RANK, tc), lambda i, t, c: (0, c))],
        out_specs=[blk, blk, pl.BlockSpec((1, ts // ATTN_TILE, tc // LANES * MLA_V, ATTN_TILE),
                                          lambda i, t, c: (i, t, c, 0))],
        out_shape=[out, out, jax.ShapeDtypeStruct((b, s // ATTN_TILE, wtot // LANES * MLA_V, ATTN_TILE), BF16)],
        scratch_shapes=[pltpu.VMEM((ts, MLA_Q_RANK), BF16), pltpu.VMEM((ts, MLA_KV_RANK), BF16),
                        pltpu.VMEM((ts, LANES), F32)],
        compiler_params=_params("parallel", "parallel", "arbitrary"),
        name="mla_prep",
    )(u, u, u, cos, sin, seg, gq, gkn, gkp, inv_n.reshape(1, LANES).astype(F32),
      gcq.reshape(1, -1).astype(F32), gckv.reshape(1, -1).astype(F32), w_uq, w_ukv)


def _diff_prep_kernel(q_ref, k_ref, v_ref, cos_ref, sin_ref, e_ref, gq_ref, gk_ref, qo_ref, ko_ref, vo_ref):
    cos, sin = cos_ref[0], sin_ref[0]
    _store_transposed_tiles(vo_ref, v_ref[0])

    def prep(x, g, scale):
        w = x.shape[1]
        xg = x * _tile_lanes(g, w)
        xr = xg * _tile_lanes(cos, w) + _swap_halves(xg, DIFF_QK // 2) * _tile_lanes(sin, w)
        inv = lax.rsqrt(_seg_sum(x * x, e_ref[...]) * (1.0 / DIFF_QK) + EPS)
        return xr * inv * scale

    qo_ref[0] = prep(q_ref[0], gq_ref[...], DIFF_QK ** -0.5 * LOG2E).astype(qo_ref.dtype)
    ko_ref[0] = prep(k_ref[0], gk_ref[...], 1.0).astype(ko_ref.dtype)


def diff_prep(u, cos, sin, gq, gk, *, ts=512, tc=512):
    b, s, _ = u.shape
    ts = min(ts, s)
    wtot = 2 * DIFF_HEADS * DIFF_QK
    q_base = (CD_IN_PAD - 3 * wtot) // tc
    seg = (jnp.arange(SEG_W)[:, None] // DIFF_QK == jnp.arange(SEG_W)[None, :] // DIFF_QK).astype(BF16)
    blk = lambda off: pl.BlockSpec((1, ts, tc), lambda i, t, c: (i, t, off + c))
    tab = pl.BlockSpec((1, ts, LANES), lambda i, t, c: (i, t, 0))
    vec = pl.BlockSpec((1, LANES), lambda i, t, c: (0, 0))
    out = jax.ShapeDtypeStruct((b, s, wtot), BF16)
    return pl.pallas_call(
        _diff_prep_kernel,
        grid=(b, s // ts, wtot // tc),
        in_specs=[blk(q_base), blk(q_base + wtot // tc), blk(q_base + 2 * wtot // tc), tab, tab,
                  pl.BlockSpec((SEG_W, SEG_W), lambda i, t, c: (0, 0)), vec, vec],
        out_specs=[blk(0), blk(0),
                   pl.BlockSpec((1, ts // ATTN_TILE, tc, ATTN_TILE), lambda i, t, c: (i, t, c, 0))],
        out_shape=[out, out, jax.ShapeDtypeStruct((b, s // ATTN_TILE, wtot, ATTN_TILE), BF16)],
        compiler_params=_params("parallel", "parallel", "arbitrary"),
        name="diff_prep",
    )(u, u, u, cos, sin, seg, gq, gk)


def _causal_attn_kernel(lam_ref, q_ref, k_ref, vt_ref, g_ref, o_ref, *, n_sm, tq, ow, out_scale):
    qi = pl.program_id(2)
    q = q_ref[0]
    hps = q.shape[1] // LANES
    slab = lambda x, h: x[:, h * LANES:(h + 1) * LANES]
    lane = lax.broadcasted_iota(jnp.int32, (tq, LANES), 1)
    qs, src = [], []
    for h in range(hps):
        qh = slab(q, h)
        if n_sm == 2:
            zero = jnp.zeros_like(qh)
            qs += [jnp.where(lane < DIFF_QK, qh, zero), jnp.where(lane < DIFF_QK, zero, qh)]
            src += [h, h]
        else:
            qs.append(qh)
            src.append(h)
    nch = len(qs)
    key_i = lax.broadcasted_iota(jnp.int32, (tq, tq), 0)
    qry_i = lax.broadcasted_iota(jnp.int32, (tq, tq), 1)
    ones = jnp.ones((8, tq), BF16)

    def step(j, carry, diagonal):
        kj = k_ref[0, pl.ds(pl.multiple_of(j * tq, tq), tq), :]
        scores = lambda i: _dot_t(slab(kj, src[i]), qs[i])
        new = []
        ahead = [scores(i) for i in range(min(ATTN_LOOKAHEAD, nch))]
        for i in range(nch):
            s = ahead.pop(0)
            if i + ATTN_LOOKAHEAD < nch:
                ahead.append(scores(i + ATTN_LOOKAHEAD))
            if diagonal:
                s = jnp.where(key_i <= qry_i, s, NEG_INF)
            m, l, acc = carry[3 * i:3 * i + 3]
            m_new = jnp.maximum(m, jnp.max(s, axis=0, keepdims=True))
            alpha = jnp.exp2(m - m_new)
            p = jnp.exp2(s - m_new).astype(BF16)
            new += [m_new, alpha * l + _dot(ones, p)[0:1],
                    alpha * acc + _dot(vt_ref[0, j, src[i] * ow:(src[i] + 1) * ow, :], p)]
        return tuple(new)

    init = (jnp.full((1, tq), NEG_INF, F32), jnp.zeros((1, tq), F32),
            jnp.zeros((ow, tq), F32)) * nch
    carry = lax.fori_loop(0, qi, lambda j, cr: step(j, cr, False), init)
    carry = step(qi, carry, True)
    outs = []
    for h in range(hps):
        c0 = 3 * n_sm * h
        o = carry[c0 + 2] / carry[c0 + 1]
        if n_sm == 2:
            o = o - lam_ref[0] * (carry[c0 + 5] / carry[c0 + 4])
            o = o * lax.rsqrt(jnp.mean(o * o, axis=0, keepdims=True) + EPS) * g_ref[...] * out_scale
        outs.append(o.T.astype(o_ref.dtype))
    o_ref[0] = jnp.concatenate(outs, axis=-1)


def causal_attention(q, k, vt, *, n_sm, lam=None, g=None, out_scale=1.0, hps=4, name="causal_attention"):
    b, s, wtot = q.shape
    tq = vt.shape[3]
    width = hps * LANES
    groups = wtot // width
    ow = vt.shape[2] // (wtot // LANES)
    lam = jnp.zeros((1,), F32) if lam is None else lam.reshape(1).astype(F32)
    g = jnp.ones((ow, 1), F32) if g is None else g.reshape(ow, 1).astype(F32)
    seq = pl.BlockSpec((1, s, width), lambda i, h, t: (i, 0, h))
    tile = pl.BlockSpec((1, tq, width), lambda i, h, t: (i, t, h))
    return pl.pallas_call(
        functools.partial(_causal_attn_kernel, n_sm=n_sm, tq=tq, ow=ow, out_scale=out_scale),
        grid=(b, groups, s // tq),
        in_specs=[pl.BlockSpec(memory_space=pltpu.SMEM), tile, seq,
                  pl.BlockSpec((1, s // tq, hps * ow, tq), lambda i, h, t: (i, 0, h, 0)),
                  pl.BlockSpec((ow, 1), lambda i, h, t: (0, 0))],
        out_specs=pl.BlockSpec((1, tq, hps * ow), lambda i, h, t: (i, t, h)),
        out_shape=jax.ShapeDtypeStruct((b, s, groups * hps * ow), BF16),
        compiler_params=_params("parallel", "parallel", "arbitrary"),
        name=name,
    )(lam, q, k, vt, g)


def _memx_kernel(x_ref, ya_ref, yb_ref, wa_ref, wb_ref, g_ref, wq_ref, kv_ref, gq_ref, gk_ref, wo_ref, o_ref):
    x = x_ref[0] + _dot(ya_ref[0], wa_ref[...]) + _dot(yb_ref[0], wb_ref[...])
    q = _dot(_rms(x, g_ref[...]).astype(BF16), wq_ref[...])
    kv = kv_ref[0]
    outs = []
    for h in range(MEM_HEADS):
        sl = slice(h * MEM_HEAD_DIM, (h + 1) * MEM_HEAD_DIM)
        qh = (_rms(q[:, sl], gq_ref[...]) * MEM_HEAD_DIM ** -0.5).astype(BF16)
        kh = _rms(kv[:, sl], gk_ref[...]).astype(BF16)
        vh = kv[:, MEM_W + h * MEM_HEAD_DIM:MEM_W + (h + 1) * MEM_HEAD_DIM].astype(BF16)
        s = _dot_t(qh, kh)
        p = jnp.exp(s - jnp.max(s, axis=-1, keepdims=True))
        outs.append(_dot(p.astype(BF16), vh) / jnp.sum(p, axis=-1, keepdims=True))
    o_ref[0] = x + _dot(jnp.concatenate(outs, axis=-1).astype(BF16), wo_ref[...])


def mix_out_mem_attention(x, ya, yb, wa, wb, mem_kv, g, wq, gq, gk, wo, *, tm=512):
    b, s, d = x.shape
    m = mem_kv.shape[1]
    tm = min(tm, s)
    const = lambda shape: pl.BlockSpec(shape, lambda i, t: (0,) * len(shape), pipeline_mode=pl.Buffered(1))
    tile = lambda w: pl.BlockSpec((1, tm, w), lambda i, t: (i, t, 0))
    return pl.pallas_call(
        _memx_kernel,
        grid=(b, s // tm),
        in_specs=[tile(d), tile(ya.shape[2]), tile(yb.shape[2]), const(wa.shape), const(wb.shape),
                  const((1, d)), const((d, MEM_W)),
                  pl.BlockSpec((1, m, 2 * MEM_W), lambda i, t: (i, 0, 0)),
                  const((1, MEM_HEAD_DIM)), const((1, MEM_HEAD_DIM)), const((MEM_W, d))],
        out_specs=tile(d),
        out_shape=jax.ShapeDtypeStruct((b, s, d), F32),
        compiler_params=_params("parallel", "arbitrary"),
        name="mix_out_mem_attention",
    )(x, ya, yb, wa, wb, g.reshape(1, d).astype(F32), wq, mem_kv, gq.reshape(1, -1).astype(F32),
      gk.reshape(1, -1).astype(F32), wo)


def _rope_tables(positions, dim, lead_ones, tail):
    inv = 1.0 / (ROPE_THETA ** (jnp.arange(0, dim, 2, dtype=F32) / dim))
    ang = positions.astype(F32)[..., None] * inv
    c, s = jnp.cos(ang), jnp.sin(ang)
    shape = positions.shape
    cos = jnp.concatenate([jnp.ones(shape + (lead_ones,), F32), c, c, jnp.ones(shape + (tail,), F32)], axis=-1)
    sin = jnp.concatenate([jnp.zeros(shape + (lead_ones,), F32), -s, s, jnp.zeros(shape + (tail,), F32)], axis=-1)
    reps = LANES // cos.shape[-1]
    return jnp.tile(cos, (1, 1, reps)), jnp.tile(sin, (1, 1, reps))


IN_TILE = 512


def _pad_rows(w, rows):
    return jnp.pad(w, ((0, rows - w.shape[0]), (0, 0)))


def _ab_in_segments(w):
    c = (SWA_HEADS + 2 * SWA_KV_HEADS) * HEAD_DIM + 3 * RWKV_DIM
    wt = w.T.astype(BF16)
    lora = jnp.concatenate([_pad_rows(wt[c:c + DECAY_LORA], LANES),
                            _pad_rows(wt[c + DECAY_LORA:c + DECAY_LORA + AAA_LORA], LANES),
                            _pad_rows(wt[c + DECAY_LORA + AAA_LORA:], 2 * LANES)], axis=0)
    return [wt[:c], lora]


def _cd_in_segments(w):
    c1 = MLA_Q_RANK + MLA_KV_RANK
    wt = w.T.astype(BF16)
    z = lambda n: jnp.zeros((n, wt.shape[1]), wt.dtype)
    mid = jnp.concatenate([wt[MLA_Q_RANK:c1], z(MLA_NOPE), wt[c1:c1 + MLA_ROPE],
                           z(LANES - MLA_NOPE - MLA_ROPE), z(LANES)], axis=0)
    return [wt[:MLA_Q_RANK], mid, wt[c1 + MLA_ROPE:]]


def _head_slabs(w, per_head):
    k = w.shape[0]
    return jnp.pad(w.reshape(k, -1, per_head), ((0, 0), (0, 0), (0, LANES - per_head))).reshape(k, -1)


def _slab_vec(*parts):
    v = jnp.concatenate([p.astype(F32) for p in parts])
    return jnp.pad(v, (0, LANES - v.shape[0])).reshape(1, LANES)


def kernel(x, mem, positions, ffn1_norm, ffn1_w_gate, ffn1_w_up, ffn1_w_down, mix_norm, ab_w_in, ab_w_out, swa_q_norm, swa_k_norm, swa_sinks, rwkv_mu, rwkv_w0, rwkv_w2, rwkv_a0, rwkv_a2, rwkv_g2, rwkv_k_k, rwkv_k_a, rwkv_r_k, rwkv_gn_g, rwkv_gn_b, cd_w_in, cd_w_out, mla_cq_norm, mla_ckv_norm, mla_w_uq, mla_w_ukv, mla_q_nope_norm, mla_k_nope_norm, mla_q_rope_norm, mla_k_rope_norm, diff_q_norm, diff_k_norm, diff_lq1, diff_lk1, diff_lq2, diff_lk2, diff_subln, memx_norm, memx_w_q, memx_q_norm, memx_w_o, mem_norm, mem_w_kv, mem_k_norm, ffn2_norm, ffn2_w_gate, ffn2_w_up, ffn2_w_down):
    b, s, d = x.shape
    m = mem.shape[1]
    t = b * s
    depth = ffn1_norm.shape[0]
    bf = lambda w: w.astype(BF16)
    cos64, sin64 = _rope_tables(positions, HEAD_DIM, 0, 0)
    cos32, sin32 = _rope_tables(positions, MLA_ROPE, MLA_NOPE, LANES - MLA_NOPE - MLA_ROPE)

    mem_kv = norm_matmul(mem.reshape(b * m, d), mem_norm, [bf(mem_w_kv).T],
                         tn=IN_TILE, name="mem_kv").reshape(b, m, 2 * MEM_W)

    x = x.reshape(t, d)
    for layer in range(depth):
        j = layer // 2
        x = ffn(x, ffn1_norm[layer], ffn1_w_gate, ffn1_w_up, ffn1_w_down, layer, name="ffn1")
        if layer % 2 == 0:
            u = norm_matmul(x, mix_norm[layer], _ab_in_segments(ab_w_in[j]), tn=IN_TILE, name="ab_in")
            u = u.reshape(b, s, AB_IN_PAD)
            y_a = swa_attention(u, cos64, sin64, _slab_vec(swa_q_norm[j], swa_q_norm[j]),
                                _slab_vec(swa_k_norm[j], swa_k_norm[j]), swa_sinks[j])
            y_b = rwkv_mix(u, rwkv_mu[j], rwkv_w0[j], rwkv_w2[j], rwkv_a0[j], rwkv_a2[j], rwkv_g2[j],
                           rwkv_k_k[j], rwkv_k_a[j], rwkv_r_k[j], rwkv_gn_g[j], rwkv_gn_b[j])
            mixed, w_out, split = (y_a, y_b), bf(ab_w_out[j]), SWA_HEADS * HEAD_DIM
        else:
            u = norm_matmul(x, mix_norm[layer], _cd_in_segments(cd_w_in[j]), tn=IN_TILE, name="cd_in")
            u = u.reshape(b, s, CD_IN_PAD)
            zero64 = jnp.zeros((MLA_NOPE,), F32)
            q_c, k_c, vt_c = mla_prep(u, cos32, sin32,
                                      _slab_vec(mla_q_nope_norm[j], mla_q_rope_norm[j]),
                                      _slab_vec(mla_k_nope_norm[j]),
                                      _slab_vec(zero64, mla_k_rope_norm[j]),
                                      mla_cq_norm[j], mla_ckv_norm[j],
                                      bf(_head_slabs(mla_w_uq[j], MLA_NOPE + MLA_ROPE)), bf(mla_w_ukv[j]))
            y_c = causal_attention(q_c, k_c, vt_c, n_sm=1, hps=16, name="mla_attention")
            q_d, k_d, vt_d = diff_prep(u, cos64, sin64, _slab_vec(diff_q_norm[j], diff_q_norm[j]),
                                       _slab_vec(diff_k_norm[j], diff_k_norm[j]))
            lambda_init = 0.8 - 0.6 * math.exp(-0.3 * layer)
            lam = (jnp.exp(jnp.sum(diff_lq1[j].astype(F32) * diff_lk1[j].astype(F32)))
                   - jnp.exp(jnp.sum(diff_lq2[j].astype(F32) * diff_lk2[j].astype(F32))) + lambda_init)
            y_d = causal_attention(q_d, k_d, vt_d, n_sm=2, hps=8, lam=lam, g=diff_subln[j],
                                   out_scale=1.0 - lambda_init, name="diff_attention")
            mixed, w_out, split = (y_c, y_d), bf(cd_w_out[j]), MLA_HEADS * MLA_V
        x = mix_out_mem_attention(x.reshape(b, s, d), mixed[0], mixed[1], w_out[:split], w_out[split:], mem_kv,
                                  memx_norm[layer], bf(memx_w_q[layer]), memx_q_norm[layer], mem_k_norm,
                                  bf(memx_w_o[layer])).reshape(t, d)
        x = ffn(x, ffn2_norm[layer], ffn2_w_gate, ffn2_w_up, ffn2_w_down, layer, name="ffn2")
    return x.reshape(b, s, d)
```

```python
import functools
import math

import jax
import jax.numpy as jnp
from jax import lax
from jax.experimental import pallas as pl
from jax.experimental.pallas import tpu as pltpu

F32 = jnp.float32
BF16 = jnp.bfloat16

EPS = 1e-6
ROPE_THETA = 10000.0
NEG_INF = -1e30
LOG2E = math.log2(math.e)
ATTN_LOOKAHEAD = 8
SWA_LOOKAHEAD = 4
ATTN_TILE = 256
SEG_W = 256
LANES = 128

D_MODEL = 2048
D_FF = 5632
HEAD_DIM = 64
SWA_HEADS = 16
SWA_KV_HEADS = 4
SWA_BLOCK = 128
RWKV_DIM = 1024
RWKV_CHUNK = 64
RWKV_GN_EPS = 64e-5
DECAY_LORA, AAA_LORA, GATE_LORA = 64, 64, 160
MLA_HEADS, MLA_Q_RANK, MLA_KV_RANK, MLA_NOPE, MLA_ROPE, MLA_V = 16, 512, 256, 64, 32, 64
DIFF_HEADS, DIFF_QK, DIFF_V = 8, 64, 128
MEM_HEADS, MEM_HEAD_DIM = 4, 128
MEM_W = MEM_HEADS * MEM_HEAD_DIM
AB_IN_PAD = 5120
CD_IN_PAD = 4096

VMEM_LIMIT = 48 * 1024 * 1024
FFN_VMEM_LIMIT = 60 * 1024 * 1024


def _params(*sem):
    return pltpu.CompilerParams(dimension_semantics=sem, vmem_limit_bytes=VMEM_LIMIT)


def _dot(a, b):
    return jnp.dot(a, b, preferred_element_type=F32)


def _dot_t(a, b):
    return lax.dot_general(a, b, (((1,), (1,)), ((), ())), preferred_element_type=F32)


def _dot_0(a, b):
    return lax.dot_general(a, b, (((0,), (0,)), ((), ())), preferred_element_type=F32)


def _rms(x, g):
    return x * lax.rsqrt(jnp.mean(x * x, axis=-1, keepdims=True) + EPS) * g


def _seg_sum(x, e):
    xb = x.astype(BF16)
    w = e.shape[0]
    parts = [_dot(xb[:, i:i + w], e) for i in range(0, x.shape[1], w)]
    return parts[0] if len(parts) == 1 else jnp.concatenate(parts, axis=-1)


def _tile_lanes(v, width):
    return v if v.shape[-1] == width else jnp.tile(v, (1, width // v.shape[-1]))


def _store_transposed_tiles(vt_ref, v):
    for r in range(vt_ref.shape[1]):
        vt_ref[0, r] = v[r * ATTN_TILE:(r + 1) * ATTN_TILE, :].T.astype(vt_ref.dtype)


def _swap_halves(x, half):
    w = x.shape[-1]
    lane = lax.broadcasted_iota(jnp.int32, x.shape, x.ndim - 1)
    low = (lane & (2 * half - 1)) < half
    return jnp.where(low, pltpu.roll(x, w - half, x.ndim - 1), pltpu.roll(x, half, x.ndim - 1))


def _norm_matmul_kernel(x_ref, g_ref, *refs, starts, counts):
    w_refs, o_ref, xn_ref = refs[:len(starts)], refs[len(starts)], refs[len(starts) + 1]
    j = pl.program_id(1)

    @pl.when(j == 0)
    def _():
        xn_ref[...] = _rms(x_ref[...], g_ref[...]).astype(BF16)

    for w_ref, start, count in zip(w_refs, starts, counts):
        @pl.when((j >= start) & (j < start + count))
        def _(w_ref=w_ref):
            o_ref[...] = _dot_t(xn_ref[...], w_ref[...])


def norm_matmul(x, g, segments, *, tm=1024, tn=512, name="norm_matmul"):
    t, k = x.shape
    tm = min(tm, t)
    assert t % tm == 0 and all(w.shape[0] % tn == 0 and w.shape[1] == k for w in segments)
    counts = [w.shape[0] // tn for w in segments]
    starts = [sum(counts[:i]) for i in range(len(counts))]
    w_spec = lambda start, count: pl.BlockSpec((tn, k), lambda i, j: (jnp.clip(j - start, 0, count - 1), 0))
    return pl.pallas_call(
        functools.partial(_norm_matmul_kernel, starts=tuple(starts), counts=tuple(counts)),
        grid=(t // tm, sum(counts)),
        in_specs=[pl.BlockSpec((tm, k), lambda i, j: (i, 0)),
                  pl.BlockSpec((1, k), lambda i, j: (0, 0))]
                 + [w_spec(start, count) for start, count in zip(starts, counts)],
        out_specs=pl.BlockSpec((tm, tn), lambda i, j: (i, j)),
        out_shape=jax.ShapeDtypeStruct((t, sum(counts) * tn), F32),
        scratch_shapes=[pltpu.VMEM((tm, k), BF16)],
        compiler_params=_params("parallel", "arbitrary"),
        name=name,
    )(x, g.reshape(1, k).astype(F32), *segments)


def _ffn_kernel(x_ref, g_ref, wg_ref, wu_ref, wd_ref, o_ref, xn_ref):
    @pl.when(pl.program_id(1) == 0)
    def _():
        x = x_ref[...]
        xn_ref[...] = _rms(x, g_ref[...]).astype(BF16)
        o_ref[...] = x

    xn = xn_ref[...]
    a = _dot(xn, wg_ref[...].astype(BF16))
    b = _dot(xn, wu_ref[...].astype(BF16))
    h = (a * (0.5 / (1.0 + jnp.exp(-a))) * b).astype(BF16)
    o_ref[...] += _dot(h, wd_ref[...].astype(BF16))


def ffn(x, g, wg, wu, wd, layer, *, tm=1024, tf=256, name="ffn"):
    t, d = x.shape
    ff = wg.shape[2]
    tm = min(tm, t)
    assert t % tm == 0 and ff % tf == 0
    nf = ff // tf
    return pl.pallas_call(
        _ffn_kernel,
        grid=(t // tm, nf),
        in_specs=[pl.BlockSpec((tm, d), lambda i, f: (i, 0)),
                  pl.BlockSpec((1, d), lambda i, f: (0, 0)),
                  pl.BlockSpec((None, d, tf), lambda i, f: (layer, 0, f)),
                  pl.BlockSpec((None, d, tf), lambda i, f: (layer, 0, f)),
                  pl.BlockSpec((None, tf, d), lambda i, f: (layer, f, 0))],
        out_specs=pl.BlockSpec((tm, d), lambda i, f: (i, 0)),
        out_shape=jax.ShapeDtypeStruct((t, d), F32),
        scratch_shapes=[pltpu.VMEM((tm, d), BF16)],
        compiler_params=pltpu.CompilerParams(dimension_semantics=("parallel", "arbitrary"),
                                             vmem_limit_bytes=FFN_VMEM_LIMIT),
        name=name,
    )(x, g.reshape(1, d).astype(F32), wg, wu, wd)


def _swa_kernel(sink_ref, q_ref, kc_ref, kp_ref, vc_ref, vp_ref, cc_ref, sc_ref, cp_ref, sp_ref,
                gq_ref, gk_ref, e_ref, o_ref):
    n = pl.program_id(1)
    blk = SWA_BLOCK
    group = SWA_HEADS // SWA_KV_HEADS
    q = q_ref[0]
    k = jnp.concatenate([kp_ref[0], kc_ref[0]], axis=0)
    v = jnp.concatenate([vp_ref[0], vc_ref[0]], axis=0)
    cos_q, sin_q = cc_ref[0], sc_ref[0]
    cos_k = jnp.concatenate([cp_ref[0], cos_q], axis=0)
    sin_k = jnp.concatenate([sp_ref[0], sin_q], axis=0)
    def prep(x, g, cos, sin, scale):
        w = x.shape[1]
        xg = x * _tile_lanes(g, w)
        xr = xg * _tile_lanes(cos, w) + _swap_halves(xg, HEAD_DIM // 2) * _tile_lanes(sin, w)
        return xr * lax.rsqrt(_seg_sum(x * x, e_ref[...]) * (1.0 / HEAD_DIM) + EPS) * scale

    qr = prep(q, gq_ref[...], cos_q, sin_q, HEAD_DIM ** -0.5 * LOG2E).astype(BF16)
    kr = prep(k, gk_ref[...], cos_k, sin_k, 1.0)
    key_i = lax.broadcasted_iota(jnp.int32, (2 * blk, blk), 0)
    qry_i = lax.broadcasted_iota(jnp.int32, (2 * blk, blk), 1)
    rel = qry_i + blk - key_i
    valid = (rel >= 0) & (rel < blk) & ((n > 0) | (key_i >= blk))
    low = lax.broadcasted_iota(jnp.int32, (blk, LANES), 1) < HEAD_DIM
    zero = jnp.zeros((blk, LANES), BF16)
    k_dup, v_t = [], []
    for g in range(SWA_KV_HEADS):
        kg = kr[:, g * HEAD_DIM:(g + 1) * HEAD_DIM]
        k_dup.append(jnp.concatenate([kg, kg], axis=-1).astype(BF16))
        v_t.append(v[:, g * HEAD_DIM:(g + 1) * HEAD_DIM].T.astype(BF16))

    def scores(h):
        slab = qr[:, (h // 2) * LANES:(h // 2 + 1) * LANES]
        qh = jnp.where(low, slab, zero) if h % 2 == 0 else jnp.where(low, zero, slab)
        return _dot_t(k_dup[h // group], qh)

    ahead = [scores(h) for h in range(SWA_LOOKAHEAD)]
    ones = jnp.ones((8, 2 * blk), BF16)
    outs = []
    for h in range(SWA_HEADS):
        s = jnp.where(valid, ahead.pop(0), NEG_INF)
        if h + SWA_LOOKAHEAD < SWA_HEADS:
            ahead.append(scores(h + SWA_LOOKAHEAD))
        sink = sink_ref[h] * LOG2E
        m = jnp.maximum(jnp.max(s, axis=0, keepdims=True), sink)
        p = jnp.exp2(s - m).astype(BF16)
        den = _dot(ones, p)[0:1] + jnp.exp2(sink - m)
        outs.append(_dot(v_t[h // group], p) / den)
    slabs = [jnp.concatenate(outs[i:i + 2], axis=0).T for i in range(0, SWA_HEADS, 2)]
    o_ref[0] = jnp.concatenate(slabs, axis=-1).astype(o_ref.dtype)


def swa_attention(u, cos, sin, gq, gk, sinks):
    b, s, _ = u.shape
    nb = s // SWA_BLOCK
    qw, kw = SWA_HEADS * HEAD_DIM, SWA_KV_HEADS * HEAD_DIM
    cur = lambda c: (lambda i, n: (i, n, c))
    prev = lambda c: (lambda i, n: (i, jnp.maximum(n - 1, 0), c))
    tab = pl.BlockSpec((1, SWA_BLOCK, LANES), cur(0))
    tab_prev = pl.BlockSpec((1, SWA_BLOCK, LANES), prev(0))
    gain = pl.BlockSpec((1, LANES), lambda i, n: (0, 0))
    seg = (jnp.arange(kw)[:, None] // HEAD_DIM == jnp.arange(kw)[None, :] // HEAD_DIM).astype(BF16)
    return pl.pallas_call(
        _swa_kernel,
        grid=(b, nb),
        in_specs=[pl.BlockSpec(memory_space=pltpu.SMEM),
                  pl.BlockSpec((1, SWA_BLOCK, qw), cur(0)),
                  pl.BlockSpec((1, SWA_BLOCK, kw), cur(qw // kw)),
                  pl.BlockSpec((1, SWA_BLOCK, kw), prev(qw // kw)),
                  pl.BlockSpec((1, SWA_BLOCK, kw), cur(qw // kw + 1)),
                  pl.BlockSpec((1, SWA_BLOCK, kw), prev(qw // kw + 1)),
                  tab, tab, tab_prev, tab_prev, gain, gain, pl.BlockSpec((kw, kw), lambda i, n: (0, 0))],
        out_specs=pl.BlockSpec((1, SWA_BLOCK, qw), cur(0)),
        out_shape=jax.ShapeDtypeStruct((b, s, qw), BF16),
        compiler_params=_params("parallel", "arbitrary"),
        name="swa_attention",
    )(sinks.astype(F32), u, u, u, u, u, cos, sin, cos, sin, gq, gk, seg)


def _mm(a, b, dims, passes):
    dn = (dims, ((), ()))
    dg = lambda x, y: lax.dot_general(x, y, dn, preferred_element_type=F32)
    ah = a.astype(BF16)
    bh = b.astype(BF16)
    if passes == 1:
        return dg(ah, bh)
    al = (a - ah.astype(F32)).astype(BF16)
    bl = (b - bh.astype(F32)).astype(BF16)
    return dg(ah, bh) + dg(ah, bl) + dg(al, bh)


_NN = ((1,), (0,))
_NT = ((1,), (1,))
_TN = ((0,), (0,))
P_SC, P_INV, P_PQ, P_OUT, P_ST = 1, 1, 1, 1, 1
RWKV_UNROLL = 8


RWKV_PREP_ROWS = 256


def _rwkv_kernel(ur_ref, uk_ref, uv_ref, ul_ref, mur_ref, muk_ref, muv_ref, mul_ref, w0_ref, w2_ref, a0_ref,
                 a2_ref, g2_ref, kkw_ref, ka_ref, gng_ref, gnb_ref, rk_ref,
                 o_ref, r_ref, k_ref, v_ref, kk_ref, b_ref, lw_ref, g_ref,
                 st_ref, y1_ref, y0_ref, n_ref, z_ref, dec_ref, *, nchunk):
    c = RWKV_CHUNK
    lane_c = lax.broadcasted_iota(jnp.int32, (c, LANES), 1)
    head0 = lane_c < HEAD_DIM
    ri = lax.broadcasted_iota(jnp.int32, (2 * c, 2 * c), 0)
    ci = lax.broadcasted_iota(jnp.int32, (2 * c, 2 * c), 1)
    eye = jnp.where(ri == ci, 1.0, 0.0)
    tril_c = jnp.where(lax.broadcasted_iota(jnp.int32, (c, c), 0) >= lax.broadcasted_iota(jnp.int32, (c, c), 1),
                       1.0, 0.0).astype(BF16)
    stack = lambda x: jnp.concatenate([jnp.where(head0, x, 0.0), jnp.where(head0, 0.0, x)], axis=0)

    def seg_mean(x):
        first = lax.broadcasted_iota(jnp.int32, x.shape, 1) < HEAD_DIM
        m0 = jnp.sum(jnp.where(first, x, 0.0), axis=-1, keepdims=True)
        m1 = jnp.sum(jnp.where(first, 0.0, x), axis=-1, keepdims=True)
        return jnp.where(first, m0, m1) * (1.0 / HEAD_DIM)

    def prep(io, first_group):
        group_rows = RWKV_UNROLL * c
        for tix in range(group_rows // RWKV_PREP_ROWS):
            start = pl.multiple_of(io * group_rows + tix * RWKV_PREP_ROWS, RWKV_PREP_ROWS)
            rows = pl.ds(start, RWKV_PREP_ROWS)
            at_start = first_group and tix == 0

            def shifted(ref, mu_ref):
                x = ref[0, rows, :]
                if at_start:
                    last = jnp.zeros((1, x.shape[1]), F32)
                else:
                    last = ref[0, pl.ds(pl.multiple_of(start - 8, 8), 8), :][7:8, :]
                row = lax.broadcasted_iota(jnp.int32, x.shape, 0)
                prev = jnp.where(row == 0, last, pltpu.roll(x, 1, 0))
                return x + (prev - x) * mu_ref[...]

            r = shifted(ur_ref, mur_ref)
            k = shifted(uk_ref, muk_ref)
            v = shifted(uv_ref, muv_ref)
            lo = shifted(ul_ref, mul_ref)
            yield
            w_lo, a_lo, g_lo = lo[:, 0:LANES], lo[:, LANES:2 * LANES], lo[:, 2 * LANES:4 * LANES]
            z = -(w0_ref[...] + _mm(jnp.tanh(w_lo), w2_ref[...], _NN, 3))
            w = -(jnp.maximum(z, 0.0) + jnp.log(1.0 + jnp.exp(-jnp.abs(z)))) - 0.5
            a = 1.0 / (1.0 + jnp.exp(-(a0_ref[...] + _mm(a_lo, a2_ref[...], _NN, 1))))
            g = _mm(1.0 / (1.0 + jnp.exp(-g_lo)), g2_ref[...], _NN, 1)
            yield
            kk = k * kkw_ref[...]
            kk = kk / jnp.maximum(jnp.sqrt(seg_mean(kk * kk) * float(HEAD_DIM)), 1e-12)
            r_ref[rows, :] = r
            k_ref[rows, :] = k * (1.0 + (a - 1.0) * ka_ref[...])
            v_ref[rows, :] = v
            kk_ref[rows, :] = kk
            b_ref[rows, :] = kk * a
            lw_ref[rows, :] = -jnp.exp(w)
            g_ref[rows, :] = g
            yield

    def build(ics):
        each = lambda f, *cols: [f(*args) for args in zip(*cols)]
        sls = [pl.ds(pl.multiple_of(ic * c, c), c) for ic in ics]
        load = lambda ref: [ref[sl, :] for sl in sls]
        r, k, v, kk, b, lw = (load(ref) for ref in (r_ref, k_ref, v_ref, kk_ref, b_ref, lw_ref))

        def running_sum(x):
            l1 = x.astype(BF16)
            rest = x - l1.astype(F32)
            l2 = rest.astype(BF16)
            l3 = (rest - l2.astype(F32)).astype(BF16)
            return _dot(tril_c, l1) + _dot(tril_c, l2) + _dot(tril_c, l3)

        cum = each(running_sum, lw)
        yield
        cum_end = [x[c - 1:c, :] for x in cum]
        e_neg = each(lambda x: jnp.exp(-x), cum)
        e_end = each(lambda x, xe: jnp.exp(xe - x), cum, cum_end)
        a_s = each(lambda kk_, x, l: stack(-kk_ * jnp.exp(x - l)), kk, cum, lw)
        r_s = each(lambda r_, x: stack(r_ * jnp.exp(x)), r, cum)
        b_s = each(lambda b_, e: stack(b_ * e), b, e_neg)
        k_s = each(lambda k_, e: stack(k_ * e), k, e_neg)
        bh_s = each(lambda b_, e: stack(b_ * e), b, e_end)
        kh_s = each(lambda k_, e: stack(k_ * e), k, e_end)
        v_s = each(stack, v)
        n2 = 2 * c
        sc = each(lambda a_, r_, b_, k_: _mm(jnp.concatenate([a_, r_], axis=0),
                                             jnp.concatenate([b_, k_], axis=0), _NT, P_SC), a_s, r_s, b_s, k_s)
        low = [jnp.where(ri > ci, x[:n2, :n2], 0.0) for x in sc]
        a_ak = [jnp.where(ri > ci, x[:n2, n2:], 0.0) for x in sc]
        a_rb = [jnp.where(ri >= ci, x[n2:, :n2], 0.0) for x in sc]
        a_rk = [jnp.where(ri >= ci, x[n2:, n2:], 0.0) for x in sc]
        yield
        inv = [eye + x for x in low]
        pw = each(lambda x: _mm(x, x, _NN, P_INV), low)
        yield
        levels = 5
        for lvl in range(levels - 1):
            both = each(lambda t, x: _mm(jnp.concatenate([t, x], axis=0), x, _NN, P_INV), inv, pw)
            inv = each(lambda t, r: t + r[:n2], inv, both)
            pw = [r[n2:] for r in both]
            yield
        inv = each(lambda t, x: t + _mm(t, x, _NN, P_INV), inv, pw)
        yield
        akv = each(lambda x, y: _mm(x, y, _NN, P_PQ), a_ak, v_s)
        yield
        pq = each(lambda t, x, y: _mm(t, jnp.concatenate([x, y], axis=1), _NN, P_PQ), inv, a_s, akv)
        yield
        yy = each(lambda x, y: _mm(x, y, _NN, P_OUT), a_rb, pq)
        y0b = each(lambda x, y: _mm(x, y, _NN, P_OUT), a_rk, v_s)
        nz = each(lambda x, y: _mm(x, y, _TN, P_OUT), pq, bh_s)
        zb = each(lambda x, y: _mm(x, y, _TN, P_OUT), v_s, kh_s)
        for i, ic in enumerate(ics):
            y1_ref[ic] = r_s[i] + yy[i][:, :LANES]
            y0_ref[ic] = yy[i][:, LANES:] + y0b[i]
            n_ref[ic] = nz[i][:LANES]
            z_ref[ic] = nz[i][LANES:] + zb[i]
            dec_ref[ic] = jnp.broadcast_to(jnp.exp(cum_end[i]), (8, LANES))

    def emit(ic, st):
        sl = pl.ds(pl.multiple_of(ic * c, c), c)
        r, k, v, g = r_ref[sl, :], k_ref[sl, :], v_ref[sl, :], g_ref[sl, :]
        y_st = _mm(y1_ref[ic], st, _NT, P_ST) + y0_ref[ic]
        y = y_st[0:c] + y_st[c:2 * c]
        mean = seg_mean(y)
        var = seg_mean((y - mean) * (y - mean))
        yn = (y - mean) * lax.rsqrt(var + RWKV_GN_EPS) * gng_ref[...] + gnb_ref[...]
        bonus = seg_mean(r * k * rk_ref[...]) * float(HEAD_DIM) * v
        o_ref[0, sl, :] = ((yn + bonus) * g).astype(o_ref.dtype)
        return st * dec_ref[ic][0:1, :] + _mm(st, n_ref[ic], _NN, P_ST) + z_ref[ic]

    def scan(ics):
        st = st_ref[...]
        for ic in ics:
            st = emit(ic, st)
            yield
        st_ref[...] = st

    def run(*gens):
        live = list(gens)
        while live:
            live = [gen for gen in live if next(gen, live) is not live]

    group = lambda io: [io * RWKV_UNROLL + i for i in range(RWKV_UNROLL)]
    ngroup = nchunk // RWKV_UNROLL
    st_ref[...] = jnp.zeros_like(st_ref)
    run(prep(0, True))
    if ngroup > 1:
        run(prep(1, False), build(group(0)))
    else:
        run(build(group(0)))

    @pl.loop(1, ngroup - 1)
    def _(io):
        run(prep(io + 1, False), build(group(io)), scan(group(io - 1)))

    if ngroup > 1:
        run(build(group(ngroup - 1)), scan(group(ngroup - 2)))
    run(scan(group(ngroup - 1)))


def rwkv_mix(u, mu, w0, w2, a0, a2, g2, k_k, k_a, r_k, gn_g, gn_b):
    bsz, s, _ = u.shape
    npair = RWKV_DIM // LANES
    nchunk = s // RWKV_CHUNK
    assert nchunk % RWKV_UNROLL == 0 and (RWKV_UNROLL * RWKV_CHUNK) % RWKV_PREP_ROWS == 0
    base = (SWA_HEADS + 2 * SWA_KV_HEADS) * HEAD_DIM // LANES
    lora_w = 4 * LANES
    slab = lambda off: pl.BlockSpec((1, s, LANES), lambda i, p: (i, 0, off + p))
    vec = pl.BlockSpec((1, LANES), lambda i, p: (0, p))
    cols = lambda rows: pl.BlockSpec((rows, LANES), lambda i, p: (0, p))
    row = lambda vv: vv.reshape(1, -1).astype(F32)
    pad_rows = lambda m, rows: jnp.pad(m, ((0, rows - m.shape[0]), (0, 0))).astype(F32)
    pad_cols = lambda vv, n: jnp.pad(vv, (0, n - vv.shape[0]))
    c3 = 3 * RWKV_DIM
    mu_l = jnp.concatenate([pad_cols(mu[c3:c3 + DECAY_LORA], LANES),
                            pad_cols(mu[c3 + DECAY_LORA:c3 + DECAY_LORA + AAA_LORA], LANES),
                            pad_cols(mu[c3 + DECAY_LORA + AAA_LORA:], 2 * LANES)])
    seq = pltpu.VMEM((s, LANES), F32)
    mat = pltpu.VMEM((nchunk, LANES, LANES), F32)
    return pl.pallas_call(
        functools.partial(_rwkv_kernel, nchunk=nchunk),
        grid=(bsz, npair),
        in_specs=[slab(base), slab(base + npair), slab(base + 2 * npair),
                  pl.BlockSpec((1, s, lora_w), lambda i, p: (i, 0, (base + 3 * npair) * LANES // lora_w)),
                  vec, vec, vec, pl.BlockSpec((1, lora_w), lambda i, p: (0, 0)),
                  vec, cols(LANES), vec, cols(LANES), cols(2 * LANES), vec, vec, vec, vec, vec],
        out_specs=pl.BlockSpec((1, s, LANES), lambda i, p: (i, 0, p)),
        out_shape=jax.ShapeDtypeStruct((bsz, s, RWKV_DIM), BF16),
        scratch_shapes=[seq] * 7 + [pltpu.VMEM((LANES, LANES), F32), mat, mat, mat, mat,
                                    pltpu.VMEM((nchunk, 8, LANES), F32)],
        compiler_params=_params("parallel", "arbitrary"),
        name="rwkv_mix",
    )(u, u, u, u, row(mu[:RWKV_DIM]), row(mu[RWKV_DIM:2 * RWKV_DIM]), row(mu[2 * RWKV_DIM:c3]), row(mu_l),
      row(w0), pad_rows(w2, LANES), row(a0), pad_rows(a2, LANES), pad_rows(g2, 2 * LANES),
      row(k_k), row(k_a), row(gn_g), row(gn_b), row(r_k))


def _mla_prep_kernel(cq_ref, ckv_ref, pe_ref, cos_ref, sin_ref, e_ref, gq_ref, gkn_ref, gkp_ref, invn_ref,
                     gcq_ref, gckv_ref, wuq_ref, wukv_ref, qo_ref, ko_ref, vo_ref, cqn_ref, ckvn_ref, kpe_ref):
    cos, sin = cos_ref[0], sin_ref[0]
    half = MLA_ROPE // 2

    def rope(x, g):
        w = x.shape[1]
        xg = x * _tile_lanes(g, w)
        return xg * _tile_lanes(cos, w) + _swap_halves(xg, half) * _tile_lanes(sin, w)

    @pl.when(pl.program_id(2) == 0)
    def _():
        cqn_ref[...] = _rms(cq_ref[0], gcq_ref[...]).astype(BF16)
        ckvn_ref[...] = _rms(ckv_ref[0], gckv_ref[...]).astype(BF16)
        pe = pe_ref[0]
        inv_pe = lax.rsqrt(jnp.sum(pe * pe, axis=-1, keepdims=True) * (1.0 / MLA_ROPE) + EPS)
        kpe_ref[...] = rope(pe, gkp_ref[...]) * inv_pe

    x = _dot(cqn_ref[...], wuq_ref[...])
    w = x.shape[1]
    inv_n = _tile_lanes(invn_ref[...], w)
    inv = lax.rsqrt(_seg_sum(x * x, e_ref[...]) * inv_n + EPS)
    qo_ref[0] = (rope(x, gq_ref[...]) * inv * ((MLA_NOPE + MLA_ROPE) ** -0.5 * LOG2E)).astype(qo_ref.dtype)
    kv = _dot(ckvn_ref[...], wukv_ref[...])
    inv_k = lax.rsqrt(_seg_sum(kv * kv, e_ref[...]) * inv_n + EPS)
    k_nope = kv * inv_k * _tile_lanes(gkn_ref[...], w)
    ko_ref[0] = (k_nope + _tile_lanes(kpe_ref[...], w)).astype(ko_ref.dtype)
    v = jnp.concatenate([kv[:, h * LANES + MLA_NOPE:(h + 1) * LANES] for h in range(w // LANES)], axis=-1)
    _store_transposed_tiles(vo_ref, v)


def mla_prep(u, cos, sin, gq, gkn, gkp, gcq, gckv, w_uq, w_ukv, *, ts=1024, tc=512):
    b, s, _ = u.shape
    wtot = w_uq.shape[1]
    ts = min(ts, s)
    lane = jnp.arange(SEG_W)
    same = (lane[:, None] // LANES == lane[None, :] // LANES)
    pos = lane % LANES
    nope = pos < MLA_NOPE
    pe = (pos >= MLA_NOPE) & (pos < MLA_NOPE + MLA_ROPE)
    seg = (same & ((nope[:, None] & nope[None, :]) | (pe[:, None] & pe[None, :]))).astype(BF16)
    p1 = jnp.arange(LANES)
    inv_n = jnp.where(p1 < MLA_NOPE, 1.0 / MLA_NOPE, jnp.where(p1 < MLA_NOPE + MLA_ROPE, 1.0 / MLA_ROPE, 0.0))
    blk = pl.BlockSpec((1, ts, tc), lambda i, t, c: (i, t, c))
    tab = pl.BlockSpec((1, ts, LANES), lambda i, t, c: (i, t, 0))
    vec = pl.BlockSpec((1, LANES), lambda i, t, c: (0, 0))
    pe_blk = (MLA_Q_RANK + MLA_KV_RANK) // LANES
    out = jax.ShapeDtypeStruct((b, s, wtot), BF16)
    return pl.pallas_call(
        _mla_prep_kernel,
        grid=(b, s // ts, wtot // tc),
        in_specs=[pl.BlockSpec((1, ts, MLA_Q_RANK), lambda i, t, c: (i, t, 0)),
                  pl.BlockSpec((1, ts, MLA_KV_RANK), lambda i, t, c: (i, t, MLA_Q_RANK // MLA_KV_RANK)),
                  pl.BlockSpec((1, ts, LANES), lambda i, t, c: (i, t, pe_blk)), tab, tab,
                  pl.BlockSpec((SEG_W, SEG_W), lambda i, t, c: (0, 0)), vec, vec, vec, vec,
                  pl.BlockSpec((1, MLA_Q_RANK), lambda i, t, c: (0, 0)),
                  pl.BlockSpec((1, MLA_KV_RANK), lambda i, t, c: (0, 0)),
                  pl.BlockSpec((MLA_Q_RANK, tc), lambda i, t, c: (0, c)),
                  pl.BlockSpec((MLA_KV_RANK, tc), lambda i, t, c: (0, c))],
        out_specs=[blk, blk, pl.BlockSpec((1, ts // ATTN_TILE, tc // LANES * MLA_V, ATTN_TILE),
                                          lambda i, t, c: (i, t, c, 0))],
        out_shape=[out, out, jax.ShapeDtypeStruct((b, s // ATTN_TILE, wtot // LANES * MLA_V, ATTN_TILE), BF16)],
        scratch_shapes=[pltpu.VMEM((ts, MLA_Q_RANK), BF16), pltpu.VMEM((ts, MLA_KV_RANK), BF16),
                        pltpu.VMEM((ts, LANES), F32)],
        compiler_params=_params("parallel", "parallel", "arbitrary"),
        name="mla_prep",
    )(u, u, u, cos, sin, seg, gq, gkn, gkp, inv_n.reshape(1, LANES).astype(F32),
      gcq.reshape(1, -1).astype(F32), gckv.reshape(1, -1).astype(F32), w_uq, w_ukv)


def _diff_prep_kernel(q_ref, k_ref, v_ref, cos_ref, sin_ref, e_ref, gq_ref, gk_ref, qo_ref, ko_ref, vo_ref):
    cos, sin = cos_ref[0], sin_ref[0]
    _store_transposed_tiles(vo_ref, v_ref[0])

    def prep(x, g, scale):
        w = x.shape[1]
        xg = x * _tile_lanes(g, w)
        xr = xg * _tile_lanes(cos, w) + _swap_halves(xg, DIFF_QK // 2) * _tile_lanes(sin, w)
        inv = lax.rsqrt(_seg_sum(x * x, e_ref[...]) * (1.0 / DIFF_QK) + EPS)
        return xr * inv * scale

    qo_ref[0] = prep(q_ref[0], gq_ref[...], DIFF_QK ** -0.5 * LOG2E).astype(qo_ref.dtype)
    ko_ref[0] = prep(k_ref[0], gk_ref[...], 1.0).astype(ko_ref.dtype)


def diff_prep(u, cos, sin, gq, gk, *, ts=1024, tc=512):
    b, s, _ = u.shape
    ts = min(ts, s)
    wtot = 2 * DIFF_HEADS * DIFF_QK
    q_base = (CD_IN_PAD - 3 * wtot) // tc
    seg = (jnp.arange(SEG_W)[:, None] // DIFF_QK == jnp.arange(SEG_W)[None, :] // DIFF_QK).astype(BF16)
    blk = lambda off: pl.BlockSpec((1, ts, tc), lambda i, t, c: (i, t, off + c))
    tab = pl.BlockSpec((1, ts, LANES), lambda i, t, c: (i, t, 0))
    vec = pl.BlockSpec((1, LANES), lambda i, t, c: (0, 0))
    out = jax.ShapeDtypeStruct((b, s, wtot), BF16)
    return pl.pallas_call(
        _diff_prep_kernel,
        grid=(b, s // ts, wtot // tc),
        in_specs=[blk(q_base), blk(q_base + wtot // tc), blk(q_base + 2 * wtot // tc), tab, tab,
                  pl.BlockSpec((SEG_W, SEG_W), lambda i, t, c: (0, 0)), vec, vec],
        out_specs=[blk(0), blk(0),
                   pl.BlockSpec((1, ts // ATTN_TILE, tc, ATTN_TILE), lambda i, t, c: (i, t, c, 0))],
        out_shape=[out, out, jax.ShapeDtypeStruct((b, s // ATTN_TILE, wtot, ATTN_TILE), BF16)],
        compiler_params=_params("parallel", "parallel", "arbitrary"),
        name="diff_prep",
    )(u, u, u, cos, sin, seg, gq, gk)


def _causal_attn_kernel(lam_ref, q_ref, k_ref, vt_ref, g_ref, o_ref, *, n_sm, tq, ow, out_scale):
    qi = pl.program_id(2)
    q = q_ref[0]
    hps = q.shape[1] // LANES
    slab = lambda x, h: x[:, h * LANES:(h + 1) * LANES]
    lane = lax.broadcasted_iota(jnp.int32, (tq, LANES), 1)
    qs, src = [], []
    for h in range(hps):
        qh = slab(q, h)
        if n_sm == 2:
            zero = jnp.zeros_like(qh)
            qs += [jnp.where(lane < DIFF_QK, qh, zero), jnp.where(lane < DIFF_QK, zero, qh)]
            src += [h, h]
        else:
            qs.append(qh)
            src.append(h)
    nch = len(qs)
    key_i = lax.broadcasted_iota(jnp.int32, (tq, tq), 0)
    qry_i = lax.broadcasted_iota(jnp.int32, (tq, tq), 1)
    ones = jnp.ones((8, tq), BF16)

    def step(j, carry, diagonal):
        kj = k_ref[0, pl.ds(pl.multiple_of(j * tq, tq), tq), :]
        scores = lambda i: _dot_t(slab(kj, src[i]), qs[i])
        new = []
        ahead = [scores(i) for i in range(min(ATTN_LOOKAHEAD, nch))]
        for i in range(nch):
            s = ahead.pop(0)
            if i + ATTN_LOOKAHEAD < nch:
                ahead.append(scores(i + ATTN_LOOKAHEAD))
            if diagonal:
                s = jnp.where(key_i <= qry_i, s, NEG_INF)
            m, l, acc = carry[3 * i:3 * i + 3]
            m_new = jnp.maximum(m, jnp.max(s, axis=0, keepdims=True))
            alpha = jnp.exp2(m - m_new)
            p = jnp.exp2(s - m_new).astype(BF16)
            new += [m_new, alpha * l + _dot(ones, p)[0:1],
                    alpha * acc + _dot(vt_ref[0, j, src[i] * ow:(src[i] + 1) * ow, :], p)]
        return tuple(new)

    init = (jnp.full((1, tq), NEG_INF, F32), jnp.zeros((1, tq), F32),
            jnp.zeros((ow, tq), F32)) * nch
    carry = lax.fori_loop(0, qi, lambda j, cr: step(j, cr, False), init)
    carry = step(qi, carry, True)
    outs = []
    for h in range(hps):
        c0 = 3 * n_sm * h
        o = carry[c0 + 2] / carry[c0 + 1]
        if n_sm == 2:
            o = o - lam_ref[0] * (carry[c0 + 5] / carry[c0 + 4])
            o = o * lax.rsqrt(jnp.mean(o * o, axis=0, keepdims=True) + EPS) * g_ref[...] * out_scale
        outs.append(o.T.astype(o_ref.dtype))
    o_ref[0] = jnp.concatenate(outs, axis=-1)


def causal_attention(q, k, vt, *, n_sm, lam=None, g=None, out_scale=1.0, hps=4, name="causal_attention"):
    b, s, wtot = q.shape
    tq = vt.shape[3]
    width = hps * LANES
    groups = wtot // width
    ow = vt.shape[2] // (wtot // LANES)
    lam = jnp.zeros((1,), F32) if lam is None else lam.reshape(1).astype(F32)
    g = jnp.ones((ow, 1), F32) if g is None else g.reshape(ow, 1).astype(F32)
    seq = pl.BlockSpec((1, s, width), lambda i, h, t: (i, 0, h))
    tile = pl.BlockSpec((1, tq, width), lambda i, h, t: (i, t, h))
    return pl.pallas_call(
        functools.partial(_causal_attn_kernel, n_sm=n_sm, tq=tq, ow=ow, out_scale=out_scale),
        grid=(b, groups, s // tq),
        in_specs=[pl.BlockSpec(memory_space=pltpu.SMEM), tile, seq,
                  pl.BlockSpec((1, s // tq, hps * ow, tq), lambda i, h, t: (i, 0, h, 0)),
                  pl.BlockSpec((ow, 1), lambda i, h, t: (0, 0))],
        out_specs=pl.BlockSpec((1, tq, hps * ow), lambda i, h, t: (i, t, h)),
        out_shape=jax.ShapeDtypeStruct((b, s, groups * hps * ow), BF16),
        compiler_params=_params("parallel", "parallel", "arbitrary"),
        name=name,
    )(lam, q, k, vt, g)


def _memx_kernel(x_ref, ya_ref, yb_ref, wa_ref, wb_ref, g_ref, wq_ref, kv_ref, gq_ref, gk_ref, wo_ref, o_ref):
    x = x_ref[0] + _dot(ya_ref[0], wa_ref[...]) + _dot(yb_ref[0], wb_ref[...])
    q = _dot(_rms(x, g_ref[...]).astype(BF16), wq_ref[...])
    kv = kv_ref[0]
    outs = []
    for h in range(MEM_HEADS):
        sl = slice(h * MEM_HEAD_DIM, (h + 1) * MEM_HEAD_DIM)
        qh = (_rms(q[:, sl], gq_ref[...]) * MEM_HEAD_DIM ** -0.5).astype(BF16)
        kh = _rms(kv[:, sl], gk_ref[...]).astype(BF16)
        vh = kv[:, MEM_W + h * MEM_HEAD_DIM:MEM_W + (h + 1) * MEM_HEAD_DIM].astype(BF16)
        s = _dot_t(qh, kh)
        p = jnp.exp(s - jnp.max(s, axis=-1, keepdims=True))
        outs.append(_dot(p.astype(BF16), vh) / jnp.sum(p, axis=-1, keepdims=True))
    o_ref[0] = x + _dot(jnp.concatenate(outs, axis=-1).astype(BF16), wo_ref[...])


def mix_out_mem_attention(x, ya, yb, wa, wb, mem_kv, g, wq, gq, gk, wo, *, tm=512):
    b, s, d = x.shape
    m = mem_kv.shape[1]
    tm = min(tm, s)
    const = lambda shape: pl.BlockSpec(shape, lambda i, t: (0,) * len(shape), pipeline_mode=pl.Buffered(1))
    tile = lambda w: pl.BlockSpec((1, tm, w), lambda i, t: (i, t, 0))
    return pl.pallas_call(
        _memx_kernel,
        grid=(b, s // tm),
        in_specs=[tile(d), tile(ya.shape[2]), tile(yb.shape[2]), const(wa.shape), const(wb.shape),
                  const((1, d)), const((d, MEM_W)),
                  pl.BlockSpec((1, m, 2 * MEM_W), lambda i, t: (i, 0, 0)),
                  const((1, MEM_HEAD_DIM)), const((1, MEM_HEAD_DIM)), const((MEM_W, d))],
        out_specs=tile(d),
        out_shape=jax.ShapeDtypeStruct((b, s, d), F32),
        compiler_params=_params("parallel", "arbitrary"),
        name="mix_out_mem_attention",
    )(x, ya, yb, wa, wb, g.reshape(1, d).astype(F32), wq, mem_kv, gq.reshape(1, -1).astype(F32),
      gk.reshape(1, -1).astype(F32), wo)


def _rope_tables(positions, dim, lead_ones, tail):
    inv = 1.0 / (ROPE_THETA ** (jnp.arange(0, dim, 2, dtype=F32) / dim))
    ang = positions.astype(F32)[..., None] * inv
    c, s = jnp.cos(ang), jnp.sin(ang)
    shape = positions.shape
    cos = jnp.concatenate([jnp.ones(shape + (lead_ones,), F32), c, c, jnp.ones(shape + (tail,), F32)], axis=-1)
    sin = jnp.concatenate([jnp.zeros(shape + (lead_ones,), F32), -s, s, jnp.zeros(shape + (tail,), F32)], axis=-1)
    reps = LANES // cos.shape[-1]
    return jnp.tile(cos, (1, 1, reps)), jnp.tile(sin, (1, 1, reps))


IN_TILE = 512


def _pad_rows(w, rows):
    return jnp.pad(w, ((0, rows - w.shape[0]), (0, 0)))


def _ab_in_segments(w):
    c = (SWA_HEADS + 2 * SWA_KV_HEADS) * HEAD_DIM + 3 * RWKV_DIM
    wt = w.T.astype(BF16)
    lora = jnp.concatenate([_pad_rows(wt[c:c + DECAY_LORA], LANES),
                            _pad_rows(wt[c + DECAY_LORA:c + DECAY_LORA + AAA_LORA], LANES),
                            _pad_rows(wt[c + DECAY_LORA + AAA_LORA:], 2 * LANES)], axis=0)
    return [wt[:c], lora]


def _cd_in_segments(w):
    c1 = MLA_Q_RANK + MLA_KV_RANK
    wt = w.T.astype(BF16)
    z = lambda n: jnp.zeros((n, wt.shape[1]), wt.dtype)
    mid = jnp.concatenate([wt[MLA_Q_RANK:c1], z(MLA_NOPE), wt[c1:c1 + MLA_ROPE],
                           z(LANES - MLA_NOPE - MLA_ROPE), z(LANES)], axis=0)
    return [wt[:MLA_Q_RANK], mid, wt[c1 + MLA_ROPE:]]


def _head_slabs(w, per_head):
    k = w.shape[0]
    return jnp.pad(w.reshape(k, -1, per_head), ((0, 0), (0, 0), (0, LANES - per_head))).reshape(k, -1)


def _slab_vec(*parts):
    v = jnp.concatenate([p.astype(F32) for p in parts])
    return jnp.pad(v, (0, LANES - v.shape[0])).reshape(1, LANES)


def kernel(x, mem, positions, ffn1_norm, ffn1_w_gate, ffn1_w_up, ffn1_w_down, mix_norm, ab_w_in, ab_w_out, swa_q_norm, swa_k_norm, swa_sinks, rwkv_mu, rwkv_w0, rwkv_w2, rwkv_a0, rwkv_a2, rwkv_g2, rwkv_k_k, rwkv_k_a, rwkv_r_k, rwkv_gn_g, rwkv_gn_b, cd_w_in, cd_w_out, mla_cq_norm, mla_ckv_norm, mla_w_uq, mla_w_ukv, mla_q_nope_norm, mla_k_nope_norm, mla_q_rope_norm, mla_k_rope_norm, diff_q_norm, diff_k_norm, diff_lq1, diff_lk1, diff_lq2, diff_lk2, diff_subln, memx_norm, memx_w_q, memx_q_norm, memx_w_o, mem_norm, mem_w_kv, mem_k_norm, ffn2_norm, ffn2_w_gate, ffn2_w_up, ffn2_w_down):
    b, s, d = x.shape
    m = mem.shape[1]
    t = b * s
    depth = ffn1_norm.shape[0]
    bf = lambda w: w.astype(BF16)
    cos64, sin64 = _rope_tables(positions, HEAD_DIM, 0, 0)
    cos32, sin32 = _rope_tables(positions, MLA_ROPE, MLA_NOPE, LANES - MLA_NOPE - MLA_ROPE)

    mem_kv = norm_matmul(mem.reshape(b * m, d), mem_norm, [bf(mem_w_kv).T],
                         tn=IN_TILE, name="mem_kv").reshape(b, m, 2 * MEM_W)

    x = x.reshape(t, d)
    for layer in range(depth):
        j = layer // 2
        x = ffn(x, ffn1_norm[layer], ffn1_w_gate, ffn1_w_up, ffn1_w_down, layer, name="ffn1")
        if layer % 2 == 0:
            u = norm_matmul(x, mix_norm[layer], _ab_in_segments(ab_w_in[j]), tn=IN_TILE, name="ab_in")
            u = u.reshape(b, s, AB_IN_PAD)
            y_a = swa_attention(u, cos64, sin64, _slab_vec(swa_q_norm[j], swa_q_norm[j]),
                                _slab_vec(swa_k_norm[j], swa_k_norm[j]), swa_sinks[j])
            y_b = rwkv_mix(u, rwkv_mu[j], rwkv_w0[j], rwkv_w2[j], rwkv_a0[j], rwkv_a2[j], rwkv_g2[j],
                           rwkv_k_k[j], rwkv_k_a[j], rwkv_r_k[j], rwkv_gn_g[j], rwkv_gn_b[j])
            mixed, w_out, split = (y_a, y_b), bf(ab_w_out[j]), SWA_HEADS * HEAD_DIM
        else:
            u = norm_matmul(x, mix_norm[layer], _cd_in_segments(cd_w_in[j]), tn=IN_TILE, name="cd_in")
            u = u.reshape(b, s, CD_IN_PAD)
            zero64 = jnp.zeros((MLA_NOPE,), F32)
            q_c, k_c, vt_c = mla_prep(u, cos32, sin32,
                                      _slab_vec(mla_q_nope_norm[j], mla_q_rope_norm[j]),
                                      _slab_vec(mla_k_nope_norm[j]),
                                      _slab_vec(zero64, mla_k_rope_norm[j]),
                                      mla_cq_norm[j], mla_ckv_norm[j],
                                      bf(_head_slabs(mla_w_uq[j], MLA_NOPE + MLA_ROPE)), bf(mla_w_ukv[j]))
            y_c = causal_attention(q_c, k_c, vt_c, n_sm=1, hps=16, name="mla_attention")
            q_d, k_d, vt_d = diff_prep(u, cos64, sin64, _slab_vec(diff_q_norm[j], diff_q_norm[j]),
                                       _slab_vec(diff_k_norm[j], diff_k_norm[j]))
            lambda_init = 0.8 - 0.6 * math.exp(-0.3 * layer)
            lam = (jnp.exp(jnp.sum(diff_lq1[j].astype(F32) * diff_lk1[j].astype(F32)))
                   - jnp.exp(jnp.sum(diff_lq2[j].astype(F32) * diff_lk2[j].astype(F32))) + lambda_init)
            y_d = causal_attention(q_d, k_d, vt_d, n_sm=2, hps=8, lam=lam, g=diff_subln[j],
                                   out_scale=1.0 - lambda_init, name="diff_attention")
            mixed, w_out, split = (y_c, y_d), bf(cd_w_out[j]), MLA_HEADS * MLA_V
        x = mix_out_mem_attention(x.reshape(b, s, d), mixed[0], mixed[1], w_out[:split], w_out[split:], mem_kv,
                                  memx_norm[layer], bf(memx_w_q[layer]), memx_q_norm[layer], mem_k_norm,
                                  bf(memx_w_o[layer])).reshape(t, d)
        x = ffn(x, ffn2_norm[layer], ffn2_w_gate, ffn2_w_up, ffn2_w_down, layer, name="ffn2")
    return x.reshape(b, s, d)
```

```python
import functools
import math

import jax
import jax.numpy as jnp
from jax import lax
from jax.experimental import pallas as pl
from jax.experimental.pallas import tpu as pltpu

F32 = jnp.float32
BF16 = jnp.bfloat16

EPS = 1e-6
ROPE_THETA = 10000.0
NEG_INF = -1e30
LOG2E = math.log2(math.e)
ATTN_LOOKAHEAD = 8
SWA_LOOKAHEAD = 4
ATTN_TILE = 256
SEG_W = 256
LANES = 128

D_MODEL = 2048
D_FF = 5632
HEAD_DIM = 64
SWA_HEADS = 16
SWA_KV_HEADS = 4
SWA_BLOCK = 128
RWKV_DIM = 1024
RWKV_CHUNK = 64
RWKV_GN_EPS = 64e-5
DECAY_LORA, AAA_LORA, GATE_LORA = 64, 64, 160
MLA_HEADS, MLA_Q_RANK, MLA_KV_RANK, MLA_NOPE, MLA_ROPE, MLA_V = 16, 512, 256, 64, 32, 64
DIFF_HEADS, DIFF_QK, DIFF_V = 8, 64, 128
MEM_HEADS, MEM_HEAD_DIM = 4, 128
MEM_W = MEM_HEADS * MEM_HEAD_DIM
AB_IN_PAD = 5120
CD_IN_PAD = 4096

VMEM_LIMIT = 48 * 1024 * 1024
FFN_VMEM_LIMIT = 60 * 1024 * 1024


def _params(*sem):
    return pltpu.CompilerParams(dimension_semantics=sem, vmem_limit_bytes=VMEM_LIMIT)


def _dot(a, b):
    return jnp.dot(a, b, preferred_element_type=F32)


def _dot_t(a, b):
    return lax.dot_general(a, b, (((1,), (1,)), ((), ())), preferred_element_type=F32)


def _dot_0(a, b):
    return lax.dot_general(a, b, (((0,), (0,)), ((), ())), preferred_element_type=F32)


def _rms(x, g):
    return x * lax.rsqrt(jnp.mean(x * x, axis=-1, keepdims=True) + EPS) * g


def _seg_sum(x, e):
    xb = x.astype(BF16)
    w = e.shape[0]
    parts = [_dot(xb[:, i:i + w], e) for i in range(0, x.shape[1], w)]
    return parts[0] if len(parts) == 1 else jnp.concatenate(parts, axis=-1)


def _tile_lanes(v, width):
    return v if v.shape[-1] == width else jnp.tile(v, (1, width // v.shape[-1]))


def _store_transposed_tiles(vt_ref, v):
    for r in range(vt_ref.shape[1]):
        vt_ref[0, r] = v[r * ATTN_TILE:(r + 1) * ATTN_TILE, :].T.astype(vt_ref.dtype)


def _swap_halves(x, half):
    w = x.shape[-1]
    lane = lax.broadcasted_iota(jnp.int32, x.shape, x.ndim - 1)
    low = (lane & (2 * half - 1)) < half
    return jnp.where(low, pltpu.roll(x, w - half, x.ndim - 1), pltpu.roll(x, half, x.ndim - 1))


def _norm_matmul_kernel(x_ref, g_ref, *refs, starts, counts):
    w_refs, o_ref, xn_ref = refs[:len(starts)], refs[len(starts)], refs[len(starts) + 1]
    j = pl.program_id(1)

    @pl.when(j == 0)
    def _():
        xn_ref[...] = _rms(x_ref[...], g_ref[...]).astype(BF16)

    for w_ref, start, count in zip(w_refs, starts, counts):
        @pl.when((j >= start) & (j < start + count))
        def _(w_ref=w_ref):
            o_ref[...] = _dot_t(xn_ref[...], w_ref[...])


def norm_matmul(x, g, segments, *, tm=1024, tn=512, name="norm_matmul"):
    t, k = x.shape
    tm = min(tm, t)
    assert t % tm == 0 and all(w.shape[0] % tn == 0 and w.shape[1] == k for w in segments)
    counts = [w.shape[0] // tn for w in segments]
    starts = [sum(counts[:i]) for i in range(len(counts))]
    w_spec = lambda start, count: pl.BlockSpec((tn, k), lambda i, j: (jnp.clip(j - start, 0, count - 1), 0))
    return pl.pallas_call(
        functools.partial(_norm_matmul_kernel, starts=tuple(starts), counts=tuple(counts)),
        grid=(t // tm, sum(counts)),
        in_specs=[pl.BlockSpec((tm, k), lambda i, j: (i, 0)),
                  pl.BlockSpec((1, k), lambda i, j: (0, 0))]
                 + [w_spec(start, count) for start, count in zip(starts, counts)],
        out_specs=pl.BlockSpec((tm, tn), lambda i, j: (i, j)),
        out_shape=jax.ShapeDtypeStruct((t, sum(counts) * tn), F32),
        scratch_shapes=[pltpu.VMEM((tm, k), BF16)],
        compiler_params=_params("parallel", "arbitrary"),
        name=name,
    )(x, g.reshape(1, k).astype(F32), *segments)


def _ffn_kernel(x_ref, g_ref, wg_ref, wu_ref, wd_ref, o_ref, xn_ref):
    @pl.when(pl.program_id(1) == 0)
    def _():
        x = x_ref[...]
        xn_ref[...] = _rms(x, g_ref[...]).astype(BF16)
        o_ref[...] = x

    xn = xn_ref[...]
    a = _dot(xn, wg_ref[...].astype(BF16))
    b = _dot(xn, wu_ref[...].astype(BF16))
    h = (a * (0.5 / (1.0 + jnp.exp(-a))) * b).astype(BF16)
    o_ref[...] += _dot(h, wd_ref[...].astype(BF16))


def ffn(x, g, wg, wu, wd, layer, *, tm=1024, tf=256, name="ffn"):
    t, d = x.shape
    ff = wg.shape[2]
    tm = min(tm, t)
    assert t % tm == 0 and ff % tf == 0
    nf = ff // tf
    return pl.pallas_call(
        _ffn_kernel,
        grid=(t // tm, nf),
        in_specs=[pl.BlockSpec((tm, d), lambda i, f: (i, 0)),
                  pl.BlockSpec((1, d), lambda i, f: (0, 0)),
                  pl.BlockSpec((None, d, tf), lambda i, f: (layer, 0, f)),
                  pl.BlockSpec((None, d, tf), lambda i, f: (layer, 0, f)),
                  pl.BlockSpec((None, tf, d), lambda i, f: (layer, f, 0))],
        out_specs=pl.BlockSpec((tm, d), lambda i, f: (i, 0)),
        out_shape=jax.ShapeDtypeStruct((t, d), F32),
        scratch_shapes=[pltpu.VMEM((tm, d), BF16)],
        compiler_params=pltpu.CompilerParams(dimension_semantics=("parallel", "arbitrary"),
                                             vmem_limit_bytes=FFN_VMEM_LIMIT),
        name=name,
    )(x, g.reshape(1, d).astype(F32), wg, wu, wd)


def _swa_kernel(sink_ref, q_ref, kc_ref, kp_ref, vc_ref, vp_ref, cc_ref, sc_ref, cp_ref, sp_ref,
                gq_ref, gk_ref, e_ref, o_ref):
    n = pl.program_id(1)
    blk = SWA_BLOCK
    group = SWA_HEADS // SWA_KV_HEADS
    q = q_ref[0]
    k = jnp.concatenate([kp_ref[0], kc_ref[0]], axis=0)
    v = jnp.concatenate([vp_ref[0], vc_ref[0]], axis=0)
    cos_q, sin_q = cc_ref[0], sc_ref[0]
    cos_k = jnp.concatenate([cp_ref[0], cos_q], axis=0)
    sin_k = jnp.concatenate([sp_ref[0], sin_q], axis=0)
    def prep(x, g, cos, sin, scale):
        w = x.shape[1]
        xg = x * _tile_lanes(g, w)
        xr = xg * _tile_lanes(cos, w) + _swap_halves(xg, HEAD_DIM // 2) * _tile_lanes(sin, w)
        return xr * lax.rsqrt(_seg_sum(x * x, e_ref[...]) * (1.0 / HEAD_DIM) + EPS) * scale

    qr = prep(q, gq_ref[...], cos_q, sin_q, HEAD_DIM ** -0.5 * LOG2E).astype(BF16)
    kr = prep(k, gk_ref[...], cos_k, sin_k, 1.0)
    key_i = lax.broadcasted_iota(jnp.int32, (2 * blk, blk), 0)
    qry_i = lax.broadcasted_iota(jnp.int32, (2 * blk, blk), 1)
    rel = qry_i + blk - key_i
    valid = (rel >= 0) & (rel < blk) & ((n > 0) | (key_i >= blk))
    low = lax.broadcasted_iota(jnp.int32, (blk, LANES), 1) < HEAD_DIM
    zero = jnp.zeros((blk, LANES), BF16)
    k_dup, v_t = [], []
    for g in range(SWA_KV_HEADS):
        kg = kr[:, g * HEAD_DIM:(g + 1) * HEAD_DIM]
        k_dup.append(jnp.concatenate([kg, kg], axis=-1).astype(BF16))
        v_t.append(v[:, g * HEAD_DIM:(g + 1) * HEAD_DIM].T.astype(BF16))

    def scores(h):
        slab = qr[:, (h // 2) * LANES:(h // 2 + 1) * LANES]
        qh = jnp.where(low, slab, zero) if h % 2 == 0 else jnp.where(low, zero, slab)
        return _dot_t(k_dup[h // group], qh)

    ahead = [scores(h) for h in range(SWA_LOOKAHEAD)]
    ones = jnp.ones((8, 2 * blk), BF16)
    outs = []
    for h in range(SWA_HEADS):
        s = jnp.where(valid, ahead.pop(0), NEG_INF)
        if h + SWA_LOOKAHEAD < SWA_HEADS:
            ahead.append(scores(h + SWA_LOOKAHEAD))
        sink = sink_ref[h] * LOG2E
        m = jnp.maximum(jnp.max(s, axis=0, keepdims=True), sink)
        p = jnp.exp2(s - m).astype(BF16)
        den = _dot(ones, p)[0:1] + jnp.exp2(sink - m)
        outs.append(_dot(v_t[h // group], p) / den)
    slabs = [jnp.concatenate(outs[i:i + 2], axis=0).T for i in range(0, SWA_HEADS, 2)]
    o_ref[0] = jnp.concatenate(slabs, axis=-1).astype(o_ref.dtype)


def swa_attention(u, cos, sin, gq, gk, sinks):
    b, s, _ = u.shape
    nb = s // SWA_BLOCK
    qw, kw = SWA_HEADS * HEAD_DIM, SWA_KV_HEADS * HEAD_DIM
    cur = lambda c: (lambda i, n: (i, n, c))
    prev = lambda c: (lambda i, n: (i, jnp.maximum(n - 1, 0), c))
    tab = pl.BlockSpec((1, SWA_BLOCK, LANES), cur(0))
    tab_prev = pl.BlockSpec((1, SWA_BLOCK, LANES), prev(0))
    gain = pl.BlockSpec((1, LANES), lambda i, n: (0, 0))
    seg = (jnp.arange(kw)[:, None] // HEAD_DIM == jnp.arange(kw)[None, :] // HEAD_DIM).astype(BF16)
    return pl.pallas_call(
        _swa_kernel,
        grid=(b, nb),
        in_specs=[pl.BlockSpec(memory_space=pltpu.SMEM),
                  pl.BlockSpec((1, SWA_BLOCK, qw), cur(0)),
                  pl.BlockSpec((1, SWA_BLOCK, kw), cur(qw // kw)),
                  pl.BlockSpec((1, SWA_BLOCK, kw), prev(qw // kw)),
                  pl.BlockSpec((1, SWA_BLOCK, kw), cur(qw // kw + 1)),
                  pl.BlockSpec((1, SWA_BLOCK, kw), prev(qw // kw + 1)),
                  tab, tab, tab_prev, tab_prev, gain, gain, pl.BlockSpec((kw, kw), lambda i, n: (0, 0))],
        out_specs=pl.BlockSpec((1, SWA_BLOCK, qw), cur(0)),
        out_shape=jax.ShapeDtypeStruct((b, s, qw), BF16),
        compiler_params=_params("parallel", "arbitrary"),
        name="swa_attention",
    )(sinks.astype(F32), u, u, u, u, u, cos, sin, cos, sin, gq, gk, seg)


def _mm(a, b, dims, passes):
    dn = (dims, ((), ()))
    dg = lambda x, y: lax.dot_general(x, y, dn, preferred_element_type=F32)
    ah = a.astype(BF16)
    bh = b.astype(BF16)
    if passes == 1:
        return dg(ah, bh)
    al = (a - ah.astype(F32)).astype(BF16)
    bl = (b - bh.astype(F32)).astype(BF16)
    return dg(ah, bh) + dg(ah, bl) + dg(al, bh)


_NN = ((1,), (0,))
_NT = ((1,), (1,))
_TN = ((0,), (0,))
P_SC, P_INV, P_PQ, P_OUT, P_ST = 1, 1, 1, 1, 1
RWKV_UNROLL = 8


RWKV_PREP_ROWS = 128


def _rwkv_kernel(ur_ref, uk_ref, uv_ref, ul_ref, mur_ref, muk_ref, muv_ref, mul_ref, w0_ref, w2_ref, a0_ref,
                 a2_ref, g2_ref, kkw_ref, ka_ref, gng_ref, gnb_ref, rk_ref,
                 o_ref, r_ref, k_ref, v_ref, kk_ref, b_ref, lw_ref, g_ref,
                 st_ref, y1_ref, y0_ref, n_ref, z_ref, dec_ref, *, nchunk):
    c = RWKV_CHUNK
    lane_c = lax.broadcasted_iota(jnp.int32, (c, LANES), 1)
    head0 = lane_c < HEAD_DIM
    ri = lax.broadcasted_iota(jnp.int32, (2 * c, 2 * c), 0)
    ci = lax.broadcasted_iota(jnp.int32, (2 * c, 2 * c), 1)
    eye = jnp.where(ri == ci, 1.0, 0.0)
    tril_c = jnp.where(lax.broadcasted_iota(jnp.int32, (c, c), 0) >= lax.broadcasted_iota(jnp.int32, (c, c), 1),
                       1.0, 0.0).astype(BF16)
    stack = lambda x: jnp.concatenate([jnp.where(head0, x, 0.0), jnp.where(head0, 0.0, x)], axis=0)

    def seg_mean(x):
        first = lax.broadcasted_iota(jnp.int32, x.shape, 1) < HEAD_DIM
        m0 = jnp.sum(jnp.where(first, x, 0.0), axis=-1, keepdims=True)
        m1 = jnp.sum(jnp.where(first, 0.0, x), axis=-1, keepdims=True)
        return jnp.where(first, m0, m1) * (1.0 / HEAD_DIM)

    def prep(io, first_group):
        group_rows = RWKV_UNROLL * c
        for tix in range(group_rows // RWKV_PREP_ROWS):
            start = pl.multiple_of(io * group_rows + tix * RWKV_PREP_ROWS, RWKV_PREP_ROWS)
            rows = pl.ds(start, RWKV_PREP_ROWS)
            at_start = first_group and tix == 0

            def shifted(ref, mu_ref):
                x = ref[0, rows, :]
                if at_start:
                    last = jnp.zeros((1, x.shape[1]), F32)
                else:
                    last = ref[0, pl.ds(pl.multiple_of(start - 8, 8), 8), :][7:8, :]
                row = lax.broadcasted_iota(jnp.int32, x.shape, 0)
                prev = jnp.where(row == 0, last, pltpu.roll(x, 1, 0))
                return x + (prev - x) * mu_ref[...]

            r = shifted(ur_ref, mur_ref)
            k = shifted(uk_ref, muk_ref)
            v = shifted(uv_ref, muv_ref)
            lo = shifted(ul_ref, mul_ref)
            yield
            w_lo, a_lo, g_lo = lo[:, 0:LANES], lo[:, LANES:2 * LANES], lo[:, 2 * LANES:4 * LANES]
            z = -(w0_ref[...] + _mm(jnp.tanh(w_lo), w2_ref[...], _NN, 3))
            w = -(jnp.maximum(z, 0.0) + jnp.log(1.0 + jnp.exp(-jnp.abs(z)))) - 0.5
            a = 1.0 / (1.0 + jnp.exp(-(a0_ref[...] + _mm(a_lo, a2_ref[...], _NN, 1))))
            g = _mm(1.0 / (1.0 + jnp.exp(-g_lo)), g2_ref[...], _NN, 1)
            yield
            kk = k * kkw_ref[...]
            kk = kk / jnp.maximum(jnp.sqrt(seg_mean(kk * kk) * float(HEAD_DIM)), 1e-12)
            r_ref[rows, :] = r
            k_ref[rows, :] = k * (1.0 + (a - 1.0) * ka_ref[...])
            v_ref[rows, :] = v
            kk_ref[rows, :] = kk
            b_ref[rows, :] = kk * a
            lw_ref[rows, :] = -jnp.exp(w)
            g_ref[rows, :] = g
            yield

    def build(ics):
        each = lambda f, *cols: [f(*args) for args in zip(*cols)]
        sls = [pl.ds(pl.multiple_of(ic * c, c), c) for ic in ics]
        load = lambda ref: [ref[sl, :] for sl in sls]
        r, k, v, kk, b, lw = (load(ref) for ref in (r_ref, k_ref, v_ref, kk_ref, b_ref, lw_ref))

        def running_sum(x):
            l1 = x.astype(BF16)
            rest = x - l1.astype(F32)
            l2 = rest.astype(BF16)
            l3 = (rest - l2.astype(F32)).astype(BF16)
            return _dot(tril_c, l1) + _dot(tril_c, l2) + _dot(tril_c, l3)

        cum = each(running_sum, lw)
        yield
        cum_end = [x[c - 1:c, :] for x in cum]
        e_neg = each(lambda x: jnp.exp(-x), cum)
        e_end = each(lambda x, xe: jnp.exp(xe - x), cum, cum_end)
        a_s = each(lambda kk_, x, l: stack(-kk_ * jnp.exp(x - l)), kk, cum, lw)
        r_s = each(lambda r_, x: stack(r_ * jnp.exp(x)), r, cum)
        b_s = each(lambda b_, e: stack(b_ * e), b, e_neg)
        k_s = each(lambda k_, e: stack(k_ * e), k, e_neg)
        bh_s = each(lambda b_, e: stack(b_ * e), b, e_end)
        kh_s = each(lambda k_, e: stack(k_ * e), k, e_end)
        v_s = each(stack, v)
        n2 = 2 * c
        sc = each(lambda a_, r_, b_, k_: _mm(jnp.concatenate([a_, r_], axis=0),
                                             jnp.concatenate([b_, k_], axis=0), _NT, P_SC), a_s, r_s, b_s, k_s)
        low = [jnp.where(ri > ci, x[:n2, :n2], 0.0) for x in sc]
        a_ak = [jnp.where(ri > ci, x[:n2, n2:], 0.0) for x in sc]
        a_rb = [jnp.where(ri >= ci, x[n2:, :n2], 0.0) for x in sc]
        a_rk = [jnp.where(ri >= ci, x[n2:, n2:], 0.0) for x in sc]
        yield
        inv = [eye + x for x in low]
        pw = each(lambda x: _mm(x, x, _NN, P_INV), low)
        yield
        levels = 5
        for lvl in range(levels - 1):
            both = each(lambda t, x: _mm(jnp.concatenate([t, x], axis=0), x, _NN, P_INV), inv, pw)
            inv = each(lambda t, r: t + r[:n2], inv, both)
            pw = [r[n2:] for r in both]
            yield
        inv = each(lambda t, x: t + _mm(t, x, _NN, P_INV), inv, pw)
        yield
        akv = each(lambda x, y: _mm(x, y, _NN, P_PQ), a_ak, v_s)
        yield
        pq = each(lambda t, x, y: _mm(t, jnp.concatenate([x, y], axis=1), _NN, P_PQ), inv, a_s, akv)
        yield
        yy = each(lambda x, y: _mm(x, y, _NN, P_OUT), a_rb, pq)
        y0b = each(lambda x, y: _mm(x, y, _NN, P_OUT), a_rk, v_s)
        nz = each(lambda x, y: _mm(x, y, _TN, P_OUT), pq, bh_s)
        zb = each(lambda x, y: _mm(x, y, _TN, P_OUT), v_s, kh_s)
        for i, ic in enumerate(ics):
            y1_ref[ic] = r_s[i] + yy[i][:, :LANES]
            y0_ref[ic] = yy[i][:, LANES:] + y0b[i]
            n_ref[ic] = nz[i][:LANES]
            z_ref[ic] = nz[i][LANES:] + zb[i]
            dec_ref[ic] = jnp.broadcast_to(jnp.exp(cum_end[i]), (8, LANES))

    def emit(ic, st):
        sl = pl.ds(pl.multiple_of(ic * c, c), c)
        r, k, v, g = r_ref[sl, :], k_ref[sl, :], v_ref[sl, :], g_ref[sl, :]
        y_st = _mm(y1_ref[ic], st, _NT, P_ST) + y0_ref[ic]
        y = y_st[0:c] + y_st[c:2 * c]
        mean = seg_mean(y)
        var = seg_mean((y - mean) * (y - mean))
        yn = (y - mean) * lax.rsqrt(var + RWKV_GN_EPS) * gng_ref[...] + gnb_ref[...]
        bonus = seg_mean(r * k * rk_ref[...]) * float(HEAD_DIM) * v
        o_ref[0, sl, :] = ((yn + bonus) * g).astype(o_ref.dtype)
        return st * dec_ref[ic][0:1, :] + _mm(st, n_ref[ic], _NN, P_ST) + z_ref[ic]

    def scan(ics):
        st = st_ref[...]
        for ic in ics:
            st = emit(ic, st)
            yield
        st_ref[...] = st

    def run(*gens):
        live = list(gens)
        while live:
            live = [gen for gen in live if next(gen, live) is not live]

    group = lambda io: [io * RWKV_UNROLL + i for i in range(RWKV_UNROLL)]
    ngroup = nchunk // RWKV_UNROLL
    st_ref[...] = jnp.zeros_like(st_ref)
    run(prep(0, True))
    if ngroup > 1:
        run(prep(1, False), build(group(0)))
    else:
        run(build(group(0)))

    @pl.loop(1, ngroup - 1)
    def _(io):
        run(prep(io + 1, False), build(group(io)), scan(group(io - 1)))

    if ngroup > 1:
        run(build(group(ngroup - 1)), scan(group(ngroup - 2)))
    run(scan(group(ngroup - 1)))


def rwkv_mix(u, mu, w0, w2, a0, a2, g2, k_k, k_a, r_k, gn_g, gn_b):
    bsz, s, _ = u.shape
    npair = RWKV_DIM // LANES
    nchunk = s // RWKV_CHUNK
    assert nchunk % RWKV_UNROLL == 0 and (RWKV_UNROLL * RWKV_CHUNK) % RWKV_PREP_ROWS == 0
    base = (SWA_HEADS + 2 * SWA_KV_HEADS) * HEAD_DIM // LANES
    lora_w = 4 * LANES
    slab = lambda off: pl.BlockSpec((1, s, LANES), lambda i, p: (i, 0, off + p))
    vec = pl.BlockSpec((1, LANES), lambda i, p: (0, p))
    cols = lambda rows: pl.BlockSpec((rows, LANES), lambda i, p: (0, p))
    row = lambda vv: vv.reshape(1, -1).astype(F32)
    pad_rows = lambda m, rows: jnp.pad(m, ((0, rows - m.shape[0]), (0, 0))).astype(F32)
    pad_cols = lambda vv, n: jnp.pad(vv, (0, n - vv.shape[0]))
    c3 = 3 * RWKV_DIM
    mu_l = jnp.concatenate([pad_cols(mu[c3:c3 + DECAY_LORA], LANES),
                            pad_cols(mu[c3 + DECAY_LORA:c3 + DECAY_LORA + AAA_LORA], LANES),
                            pad_cols(mu[c3 + DECAY_LORA + AAA_LORA:], 2 * LANES)])
    seq = pltpu.VMEM((s, LANES), F32)
    mat = pltpu.VMEM((nchunk, LANES, LANES), F32)
    return pl.pallas_call(
        functools.partial(_rwkv_kernel, nchunk=nchunk),
        grid=(bsz, npair),
        in_specs=[slab(base), slab(base + npair), slab(base + 2 * npair),
                  pl.BlockSpec((1, s, lora_w), lambda i, p: (i, 0, (base + 3 * npair) * LANES // lora_w)),
                  vec, vec, vec, pl.BlockSpec((1, lora_w), lambda i, p: (0, 0)),
                  vec, cols(LANES), vec, cols(LANES), cols(2 * LANES), vec, vec, vec, vec, vec],
        out_specs=pl.BlockSpec((1, s, LANES), lambda i, p: (i, 0, p)),
        out_shape=jax.ShapeDtypeStruct((bsz, s, RWKV_DIM), BF16),
        scratch_shapes=[seq] * 7 + [pltpu.VMEM((LANES, LANES), F32), mat, mat, mat, mat,
                                    pltpu.VMEM((nchunk, 8, LANES), F32)],
        compiler_params=_params("parallel", "arbitrary"),
        name="rwkv_mix",
    )(u, u, u, u, row(mu[:RWKV_DIM]), row(mu[RWKV_DIM:2 * RWKV_DIM]), row(mu[2 * RWKV_DIM:c3]), row(mu_l),
      row(w0), pad_rows(w2, LANES), row(a0), pad_rows(a2, LANES), pad_rows(g2, 2 * LANES),
      row(k_k), row(k_a), row(gn_g), row(gn_b), row(r_k))


def _mla_prep_kernel(cq_ref, ckv_ref, pe_ref, cos_ref, sin_ref, e_ref, gq_ref, gkn_ref, gkp_ref, invn_ref,
                     gcq_ref, gckv_ref, wuq_ref, wukv_ref, qo_ref, ko_ref, vo_ref, cqn_ref, ckvn_ref, kpe_ref):
    cos, sin = cos_ref[0], sin_ref[0]
    half = MLA_ROPE // 2

    def rope(x, g):
        w = x.shape[1]
        xg = x * _tile_lanes(g, w)
        return xg * _tile_lanes(cos, w) + _swap_halves(xg, half) * _tile_lanes(sin, w)

    @pl.when(pl.program_id(2) == 0)
    def _():
        cqn_ref[...] = _rms(cq_ref[0], gcq_ref[...]).astype(BF16)
        ckvn_ref[...] = _rms(ckv_ref[0], gckv_ref[...]).astype(BF16)
        pe = pe_ref[0]
        inv_pe = lax.rsqrt(jnp.sum(pe * pe, axis=-1, keepdims=True) * (1.0 / MLA_ROPE) + EPS)
        kpe_ref[...] = rope(pe, gkp_ref[...]) * inv_pe

    x = _dot(cqn_ref[...], wuq_ref[...])
    w = x.shape[1]
    inv_n = _tile_lanes(invn_ref[...], w)
    inv = lax.rsqrt(_seg_sum(x * x, e_ref[...]) * inv_n + EPS)
    qo_ref[0] = (rope(x, gq_ref[...]) * inv * ((MLA_NOPE + MLA_ROPE) ** -0.5 * LOG2E)).astype(qo_ref.dtype)
    kv = _dot(ckvn_ref[...], wukv_ref[...])
    inv_k = lax.rsqrt(_seg_sum(kv * kv, e_ref[...]) * inv_n + EPS)
    k_nope = kv * inv_k * _tile_lanes(gkn_ref[...], w)
    ko_ref[0] = (k_nope + _tile_lanes(kpe_ref[...], w)).astype(ko_ref.dtype)
    v = jnp.concatenate([kv[:, h * LANES + MLA_NOPE:(h + 1) * LANES] for h in range(w // LANES)], axis=-1)
    _store_transposed_tiles(vo_ref, v)


def mla_prep(u, cos, sin, gq, gkn, gkp, gcq, gckv, w_uq, w_ukv, *, ts=1024, tc=512):
    b, s, _ = u.shape
    wtot = w_uq.shape[1]
    ts = min(ts, s)
    lane = jnp.arange(SEG_W)
    same = (lane[:, None] // LANES == lane[None, :] // LANES)
    pos = lane % LANES
    nope = pos < MLA_NOPE
    pe = (pos >= MLA_NOPE) & (pos < MLA_NOPE + MLA_ROPE)
    seg = (same & ((nope[:, None] & nope[None, :]) | (pe[:, None] & pe[None, :]))).astype(BF16)
    p1 = jnp.arange(LANES)
    inv_n = jnp.where(p1 < MLA_NOPE, 1.0 / MLA_NOPE, jnp.where(p1 < MLA_NOPE + MLA_ROPE, 1.0 / MLA_ROPE, 0.0))
    blk = pl.BlockSpec((1, ts, tc), lambda i, t, c: (i, t, c))
    tab = pl.BlockSpec((1, ts, LANES), lambda i, t, c: (i, t, 0))
    vec = pl.BlockSpec((1, LANES), lambda i, t, c: (0, 0))
    pe_blk = (MLA_Q_RANK + MLA_KV_RANK) // LANES
    out = jax.ShapeDtypeStruct((b, s, wtot), BF16)
    return pl.pallas_call(
        _mla_prep_kernel,
        grid=(b, s // ts, wtot // tc),
        in_specs=[pl.BlockSpec((1, ts, MLA_Q_RANK), lambda i, t, c: (i, t, 0)),
                  pl.BlockSpec((1, ts, MLA_KV_RANK), lambda i, t, c: (i, t, MLA_Q_RANK // MLA_KV_RANK)),
                  pl.BlockSpec((1, ts, LANES), lambda i, t, c: (i, t, pe_blk)), tab, tab,
                  pl.BlockSpec((SEG_W, SEG_W), lambda i, t, c: (0, 0)), vec, vec, vec, vec,
                  pl.BlockSpec((1, MLA_Q_RANK), lambda i, t, c: (0, 0)),
                  pl.BlockSpec((1, MLA_KV_RANK), lambda i, t, c: (0, 0)),
                  pl.BlockSpec((MLA_Q_RANK, tc), lambda i, t, c: (0, c)),
                  pl.BlockSpec((MLA_KV_RANK, tc), lambda i, t, c: (0, c))],
        out_specs=[blk, blk, pl.BlockSpec((1, ts // ATTN_TILE, tc // LANES * MLA_V, ATTN_TILE),
                                          lambda i, t, c: (i, t, c, 0))],
        out_shape=[out, out, jax.ShapeDtypeStruct((b, s // ATTN_TILE, wtot // LANES * MLA_V, ATTN_TILE), BF16)],
        scratch_shapes=[pltpu.VMEM((ts, MLA_Q_RANK), BF16), pltpu.VMEM((ts, MLA_KV_RANK), BF16),
                        pltpu.VMEM((ts, LANES), F32)],
        compiler_params=_params("parallel", "parallel", "arbitrary"),
        name="mla_prep",
    )(u, u, u, cos, sin, seg, gq, gkn, gkp, inv_n.reshape(1, LANES).astype(F32),
      gcq.reshape(1, -1).astype(F32), gckv.reshape(1, -1).astype(F32), w_uq, w_ukv)


def _diff_prep_kernel(q_ref, k_ref, v_ref, cos_ref, sin_ref, e_ref, gq_ref, gk_ref, qo_ref, ko_ref, vo_ref):
    cos, sin = cos_ref[0], sin_ref[0]
    _store_transposed_tiles(vo_ref, v_ref[0])

    def prep(x, g, scale):
        w = x.shape[1]
        xg = x * _tile_lanes(g, w)
        xr = xg * _tile_lanes(cos, w) + _swap_halves(xg, DIFF_QK // 2) * _tile_lanes(sin, w)
        inv = lax.rsqrt(_seg_sum(x * x, e_ref[...]) * (1.0 / DIFF_QK) + EPS)
        return xr * inv * scale

    qo_ref[0] = prep(q_ref[0], gq_ref[...], DIFF_QK ** -0.5 * LOG2E).astype(qo_ref.dtype)
    ko_ref[0] = prep(k_ref[0], gk_ref[...], 1.0).astype(ko_ref.dtype)


def diff_prep(u, cos, sin, gq, gk, *, ts=1024, tc=512):
    b, s, _ = u.shape
    ts = min(ts, s)
    wtot = 2 * DIFF_HEADS * DIFF_QK
    q_base = (CD_IN_PAD - 3 * wtot) // tc
    seg = (jnp.arange(SEG_W)[:, None] // DIFF_QK == jnp.arange(SEG_W)[None, :] // DIFF_QK).astype(BF16)
    blk = lambda off: pl.BlockSpec((1, ts, tc), lambda i, t, c: (i, t, off + c))
    tab = pl.BlockSpec((1, ts, LANES), lambda i, t, c: (i, t, 0))
    vec = pl.BlockSpec((1, LANES), lambda i, t, c: (0, 0))
    out = jax.ShapeDtypeStruct((b, s, wtot), BF16)
    return pl.pallas_call(
        _diff_prep_kernel,
        grid=(b, s // ts, wtot // tc),
        in_specs=[blk(q_base), blk(q_base + wtot // tc), blk(q_base + 2 * wtot // tc), tab, tab,
                  pl.BlockSpec((SEG_W, SEG_W), lambda i, t, c: (0, 0)), vec, vec],
        out_specs=[blk(0), blk(0),
                   pl.BlockSpec((1, ts // ATTN_TILE, tc, ATTN_TILE), lambda i, t, c: (i, t, c, 0))],
        out_shape=[out, out, jax.ShapeDtypeStruct((b, s // ATTN_TILE, wtot, ATTN_TILE), BF16)],
        compiler_params=_params("parallel", "parallel", "arbitrary"),
        name="diff_prep",
    )(u, u, u, cos, sin, seg, gq, gk)


def _causal_attn_kernel(lam_ref, q_ref, k_ref, vt_ref, g_ref, o_ref, *, n_sm, tq, ow, out_scale):
    qi = pl.program_id(2)
    q = q_ref[0]
    hps = q.shape[1] // LANES
    slab = lambda x, h: x[:, h * LANES:(h + 1) * LANES]
    lane = lax.broadcasted_iota(jnp.int32, (tq, LANES), 1)
    qs, src = [], []
    for h in range(hps):
        qh = slab(q, h)
        if n_sm == 2:
            zero = jnp.zeros_like(qh)
            qs += [jnp.where(lane < DIFF_QK, qh, zero), jnp.where(lane < DIFF_QK, zero, qh)]
            src += [h, h]
        else:
            qs.append(qh)
            src.append(h)
    nch = len(qs)
    key_i = lax.broadcasted_iota(jnp.int32, (tq, tq), 0)
    qry_i = lax.broadcasted_iota(jnp.int32, (tq, tq), 1)
    ones = jnp.ones((8, tq), BF16)

    def step(j, carry, diagonal):
        kj = k_ref[0, pl.ds(pl.multiple_of(j * tq, tq), tq), :]
        scores = lambda i: _dot_t(slab(kj, src[i]), qs[i])
        new = []
        ahead = [scores(i) for i in range(min(ATTN_LOOKAHEAD, nch))]
        for i in range(nch):
            s = ahead.pop(0)
            if i + ATTN_LOOKAHEAD < nch:
                ahead.append(scores(i + ATTN_LOOKAHEAD))
            if diagonal:
                s = jnp.where(key_i <= qry_i, s, NEG_INF)
            m, l, acc = carry[3 * i:3 * i + 3]
            m_new = jnp.maximum(m, jnp.max(s, axis=0, keepdims=True))
            alpha = jnp.exp2(m - m_new)
            p = jnp.exp2(s - m_new).astype(BF16)
            new += [m_new, alpha * l + _dot(ones, p)[0:1],
                    alpha * acc + _dot(vt_ref[0, j, src[i] * ow:(src[i] + 1) * ow, :], p)]
        return tuple(new)

    init = (jnp.full((1, tq), NEG_INF, F32), jnp.zeros((1, tq), F32),
            jnp.zeros((ow, tq), F32)) * nch
    carry = lax.fori_loop(0, qi, lambda j, cr: step(j, cr, False), init)
    carry = step(qi, carry, True)
    outs = []
    for h in range(hps):
        c0 = 3 * n_sm * h
        o = carry[c0 + 2] / carry[c0 + 1]
        if n_sm == 2:
            o = o - lam_ref[0] * (carry[c0 + 5] / carry[c0 + 4])
            o = o * lax.rsqrt(jnp.mean(o * o, axis=0, keepdims=True) + EPS) * g_ref[...] * out_scale
        outs.append(o.T.astype(o_ref.dtype))
    o_ref[0] = jnp.concatenate(outs, axis=-1)


def causal_attention(q, k, vt, *, n_sm, lam=None, g=None, out_scale=1.0, hps=4, name="causal_attention"):
    b, s, wtot = q.shape
    tq = vt.shape[3]
    width = hps * LANES
    groups = wtot // width
    ow = vt.shape[2] // (wtot // LANES)
    lam = jnp.zeros((1,), F32) if lam is None else lam.reshape(1).astype(F32)
    g = jnp.ones((ow, 1), F32) if g is None else g.reshape(ow, 1).astype(F32)
    seq = pl.BlockSpec((1, s, width), lambda i, h, t: (i, 0, h))
    tile = pl.BlockSpec((1, tq, width), lambda i, h, t: (i, t, h))
    return pl.pallas_call(
        functools.partial(_causal_attn_kernel, n_sm=n_sm, tq=tq, ow=ow, out_scale=out_scale),
        grid=(b, groups, s // tq),
        in_specs=[pl.BlockSpec(memory_space=pltpu.SMEM), tile, seq,
                  pl.BlockSpec((1, s // tq, hps * ow, tq), lambda i, h, t: (i, 0, h, 0)),
                  pl.BlockSpec((ow, 1), lambda i, h, t: (0, 0))],
        out_specs=pl.BlockSpec((1, tq, hps * ow), lambda i, h, t: (i, t, h)),
        out_shape=jax.ShapeDtypeStruct((b, s, groups * hps * ow), BF16),
        compiler_params=_params("parallel", "parallel", "arbitrary"),
        name=name,
    )(lam, q, k, vt, g)


def _memx_kernel(x_ref, ya_ref, yb_ref, wa_ref, wb_ref, g_ref, wq_ref, kv_ref, gq_ref, gk_ref, wo_ref, o_ref):
    x = x_ref[0] + _dot(ya_ref[0], wa_ref[...]) + _dot(yb_ref[0], wb_ref[...])
    q = _dot(_rms(x, g_ref[...]).astype(BF16), wq_ref[...])
    kv = kv_ref[0]
    outs = []
    for h in range(MEM_HEADS):
        sl = slice(h * MEM_HEAD_DIM, (h + 1) * MEM_HEAD_DIM)
        qh = (_rms(q[:, sl], gq_ref[...]) * MEM_HEAD_DIM ** -0.5).astype(BF16)
        kh = _rms(kv[:, sl], gk_ref[...]).astype(BF16)
        vh = kv[:, MEM_W + h * MEM_HEAD_DIM:MEM_W + (h + 1) * MEM_HEAD_DIM].astype(BF16)
        s = _dot_t(qh, kh)
        p = jnp.exp(s - jnp.max(s, axis=-1, keepdims=True))
        outs.append(_dot(p.astype(BF16), vh) / jnp.sum(p, axis=-1, keepdims=True))
    o_ref[0] = x + _dot(jnp.concatenate(outs, axis=-1).astype(BF16), wo_ref[...])


def mix_out_mem_attention(x, ya, yb, wa, wb, mem_kv, g, wq, gq, gk, wo, *, tm=512):
    b, s, d = x.shape
    m = mem_kv.shape[1]
    tm = min(tm, s)
    const = lambda shape: pl.BlockSpec(shape, lambda i, t: (0,) * len(shape), pipeline_mode=pl.Buffered(1))
    tile = lambda w: pl.BlockSpec((1, tm, w), lambda i, t: (i, t, 0))
    return pl.pallas_call(
        _memx_kernel,
        grid=(b, s // tm),
        in_specs=[tile(d), tile(ya.shape[2]), tile(yb.shape[2]), const(wa.shape), const(wb.shape),
                  const((1, d)), const((d, MEM_W)),
                  pl.BlockSpec((1, m, 2 * MEM_W), lambda i, t: (i, 0, 0)),
                  const((1, MEM_HEAD_DIM)), const((1, MEM_HEAD_DIM)), const((MEM_W, d))],
        out_specs=tile(d),
        out_shape=jax.ShapeDtypeStruct((b, s, d), F32),
        compiler_params=_params("parallel", "arbitrary"),
        name="mix_out_mem_attention",
    )(x, ya, yb, wa, wb, g.reshape(1, d).astype(F32), wq, mem_kv, gq.reshape(1, -1).astype(F32),
      gk.reshape(1, -1).astype(F32), wo)


def _rope_tables(positions, dim, lead_ones, tail):
    inv = 1.0 / (ROPE_THETA ** (jnp.arange(0, dim, 2, dtype=F32) / dim))
    ang = positions.astype(F32)[..., None] * inv
    c, s = jnp.cos(ang), jnp.sin(ang)
    shape = positions.shape
    cos = jnp.concatenate([jnp.ones(shape + (lead_ones,), F32), c, c, jnp.ones(shape + (tail,), F32)], axis=-1)
    sin = jnp.concatenate([jnp.zeros(shape + (lead_ones,), F32), -s, s, jnp.zeros(shape + (tail,), F32)], axis=-1)
    reps = LANES // cos.shape[-1]
    return jnp.tile(cos, (1, 1, reps)), jnp.tile(sin, (1, 1, reps))


IN_TILE = 512


def _pad_rows(w, rows):
    return jnp.pad(w, ((0, rows - w.shape[0]), (0, 0)))


def _ab_in_segments(w):
    c = (SWA_HEADS + 2 * SWA_KV_HEADS) * HEAD_DIM + 3 * RWKV_DIM
    wt = w.T.astype(BF16)
    lora = jnp.concatenate([_pad_rows(wt[c:c + DECAY_LORA], LANES),
                            _pad_rows(wt[c + DECAY_LORA:c + DECAY_LORA + AAA_LORA], LANES),
                            _pad_rows(wt[c + DECAY_LORA + AAA_LORA:], 2 * LANES)], axis=0)
    return [wt[:c], lora]


def _cd_in_segments(w):
    c1 = MLA_Q_RANK + MLA_KV_RANK
    wt = w.T.astype(BF16)
    z = lambda n: jnp.zeros((n, wt.shape[1]), wt.dtype)
    mid = jnp.concatenate([wt[MLA_Q_RANK:c1], z(MLA_NOPE), wt[c1:c1 + MLA_ROPE],
                           z(LANES - MLA_NOPE - MLA_ROPE), z(LANES)], axis=0)
    return [wt[:MLA_Q_RANK], mid, wt[c1 + MLA_ROPE:]]


def _head_slabs(w, per_head):
    k = w.shape[0]
    return jnp.pad(w.reshape(k, -1, per_head), ((0, 0), (0, 0), (0, LANES - per_head))).reshape(k, -1)


def _slab_vec(*parts):
    v = jnp.concatenate([p.astype(F32) for p in parts])
    return jnp.pad(v, (0, LANES - v.shape[0])).reshape(1, LANES)


def kernel(x, mem, positions, ffn1_norm, ffn1_w_gate, ffn1_w_up, ffn1_w_down, mix_norm, ab_w_in, ab_w_out, swa_q_norm, swa_k_norm, swa_sinks, rwkv_mu, rwkv_w0, rwkv_w2, rwkv_a0, rwkv_a2, rwkv_g2, rwkv_k_k, rwkv_k_a, rwkv_r_k, rwkv_gn_g, rwkv_gn_b, cd_w_in, cd_w_out, mla_cq_norm, mla_ckv_norm, mla_w_uq, mla_w_ukv, mla_q_nope_norm, mla_k_nope_norm, mla_q_rope_norm, mla_k_rope_norm, diff_q_norm, diff_k_norm, diff_lq1, diff_lk1, diff_lq2, diff_lk2, diff_subln, memx_norm, memx_w_q, memx_q_norm, memx_w_o, mem_norm, mem_w_kv, mem_k_norm, ffn2_norm, ffn2_w_gate, ffn2_w_up, ffn2_w_down):
    b, s, d = x.shape
    m = mem.shape[1]
    t = b * s
    depth = ffn1_norm.shape[0]
    bf = lambda w: w.astype(BF16)
    cos64, sin64 = _rope_tables(positions, HEAD_DIM, 0, 0)
    cos32, sin32 = _rope_tables(positions, MLA_ROPE, MLA_NOPE, LANES - MLA_NOPE - MLA_ROPE)

    mem_kv = norm_matmul(mem.reshape(b * m, d), mem_norm, [bf(mem_w_kv).T],
                         tn=IN_TILE, name="mem_kv").reshape(b, m, 2 * MEM_W)

    x = x.reshape(t, d)
    for layer in range(depth):
        j = layer // 2
        x = ffn(x, ffn1_norm[layer], ffn1_w_gate, ffn1_w_up, ffn1_w_down, layer, name="ffn1")
        if layer % 2 == 0:
            u = norm_matmul(x, mix_norm[layer], _ab_in_segments(ab_w_in[j]), tn=IN_TILE, name="ab_in")
            u = u.reshape(b, s, AB_IN_PAD)
            y_a = swa_attention(u, cos64, sin64, _slab_vec(swa_q_norm[j], swa_q_norm[j]),
                                _slab_vec(swa_k_norm[j], swa_k_norm[j]), swa_sinks[j])
            y_b = rwkv_mix(u, rwkv_mu[j], rwkv_w0[j], rwkv_w2[j], rwkv_a0[j], rwkv_a2[j], rwkv_g2[j],
                           rwkv_k_k[j], rwkv_k_a[j], rwkv_r_k[j], rwkv_gn_g[j], rwkv_gn_b[j])
            mixed, w_out, split = (y_a, y_b), bf(ab_w_out[j]), SWA_HEADS * HEAD_DIM
        else:
            u = norm_matmul(x, mix_norm[layer], _cd_in_segments(cd_w_in[j]), tn=IN_TILE, name="cd_in")
            u = u.reshape(b, s, CD_IN_PAD)
            zero64 = jnp.zeros((MLA_NOPE,), F32)
            q_c, k_c, vt_c = mla_prep(u, cos32, sin32,
                                      _slab_vec(mla_q_nope_norm[j], mla_q_rope_norm[j]),
                                      _slab_vec(mla_k_nope_norm[j]),
                                      _slab_vec(zero64, mla_k_rope_norm[j]),
                                      mla_cq_norm[j], mla_ckv_norm[j],
                                      bf(_head_slabs(mla_w_uq[j], MLA_NOPE + MLA_ROPE)), bf(mla_w_ukv[j]))
            y_c = causal_attention(q_c, k_c, vt_c, n_sm=1, hps=16, name="mla_attention")
            q_d, k_d, vt_d = diff_prep(u, cos64, sin64, _slab_vec(diff_q_norm[j], diff_q_norm[j]),
                                       _slab_vec(diff_k_norm[j], diff_k_norm[j]))
            lambda_init = 0.8 - 0.6 * math.exp(-0.3 * layer)
            lam = (jnp.exp(jnp.sum(diff_lq1[j].astype(F32) * diff_lk1[j].astype(F32)))
                   - jnp.exp(jnp.sum(diff_lq2[j].astype(F32) * diff_lk2[j].astype(F32))) + lambda_init)
            y_d = causal_attention(q_d, k_d, vt_d, n_sm=2, hps=8, lam=lam, g=diff_subln[j],
                                   out_scale=1.0 - lambda_init, name="diff_attention")
            mixed, w_out, split = (y_c, y_d), bf(cd_w_out[j]), MLA_HEADS * MLA_V
        x = mix_out_mem_attention(x.reshape(b, s, d), mixed[0], mixed[1], w_out[:split], w_out[split:], mem_kv,
                                  memx_norm[layer], bf(memx_w_q[layer]), memx_q_norm[layer], mem_k_norm,
                                  bf(memx_w_o[layer])).reshape(t, d)
        x = ffn(x, ffn2_norm[layer], ffn2_w_gate, ffn2_w_up, ffn2_w_down, layer, name="ffn2")
    return x.reshape(b, s, d)
```

```python
import functools
import math

import jax
import jax.numpy as jnp
from jax import lax
from jax.experimental import pallas as pl
from jax.experimental.pallas import tpu as pltpu

F32 = jnp.float32
BF16 = jnp.bfloat16

EPS = 1e-6
ROPE_THETA = 10000.0
NEG_INF = -1e30
LOG2E = math.log2(math.e)
ATTN_LOOKAHEAD = 8
SWA_LOOKAHEAD = 4
ATTN_TILE = 256
SEG_W = 256
LANES = 128

HEAD_DIM = 64
SWA_HEADS = 16
SWA_KV_HEADS = 4
SWA_BLOCK = 128
RWKV_DIM = 1024
RWKV_CHUNK = 64
RWKV_GN_EPS = 64e-5
DECAY_LORA, AAA_LORA, GATE_LORA = 64, 64, 160
MLA_HEADS, MLA_Q_RANK, MLA_KV_RANK, MLA_NOPE, MLA_ROPE, MLA_V = 16, 512, 256, 64, 32, 64
DIFF_HEADS, DIFF_QK, DIFF_V = 8, 64, 128
MEM_HEADS, MEM_HEAD_DIM = 4, 128
MEM_W = MEM_HEADS * MEM_HEAD_DIM
AB_IN_PAD = 5120
CD_IN_PAD = 4096

VMEM_LIMIT = 48 * 1024 * 1024
FFN_VMEM_LIMIT = 60 * 1024 * 1024


def _params(*sem):
    return pltpu.CompilerParams(dimension_semantics=sem, vmem_limit_bytes=VMEM_LIMIT)


def _dot(a, b):
    return jnp.dot(a, b, preferred_element_type=F32)


def _dot_t(a, b):
    return lax.dot_general(a, b, (((1,), (1,)), ((), ())), preferred_element_type=F32)


def _rms(x, g):
    return x * lax.rsqrt(jnp.mean(x * x, axis=-1, keepdims=True) + EPS) * g


def _seg_sum(x, e):
    xb = x.astype(BF16)
    w = e.shape[0]
    parts = [_dot(xb[:, i:i + w], e) for i in range(0, x.shape[1], w)]
    return parts[0] if len(parts) == 1 else jnp.concatenate(parts, axis=-1)


def _tile_lanes(v, width):
    return v if v.shape[-1] == width else jnp.tile(v, (1, width // v.shape[-1]))


def _store_transposed_tiles(vt_ref, v):
    for r in range(vt_ref.shape[1]):
        vt_ref[0, r] = v[r * ATTN_TILE:(r + 1) * ATTN_TILE, :].T.astype(vt_ref.dtype)


def _swap_halves(x, half):
    w = x.shape[-1]
    lane = lax.broadcasted_iota(jnp.int32, x.shape, x.ndim - 1)
    low = (lane & (2 * half - 1)) < half
    return jnp.where(low, pltpu.roll(x, w - half, x.ndim - 1), pltpu.roll(x, half, x.ndim - 1))


def _norm_matmul_kernel(x_ref, g_ref, *refs, starts, counts):
    w_refs, o_ref, xn_ref = refs[:len(starts)], refs[len(starts)], refs[len(starts) + 1]
    j = pl.program_id(1)

    @pl.when(j == 0)
    def _():
        xn_ref[...] = _rms(x_ref[...], g_ref[...]).astype(BF16)

    for w_ref, start, count in zip(w_refs, starts, counts):
        @pl.when((j >= start) & (j < start + count))
        def _(w_ref=w_ref):
            o_ref[...] = _dot_t(xn_ref[...], w_ref[...])


def norm_matmul(x, g, segments, *, tm=1024, tn=512, name="norm_matmul"):
    t, k = x.shape
    tm = min(tm, t)
    assert t % tm == 0 and all(w.shape[0] % tn == 0 and w.shape[1] == k for w in segments)
    counts = [w.shape[0] // tn for w in segments]
    starts = [sum(counts[:i]) for i in range(len(counts))]
    w_spec = lambda start, count: pl.BlockSpec((tn, k), lambda i, j: (jnp.clip(j - start, 0, count - 1), 0))
    return pl.pallas_call(
        functools.partial(_norm_matmul_kernel, starts=tuple(starts), counts=tuple(counts)),
        grid=(t // tm, sum(counts)),
        in_specs=[pl.BlockSpec((tm, k), lambda i, j: (i, 0)),
                  pl.BlockSpec((1, k), lambda i, j: (0, 0))]
                 + [w_spec(start, count) for start, count in zip(starts, counts)],
        out_specs=pl.BlockSpec((tm, tn), lambda i, j: (i, j)),
        out_shape=jax.ShapeDtypeStruct((t, sum(counts) * tn), F32),
        scratch_shapes=[pltpu.VMEM((tm, k), BF16)],
        compiler_params=_params("parallel", "arbitrary"),
        name=name,
    )(x, g.reshape(1, k).astype(F32), *segments)


def _ffn_kernel(x_ref, g_ref, wg_ref, wu_ref, wd_ref, o_ref, xn_ref):
    @pl.when(pl.program_id(1) == 0)
    def _():
        x = x_ref[...]
        xn_ref[...] = _rms(x, g_ref[...]).astype(BF16)
        o_ref[...] = x

    xn = xn_ref[...]
    a = _dot(xn, wg_ref[...].astype(BF16))
    b = _dot(xn, wu_ref[...].astype(BF16))
    h = (a * (0.5 / (1.0 + jnp.exp(-a))) * b).astype(BF16)
    o_ref[...] += _dot(h, wd_ref[...].astype(BF16))


def ffn(x, g, wg, wu, wd, layer, *, tm=1024, tf=256, name="ffn"):
    t, d = x.shape
    ff = wg.shape[2]
    tm = min(tm, t)
    assert t % tm == 0 and ff % tf == 0
    nf = ff // tf
    return pl.pallas_call(
        _ffn_kernel,
        grid=(t // tm, nf),
        in_specs=[pl.BlockSpec((tm, d), lambda i, f: (i, 0)),
                  pl.BlockSpec((1, d), lambda i, f: (0, 0)),
                  pl.BlockSpec((None, d, tf), lambda i, f: (layer, 0, f)),
                  pl.BlockSpec((None, d, tf), lambda i, f: (layer, 0, f)),
                  pl.BlockSpec((None, tf, d), lambda i, f: (layer, f, 0))],
        out_specs=pl.BlockSpec((tm, d), lambda i, f: (i, 0)),
        out_shape=jax.ShapeDtypeStruct((t, d), F32),
        scratch_shapes=[pltpu.VMEM((tm, d), BF16)],
        compiler_params=pltpu.CompilerParams(dimension_semantics=("parallel", "arbitrary"),
                                             vmem_limit_bytes=FFN_VMEM_LIMIT),
        name=name,
    )(x, g.reshape(1, d).astype(F32), wg, wu, wd)


def _swa_kernel(sink_ref, q_ref, kc_ref, kp_ref, vc_ref, vp_ref, cc_ref, sc_ref, cp_ref, sp_ref,
                gq_ref, gk_ref, e_ref, o_ref):
    n = pl.program_id(1)
    blk = SWA_BLOCK
    group = SWA_HEADS // SWA_KV_HEADS
    q = q_ref[0]
    k = jnp.concatenate([kp_ref[0], kc_ref[0]], axis=0)
    v = jnp.concatenate([vp_ref[0], vc_ref[0]], axis=0)
    cos_q, sin_q = cc_ref[0], sc_ref[0]
    cos_k = jnp.concatenate([cp_ref[0], cos_q], axis=0)
    sin_k = jnp.concatenate([sp_ref[0], sin_q], axis=0)
    def prep(x, g, cos, sin, scale):
        w = x.shape[1]
        xg = x * _tile_lanes(g, w)
        xr = xg * _tile_lanes(cos, w) + _swap_halves(xg, HEAD_DIM // 2) * _tile_lanes(sin, w)
        return xr * lax.rsqrt(_seg_sum(x * x, e_ref[...]) * (1.0 / HEAD_DIM) + EPS) * scale

    qr = prep(q, gq_ref[...], cos_q, sin_q, HEAD_DIM ** -0.5 * LOG2E).astype(BF16)
    kr = prep(k, gk_ref[...], cos_k, sin_k, 1.0)
    key_i = lax.broadcasted_iota(jnp.int32, (2 * blk, blk), 0)
    qry_i = lax.broadcasted_iota(jnp.int32, (2 * blk, blk), 1)
    rel = qry_i + blk - key_i
    valid = (rel >= 0) & (rel < blk) & ((n > 0) | (key_i >= blk))
    low = lax.broadcasted_iota(jnp.int32, (blk, LANES), 1) < HEAD_DIM
    zero = jnp.zeros((blk, LANES), BF16)
    k_dup, v_t = [], []
    for g in range(SWA_KV_HEADS):
        kg = kr[:, g * HEAD_DIM:(g + 1) * HEAD_DIM]
        k_dup.append(jnp.concatenate([kg, kg], axis=-1).astype(BF16))
        v_t.append(v[:, g * HEAD_DIM:(g + 1) * HEAD_DIM].T.astype(BF16))

    def scores(h):
        slab = qr[:, (h // 2) * LANES:(h // 2 + 1) * LANES]
        qh = jnp.where(low, slab, zero) if h % 2 == 0 else jnp.where(low, zero, slab)
        return _dot_t(k_dup[h // group], qh)

    ahead = [scores(h) for h in range(SWA_LOOKAHEAD)]
    ones = jnp.ones((8, 2 * blk), BF16)
    outs = []
    for h in range(SWA_HEADS):
        s = jnp.where(valid, ahead.pop(0), NEG_INF)
        if h + SWA_LOOKAHEAD < SWA_HEADS:
            ahead.append(scores(h + SWA_LOOKAHEAD))
        sink = sink_ref[h] * LOG2E
        m = jnp.maximum(jnp.max(s, axis=0, keepdims=True), sink)
        p = jnp.exp2(s - m).astype(BF16)
        den = _dot(ones, p)[0:1] + jnp.exp2(sink - m)
        outs.append(_dot(v_t[h // group], p) / den)
    slabs = [jnp.concatenate(outs[i:i + 2], axis=0).T for i in range(0, SWA_HEADS, 2)]
    o_ref[0] = jnp.concatenate(slabs, axis=-1).astype(o_ref.dtype)


def swa_attention(u, cos, sin, gq, gk, sinks):
    b, s, _ = u.shape
    nb = s // SWA_BLOCK
    qw, kw = SWA_HEADS * HEAD_DIM, SWA_KV_HEADS * HEAD_DIM
    cur = lambda c: (lambda i, n: (i, n, c))
    prev = lambda c: (lambda i, n: (i, jnp.maximum(n - 1, 0), c))
    tab = pl.BlockSpec((1, SWA_BLOCK, LANES), cur(0))
    tab_prev = pl.BlockSpec((1, SWA_BLOCK, LANES), prev(0))
    gain = pl.BlockSpec((1, LANES), lambda i, n: (0, 0))
    seg = (jnp.arange(kw)[:, None] // HEAD_DIM == jnp.arange(kw)[None, :] // HEAD_DIM).astype(BF16)
    return pl.pallas_call(
        _swa_kernel,
        grid=(b, nb),
        in_specs=[pl.BlockSpec(memory_space=pltpu.SMEM),
                  pl.BlockSpec((1, SWA_BLOCK, qw), cur(0)),
                  pl.BlockSpec((1, SWA_BLOCK, kw), cur(qw // kw)),
                  pl.BlockSpec((1, SWA_BLOCK, kw), prev(qw // kw)),
                  pl.BlockSpec((1, SWA_BLOCK, kw), cur(qw // kw + 1)),
                  pl.BlockSpec((1, SWA_BLOCK, kw), prev(qw // kw + 1)),
                  tab, tab, tab_prev, tab_prev, gain, gain, pl.BlockSpec((kw, kw), lambda i, n: (0, 0))],
        out_specs=pl.BlockSpec((1, SWA_BLOCK, qw), cur(0)),
        out_shape=jax.ShapeDtypeStruct((b, s, qw), BF16),
        compiler_params=_params("parallel", "arbitrary"),
        name="swa_attention",
    )(sinks.astype(F32), u, u, u, u, u, cos, sin, cos, sin, gq, gk, seg)


def _mm(a, b, dims, passes):
    dn = (dims, ((), ()))
    dg = lambda x, y: lax.dot_general(x, y, dn, preferred_element_type=F32)
    ah = a.astype(BF16)
    bh = b.astype(BF16)
    if passes == 1:
        return dg(ah, bh)
    al = (a - ah.astype(F32)).astype(BF16)
    bl = (b - bh.astype(F32)).astype(BF16)
    return dg(ah, bh) + dg(ah, bl) + dg(al, bh)


_NN = ((1,), (0,))
_NT = ((1,), (1,))
_TN = ((0,), (0,))
P_SC, P_INV, P_PQ, P_OUT, P_ST = 1, 1, 1, 1, 1
RWKV_UNROLL = 8


RWKV_PREP_ROWS = 128


def _rwkv_kernel(ur_ref, uk_ref, uv_ref, ul_ref, mur_ref, muk_ref, muv_ref, mul_ref, w0_ref, w2_ref, a0_ref,
                 a2_ref, g2_ref, kkw_ref, ka_ref, gng_ref, gnb_ref, rk_ref,
                 o_ref, r_ref, k_ref, v_ref, kk_ref, b_ref, lw_ref, g_ref,
                 st_ref, y1_ref, y0_ref, n_ref, z_ref, dec_ref, *, nchunk):
    c = RWKV_CHUNK
    lane_c = lax.broadcasted_iota(jnp.int32, (c, LANES), 1)
    head0 = lane_c < HEAD_DIM
    ri = lax.broadcasted_iota(jnp.int32, (2 * c, 2 * c), 0)
    ci = lax.broadcasted_iota(jnp.int32, (2 * c, 2 * c), 1)
    eye = jnp.where(ri == ci, 1.0, 0.0)
    tril_c = jnp.where(lax.broadcasted_iota(jnp.int32, (c, c), 0) >= lax.broadcasted_iota(jnp.int32, (c, c), 1),
                       1.0, 0.0).astype(BF16)
    stack = lambda x: jnp.concatenate([jnp.where(head0, x, 0.0), jnp.where(head0, 0.0, x)], axis=0)

    def seg_mean(x):
        first = lax.broadcasted_iota(jnp.int32, x.shape, 1) < HEAD_DIM
        m0 = jnp.sum(jnp.where(first, x, 0.0), axis=-1, keepdims=True)
        m1 = jnp.sum(jnp.where(first, 0.0, x), axis=-1, keepdims=True)
        return jnp.where(first, m0, m1) * (1.0 / HEAD_DIM)

    def prep(io, first_group):
        group_rows = RWKV_UNROLL * c
        for tix in range(group_rows // RWKV_PREP_ROWS):
            start = pl.multiple_of(io * group_rows + tix * RWKV_PREP_ROWS, RWKV_PREP_ROWS)
            rows = pl.ds(start, RWKV_PREP_ROWS)
            at_start = first_group and tix == 0

            def shifted(ref, mu_ref):
                x = ref[0, rows, :]
                if at_start:
                    last = jnp.zeros((1, x.shape[1]), F32)
                else:
                    last = ref[0, pl.ds(pl.multiple_of(start - 8, 8), 8), :][7:8, :]
                row = lax.broadcasted_iota(jnp.int32, x.shape, 0)
                prev = jnp.where(row == 0, last, pltpu.roll(x, 1, 0))
                return x + (prev - x) * mu_ref[...]

            r = shifted(ur_ref, mur_ref)
            k = shifted(uk_ref, muk_ref)
            v = shifted(uv_ref, muv_ref)
            lo = shifted(ul_ref, mul_ref)
            yield
            w_lo, a_lo, g_lo = lo[:, 0:LANES], lo[:, LANES:2 * LANES], lo[:, 2 * LANES:4 * LANES]
            z = -(w0_ref[...] + _mm(jnp.tanh(w_lo), w2_ref[...], _NN, 3))
            w = -(jnp.maximum(z, 0.0) + jnp.log(1.0 + jnp.exp(-jnp.abs(z)))) - 0.5
            a = 1.0 / (1.0 + jnp.exp(-(a0_ref[...] + _mm(a_lo, a2_ref[...], _NN, 1))))
            g = _mm(1.0 / (1.0 + jnp.exp(-g_lo)), g2_ref[...], _NN, 1)
            yield
            kk = k * kkw_ref[...]
            kk = kk / jnp.maximum(jnp.sqrt(seg_mean(kk * kk) * float(HEAD_DIM)), 1e-12)
            r_ref[rows, :] = r
            k_ref[rows, :] = k * (1.0 + (a - 1.0) * ka_ref[...])
            v_ref[rows, :] = v
            kk_ref[rows, :] = kk
            b_ref[rows, :] = kk * a
            lw_ref[rows, :] = -jnp.exp(w)
            g_ref[rows, :] = g
            yield

    def build(ics):
        each = lambda f, *cols: [f(*args) for args in zip(*cols)]
        sls = [pl.ds(pl.multiple_of(ic * c, c), c) for ic in ics]
        load = lambda ref: [ref[sl, :] for sl in sls]
        r, k, v, kk, b, lw = (load(ref) for ref in (r_ref, k_ref, v_ref, kk_ref, b_ref, lw_ref))

        def running_sum(x):
            l1 = x.astype(BF16)
            rest = x - l1.astype(F32)
            l2 = rest.astype(BF16)
            l3 = (rest - l2.astype(F32)).astype(BF16)
            return _dot(tril_c, l1) + _dot(tril_c, l2) + _dot(tril_c, l3)

        cum = each(running_sum, lw)
        yield
        cum_end = [x[c - 1:c, :] for x in cum]
        e_neg = each(lambda x: jnp.exp(-x), cum)
        e_end = each(lambda x, xe: jnp.exp(xe - x), cum, cum_end)
        a_s = each(lambda kk_, x, l: stack(-kk_ * jnp.exp(x - l)), kk, cum, lw)
        r_s = each(lambda r_, x: stack(r_ * jnp.exp(x)), r, cum)
        b_s = each(lambda b_, e: stack(b_ * e), b, e_neg)
        k_s = each(lambda k_, e: stack(k_ * e), k, e_neg)
        bh_s = each(lambda b_, e: stack(b_ * e), b, e_end)
        kh_s = each(lambda k_, e: stack(k_ * e), k, e_end)
        v_s = each(stack, v)
        n2 = 2 * c
        sc = each(lambda a_, r_, b_, k_: _mm(jnp.concatenate([a_, r_], axis=0),
                                             jnp.concatenate([b_, k_], axis=0), _NT, P_SC), a_s, r_s, b_s, k_s)
        low = [jnp.where(ri > ci, x[:n2, :n2], 0.0) for x in sc]
        a_ak = [jnp.where(ri > ci, x[:n2, n2:], 0.0) for x in sc]
        a_rb = [jnp.where(ri >= ci, x[n2:, :n2], 0.0) for x in sc]
        a_rk = [jnp.where(ri >= ci, x[n2:, n2:], 0.0) for x in sc]
        yield
        inv = [eye + x for x in low]
        pw = each(lambda x: _mm(x, x, _NN, P_INV), low)
        yield
        levels = 5
        for lvl in range(levels - 1):
            both = each(lambda t, x: _mm(jnp.concatenate([t, x], axis=0), x, _NN, P_INV), inv, pw)
            inv = each(lambda t, r: t + r[:n2], inv, both)
            pw = [r[n2:] for r in both]
            yield
        inv = each(lambda t, x: t + _mm(t, x, _NN, P_INV), inv, pw)
        yield
        akv = each(lambda x, y: _mm(x, y, _NN, P_PQ), a_ak, v_s)
        yield
        pq = each(lambda t, x, y: _mm(t, jnp.concatenate([x, y], axis=1), _NN, P_PQ), inv, a_s, akv)
        yield
        yy = each(lambda x, y: _mm(x, y, _NN, P_OUT), a_rb, pq)
        y0b = each(lambda x, y: _mm(x, y, _NN, P_OUT), a_rk, v_s)
        nz = each(lambda x, y: _mm(x, y, _TN, P_OUT), pq, bh_s)
        zb = each(lambda x, y: _mm(x, y, _TN, P_OUT), v_s, kh_s)
        for i, ic in enumerate(ics):
            y1_ref[ic] = r_s[i] + yy[i][:, :LANES]
            y0_ref[ic] = yy[i][:, LANES:] + y0b[i]
            n_ref[ic] = nz[i][:LANES]
            z_ref[ic] = nz[i][LANES:] + zb[i]
            dec_ref[ic] = jnp.broadcast_to(jnp.exp(cum_end[i]), (8, LANES))

    def emit(ic, st):
        sl = pl.ds(pl.multiple_of(ic * c, c), c)
        r, k, v, g = r_ref[sl, :], k_ref[sl, :], v_ref[sl, :], g_ref[sl, :]
        y_st = _mm(y1_ref[ic], st, _NT, P_ST) + y0_ref[ic]
        y = y_st[0:c] + y_st[c:2 * c]
        mean = seg_mean(y)
        var = seg_mean((y - mean) * (y - mean))
        yn = (y - mean) * lax.rsqrt(var + RWKV_GN_EPS) * gng_ref[...] + gnb_ref[...]
        bonus = seg_mean(r * k * rk_ref[...]) * float(HEAD_DIM) * v
        o_ref[0, sl, :] = ((yn + bonus) * g).astype(o_ref.dtype)
        return st * dec_ref[ic][0:1, :] + _mm(st, n_ref[ic], _NN, P_ST) + z_ref[ic]

    def scan(ics):
        st = st_ref[...]
        for ic in ics:
            st = emit(ic, st)
            yield
        st_ref[...] = st

    def run(*gens):
        live = list(gens)
        while live:
            live = [gen for gen in live if next(gen, live) is not live]

    group = lambda io: [io * RWKV_UNROLL + i for i in range(RWKV_UNROLL)]
    ngroup = nchunk // RWKV_UNROLL
    st_ref[...] = jnp.zeros_like(st_ref)
    run(prep(0, True))
    if ngroup > 1:
        run(prep(1, False), build(group(0)))
    else:
        run(build(group(0)))

    @pl.loop(1, ngroup - 1)
    def _(io):
        run(prep(io + 1, False), build(group(io)), scan(group(io - 1)))

    if ngroup > 1:
        run(build(group(ngroup - 1)), scan(group(ngroup - 2)))
    run(scan(group(ngroup - 1)))


def rwkv_mix(u, mu, w0, w2, a0, a2, g2, k_k, k_a, r_k, gn_g, gn_b):
    bsz, s, _ = u.shape
    npair = RWKV_DIM // LANES
    nchunk = s // RWKV_CHUNK
    assert nchunk % RWKV_UNROLL == 0 and (RWKV_UNROLL * RWKV_CHUNK) % RWKV_PREP_ROWS == 0
    base = (SWA_HEADS + 2 * SWA_KV_HEADS) * HEAD_DIM // LANES
    lora_w = 4 * LANES
    slab = lambda off: pl.BlockSpec((1, s, LANES), lambda i, p: (i, 0, off + p))
    vec = pl.BlockSpec((1, LANES), lambda i, p: (0, p))
    cols = lambda rows: pl.BlockSpec((rows, LANES), lambda i, p: (0, p))
    row = lambda vv: vv.reshape(1, -1).astype(F32)
    pad_rows = lambda m, rows: jnp.pad(m, ((0, rows - m.shape[0]), (0, 0))).astype(F32)
    pad_cols = lambda vv, n: jnp.pad(vv, (0, n - vv.shape[0]))
    c3 = 3 * RWKV_DIM
    mu_l = jnp.concatenate([pad_cols(mu[c3:c3 + DECAY_LORA], LANES),
                            pad_cols(mu[c3 + DECAY_LORA:c3 + DECAY_LORA + AAA_LORA], LANES),
                            pad_cols(mu[c3 + DECAY_LORA + AAA_LORA:], 2 * LANES)])
    seq = pltpu.VMEM((s, LANES), F32)
    mat = pltpu.VMEM((nchunk, LANES, LANES), F32)
    return pl.pallas_call(
        functools.partial(_rwkv_kernel, nchunk=nchunk),
        grid=(bsz, npair),
        in_specs=[slab(base), slab(base + npair), slab(base + 2 * npair),
                  pl.BlockSpec((1, s, lora_w), lambda i, p: (i, 0, (base + 3 * npair) * LANES // lora_w)),
                  vec, vec, vec, pl.BlockSpec((1, lora_w), lambda i, p: (0, 0)),
                  vec, cols(LANES), vec, cols(LANES), cols(2 * LANES), vec, vec, vec, vec, vec],
        out_specs=pl.BlockSpec((1, s, LANES), lambda i, p: (i, 0, p)),
        out_shape=jax.ShapeDtypeStruct((bsz, s, RWKV_DIM), BF16),
        scratch_shapes=[seq] * 7 + [pltpu.VMEM((LANES, LANES), F32), mat, mat, mat, mat,
                                    pltpu.VMEM((nchunk, 8, LANES), F32)],
        compiler_params=_params("parallel", "arbitrary"),
        name="rwkv_mix",
    )(u, u, u, u, row(mu[:RWKV_DIM]), row(mu[RWKV_DIM:2 * RWKV_DIM]), row(mu[2 * RWKV_DIM:c3]), row(mu_l),
      row(w0), pad_rows(w2, LANES), row(a0), pad_rows(a2, LANES), pad_rows(g2, 2 * LANES),
      row(k_k), row(k_a), row(gn_g), row(gn_b), row(r_k))


def _mla_prep_kernel(cq_ref, ckv_ref, pe_ref, cos_ref, sin_ref, e_ref, gq_ref, gkn_ref, gkp_ref, invn_ref,
                     gcq_ref, gckv_ref, wuq_ref, wukv_ref, qo_ref, ko_ref, vo_ref, cqn_ref, ckvn_ref, kpe_ref):
    cos, sin = cos_ref[0], sin_ref[0]
    half = MLA_ROPE // 2

    def rope(x, g):
        w = x.shape[1]
        xg = x * _tile_lanes(g, w)
        return xg * _tile_lanes(cos, w) + _swap_halves(xg, half) * _tile_lanes(sin, w)

    @pl.when(pl.program_id(2) == 0)
    def _():
        cqn_ref[...] = _rms(cq_ref[0], gcq_ref[...]).astype(BF16)
        ckvn_ref[...] = _rms(ckv_ref[0], gckv_ref[...]).astype(BF16)
        pe = pe_ref[0]
        inv_pe = lax.rsqrt(jnp.sum(pe * pe, axis=-1, keepdims=True) * (1.0 / MLA_ROPE) + EPS)
        kpe_ref[...] = rope(pe, gkp_ref[...]) * inv_pe

    x = _dot(cqn_ref[...], wuq_ref[...])
    w = x.shape[1]
    inv_n = _tile_lanes(invn_ref[...], w)
    inv = lax.rsqrt(_seg_sum(x * x, e_ref[...]) * inv_n + EPS)
    qo_ref[0] = (rope(x, gq_ref[...]) * inv * ((MLA_NOPE + MLA_ROPE) ** -0.5 * LOG2E)).astype(qo_ref.dtype)
    kv = _dot(ckvn_ref[...], wukv_ref[...])
    inv_k = lax.rsqrt(_seg_sum(kv * kv, e_ref[...]) * inv_n + EPS)
    k_nope = kv * inv_k * _tile_lanes(gkn_ref[...], w)
    ko_ref[0] = (k_nope + _tile_lanes(kpe_ref[...], w)).astype(ko_ref.dtype)
    v = jnp.concatenate([kv[:, h * LANES + MLA_NOPE:(h + 1) * LANES] for h in range(w // LANES)], axis=-1)
    _store_transposed_tiles(vo_ref, v)


def mla_prep(u, cos, sin, gq, gkn, gkp, gcq, gckv, w_uq, w_ukv, *, ts=1024, tc=512):
    b, s, _ = u.shape
    wtot = w_uq.shape[1]
    ts = min(ts, s)
    lane = jnp.arange(SEG_W)
    same = (lane[:, None] // LANES == lane[None, :] // LANES)
    pos = lane % LANES
    nope = pos < MLA_NOPE
    pe = (pos >= MLA_NOPE) & (pos < MLA_NOPE + MLA_ROPE)
    seg = (same & ((nope[:, None] & nope[None, :]) | (pe[:, None] & pe[None, :]))).astype(BF16)
    p1 = jnp.arange(LANES)
    inv_n = jnp.where(p1 < MLA_NOPE, 1.0 / MLA_NOPE, jnp.where(p1 < MLA_NOPE + MLA_ROPE, 1.0 / MLA_ROPE, 0.0))
    blk = pl.BlockSpec((1, ts, tc), lambda i, t, c: (i, t, c))
    tab = pl.BlockSpec((1, ts, LANES), lambda i, t, c: (i, t, 0))
    vec = pl.BlockSpec((1, LANES), lambda i, t, c: (0, 0))
    pe_blk = (MLA_Q_RANK + MLA_KV_RANK) // LANES
    out = jax.ShapeDtypeStruct((b, s, wtot), BF16)
    return pl.pallas_call(
        _mla_prep_kernel,
        grid=(b, s // ts, wtot // tc),
        in_specs=[pl.BlockSpec((1, ts, MLA_Q_RANK), lambda i, t, c: (i, t, 0)),
                  pl.BlockSpec((1, ts, MLA_KV_RANK), lambda i, t, c: (i, t, MLA_Q_RANK // MLA_KV_RANK)),
                  pl.BlockSpec((1, ts, LANES), lambda i, t, c: (i, t, pe_blk)), tab, tab,
                  pl.BlockSpec((SEG_W, SEG_W), lambda i, t, c: (0, 0)), vec, vec, vec, vec,
                  pl.BlockSpec((1, MLA_Q_RANK), lambda i, t, c: (0, 0)),
                  pl.BlockSpec((1, MLA_KV_RANK), lambda i, t, c: (0, 0)),
                  pl.BlockSpec((MLA_Q_RANK, tc), lambda i, t, c: (0, c)),
                  pl.BlockSpec((MLA_KV_RANK, tc), lambda i, t, c: (0, c))],
        out_specs=[blk, blk, pl.BlockSpec((1, ts // ATTN_TILE, tc // LANES * MLA_V, ATTN_TILE),
                                          lambda i, t, c: (i, t, c, 0))],
        out_shape=[out, out, jax.ShapeDtypeStruct((b, s // ATTN_TILE, wtot // LANES * MLA_V, ATTN_TILE), BF16)],
        scratch_shapes=[pltpu.VMEM((ts, MLA_Q_RANK), BF16), pltpu.VMEM((ts, MLA_KV_RANK), BF16),
                        pltpu.VMEM((ts, LANES), F32)],
        compiler_params=_params("parallel", "parallel", "arbitrary"),
        name="mla_prep",
    )(u, u, u, cos, sin, seg, gq, gkn, gkp, inv_n.reshape(1, LANES).astype(F32),
      gcq.reshape(1, -1).astype(F32), gckv.reshape(1, -1).astype(F32), w_uq, w_ukv)


def _diff_prep_kernel(q_ref, k_ref, v_ref, cos_ref, sin_ref, e_ref, gq_ref, gk_ref, qo_ref, ko_ref, vo_ref):
    cos, sin = cos_ref[0], sin_ref[0]
    _store_transposed_tiles(vo_ref, v_ref[0])

    def prep(x, g, scale):
        w = x.shape[1]
        xg = x * _tile_lanes(g, w)
        xr = xg * _tile_lanes(cos, w) + _swap_halves(xg, DIFF_QK // 2) * _tile_lanes(sin, w)
        inv = lax.rsqrt(_seg_sum(x * x, e_ref[...]) * (1.0 / DIFF_QK) + EPS)
        return xr * inv * scale

    qo_ref[0] = prep(q_ref[0], gq_ref[...], DIFF_QK ** -0.5 * LOG2E).astype(qo_ref.dtype)
    ko_ref[0] = prep(k_ref[0], gk_ref[...], 1.0).astype(ko_ref.dtype)


def diff_prep(u, cos, sin, gq, gk, *, ts=1024, tc=512):
    b, s, _ = u.shape
    ts = min(ts, s)
    wtot = 2 * DIFF_HEADS * DIFF_QK
    q_base = (CD_IN_PAD - 3 * wtot) // tc
    seg = (jnp.arange(SEG_W)[:, None] // DIFF_QK == jnp.arange(SEG_W)[None, :] // DIFF_QK).astype(BF16)
    blk = lambda off: pl.BlockSpec((1, ts, tc), lambda i, t, c: (i, t, off + c))
    tab = pl.BlockSpec((1, ts, LANES), lambda i, t, c: (i, t, 0))
    vec = pl.BlockSpec((1, LANES), lambda i, t, c: (0, 0))
    out = jax.ShapeDtypeStruct((b, s, wtot), BF16)
    return pl.pallas_call(
        _diff_prep_kernel,
        grid=(b, s // ts, wtot // tc),
        in_specs=[blk(q_base), blk(q_base + wtot // tc), blk(q_base + 2 * wtot // tc), tab, tab,
                  pl.BlockSpec((SEG_W, SEG_W), lambda i, t, c: (0, 0)), vec, vec],
        out_specs=[blk(0), blk(0),
                   pl.BlockSpec((1, ts // ATTN_TILE, tc, ATTN_TILE), lambda i, t, c: (i, t, c, 0))],
        out_shape=[out, out, jax.ShapeDtypeStruct((b, s // ATTN_TILE, wtot, ATTN_TILE), BF16)],
        compiler_params=_params("parallel", "parallel", "arbitrary"),
        name="diff_prep",
    )(u, u, u, cos, sin, seg, gq, gk)


def _causal_attn_kernel(lam_ref, q_ref, k_ref, vt_ref, g_ref, o_ref, *, n_sm, tq, ow, out_scale):
    qi = pl.program_id(2)
    q = q_ref[0]
    hps = q.shape[1] // LANES
    slab = lambda x, h: x[:, h * LANES:(h + 1) * LANES]
    lane = lax.broadcasted_iota(jnp.int32, (tq, LANES), 1)
    qs, src = [], []
    for h in range(hps):
        qh = slab(q, h)
        if n_sm == 2:
            zero = jnp.zeros_like(qh)
            qs += [jnp.where(lane < DIFF_QK, qh, zero), jnp.where(lane < DIFF_QK, zero, qh)]
            src += [h, h]
        else:
            qs.append(qh)
            src.append(h)
    nch = len(qs)
    key_i = lax.broadcasted_iota(jnp.int32, (tq, tq), 0)
    qry_i = lax.broadcasted_iota(jnp.int32, (tq, tq), 1)
    ones = jnp.ones((8, tq), BF16)

    def step(j, carry, diagonal):
        kj = k_ref[0, pl.ds(pl.multiple_of(j * tq, tq), tq), :]
        scores = lambda i: _dot_t(slab(kj, src[i]), qs[i])
        new = []
        ahead = [scores(i) for i in range(min(ATTN_LOOKAHEAD, nch))]
        for i in range(nch):
            s = ahead.pop(0)
            if i + ATTN_LOOKAHEAD < nch:
                ahead.append(scores(i + ATTN_LOOKAHEAD))
            if diagonal:
                s = jnp.where(key_i <= qry_i, s, NEG_INF)
            m, l, acc = carry[3 * i:3 * i + 3]
            m_new = jnp.maximum(m, jnp.max(s, axis=0, keepdims=True))
            alpha = jnp.exp2(m - m_new)
            p = jnp.exp2(s - m_new).astype(BF16)
            new += [m_new, alpha * l + _dot(ones, p)[0:1],
                    alpha * acc + _dot(vt_ref[0, j, src[i] * ow:(src[i] + 1) * ow, :], p)]
        return tuple(new)

    init = (jnp.full((1, tq), NEG_INF, F32), jnp.zeros((1, tq), F32),
            jnp.zeros((ow, tq), F32)) * nch
    carry = lax.fori_loop(0, qi, lambda j, cr: step(j, cr, False), init)
    carry = step(qi, carry, True)
    outs = []
    for h in range(hps):
        c0 = 3 * n_sm * h
        o = carry[c0 + 2] / carry[c0 + 1]
        if n_sm == 2:
            o = o - lam_ref[0] * (carry[c0 + 5] / carry[c0 + 4])
            o = o * lax.rsqrt(jnp.mean(o * o, axis=0, keepdims=True) + EPS) * g_ref[...] * out_scale
        outs.append(o.T.astype(o_ref.dtype))
    o_ref[0] = jnp.concatenate(outs, axis=-1)


def causal_attention(q, k, vt, *, n_sm, lam=None, g=None, out_scale=1.0, hps=4, name="causal_attention"):
    b, s, wtot = q.shape
    tq = vt.shape[3]
    width = hps * LANES
    groups = wtot // width
    ow = vt.shape[2] // (wtot // LANES)
    lam = jnp.zeros((1,), F32) if lam is None else lam.reshape(1).astype(F32)
    g = jnp.ones((ow, 1), F32) if g is None else g.reshape(ow, 1).astype(F32)
    seq = pl.BlockSpec((1, s, width), lambda i, h, t: (i, 0, h))
    tile = pl.BlockSpec((1, tq, width), lambda i, h, t: (i, t, h))
    return pl.pallas_call(
        functools.partial(_causal_attn_kernel, n_sm=n_sm, tq=tq, ow=ow, out_scale=out_scale),
        grid=(b, groups, s // tq),
        in_specs=[pl.BlockSpec(memory_space=pltpu.SMEM), tile, seq,
                  pl.BlockSpec((1, s // tq, hps * ow, tq), lambda i, h, t: (i, 0, h, 0)),
                  pl.BlockSpec((ow, 1), lambda i, h, t: (0, 0))],
        out_specs=pl.BlockSpec((1, tq, hps * ow), lambda i, h, t: (i, t, h)),
        out_shape=jax.ShapeDtypeStruct((b, s, groups * hps * ow), BF16),
        compiler_params=_params("parallel", "parallel", "arbitrary"),
        name=name,
    )(lam, q, k, vt, g)


def _memx_kernel(x_ref, ya_ref, yb_ref, wa_ref, wb_ref, g_ref, wq_ref, kv_ref, gq_ref, gk_ref, wo_ref, o_ref):
    x = x_ref[0] + _dot(ya_ref[0], wa_ref[...]) + _dot(yb_ref[0], wb_ref[...])
    q = _dot(_rms(x, g_ref[...]).astype(BF16), wq_ref[...])
    kv = kv_ref[0]
    outs = []
    for h in range(MEM_HEADS):
        sl = slice(h * MEM_HEAD_DIM, (h + 1) * MEM_HEAD_DIM)
        qh = (_rms(q[:, sl], gq_ref[...]) * MEM_HEAD_DIM ** -0.5).astype(BF16)
        kh = _rms(kv[:, sl], gk_ref[...]).astype(BF16)
        vh = kv[:, MEM_W + h * MEM_HEAD_DIM:MEM_W + (h + 1) * MEM_HEAD_DIM].astype(BF16)
        s = _dot_t(qh, kh)
        p = jnp.exp(s - jnp.max(s, axis=-1, keepdims=True))
        outs.append(_dot(p.astype(BF16), vh) / jnp.sum(p, axis=-1, keepdims=True))
    o_ref[0] = x + _dot(jnp.concatenate(outs, axis=-1).astype(BF16), wo_ref[...])


def mix_out_mem_attention(x, ya, yb, wa, wb, mem_kv, g, wq, gq, gk, wo, *, tm=512):
    b, s, d = x.shape
    m = mem_kv.shape[1]
    tm = min(tm, s)
    const = lambda shape: pl.BlockSpec(shape, lambda i, t: (0,) * len(shape), pipeline_mode=pl.Buffered(1))
    tile = lambda w: pl.BlockSpec((1, tm, w), lambda i, t: (i, t, 0))
    return pl.pallas_call(
        _memx_kernel,
        grid=(b, s // tm),
        in_specs=[tile(d), tile(ya.shape[2]), tile(yb.shape[2]), const(wa.shape), const(wb.shape),
                  const((1, d)), const((d, MEM_W)),
                  pl.BlockSpec((1, m, 2 * MEM_W), lambda i, t: (i, 0, 0)),
                  const((1, MEM_HEAD_DIM)), const((1, MEM_HEAD_DIM)), const((MEM_W, d))],
        out_specs=tile(d),
        out_shape=jax.ShapeDtypeStruct((b, s, d), F32),
        compiler_params=_params("parallel", "arbitrary"),
        name="mix_out_mem_attention",
    )(x, ya, yb, wa, wb, g.reshape(1, d).astype(F32), wq, mem_kv, gq.reshape(1, -1).astype(F32),
      gk.reshape(1, -1).astype(F32), wo)


def _rope_tables(positions, dim, lead_ones, tail):
    inv = 1.0 / (ROPE_THETA ** (jnp.arange(0, dim, 2, dtype=F32) / dim))
    ang = positions.astype(F32)[..., None] * inv
    c, s = jnp.cos(ang), jnp.sin(ang)
    shape = positions.shape
    cos = jnp.concatenate([jnp.ones(shape + (lead_ones,), F32), c, c, jnp.ones(shape + (tail,), F32)], axis=-1)
    sin = jnp.concatenate([jnp.zeros(shape + (lead_ones,), F32), -s, s, jnp.zeros(shape + (tail,), F32)], axis=-1)
    reps = LANES // cos.shape[-1]
    return jnp.tile(cos, (1, 1, reps)), jnp.tile(sin, (1, 1, reps))


IN_TILE = 512


def _pad_rows(w, rows):
    return jnp.pad(w, ((0, rows - w.shape[0]), (0, 0)))


def _ab_in_segments(w):
    c = (SWA_HEADS + 2 * SWA_KV_HEADS) * HEAD_DIM + 3 * RWKV_DIM
    wt = w.T.astype(BF16)
    lora = jnp.concatenate([_pad_rows(wt[c:c + DECAY_LORA], LANES),
                            _pad_rows(wt[c + DECAY_LORA:c + DECAY_LORA + AAA_LORA], LANES),
                            _pad_rows(wt[c + DECAY_LORA + AAA_LORA:], 2 * LANES)], axis=0)
    return [wt[:c], lora]


def _cd_in_segments(w):
    c1 = MLA_Q_RANK + MLA_KV_RANK
    wt = w.T.astype(BF16)
    z = lambda n: jnp.zeros((n, wt.shape[1]), wt.dtype)
    mid = jnp.concatenate([wt[MLA_Q_RANK:c1], z(MLA_NOPE), wt[c1:c1 + MLA_ROPE],
                           z(LANES - MLA_NOPE - MLA_ROPE), z(LANES)], axis=0)
    return [wt[:MLA_Q_RANK], mid, wt[c1 + MLA_ROPE:]]


def _head_slabs(w, per_head):
    k = w.shape[0]
    return jnp.pad(w.reshape(k, -1, per_head), ((0, 0), (0, 0), (0, LANES - per_head))).reshape(k, -1)


def _slab_vec(*parts):
    v = jnp.concatenate([p.astype(F32) for p in parts])
    return jnp.pad(v, (0, LANES - v.shape[0])).reshape(1, LANES)


def kernel(x, mem, positions, ffn1_norm, ffn1_w_gate, ffn1_w_up, ffn1_w_down, mix_norm, ab_w_in, ab_w_out, swa_q_norm, swa_k_norm, swa_sinks, rwkv_mu, rwkv_w0, rwkv_w2, rwkv_a0, rwkv_a2, rwkv_g2, rwkv_k_k, rwkv_k_a, rwkv_r_k, rwkv_gn_g, rwkv_gn_b, cd_w_in, cd_w_out, mla_cq_norm, mla_ckv_norm, mla_w_uq, mla_w_ukv, mla_q_nope_norm, mla_k_nope_norm, mla_q_rope_norm, mla_k_rope_norm, diff_q_norm, diff_k_norm, diff_lq1, diff_lk1, diff_lq2, diff_lk2, diff_subln, memx_norm, memx_w_q, memx_q_norm, memx_w_o, mem_norm, mem_w_kv, mem_k_norm, ffn2_norm, ffn2_w_gate, ffn2_w_up, ffn2_w_down):
    b, s, d = x.shape
    m = mem.shape[1]
    t = b * s
    depth = ffn1_norm.shape[0]
    bf = lambda w: w.astype(BF16)
    cos64, sin64 = _rope_tables(positions, HEAD_DIM, 0, 0)
    cos32, sin32 = _rope_tables(positions, MLA_ROPE, MLA_NOPE, LANES - MLA_NOPE - MLA_ROPE)

    mem_kv = norm_matmul(mem.reshape(b * m, d), mem_norm, [bf(mem_w_kv).T],
                         tn=IN_TILE, name="mem_kv").reshape(b, m, 2 * MEM_W)

    x = x.reshape(t, d)
    for layer in range(depth):
        j = layer // 2
        x = ffn(x, ffn1_norm[layer], ffn1_w_gate, ffn1_w_up, ffn1_w_down, layer, name="ffn1")
        if layer % 2 == 0:
            u = norm_matmul(x, mix_norm[layer], _ab_in_segments(ab_w_in[j]), tn=IN_TILE, name="ab_in")
            u = u.reshape(b, s, AB_IN_PAD)
            y_a = swa_attention(u, cos64, sin64, _slab_vec(swa_q_norm[j], swa_q_norm[j]),
                                _slab_vec(swa_k_norm[j], swa_k_norm[j]), swa_sinks[j])
            y_b = rwkv_mix(u, rwkv_mu[j], rwkv_w0[j], rwkv_w2[j], rwkv_a0[j], rwkv_a2[j], rwkv_g2[j],
                           rwkv_k_k[j], rwkv_k_a[j], rwkv_r_k[j], rwkv_gn_g[j], rwkv_gn_b[j])
            mixed, w_out, split = (y_a, y_b), bf(ab_w_out[j]), SWA_HEADS * HEAD_DIM
        else:
            u = norm_matmul(x, mix_norm[layer], _cd_in_segments(cd_w_in[j]), tn=IN_TILE, name="cd_in")
            u = u.reshape(b, s, CD_IN_PAD)
            zero64 = jnp.zeros((MLA_NOPE,), F32)
            q_c, k_c, vt_c = mla_prep(u, cos32, sin32,
                                      _slab_vec(mla_q_nope_norm[j], mla_q_rope_norm[j]),
                                      _slab_vec(mla_k_nope_norm[j]),
                                      _slab_vec(zero64, mla_k_rope_norm[j]),
                                      mla_cq_norm[j], mla_ckv_norm[j],
                                      bf(_head_slabs(mla_w_uq[j], MLA_NOPE + MLA_ROPE)), bf(mla_w_ukv[j]))
            y_c = causal_attention(q_c, k_c, vt_c, n_sm=1, hps=16, name="mla_attention")
            q_d, k_d, vt_d = diff_prep(u, cos64, sin64, _slab_vec(diff_q_norm[j], diff_q_norm[j]),
                                       _slab_vec(diff_k_norm[j], diff_k_norm[j]))
            lambda_init = 0.8 - 0.6 * math.exp(-0.3 * layer)
            lam = (jnp.exp(jnp.sum(diff_lq1[j].astype(F32) * diff_lk1[j].astype(F32)))
                   - jnp.exp(jnp.sum(diff_lq2[j].astype(F32) * diff_lk2[j].astype(F32))) + lambda_init)
            y_d = causal_attention(q_d, k_d, vt_d, n_sm=2, hps=8, lam=lam, g=diff_subln[j],
                                   out_scale=1.0 - lambda_init, name="diff_attention")
            mixed, w_out, split = (y_c, y_d), bf(cd_w_out[j]), MLA_HEADS * MLA_V
        x = mix_out_mem_attention(x.reshape(b, s, d), mixed[0], mixed[1], w_out[:split], w_out[split:], mem_kv,
                                  memx_norm[layer], bf(memx_w_q[layer]), memx_q_norm[layer], mem_k_norm,
                                  bf(memx_w_o[layer])).reshape(t, d)
        x = ffn(x, ffn2_norm[layer], ffn2_w_gate, ffn2_w_up, ffn2_w_down, layer, name="ffn2")
    return x.reshape(b, s, d)
```

```python
import functools
import math

import jax
import jax.numpy as jnp
from jax import lax
from jax.experimental import pallas as pl
from jax.experimental.pallas import tpu as pltpu

F32 = jnp.float32
BF16 = jnp.bfloat16

EPS = 1e-6
ROPE_THETA = 10000.0
NEG_INF = -1e30
LOG2E = math.log2(math.e)
ATTN_LOOKAHEAD = 8
SWA_LOOKAHEAD = 4
ATTN_TILE = 256
SEG_W = 256
LANES = 128

HEAD_DIM = 64
SWA_HEADS = 16
SWA_KV_HEADS = 4
SWA_BLOCK = 128
RWKV_DIM = 1024
RWKV_CHUNK = 64
RWKV_GN_EPS = 64e-5
DECAY_LORA, AAA_LORA, GATE_LORA = 64, 64, 160
MLA_HEADS, MLA_Q_RANK, MLA_KV_RANK, MLA_NOPE, MLA_ROPE, MLA_V = 16, 512, 256, 64, 32, 64
DIFF_HEADS, DIFF_QK, DIFF_V = 8, 64, 128
MEM_HEADS, MEM_HEAD_DIM = 4, 128
MEM_W = MEM_HEADS * MEM_HEAD_DIM
AB_IN_PAD = 5120
CD_IN_PAD = 4096

VMEM_LIMIT = 48 * 1024 * 1024
FFN_VMEM_LIMIT = 60 * 1024 * 1024


def _params(*sem):
    return pltpu.CompilerParams(dimension_semantics=sem, vmem_limit_bytes=VMEM_LIMIT)


def _dot(a, b):
    return jnp.dot(a, b, preferred_element_type=F32)


def _dot_t(a, b):
    return lax.dot_general(a, b, (((1,), (1,)), ((), ())), preferred_element_type=F32)


def _rms(x, g):
    return x * lax.rsqrt(jnp.mean(x * x, axis=-1, keepdims=True) + EPS) * g


def _seg_sum(x, e):
    xb = x.astype(BF16)
    w = e.shape[0]
    parts = [_dot(xb[:, i:i + w], e) for i in range(0, x.shape[1], w)]
    return parts[0] if len(parts) == 1 else jnp.concatenate(parts, axis=-1)


def _tile_lanes(v, width):
    return v if v.shape[-1] == width else jnp.tile(v, (1, width // v.shape[-1]))


def _store_transposed_tiles(vt_ref, v):
    for r in range(vt_ref.shape[1]):
        vt_ref[0, r] = v[r * ATTN_TILE:(r + 1) * ATTN_TILE, :].T.astype(vt_ref.dtype)


def _swap_halves(x, half):
    w = x.shape[-1]
    lane = lax.broadcasted_iota(jnp.int32, x.shape, x.ndim - 1)
    low = (lane & (2 * half - 1)) < half
    return jnp.where(low, pltpu.roll(x, w - half, x.ndim - 1), pltpu.roll(x, half, x.ndim - 1))


def _norm_matmul_kernel(x_ref, g_ref, *refs, starts, counts):
    w_refs, o_ref, xn_ref = refs[:len(starts)], refs[len(starts)], refs[len(starts) + 1]
    j = pl.program_id(1)

    @pl.when(j == 0)
    def _():
        xn_ref[...] = _rms(x_ref[...], g_ref[...]).astype(BF16)

    for w_ref, start, count in zip(w_refs, starts, counts):
        @pl.when((j >= start) & (j < start + count))
        def _(w_ref=w_ref):
            o_ref[...] = _dot_t(xn_ref[...], w_ref[...])


def norm_matmul(x, g, segments, *, tm=1024, tn=512, name="norm_matmul"):
    t, k = x.shape
    tm = min(tm, t)
    assert t % tm == 0 and all(w.shape[0] % tn == 0 and w.shape[1] == k for w in segments)
    counts = [w.shape[0] // tn for w in segments]
    starts = [sum(counts[:i]) for i in range(len(counts))]
    w_spec = lambda start, count: pl.BlockSpec((tn, k), lambda i, j: (jnp.clip(j - start, 0, count - 1), 0))
    return pl.pallas_call(
        functools.partial(_norm_matmul_kernel, starts=tuple(starts), counts=tuple(counts)),
        grid=(t // tm, sum(counts)),
        in_specs=[pl.BlockSpec((tm, k), lambda i, j: (i, 0)),
                  pl.BlockSpec((1, k), lambda i, j: (0, 0))]
                 + [w_spec(start, count) for start, count in zip(starts, counts)],
        out_specs=pl.BlockSpec((tm, tn), lambda i, j: (i, j)),
        out_shape=jax.ShapeDtypeStruct((t, sum(counts) * tn), F32),
        scratch_shapes=[pltpu.VMEM((tm, k), BF16)],
        compiler_params=_params("parallel", "arbitrary"),
        name=name,
    )(x, g.reshape(1, k).astype(F32), *segments)


def _ffn_kernel(x_ref, g_ref, wg_ref, wu_ref, wd_ref, o_ref, xn_ref):
    @pl.when(pl.program_id(1) == 0)
    def _():
        x = x_ref[...]
        xn_ref[...] = _rms(x, g_ref[...]).astype(BF16)
        o_ref[...] = x

    xn = xn_ref[...]
    a = _dot(xn, wg_ref[...].astype(BF16))
    b = _dot(xn, wu_ref[...].astype(BF16))
    h = (a * (0.5 / (1.0 + jnp.exp(-a))) * b).astype(BF16)
    o_ref[...] += _dot(h, wd_ref[...].astype(BF16))


def ffn(x, g, wg, wu, wd, layer, *, tm=1024, tf=256, name="ffn"):
    t, d = x.shape
    ff = wg.shape[2]
    tm = min(tm, t)
    assert t % tm == 0 and ff % tf == 0
    nf = ff // tf
    return pl.pallas_call(
        _ffn_kernel,
        grid=(t // tm, nf),
        in_specs=[pl.BlockSpec((tm, d), lambda i, f: (i, 0)),
                  pl.BlockSpec((1, d), lambda i, f: (0, 0)),
                  pl.BlockSpec((None, d, tf), lambda i, f: (layer, 0, f)),
                  pl.BlockSpec((None, d, tf), lambda i, f: (layer, 0, f)),
                  pl.BlockSpec((None, tf, d), lambda i, f: (layer, f, 0))],
        out_specs=pl.BlockSpec((tm, d), lambda i, f: (i, 0)),
        out_shape=jax.ShapeDtypeStruct((t, d), F32),
        scratch_shapes=[pltpu.VMEM((tm, d), BF16)],
        compiler_params=pltpu.CompilerParams(dimension_semantics=("parallel", "arbitrary"),
                                             vmem_limit_bytes=FFN_VMEM_LIMIT),
        name=name,
    )(x, g.reshape(1, d).astype(F32), wg, wu, wd)


def _swa_kernel(sink_ref, q_ref, kc_ref, kp_ref, vc_ref, vp_ref, cc_ref, sc_ref, cp_ref, sp_ref,
                gq_ref, gk_ref, e_ref, o_ref):
    n = pl.program_id(1)
    blk = SWA_BLOCK
    group = SWA_HEADS // SWA_KV_HEADS
    q = q_ref[0]
    k = jnp.concatenate([kp_ref[0], kc_ref[0]], axis=0)
    v = jnp.concatenate([vp_ref[0], vc_ref[0]], axis=0)
    cos_q, sin_q = cc_ref[0], sc_ref[0]
    cos_k = jnp.concatenate([cp_ref[0], cos_q], axis=0)
    sin_k = jnp.concatenate([sp_ref[0], sin_q], axis=0)
    def prep(x, g, cos, sin, scale):
        w = x.shape[1]
        xg = x * _tile_lanes(g, w)
        xr = xg * _tile_lanes(cos, w) + _swap_halves(xg, HEAD_DIM // 2) * _tile_lanes(sin, w)
        return xr * lax.rsqrt(_seg_sum(x * x, e_ref[...]) * (1.0 / HEAD_DIM) + EPS) * scale

    qr = prep(q, gq_ref[...], cos_q, sin_q, HEAD_DIM ** -0.5 * LOG2E).astype(BF16)
    kr = prep(k, gk_ref[...], cos_k, sin_k, 1.0)
    key_i = lax.broadcasted_iota(jnp.int32, (2 * blk, blk), 0)
    qry_i = lax.broadcasted_iota(jnp.int32, (2 * blk, blk), 1)
    rel = qry_i + blk - key_i
    valid = (rel >= 0) & (rel < blk) & ((n > 0) | (key_i >= blk))
    low = lax.broadcasted_iota(jnp.int32, (blk, LANES), 1) < HEAD_DIM
    zero = jnp.zeros((blk, LANES), BF16)
    k_dup, v_t = [], []
    for g in range(SWA_KV_HEADS):
        kg = kr[:, g * HEAD_DIM:(g + 1) * HEAD_DIM]
        k_dup.append(jnp.concatenate([kg, kg], axis=-1).astype(BF16))
        v_t.append(v[:, g * HEAD_DIM:(g + 1) * HEAD_DIM].T.astype(BF16))

    def scores(h):
        slab = qr[:, (h // 2) * LANES:(h // 2 + 1) * LANES]
        qh = jnp.where(low, slab, zero) if h % 2 == 0 else jnp.where(low, zero, slab)
        return _dot_t(k_dup[h // group], qh)

    ahead = [scores(h) for h in range(SWA_LOOKAHEAD)]
    ones = jnp.ones((8, 2 * blk), BF16)
    outs = []
    for h in range(SWA_HEADS):
        s = jnp.where(valid, ahead.pop(0), NEG_INF)
        if h + SWA_LOOKAHEAD < SWA_HEADS:
            ahead.append(scores(h + SWA_LOOKAHEAD))
        sink = sink_ref[h] * LOG2E
        m = jnp.maximum(jnp.max(s, axis=0, keepdims=True), sink)
        p = jnp.exp2(s - m).astype(BF16)
        den = _dot(ones, p)[0:1] + jnp.exp2(sink - m)
        outs.append(_dot(v_t[h // group], p) / den)
    slabs = [jnp.concatenate(outs[i:i + 2], axis=0).T for i in range(0, SWA_HEADS, 2)]
    o_ref[0] = jnp.concatenate(slabs, axis=-1).astype(o_ref.dtype)


def swa_attention(u, cos, sin, gq, gk, sinks):
    b, s, _ = u.shape
    nb = s // SWA_BLOCK
    qw, kw = SWA_HEADS * HEAD_DIM, SWA_KV_HEADS * HEAD_DIM
    cur = lambda c: (lambda i, n: (i, n, c))
    prev = lambda c: (lambda i, n: (i, jnp.maximum(n - 1, 0), c))
    tab = pl.BlockSpec((1, SWA_BLOCK, LANES), cur(0))
    tab_prev = pl.BlockSpec((1, SWA_BLOCK, LANES), prev(0))
    gain = pl.BlockSpec((1, LANES), lambda i, n: (0, 0))
    seg = (jnp.arange(kw)[:, None] // HEAD_DIM == jnp.arange(kw)[None, :] // HEAD_DIM).astype(BF16)
    return pl.pallas_call(
        _swa_kernel,
        grid=(b, nb),
        in_specs=[pl.BlockSpec(memory_space=pltpu.SMEM),
                  pl.BlockSpec((1, SWA_BLOCK, qw), cur(0)),
                  pl.BlockSpec((1, SWA_BLOCK, kw), cur(qw // kw)),
                  pl.BlockSpec((1, SWA_BLOCK, kw), prev(qw // kw)),
                  pl.BlockSpec((1, SWA_BLOCK, kw), cur(qw // kw + 1)),
                  pl.BlockSpec((1, SWA_BLOCK, kw), prev(qw // kw + 1)),
                  tab, tab, tab_prev, tab_prev, gain, gain, pl.BlockSpec((kw, kw), lambda i, n: (0, 0))],
        out_specs=pl.BlockSpec((1, SWA_BLOCK, qw), cur(0)),
        out_shape=jax.ShapeDtypeStruct((b, s, qw), BF16),
        compiler_params=_params("parallel", "arbitrary"),
        name="swa_attention",
    )(sinks.astype(F32), u, u, u, u, u, cos, sin, cos, sin, gq, gk, seg)


def _mm(a, b, dims, passes):
    dn = (dims, ((), ()))
    dg = lambda x, y: lax.dot_general(x, y, dn, preferred_element_type=F32)
    ah = a.astype(BF16)
    bh = b.astype(BF16)
    if passes == 1:
        return dg(ah, bh)
    al = (a - ah.astype(F32)).astype(BF16)
    bl = (b - bh.astype(F32)).astype(BF16)
    return dg(ah, bh) + dg(ah, bl) + dg(al, bh)


_NN = ((1,), (0,))
_NT = ((1,), (1,))
_TN = ((0,), (0,))
P_SC, P_INV, P_PQ, P_OUT, P_ST = 1, 1, 1, 1, 1
RWKV_UNROLL = 8


RWKV_PREP_ROWS = 128


def _rwkv_kernel(ur_ref, uk_ref, uv_ref, ul_ref, mur_ref, muk_ref, muv_ref, mul_ref, w0_ref, w2_ref, a0_ref,
                 a2_ref, g2_ref, kkw_ref, ka_ref, gng_ref, gnb_ref, rk_ref,
                 o_ref, r_ref, k_ref, v_ref, kk_ref, b_ref, lw_ref, g_ref,
                 st_ref, y1_ref, y0_ref, n_ref, z_ref, dec_ref, *, nchunk):
    c = RWKV_CHUNK
    lane_c = lax.broadcasted_iota(jnp.int32, (c, LANES), 1)
    head0 = lane_c < HEAD_DIM
    ri = lax.broadcasted_iota(jnp.int32, (2 * c, 2 * c), 0)
    ci = lax.broadcasted_iota(jnp.int32, (2 * c, 2 * c), 1)
    eye = jnp.where(ri == ci, 1.0, 0.0)
    tril_c = jnp.where(lax.broadcasted_iota(jnp.int32, (c, c), 0) >= lax.broadcasted_iota(jnp.int32, (c, c), 1),
                       1.0, 0.0).astype(BF16)
    stack = lambda x: jnp.concatenate([jnp.where(head0, x, 0.0), jnp.where(head0, 0.0, x)], axis=0)

    def seg_mean(x):
        first = lax.broadcasted_iota(jnp.int32, x.shape, 1) < HEAD_DIM
        m0 = jnp.sum(jnp.where(first, x, 0.0), axis=-1, keepdims=True)
        m1 = jnp.sum(jnp.where(first, 0.0, x), axis=-1, keepdims=True)
        return jnp.where(first, m0, m1) * (1.0 / HEAD_DIM)

    def prep(io, first_group):
        group_rows = RWKV_UNROLL * c
        for tix in range(group_rows // RWKV_PREP_ROWS):
            start = pl.multiple_of(io * group_rows + tix * RWKV_PREP_ROWS, RWKV_PREP_ROWS)
            rows = pl.ds(start, RWKV_PREP_ROWS)
            at_start = first_group and tix == 0

            def shifted(ref, mu_ref):
                x = ref[0, rows, :]
                if at_start:
                    last = jnp.zeros((1, x.shape[1]), F32)
                else:
                    last = ref[0, pl.ds(pl.multiple_of(start - 8, 8), 8), :][7:8, :]
                row = lax.broadcasted_iota(jnp.int32, x.shape, 0)
                prev = jnp.where(row == 0, last, pltpu.roll(x, 1, 0))
                return x + (prev - x) * mu_ref[...]

            r = shifted(ur_ref, mur_ref)
            k = shifted(uk_ref, muk_ref)
            v = shifted(uv_ref, muv_ref)
            lo = shifted(ul_ref, mul_ref)
            yield
            w_lo, a_lo, g_lo = lo[:, 0:LANES], lo[:, LANES:2 * LANES], lo[:, 2 * LANES:4 * LANES]
            z = -(w0_ref[...] + _mm(jnp.tanh(w_lo), w2_ref[...], _NN, 3))
            w = -(jnp.maximum(z, 0.0) + jnp.log(1.0 + jnp.exp(-jnp.abs(z)))) - 0.5
            a = 1.0 / (1.0 + jnp.exp(-(a0_ref[...] + _mm(a_lo, a2_ref[...], _NN, 1))))
            g = _mm(1.0 / (1.0 + jnp.exp(-g_lo)), g2_ref[...], _NN, 1)
            yield
            kk = k * kkw_ref[...]
            kk = kk / jnp.maximum(jnp.sqrt(seg_mean(kk * kk) * float(HEAD_DIM)), 1e-12)
            r_ref[rows, :] = r
            k_ref[rows, :] = k * (1.0 + (a - 1.0) * ka_ref[...])
            v_ref[rows, :] = v
            kk_ref[rows, :] = kk
            b_ref[rows, :] = kk * a
            lw_ref[rows, :] = -jnp.exp(w)
            g_ref[rows, :] = g
            yield

    def build(ics):
        each = lambda f, *cols: [f(*args) for args in zip(*cols)]
        sls = [pl.ds(pl.multiple_of(ic * c, c), c) for ic in ics]
        load = lambda ref: [ref[sl, :] for sl in sls]
        r, k, v, kk, b, lw = (load(ref) for ref in (r_ref, k_ref, v_ref, kk_ref, b_ref, lw_ref))

        def running_sum(x):
            l1 = x.astype(BF16)
            rest = x - l1.astype(F32)
            l2 = rest.astype(BF16)
            l3 = (rest - l2.astype(F32)).astype(BF16)
            return _dot(tril_c, l1) + _dot(tril_c, l2) + _dot(tril_c, l3)

        cum = each(running_sum, lw)
        yield
        cum_end = [x[c - 1:c, :] for x in cum]
        e_neg = each(lambda x: jnp.exp(-x), cum)
        e_end = each(lambda x, xe: jnp.exp(xe - x), cum, cum_end)
        a_s = each(lambda kk_, x, l: stack(-kk_ * jnp.exp(x - l)), kk, cum, lw)
        r_s = each(lambda r_, x: stack(r_ * jnp.exp(x)), r, cum)
        b_s = each(lambda b_, e: stack(b_ * e), b, e_neg)
        k_s = each(lambda k_, e: stack(k_ * e), k, e_neg)
        bh_s = each(lambda b_, e: stack(b_ * e), b, e_end)
        kh_s = each(lambda k_, e: stack(k_ * e), k, e_end)
        v_s = each(stack, v)
        n2 = 2 * c
        sc = each(lambda a_, r_, b_, k_: _mm(jnp.concatenate([a_, r_], axis=0),
                                             jnp.concatenate([b_, k_], axis=0), _NT, P_SC), a_s, r_s, b_s, k_s)
        low = [jnp.where(ri > ci, x[:n2, :n2], 0.0) for x in sc]
        a_ak = [jnp.where(ri > ci, x[:n2, n2:], 0.0) for x in sc]
        a_rb = [jnp.where(ri >= ci, x[n2:, :n2], 0.0) for x in sc]
        a_rk = [jnp.where(ri >= ci, x[n2:, n2:], 0.0) for x in sc]
        yield
        inv = [eye + x for x in low]
        pw = each(lambda x: _mm(x, x, _NN, P_INV), low)
        yield
        levels = 5
        for lvl in range(levels - 1):
            both = each(lambda t, x: _mm(jnp.concatenate([t, x], axis=0), x, _NN, P_INV), inv, pw)
            inv = each(lambda t, r: t + r[:n2], inv, both)
            pw = [r[n2:] for r in both]
            yield
        inv = each(lambda t, x: t + _mm(t, x, _NN, P_INV), inv, pw)
        yield
        akv = each(lambda x, y: _mm(x, y, _NN, P_PQ), a_ak, v_s)
        yield
        pq = each(lambda t, x, y: _mm(t, jnp.concatenate([x, y], axis=1), _NN, P_PQ), inv, a_s, akv)
        yield
        yy = each(lambda x, y: _mm(x, y, _NN, P_OUT), a_rb, pq)
        y0b = each(lambda x, y: _mm(x, y, _NN, P_OUT), a_rk, v_s)
        nz = each(lambda x, y: _mm(x, y, _TN, P_OUT), pq, bh_s)
        zb = each(lambda x, y: _mm(x, y, _TN, P_OUT), v_s, kh_s)
        for i, ic in enumerate(ics):
            y1_ref[ic] = r_s[i] + yy[i][:, :LANES]
            y0_ref[ic] = yy[i][:, LANES:] + y0b[i]
            n_ref[ic] = nz[i][:LANES]
            z_ref[ic] = nz[i][LANES:] + zb[i]
            dec_ref[ic] = jnp.broadcast_to(jnp.exp(cum_end[i]), (8, LANES))

    def emit(ic, st):
        sl = pl.ds(pl.multiple_of(ic * c, c), c)
        r, k, v, g = r_ref[sl, :], k_ref[sl, :], v_ref[sl, :], g_ref[sl, :]
        y_st = _mm(y1_ref[ic], st, _NT, P_ST) + y0_ref[ic]
        y = y_st[0:c] + y_st[c:2 * c]
        mean = seg_mean(y)
        var = seg_mean((y - mean) * (y - mean))
        yn = (y - mean) * lax.rsqrt(var + RWKV_GN_EPS) * gng_ref[...] + gnb_ref[...]
        bonus = seg_mean(r * k * rk_ref[...]) * float(HEAD_DIM) * v
        o_ref[0, sl, :] = ((yn + bonus) * g).astype(o_ref.dtype)
        return st * dec_ref[ic][0:1, :] + _mm(st, n_ref[ic], _NN, P_ST) + z_ref[ic]

    def scan(ics):
        st = st_ref[...]
        for ic in ics:
            st = emit(ic, st)
            yield
        st_ref[...] = st

    def run(*gens):
        live = list(gens)
        while live:
            live = [gen for gen in live if next(gen, live) is not live]

    group = lambda io: [io * RWKV_UNROLL + i for i in range(RWKV_UNROLL)]
    ngroup = nchunk // RWKV_UNROLL
    st_ref[...] = jnp.zeros_like(st_ref)
    run(prep(0, True))
    if ngroup > 1:
        run(prep(1, False), build(group(0)))
    else:
        run(build(group(0)))

    @pl.loop(1, ngroup - 1)
    def _(io):
        run(prep(io + 1, False), build(group(io)), scan(group(io - 1)))

    if ngroup > 1:
        run(build(group(ngroup - 1)), scan(group(ngroup - 2)))
    run(scan(group(ngroup - 1)))


def rwkv_mix(u, mu, w0, w2, a0, a2, g2, k_k, k_a, r_k, gn_g, gn_b):
    bsz, s, _ = u.shape
    npair = RWKV_DIM // LANES
    nchunk = s // RWKV_CHUNK
    assert nchunk % RWKV_UNROLL == 0 and (RWKV_UNROLL * RWKV_CHUNK) % RWKV_PREP_ROWS == 0
    base = (SWA_HEADS + 2 * SWA_KV_HEADS) * HEAD_DIM // LANES
    lora_w = 4 * LANES
    slab = lambda off: pl.BlockSpec((1, s, LANES), lambda i, p: (i, 0, off + p))
    vec = pl.BlockSpec((1, LANES), lambda i, p: (0, p))
    cols = lambda rows: pl.BlockSpec((rows, LANES), lambda i, p: (0, p))
    row = lambda vv: vv.reshape(1, -1).astype(F32)
    pad_rows = lambda m, rows: jnp.pad(m, ((0, rows - m.shape[0]), (0, 0))).astype(F32)
    pad_cols = lambda vv, n: jnp.pad(vv, (0, n - vv.shape[0]))
    c3 = 3 * RWKV_DIM
    mu_l = jnp.concatenate([pad_cols(mu[c3:c3 + DECAY_LORA], LANES),
                            pad_cols(mu[c3 + DECAY_LORA:c3 + DECAY_LORA + AAA_LORA], LANES),
                            pad_cols(mu[c3 + DECAY_LORA + AAA_LORA:], 2 * LANES)])
    seq = pltpu.VMEM((s, LANES), F32)
    mat = pltpu.VMEM((nchunk, LANES, LANES), F32)
    return pl.pallas_call(
        functools.partial(_rwkv_kernel, nchunk=nchunk),
        grid=(bsz, npair),
        in_specs=[slab(base), slab(base + npair), slab(base + 2 * npair),
                  pl.BlockSpec((1, s, lora_w), lambda i, p: (i, 0, (base + 3 * npair) * LANES // lora_w)),
                  vec, vec, vec, pl.BlockSpec((1, lora_w), lambda i, p: (0, 0)),
                  vec, cols(LANES), vec, cols(LANES), cols(2 * LANES), vec, vec, vec, vec, vec],
        out_specs=pl.BlockSpec((1, s, LANES), lambda i, p: (i, 0, p)),
        out_shape=jax.ShapeDtypeStruct((bsz, s, RWKV_DIM), BF16),
        scratch_shapes=[seq] * 7 + [pltpu.VMEM((LANES, LANES), F32), mat, mat, mat, mat,
                                    pltpu.VMEM((nchunk, 8, LANES), F32)],
        compiler_params=_params("parallel", "arbitrary"),
        name="rwkv_mix",
    )(u, u, u, u, row(mu[:RWKV_DIM]), row(mu[RWKV_DIM:2 * RWKV_DIM]), row(mu[2 * RWKV_DIM:c3]), row(mu_l),
      row(w0), pad_rows(w2, LANES), row(a0), pad_rows(a2, LANES), pad_rows(g2, 2 * LANES),
      row(k_k), row(k_a), row(gn_g), row(gn_b), row(r_k))


def _mla_prep_kernel(cq_ref, ckv_ref, pe_ref, cos_ref, sin_ref, e_ref, gq_ref, gkn_ref, gkp_ref, invn_ref,
                     gcq_ref, gckv_ref, wuq_ref, wukv_ref, qo_ref, ko_ref, vo_ref, cqn_ref, ckvn_ref, kpe_ref):
    cos, sin = cos_ref[0], sin_ref[0]
    half = MLA_ROPE // 2

    def rope(x, g):
        w = x.shape[1]
        xg = x * _tile_lanes(g, w)
        return xg * _tile_lanes(cos, w) + _swap_halves(xg, half) * _tile_lanes(sin, w)

    @pl.when(pl.program_id(2) == 0)
    def _():
        cqn_ref[...] = _rms(cq_ref[0], gcq_ref[...]).astype(BF16)
        ckvn_ref[...] = _rms(ckv_ref[0], gckv_ref[...]).astype(BF16)
        pe = pe_ref[0]
        inv_pe = lax.rsqrt(jnp.sum(pe * pe, axis=-1, keepdims=True) * (1.0 / MLA_ROPE) + EPS)
        kpe_ref[...] = rope(pe, gkp_ref[...]) * inv_pe

    x = _dot(cqn_ref[...], wuq_ref[...])
    w = x.shape[1]
    inv_n = _tile_lanes(invn_ref[...], w)
    inv = lax.rsqrt(_seg_sum(x * x, e_ref[...]) * inv_n + EPS)
    qo_ref[0] = (rope(x, gq_ref[...]) * inv * ((MLA_NOPE + MLA_ROPE) ** -0.5 * LOG2E)).astype(qo_ref.dtype)
    kv = _dot(ckvn_ref[...], wukv_ref[...])
    inv_k = lax.rsqrt(_seg_sum(kv * kv, e_ref[...]) * inv_n + EPS)
    k_nope = kv * inv_k * _tile_lanes(gkn_ref[...], w)
    ko_ref[0] = (k_nope + _tile_lanes(kpe_ref[...], w)).astype(ko_ref.dtype)
    v = jnp.concatenate([kv[:, h * LANES + MLA_NOPE:(h + 1) * LANES] for h in range(w // LANES)], axis=-1)
    _store_transposed_tiles(vo_ref, v)


def mla_prep(u, cos, sin, gq, gkn, gkp, gcq, gckv, w_uq, w_ukv, *, ts=1024, tc=512):
    b, s, _ = u.shape
    wtot = w_uq.shape[1]
    ts = min(ts, s)
    lane = jnp.arange(SEG_W)
    same = (lane[:, None] // LANES == lane[None, :] // LANES)
    pos = lane % LANES
    nope = pos < MLA_NOPE
    pe = (pos >= MLA_NOPE) & (pos < MLA_NOPE + MLA_ROPE)
    seg = (same & ((nope[:, None] & nope[None, :]) | (pe[:, None] & pe[None, :]))).astype(BF16)
    p1 = jnp.arange(LANES)
    inv_n = jnp.where(p1 < MLA_NOPE, 1.0 / MLA_NOPE, jnp.where(p1 < MLA_NOPE + MLA_ROPE, 1.0 / MLA_ROPE, 0.0))
    blk = pl.BlockSpec((1, ts, tc), lambda i, t, c: (i, t, c))
    tab = pl.BlockSpec((1, ts, LANES), lambda i, t, c: (i, t, 0))
    vec = pl.BlockSpec((1, LANES), lambda i, t, c: (0, 0))
    pe_blk = (MLA_Q_RANK + MLA_KV_RANK) // LANES
    out = jax.ShapeDtypeStruct((b, s, wtot), BF16)
    return pl.pallas_call(
        _mla_prep_kernel,
        grid=(b, s // ts, wtot // tc),
        in_specs=[pl.BlockSpec((1, ts, MLA_Q_RANK), lambda i, t, c: (i, t, 0)),
                  pl.BlockSpec((1, ts, MLA_KV_RANK), lambda i, t, c: (i, t, MLA_Q_RANK // MLA_KV_RANK)),
                  pl.BlockSpec((1, ts, LANES), lambda i, t, c: (i, t, pe_blk)), tab, tab,
                  pl.BlockSpec((SEG_W, SEG_W), lambda i, t, c: (0, 0)), vec, vec, vec, vec,
                  pl.BlockSpec((1, MLA_Q_RANK), lambda i, t, c: (0, 0)),
                  pl.BlockSpec((1, MLA_KV_RANK), lambda i, t, c: (0, 0)),
                  pl.BlockSpec((MLA_Q_RANK, tc), lambda i, t, c: (0, c)),
                  pl.BlockSpec((MLA_KV_RANK, tc), lambda i, t, c: (0, c))],
        out_specs=[blk, blk, pl.BlockSpec((1, ts // ATTN_TILE, tc // LANES * MLA_V, ATTN_TILE),
                                          lambda i, t, c: (i, t, c, 0))],
        out_shape=[out, out, jax.ShapeDtypeStruct((b, s // ATTN_TILE, wtot // LANES * MLA_V, ATTN_TILE), BF16)],
        scratch_shapes=[pltpu.VMEM((ts, MLA_Q_RANK), BF16), pltpu.VMEM((ts, MLA_KV_RANK), BF16),
                        pltpu.VMEM((ts, LANES), F32)],
        compiler_params=_params("parallel", "parallel", "arbitrary"),
        name="mla_prep",
    )(u, u, u, cos, sin, seg, gq, gkn, gkp, inv_n.reshape(1, LANES).astype(F32),
      gcq.reshape(1, -1).astype(F32), gckv.reshape(1, -1).astype(F32), w_uq, w_ukv)


def _diff_prep_kernel(q_ref, k_ref, v_ref, cos_ref, sin_ref, e_ref, p_ref, gq_ref, gk_ref, qo_ref, ko_ref, vo_ref):
    cos, sin = cos_ref[0], sin_ref[0]
    _store_transposed_tiles(vo_ref, v_ref[0])

    def swap_halves_mxu(x):
        perm = p_ref[...]
        w = perm.shape[0]
        hi = x.astype(BF16)
        lo = (x - hi.astype(F32)).astype(BF16)
        return jnp.concatenate([_dot(hi[:, i:i + w], perm) + _dot(lo[:, i:i + w], perm)
                                for i in range(0, x.shape[1], w)], axis=-1)

    def prep(x, g, scale):
        w = x.shape[1]
        xg = x * _tile_lanes(g, w)
        xr = xg * _tile_lanes(cos, w) + swap_halves_mxu(xg) * _tile_lanes(sin, w)
        inv = lax.rsqrt(_seg_sum(x * x, e_ref[...]) * (1.0 / DIFF_QK) + EPS)
        return xr * inv * scale

    qo_ref[0] = prep(q_ref[0], gq_ref[...], DIFF_QK ** -0.5 * LOG2E).astype(qo_ref.dtype)
    ko_ref[0] = prep(k_ref[0], gk_ref[...], 1.0).astype(ko_ref.dtype)


def diff_prep(u, cos, sin, gq, gk, *, ts=1024, tc=512):
    b, s, _ = u.shape
    ts = min(ts, s)
    wtot = 2 * DIFF_HEADS * DIFF_QK
    q_base = (CD_IN_PAD - 3 * wtot) // tc
    seg = (jnp.arange(SEG_W)[:, None] // DIFF_QK == jnp.arange(SEG_W)[None, :] // DIFF_QK).astype(BF16)
    lane = jnp.arange(SEG_W)
    perm = (lane[:, None] == (lane[None, :] ^ (DIFF_QK // 2))).astype(BF16)
    blk = lambda off: pl.BlockSpec((1, ts, tc), lambda i, t, c: (i, t, off + c))
    tab = pl.BlockSpec((1, ts, LANES), lambda i, t, c: (i, t, 0))
    vec = pl.BlockSpec((1, LANES), lambda i, t, c: (0, 0))
    out = jax.ShapeDtypeStruct((b, s, wtot), BF16)
    return pl.pallas_call(
        _diff_prep_kernel,
        grid=(b, s // ts, wtot // tc),
        in_specs=[blk(q_base), blk(q_base + wtot // tc), blk(q_base + 2 * wtot // tc), tab, tab,
                  pl.BlockSpec((SEG_W, SEG_W), lambda i, t, c: (0, 0)),
                  pl.BlockSpec((SEG_W, SEG_W), lambda i, t, c: (0, 0)), vec, vec],
        out_specs=[blk(0), blk(0),
                   pl.BlockSpec((1, ts // ATTN_TILE, tc, ATTN_TILE), lambda i, t, c: (i, t, c, 0))],
        out_shape=[out, out, jax.ShapeDtypeStruct((b, s // ATTN_TILE, wtot, ATTN_TILE), BF16)],
        compiler_params=_params("parallel", "parallel", "arbitrary"),
        name="diff_prep",
    )(u, u, u, cos, sin, seg, perm, gq, gk)


def _causal_attn_kernel(lam_ref, q_ref, k_ref, vt_ref, g_ref, o_ref, *, n_sm, tq, ow, out_scale):
    qi = pl.program_id(2)
    q = q_ref[0]
    hps = q.shape[1] // LANES
    slab = lambda x, h: x[:, h * LANES:(h + 1) * LANES]
    lane = lax.broadcasted_iota(jnp.int32, (tq, LANES), 1)
    qs, src = [], []
    for h in range(hps):
        qh = slab(q, h)
        if n_sm == 2:
            zero = jnp.zeros_like(qh)
            qs += [jnp.where(lane < DIFF_QK, qh, zero), jnp.where(lane < DIFF_QK, zero, qh)]
            src += [h, h]
        else:
            qs.append(qh)
            src.append(h)
    nch = len(qs)
    key_i = lax.broadcasted_iota(jnp.int32, (tq, tq), 0)
    qry_i = lax.broadcasted_iota(jnp.int32, (tq, tq), 1)
    ones = jnp.ones((8, tq), BF16)

    def step(j, carry, diagonal):
        kj = k_ref[0, pl.ds(pl.multiple_of(j * tq, tq), tq), :]
        scores = lambda i: _dot_t(slab(kj, src[i]), qs[i])
        new = []
        ahead = [scores(i) for i in range(min(ATTN_LOOKAHEAD, nch))]
        for i in range(nch):
            s = ahead.pop(0)
            if i + ATTN_LOOKAHEAD < nch:
                ahead.append(scores(i + ATTN_LOOKAHEAD))
            if diagonal:
                s = jnp.where(key_i <= qry_i, s, NEG_INF)
            m, l, acc = carry[3 * i:3 * i + 3]
            m_new = jnp.maximum(m, jnp.max(s, axis=0, keepdims=True))
            alpha = jnp.exp2(m - m_new)
            p = jnp.exp2(s - m_new).astype(BF16)
            new += [m_new, alpha * l + _dot(ones, p)[0:1],
                    alpha * acc + _dot(vt_ref[0, j, src[i] * ow:(src[i] + 1) * ow, :], p)]
        return tuple(new)

    init = (jnp.full((1, tq), NEG_INF, F32), jnp.zeros((1, tq), F32),
            jnp.zeros((ow, tq), F32)) * nch
    carry = lax.fori_loop(0, qi, lambda j, cr: step(j, cr, False), init)
    carry = step(qi, carry, True)
    outs = []
    for h in range(hps):
        c0 = 3 * n_sm * h
        o = carry[c0 + 2] / carry[c0 + 1]
        if n_sm == 2:
            o = o - lam_ref[0] * (carry[c0 + 5] / carry[c0 + 4])
            o = o * lax.rsqrt(jnp.mean(o * o, axis=0, keepdims=True) + EPS) * g_ref[...] * out_scale
        outs.append(o.T.astype(o_ref.dtype))
    o_ref[0] = jnp.concatenate(outs, axis=-1)


def causal_attention(q, k, vt, *, n_sm, lam=None, g=None, out_scale=1.0, hps=4, name="causal_attention"):
    b, s, wtot = q.shape
    tq = vt.shape[3]
    width = hps * LANES
    groups = wtot // width
    ow = vt.shape[2] // (wtot // LANES)
    lam = jnp.zeros((1,), F32) if lam is None else lam.reshape(1).astype(F32)
    g = jnp.ones((ow, 1), F32) if g is None else g.reshape(ow, 1).astype(F32)
    seq = pl.BlockSpec((1, s, width), lambda i, h, t: (i, 0, h))
    tile = pl.BlockSpec((1, tq, width), lambda i, h, t: (i, t, h))
    return pl.pallas_call(
        functools.partial(_causal_attn_kernel, n_sm=n_sm, tq=tq, ow=ow, out_scale=out_scale),
        grid=(b, groups, s // tq),
        in_specs=[pl.BlockSpec(memory_space=pltpu.SMEM), tile, seq,
                  pl.BlockSpec((1, s // tq, hps * ow, tq), lambda i, h, t: (i, 0, h, 0)),
                  pl.BlockSpec((ow, 1), lambda i, h, t: (0, 0))],
        out_specs=pl.BlockSpec((1, tq, hps * ow), lambda i, h, t: (i, t, h)),
        out_shape=jax.ShapeDtypeStruct((b, s, groups * hps * ow), BF16),
        compiler_params=_params("parallel", "parallel", "arbitrary"),
        name=name,
    )(lam, q, k, vt, g)


def _memx_kernel(x_ref, ya_ref, yb_ref, wa_ref, wb_ref, g_ref, wq_ref, kv_ref, gq_ref, gk_ref, wo_ref, o_ref):
    x = x_ref[0] + _dot(ya_ref[0], wa_ref[...]) + _dot(yb_ref[0], wb_ref[...])
    q = _dot(_rms(x, g_ref[...]).astype(BF16), wq_ref[...])
    kv = kv_ref[0]
    outs = []
    for h in range(MEM_HEADS):
        sl = slice(h * MEM_HEAD_DIM, (h + 1) * MEM_HEAD_DIM)
        qh = (_rms(q[:, sl], gq_ref[...]) * MEM_HEAD_DIM ** -0.5).astype(BF16)
        kh = _rms(kv[:, sl], gk_ref[...]).astype(BF16)
        vh = kv[:, MEM_W + h * MEM_HEAD_DIM:MEM_W + (h + 1) * MEM_HEAD_DIM].astype(BF16)
        s = _dot_t(qh, kh)
        p = jnp.exp(s - jnp.max(s, axis=-1, keepdims=True))
        outs.append(_dot(p.astype(BF16), vh) / jnp.sum(p, axis=-1, keepdims=True))
    o_ref[0] = x + _dot(jnp.concatenate(outs, axis=-1).astype(BF16), wo_ref[...])


def mix_out_mem_attention(x, ya, yb, wa, wb, mem_kv, g, wq, gq, gk, wo, *, tm=512):
    b, s, d = x.shape
    m = mem_kv.shape[1]
    tm = min(tm, s)
    const = lambda shape: pl.BlockSpec(shape, lambda i, t: (0,) * len(shape), pipeline_mode=pl.Buffered(1))
    tile = lambda w: pl.BlockSpec((1, tm, w), lambda i, t: (i, t, 0))
    return pl.pallas_call(
        _memx_kernel,
        grid=(b, s // tm),
        in_specs=[tile(d), tile(ya.shape[2]), tile(yb.shape[2]), const(wa.shape), const(wb.shape),
                  const((1, d)), const((d, MEM_W)),
                  pl.BlockSpec((1, m, 2 * MEM_W), lambda i, t: (i, 0, 0)),
                  const((1, MEM_HEAD_DIM)), const((1, MEM_HEAD_DIM)), const((MEM_W, d))],
        out_specs=tile(d),
        out_shape=jax.ShapeDtypeStruct((b, s, d), F32),
        compiler_params=_params("parallel", "arbitrary"),
        name="mix_out_mem_attention",
    )(x, ya, yb, wa, wb, g.reshape(1, d).astype(F32), wq, mem_kv, gq.reshape(1, -1).astype(F32),
      gk.reshape(1, -1).astype(F32), wo)


def _rope_tables(positions, dim, lead_ones, tail):
    inv = 1.0 / (ROPE_THETA ** (jnp.arange(0, dim, 2, dtype=F32) / dim))
    ang = positions.astype(F32)[..., None] * inv
    c, s = jnp.cos(ang), jnp.sin(ang)
    shape = positions.shape
    cos = jnp.concatenate([jnp.ones(shape + (lead_ones,), F32), c, c, jnp.ones(shape + (tail,), F32)], axis=-1)
    sin = jnp.concatenate([jnp.zeros(shape + (lead_ones,), F32), -s, s, jnp.zeros(shape + (tail,), F32)], axis=-1)
    reps = LANES // cos.shape[-1]
    return jnp.tile(cos, (1, 1, reps)), jnp.tile(sin, (1, 1, reps))


IN_TILE = 512


def _pad_rows(w, rows):
    return jnp.pad(w, ((0, rows - w.shape[0]), (0, 0)))


def _ab_in_segments(w):
    c = (SWA_HEADS + 2 * SWA_KV_HEADS) * HEAD_DIM + 3 * RWKV_DIM
    wt = w.T.astype(BF16)
    lora = jnp.concatenate([_pad_rows(wt[c:c + DECAY_LORA], LANES),
                            _pad_rows(wt[c + DECAY_LORA:c + DECAY_LORA + AAA_LORA], LANES),
                            _pad_rows(wt[c + DECAY_LORA + AAA_LORA:], 2 * LANES)], axis=0)
    return [wt[:c], lora]


def _cd_in_segments(w):
    c1 = MLA_Q_RANK + MLA_KV_RANK
    wt = w.T.astype(BF16)
    z = lambda n: jnp.zeros((n, wt.shape[1]), wt.dtype)
    mid = jnp.concatenate([wt[MLA_Q_RANK:c1], z(MLA_NOPE), wt[c1:c1 + MLA_ROPE],
                           z(LANES - MLA_NOPE - MLA_ROPE), z(LANES)], axis=0)
    return [wt[:MLA_Q_RANK], mid, wt[c1 + MLA_ROPE:]]


def _head_slabs(w, per_head):
    k = w.shape[0]
    return jnp.pad(w.reshape(k, -1, per_head), ((0, 0), (0, 0), (0, LANES - per_head))).reshape(k, -1)


def _slab_vec(*parts):
    v = jnp.concatenate([p.astype(F32) for p in parts])
    return jnp.pad(v, (0, LANES - v.shape[0])).reshape(1, LANES)


def kernel(x, mem, positions, ffn1_norm, ffn1_w_gate, ffn1_w_up, ffn1_w_down, mix_norm, ab_w_in, ab_w_out, swa_q_norm, swa_k_norm, swa_sinks, rwkv_mu, rwkv_w0, rwkv_w2, rwkv_a0, rwkv_a2, rwkv_g2, rwkv_k_k, rwkv_k_a, rwkv_r_k, rwkv_gn_g, rwkv_gn_b, cd_w_in, cd_w_out, mla_cq_norm, mla_ckv_norm, mla_w_uq, mla_w_ukv, mla_q_nope_norm, mla_k_nope_norm, mla_q_rope_norm, mla_k_rope_norm, diff_q_norm, diff_k_norm, diff_lq1, diff_lk1, diff_lq2, diff_lk2, diff_subln, memx_norm, memx_w_q, memx_q_norm, memx_w_o, mem_norm, mem_w_kv, mem_k_norm, ffn2_norm, ffn2_w_gate, ffn2_w_up, ffn2_w_down):
    b, s, d = x.shape
    m = mem.shape[1]
    t = b * s
    depth = ffn1_norm.shape[0]
    bf = lambda w: w.astype(BF16)
    cos64, sin64 = _rope_tables(positions, HEAD_DIM, 0, 0)
    cos32, sin32 = _rope_tables(positions, MLA_ROPE, MLA_NOPE, LANES - MLA_NOPE - MLA_ROPE)

    mem_kv = norm_matmul(mem.reshape(b * m, d), mem_norm, [bf(mem_w_kv).T],
                         tn=IN_TILE, name="mem_kv").reshape(b, m, 2 * MEM_W)

    x = x.reshape(t, d)
    for layer in range(depth):
        j = layer // 2
        x = ffn(x, ffn1_norm[layer], ffn1_w_gate, ffn1_w_up, ffn1_w_down, layer, name="ffn1")
        if layer % 2 == 0:
            u = norm_matmul(x, mix_norm[layer], _ab_in_segments(ab_w_in[j]), tn=IN_TILE, name="ab_in")
            u = u.reshape(b, s, AB_IN_PAD)
            y_a = swa_attention(u, cos64, sin64, _slab_vec(swa_q_norm[j], swa_q_norm[j]),
                                _slab_vec(swa_k_norm[j], swa_k_norm[j]), swa_sinks[j])
            y_b = rwkv_mix(u, rwkv_mu[j], rwkv_w0[j], rwkv_w2[j], rwkv_a0[j], rwkv_a2[j], rwkv_g2[j],
                           rwkv_k_k[j], rwkv_k_a[j], rwkv_r_k[j], rwkv_gn_g[j], rwkv_gn_b[j])
            mixed, w_out, split = (y_a, y_b), bf(ab_w_out[j]), SWA_HEADS * HEAD_DIM
        else:
            u = norm_matmul(x, mix_norm[layer], _cd_in_segments(cd_w_in[j]), tn=IN_TILE, name="cd_in")
            u = u.reshape(b, s, CD_IN_PAD)
            zero64 = jnp.zeros((MLA_NOPE,), F32)
            q_c, k_c, vt_c = mla_prep(u, cos32, sin32,
                                      _slab_vec(mla_q_nope_norm[j], mla_q_rope_norm[j]),
                                      _slab_vec(mla_k_nope_norm[j]),
                                      _slab_vec(zero64, mla_k_rope_norm[j]),
                                      mla_cq_norm[j], mla_ckv_norm[j],
                                      bf(_head_slabs(mla_w_uq[j], MLA_NOPE + MLA_ROPE)), bf(mla_w_ukv[j]))
            y_c = causal_attention(q_c, k_c, vt_c, n_sm=1, hps=16, name="mla_attention")
            q_d, k_d, vt_d = diff_prep(u, cos64, sin64, _slab_vec(diff_q_norm[j], diff_q_norm[j]),
                                       _slab_vec(diff_k_norm[j], diff_k_norm[j]))
            lambda_init = 0.8 - 0.6 * math.exp(-0.3 * layer)
            lam = (jnp.exp(jnp.sum(diff_lq1[j].astype(F32) * diff_lk1[j].astype(F32)))
                   - jnp.exp(jnp.sum(diff_lq2[j].astype(F32) * diff_lk2[j].astype(F32))) + lambda_init)
            y_d = causal_attention(q_d, k_d, vt_d, n_sm=2, hps=8, lam=lam, g=diff_subln[j],
                                   out_scale=1.0 - lambda_init, name="diff_attention")
            mixed, w_out, split = (y_c, y_d), bf(cd_w_out[j]), MLA_HEADS * MLA_V
        x = mix_out_mem_attention(x.reshape(b, s, d), mixed[0], mixed[1], w_out[:split], w_out[split:], mem_kv,
                                  memx_norm[layer], bf(memx_w_q[layer]), memx_q_norm[layer], mem_k_norm,
                                  bf(memx_w_o[layer])).reshape(t, d)
        x = ffn(x, ffn2_norm[layer], ffn2_w_gate, ffn2_w_up, ffn2_w_down, layer, name="ffn2")
    return x.reshape(b, s, d)
```
